```python
import jax, jax.numpy as jnp
from jax import lax
import numpy as np

D_MODEL = 1024
BATCH = 16
SEQ = 4096
DEPTH = 2

SSD_EXPAND = 2
SSD_D_INNER = SSD_EXPAND * D_MODEL
SSD_HEAD_DIM = 64
SSD_N_HEADS = SSD_D_INNER // SSD_HEAD_DIM
SSD_N_GROUPS = 4
SSD_HEADS_PER_GROUP = SSD_N_HEADS // SSD_N_GROUPS
SSD_D_STATE = 128
SSD_CONV_WIDTH = 4
SSD_CHUNK = 128
SSD_CONV_DIM = SSD_D_INNER + 2 * SSD_N_GROUPS * SSD_D_STATE
DT_MIN = 1e-3
DT_MAX = 1e-1

SC_WIDTH = D_MODEL
SC_CONV_WIDTH = 3

N_BRANCHES = 2

D_FF = 2816
FFN_CONV_WIDTH = 3

EPS = 1e-6

IN_SPLIT_SIZES = (SSD_D_INNER, SSD_CONV_DIM, SSD_N_HEADS,
                  SC_WIDTH, SC_WIDTH, SC_WIDTH, N_BRANCHES * D_MODEL)
D_IN_PROJ = sum(IN_SPLIT_SIZES)

kernel_name = "hybrid_ssd_shortconv_adaln_block"


def rmsnorm(x, g):
    xf = x.astype(jnp.float32)
    xf = xf * lax.rsqrt(jnp.mean(xf * xf, axis=-1, keepdims=True) + EPS)
    return xf.astype(x.dtype) * g


def grouped_rmsnorm(y, g, n_groups):
    shp = y.shape
    yf = y.astype(jnp.float32).reshape(*shp[:-1], n_groups, shp[-1] // n_groups)
    yf = yf * lax.rsqrt(jnp.mean(yf * yf, axis=-1, keepdims=True) + EPS)
    return yf.reshape(shp) * g


def causal_dwconv(u, w, b=None):
    k_width = w.shape[0]
    seqlen = u.shape[1]
    up = jnp.pad(u, ((0, 0), (k_width - 1, 0), (0, 0)))
    y = up[:, 0:seqlen] * w[0]
    for k in range(1, k_width):
        y = y + up[:, k:k + seqlen] * w[k]
    if b is not None:
        y = y + b
    return y


def ssd_chunked(xh, dt, a, bmat, cmat):
    bsz, seqlen = xh.shape[0], xh.shape[1]
    nc = seqlen // SSD_CHUNK
    L, G, R, P, N = SSD_CHUNK, SSD_N_GROUPS, SSD_HEADS_PER_GROUP, SSD_HEAD_DIM, SSD_D_STATE
    x = (xh * dt[..., None]).reshape(bsz, nc, L, G, R, P)
    adt = jnp.moveaxis((dt * a).reshape(bsz, nc, L, G, R), 2, -1)
    a_cs = jnp.cumsum(adt, axis=-1)
    bc = bmat.reshape(bsz, nc, L, G, N)
    cc = cmat.reshape(bsz, nc, L, G, N)

    causal = jnp.tril(jnp.ones((L, L), dtype=bool))
    decay = jnp.exp(jnp.where(causal, a_cs[..., :, None] - a_cs[..., None, :], -jnp.inf))
    scores = jnp.einsum("bclgn,bcsgn->bcgls", cc, bc)
    m = scores[:, :, :, None] * decay
    y_diag = jnp.einsum("bcgrls,bcsgrp->bclgrp", m, x)

    decay_states = jnp.exp(a_cs[..., -1:] - a_cs)
    states = jnp.einsum("bclgn,bcgrl,bclgrp->bcgrpn", bc, decay_states, x)
    chunk_decay = jnp.exp(a_cs[..., -1])

    def step(h, inp):
        s_c, d_c = inp
        h_new = h * d_c[..., None, None] + s_c
        return h_new, h

    h0 = jnp.zeros((bsz, G, R, P, N), dtype=states.dtype)
    _, prev = lax.scan(step, h0, (jnp.moveaxis(states, 1, 0), jnp.moveaxis(chunk_decay, 1, 0)))
    prev = jnp.moveaxis(prev, 0, 1)

    y_off = jnp.einsum("bclgn,bcgrpn,bcgrl->bclgrp", cc, prev, jnp.exp(a_cs))
    return (y_diag + y_off).reshape(bsz, seqlen, SSD_N_HEADS, P)


def ssd_branch(z, xbc, dt_raw, conv_w, conv_b, dt_bias, a_log, d_skip, norm_g):
    bsz, seqlen = z.shape[0], z.shape[1]
    xbc = jax.nn.silu(causal_dwconv(xbc, conv_w, conv_b))
    gn = SSD_N_GROUPS * SSD_D_STATE
    xs, bm, cm = jnp.split(xbc, [SSD_D_INNER, SSD_D_INNER + gn], axis=-1)
    xh = xs.reshape(bsz, seqlen, SSD_N_HEADS, SSD_HEAD_DIM)
    dt = jax.nn.softplus(dt_raw.astype(jnp.float32) + dt_bias.astype(jnp.float32))
    a = -jnp.exp(a_log.astype(jnp.float32))
    y = ssd_chunked(xh.astype(jnp.float32), dt, a,
                    bm.reshape(bsz, seqlen, SSD_N_GROUPS, SSD_D_STATE).astype(jnp.float32),
                    cm.reshape(bsz, seqlen, SSD_N_GROUPS, SSD_D_STATE).astype(jnp.float32))
    y = y + d_skip.astype(jnp.float32)[:, None] * xh.astype(jnp.float32)
    y = y.reshape(bsz, seqlen, SSD_D_INNER) * jax.nn.silu(z.astype(jnp.float32))
    return grouped_rmsnorm(y, norm_g, SSD_N_GROUPS).astype(z.dtype)


def short_conv_branch(b_gate, c_gate, h, conv_w):
    return b_gate * causal_dwconv(c_gate * h, conv_w)


def hybrid_layer(x, c_act, ada_w, ada_b, mix_pre_g, mix_post_g, w_in,
                 ssd_conv_w, ssd_conv_b, ssd_dt_bias, ssd_a_log, ssd_d, ssd_norm_g,
                 w_ssd_out, sc_conv_w, w_sc_out, w_o,
                 ffn_pre_g, ffn_post_g, w_up, ffn_conv_w, ffn_conv_b, w_down):
    mod = c_act @ ada_w + ada_b
    sh1, sc1, gt1, sh2, sc2, gt2 = [m[:, None, :] for m in jnp.split(mod, 6, axis=-1)]

    h = rmsnorm(x, mix_pre_g) * (1.0 + sc1) + sh1
    proj = h @ w_in
    points, acc = [], 0
    for s in IN_SPLIT_SIZES[:-1]:
        acc += s
        points.append(acc)
    z, xbc, dt_raw, sc_b, sc_c, sc_h, gates = jnp.split(proj, points, axis=-1)
    y_ssd = ssd_branch(z, xbc, dt_raw, ssd_conv_w, ssd_conv_b, ssd_dt_bias,
                       ssd_a_log, ssd_d, ssd_norm_g) @ w_ssd_out
    y_sc = short_conv_branch(sc_b, sc_c, sc_h, sc_conv_w) @ w_sc_out
    g_ssd, g_sc = jnp.split(jax.nn.sigmoid(gates), 2, axis=-1)
    mix = (g_ssd * y_ssd + g_sc * y_sc) @ w_o
    x = x + gt1 * rmsnorm(mix, mix_post_g)

    h = rmsnorm(x, ffn_pre_g) * (1.0 + sc2) + sh2
    u = causal_dwconv(h @ w_up, ffn_conv_w, ffn_conv_b)
    u_gate, u_val = jnp.split(u, 2, axis=-1)
    f = (jax.nn.silu(u_gate) * u_val) @ w_down
    x = x + gt2 * rmsnorm(f, ffn_post_g)
    return x


def _fwd_setup_inputs(seed: int = 0) -> dict:
    key = jax.random.key(seed)
    ks = jax.random.split(key, 32)
    f32 = jnp.float32

    def dense(k, fan_in, fan_out, scale=1.0):
        return jax.random.normal(k, (DEPTH, fan_in, fan_out), f32) * (scale * fan_in ** -0.5)

    def gain(k, n):
        return 1.0 + 0.05 * jax.random.normal(k, (DEPTH, n), f32)

    def small(k, shape, s=0.02):
        return s * jax.random.normal(k, shape, f32)

    u = jax.random.uniform(ks[8], (DEPTH, SSD_N_HEADS), f32)
    dt0 = jnp.exp(u * (np.log(DT_MAX) - np.log(DT_MIN)) + np.log(DT_MIN))
    dt_bias = dt0 + jnp.log(-jnp.expm1(-dt0))
    a_log = jnp.log(jax.random.uniform(ks[9], (DEPTH, SSD_N_HEADS), f32, 1.0, 16.0))

    return {
        "x": jax.random.normal(ks[0], (BATCH, SEQ, D_MODEL), f32),
        "c": jax.random.normal(ks[1], (BATCH, D_MODEL), f32),
        "ada_w": dense(ks[2], D_MODEL, 6 * D_MODEL, 0.5),
        "ada_b": small(ks[3], (DEPTH, 6 * D_MODEL)),
        "mix_pre_g": gain(ks[4], D_MODEL),
        "mix_post_g": gain(ks[5], D_MODEL),
        "w_in": dense(ks[6], D_MODEL, D_IN_PROJ),
        "ssd_conv_w": jax.random.normal(ks[7], (DEPTH, SSD_CONV_WIDTH, SSD_CONV_DIM), f32) * SSD_CONV_WIDTH ** -0.5,
        "ssd_conv_b": small(ks[10], (DEPTH, SSD_CONV_DIM)),
        "ssd_dt_bias": dt_bias,
        "ssd_a_log": a_log,
        "ssd_d": 1.0 + 0.1 * jax.random.normal(ks[11], (DEPTH, SSD_N_HEADS), f32),
        "ssd_norm_g": gain(ks[12], SSD_D_INNER),
        "w_ssd_out": dense(ks[13], SSD_D_INNER, D_MODEL),
        "sc_conv_w": jax.random.normal(ks[14], (DEPTH, SC_CONV_WIDTH, SC_WIDTH), f32) * SC_CONV_WIDTH ** -0.5,
        "w_sc_out": dense(ks[15], SC_WIDTH, D_MODEL),
        "w_o": dense(ks[16], D_MODEL, D_MODEL),
        "ffn_pre_g": gain(ks[17], D_MODEL),
        "ffn_post_g": gain(ks[18], D_MODEL),
        "w_up": dense(ks[19], D_MODEL, 2 * D_FF),
        "ffn_conv_w": jax.random.normal(ks[20], (DEPTH, FFN_CONV_WIDTH, 2 * D_FF), f32) * FFN_CONV_WIDTH ** -0.5,
        "ffn_conv_b": small(ks[21], (DEPTH, 2 * D_FF)),
        "w_down": dense(ks[22], D_FF, D_MODEL),
    }


def _fwd_reference(x, c, ada_w, ada_b, mix_pre_g, mix_post_g, w_in,
              ssd_conv_w, ssd_conv_b, ssd_dt_bias, ssd_a_log, ssd_d, ssd_norm_g,
              w_ssd_out, sc_conv_w, w_sc_out, w_o,
              ffn_pre_g, ffn_post_g, w_up, ffn_conv_w, ffn_conv_b, w_down):
    c_act = jax.nn.silu(c)
    for i in range(DEPTH):
        x = hybrid_layer(x, c_act, ada_w[i], ada_b[i], mix_pre_g[i], mix_post_g[i], w_in[i],
                         ssd_conv_w[i], ssd_conv_b[i], ssd_dt_bias[i], ssd_a_log[i], ssd_d[i],
                         ssd_norm_g[i], w_ssd_out[i], sc_conv_w[i], w_sc_out[i], w_o[i],
                         ffn_pre_g[i], ffn_post_g[i], w_up[i], ffn_conv_w[i], ffn_conv_b[i],
                         w_down[i])
    return x


import jax as _jax
import jax.numpy as _jnp

TWIN_FORMAT = 'train_step'
FWD_PARAMS = ['x', 'c', 'ada_w', 'ada_b', 'mix_pre_g', 'mix_post_g', 'w_in', 'ssd_conv_w', 'ssd_conv_b', 'ssd_dt_bias', 'ssd_a_log', 'ssd_d', 'ssd_norm_g', 'w_ssd_out', 'sc_conv_w', 'w_sc_out', 'w_o', 'ffn_pre_g', 'ffn_post_g', 'w_up', 'ffn_conv_w', 'ffn_conv_b', 'w_down']
TWIN_WEIGHTS = ['ada_w', 'ada_b', 'mix_pre_g', 'mix_post_g', 'w_in', 'ssd_conv_w', 'ssd_conv_b', 'ssd_dt_bias', 'ssd_a_log', 'ssd_d', 'ssd_norm_g', 'w_ssd_out', 'sc_conv_w', 'w_sc_out', 'w_o', 'ffn_pre_g', 'ffn_post_g', 'w_up', 'ffn_conv_w', 'ffn_conv_b', 'w_down']
TWIN_DIFF_INPUT = 'x'
TWIN_INPUTS = ['x', 'c', 'ada_w', 'ada_b', 'mix_pre_g', 'mix_post_g', 'w_in', 'ssd_conv_w', 'ssd_conv_b', 'ssd_dt_bias', 'ssd_a_log', 'ssd_d', 'ssd_norm_g', 'w_ssd_out', 'sc_conv_w', 'w_sc_out', 'w_o', 'ffn_pre_g', 'ffn_post_g', 'w_up', 'ffn_conv_w', 'ffn_conv_b', 'w_down', 'loss_target', 'm_ada_w', 'm_ada_b', 'm_mix_pre_g', 'm_mix_post_g', 'm_w_in', 'm_ssd_conv_w', 'm_ssd_conv_b', 'm_ssd_dt_bias', 'm_ssd_a_log', 'm_ssd_d', 'm_ssd_norm_g', 'm_w_ssd_out', 'm_sc_conv_w', 'm_w_sc_out', 'm_w_o', 'm_ffn_pre_g', 'm_ffn_post_g', 'm_w_up', 'm_ffn_conv_w', 'm_ffn_conv_b', 'm_w_down', 'v_ada_w', 'v_ada_b', 'v_mix_pre_g', 'v_mix_post_g', 'v_w_in', 'v_ssd_conv_w', 'v_ssd_conv_b', 'v_ssd_dt_bias', 'v_ssd_a_log', 'v_ssd_d', 'v_ssd_norm_g', 'v_w_ssd_out', 'v_sc_conv_w', 'v_w_sc_out', 'v_w_o', 'v_ffn_pre_g', 'v_ffn_post_g', 'v_w_up', 'v_ffn_conv_w', 'v_ffn_conv_b', 'v_w_down']
TWIN_OUTPUTS = ['loss', 'grad_x', 'grad_ada_w', 'grad_ada_b', 'grad_mix_pre_g', 'grad_mix_post_g', 'grad_w_in', 'grad_ssd_conv_w', 'grad_ssd_conv_b', 'grad_ssd_dt_bias', 'grad_ssd_a_log', 'grad_ssd_d', 'grad_ssd_norm_g', 'grad_w_ssd_out', 'grad_sc_conv_w', 'grad_w_sc_out', 'grad_w_o', 'grad_ffn_pre_g', 'grad_ffn_post_g', 'grad_w_up', 'grad_ffn_conv_w', 'grad_ffn_conv_b', 'grad_w_down', 'delta_ada_w', 'delta_ada_b', 'delta_mix_pre_g', 'delta_mix_post_g', 'delta_w_in', 'delta_ssd_conv_w', 'delta_ssd_conv_b', 'delta_ssd_dt_bias', 'delta_ssd_a_log', 'delta_ssd_d', 'delta_ssd_norm_g', 'delta_w_ssd_out', 'delta_sc_conv_w', 'delta_w_sc_out', 'delta_w_o', 'delta_ffn_pre_g', 'delta_ffn_post_g', 'delta_w_up', 'delta_ffn_conv_w', 'delta_ffn_conv_b', 'delta_w_down', 'new_m_ada_w', 'new_m_ada_b', 'new_m_mix_pre_g', 'new_m_mix_post_g', 'new_m_w_in', 'new_m_ssd_conv_w', 'new_m_ssd_conv_b', 'new_m_ssd_dt_bias', 'new_m_ssd_a_log', 'new_m_ssd_d', 'new_m_ssd_norm_g', 'new_m_w_ssd_out', 'new_m_sc_conv_w', 'new_m_w_sc_out', 'new_m_w_o', 'new_m_ffn_pre_g', 'new_m_ffn_post_g', 'new_m_w_up', 'new_m_ffn_conv_w', 'new_m_ffn_conv_b', 'new_m_w_down', 'new_v_ada_w', 'new_v_ada_b', 'new_v_mix_pre_g', 'new_v_mix_post_g', 'new_v_w_in', 'new_v_ssd_conv_w', 'new_v_ssd_conv_b', 'new_v_ssd_dt_bias', 'new_v_ssd_a_log', 'new_v_ssd_d', 'new_v_ssd_norm_g', 'new_v_w_ssd_out', 'new_v_sc_conv_w', 'new_v_w_sc_out', 'new_v_w_o', 'new_v_ffn_pre_g', 'new_v_ffn_post_g', 'new_v_w_up', 'new_v_ffn_conv_w', 'new_v_ffn_conv_b', 'new_v_w_down']
TWIN_LEAF_KINDS = {'loss': 'loss', 'grad_x': 'grad_x', 'grad_ada_w': 'grad_w', 'grad_ada_b': 'grad_w', 'grad_mix_pre_g': 'grad_w', 'grad_mix_post_g': 'grad_w', 'grad_w_in': 'grad_w', 'grad_ssd_conv_w': 'grad_w', 'grad_ssd_conv_b': 'grad_w', 'grad_ssd_dt_bias': 'grad_w', 'grad_ssd_a_log': 'grad_w', 'grad_ssd_d': 'grad_w', 'grad_ssd_norm_g': 'grad_w', 'grad_w_ssd_out': 'grad_w', 'grad_sc_conv_w': 'grad_w', 'grad_w_sc_out': 'grad_w', 'grad_w_o': 'grad_w', 'grad_ffn_pre_g': 'grad_w', 'grad_ffn_post_g': 'grad_w', 'grad_w_up': 'grad_w', 'grad_ffn_conv_w': 'grad_w', 'grad_ffn_conv_b': 'grad_w', 'grad_w_down': 'grad_w', 'delta_ada_w': 'delta_w', 'delta_ada_b': 'delta_w', 'delta_mix_pre_g': 'delta_w', 'delta_mix_post_g': 'delta_w', 'delta_w_in': 'delta_w', 'delta_ssd_conv_w': 'delta_w', 'delta_ssd_conv_b': 'delta_w', 'delta_ssd_dt_bias': 'delta_w', 'delta_ssd_a_log': 'delta_w', 'delta_ssd_d': 'delta_w', 'delta_ssd_norm_g': 'delta_w', 'delta_w_ssd_out': 'delta_w', 'delta_sc_conv_w': 'delta_w', 'delta_w_sc_out': 'delta_w', 'delta_w_o': 'delta_w', 'delta_ffn_pre_g': 'delta_w', 'delta_ffn_post_g': 'delta_w', 'delta_w_up': 'delta_w', 'delta_ffn_conv_w': 'delta_w', 'delta_ffn_conv_b': 'delta_w', 'delta_w_down': 'delta_w', 'new_m_ada_w': 'new_m', 'new_m_ada_b': 'new_m', 'new_m_mix_pre_g': 'new_m', 'new_m_mix_post_g': 'new_m', 'new_m_w_in': 'new_m', 'new_m_ssd_conv_w': 'new_m', 'new_m_ssd_conv_b': 'new_m', 'new_m_ssd_dt_bias': 'new_m', 'new_m_ssd_a_log': 'new_m', 'new_m_ssd_d': 'new_m', 'new_m_ssd_norm_g': 'new_m', 'new_m_w_ssd_out': 'new_m', 'new_m_sc_conv_w': 'new_m', 'new_m_w_sc_out': 'new_m', 'new_m_w_o': 'new_m', 'new_m_ffn_pre_g': 'new_m', 'new_m_ffn_post_g': 'new_m', 'new_m_w_up': 'new_m', 'new_m_ffn_conv_w': 'new_m', 'new_m_ffn_conv_b': 'new_m', 'new_m_w_down': 'new_m', 'new_v_ada_w': 'new_v', 'new_v_ada_b': 'new_v', 'new_v_mix_pre_g': 'new_v', 'new_v_mix_post_g': 'new_v', 'new_v_w_in': 'new_v', 'new_v_ssd_conv_w': 'new_v', 'new_v_ssd_conv_b': 'new_v', 'new_v_ssd_dt_bias': 'new_v', 'new_v_ssd_a_log': 'new_v', 'new_v_ssd_d': 'new_v', 'new_v_ssd_norm_g': 'new_v', 'new_v_w_ssd_out': 'new_v', 'new_v_sc_conv_w': 'new_v', 'new_v_w_sc_out': 'new_v', 'new_v_w_o': 'new_v', 'new_v_ffn_pre_g': 'new_v', 'new_v_ffn_post_g': 'new_v', 'new_v_w_up': 'new_v', 'new_v_ffn_conv_w': 'new_v', 'new_v_ffn_conv_b': 'new_v', 'new_v_w_down': 'new_v'}


def _forward(args):
    return _fwd_reference(*[args[k] for k in FWD_PARAMS])


def _output_shape():
    out = _jax.eval_shape(lambda: _forward(_fwd_setup_inputs(0)))
    return out.shape, out.dtype

N_MICROBATCH = 1
ADAM_LR = 0.001
ADAM_B1 = 0.9
ADAM_B2 = 0.999
ADAM_EPS = 1e-08
ADAM_WD = 0.01
ADAM_STEP = 10
PER_EXAMPLE_BATCH_AXIS = {'x': 0, 'c': 0, 'loss_target': 0}
SHARED_INPUTS = []
_WEIGHT_DTYPES = {'ada_w': _jnp.float32, 'ada_b': _jnp.float32, 'mix_pre_g': _jnp.float32, 'mix_post_g': _jnp.float32, 'w_in': _jnp.float32, 'ssd_conv_w': _jnp.float32, 'ssd_conv_b': _jnp.float32, 'ssd_dt_bias': _jnp.float32, 'ssd_a_log': _jnp.float32, 'ssd_d': _jnp.float32, 'ssd_norm_g': _jnp.float32, 'w_ssd_out': _jnp.float32, 'sc_conv_w': _jnp.float32, 'w_sc_out': _jnp.float32, 'w_o': _jnp.float32, 'ffn_pre_g': _jnp.float32, 'ffn_post_g': _jnp.float32, 'w_up': _jnp.float32, 'ffn_conv_w': _jnp.float32, 'ffn_conv_b': _jnp.float32, 'w_down': _jnp.float32}
MOMENT_SCALE = {'ada_w': 3.327848e+00, 'ada_b': 6.362462e+00, 'mix_pre_g': 2.382251e-01, 'mix_post_g': 7.495270e+00, 'w_in': 8.100325e-02, 'ssd_conv_w': 6.374858e-02, 'ssd_conv_b': 1.175940e-01, 'ssd_dt_bias': 1.435141e-01, 'ssd_a_log': 4.916635e-01, 'ssd_d': 3.511300e-01, 'ssd_norm_g': 9.539391e-02, 'w_ssd_out': 1.312080e-01, 'sc_conv_w': 1.173924e-01, 'w_sc_out': 1.211048e-01, 'w_o': 1.883860e-01, 'ffn_pre_g': 1.855835e-01, 'ffn_post_g': 7.444473e+00, 'w_up': 9.009744e-02, 'ffn_conv_w': 9.225496e-02, 'ffn_conv_b': 1.376713e-01, 'w_down': 1.654774e-01}


def _to_microbatches(a, axis):
    t = _jnp.moveaxis(a, axis, 0)
    t = t.reshape((N_MICROBATCH, t.shape[0] // N_MICROBATCH) + t.shape[1:])
    return _jnp.moveaxis(t, 1, axis + 1)


def setup_inputs(seed: int = 0) -> dict:
    inp = _fwd_setup_inputs(seed)
    key = _jax.random.fold_in(_jax.random.key(seed), 7919)
    shape, _ = _output_shape()
    out = dict(inp)
    out["loss_target"] = _jax.random.normal(_jax.random.fold_in(key, 0), shape, _jnp.float32)
    for i, name in enumerate(TWIN_WEIGHTS):
        w = inp[name].astype(_jnp.float32)
        if MOMENT_SCALE is None:
            s = _jnp.sqrt(_jnp.mean(_jnp.square(w)) + 1e-30)
        else:
            s = MOMENT_SCALE[name]
        km, kv = _jax.random.split(_jax.random.fold_in(key, i + 1))
        out[name] = w
        out["m_" + name] = s * _jax.random.normal(km, w.shape, _jnp.float32)
        out["v_" + name] = (s * s) * _jax.random.uniform(kv, w.shape, _jnp.float32, 0.5, 1.5)
    if N_MICROBATCH > 1:
        for name, axis in PER_EXAMPLE_BATCH_AXIS.items():
            out[name] = _to_microbatches(out[name], axis)
    return {'x': out['x'], 'c': out['c'], 'ada_w': out['ada_w'], 'ada_b': out['ada_b'], 'mix_pre_g': out['mix_pre_g'], 'mix_post_g': out['mix_post_g'], 'w_in': out['w_in'], 'ssd_conv_w': out['ssd_conv_w'], 'ssd_conv_b': out['ssd_conv_b'], 'ssd_dt_bias': out['ssd_dt_bias'], 'ssd_a_log': out['ssd_a_log'], 'ssd_d': out['ssd_d'], 'ssd_norm_g': out['ssd_norm_g'], 'w_ssd_out': out['w_ssd_out'], 'sc_conv_w': out['sc_conv_w'], 'w_sc_out': out['w_sc_out'], 'w_o': out['w_o'], 'ffn_pre_g': out['ffn_pre_g'], 'ffn_post_g': out['ffn_post_g'], 'w_up': out['w_up'], 'ffn_conv_w': out['ffn_conv_w'], 'ffn_conv_b': out['ffn_conv_b'], 'w_down': out['w_down'], 'loss_target': out['loss_target'], 'm_ada_w': out['m_ada_w'], 'm_ada_b': out['m_ada_b'], 'm_mix_pre_g': out['m_mix_pre_g'], 'm_mix_post_g': out['m_mix_post_g'], 'm_w_in': out['m_w_in'], 'm_ssd_conv_w': out['m_ssd_conv_w'], 'm_ssd_conv_b': out['m_ssd_conv_b'], 'm_ssd_dt_bias': out['m_ssd_dt_bias'], 'm_ssd_a_log': out['m_ssd_a_log'], 'm_ssd_d': out['m_ssd_d'], 'm_ssd_norm_g': out['m_ssd_norm_g'], 'm_w_ssd_out': out['m_w_ssd_out'], 'm_sc_conv_w': out['m_sc_conv_w'], 'm_w_sc_out': out['m_w_sc_out'], 'm_w_o': out['m_w_o'], 'm_ffn_pre_g': out['m_ffn_pre_g'], 'm_ffn_post_g': out['m_ffn_post_g'], 'm_w_up': out['m_w_up'], 'm_ffn_conv_w': out['m_ffn_conv_w'], 'm_ffn_conv_b': out['m_ffn_conv_b'], 'm_w_down': out['m_w_down'], 'v_ada_w': out['v_ada_w'], 'v_ada_b': out['v_ada_b'], 'v_mix_pre_g': out['v_mix_pre_g'], 'v_mix_post_g': out['v_mix_post_g'], 'v_w_in': out['v_w_in'], 'v_ssd_conv_w': out['v_ssd_conv_w'], 'v_ssd_conv_b': out['v_ssd_conv_b'], 'v_ssd_dt_bias': out['v_ssd_dt_bias'], 'v_ssd_a_log': out['v_ssd_a_log'], 'v_ssd_d': out['v_ssd_d'], 'v_ssd_norm_g': out['v_ssd_norm_g'], 'v_w_ssd_out': out['v_w_ssd_out'], 'v_sc_conv_w': out['v_sc_conv_w'], 'v_w_sc_out': out['v_w_sc_out'], 'v_w_o': out['v_w_o'], 'v_ffn_pre_g': out['v_ffn_pre_g'], 'v_ffn_post_g': out['v_ffn_post_g'], 'v_w_up': out['v_w_up'], 'v_ffn_conv_w': out['v_ffn_conv_w'], 'v_ffn_conv_b': out['v_ffn_conv_b'], 'v_w_down': out['v_w_down']}


def _loss(weights, diff, rest, loss_target):
    with _jax.named_scope("forward"):
        args = {**rest, TWIN_DIFF_INPUT: diff, **{k: w.astype(_WEIGHT_DTYPES[k]) for k, w in weights.items()}}
        y = _forward(args)
    with _jax.named_scope("loss_head"):
        err = _jnp.square(y.astype(_jnp.float32) - loss_target)
        return 0.5 * _jnp.sum(_jnp.mean(err, axis=-1)) if err.ndim else 0.5 * err


def _adamw(w, g, m, v):
    m = ADAM_B1 * m + (1.0 - ADAM_B1) * g
    v = ADAM_B2 * v + (1.0 - ADAM_B2) * _jnp.square(g)
    m_hat = m / (1.0 - ADAM_B1 ** ADAM_STEP)
    v_hat = v / (1.0 - ADAM_B2 ** ADAM_STEP)
    delta = -ADAM_LR * (m_hat / (_jnp.sqrt(v_hat) + ADAM_EPS) + ADAM_WD * w)
    return delta, m, v


def reference(x, c, ada_w, ada_b, mix_pre_g, mix_post_g, w_in, ssd_conv_w, ssd_conv_b, ssd_dt_bias, ssd_a_log, ssd_d, ssd_norm_g, w_ssd_out, sc_conv_w, w_sc_out, w_o, ffn_pre_g, ffn_post_g, w_up, ffn_conv_w, ffn_conv_b, w_down, loss_target, m_ada_w, m_ada_b, m_mix_pre_g, m_mix_post_g, m_w_in, m_ssd_conv_w, m_ssd_conv_b, m_ssd_dt_bias, m_ssd_a_log, m_ssd_d, m_ssd_norm_g, m_w_ssd_out, m_sc_conv_w, m_w_sc_out, m_w_o, m_ffn_pre_g, m_ffn_post_g, m_w_up, m_ffn_conv_w, m_ffn_conv_b, m_w_down, v_ada_w, v_ada_b, v_mix_pre_g, v_mix_post_g, v_w_in, v_ssd_conv_w, v_ssd_conv_b, v_ssd_dt_bias, v_ssd_a_log, v_ssd_d, v_ssd_norm_g, v_w_ssd_out, v_sc_conv_w, v_w_sc_out, v_w_o, v_ffn_pre_g, v_ffn_post_g, v_w_up, v_ffn_conv_w, v_ffn_conv_b, v_w_down):
    given = dict(x=x, c=c, ada_w=ada_w, ada_b=ada_b, mix_pre_g=mix_pre_g, mix_post_g=mix_post_g, w_in=w_in, ssd_conv_w=ssd_conv_w, ssd_conv_b=ssd_conv_b, ssd_dt_bias=ssd_dt_bias, ssd_a_log=ssd_a_log, ssd_d=ssd_d, ssd_norm_g=ssd_norm_g, w_ssd_out=w_ssd_out, sc_conv_w=sc_conv_w, w_sc_out=w_sc_out, w_o=w_o, ffn_pre_g=ffn_pre_g, ffn_post_g=ffn_post_g, w_up=w_up, ffn_conv_w=ffn_conv_w, ffn_conv_b=ffn_conv_b, w_down=w_down, loss_target=loss_target, m_ada_w=m_ada_w, m_ada_b=m_ada_b, m_mix_pre_g=m_mix_pre_g, m_mix_post_g=m_mix_post_g, m_w_in=m_w_in, m_ssd_conv_w=m_ssd_conv_w, m_ssd_conv_b=m_ssd_conv_b, m_ssd_dt_bias=m_ssd_dt_bias, m_ssd_a_log=m_ssd_a_log, m_ssd_d=m_ssd_d, m_ssd_norm_g=m_ssd_norm_g, m_w_ssd_out=m_w_ssd_out, m_sc_conv_w=m_sc_conv_w, m_w_sc_out=m_w_sc_out, m_w_o=m_w_o, m_ffn_pre_g=m_ffn_pre_g, m_ffn_post_g=m_ffn_post_g, m_w_up=m_w_up, m_ffn_conv_w=m_ffn_conv_w, m_ffn_conv_b=m_ffn_conv_b, m_w_down=m_w_down, v_ada_w=v_ada_w, v_ada_b=v_ada_b, v_mix_pre_g=v_mix_pre_g, v_mix_post_g=v_mix_post_g, v_w_in=v_w_in, v_ssd_conv_w=v_ssd_conv_w, v_ssd_conv_b=v_ssd_conv_b, v_ssd_dt_bias=v_ssd_dt_bias, v_ssd_a_log=v_ssd_a_log, v_ssd_d=v_ssd_d, v_ssd_norm_g=v_ssd_norm_g, v_w_ssd_out=v_w_ssd_out, v_sc_conv_w=v_sc_conv_w, v_w_sc_out=v_w_sc_out, v_w_o=v_w_o, v_ffn_pre_g=v_ffn_pre_g, v_ffn_post_g=v_ffn_post_g, v_w_up=v_w_up, v_ffn_conv_w=v_ffn_conv_w, v_ffn_conv_b=v_ffn_conv_b, v_w_down=v_w_down)
    weights = {n: given[n] for n in TWIN_WEIGHTS}
    shared = {n: given[n] for n in SHARED_INPUTS}
    per_example = {n: given[n] for n in ['x', 'c']}
    grad_fn = _jax.value_and_grad(_loss, argnums=(0, 1))

    def one_microbatch(ex, loss_target):
        ex = dict(ex)
        diff = ex.pop(TWIN_DIFF_INPUT)
        return grad_fn(weights, diff, {**shared, **ex}, loss_target)

    if N_MICROBATCH == 1:
        loss, (grad_w, grad_x) = one_microbatch(per_example, given["loss_target"])
    else:
        def body(carry, xs):
            loss_sum, grad_sum = carry
            l_k, (gw_k, gx_k) = one_microbatch(xs[0], xs[1])
            with _jax.named_scope("update"):
                return (loss_sum + l_k, _jax.tree.map(_jnp.add, grad_sum, gw_k)), gx_k

        init = (_jnp.zeros((), _jnp.float32), _jax.tree.map(_jnp.zeros_like, weights))
        (loss, grad_w), grad_x = _jax.lax.scan(body, init, (per_example, given["loss_target"]))
    with _jax.named_scope("update"):
        delta_w, new_m, new_v = {}, {}, {}
        for n in TWIN_WEIGHTS:
            delta_w[n], new_m[n], new_v[n] = _adamw(weights[n], grad_w[n], given["m_" + n], given["v_" + n])
    return (loss, grad_x, *[grad_w[n] for n in TWIN_WEIGHTS], *[delta_w[n] for n in TWIN_WEIGHTS],
            *[new_m[n] for n in TWIN_WEIGHTS], *[new_v[n] for n in TWIN_WEIGHTS])
```

```python
import functools

import jax
import jax.numpy as jnp
from jax import lax
from jax.experimental import pallas as pl
from jax.experimental.pallas import tpu as pltpu

F32 = jnp.float32
MXU_DTYPE = jnp.bfloat16
WIRE_DTYPE = jnp.bfloat16
HI = lax.Precision.HIGHEST
EPS = 1e-6
N_GROUPS = 4
CHUNK = 128
HEAD_DIM = 64
LANES = 128
HALO = 8
N_CHIPS = 4
N_DEV = 8
VMEM_LIMIT = 56 * 1024 * 1024
ADAM_LR, ADAM_B1, ADAM_B2, ADAM_EPS, ADAM_WD, ADAM_STEP = 0.001, 0.9, 0.999, 1e-08, 0.01, 10
MESH = pl.DeviceIdType.MESH

NN = (((1,), (0,)), ((), ()))
NT = (((1,), (1,)), ((), ()))
TN = (((0,), (0,)), ((), ()))


def _dg(a, b, dn, precision=None):
    return lax.dot_general(a, b, dn, precision=precision, preferred_element_type=F32)


def _tile(dim, pref, mult=LANES):
    t = (min(pref, dim) // mult) * mult
    while t >= mult:
        if dim % t == 0:
            return t
        t -= mult
    return dim


def _cp(sem):
    return pltpu.CompilerParams(dimension_semantics=sem, vmem_limit_bytes=VMEM_LIMIT)


def _sigmoid(x):
    return 1.0 / (1.0 + jnp.exp(-x))


def _softplus(x):
    return jnp.maximum(x, 0.0) + jnp.log1p(jnp.exp(-jnp.abs(x)))


def _rows8(v):
    v = v.reshape(1, -1).astype(F32)
    return jnp.pad(v, ((0, 7), (0, 0)))


def _matmul(a, b, mode, out_dtype, name, tm=512, tn=512, tk=512):
    if mode == "NN":
        (M, K), N = a.shape, b.shape[1]
    elif mode == "NT":
        (M, K), N = a.shape, b.shape[0]
    else:
        (K, M), N = a.shape, b.shape[1]
    tm, tn, tk = _tile(M, tm), _tile(N, tn), _tile(K, tk)
    nk = K // tk
    dn = {"NN": NN, "NT": NT, "TN": TN}[mode]

    def body(a_ref, b_ref, o_ref, acc_ref):
        k = pl.program_id(2)

        @pl.when(k == 0)
        def _():
            acc_ref[...] = jnp.zeros_like(acc_ref)

        acc_ref[...] += _dg(a_ref[...], b_ref[...], dn)

        @pl.when(k == nk - 1)
        def _():
            o_ref[...] = acc_ref[...].astype(o_ref.dtype)

    a_spec = (pl.BlockSpec((tk, tm), lambda i, j, k: (k, i)) if mode == "TN"
              else pl.BlockSpec((tm, tk), lambda i, j, k: (i, k)))
    b_spec = (pl.BlockSpec((tn, tk), lambda i, j, k: (j, k)) if mode == "NT"
              else pl.BlockSpec((tk, tn), lambda i, j, k: (k, j)))
    return pl.pallas_call(
        body, name=name, grid=(M // tm, N // tn, nk),
        in_specs=[a_spec, b_spec],
        out_specs=pl.BlockSpec((tm, tn), lambda i, j, k: (i, j)),
        out_shape=jax.ShapeDtypeStruct((M, N), out_dtype),
        scratch_shapes=[pltpu.VMEM((tm, tn), F32)],
        compiler_params=_cp(("parallel", "parallel", "arbitrary")),
    )(a, b)


def _ada_fwd(c16, ada_w, ada_b, name):
    rows, D = c16.shape
    N6 = ada_w.shape[1]
    tn = _tile(N6, 1536)

    def body(c_ref, w_ref, b_ref, act_ref, mod_ref):
        c = c_ref[...]
        act = (c * _sigmoid(c)).astype(act_ref.dtype)
        act_ref[...] = act
        mod_ref[...] = _dg(act, w_ref[...], NN) + b_ref[...]

    return pl.pallas_call(
        body, name=name, grid=(N6 // tn,),
        in_specs=[pl.BlockSpec((rows, D), lambda j: (0, 0)),
                  pl.BlockSpec((D, tn), lambda j: (0, j)),
                  pl.BlockSpec((1, tn), lambda j: (0, j))],
        out_specs=[pl.BlockSpec((rows, D), lambda j: (0, 0)),
                   pl.BlockSpec((rows, tn), lambda j: (0, j))],
        out_shape=[jax.ShapeDtypeStruct((rows, D), MXU_DTYPE),
                   jax.ShapeDtypeStruct((rows, N6), F32)],
        compiler_params=_cp(("arbitrary",)),
    )(c16, ada_w, ada_b.reshape(1, N6))


def _norm_mod_rows(x, g, sc, sh):
    r = lax.rsqrt(jnp.mean(x * x, axis=-1, keepdims=True) + EPS)
    return ((x * r) * g) * (1.0 + sc) + sh


def _norm_mod_fwd(x, g, mod, row_sh, name, ts=256):
    Bl, S, D = x.shape
    ts = _tile(S, ts, 8)

    def body(x_ref, g_ref, mod_ref, h_ref):
        sh = mod_ref[row_sh:row_sh + 1, :]
        sc = mod_ref[row_sh + 1:row_sh + 2, :]
        h_ref[...] = _norm_mod_rows(x_ref[...], g_ref[...], sc, sh).astype(h_ref.dtype)

    tok = pl.BlockSpec((None, ts, D), lambda b, s: (b, s, 0))
    return pl.pallas_call(
        body, name=name, grid=(Bl, S // ts),
        in_specs=[tok, pl.BlockSpec((1, D), lambda b, s: (0, 0)),
                  pl.BlockSpec((None, 8, D), lambda b, s: (b, 0, 0))],
        out_specs=tok, out_shape=jax.ShapeDtypeStruct((Bl, S, D), MXU_DTYPE),
        compiler_params=_cp(("parallel", "parallel")),
    )(x, g.reshape(1, D), mod)


def _post_norm_fwd(xp, f, post_g, mod, row_gt, nxt, name, ts=256):
    Bl, S, D = xp.shape
    ts = _tile(S, ts, 8)
    has_next = nxt is not None

    def body(*refs):
        if has_next:
            xp_ref, f_ref, pg_ref, mod_ref, ng_ref, nmod_ref, x_ref, h_ref = refs
        else:
            xp_ref, f_ref, pg_ref, mod_ref, x_ref = refs
        f = f_ref[...]
        r = lax.rsqrt(jnp.mean(f * f, axis=-1, keepdims=True) + EPS)
        x = xp_ref[...] + mod_ref[row_gt:row_gt + 1, :] * ((f * r) * pg_ref[...])
        x_ref[...] = x
        if has_next:
            rs = nxt[2]
            h_ref[...] = _norm_mod_rows(x, ng_ref[...], nmod_ref[rs + 1:rs + 2, :], nmod_ref[rs:rs + 1, :]).astype(h_ref.dtype)

    tok = pl.BlockSpec((None, ts, D), lambda b, s: (b, s, 0))
    vec = pl.BlockSpec((1, D), lambda b, s: (0, 0))
    modspec = pl.BlockSpec((None, 8, D), lambda b, s: (b, 0, 0))
    ins = [xp, f, post_g.reshape(1, D), mod]
    in_specs = [tok, tok, vec, modspec]
    out_specs = [tok]
    out_shape = [jax.ShapeDtypeStruct((Bl, S, D), F32)]
    if has_next:
        ins += [nxt[0].reshape(1, D), nxt[1]]
        in_specs += [vec, modspec]
        out_specs += [tok]
        out_shape += [jax.ShapeDtypeStruct((Bl, S, D), MXU_DTYPE)]
    out = pl.pallas_call(
        body, name=name, grid=(Bl, S // ts), in_specs=in_specs, out_specs=out_specs, out_shape=out_shape,
        compiler_params=_cp(("parallel", "parallel")),
    )(*ins)
    return (out[0], out[1]) if has_next else (out[0], None)


def _loss_fwd_bwd(y, target, name, ts=256):
    Bl, S, D = y.shape
    ts = _tile(S, ts, 8)

    def body(y_ref, t_ref, dy_ref, l_ref):
        @pl.when((pl.program_id(0) == 0) & (pl.program_id(1) == 0))
        def _():
            l_ref[...] = jnp.zeros_like(l_ref)

        e = y_ref[...] - t_ref[...]
        dy_ref[...] = e * (1.0 / D)
        l_ref[...] += 0.5 * jnp.sum(jnp.mean(e * e, axis=-1, keepdims=True), axis=0, keepdims=True)

    tok = pl.BlockSpec((None, ts, D), lambda b, s: (b, s, 0))
    dy, l = pl.pallas_call(
        body, name=name, grid=(Bl, S // ts), in_specs=[tok, tok],
        out_specs=[tok, pl.BlockSpec((8, LANES), lambda b, s: (0, 0))],
        out_shape=[jax.ShapeDtypeStruct((Bl, S, D), F32), jax.ShapeDtypeStruct((8, LANES), F32)],
        compiler_params=_cp(("arbitrary", "arbitrary")),
    )(y, target)
    return dy, l[0, 0]


def _norm_bwd(dx_res, nxt, prv, name, ts=256):
    Bl, S, D = dx_res.shape
    ts = _tile(S, ts, 8)
    has_next, has_prev = nxt is not None, prv is not None

    def body(*refs):
        refs = list(refs)
        dxr_ref = refs.pop(0)
        if has_next:
            dh_ref, x_ref, g_ref, nmod_ref = refs[:4]
            refs = refs[4:]
        if has_prev:
            f_ref, pg_ref, pmod_ref = refs[:3]
            refs = refs[3:]
        dx_ref = refs.pop(0)
        if has_prev:
            df_ref = refs.pop(0)
        pb_ref, sh_ref = refs
        b, s = pl.program_id(0), pl.program_id(1)

        @pl.when(s == 0)
        def _():
            pb_ref[...] = jnp.zeros_like(pb_ref)

        @pl.when((b == 0) & (s == 0))
        def _():
            sh_ref[...] = jnp.zeros_like(sh_ref)

        dx = dxr_ref[...]
        if has_next:
            rs = nxt[4]
            x, dh, g = x_ref[...], dh_ref[...], g_ref[...]
            sc1 = 1.0 + nmod_ref[rs + 1:rs + 2, :]
            r = lax.rsqrt(jnp.mean(x * x, axis=-1, keepdims=True) + EPS)
            xn = x * r
            pb_ref[0:1, :] += jnp.sum(dh, axis=0, keepdims=True)
            pb_ref[1:2, :] += jnp.sum(dh * (xn * g), axis=0, keepdims=True)
            sh_ref[0:1, :] += jnp.sum(dh * sc1 * xn, axis=0, keepdims=True)
            dxn = dh * sc1 * g
            dx = dx + r * (dxn - xn * jnp.mean(dxn * xn, axis=-1, keepdims=True))
        dx_ref[...] = dx
        if has_prev:
            rg = prv[3]
            f, pg = f_ref[...], pg_ref[...]
            gt = pmod_ref[rg:rg + 1, :]
            r = lax.rsqrt(jnp.mean(f * f, axis=-1, keepdims=True) + EPS)
            fn = f * r
            pb_ref[2:3, :] += jnp.sum(dx * (fn * pg), axis=0, keepdims=True)
            drn = dx * gt
            sh_ref[1:2, :] += jnp.sum(drn * fn, axis=0, keepdims=True)
            dfn = drn * pg
            df_ref[...] = (r * (dfn - fn * jnp.mean(dfn * fn, axis=-1, keepdims=True))).astype(df_ref.dtype)

    tok = pl.BlockSpec((None, ts, D), lambda b, s: (b, s, 0))
    vec = pl.BlockSpec((1, D), lambda b, s: (0, 0))
    modspec = pl.BlockSpec((None, 8, D), lambda b, s: (b, 0, 0))
    ins, in_specs = [dx_res], [tok]
    if has_next:
        ins += [nxt[0], nxt[1], nxt[2].reshape(1, D), nxt[3]]
        in_specs += [tok, tok, vec, modspec]
    if has_prev:
        ins += [prv[0], prv[1].reshape(1, D), prv[2]]
        in_specs += [tok, vec, modspec]
    out_specs, out_shape = [tok], [jax.ShapeDtypeStruct((Bl, S, D), F32)]
    if has_prev:
        out_specs += [tok]
        out_shape += [jax.ShapeDtypeStruct((Bl, S, D), MXU_DTYPE)]
    out_specs += [modspec, pl.BlockSpec((8, D), lambda b, s: (0, 0))]
    out_shape += [jax.ShapeDtypeStruct((Bl, 8, D), F32), jax.ShapeDtypeStruct((8, D), F32)]
    out = pl.pallas_call(
        body, name=name, grid=(Bl, S // ts), in_specs=in_specs, out_specs=out_specs, out_shape=out_shape,
        compiler_params=_cp(("arbitrary", "arbitrary")),
    )(*ins)
    if has_prev:
        return out[0], out[1], out[2], out[3]
    return out[0], None, out[1], out[2]


def _shift_down(x, j):
    return x if j == 0 else pltpu.roll(x, j, axis=0)


def _shift_up(x, j):
    return x if j == 0 else pltpu.roll(x, x.shape[0] - j, axis=0)


def _conv(xall, w_ref, K):
    y = w_ref[K - 1:K, :] * xall
    for k in range(K - 1):
        y = y + w_ref[k:k + 1, :] * _shift_down(xall, K - 1 - k)
    return y


def _conv_t(dall, w_ref, K):
    y = w_ref[K - 1:K, :] * dall
    for k in range(K - 1):
        y = y + w_ref[k:k + 1, :] * _shift_up(dall, K - 1 - k)
    return y


def _conv_wgrad(acc_ref, row0, dtile, xall, K, ts):
    for k in range(K):
        xs = _shift_down(xall, K - 1 - k)[HALO:HALO + ts]
        acc_ref[row0 + k:row0 + k + 1, :] += jnp.sum(dtile * xs, axis=0, keepdims=True)


def _halo_specs(ts, W, nS, colblk):
    per = ts // HALO
    tile = pl.BlockSpec((None, ts, W), lambda b, c, s: (b, s, colblk(c)))
    prev = pl.BlockSpec((None, HALO, W), lambda b, c, s: (b, jnp.maximum(s * per - 1, 0), colblk(c)))
    nxt = pl.BlockSpec((None, HALO, W), lambda b, c, s: (b, jnp.minimum((s + 1) * per, nS * per - 1), colblk(c)))
    return tile, prev, nxt


def _masked(ref, keep):
    v = ref[...]
    return jnp.where(keep, v, jnp.zeros_like(v))


def _ssd_conv_fwd(proj, w8, b, off, CD, name, ts=256, W=512):
    Bl, S, _ = proj.shape
    K = 4
    ts, W = _tile(S, ts, 8), _tile(CD, W)
    assert off % W == 0
    nS, nW, ob = S // ts, CD // W, off // W

    def body(x_ref, xp_ref, w_ref, b_ref, o_ref):
        s = pl.program_id(2)
        xall = jnp.concatenate([_masked(xp_ref, s > 0), x_ref[...]], axis=0)
        xc = _conv(xall, w_ref, K)[HALO:] + b_ref[...]
        o_ref[...] = xc * _sigmoid(xc)

    tile, prev, _ = _halo_specs(ts, W, nS, lambda c: ob + c)
    return pl.pallas_call(
        body, name=name, grid=(Bl, nW, nS),
        in_specs=[tile, prev, pl.BlockSpec((8, W), lambda b_, c, s: (0, c)), pl.BlockSpec((1, W), lambda b_, c, s: (0, c))],
        out_specs=pl.BlockSpec((None, ts, W), lambda b_, c, s: (b_, s, c)),
        out_shape=jax.ShapeDtypeStruct((Bl, S, CD), F32),
        compiler_params=_cp(("parallel", "parallel", "parallel")),
    )(proj, proj, w8, b.reshape(1, CD))


def _ssd_conv_bwd(dxa, proj, w8, b, off, CD, name, ts=256, W=512):
    Bl, S, _ = proj.shape
    K = 4
    ts, W = _tile(S, ts, 8), _tile(CD, W)
    nS, nW, ob = S // ts, CD // W, off // W

    def body(d_ref, dn_ref, x_ref, xp_ref, xn_ref, w_ref, b_ref, dx_ref, dw_ref, db_ref):
        bb, s = pl.program_id(1), pl.program_id(2)

        @pl.when((bb == 0) & (s == 0))
        def _():
            dw_ref[...] = jnp.zeros_like(dw_ref)
            db_ref[...] = jnp.zeros_like(db_ref)

        last = s == nS - 1
        xall = jnp.concatenate([_masked(xp_ref, s > 0), x_ref[...], _masked(xn_ref, ~last)], axis=0)
        dall = jnp.concatenate([jnp.zeros((HALO, W), F32), d_ref[...], _masked(dn_ref, ~last)], axis=0)
        xc = _conv(xall, w_ref, K) + b_ref[...]
        sg = _sigmoid(xc)
        dxc = dall * (sg * (1.0 + xc * (1.0 - sg)))
        dx_ref[...] = _conv_t(dxc, w_ref, K)[HALO:HALO + ts].astype(dx_ref.dtype)
        dtile = dxc[HALO:HALO + ts]
        db_ref[0:1, :] += jnp.sum(dtile, axis=0, keepdims=True)
        _conv_wgrad(dw_ref, 0, dtile, xall, K, ts)

    per = ts // HALO
    dtile_s = pl.BlockSpec((None, ts, W), lambda c, b_, s: (b_, s, c))
    dnext_s = pl.BlockSpec((None, HALO, W), lambda c, b_, s: (b_, jnp.minimum((s + 1) * per, nS * per - 1), c))
    xtile_s = pl.BlockSpec((None, ts, W), lambda c, b_, s: (b_, s, ob + c))
    xprev_s = pl.BlockSpec((None, HALO, W), lambda c, b_, s: (b_, jnp.maximum(s * per - 1, 0), ob + c))
    xnext_s = pl.BlockSpec((None, HALO, W), lambda c, b_, s: (b_, jnp.minimum((s + 1) * per, nS * per - 1), ob + c))
    wspec = pl.BlockSpec((8, W), lambda c, b_, s: (0, c))
    return pl.pallas_call(
        body, name=name, grid=(nW, Bl, nS),
        in_specs=[dtile_s, dnext_s, xtile_s, xprev_s, xnext_s, wspec, pl.BlockSpec((1, W), lambda c, b_, s: (0, c))],
        out_specs=[dtile_s, wspec, wspec],
        out_shape=[jax.ShapeDtypeStruct((Bl, S, CD), MXU_DTYPE), jax.ShapeDtypeStruct((8, CD), F32),
                   jax.ShapeDtypeStruct((8, CD), F32)],
        compiler_params=_cp(("arbitrary", "arbitrary", "arbitrary")),
    )(dxa, dxa, proj, proj, proj, w8, b.reshape(1, CD))


def _sc_conv_fwd(proj, w8, offs, D, name, ts=256, W=512):
    Bl, S, _ = proj.shape
    K = 3
    ts, W = _tile(S, ts, 8), _tile(D, W)
    nS, nW = S // ts, D // W
    ob, oc, oh = [o // W for o in offs]

    def body(b_ref, c_ref, cp_ref, h_ref, hp_ref, w_ref, o_ref):
        s = pl.program_id(2)
        keep = s > 0
        vall = jnp.concatenate([_masked(cp_ref, keep) * _masked(hp_ref, keep), c_ref[...] * h_ref[...]], axis=0)
        o_ref[...] = (b_ref[...] * _conv(vall, w_ref, K)[HALO:]).astype(o_ref.dtype)

    tb, _, _ = _halo_specs(ts, W, nS, lambda c: ob + c)
    tc, pc, _ = _halo_specs(ts, W, nS, lambda c: oc + c)
    th, ph, _ = _halo_specs(ts, W, nS, lambda c: oh + c)
    return pl.pallas_call(
        body, name=name, grid=(Bl, nW, nS),
        in_specs=[tb, tc, pc, th, ph, pl.BlockSpec((8, W), lambda b_, c, s: (0, c))],
        out_specs=pl.BlockSpec((None, ts, W), lambda b_, c, s: (b_, s, c)),
        out_shape=jax.ShapeDtypeStruct((Bl, S, D), MXU_DTYPE),
        compiler_params=_cp(("parallel", "parallel", "parallel")),
    )(proj, proj, proj, proj, proj, w8)


def _sc_conv_bwd(ds, proj, w8, offs, D, name, ts=256, W=512):
    Bl, S, _ = proj.shape
    K = 3
    ts, W = _tile(S, ts, 8), _tile(D, W)
    nS, nW = S // ts, D // W
    ob, oc, oh = [o // W for o in offs]

    def body(d_ref, dn_ref, b_ref, bn_ref, c_ref, cp_ref, cn_ref, h_ref, hp_ref, hn_ref, w_ref,
             db_ref, dc_ref, dh_ref, dw_ref):
        bb, s = pl.program_id(1), pl.program_id(2)

        @pl.when((bb == 0) & (s == 0))
        def _():
            dw_ref[...] = jnp.zeros_like(dw_ref)

        first, last = s > 0, s < nS - 1
        zeros = jnp.zeros((HALO, W), F32)
        c_t, h_t = c_ref[...], h_ref[...]
        vall = jnp.concatenate([_masked(cp_ref, first) * _masked(hp_ref, first), c_t * h_t,
                                _masked(cn_ref, last) * _masked(hn_ref, last)], axis=0)
        dcv = jnp.concatenate([zeros, d_ref[...] * b_ref[...], _masked(dn_ref, last) * _masked(bn_ref, last)], axis=0)
        cv = _conv(vall, w_ref, K)[HALO:HALO + ts]
        db_ref[...] = (d_ref[...] * cv).astype(db_ref.dtype)
        dv = _conv_t(dcv, w_ref, K)[HALO:HALO + ts]
        dc_ref[...] = (dv * h_t).astype(dc_ref.dtype)
        dh_ref[...] = (dv * c_t).astype(dh_ref.dtype)
        _conv_wgrad(dw_ref, 0, dcv[HALO:HALO + ts], vall, K, ts)

    per = ts // HALO

    def specs(o):
        t = pl.BlockSpec((None, ts, W), lambda c, b_, s: (b_, s, o + c))
        p = pl.BlockSpec((None, HALO, W), lambda c, b_, s: (b_, jnp.maximum(s * per - 1, 0), o + c))
        n = pl.BlockSpec((None, HALO, W), lambda c, b_, s: (b_, jnp.minimum((s + 1) * per, nS * per - 1), o + c))
        return t, p, n

    dt_, _, dn_ = specs(0)
    bt, _, bn = specs(ob)
    ct, cp, cn = specs(oc)
    ht, hp, hn = specs(oh)
    wspec = pl.BlockSpec((8, W), lambda c, b_, s: (0, c))
    act = jax.ShapeDtypeStruct((Bl, S, D), MXU_DTYPE)
    return pl.pallas_call(
        body, name=name, grid=(nW, Bl, nS),
        in_specs=[dt_, dn_, bt, bn, ct, cp, cn, ht, hp, hn, wspec],
        out_specs=[dt_, dt_, dt_, wspec],
        out_shape=[act, act, act, jax.ShapeDtypeStruct((8, D), F32)],
        compiler_params=_cp(("arbitrary", "arbitrary", "arbitrary")),
    )(ds, ds, proj, proj, proj, proj, proj, proj, proj, proj, w8)


def _ffn_conv_fwd(up, w8, b, DFF, name, ts=256, W=512):
    Bl, S, _ = up.shape
    K = 3
    ts, W = _tile(S, ts, 8), _tile(DFF, W)
    nS, nW = S // ts, DFF // W

    def body(g_ref, gp_ref, v_ref, vp_ref, wg_ref, wv_ref, bg_ref, bv_ref, o_ref):
        keep = pl.program_id(2) > 0
        ug = _conv(jnp.concatenate([_masked(gp_ref, keep), g_ref[...]], axis=0), wg_ref, K)[HALO:] + bg_ref[...]
        uv = _conv(jnp.concatenate([_masked(vp_ref, keep), v_ref[...]], axis=0), wv_ref, K)[HALO:] + bv_ref[...]
        o_ref[...] = (ug * _sigmoid(ug) * uv).astype(o_ref.dtype)

    tg, pg, _ = _halo_specs(ts, W, nS, lambda c: c)
    tv, pv, _ = _halo_specs(ts, W, nS, lambda c: nW + c)
    wg = pl.BlockSpec((8, W), lambda b_, c, s: (0, c))
    wv = pl.BlockSpec((8, W), lambda b_, c, s: (0, nW + c))
    bg = pl.BlockSpec((1, W), lambda b_, c, s: (0, c))
    bv = pl.BlockSpec((1, W), lambda b_, c, s: (0, nW + c))
    b2 = b.reshape(1, 2 * DFF)
    return pl.pallas_call(
        body, name=name, grid=(Bl, nW, nS),
        in_specs=[tg, pg, tv, pv, wg, wv, bg, bv],
        out_specs=pl.BlockSpec((None, ts, W), lambda b_, c, s: (b_, s, c)),
        out_shape=jax.ShapeDtypeStruct((Bl, S, DFF), MXU_DTYPE),
        compiler_params=_cp(("parallel", "parallel", "parallel")),
    )(up, up, up, up, w8, w8, b2, b2)


def _ffn_conv_bwd(da, up, w8, b, DFF, name, ts=256, W=256):
    Bl, S, _ = up.shape
    K = 3
    ts, W = _tile(S, ts, 8), _tile(DFF, W)
    nS, nW = S // ts, DFF // W

    def body(d_ref, dn_ref, g_ref, gp_ref, gn_ref, v_ref, vp_ref, vn_ref, wg_ref, wv_ref, bg_ref, bv_ref,
             dg_ref, dv_ref, dwg_ref, dwv_ref, dbg_ref, dbv_ref):
        bb, s = pl.program_id(1), pl.program_id(2)

        @pl.when((bb == 0) & (s == 0))
        def _():
            for r in (dwg_ref, dwv_ref, dbg_ref, dbv_ref):
                r[...] = jnp.zeros_like(r)

        first, last = s > 0, s < nS - 1
        gall = jnp.concatenate([_masked(gp_ref, first), g_ref[...], _masked(gn_ref, last)], axis=0)
        vall = jnp.concatenate([_masked(vp_ref, first), v_ref[...], _masked(vn_ref, last)], axis=0)
        dall = jnp.concatenate([jnp.zeros((HALO, W), F32), d_ref[...], _masked(dn_ref, last)], axis=0)
        ug = _conv(gall, wg_ref, K) + bg_ref[...]
        uv = _conv(vall, wv_ref, K) + bv_ref[...]
        sg = _sigmoid(ug)
        dug = dall * uv * (sg * (1.0 + ug * (1.0 - sg)))
        duv = dall * (ug * sg)
        dg_ref[...] = _conv_t(dug, wg_ref, K)[HALO:HALO + ts].astype(dg_ref.dtype)
        dv_ref[...] = _conv_t(duv, wv_ref, K)[HALO:HALO + ts].astype(dv_ref.dtype)
        dug_t, duv_t = dug[HALO:HALO + ts], duv[HALO:HALO + ts]
        dbg_ref[0:1, :] += jnp.sum(dug_t, axis=0, keepdims=True)
        dbv_ref[0:1, :] += jnp.sum(duv_t, axis=0, keepdims=True)
        _conv_wgrad(dwg_ref, 0, dug_t, gall, K, ts)
        _conv_wgrad(dwv_ref, 0, duv_t, vall, K, ts)

    per = ts // HALO

    def specs(o):
        t = pl.BlockSpec((None, ts, W), lambda c, b_, s: (b_, s, o + c))
        p = pl.BlockSpec((None, HALO, W), lambda c, b_, s: (b_, jnp.maximum(s * per - 1, 0), o + c))
        n = pl.BlockSpec((None, HALO, W), lambda c, b_, s: (b_, jnp.minimum((s + 1) * per, nS * per - 1), o + c))
        return t, p, n

    dt_, _, dn_ = specs(0)
    gt, gp, gn = specs(0)
    vt, vp, vn = specs(nW)
    wg = pl.BlockSpec((8, W), lambda c, b_, s: (0, c))
    wv = pl.BlockSpec((8, W), lambda c, b_, s: (0, nW + c))
    bg = pl.BlockSpec((1, W), lambda c, b_, s: (0, c))
    bv = pl.BlockSpec((1, W), lambda c, b_, s: (0, nW + c))
    b2 = b.reshape(1, 2 * DFF)
    act = jax.ShapeDtypeStruct((Bl, S, DFF), MXU_DTYPE)
    small = jax.ShapeDtypeStruct((8, DFF), F32)
    dg, dv, dwg, dwv, dbg, dbv = pl.pallas_call(
        body, name=name, grid=(nW, Bl, nS),
        in_specs=[dt_, dn_, gt, gp, gn, vt, vp, vn, wg, wv, bg, bv],
        out_specs=[dt_, dt_, wg, wg, wg, wg],
        out_shape=[act, act, small, small, small, small],
        compiler_params=_cp(("arbitrary", "arbitrary", "arbitrary")),
    )(da, da, up, up, up, up, up, up, w8, w8, b2, b2)
    return dg, dv, jnp.concatenate([dwg, dwv], axis=1), jnp.concatenate([dbg, dbv], axis=1)


def _merge_fwd(proj, y_ssd, y_sc, off, D, name, ts=256):
    Bl, S, _ = proj.shape
    ts = _tile(S, ts, 8)
    og = off // D

    def body(g1_ref, g2_ref, a_ref, b_ref, o_ref):
        o_ref[...] = (_sigmoid(g1_ref[...]) * a_ref[...] + _sigmoid(g2_ref[...]) * b_ref[...]).astype(o_ref.dtype)

    tok = pl.BlockSpec((None, ts, D), lambda b, s: (b, s, 0))
    return pl.pallas_call(
        body, name=name, grid=(Bl, S // ts),
        in_specs=[pl.BlockSpec((None, ts, D), lambda b, s: (b, s, og)),
                  pl.BlockSpec((None, ts, D), lambda b, s: (b, s, og + 1)), tok, tok],
        out_specs=tok, out_shape=jax.ShapeDtypeStruct((Bl, S, D), MXU_DTYPE),
        compiler_params=_cp(("parallel", "parallel")),
    )(proj, proj, y_ssd, y_sc)


def _merge_bwd(dmixin, proj, y_ssd, y_sc, off, D, name, ts=256):
    Bl, S, _ = proj.shape
    ts = _tile(S, ts, 8)
    og = off // D

    def body(d_ref, g1_ref, g2_ref, a_ref, b_ref, da_ref, db_ref, dg1_ref, dg2_ref):
        d = d_ref[...]
        s1, s2 = _sigmoid(g1_ref[...]), _sigmoid(g2_ref[...])
        da_ref[...] = (d * s1).astype(da_ref.dtype)
        db_ref[...] = (d * s2).astype(db_ref.dtype)
        dg1_ref[...] = (d * a_ref[...] * (s1 * (1.0 - s1))).astype(dg1_ref.dtype)
        dg2_ref[...] = (d * b_ref[...] * (s2 * (1.0 - s2))).astype(dg2_ref.dtype)

    tok = pl.BlockSpec((None, ts, D), lambda b, s: (b, s, 0))
    act = jax.ShapeDtypeStruct((Bl, S, D), MXU_DTYPE)
    return pl.pallas_call(
        body, name=name, grid=(Bl, S // ts),
        in_specs=[tok, pl.BlockSpec((None, ts, D), lambda b, s: (b, s, og)),
                  pl.BlockSpec((None, ts, D), lambda b, s: (b, s, og + 1)), tok, tok],
        out_specs=[tok, tok, tok, tok], out_shape=[act, act, act, act],
        compiler_params=_cp(("parallel", "parallel")),
    )(dmixin, proj, proj, y_ssd, y_sc)


def _expand_cols(mat, R, lane):
    half = lane < HEAD_DIM
    return jnp.concatenate(
        [jnp.where(half, mat[:, 2 * q:2 * q + 1], mat[:, 2 * q + 1:2 * q + 2]) for q in range(R // 2)], axis=1)


def _head_rows(colvec, R, N):
    return jnp.concatenate([jnp.broadcast_to(colvec[r:r + 1, :], (HEAD_DIM, N)) for r in range(R)], axis=0)


def _ssd_common(dtr_ref, dtb_ref, alog_ref, sel, L):
    row = lax.broadcasted_iota(jnp.int32, (L, L), 0)
    col = lax.broadcasted_iota(jnp.int32, (L, L), 1)
    causal = row >= col
    eye = (row == col).astype(F32)
    dt_all = _softplus(dtr_ref[...] + dtb_ref[...])
    a_all = -jnp.exp(alog_ref[...])
    dtg = _dg(dt_all, sel, NN, HI)
    adtg = _dg(dt_all * a_all, sel, NN, HI)
    acs = _dg(causal.astype(F32), adtg, NN, HI)
    acs_t = _dg(eye, acs, NT, HI)
    return causal, row, col, dt_all, a_all, dtg, acs, acs_t


def _ssd_fwd(xbc_a, proj, dt_bias, a_log, d_skip, norm_g, sel, dm, name):
    Bl, S, DI, N, R, L, G = dm.Bl, dm.S, dm.DI, dm.N, dm.R, CHUNK, N_GROUPS
    RP = R * HEAD_DIM
    nc = S // L
    ob, ocm, odt = DI // N, DI // N + G, dm.ODT // LANES

    def body(xs_ref, bm_ref, cm_ref, z_ref, dtr_ref, dtb_ref, alog_ref, dsk_ref, ng_ref, sel_ref,
             y_ref, yn_ref, hp_ref, h_ref):
        @pl.when(pl.program_id(2) == 0)
        def _():
            h_ref[...] = jnp.zeros_like(h_ref)

        sel_g = sel_ref[...]
        causal, row, col, dt_all, a_all, dtg, acs, acs_t = _ssd_common(dtr_ref, dtb_ref, alog_ref, sel_g, L)
        lane = lax.broadcasted_iota(jnp.int32, (L, LANES), 1)
        lane1 = lax.broadcasted_iota(jnp.int32, (1, LANES), 1)
        xs = xs_ref[...]
        bmb, cmb = bm_ref[...].astype(MXU_DTYPE), cm_ref[...].astype(MXU_DTYPE)
        sg = _dg(cmb, bmb, NT)
        xdt = xs * _expand_cols(dtg, R, lane)
        xb = xdt.astype(MXU_DTYPE)
        parts = []
        for q in range(R // 2):
            x2 = xb[:, LANES * q:LANES * (q + 1)]
            ys = []
            for r in (2 * q, 2 * q + 1):
                dec = jnp.exp(jnp.where(causal, acs[:, r:r + 1] - acs_t[r:r + 1, :], -1e30))
                ys.append(_dg((sg * dec).astype(MXU_DTYPE), x2, NN))
            parts.append(jnp.where(lane < HEAD_DIM, ys[0], ys[1]))
        ydiag = jnp.concatenate(parts, axis=1)
        acs_last = acs[L - 1:L, :]
        h_cur = h_ref[...]
        hb = h_cur.astype(MXU_DTYPE)
        yoff = _dg(cmb, hb, NT) * _expand_cols(jnp.exp(acs), R, lane)
        st = _dg((xdt * _expand_cols(jnp.exp(acs_last - acs), R, lane)).astype(MXU_DTYPE), bmb, TN)
        hp_ref[...] = hb
        h_ref[...] = h_cur * _head_rows(jnp.exp(acs_t[:, L - 1:L]), R, N) + st
        dsk_g = _dg(jnp.broadcast_to(dsk_ref[...], (8, LANES)), sel_g, NN, HI)[0:1, :]
        y = ydiag + yoff + _expand_cols(dsk_g, R, lane1) * xs
        y_ref[...] = y
        z = z_ref[...]
        yg = y * (z * _sigmoid(z))
        rr = lax.rsqrt(jnp.mean(yg * yg, axis=-1, keepdims=True) + EPS)
        yn_ref[...] = (yg * rr * ng_ref[...]).astype(yn_ref.dtype)

    vec = pl.BlockSpec((1, LANES), lambda b, g, c: (0, 0))
    grp = pl.BlockSpec((None, L, RP), lambda b, g, c: (b, c, g))
    return pl.pallas_call(
        body, name=name, grid=(Bl, G, nc),
        in_specs=[grp,
                  pl.BlockSpec((None, L, N), lambda b, g, c: (b, c, ob + g)),
                  pl.BlockSpec((None, L, N), lambda b, g, c: (b, c, ocm + g)),
                  grp,
                  pl.BlockSpec((None, L, LANES), lambda b, g, c: (b, c, odt)),
                  vec, vec, vec,
                  pl.BlockSpec((1, RP), lambda b, g, c: (0, g)),
                  pl.BlockSpec((None, LANES, LANES), lambda b, g, c: (g, 0, 0))],
        out_specs=[grp, grp, pl.BlockSpec((None, None, None, RP, N), lambda b, g, c: (b, g, c, 0, 0))],
        out_shape=[jax.ShapeDtypeStruct((Bl, S, DI), F32), jax.ShapeDtypeStruct((Bl, S, DI), MXU_DTYPE),
                   jax.ShapeDtypeStruct((Bl, G, nc, RP, N), MXU_DTYPE)],
        scratch_shapes=[pltpu.VMEM((RP, N), F32)],
        compiler_params=_cp(("parallel", "parallel", "arbitrary")),
    )(xbc_a, xbc_a, xbc_a, proj, proj, dt_bias, a_log, d_skip, norm_g.reshape(1, DI), sel)


def _ssd_bwd(dyn, y, xbc_a, proj, hprev, dt_bias, a_log, d_skip, norm_g, sel, dm, name):
    Bl, S, DI, N, R, L, G = dm.Bl, dm.S, dm.DI, dm.N, dm.R, CHUNK, N_GROUPS
    RP = R * HEAD_DIM
    nc = S // L
    ob, ocm, odt = DI // N, DI // N + G, dm.ODT // LANES

    def body(dyn_ref, y_ref, z_ref, xs_ref, bm_ref, cm_ref, dtr_ref, hp_ref, dtb_ref, alog_ref, dsk_ref, ng_ref,
             sel_ref, dz_ref, dxs_ref, dbm_ref, dcm_ref, ddtr_ref, dpar_ref, dng_ref, dh_ref):
        b, c = pl.program_id(1), pl.program_id(2)

        @pl.when(c == 0)
        def _():
            dh_ref[...] = jnp.zeros_like(dh_ref)

        @pl.when((b == 0) & (c == 0))
        def _():
            dpar_ref[...] = jnp.zeros_like(dpar_ref)
            dng_ref[...] = jnp.zeros_like(dng_ref)

        sel_g = sel_ref[...]
        causal, row, col, dt_all, a_all, dtg, acs, acs_t = _ssd_common(dtr_ref, dtb_ref, alog_ref, sel_g, L)
        anti = col >= row
        lane = lax.broadcasted_iota(jnp.int32, (L, LANES), 1)
        lane1 = lax.broadcasted_iota(jnp.int32, (1, LANES), 1)
        et = (lax.shift_right_logical(lax.broadcasted_iota(jnp.int32, (RP, LANES), 0), 6)
              == lax.broadcasted_iota(jnp.int32, (RP, LANES), 1)).astype(F32)

        def headsum(v):
            return _dg(v, et, NN, HI)

        def to_all(v):
            return _dg(v, sel_g, NT, HI)

        xs, z, y, dyn = xs_ref[...], z_ref[...], y_ref[...], dyn_ref[...]
        bm, cm = bm_ref[...], cm_ref[...]
        bmb, cmb = bm.astype(MXU_DTYPE), cm.astype(MXU_DTYPE)
        hpb = hp_ref[...]
        ng = ng_ref[...]

        sz = _sigmoid(z)
        siluz = z * sz
        yg = y * siluz
        rr = lax.rsqrt(jnp.mean(yg * yg, axis=-1, keepdims=True) + EPS)
        yhat = yg * rr
        dng_ref[...] += jnp.sum(dyn * yhat, axis=0, keepdims=True)
        dyhat = dyn * ng
        dyg = rr * (dyhat - yhat * jnp.mean(dyhat * yhat, axis=-1, keepdims=True))
        dy = dyg * siluz
        dz_ref[...] = (dyg * y * (sz * (1.0 + z * (1.0 - sz)))).astype(dz_ref.dtype)

        dsk_g = _dg(jnp.broadcast_to(dsk_ref[...], (8, LANES)), sel_g, NN, HI)[0:1, :]
        dxs = dy * _expand_cols(dsk_g, R, lane1)
        dd_lane = headsum(jnp.sum(dy * xs, axis=0, keepdims=True))

        dt_exp = _expand_cols(dtg, R, lane)
        xdt = xs * dt_exp
        xb = xdt.astype(MXU_DTYPE)
        dyb = dy.astype(MXU_DTYPE)
        acs_last = acs[L - 1:L, :]
        e_a = jnp.exp(acs)
        dsd = jnp.exp(acs_last - acs)
        cd = jnp.exp(acs_last)
        cd_rows = _head_rows(jnp.exp(acs_t[:, L - 1:L]), R, N)

        q_ = _dg(cmb, hpb, NT)
        dq = dy * _expand_cols(e_a, R, lane)
        dqb = dq.astype(MXU_DTYPE)
        dcm = _dg(dqb, hpb, NN)
        dh_yoff = _dg(dqb, cmb, TN)
        d_a = headsum(dq * q_)

        dhn = dh_ref[...]
        dcd_lane = jnp.sum(_dg(dhn * hpb.astype(F32), et, TN, HI), axis=0, keepdims=True)
        d_a_last = dcd_lane * cd
        dh_ref[...] = dhn * cd_rows + dh_yoff
        dhnb = dhn.astype(MXU_DTYPE)

        ds_exp = _expand_cols(dsd, R, lane)
        e_ = _dg(bmb, dhnb, NT)
        dxdt = ds_exp * e_
        t2 = headsum(xdt * e_) * dsd
        d_a = d_a - t2
        d_a_last = d_a_last + jnp.sum(t2, axis=0, keepdims=True)
        dbm = _dg((xdt * ds_exp).astype(MXU_DTYPE), dhnb, NN)

        sg = _dg(cmb, bmb, NT)
        sg_t = _dg(bmb, cmb, NT)
        dsg = jnp.zeros((L, L), F32)
        dsg_t = jnp.zeros((L, L), F32)
        parts = []
        for q in range(R // 2):
            x2 = xb[:, LANES * q:LANES * (q + 1)]
            dy2 = dyb[:, LANES * q:LANES * (q + 1)]
            dxs2 = []
            for hh, r in enumerate((2 * q, 2 * q + 1)):
                mine = (lane < HEAD_DIM) if hh == 0 else (lane >= HEAD_DIM)
                dec = jnp.exp(jnp.where(causal, acs[:, r:r + 1] - acs_t[r:r + 1, :], -1e30))
                dec_t = jnp.exp(jnp.where(anti, acs_t[r:r + 1, :] - acs[:, r:r + 1], -1e30))
                dy2m = jnp.where(mine, dy2, jnp.zeros_like(dy2))
                dm_ = _dg(dy2m, x2, NT)
                dm_t = _dg(x2, dy2m, NT)
                m_t = sg_t * dec_t
                da_col = (jnp.sum(dm_ * (sg * dec), axis=1, keepdims=True)
                          - jnp.sum(dm_t * m_t, axis=1, keepdims=True))
                d_a = d_a + jnp.where(lane == r, da_col, 0.0)
                dsg = dsg + dm_ * dec
                dsg_t = dsg_t + dm_t * dec_t
                dxs2.append(_dg(m_t.astype(MXU_DTYPE), dy2, NN))
            parts.append(jnp.where(lane < HEAD_DIM, dxs2[0], dxs2[1]))
        dxdt = dxdt + jnp.concatenate(parts, axis=1)
        dcm = dcm + _dg(dsg.astype(MXU_DTYPE), bmb, NN)
        dbm = dbm + _dg(dsg_t.astype(MXU_DTYPE), cmb, NN)
        dcm_ref[...] = dcm
        dbm_ref[...] = dbm

        dxs_ref[...] = dxs + dxdt * dt_exp
        ddt = headsum(dxdt * xs)
        rowl = lax.broadcasted_iota(jnp.int32, (L, LANES), 0)
        d_a = d_a + jnp.where(rowl == L - 1, d_a_last, 0.0)
        dadt = _dg(anti.astype(F32), d_a, NN, HI)
        a_g = _dg(jnp.broadcast_to(a_all, (8, LANES)), sel_g, NN, HI)[0:1, :]
        ddt = ddt + dadt * a_g
        da_lane = jnp.sum(dadt * dtg, axis=0, keepdims=True)
        ddtr = to_all(ddt) * _sigmoid(dtr_ref[...] + dtb_ref[...])
        ddtr_ref[...] = ddtr
        dpar_ref[0:1, :] += jnp.sum(ddtr, axis=0, keepdims=True)
        small = to_all(jnp.concatenate([da_lane, dd_lane, jnp.zeros((6, LANES), F32)], axis=0))
        dpar_ref[1:2, :] += small[0:1, :] * a_all
        dpar_ref[2:3, :] += small[1:2, :]

    vec = pl.BlockSpec((1, LANES), lambda g, b, c: (0, 0))
    grp = pl.BlockSpec((None, L, RP), lambda g, b, c: (b, nc - 1 - c, g))
    bspec = pl.BlockSpec((None, L, N), lambda g, b, c: (b, nc - 1 - c, ob + g))
    cspec = pl.BlockSpec((None, L, N), lambda g, b, c: (b, nc - 1 - c, ocm + g))
    gn = pl.BlockSpec((None, L, N), lambda g, b, c: (b, nc - 1 - c, g))
    return pl.pallas_call(
        body, name=name, grid=(G, Bl, nc),
        in_specs=[grp, grp, grp, grp, bspec, cspec,
                  pl.BlockSpec((None, L, LANES), lambda g, b, c: (b, nc - 1 - c, odt)),
                  pl.BlockSpec((None, None, None, RP, N), lambda g, b, c: (b, g, nc - 1 - c, 0, 0)),
                  vec, vec, vec,
                  pl.BlockSpec((1, RP), lambda g, b, c: (0, g)),
                  pl.BlockSpec((None, LANES, LANES), lambda g, b, c: (g, 0, 0))],
        out_specs=[grp, grp, gn, gn,
                   pl.BlockSpec((None, None, L, LANES), lambda g, b, c: (b, g, nc - 1 - c, 0)),
                   pl.BlockSpec((None, 8, LANES), lambda g, b, c: (g, 0, 0)),
                   pl.BlockSpec((1, RP), lambda g, b, c: (0, g))],
        out_shape=[jax.ShapeDtypeStruct((Bl, S, DI), MXU_DTYPE), jax.ShapeDtypeStruct((Bl, S, DI), F32),
                   jax.ShapeDtypeStruct((Bl, S, G * N), F32), jax.ShapeDtypeStruct((Bl, S, G * N), F32),
                   jax.ShapeDtypeStruct((Bl, G, S, LANES), F32), jax.ShapeDtypeStruct((G, 8, LANES), F32),
                   jax.ShapeDtypeStruct((1, DI), F32)],
        scratch_shapes=[pltpu.VMEM((RP, N), F32)],
        compiler_params=_cp(("arbitrary", "arbitrary", "arbitrary")),
    )(dyn, y, proj, xbc_a, xbc_a, xbc_a, proj, hprev, dt_bias, a_log, d_skip, norm_g.reshape(1, DI), sel)


def _adamw(w, m, v, ga, gb, name, tr=128):
    rows, cols = w.shape
    tr = _tile(rows, tr, 8)
    c1 = 1.0 - ADAM_B1 ** ADAM_STEP
    c2 = 1.0 - ADAM_B2 ** ADAM_STEP
    two = gb is not None

    def body(*refs):
        if two:
            w_ref, m_ref, v_ref, ga_ref, gb_ref, g_ref, d_ref, nm_ref, nv_ref = refs
            g = ga_ref[...] + gb_ref[...]
        else:
            w_ref, m_ref, v_ref, ga_ref, g_ref, d_ref, nm_ref, nv_ref = refs
            g = ga_ref[...]
        nm = ADAM_B1 * m_ref[...] + (1.0 - ADAM_B1) * g
        nv = ADAM_B2 * v_ref[...] + (1.0 - ADAM_B2) * (g * g)
        g_ref[...] = g
        nm_ref[...] = nm
        nv_ref[...] = nv
        d_ref[...] = -ADAM_LR * ((nm / c1) / (jnp.sqrt(nv / c2) + ADAM_EPS) + ADAM_WD * w_ref[...])

    blk = pl.BlockSpec((tr, cols), lambda i: (i, 0))
    ins = [w, m, v, ga] + ([gb] if two else [])
    shp = jax.ShapeDtypeStruct((rows, cols), F32)
    return pl.pallas_call(
        body, name=name, grid=(rows // tr,), in_specs=[blk] * len(ins), out_specs=[blk] * 4,
        out_shape=[shp] * 4, compiler_params=_cp(("parallel",)),
    )(*ins)


def _sum_slots(buf, name, tr=256):
    n, rows, cols = buf.shape
    tr = _tile(rows, tr, 8)

    def body(b_ref, o_ref):
        acc = b_ref[0].astype(F32)
        for k in range(1, n):
            acc = acc + b_ref[k].astype(F32)
        o_ref[...] = acc

    return pl.pallas_call(
        body, name=name, grid=(rows // tr,),
        in_specs=[pl.BlockSpec((n, tr, cols), lambda i: (0, i, 0))],
        out_specs=pl.BlockSpec((tr, cols), lambda i: (i, 0)),
        out_shape=jax.ShapeDtypeStruct((rows, cols), F32), compiler_params=_cp(("parallel",)),
    )(buf)


_ANY = pl.BlockSpec(memory_space=pl.ANY)


def _exchange_chips(src, per_dest, name):
    rows, cols = src.shape[-2:]

    def body(in_ref, out_ref, send_sems, recv_sems, local_sem):
        x, y, c = lax.axis_index("x"), lax.axis_index("y"), lax.axis_index("c")
        me = 2 * x + y
        chips = [(1 - x, y), (x, 1 - y), (1 - x, 1 - y)]

        def block(j):
            return in_ref.at[j] if per_dest else in_ref

        mine = pltpu.make_async_copy(block(me), out_ref.at[me], local_sem)
        mine.start()
        sends = []
        for k, (px, py) in enumerate(chips):
            cp = pltpu.make_async_remote_copy(
                src_ref=block(2 * px + py), dst_ref=out_ref.at[me], send_sem=send_sems.at[k],
                recv_sem=recv_sems.at[k], device_id=(px, py, c), device_id_type=MESH)
            cp.start()
            sends.append(cp)
        for k, (px, py) in enumerate(chips):
            pltpu.make_async_remote_copy(
                src_ref=block(me), dst_ref=out_ref.at[2 * px + py], send_sem=send_sems.at[k],
                recv_sem=recv_sems.at[k], device_id=(px, py, c), device_id_type=MESH).wait_recv()
        for cp in sends:
            cp.wait_send()
        mine.wait()

    return pl.pallas_call(
        body, name=name, in_specs=[_ANY], out_specs=_ANY,
        out_shape=jax.ShapeDtypeStruct((N_CHIPS, rows, cols), src.dtype),
        scratch_shapes=[pltpu.SemaphoreType.DMA((3,)), pltpu.SemaphoreType.DMA((3,)), pltpu.SemaphoreType.DMA(())],
    )(src)


def _sibling_swap(v, name):
    def body(in_ref, out_ref, send_sem, recv_sem):
        x, y, c = lax.axis_index("x"), lax.axis_index("y"), lax.axis_index("c")
        cp = pltpu.make_async_remote_copy(src_ref=in_ref, dst_ref=out_ref, send_sem=send_sem, recv_sem=recv_sem,
                                          device_id=(x, y, 1 - c), device_id_type=MESH)
        cp.start()
        cp.wait()

    return pl.pallas_call(
        body, name=name, in_specs=[_ANY], out_specs=_ANY, out_shape=jax.ShapeDtypeStruct(v.shape, v.dtype),
        scratch_shapes=[pltpu.SemaphoreType.DMA(()), pltpu.SemaphoreType.DMA(())],
    )(v)


def _allgather_all(v, name):
    rows, cols = v.shape

    def body(in_ref, out_ref, send_sems, recv_sems, local_sem):
        x, y, c = lax.axis_index("x"), lax.axis_index("y"), lax.axis_index("c")
        me = 4 * x + 2 * y + c
        peers = []
        for k in range(1, N_DEV):
            peers.append(((1 - x) if k & 4 else x, (1 - y) if k & 2 else y, (1 - c) if k & 1 else c))
        mine = pltpu.make_async_copy(in_ref, out_ref.at[me], local_sem)
        mine.start()
        sends = []
        for k, peer in enumerate(peers):
            cp = pltpu.make_async_remote_copy(src_ref=in_ref, dst_ref=out_ref.at[me], send_sem=send_sems.at[k],
                                              recv_sem=recv_sems.at[k], device_id=peer, device_id_type=MESH)
            cp.start()
            sends.append(cp)
        for k, (px, py, pc) in enumerate(peers):
            pltpu.make_async_remote_copy(src_ref=in_ref, dst_ref=out_ref.at[4 * px + 2 * py + pc],
                                         send_sem=send_sems.at[k], recv_sem=recv_sems.at[k],
                                         device_id=(px, py, pc), device_id_type=MESH).wait_recv()
        for cp in sends:
            cp.wait_send()
        mine.wait()

    return pl.pallas_call(
        body, name=name, in_specs=[_ANY], out_specs=_ANY,
        out_shape=jax.ShapeDtypeStruct((N_DEV, rows, cols), v.dtype),
        scratch_shapes=[pltpu.SemaphoreType.DMA((N_DEV - 1,)), pltpu.SemaphoreType.DMA((N_DEV - 1,)),
                        pltpu.SemaphoreType.DMA(())],
    )(v)


def _pack(arrs, dtype, width, row_mult):
    flat = jnp.concatenate([a.reshape(-1).astype(dtype) for a in arrs])
    unit = width * row_mult
    total = -(-flat.shape[0] // unit) * unit
    return jnp.pad(flat, (0, total - flat.shape[0])).reshape(-1, width)


def _unpack(buf, shapes):
    flat = buf.reshape(-1)
    out, off = [], 0
    for shp in shapes:
        n = 1
        for d in shp:
            n *= d
        out.append(flat[off:off + n].reshape(shp))
        off += n
    return out


class _Dims:
    pass


def _dims(x, ssd_dt_bias, ssd_norm_g, ssd_conv_b, ffn_conv_b):
    dm = _Dims()
    dm.Bl, dm.S, dm.D = x.shape
    dm.H, dm.DI, dm.CD = ssd_dt_bias.shape[-1], ssd_norm_g.shape[-1], ssd_conv_b.shape[-1]
    dm.N = (dm.CD - dm.DI) // (2 * N_GROUPS)
    dm.R = dm.H // N_GROUPS
    dm.DFF = ffn_conv_b.shape[-1] // 2
    dm.OX = dm.DI
    dm.OB = dm.DI + dm.CD
    dm.OC = dm.OB + dm.D
    dm.OH = dm.OC + dm.D
    dm.OG = dm.OH + dm.D
    dm.ODT = dm.OG + 2 * dm.D
    dm.NP = dm.ODT + LANES
    assert dm.DI // dm.H == HEAD_DIM and dm.N == LANES and dm.R % 2 == 0 and dm.S % CHUNK == 0 and dm.H <= LANES
    return dm


def _permute_w_in(w, dm):
    o = dm.DI + dm.CD
    pad = jnp.zeros((w.shape[0], LANES - dm.H), w.dtype)
    return jnp.concatenate([w[:, :o], w[:, o + dm.H:], w[:, o:o + dm.H], pad], axis=1)


def _unpermute_w_in(dw, dm):
    o = dm.DI + dm.CD
    return jnp.concatenate([dw[:, :o], dw[:, dm.ODT:dm.ODT + dm.H], dw[:, o:dm.ODT]], axis=1)


def _lane_pad(v):
    return jnp.pad(v.reshape(1, -1).astype(F32), ((0, 0), (0, LANES - v.shape[-1])))


def _pad8(w):
    return jnp.pad(w.astype(F32), ((0, 8 - w.shape[0]), (0, 0)))


def _head_select(dm):
    j = jnp.arange(LANES)[None, :, None]
    r = jnp.arange(LANES)[None, None, :]
    g = jnp.arange(N_GROUPS)[:, None, None]
    return ((j == dm.R * g + r) & (r < dm.R)).astype(F32)


def _mix_fwd(dm, h, w, sp, sel, tag):
    Bl, S, D = dm.Bl, dm.S, dm.D
    T = Bl * S
    proj = _matmul(h.reshape(T, D), w["w_in_p"], "NN", F32, tag + "_in_proj", tk=1024).reshape(Bl, S, dm.NP)
    xbc_a = _ssd_conv_fwd(proj, sp["ssd_conv_w8"], sp["ssd_conv_b"], dm.OX, dm.CD, tag + "_ssd_conv")
    y, yn, hprev = _ssd_fwd(xbc_a, proj, sp["dt_bias"], sp["a_log"], sp["d_skip"], sp["ssd_norm_g"], sel, dm,
                            tag + "_ssd_scan")
    y_ssd = _matmul(yn.reshape(T, dm.DI), w["w_ssd_out"], "NN", F32, tag + "_ssd_out", tk=1024).reshape(Bl, S, D)
    s = _sc_conv_fwd(proj, sp["sc_conv_w8"], (dm.OB, dm.OC, dm.OH), D, tag + "_sc_conv")
    y_sc = _matmul(s.reshape(T, D), w["w_sc_out"], "NN", F32, tag + "_sc_out", tk=1024).reshape(Bl, S, D)
    mixin = _merge_fwd(proj, y_ssd, y_sc, dm.OG, D, tag + "_merge")
    mix = _matmul(mixin.reshape(T, D), w["w_o"], "NN", F32, tag + "_o", tk=1024).reshape(Bl, S, D)
    return mix, (h, proj, xbc_a, y, yn, hprev, y_ssd, y_sc, s, mixin)


def _mix_bwd(dm, dmix, saved, w, sp, sel, tag):
    Bl, S, D = dm.Bl, dm.S, dm.D
    T = Bl * S
    h, proj, xbc_a, y, yn, hprev, y_ssd, y_sc, s, mixin = saved
    dmix2 = dmix.reshape(T, D)
    g = {}
    g["w_o"] = _matmul(mixin.reshape(T, D), dmix2, "TN", F32, tag + "_dw_o", tk=1024)
    dmixin = _matmul(dmix2, w["w_o"], "NT", F32, tag + "_d_o", tk=1024).reshape(Bl, S, D)
    dy_ssd, dy_sc, dg1, dg2 = _merge_bwd(dmixin, proj, y_ssd, y_sc, dm.OG, D, tag + "_merge_bwd")
    g["w_sc_out"] = _matmul(s.reshape(T, D), dy_sc.reshape(T, D), "TN", F32, tag + "_dw_sc_out", tk=1024)
    ds = _matmul(dy_sc.reshape(T, D), w["w_sc_out"], "NT", F32, tag + "_d_sc_out", tk=1024).reshape(Bl, S, D)
    dscb, dscc, dsch, dscw = _sc_conv_bwd(ds, proj, sp["sc_conv_w8"], (dm.OB, dm.OC, dm.OH), D, tag + "_sc_conv_bwd")
    g["sc_conv_w"] = dscw[:3]
    g["w_ssd_out"] = _matmul(yn.reshape(T, dm.DI), dy_ssd.reshape(T, D), "TN", F32, tag + "_dw_ssd_out", tk=1024)
    dyn = _matmul(dy_ssd.reshape(T, D), w["w_ssd_out"], "NT", F32, tag + "_d_ssd_out", tk=1024).reshape(Bl, S, dm.DI)
    dz, dxs, dbm, dcm, ddtr_g, dpar, dng = _ssd_bwd(dyn, y, xbc_a, proj, hprev, sp["dt_bias"], sp["a_log"],
                                                    sp["d_skip"], sp["ssd_norm_g"], sel, dm, tag + "_ssd_scan_bwd")
    dpar = jnp.sum(dpar, axis=0)
    g["ssd_dt_bias"], g["ssd_a_log"], g["ssd_d"] = dpar[0, :dm.H], dpar[1, :dm.H], dpar[2, :dm.H]
    g["ssd_norm_g"] = dng[0]
    dxa = jnp.concatenate([dxs, dbm, dcm], axis=-1)
    dxbc, dcw, dcb = _ssd_conv_bwd(dxa, proj, sp["ssd_conv_w8"], sp["ssd_conv_b"], dm.OX, dm.CD, tag + "_ssd_conv_bwd")
    g["ssd_conv_w"], g["ssd_conv_b"] = dcw[:4], dcb[0]
    ddt = jnp.sum(ddtr_g, axis=1).astype(MXU_DTYPE)
    dproj = jnp.concatenate([dz, dxbc, dscb, dscc, dsch, dg1, dg2, ddt], axis=-1).reshape(T, dm.NP)
    g["w_in_p"] = _matmul(h.reshape(T, D), dproj, "TN", F32, tag + "_dw_in", tk=1024)
    dh = _matmul(dproj, w["w_in_p"], "NT", F32, tag + "_d_in", tk=1152).reshape(Bl, S, D)
    return dh, g


def _ffn_fwd(dm, h, w, sp, tag):
    Bl, S, D = dm.Bl, dm.S, dm.D
    T = Bl * S
    up = _matmul(h.reshape(T, D), w["w_up"], "NN", F32, tag + "_up", tk=1024).reshape(Bl, S, 2 * dm.DFF)
    a = _ffn_conv_fwd(up, sp["ffn_conv_w8"], sp["ffn_conv_b"], dm.DFF, tag + "_ffn_conv")
    f = _matmul(a.reshape(T, dm.DFF), w["w_down"], "NN", F32, tag + "_down", tk=1408).reshape(Bl, S, D)
    return f, (h, up, a)


def _ffn_bwd(dm, df, saved, w, sp, tag):
    Bl, S, D = dm.Bl, dm.S, dm.D
    T = Bl * S
    h, up, a = saved
    df2 = df.reshape(T, D)
    g = {}
    g["w_down"] = _matmul(a.reshape(T, dm.DFF), df2, "TN", F32, tag + "_dw_down", tk=1024)
    da = _matmul(df2, w["w_down"], "NT", F32, tag + "_d_down", tk=1024).reshape(Bl, S, dm.DFF)
    dg, dv, dcw, dcb = _ffn_conv_bwd(da, up, sp["ffn_conv_w8"], sp["ffn_conv_b"], dm.DFF, tag + "_ffn_conv_bwd")
    g["ffn_conv_w"], g["ffn_conv_b"] = dcw[:3], dcb[0]
    dup = jnp.concatenate([dg, dv], axis=-1).reshape(T, 2 * dm.DFF)
    g["w_up"] = _matmul(h.reshape(T, D), dup, "TN", F32, tag + "_dw_up", tk=1024)
    dh = _matmul(dup, w["w_up"], "NT", F32, tag + "_d_up", tk=1408).reshape(Bl, S, D)
    return dh, g


def _local_step(dm, x, c, target, wfull, small):
    Bl, S, D = dm.Bl, dm.S, dm.D
    depth = len(wfull)
    sel = _head_select(dm)
    c16 = jnp.pad(c.astype(F32), ((0, 16 - Bl), (0, 0)))
    sps, mods, acts = [], [], []
    for l in range(depth):
        sm = small[l]
        sps.append(dict(
            ssd_conv_w8=_pad8(sm["ssd_conv_w"]), ssd_conv_b=sm["ssd_conv_b"], dt_bias=_lane_pad(sm["ssd_dt_bias"]),
            a_log=_lane_pad(sm["ssd_a_log"]), d_skip=_lane_pad(sm["ssd_d"]), ssd_norm_g=sm["ssd_norm_g"],
            sc_conv_w8=_pad8(sm["sc_conv_w"]), ffn_conv_w8=_pad8(sm["ffn_conv_w"]), ffn_conv_b=sm["ffn_conv_b"]))
        act, mod = _ada_fwd(c16, wfull[l]["ada_w"], sm["ada_b"], f"l{l}_ada")
        acts.append(act)
        mods.append(jnp.pad(mod[:Bl].reshape(Bl, 6, D), ((0, 0), (0, 2), (0, 0))))

    def sub(i):
        l, ffn = i // 2, i % 2
        sm = small[l]
        return dict(l=l, ffn=ffn, pre_g=sm["ffn_pre_g" if ffn else "mix_pre_g"],
                    post_g=sm["ffn_post_g" if ffn else "mix_post_g"], mod=mods[l], row=3 * ffn,
                    tag=f"l{l}_{'ffn' if ffn else 'mix'}")

    nsub = 2 * depth
    subs = [sub(i) for i in range(nsub)]
    xs, fs, saves = [x], [], []
    h = _norm_mod_fwd(x, subs[0]["pre_g"], subs[0]["mod"], subs[0]["row"], "l0_mix_pre_norm")
    for i, sb in enumerate(subs):
        l = sb["l"]
        if sb["ffn"]:
            f, sv = _ffn_fwd(dm, h, wfull[l], sps[l], sb["tag"])
        else:
            f, sv = _mix_fwd(dm, h, wfull[l], sps[l], sel, sb["tag"])
        nxt = None
        if i + 1 < nsub:
            nb = subs[i + 1]
            nxt = (nb["pre_g"], nb["mod"], nb["row"])
        xn, h = _post_norm_fwd(xs[-1], f, sb["post_g"], sb["mod"], sb["row"] + 2, nxt, sb["tag"] + "_post_norm")
        xs.append(xn)
        fs.append(f)
        saves.append(sv)

    dy, loss = _loss_fwd_bwd(xs[-1], target, "loss")

    grads = [dict() for _ in range(depth)]
    dmod = [[None] * 6 for _ in range(depth)]
    dx, dh = dy, None
    for i in reversed(range(nsub)):
        sb = subs[i]
        l = sb["l"]
        nxt = None
        if i + 1 < nsub:
            nb = subs[i + 1]
            nxt = (dh, xs[i + 1], nb["pre_g"], nb["mod"], nb["row"])
        dx, df, pb, shg = _norm_bwd(dx, nxt, (fs[i], sb["post_g"], sb["mod"], sb["row"] + 2), sb["tag"] + "_post_norm_bwd")
        if nxt is not None:
            nb = subs[i + 1]
            dmod[nb["l"]][nb["row"]], dmod[nb["l"]][nb["row"] + 1] = pb[:, 0], pb[:, 1]
            grads[nb["l"]]["ffn_pre_g" if nb["ffn"] else "mix_pre_g"] = shg[0]
        dmod[l][sb["row"] + 2] = pb[:, 2]
        grads[l]["ffn_post_g" if sb["ffn"] else "mix_post_g"] = shg[1]
        if sb["ffn"]:
            dh, g = _ffn_bwd(dm, df, saves[i], wfull[l], sps[l], sb["tag"])
        else:
            dh, g = _mix_bwd(dm, df, saves[i], wfull[l], sps[l], sel, sb["tag"])
        grads[l].update(g)
    sb = subs[0]
    grad_x, _, pb, shg = _norm_bwd(dx, (dh, xs[0], sb["pre_g"], sb["mod"], sb["row"]), None, "l0_mix_pre_norm_bwd")
    dmod[0][0], dmod[0][1] = pb[:, 0], pb[:, 1]
    grads[0]["mix_pre_g"] = shg[0]

    for l in range(depth):
        dm6 = jnp.concatenate(dmod[l], axis=-1)
        grads[l]["ada_b"] = jnp.sum(dm6, axis=0)
        dm16 = jnp.pad(dm6, ((0, 16 - Bl), (0, 0))).astype(MXU_DTYPE)
        grads[l]["ada_w"] = _matmul(acts[l], dm16, "TN", F32, f"l{l}_dw_ada")
    return loss, grad_x, grads


_WEIGHTS = ("ada_w", "ada_b", "mix_pre_g", "mix_post_g", "w_in", "ssd_conv_w", "ssd_conv_b", "ssd_dt_bias",
            "ssd_a_log", "ssd_d", "ssd_norm_g", "w_ssd_out", "sc_conv_w", "w_sc_out", "w_o", "ffn_pre_g",
            "ffn_post_g", "w_up", "ffn_conv_w", "ffn_conv_b", "w_down")
_INPUTS = ("x", "c") + _WEIGHTS + ("loss_target",) + tuple("m_" + n for n in _WEIGHTS) + tuple("v_" + n for n in _WEIGHTS)
_BIG = {"ada_w": 2, "w_in": 2, "w_ssd_out": 1, "w_sc_out": 1, "w_o": 1, "w_up": 2, "w_down": 1}
_CONV = ("ssd_conv_w", "sc_conv_w", "ffn_conv_w")
_SMALL = tuple(n for n in _WEIGHTS if n not in _BIG)
PACK_COLS, PACK_ROWS = 1024, 256


def _block(a, axis, j, n=N_CHIPS):
    w = a.shape[axis] // n
    return lax.slice_in_dim(a, j * w, (j + 1) * w, axis=axis)


def _step(a):
    x, c, target = a["x"], a["c"], a["loss_target"]
    depth = a["ada_w"].shape[0]
    dm = _dims(x, a["ssd_dt_bias"], a["ssd_norm_g"], a["ssd_conv_b"], a["ffn_conv_b"])
    chip = 2 * lax.axis_index("x") + lax.axis_index("y")

    big_shapes = [a[n].shape for n in _BIG]
    got = _exchange_chips(_pack([a[n] for n in _BIG], WIRE_DTYPE, PACK_COLS, PACK_ROWS), False, "gather_weights")
    pieces = [_unpack(got[j], big_shapes) for j in range(N_CHIPS)]
    full = {n: jnp.concatenate([pieces[j][i] for j in range(N_CHIPS)], axis=ax).astype(MXU_DTYPE)
            for i, (n, ax) in enumerate(_BIG.items())}
    conv_shapes = [a[n].shape for n in _CONV]
    gotc = _exchange_chips(_pack([a[n] for n in _CONV], F32, LANES, 8), False, "gather_conv_weights")
    piecesc = [_unpack(gotc[j], conv_shapes) for j in range(N_CHIPS)]
    fullc = {n: jnp.concatenate([piecesc[j][i] for j in range(N_CHIPS)], axis=2) for i, n in enumerate(_CONV)}

    wfull, small = [], []
    for l in range(depth):
        wf = {n: full[n][l] for n in _BIG if n != "w_in"}
        wf["w_in_p"] = _permute_w_in(full["w_in"][l], dm)
        wfull.append(wf)
        small.append({n: (fullc[n][l] if n in _CONV else a[n][l]) for n in _SMALL})

    loss_part, grad_x, grads = _local_step(dm, x, c, target, wfull, small)

    gbig = {}
    for n in _BIG:
        per_layer = [(_unpermute_w_in(grads[l]["w_in_p"], dm) if n == "w_in" else grads[l][n]) for l in range(depth)]
        gbig[n] = jnp.stack(per_layer)
    parts = jnp.stack([_pack([_block(gbig[n], ax, j) for n, ax in _BIG.items()], WIRE_DTYPE, PACK_COLS, PACK_ROWS)
                       for j in range(N_CHIPS)])
    mine = _sum_slots(_exchange_chips(parts, True, "scatter_grads"), "sum_chip_grads")
    theirs = _sibling_swap(mine, "swap_core_grads")
    g_mine, g_theirs = _unpack(mine, big_shapes), _unpack(theirs, big_shapes)

    out = {}
    for i, n in enumerate(_BIG):
        shp = a[n].shape
        two_d = (shp[0] * shp[1], shp[2])
        res = _adamw(a[n].reshape(two_d), a["m_" + n].reshape(two_d), a["v_" + n].reshape(two_d),
                     g_mine[i].reshape(two_d), g_theirs[i].reshape(two_d), "adamw_" + n)
        out[n] = [r.reshape(shp) for r in res]

    gsmall = [jnp.stack([grads[l][n] for l in range(depth)]) for n in _SMALL]
    small_shapes = [g.shape for g in gsmall]
    summed = _sum_slots(_allgather_all(_pack(gsmall, F32, LANES, 8), "gather_small_grads"), "sum_small_grads")
    gs = dict(zip(_SMALL, _unpack(summed, small_shapes)))
    for n in _CONV:
        wcols = a[n].shape[2]
        gs[n] = lax.dynamic_slice_in_dim(gs[n], chip * wcols, wcols, axis=2)
    local_shapes = [a[n].shape for n in _SMALL]
    res = _adamw(_pack([a[n] for n in _SMALL], F32, LANES, 8), _pack([a["m_" + n] for n in _SMALL], F32, LANES, 8),
                 _pack([a["v_" + n] for n in _SMALL], F32, LANES, 8), _pack([gs[n] for n in _SMALL], F32, LANES, 8),
                 None, "adamw_small")
    res = [_unpack(r, local_shapes) for r in res]
    for i, n in enumerate(_SMALL):
        out[n] = [r[i] for r in res]

    loss = lax.psum(loss_part, ("x", "y", "c"))
    return (loss, grad_x) + tuple(out[n][k] for k in range(4) for n in _WEIGHTS)


def kernel(x, c, ada_w, ada_b, mix_pre_g, mix_post_g, w_in, ssd_conv_w, ssd_conv_b, ssd_dt_bias, ssd_a_log, ssd_d, ssd_norm_g, w_ssd_out, sc_conv_w, w_sc_out, w_o, ffn_pre_g, ffn_post_g, w_up, ffn_conv_w, ffn_conv_b, w_down, loss_target, m_ada_w, m_ada_b, m_mix_pre_g, m_mix_post_g, m_w_in, m_ssd_conv_w, m_ssd_conv_b, m_ssd_dt_bias, m_ssd_a_log, m_ssd_d, m_ssd_norm_g, m_w_ssd_out, m_sc_conv_w, m_w_sc_out, m_w_o, m_ffn_pre_g, m_ffn_post_g, m_w_up, m_ffn_conv_w, m_ffn_conv_b, m_w_down, v_ada_w, v_ada_b, v_mix_pre_g, v_mix_post_g, v_w_in, v_ssd_conv_w, v_ssd_conv_b, v_ssd_dt_bias, v_ssd_a_log, v_ssd_d, v_ssd_norm_g, v_w_ssd_out, v_sc_conv_w, v_w_sc_out, v_w_o, v_ffn_pre_g, v_ffn_post_g, v_w_up, v_ffn_conv_w, v_ffn_conv_b, v_w_down):
    return _step(dict(zip(_INPUTS, (
        x, c, ada_w, ada_b, mix_pre_g, mix_post_g, w_in, ssd_conv_w, ssd_conv_b, ssd_dt_bias, ssd_a_log, ssd_d, ssd_norm_g, w_ssd_out, sc_conv_w, w_sc_out, w_o, ffn_pre_g, ffn_post_g, w_up, ffn_conv_w, ffn_conv_b, w_down, loss_target, m_ada_w, m_ada_b, m_mix_pre_g, m_mix_post_g, m_w_in, m_ssd_conv_w, m_ssd_conv_b, m_ssd_dt_bias, m_ssd_a_log, m_ssd_d, m_ssd_norm_g, m_w_ssd_out, m_sc_conv_w, m_w_sc_out, m_w_o, m_ffn_pre_g, m_ffn_post_g, m_w_up, m_ffn_conv_w, m_ffn_conv_b, m_w_down, v_ada_w, v_ada_b, v_mix_pre_g, v_mix_post_g, v_w_in, v_ssd_conv_w, v_ssd_conv_b, v_ssd_dt_bias, v_ssd_a_log, v_ssd_d, v_ssd_norm_g, v_w_ssd_out, v_sc_conv_w, v_w_sc_out, v_w_o, v_ffn_pre_g, v_ffn_post_g, v_w_up, v_ffn_conv_w, v_ffn_conv_b, v_w_down))))
```

```python
import functools

import jax
import jax.numpy as jnp
from jax import lax
from jax.experimental import pallas as pl
from jax.experimental.pallas import tpu as pltpu

F32 = jnp.float32
MXU_DTYPE = jnp.bfloat16
WIRE_DTYPE = jnp.bfloat16
HI = lax.Precision.HIGHEST
EPS = 1e-6
N_GROUPS = 4
CHUNK = 128
HEAD_DIM = 64
LANES = 128
HALO = 8
N_CHIPS = 4
N_DEV = 8
VMEM_LIMIT = 56 * 1024 * 1024
ADAM_LR, ADAM_B1, ADAM_B2, ADAM_EPS, ADAM_WD, ADAM_STEP = 0.001, 0.9, 0.999, 1e-08, 0.01, 10
MESH = pl.DeviceIdType.MESH

NN = (((1,), (0,)), ((), ()))
NT = (((1,), (1,)), ((), ()))
TN = (((0,), (0,)), ((), ()))


def _dg(a, b, dn, precision=None):
    return lax.dot_general(a, b, dn, precision=precision, preferred_element_type=F32)


def _tile(dim, pref, mult=LANES):
    t = (min(pref, dim) // mult) * mult
    while t >= mult:
        if dim % t == 0:
            return t
        t -= mult
    return dim


def _cp(sem):
    return pltpu.CompilerParams(dimension_semantics=sem, vmem_limit_bytes=VMEM_LIMIT)


def _sigmoid(x):
    return 1.0 / (1.0 + jnp.exp(-x))


def _softplus(x):
    return jnp.maximum(x, 0.0) + jnp.log1p(jnp.exp(-jnp.abs(x)))


def _rows8(v):
    v = v.reshape(1, -1).astype(F32)
    return jnp.pad(v, ((0, 7), (0, 0)))


def _matmul(a, b, mode, out_dtype, name, tm=1024, tn=1024, tk=1024):
    if mode == "NN":
        (M, K), N = a.shape, b.shape[1]
    elif mode == "NT":
        (M, K), N = a.shape, b.shape[0]
    else:
        (K, M), N = a.shape, b.shape[1]
    tm, tn, tk = _tile(M, tm), _tile(N, tn), _tile(K, tk)
    nk = K // tk
    dn = {"NN": NN, "NT": NT, "TN": TN}[mode]

    def body_one(a_ref, b_ref, o_ref):
        o_ref[...] = _dg(a_ref[...], b_ref[...], dn).astype(o_ref.dtype)

    def body_acc(a_ref, b_ref, o_ref, acc_ref):
        k = pl.program_id(2)

        @pl.when(k == 0)
        def _():
            acc_ref[...] = jnp.zeros_like(acc_ref)

        acc_ref[...] += _dg(a_ref[...], b_ref[...], dn)

        @pl.when(k == nk - 1)
        def _():
            o_ref[...] = acc_ref[...].astype(o_ref.dtype)

    a_spec = (pl.BlockSpec((tk, tm), lambda i, j, k: (k, i)) if mode == "TN"
              else pl.BlockSpec((tm, tk), lambda i, j, k: (i, k)))
    b_spec = (pl.BlockSpec((tn, tk), lambda i, j, k: (j, k)) if mode == "NT"
              else pl.BlockSpec((tk, tn), lambda i, j, k: (k, j)))
    return pl.pallas_call(
        body_one if nk == 1 else body_acc, name=name, grid=(M // tm, N // tn, nk),
        in_specs=[a_spec, b_spec],
        out_specs=pl.BlockSpec((tm, tn), lambda i, j, k: (i, j)),
        out_shape=jax.ShapeDtypeStruct((M, N), out_dtype),
        scratch_shapes=[] if nk == 1 else [pltpu.VMEM((tm, tn), F32)],
        compiler_params=_cp(("parallel", "parallel", "arbitrary")),
    )(a, b)


def _ada_fwd(c16, ada_w, ada_b, name):
    rows, D = c16.shape
    N6 = ada_w.shape[1]
    tn = _tile(N6, 1536)

    def body(c_ref, w_ref, b_ref, act_ref, mod_ref):
        c = c_ref[...]
        act = (c * _sigmoid(c)).astype(act_ref.dtype)
        act_ref[...] = act
        mod_ref[...] = _dg(act, w_ref[...], NN) + b_ref[...]

    return pl.pallas_call(
        body, name=name, grid=(N6 // tn,),
        in_specs=[pl.BlockSpec((rows, D), lambda j: (0, 0)),
                  pl.BlockSpec((D, tn), lambda j: (0, j)),
                  pl.BlockSpec((1, tn), lambda j: (0, j))],
        out_specs=[pl.BlockSpec((rows, D), lambda j: (0, 0)),
                   pl.BlockSpec((rows, tn), lambda j: (0, j))],
        out_shape=[jax.ShapeDtypeStruct((rows, D), MXU_DTYPE),
                   jax.ShapeDtypeStruct((rows, N6), F32)],
        compiler_params=_cp(("arbitrary",)),
    )(c16, ada_w, ada_b.reshape(1, N6))


def _norm_mod_rows(x, g, sc, sh):
    r = lax.rsqrt(jnp.mean(x * x, axis=-1, keepdims=True) + EPS)
    return ((x * r) * g) * (1.0 + sc) + sh


def _norm_mod_fwd(x, g, mod, row_sh, name, ts=256):
    Bl, S, D = x.shape
    ts = _tile(S, ts, 8)

    def body(x_ref, g_ref, mod_ref, h_ref):
        sh = mod_ref[row_sh:row_sh + 1, :]
        sc = mod_ref[row_sh + 1:row_sh + 2, :]
        h_ref[...] = _norm_mod_rows(x_ref[...], g_ref[...], sc, sh).astype(h_ref.dtype)

    tok = pl.BlockSpec((None, ts, D), lambda b, s: (b, s, 0))
    return pl.pallas_call(
        body, name=name, grid=(Bl, S // ts),
        in_specs=[tok, pl.BlockSpec((1, D), lambda b, s: (0, 0)),
                  pl.BlockSpec((None, 8, D), lambda b, s: (b, 0, 0))],
        out_specs=tok, out_shape=jax.ShapeDtypeStruct((Bl, S, D), MXU_DTYPE),
        compiler_params=_cp(("parallel", "parallel")),
    )(x, g.reshape(1, D), mod)


def _post_norm_fwd(xp, f, post_g, mod, row_gt, nxt, name, ts=256):
    Bl, S, D = xp.shape
    ts = _tile(S, ts, 8)
    has_next = nxt is not None

    def body(*refs):
        if has_next:
            xp_ref, f_ref, pg_ref, mod_ref, ng_ref, nmod_ref, x_ref, h_ref = refs
        else:
            xp_ref, f_ref, pg_ref, mod_ref, x_ref = refs
        f = f_ref[...]
        r = lax.rsqrt(jnp.mean(f * f, axis=-1, keepdims=True) + EPS)
        x = xp_ref[...] + mod_ref[row_gt:row_gt + 1, :] * ((f * r) * pg_ref[...])
        x_ref[...] = x
        if has_next:
            rs = nxt[2]
            h_ref[...] = _norm_mod_rows(x, ng_ref[...], nmod_ref[rs + 1:rs + 2, :], nmod_ref[rs:rs + 1, :]).astype(h_ref.dtype)

    tok = pl.BlockSpec((None, ts, D), lambda b, s: (b, s, 0))
    vec = pl.BlockSpec((1, D), lambda b, s: (0, 0))
    modspec = pl.BlockSpec((None, 8, D), lambda b, s: (b, 0, 0))
    ins = [xp, f, post_g.reshape(1, D), mod]
    in_specs = [tok, tok, vec, modspec]
    out_specs = [tok]
    out_shape = [jax.ShapeDtypeStruct((Bl, S, D), F32)]
    if has_next:
        ins += [nxt[0].reshape(1, D), nxt[1]]
        in_specs += [vec, modspec]
        out_specs += [tok]
        out_shape += [jax.ShapeDtypeStruct((Bl, S, D), MXU_DTYPE)]
    out = pl.pallas_call(
        body, name=name, grid=(Bl, S // ts), in_specs=in_specs, out_specs=out_specs, out_shape=out_shape,
        compiler_params=_cp(("parallel", "parallel")),
    )(*ins)
    return (out[0], out[1]) if has_next else (out[0], None)


def _loss_fwd_bwd(y, target, name, ts=256):
    Bl, S, D = y.shape
    ts = _tile(S, ts, 8)

    def body(y_ref, t_ref, dy_ref, l_ref):
        @pl.when((pl.program_id(0) == 0) & (pl.program_id(1) == 0))
        def _():
            l_ref[...] = jnp.zeros_like(l_ref)

        e = y_ref[...] - t_ref[...]
        dy_ref[...] = e * (1.0 / D)
        l_ref[...] += 0.5 * jnp.sum(jnp.mean(e * e, axis=-1, keepdims=True), axis=0, keepdims=True)

    tok = pl.BlockSpec((None, ts, D), lambda b, s: (b, s, 0))
    dy, l = pl.pallas_call(
        body, name=name, grid=(Bl, S // ts), in_specs=[tok, tok],
        out_specs=[tok, pl.BlockSpec((8, LANES), lambda b, s: (0, 0))],
        out_shape=[jax.ShapeDtypeStruct((Bl, S, D), F32), jax.ShapeDtypeStruct((8, LANES), F32)],
        compiler_params=_cp(("arbitrary", "arbitrary")),
    )(y, target)
    return dy, l[0, 0]


def _norm_bwd(dx_res, nxt, prv, name, ts=256):
    Bl, S, D = dx_res.shape
    ts = _tile(S, ts, 8)
    has_next, has_prev = nxt is not None, prv is not None

    def body(*refs):
        refs = list(refs)
        dxr_ref = refs.pop(0)
        if has_next:
            dh_ref, x_ref, g_ref, nmod_ref = refs[:4]
            refs = refs[4:]
        if has_prev:
            f_ref, pg_ref, pmod_ref = refs[:3]
            refs = refs[3:]
        dx_ref = refs.pop(0)
        if has_prev:
            df_ref = refs.pop(0)
        pb_ref, sh_ref = refs
        b, s = pl.program_id(0), pl.program_id(1)

        @pl.when(s == 0)
        def _():
            pb_ref[...] = jnp.zeros_like(pb_ref)

        @pl.when((b == 0) & (s == 0))
        def _():
            sh_ref[...] = jnp.zeros_like(sh_ref)

        dx = dxr_ref[...]
        if has_next:
            rs = nxt[4]
            x, dh, g = x_ref[...], dh_ref[...], g_ref[...]
            sc1 = 1.0 + nmod_ref[rs + 1:rs + 2, :]
            r = lax.rsqrt(jnp.mean(x * x, axis=-1, keepdims=True) + EPS)
            xn = x * r
            pb_ref[0:1, :] += jnp.sum(dh, axis=0, keepdims=True)
            pb_ref[1:2, :] += jnp.sum(dh * (xn * g), axis=0, keepdims=True)
            sh_ref[0:1, :] += jnp.sum(dh * sc1 * xn, axis=0, keepdims=True)
            dxn = dh * sc1 * g
            dx = dx + r * (dxn - xn * jnp.mean(dxn * xn, axis=-1, keepdims=True))
        dx_ref[...] = dx
        if has_prev:
            rg = prv[3]
            f, pg = f_ref[...], pg_ref[...]
            gt = pmod_ref[rg:rg + 1, :]
            r = lax.rsqrt(jnp.mean(f * f, axis=-1, keepdims=True) + EPS)
            fn = f * r
            pb_ref[2:3, :] += jnp.sum(dx * (fn * pg), axis=0, keepdims=True)
            drn = dx * gt
            sh_ref[1:2, :] += jnp.sum(drn * fn, axis=0, keepdims=True)
            dfn = drn * pg
            df_ref[...] = (r * (dfn - fn * jnp.mean(dfn * fn, axis=-1, keepdims=True))).astype(df_ref.dtype)

    tok = pl.BlockSpec((None, ts, D), lambda b, s: (b, s, 0))
    vec = pl.BlockSpec((1, D), lambda b, s: (0, 0))
    modspec = pl.BlockSpec((None, 8, D), lambda b, s: (b, 0, 0))
    ins, in_specs = [dx_res], [tok]
    if has_next:
        ins += [nxt[0], nxt[1], nxt[2].reshape(1, D), nxt[3]]
        in_specs += [tok, tok, vec, modspec]
    if has_prev:
        ins += [prv[0], prv[1].reshape(1, D), prv[2]]
        in_specs += [tok, vec, modspec]
    out_specs, out_shape = [tok], [jax.ShapeDtypeStruct((Bl, S, D), F32)]
    if has_prev:
        out_specs += [tok]
        out_shape += [jax.ShapeDtypeStruct((Bl, S, D), MXU_DTYPE)]
    out_specs += [modspec, pl.BlockSpec((8, D), lambda b, s: (0, 0))]
    out_shape += [jax.ShapeDtypeStruct((Bl, 8, D), F32), jax.ShapeDtypeStruct((8, D), F32)]
    out = pl.pallas_call(
        body, name=name, grid=(Bl, S // ts), in_specs=in_specs, out_specs=out_specs, out_shape=out_shape,
        compiler_params=_cp(("arbitrary", "arbitrary")),
    )(*ins)
    if has_prev:
        return out[0], out[1], out[2], out[3]
    return out[0], None, out[1], out[2]


def _shift_down(x, j):
    return x if j == 0 else pltpu.roll(x, j, axis=0)


def _shift_up(x, j):
    return x if j == 0 else pltpu.roll(x, x.shape[0] - j, axis=0)


def _conv(xall, w_ref, K):
    y = w_ref[K - 1:K, :] * xall
    for k in range(K - 1):
        y = y + w_ref[k:k + 1, :] * _shift_down(xall, K - 1 - k)
    return y


def _conv_t(dall, w_ref, K):
    y = w_ref[K - 1:K, :] * dall
    for k in range(K - 1):
        y = y + w_ref[k:k + 1, :] * _shift_up(dall, K - 1 - k)
    return y


def _conv_wgrad(acc_ref, row0, dtile, xall, K, ts):
    for k in range(K):
        xs = _shift_down(xall, K - 1 - k)[HALO:HALO + ts]
        acc_ref[row0 + k:row0 + k + 1, :] += jnp.sum(dtile * xs, axis=0, keepdims=True)


def _halo_specs(ts, W, nS, colblk):
    per = ts // HALO
    tile = pl.BlockSpec((None, ts, W), lambda b, c, s: (b, s, colblk(c)))
    prev = pl.BlockSpec((None, HALO, W), lambda b, c, s: (b, jnp.maximum(s * per - 1, 0), colblk(c)))
    nxt = pl.BlockSpec((None, HALO, W), lambda b, c, s: (b, jnp.minimum((s + 1) * per, nS * per - 1), colblk(c)))
    return tile, prev, nxt


def _masked(ref, keep):
    v = ref[...]
    return jnp.where(keep, v, jnp.zeros_like(v))


def _ssd_conv_fwd(proj, w8, b, off, CD, name, ts=256, W=512):
    Bl, S, _ = proj.shape
    K = 4
    ts, W = _tile(S, ts, 8), _tile(CD, W)
    assert off % W == 0
    nS, nW, ob = S // ts, CD // W, off // W

    def body(x_ref, xp_ref, w_ref, b_ref, o_ref):
        s = pl.program_id(2)
        xall = jnp.concatenate([_masked(xp_ref, s > 0), x_ref[...]], axis=0)
        xc = _conv(xall, w_ref, K)[HALO:] + b_ref[...]
        o_ref[...] = xc * _sigmoid(xc)

    tile, prev, _ = _halo_specs(ts, W, nS, lambda c: ob + c)
    return pl.pallas_call(
        body, name=name, grid=(Bl, nW, nS),
        in_specs=[tile, prev, pl.BlockSpec((8, W), lambda b_, c, s: (0, c)), pl.BlockSpec((1, W), lambda b_, c, s: (0, c))],
        out_specs=pl.BlockSpec((None, ts, W), lambda b_, c, s: (b_, s, c)),
        out_shape=jax.ShapeDtypeStruct((Bl, S, CD), F32),
        compiler_params=_cp(("parallel", "parallel", "parallel")),
    )(proj, proj, w8, b.reshape(1, CD))


def _ssd_conv_bwd(dxa, proj, w8, b, off, CD, name, ts=256, W=512):
    Bl, S, _ = proj.shape
    K = 4
    ts, W = _tile(S, ts, 8), _tile(CD, W)
    nS, nW, ob = S // ts, CD // W, off // W

    def body(d_ref, dn_ref, x_ref, xp_ref, xn_ref, w_ref, b_ref, dx_ref, dw_ref, db_ref):
        bb, s = pl.program_id(1), pl.program_id(2)

        @pl.when((bb == 0) & (s == 0))
        def _():
            dw_ref[...] = jnp.zeros_like(dw_ref)
            db_ref[...] = jnp.zeros_like(db_ref)

        last = s == nS - 1
        xall = jnp.concatenate([_masked(xp_ref, s > 0), x_ref[...], _masked(xn_ref, ~last)], axis=0)
        dall = jnp.concatenate([jnp.zeros((HALO, W), F32), d_ref[...], _masked(dn_ref, ~last)], axis=0)
        xc = _conv(xall, w_ref, K) + b_ref[...]
        sg = _sigmoid(xc)
        dxc = dall * (sg * (1.0 + xc * (1.0 - sg)))
        dx_ref[...] = _conv_t(dxc, w_ref, K)[HALO:HALO + ts].astype(dx_ref.dtype)
        dtile = dxc[HALO:HALO + ts]
        db_ref[0:1, :] += jnp.sum(dtile, axis=0, keepdims=True)
        _conv_wgrad(dw_ref, 0, dtile, xall, K, ts)

    per = ts // HALO
    dtile_s = pl.BlockSpec((None, ts, W), lambda c, b_, s: (b_, s, c))
    dnext_s = pl.BlockSpec((None, HALO, W), lambda c, b_, s: (b_, jnp.minimum((s + 1) * per, nS * per - 1), c))
    xtile_s = pl.BlockSpec((None, ts, W), lambda c, b_, s: (b_, s, ob + c))
    xprev_s = pl.BlockSpec((None, HALO, W), lambda c, b_, s: (b_, jnp.maximum(s * per - 1, 0), ob + c))
    xnext_s = pl.BlockSpec((None, HALO, W), lambda c, b_, s: (b_, jnp.minimum((s + 1) * per, nS * per - 1), ob + c))
    wspec = pl.BlockSpec((8, W), lambda c, b_, s: (0, c))
    return pl.pallas_call(
        body, name=name, grid=(nW, Bl, nS),
        in_specs=[dtile_s, dnext_s, xtile_s, xprev_s, xnext_s, wspec, pl.BlockSpec((1, W), lambda c, b_, s: (0, c))],
        out_specs=[dtile_s, wspec, wspec],
        out_shape=[jax.ShapeDtypeStruct((Bl, S, CD), MXU_DTYPE), jax.ShapeDtypeStruct((8, CD), F32),
                   jax.ShapeDtypeStruct((8, CD), F32)],
        compiler_params=_cp(("arbitrary", "arbitrary", "arbitrary")),
    )(dxa, dxa, proj, proj, proj, w8, b.reshape(1, CD))


def _sc_conv_fwd(proj, w8, offs, D, name, ts=256, W=512):
    Bl, S, _ = proj.shape
    K = 3
    ts, W = _tile(S, ts, 8), _tile(D, W)
    nS, nW = S // ts, D // W
    ob, oc, oh = [o // W for o in offs]

    def body(b_ref, c_ref, cp_ref, h_ref, hp_ref, w_ref, o_ref):
        s = pl.program_id(2)
        keep = s > 0
        vall = jnp.concatenate([_masked(cp_ref, keep) * _masked(hp_ref, keep), c_ref[...] * h_ref[...]], axis=0)
        o_ref[...] = (b_ref[...] * _conv(vall, w_ref, K)[HALO:]).astype(o_ref.dtype)

    tb, _, _ = _halo_specs(ts, W, nS, lambda c: ob + c)
    tc, pc, _ = _halo_specs(ts, W, nS, lambda c: oc + c)
    th, ph, _ = _halo_specs(ts, W, nS, lambda c: oh + c)
    return pl.pallas_call(
        body, name=name, grid=(Bl, nW, nS),
        in_specs=[tb, tc, pc, th, ph, pl.BlockSpec((8, W), lambda b_, c, s: (0, c))],
        out_specs=pl.BlockSpec((None, ts, W), lambda b_, c, s: (b_, s, c)),
        out_shape=jax.ShapeDtypeStruct((Bl, S, D), MXU_DTYPE),
        compiler_params=_cp(("parallel", "parallel", "parallel")),
    )(proj, proj, proj, proj, proj, w8)


def _sc_conv_bwd(ds, proj, w8, offs, D, name, ts=256, W=512):
    Bl, S, _ = proj.shape
    K = 3
    ts, W = _tile(S, ts, 8), _tile(D, W)
    nS, nW = S // ts, D // W
    ob, oc, oh = [o // W for o in offs]

    def body(d_ref, dn_ref, b_ref, bn_ref, c_ref, cp_ref, cn_ref, h_ref, hp_ref, hn_ref, w_ref,
             db_ref, dc_ref, dh_ref, dw_ref):
        bb, s = pl.program_id(1), pl.program_id(2)

        @pl.when((bb == 0) & (s == 0))
        def _():
            dw_ref[...] = jnp.zeros_like(dw_ref)

        first, last = s > 0, s < nS - 1
        zeros = jnp.zeros((HALO, W), F32)
        c_t, h_t = c_ref[...], h_ref[...]
        vall = jnp.concatenate([_masked(cp_ref, first) * _masked(hp_ref, first), c_t * h_t,
                                _masked(cn_ref, last) * _masked(hn_ref, last)], axis=0)
        dcv = jnp.concatenate([zeros, d_ref[...] * b_ref[...], _masked(dn_ref, last) * _masked(bn_ref, last)], axis=0)
        cv = _conv(vall, w_ref, K)[HALO:HALO + ts]
        db_ref[...] = (d_ref[...] * cv).astype(db_ref.dtype)
        dv = _conv_t(dcv, w_ref, K)[HALO:HALO + ts]
        dc_ref[...] = (dv * h_t).astype(dc_ref.dtype)
        dh_ref[...] = (dv * c_t).astype(dh_ref.dtype)
        _conv_wgrad(dw_ref, 0, dcv[HALO:HALO + ts], vall, K, ts)

    per = ts // HALO

    def specs(o):
        t = pl.BlockSpec((None, ts, W), lambda c, b_, s: (b_, s, o + c))
        p = pl.BlockSpec((None, HALO, W), lambda c, b_, s: (b_, jnp.maximum(s * per - 1, 0), o + c))
        n = pl.BlockSpec((None, HALO, W), lambda c, b_, s: (b_, jnp.minimum((s + 1) * per, nS * per - 1), o + c))
        return t, p, n

    dt_, _, dn_ = specs(0)
    bt, _, bn = specs(ob)
    ct, cp, cn = specs(oc)
    ht, hp, hn = specs(oh)
    wspec = pl.BlockSpec((8, W), lambda c, b_, s: (0, c))
    act = jax.ShapeDtypeStruct((Bl, S, D), MXU_DTYPE)
    return pl.pallas_call(
        body, name=name, grid=(nW, Bl, nS),
        in_specs=[dt_, dn_, bt, bn, ct, cp, cn, ht, hp, hn, wspec],
        out_specs=[dt_, dt_, dt_, wspec],
        out_shape=[act, act, act, jax.ShapeDtypeStruct((8, D), F32)],
        compiler_params=_cp(("arbitrary", "arbitrary", "arbitrary")),
    )(ds, ds, proj, proj, proj, proj, proj, proj, proj, proj, w8)


def _ffn_conv_fwd(up, w8, b, DFF, name, ts=256, W=512):
    Bl, S, _ = up.shape
    K = 3
    ts, W = _tile(S, ts, 8), _tile(DFF, W)
    nS, nW = S // ts, DFF // W

    def body(g_ref, gp_ref, v_ref, vp_ref, wg_ref, wv_ref, bg_ref, bv_ref, o_ref):
        keep = pl.program_id(2) > 0
        ug = _conv(jnp.concatenate([_masked(gp_ref, keep), g_ref[...]], axis=0), wg_ref, K)[HALO:] + bg_ref[...]
        uv = _conv(jnp.concatenate([_masked(vp_ref, keep), v_ref[...]], axis=0), wv_ref, K)[HALO:] + bv_ref[...]
        o_ref[...] = (ug * _sigmoid(ug) * uv).astype(o_ref.dtype)

    tg, pg, _ = _halo_specs(ts, W, nS, lambda c: c)
    tv, pv, _ = _halo_specs(ts, W, nS, lambda c: nW + c)
    wg = pl.BlockSpec((8, W), lambda b_, c, s: (0, c))
    wv = pl.BlockSpec((8, W), lambda b_, c, s: (0, nW + c))
    bg = pl.BlockSpec((1, W), lambda b_, c, s: (0, c))
    bv = pl.BlockSpec((1, W), lambda b_, c, s: (0, nW + c))
    b2 = b.reshape(1, 2 * DFF)
    return pl.pallas_call(
        body, name=name, grid=(Bl, nW, nS),
        in_specs=[tg, pg, tv, pv, wg, wv, bg, bv],
        out_specs=pl.BlockSpec((None, ts, W), lambda b_, c, s: (b_, s, c)),
        out_shape=jax.ShapeDtypeStruct((Bl, S, DFF), MXU_DTYPE),
        compiler_params=_cp(("parallel", "parallel", "parallel")),
    )(up, up, up, up, w8, w8, b2, b2)


def _ffn_conv_bwd(da, up, w8, b, DFF, name, ts=256, W=256):
    Bl, S, _ = up.shape
    K = 3
    ts, W = _tile(S, ts, 8), _tile(DFF, W)
    nS, nW = S // ts, DFF // W

    def body(d_ref, dn_ref, g_ref, gp_ref, gn_ref, v_ref, vp_ref, vn_ref, wg_ref, wv_ref, bg_ref, bv_ref,
             dg_ref, dv_ref, dwg_ref, dwv_ref, dbg_ref, dbv_ref):
        bb, s = pl.program_id(1), pl.program_id(2)

        @pl.when((bb == 0) & (s == 0))
        def _():
            for r in (dwg_ref, dwv_ref, dbg_ref, dbv_ref):
                r[...] = jnp.zeros_like(r)

        first, last = s > 0, s < nS - 1
        gall = jnp.concatenate([_masked(gp_ref, first), g_ref[...], _masked(gn_ref, last)], axis=0)
        vall = jnp.concatenate([_masked(vp_ref, first), v_ref[...], _masked(vn_ref, last)], axis=0)
        dall = jnp.concatenate([jnp.zeros((HALO, W), F32), d_ref[...], _masked(dn_ref, last)], axis=0)
        ug = _conv(gall, wg_ref, K) + bg_ref[...]
        uv = _conv(vall, wv_ref, K) + bv_ref[...]
        sg = _sigmoid(ug)
        dug = dall * uv * (sg * (1.0 + ug * (1.0 - sg)))
        duv = dall * (ug * sg)
        dg_ref[...] = _conv_t(dug, wg_ref, K)[HALO:HALO + ts].astype(dg_ref.dtype)
        dv_ref[...] = _conv_t(duv, wv_ref, K)[HALO:HALO + ts].astype(dv_ref.dtype)
        dug_t, duv_t = dug[HALO:HALO + ts], duv[HALO:HALO + ts]
        dbg_ref[0:1, :] += jnp.sum(dug_t, axis=0, keepdims=True)
        dbv_ref[0:1, :] += jnp.sum(duv_t, axis=0, keepdims=True)
        _conv_wgrad(dwg_ref, 0, dug_t, gall, K, ts)
        _conv_wgrad(dwv_ref, 0, duv_t, vall, K, ts)

    per = ts // HALO

    def specs(o):
        t = pl.BlockSpec((None, ts, W), lambda c, b_, s: (b_, s, o + c))
        p = pl.BlockSpec((None, HALO, W), lambda c, b_, s: (b_, jnp.maximum(s * per - 1, 0), o + c))
        n = pl.BlockSpec((None, HALO, W), lambda c, b_, s: (b_, jnp.minimum((s + 1) * per, nS * per - 1), o + c))
        return t, p, n

    dt_, _, dn_ = specs(0)
    gt, gp, gn = specs(0)
    vt, vp, vn = specs(nW)
    wg = pl.BlockSpec((8, W), lambda c, b_, s: (0, c))
    wv = pl.BlockSpec((8, W), lambda c, b_, s: (0, nW + c))
    bg = pl.BlockSpec((1, W), lambda c, b_, s: (0, c))
    bv = pl.BlockSpec((1, W), lambda c, b_, s: (0, nW + c))
    b2 = b.reshape(1, 2 * DFF)
    act = jax.ShapeDtypeStruct((Bl, S, DFF), MXU_DTYPE)
    small = jax.ShapeDtypeStruct((8, DFF), F32)
    dg, dv, dwg, dwv, dbg, dbv = pl.pallas_call(
        body, name=name, grid=(nW, Bl, nS),
        in_specs=[dt_, dn_, gt, gp, gn, vt, vp, vn, wg, wv, bg, bv],
        out_specs=[dt_, dt_, wg, wg, wg, wg],
        out_shape=[act, act, small, small, small, small],
        compiler_params=_cp(("arbitrary", "arbitrary", "arbitrary")),
    )(da, da, up, up, up, up, up, up, w8, w8, b2, b2)
    return dg, dv, jnp.concatenate([dwg, dwv], axis=1), jnp.concatenate([dbg, dbv], axis=1)


def _merge_fwd(proj, y_ssd, y_sc, off, D, name, ts=256):
    Bl, S, _ = proj.shape
    ts = _tile(S, ts, 8)
    og = off // D

    def body(g1_ref, g2_ref, a_ref, b_ref, o_ref):
        o_ref[...] = (_sigmoid(g1_ref[...]) * a_ref[...] + _sigmoid(g2_ref[...]) * b_ref[...]).astype(o_ref.dtype)

    tok = pl.BlockSpec((None, ts, D), lambda b, s: (b, s, 0))
    return pl.pallas_call(
        body, name=name, grid=(Bl, S // ts),
        in_specs=[pl.BlockSpec((None, ts, D), lambda b, s: (b, s, og)),
                  pl.BlockSpec((None, ts, D), lambda b, s: (b, s, og + 1)), tok, tok],
        out_specs=tok, out_shape=jax.ShapeDtypeStruct((Bl, S, D), MXU_DTYPE),
        compiler_params=_cp(("parallel", "parallel")),
    )(proj, proj, y_ssd, y_sc)


def _merge_bwd(dmixin, proj, y_ssd, y_sc, off, D, name, ts=256):
    Bl, S, _ = proj.shape
    ts = _tile(S, ts, 8)
    og = off // D

    def body(d_ref, g1_ref, g2_ref, a_ref, b_ref, da_ref, db_ref, dg1_ref, dg2_ref):
        d = d_ref[...]
        s1, s2 = _sigmoid(g1_ref[...]), _sigmoid(g2_ref[...])
        da_ref[...] = (d * s1).astype(da_ref.dtype)
        db_ref[...] = (d * s2).astype(db_ref.dtype)
        dg1_ref[...] = (d * a_ref[...] * (s1 * (1.0 - s1))).astype(dg1_ref.dtype)
        dg2_ref[...] = (d * b_ref[...] * (s2 * (1.0 - s2))).astype(dg2_ref.dtype)

    tok = pl.BlockSpec((None, ts, D), lambda b, s: (b, s, 0))
    act = jax.ShapeDtypeStruct((Bl, S, D), MXU_DTYPE)
    return pl.pallas_call(
        body, name=name, grid=(Bl, S // ts),
        in_specs=[tok, pl.BlockSpec((None, ts, D), lambda b, s: (b, s, og)),
                  pl.BlockSpec((None, ts, D), lambda b, s: (b, s, og + 1)), tok, tok],
        out_specs=[tok, tok, tok, tok], out_shape=[act, act, act, act],
        compiler_params=_cp(("parallel", "parallel")),
    )(dmixin, proj, proj, y_ssd, y_sc)


def _expand_cols(mat, R, lane):
    half = lane < HEAD_DIM
    return jnp.concatenate(
        [jnp.where(half, mat[:, 2 * q:2 * q + 1], mat[:, 2 * q + 1:2 * q + 2]) for q in range(R // 2)], axis=1)


def _head_rows(colvec, R, N):
    return jnp.concatenate([jnp.broadcast_to(colvec[r:r + 1, :], (HEAD_DIM, N)) for r in range(R)], axis=0)


def _ssd_common(dtr_ref, dtb_ref, alog_ref, sel, L):
    row = lax.broadcasted_iota(jnp.int32, (L, L), 0)
    col = lax.broadcasted_iota(jnp.int32, (L, L), 1)
    causal = row >= col
    eye = (row == col).astype(F32)
    dt_all = _softplus(dtr_ref[...] + dtb_ref[...])
    a_all = -jnp.exp(alog_ref[...])
    dtg = _dg(dt_all, sel, NN, HI)
    adtg = _dg(dt_all * a_all, sel, NN, HI)
    acs = _dg(causal.astype(F32), adtg, NN, HI)
    acs_t = _dg(eye, acs, NT, HI)
    return causal, row, col, dt_all, a_all, dtg, acs, acs_t


def _ssd_fwd(xbc_a, proj, dt_bias, a_log, d_skip, norm_g, sel, dm, name):
    Bl, S, DI, N, R, L, G = dm.Bl, dm.S, dm.DI, dm.N, dm.R, CHUNK, N_GROUPS
    RP = R * HEAD_DIM
    nc = S // L
    ob, ocm, odt = DI // N, DI // N + G, dm.ODT // LANES

    def body(xs_ref, bm_ref, cm_ref, z_ref, dtr_ref, dtb_ref, alog_ref, dsk_ref, ng_ref, sel_ref,
             y_ref, yn_ref, hp_ref, h_ref):
        @pl.when(pl.program_id(2) == 0)
        def _():
            h_ref[...] = jnp.zeros_like(h_ref)

        sel_g = sel_ref[...]
        causal, row, col, dt_all, a_all, dtg, acs, acs_t = _ssd_common(dtr_ref, dtb_ref, alog_ref, sel_g, L)
        lane = lax.broadcasted_iota(jnp.int32, (L, LANES), 1)
        lane1 = lax.broadcasted_iota(jnp.int32, (1, LANES), 1)
        xs = xs_ref[...]
        bmb, cmb = bm_ref[...].astype(MXU_DTYPE), cm_ref[...].astype(MXU_DTYPE)
        sg = _dg(cmb, bmb, NT)
        xdt = xs * _expand_cols(dtg, R, lane)
        xb = xdt.astype(MXU_DTYPE)
        parts = []
        for q in range(R // 2):
            x2 = xb[:, LANES * q:LANES * (q + 1)]
            ys = []
            for r in (2 * q, 2 * q + 1):
                dec = jnp.exp(jnp.where(causal, acs[:, r:r + 1] - acs_t[r:r + 1, :], -1e30))
                ys.append(_dg((sg * dec).astype(MXU_DTYPE), x2, NN))
            parts.append(jnp.where(lane < HEAD_DIM, ys[0], ys[1]))
        ydiag = jnp.concatenate(parts, axis=1)
        acs_last = acs[L - 1:L, :]
        h_cur = h_ref[...]
        hb = h_cur.astype(MXU_DTYPE)
        yoff = _dg(cmb, hb, NT) * _expand_cols(jnp.exp(acs), R, lane)
        st = _dg((xdt * _expand_cols(jnp.exp(acs_last - acs), R, lane)).astype(MXU_DTYPE), bmb, TN)
        hp_ref[...] = hb
        h_ref[...] = h_cur * _head_rows(jnp.exp(acs_t[:, L - 1:L]), R, N) + st
        dsk_g = _dg(jnp.broadcast_to(dsk_ref[...], (8, LANES)), sel_g, NN, HI)[0:1, :]
        y = ydiag + yoff + _expand_cols(dsk_g, R, lane1) * xs
        y_ref[...] = y
        z = z_ref[...]
        yg = y * (z * _sigmoid(z))
        rr = lax.rsqrt(jnp.mean(yg * yg, axis=-1, keepdims=True) + EPS)
        yn_ref[...] = (yg * rr * ng_ref[...]).astype(yn_ref.dtype)

    vec = pl.BlockSpec((1, LANES), lambda b, g, c: (0, 0))
    grp = pl.BlockSpec((None, L, RP), lambda b, g, c: (b, c, g))
    return pl.pallas_call(
        body, name=name, grid=(Bl, G, nc),
        in_specs=[grp,
                  pl.BlockSpec((None, L, N), lambda b, g, c: (b, c, ob + g)),
                  pl.BlockSpec((None, L, N), lambda b, g, c: (b, c, ocm + g)),
                  grp,
                  pl.BlockSpec((None, L, LANES), lambda b, g, c: (b, c, odt)),
                  vec, vec, vec,
                  pl.BlockSpec((1, RP), lambda b, g, c: (0, g)),
                  pl.BlockSpec((None, LANES, LANES), lambda b, g, c: (g, 0, 0))],
        out_specs=[grp, grp, pl.BlockSpec((None, None, None, RP, N), lambda b, g, c: (b, g, c, 0, 0))],
        out_shape=[jax.ShapeDtypeStruct((Bl, S, DI), F32), jax.ShapeDtypeStruct((Bl, S, DI), MXU_DTYPE),
                   jax.ShapeDtypeStruct((Bl, G, nc, RP, N), MXU_DTYPE)],
        scratch_shapes=[pltpu.VMEM((RP, N), F32)],
        compiler_params=_cp(("parallel", "parallel", "arbitrary")),
    )(xbc_a, xbc_a, xbc_a, proj, proj, dt_bias, a_log, d_skip, norm_g.reshape(1, DI), sel)


def _ssd_bwd(dyn, y, xbc_a, proj, hprev, dt_bias, a_log, d_skip, norm_g, sel, dm, name):
    Bl, S, DI, N, R, L, G = dm.Bl, dm.S, dm.DI, dm.N, dm.R, CHUNK, N_GROUPS
    RP = R * HEAD_DIM
    nc = S // L
    ob, ocm, odt = DI // N, DI // N + G, dm.ODT // LANES

    def body(dyn_ref, y_ref, z_ref, xs_ref, bm_ref, cm_ref, dtr_ref, hp_ref, dtb_ref, alog_ref, dsk_ref, ng_ref,
             sel_ref, dz_ref, dxs_ref, dbm_ref, dcm_ref, ddtr_ref, dpar_ref, dng_ref, dh_ref):
        b, c = pl.program_id(1), pl.program_id(2)

        @pl.when(c == 0)
        def _():
            dh_ref[...] = jnp.zeros_like(dh_ref)

        @pl.when((b == 0) & (c == 0))
        def _():
            dpar_ref[...] = jnp.zeros_like(dpar_ref)
            dng_ref[...] = jnp.zeros_like(dng_ref)

        sel_g = sel_ref[...]
        causal, row, col, dt_all, a_all, dtg, acs, acs_t = _ssd_common(dtr_ref, dtb_ref, alog_ref, sel_g, L)
        anti = col >= row
        lane = lax.broadcasted_iota(jnp.int32, (L, LANES), 1)
        lane1 = lax.broadcasted_iota(jnp.int32, (1, LANES), 1)
        et = (lax.shift_right_logical(lax.broadcasted_iota(jnp.int32, (RP, LANES), 0), 6)
              == lax.broadcasted_iota(jnp.int32, (RP, LANES), 1)).astype(F32)

        def headsum(v):
            return _dg(v, et, NN, HI)

        def to_all(v):
            return _dg(v, sel_g, NT, HI)

        xs, z, y, dyn = xs_ref[...], z_ref[...], y_ref[...], dyn_ref[...]
        bm, cm = bm_ref[...], cm_ref[...]
        bmb, cmb = bm.astype(MXU_DTYPE), cm.astype(MXU_DTYPE)
        hpb = hp_ref[...]
        ng = ng_ref[...]

        sz = _sigmoid(z)
        siluz = z * sz
        yg = y * siluz
        rr = lax.rsqrt(jnp.mean(yg * yg, axis=-1, keepdims=True) + EPS)
        yhat = yg * rr
        dng_ref[...] += jnp.sum(dyn * yhat, axis=0, keepdims=True)
        dyhat = dyn * ng
        dyg = rr * (dyhat - yhat * jnp.mean(dyhat * yhat, axis=-1, keepdims=True))
        dy = dyg * siluz
        dz_ref[...] = (dyg * y * (sz * (1.0 + z * (1.0 - sz)))).astype(dz_ref.dtype)

        dsk_g = _dg(jnp.broadcast_to(dsk_ref[...], (8, LANES)), sel_g, NN, HI)[0:1, :]
        dxs = dy * _expand_cols(dsk_g, R, lane1)
        dd_lane = headsum(jnp.sum(dy * xs, axis=0, keepdims=True))

        dt_exp = _expand_cols(dtg, R, lane)
        xdt = xs * dt_exp
        xb = xdt.astype(MXU_DTYPE)
        dyb = dy.astype(MXU_DTYPE)
        acs_last = acs[L - 1:L, :]
        e_a = jnp.exp(acs)
        dsd = jnp.exp(acs_last - acs)
        cd = jnp.exp(acs_last)
        cd_rows = _head_rows(jnp.exp(acs_t[:, L - 1:L]), R, N)

        q_ = _dg(cmb, hpb, NT)
        dq = dy * _expand_cols(e_a, R, lane)
        dqb = dq.astype(MXU_DTYPE)
        dcm = _dg(dqb, hpb, NN)
        dh_yoff = _dg(dqb, cmb, TN)
        d_a = headsum(dq * q_)

        dhn = dh_ref[...]
        dcd_lane = jnp.sum(_dg(dhn * hpb.astype(F32), et, TN, HI), axis=0, keepdims=True)
        d_a_last = dcd_lane * cd
        dh_ref[...] = dhn * cd_rows + dh_yoff
        dhnb = dhn.astype(MXU_DTYPE)

        ds_exp = _expand_cols(dsd, R, lane)
        e_ = _dg(bmb, dhnb, NT)
        dxdt = ds_exp * e_
        t2 = headsum(xdt * e_) * dsd
        d_a = d_a - t2
        d_a_last = d_a_last + jnp.sum(t2, axis=0, keepdims=True)
        dbm = _dg((xdt * ds_exp).astype(MXU_DTYPE), dhnb, NN)

        sg = _dg(cmb, bmb, NT)
        sg_t = _dg(bmb, cmb, NT)
        dsg = jnp.zeros((L, L), F32)
        dsg_t = jnp.zeros((L, L), F32)
        parts = []
        for q in range(R // 2):
            x2 = xb[:, LANES * q:LANES * (q + 1)]
            dy2 = dyb[:, LANES * q:LANES * (q + 1)]
            dxs2 = []
            for hh, r in enumerate((2 * q, 2 * q + 1)):
                mine = (lane < HEAD_DIM) if hh == 0 else (lane >= HEAD_DIM)
                dec = jnp.exp(jnp.where(causal, acs[:, r:r + 1] - acs_t[r:r + 1, :], -1e30))
                dec_t = jnp.exp(jnp.where(anti, acs_t[r:r + 1, :] - acs[:, r:r + 1], -1e30))
                dy2m = jnp.where(mine, dy2, jnp.zeros_like(dy2))
                dm_ = _dg(dy2m, x2, NT)
                dm_t = _dg(x2, dy2m, NT)
                m_t = sg_t * dec_t
                da_col = (jnp.sum(dm_ * (sg * dec), axis=1, keepdims=True)
                          - jnp.sum(dm_t * m_t, axis=1, keepdims=True))
                d_a = d_a + jnp.where(lane == r, da_col, 0.0)
                dsg = dsg + dm_ * dec
                dsg_t = dsg_t + dm_t * dec_t
                dxs2.append(_dg(m_t.astype(MXU_DTYPE), dy2, NN))
            parts.append(jnp.where(lane < HEAD_DIM, dxs2[0], dxs2[1]))
        dxdt = dxdt + jnp.concatenate(parts, axis=1)
        dcm = dcm + _dg(dsg.astype(MXU_DTYPE), bmb, NN)
        dbm = dbm + _dg(dsg_t.astype(MXU_DTYPE), cmb, NN)
        dcm_ref[...] = dcm
        dbm_ref[...] = dbm

        dxs_ref[...] = dxs + dxdt * dt_exp
        ddt = headsum(dxdt * xs)
        rowl = lax.broadcasted_iota(jnp.int32, (L, LANES), 0)
        d_a = d_a + jnp.where(rowl == L - 1, d_a_last, 0.0)
        dadt = _dg(anti.astype(F32), d_a, NN, HI)
        a_g = _dg(jnp.broadcast_to(a_all, (8, LANES)), sel_g, NN, HI)[0:1, :]
        ddt = ddt + dadt * a_g
        da_lane = jnp.sum(dadt * dtg, axis=0, keepdims=True)
        ddtr = to_all(ddt) * _sigmoid(dtr_ref[...] + dtb_ref[...])
        ddtr_ref[...] = ddtr
        dpar_ref[0:1, :] += jnp.sum(ddtr, axis=0, keepdims=True)
        small = to_all(jnp.concatenate([da_lane, dd_lane, jnp.zeros((6, LANES), F32)], axis=0))
        dpar_ref[1:2, :] += small[0:1, :] * a_all
        dpar_ref[2:3, :] += small[1:2, :]

    vec = pl.BlockSpec((1, LANES), lambda g, b, c: (0, 0))
    grp = pl.BlockSpec((None, L, RP), lambda g, b, c: (b, nc - 1 - c, g))
    bspec = pl.BlockSpec((None, L, N), lambda g, b, c: (b, nc - 1 - c, ob + g))
    cspec = pl.BlockSpec((None, L, N), lambda g, b, c: (b, nc - 1 - c, ocm + g))
    gn = pl.BlockSpec((None, L, N), lambda g, b, c: (b, nc - 1 - c, g))
    return pl.pallas_call(
        body, name=name, grid=(G, Bl, nc),
        in_specs=[grp, grp, grp, grp, bspec, cspec,
                  pl.BlockSpec((None, L, LANES), lambda g, b, c: (b, nc - 1 - c, odt)),
                  pl.BlockSpec((None, None, None, RP, N), lambda g, b, c: (b, g, nc - 1 - c, 0, 0)),
                  vec, vec, vec,
                  pl.BlockSpec((1, RP), lambda g, b, c: (0, g)),
                  pl.BlockSpec((None, LANES, LANES), lambda g, b, c: (g, 0, 0))],
        out_specs=[grp, grp, gn, gn,
                   pl.BlockSpec((None, None, L, LANES), lambda g, b, c: (b, g, nc - 1 - c, 0)),
                   pl.BlockSpec((None, 8, LANES), lambda g, b, c: (g, 0, 0)),
                   pl.BlockSpec((1, RP), lambda g, b, c: (0, g))],
        out_shape=[jax.ShapeDtypeStruct((Bl, S, DI), MXU_DTYPE), jax.ShapeDtypeStruct((Bl, S, DI), F32),
                   jax.ShapeDtypeStruct((Bl, S, G * N), F32), jax.ShapeDtypeStruct((Bl, S, G * N), F32),
                   jax.ShapeDtypeStruct((Bl, G, S, LANES), F32), jax.ShapeDtypeStruct((G, 8, LANES), F32),
                   jax.ShapeDtypeStruct((1, DI), F32)],
        scratch_shapes=[pltpu.VMEM((RP, N), F32)],
        compiler_params=_cp(("arbitrary", "arbitrary", "arbitrary")),
    )(dyn, y, proj, xbc_a, xbc_a, xbc_a, proj, hprev, dt_bias, a_log, d_skip, norm_g.reshape(1, DI), sel)


def _adamw(w, m, v, ga, gb, name, tr=128):
    rows, cols = w.shape
    tr = _tile(rows, tr, 8)
    c1 = 1.0 - ADAM_B1 ** ADAM_STEP
    c2 = 1.0 - ADAM_B2 ** ADAM_STEP
    two = gb is not None

    def body(*refs):
        if two:
            w_ref, m_ref, v_ref, ga_ref, gb_ref, g_ref, d_ref, nm_ref, nv_ref = refs
            g = ga_ref[...] + gb_ref[...]
        else:
            w_ref, m_ref, v_ref, ga_ref, g_ref, d_ref, nm_ref, nv_ref = refs
            g = ga_ref[...]
        nm = ADAM_B1 * m_ref[...] + (1.0 - ADAM_B1) * g
        nv = ADAM_B2 * v_ref[...] + (1.0 - ADAM_B2) * (g * g)
        g_ref[...] = g
        nm_ref[...] = nm
        nv_ref[...] = nv
        d_ref[...] = -ADAM_LR * ((nm / c1) / (jnp.sqrt(nv / c2) + ADAM_EPS) + ADAM_WD * w_ref[...])

    blk = pl.BlockSpec((tr, cols), lambda i: (i, 0))
    ins = [w, m, v, ga] + ([gb] if two else [])
    shp = jax.ShapeDtypeStruct((rows, cols), F32)
    return pl.pallas_call(
        body, name=name, grid=(rows // tr,), in_specs=[blk] * len(ins), out_specs=[blk] * 4,
        out_shape=[shp] * 4, compiler_params=_cp(("parallel",)),
    )(*ins)


def _sum_slots(buf, name, tr=256):
    n, rows, cols = buf.shape
    tr = _tile(rows, tr, 8)

    def body(b_ref, o_ref):
        acc = b_ref[0].astype(F32)
        for k in range(1, n):
            acc = acc + b_ref[k].astype(F32)
        o_ref[...] = acc

    return pl.pallas_call(
        body, name=name, grid=(rows // tr,),
        in_specs=[pl.BlockSpec((n, tr, cols), lambda i: (0, i, 0))],
        out_specs=pl.BlockSpec((tr, cols), lambda i: (i, 0)),
        out_shape=jax.ShapeDtypeStruct((rows, cols), F32), compiler_params=_cp(("parallel",)),
    )(buf)


_ANY = pl.BlockSpec(memory_space=pl.ANY)


def _exchange_chips(src, per_dest, name):
    rows, cols = src.shape[-2:]

    def body(in_ref, out_ref, send_sems, recv_sems, local_sem):
        x, y, c = lax.axis_index("x"), lax.axis_index("y"), lax.axis_index("c")
        me = 2 * x + y
        chips = [(1 - x, y), (x, 1 - y), (1 - x, 1 - y)]

        def block(j):
            return in_ref.at[j] if per_dest else in_ref

        mine = pltpu.make_async_copy(block(me), out_ref.at[me], local_sem)
        mine.start()
        sends = []
        for k, (px, py) in enumerate(chips):
            cp = pltpu.make_async_remote_copy(
                src_ref=block(2 * px + py), dst_ref=out_ref.at[me], send_sem=send_sems.at[k],
                recv_sem=recv_sems.at[k], device_id=(px, py, c), device_id_type=MESH)
            cp.start()
            sends.append(cp)
        for k, (px, py) in enumerate(chips):
            pltpu.make_async_remote_copy(
                src_ref=block(me), dst_ref=out_ref.at[2 * px + py], send_sem=send_sems.at[k],
                recv_sem=recv_sems.at[k], device_id=(px, py, c), device_id_type=MESH).wait_recv()
        for cp in sends:
            cp.wait_send()
        mine.wait()

    return pl.pallas_call(
        body, name=name, in_specs=[_ANY], out_specs=_ANY,
        out_shape=jax.ShapeDtypeStruct((N_CHIPS, rows, cols), src.dtype),
        scratch_shapes=[pltpu.SemaphoreType.DMA((3,)), pltpu.SemaphoreType.DMA((3,)), pltpu.SemaphoreType.DMA(())],
    )(src)


def _shard_window(ref, kind, j, lead=()):
    if kind == "slots":
        return ref.at[(j,) + lead]
    r, c = ref.shape[-2] // (N_CHIPS if kind == "rows" else 1), ref.shape[-1] // (N_CHIPS if kind == "cols" else 1)
    full = tuple(slice(None) for _ in range(len(ref.shape) - 2 - len(lead)))
    if kind == "rows":
        return ref.at[lead + full + (pl.ds(pl.multiple_of(j * r, 16), r), slice(None))]
    return ref.at[lead + full + (slice(None), pl.ds(pl.multiple_of(j * c, LANES), c))]


def _gather_kind(shard, axis):
    if axis == 1:
        return "rows"
    return "cols" if shard.shape[2] % LANES == 0 else "slots"


def _gather_weights(shards, axes, name):
    kinds = [_gather_kind(s, ax) for s, ax in zip(shards, axes)]
    nw = len(shards)

    def out_shape(s, kind):
        d, r, c = s.shape
        shp = {"rows": (d, N_CHIPS * r, c), "cols": (d, r, N_CHIPS * c), "slots": (N_CHIPS, d, r, c)}[kind]
        return jax.ShapeDtypeStruct(shp, s.dtype)

    def body(*refs):
        ins, outs = refs[:nw], refs[nw:2 * nw]
        send_sems, recv_sems, local_sems = refs[2 * nw:]
        x, y, c = lax.axis_index("x"), lax.axis_index("y"), lax.axis_index("c")
        me = 2 * x + y
        chips = [(1 - x, y), (x, 1 - y), (1 - x, 1 - y)]
        sends = []
        for i in range(nw):
            own = pltpu.make_async_copy(ins[i], _shard_window(outs[i], kinds[i], me), local_sems.at[i])
            own.start()
            sends.append((own, False))
            for k, (px, py) in enumerate(chips):
                cp = pltpu.make_async_remote_copy(
                    src_ref=ins[i], dst_ref=_shard_window(outs[i], kinds[i], me), send_sem=send_sems.at[3 * i + k],
                    recv_sem=recv_sems.at[3 * i + k], device_id=(px, py, c), device_id_type=MESH)
                cp.start()
                sends.append((cp, True))
        for i in range(nw):
            for k, (px, py) in enumerate(chips):
                pltpu.make_async_remote_copy(
                    src_ref=ins[i], dst_ref=_shard_window(outs[i], kinds[i], 2 * px + py), send_sem=send_sems.at[3 * i + k],
                    recv_sem=recv_sems.at[3 * i + k], device_id=(px, py, c), device_id_type=MESH).wait_recv()
        for cp, remote in sends:
            cp.wait_send() if remote else cp.wait()

    outs = pl.pallas_call(
        body, name=name, in_specs=[_ANY] * nw, out_specs=[_ANY] * nw,
        out_shape=[out_shape(s, k) for s, k in zip(shards, kinds)],
        scratch_shapes=[pltpu.SemaphoreType.DMA((3 * nw,)), pltpu.SemaphoreType.DMA((3 * nw,)),
                        pltpu.SemaphoreType.DMA((nw,))],
    )(*shards)
    return [jnp.concatenate([o[j] for j in range(N_CHIPS)], axis=2) if k == "slots" else o for o, k in zip(outs, kinds)]


def _scatter_grads(gfull, kinds, name):
    nw, depth = len(gfull), len(gfull[0])
    flat = [g for per in gfull for g in per]

    def shard_shape(g, kind):
        r, c = g.shape[-2] // (N_CHIPS if kind == "rows" else 1), g.shape[-1] // (N_CHIPS if kind == "cols" else 1)
        return r, c

    def body(*refs):
        ins, outs = refs[:nw * depth], refs[nw * depth:nw * depth + nw]
        send_sems, recv_sems, local_sems = refs[nw * depth + nw:]
        x, y, c = lax.axis_index("x"), lax.axis_index("y"), lax.axis_index("c")
        me = 2 * x + y
        chips = [(1 - x, y), (x, 1 - y), (1 - x, 1 - y)]
        sends = []
        for i in range(nw):
            for l in range(depth):
                src, n = ins[i * depth + l], i * depth + l
                own = pltpu.make_async_copy(_shard_window(src, kinds[i], me), outs[i].at[me, l], local_sems.at[n])
                own.start()
                sends.append((own, False))
                for k, (px, py) in enumerate(chips):
                    cp = pltpu.make_async_remote_copy(
                        src_ref=_shard_window(src, kinds[i], 2 * px + py), dst_ref=outs[i].at[me, l],
                        send_sem=send_sems.at[3 * n + k], recv_sem=recv_sems.at[3 * n + k],
                        device_id=(px, py, c), device_id_type=MESH)
                    cp.start()
                    sends.append((cp, True))
        for i in range(nw):
            for l in range(depth):
                src, n = ins[i * depth + l], i * depth + l
                for k, (px, py) in enumerate(chips):
                    pltpu.make_async_remote_copy(
                        src_ref=_shard_window(src, kinds[i], me), dst_ref=outs[i].at[2 * px + py, l],
                        send_sem=send_sems.at[3 * n + k], recv_sem=recv_sems.at[3 * n + k],
                        device_id=(px, py, c), device_id_type=MESH).wait_recv()
        for cp, remote in sends:
            cp.wait_send() if remote else cp.wait()

    n_in = nw * depth
    return pl.pallas_call(
        body, name=name, in_specs=[_ANY] * n_in, out_specs=[_ANY] * nw,
        out_shape=[jax.ShapeDtypeStruct((N_CHIPS, depth) + shard_shape(per[0], k), per[0].dtype)
                   for per, k in zip(gfull, kinds)],
        scratch_shapes=[pltpu.SemaphoreType.DMA((3 * n_in,)), pltpu.SemaphoreType.DMA((3 * n_in,)),
                        pltpu.SemaphoreType.DMA((n_in,))],
    )(*flat)


def _sibling_swap(vs, name):
    nv = len(vs)

    def body(*refs):
        ins, outs, send_sems, recv_sems = refs[:nv], refs[nv:2 * nv], refs[2 * nv], refs[2 * nv + 1]
        x, y, c = lax.axis_index("x"), lax.axis_index("y"), lax.axis_index("c")
        cps = [pltpu.make_async_remote_copy(src_ref=ins[i], dst_ref=outs[i], send_sem=send_sems.at[i],
                                            recv_sem=recv_sems.at[i], device_id=(x, y, 1 - c), device_id_type=MESH)
               for i in range(nv)]
        for cp in cps:
            cp.start()
        for cp in cps:
            cp.wait()

    return pl.pallas_call(
        body, name=name, in_specs=[_ANY] * nv, out_specs=[_ANY] * nv,
        out_shape=[jax.ShapeDtypeStruct(v.shape, v.dtype) for v in vs],
        scratch_shapes=[pltpu.SemaphoreType.DMA((nv,)), pltpu.SemaphoreType.DMA((nv,))],
    )(*vs)


def _allgather_all(v, name):
    rows, cols = v.shape

    def body(in_ref, out_ref, send_sems, recv_sems, local_sem):
        x, y, c = lax.axis_index("x"), lax.axis_index("y"), lax.axis_index("c")
        me = 4 * x + 2 * y + c
        peers = []
        for k in range(1, N_DEV):
            peers.append(((1 - x) if k & 4 else x, (1 - y) if k & 2 else y, (1 - c) if k & 1 else c))
        mine = pltpu.make_async_copy(in_ref, out_ref.at[me], local_sem)
        mine.start()
        sends = []
        for k, peer in enumerate(peers):
            cp = pltpu.make_async_remote_copy(src_ref=in_ref, dst_ref=out_ref.at[me], send_sem=send_sems.at[k],
                                              recv_sem=recv_sems.at[k], device_id=peer, device_id_type=MESH)
            cp.start()
            sends.append(cp)
        for k, (px, py, pc) in enumerate(peers):
            pltpu.make_async_remote_copy(src_ref=in_ref, dst_ref=out_ref.at[4 * px + 2 * py + pc],
                                         send_sem=send_sems.at[k], recv_sem=recv_sems.at[k],
                                         device_id=(px, py, pc), device_id_type=MESH).wait_recv()
        for cp in sends:
            cp.wait_send()
        mine.wait()

    return pl.pallas_call(
        body, name=name, in_specs=[_ANY], out_specs=_ANY,
        out_shape=jax.ShapeDtypeStruct((N_DEV, rows, cols), v.dtype),
        scratch_shapes=[pltpu.SemaphoreType.DMA((N_DEV - 1,)), pltpu.SemaphoreType.DMA((N_DEV - 1,)),
                        pltpu.SemaphoreType.DMA(())],
    )(v)


def _pack(arrs, dtype, width, row_mult):
    flat = jnp.concatenate([a.reshape(-1).astype(dtype) for a in arrs])
    unit = width * row_mult
    total = -(-flat.shape[0] // unit) * unit
    return jnp.pad(flat, (0, total - flat.shape[0])).reshape(-1, width)


def _unpack(buf, shapes):
    flat = buf.reshape(-1)
    out, off = [], 0
    for shp in shapes:
        n = 1
        for d in shp:
            n *= d
        out.append(flat[off:off + n].reshape(shp))
        off += n
    return out


class _Dims:
    pass


def _dims(x, ssd_dt_bias, ssd_norm_g, ssd_conv_b, ffn_conv_b):
    dm = _Dims()
    dm.Bl, dm.S, dm.D = x.shape
    dm.H, dm.DI, dm.CD = ssd_dt_bias.shape[-1], ssd_norm_g.shape[-1], ssd_conv_b.shape[-1]
    dm.N = (dm.CD - dm.DI) // (2 * N_GROUPS)
    dm.R = dm.H // N_GROUPS
    dm.DFF = ffn_conv_b.shape[-1] // 2
    dm.OX = dm.DI
    dm.OB = dm.DI + dm.CD
    dm.OC = dm.OB + dm.D
    dm.OH = dm.OC + dm.D
    dm.OG = dm.OH + dm.D
    dm.ODT = dm.OG + 2 * dm.D
    dm.NP = dm.ODT + LANES
    assert dm.DI // dm.H == HEAD_DIM and dm.N == LANES and dm.R % 2 == 0 and dm.S % CHUNK == 0 and dm.H <= LANES
    return dm


def _permute_w_in(w, dm):
    o = dm.DI + dm.CD
    pad = jnp.zeros((w.shape[0], LANES - dm.H), w.dtype)
    return jnp.concatenate([w[:, :o], w[:, o + dm.H:], w[:, o:o + dm.H], pad], axis=1)


def _unpermute_w_in(dw, dm):
    o = dm.DI + dm.CD
    return jnp.concatenate([dw[:, :o], dw[:, dm.ODT:dm.ODT + dm.H], dw[:, o:dm.ODT]], axis=1)


def _lane_pad(v):
    return jnp.pad(v.reshape(1, -1).astype(F32), ((0, 0), (0, LANES - v.shape[-1])))


def _pad8(w):
    return jnp.pad(w.astype(F32), ((0, 8 - w.shape[0]), (0, 0)))


def _head_select(dm):
    j = jnp.arange(LANES)[None, :, None]
    r = jnp.arange(LANES)[None, None, :]
    g = jnp.arange(N_GROUPS)[:, None, None]
    return ((j == dm.R * g + r) & (r < dm.R)).astype(F32)


def _mix_fwd(dm, h, w, sp, sel, tag):
    Bl, S, D = dm.Bl, dm.S, dm.D
    T = Bl * S
    proj = _matmul(h.reshape(T, D), w["w_in_p"], "NN", F32, tag + "_in_proj", tn=1152).reshape(Bl, S, dm.NP)
    xbc_a = _ssd_conv_fwd(proj, sp["ssd_conv_w8"], sp["ssd_conv_b"], dm.OX, dm.CD, tag + "_ssd_conv")
    y, yn, hprev = _ssd_fwd(xbc_a, proj, sp["dt_bias"], sp["a_log"], sp["d_skip"], sp["ssd_norm_g"], sel, dm,
                            tag + "_ssd_scan")
    y_ssd = _matmul(yn.reshape(T, dm.DI), w["w_ssd_out"], "NN", F32, tag + "_ssd_out", tk=2048).reshape(Bl, S, D)
    s = _sc_conv_fwd(proj, sp["sc_conv_w8"], (dm.OB, dm.OC, dm.OH), D, tag + "_sc_conv")
    y_sc = _matmul(s.reshape(T, D), w["w_sc_out"], "NN", F32, tag + "_sc_out", tk=1024).reshape(Bl, S, D)
    mixin = _merge_fwd(proj, y_ssd, y_sc, dm.OG, D, tag + "_merge")
    mix = _matmul(mixin.reshape(T, D), w["w_o"], "NN", F32, tag + "_o", tk=1024).reshape(Bl, S, D)
    return mix, (h, proj, xbc_a, y, yn, hprev, y_ssd, y_sc, s, mixin)


def _mix_bwd(dm, dmix, saved, w, sp, sel, tag):
    Bl, S, D = dm.Bl, dm.S, dm.D
    T = Bl * S
    h, proj, xbc_a, y, yn, hprev, y_ssd, y_sc, s, mixin = saved
    dmix2 = dmix.reshape(T, D)
    g = {}
    g["w_o"] = _matmul(mixin.reshape(T, D), dmix2, "TN", WIRE_DTYPE,tag + "_dw_o", tk=1024)
    dmixin = _matmul(dmix2, w["w_o"], "NT", F32, tag + "_d_o", tk=1024).reshape(Bl, S, D)
    dy_ssd, dy_sc, dg1, dg2 = _merge_bwd(dmixin, proj, y_ssd, y_sc, dm.OG, D, tag + "_merge_bwd")
    g["w_sc_out"] = _matmul(s.reshape(T, D), dy_sc.reshape(T, D), "TN", WIRE_DTYPE,tag + "_dw_sc_out", tk=1024)
    ds = _matmul(dy_sc.reshape(T, D), w["w_sc_out"], "NT", F32, tag + "_d_sc_out", tk=1024).reshape(Bl, S, D)
    dscb, dscc, dsch, dscw = _sc_conv_bwd(ds, proj, sp["sc_conv_w8"], (dm.OB, dm.OC, dm.OH), D, tag + "_sc_conv_bwd")
    g["sc_conv_w"] = dscw[:3]
    g["w_ssd_out"] = _matmul(yn.reshape(T, dm.DI), dy_ssd.reshape(T, D), "TN", WIRE_DTYPE,tag + "_dw_ssd_out", tk=1024)
    dyn = _matmul(dy_ssd.reshape(T, D), w["w_ssd_out"], "NT", F32, tag + "_d_ssd_out", tk=1024).reshape(Bl, S, dm.DI)
    dz, dxs, dbm, dcm, ddtr_g, dpar, dng = _ssd_bwd(dyn, y, xbc_a, proj, hprev, sp["dt_bias"], sp["a_log"],
                                                    sp["d_skip"], sp["ssd_norm_g"], sel, dm, tag + "_ssd_scan_bwd")
    dpar = jnp.sum(dpar, axis=0)
    g["ssd_dt_bias"], g["ssd_a_log"], g["ssd_d"] = dpar[0, :dm.H], dpar[1, :dm.H], dpar[2, :dm.H]
    g["ssd_norm_g"] = dng[0]
    dxa = jnp.concatenate([dxs, dbm, dcm], axis=-1)
    dxbc, dcw, dcb = _ssd_conv_bwd(dxa, proj, sp["ssd_conv_w8"], sp["ssd_conv_b"], dm.OX, dm.CD, tag + "_ssd_conv_bwd")
    g["ssd_conv_w"], g["ssd_conv_b"] = dcw[:4], dcb[0]
    ddt = jnp.sum(ddtr_g, axis=1).astype(MXU_DTYPE)
    dproj = jnp.concatenate([dz, dxbc, dscb, dscc, dsch, dg1, dg2, ddt], axis=-1).reshape(T, dm.NP)
    g["w_in_p"] = _matmul(h.reshape(T, D), dproj, "TN", WIRE_DTYPE,tag + "_dw_in", tn=1152)
    dh = _matmul(dproj, w["w_in_p"], "NT", F32, tag + "_d_in", tk=1152).reshape(Bl, S, D)
    return dh, g


def _ffn_fwd(dm, h, w, sp, tag):
    Bl, S, D = dm.Bl, dm.S, dm.D
    T = Bl * S
    up = _matmul(h.reshape(T, D), w["w_up"], "NN", F32, tag + "_up", tn=1408).reshape(Bl, S, 2 * dm.DFF)
    a = _ffn_conv_fwd(up, sp["ffn_conv_w8"], sp["ffn_conv_b"], dm.DFF, tag + "_ffn_conv")
    f = _matmul(a.reshape(T, dm.DFF), w["w_down"], "NN", F32, tag + "_down", tk=2816).reshape(Bl, S, D)
    return f, (h, up, a)


def _ffn_bwd(dm, df, saved, w, sp, tag):
    Bl, S, D = dm.Bl, dm.S, dm.D
    T = Bl * S
    h, up, a = saved
    df2 = df.reshape(T, D)
    g = {}
    g["w_down"] = _matmul(a.reshape(T, dm.DFF), df2, "TN", WIRE_DTYPE,tag + "_dw_down", tm=1408)
    da = _matmul(df2, w["w_down"], "NT", F32, tag + "_d_down", tn=1408).reshape(Bl, S, dm.DFF)
    dg, dv, dcw, dcb = _ffn_conv_bwd(da, up, sp["ffn_conv_w8"], sp["ffn_conv_b"], dm.DFF, tag + "_ffn_conv_bwd")
    g["ffn_conv_w"], g["ffn_conv_b"] = dcw[:3], dcb[0]
    dup = jnp.concatenate([dg, dv], axis=-1).reshape(T, 2 * dm.DFF)
    g["w_up"] = _matmul(h.reshape(T, D), dup, "TN", WIRE_DTYPE,tag + "_dw_up", tn=1408)
    dh = _matmul(dup, w["w_up"], "NT", F32, tag + "_d_up", tk=1408).reshape(Bl, S, D)
    return dh, g


def _local_step(dm, x, c, target, wfull, small):
    Bl, S, D = dm.Bl, dm.S, dm.D
    depth = len(wfull)
    sel = _head_select(dm)
    c16 = jnp.pad(c.astype(F32), ((0, 16 - Bl), (0, 0)))
    sps, mods, acts = [], [], []
    for l in range(depth):
        sm = small[l]
        sps.append(dict(
            ssd_conv_w8=_pad8(sm["ssd_conv_w"]), ssd_conv_b=sm["ssd_conv_b"], dt_bias=_lane_pad(sm["ssd_dt_bias"]),
            a_log=_lane_pad(sm["ssd_a_log"]), d_skip=_lane_pad(sm["ssd_d"]), ssd_norm_g=sm["ssd_norm_g"],
            sc_conv_w8=_pad8(sm["sc_conv_w"]), ffn_conv_w8=_pad8(sm["ffn_conv_w"]), ffn_conv_b=sm["ffn_conv_b"]))
        act, mod = _ada_fwd(c16, wfull[l]["ada_w"], sm["ada_b"], f"l{l}_ada")
        acts.append(act)
        mods.append(jnp.pad(mod[:Bl].reshape(Bl, 6, D), ((0, 0), (0, 2), (0, 0))))

    def sub(i):
        l, ffn = i // 2, i % 2
        sm = small[l]
        return dict(l=l, ffn=ffn, pre_g=sm["ffn_pre_g" if ffn else "mix_pre_g"],
                    post_g=sm["ffn_post_g" if ffn else "mix_post_g"], mod=mods[l], row=3 * ffn,
                    tag=f"l{l}_{'ffn' if ffn else 'mix'}")

    nsub = 2 * depth
    subs = [sub(i) for i in range(nsub)]
    xs, fs, saves = [x], [], []
    h = _norm_mod_fwd(x, subs[0]["pre_g"], subs[0]["mod"], subs[0]["row"], "l0_mix_pre_norm")
    for i, sb in enumerate(subs):
        l = sb["l"]
        if sb["ffn"]:
            f, sv = _ffn_fwd(dm, h, wfull[l], sps[l], sb["tag"])
        else:
            f, sv = _mix_fwd(dm, h, wfull[l], sps[l], sel, sb["tag"])
        nxt = None
        if i + 1 < nsub:
            nb = subs[i + 1]
            nxt = (nb["pre_g"], nb["mod"], nb["row"])
        xn, h = _post_norm_fwd(xs[-1], f, sb["post_g"], sb["mod"], sb["row"] + 2, nxt, sb["tag"] + "_post_norm")
        xs.append(xn)
        fs.append(f)
        saves.append(sv)

    dy, loss = _loss_fwd_bwd(xs[-1], target, "loss")

    grads = [dict() for _ in range(depth)]
    dmod = [[None] * 6 for _ in range(depth)]
    dx, dh = dy, None
    for i in reversed(range(nsub)):
        sb = subs[i]
        l = sb["l"]
        nxt = None
        if i + 1 < nsub:
            nb = subs[i + 1]
            nxt = (dh, xs[i + 1], nb["pre_g"], nb["mod"], nb["row"])
        dx, df, pb, shg = _norm_bwd(dx, nxt, (fs[i], sb["post_g"], sb["mod"], sb["row"] + 2), sb["tag"] + "_post_norm_bwd")
        if nxt is not None:
            nb = subs[i + 1]
            dmod[nb["l"]][nb["row"]], dmod[nb["l"]][nb["row"] + 1] = pb[:, 0], pb[:, 1]
            grads[nb["l"]]["ffn_pre_g" if nb["ffn"] else "mix_pre_g"] = shg[0]
        dmod[l][sb["row"] + 2] = pb[:, 2]
        grads[l]["ffn_post_g" if sb["ffn"] else "mix_post_g"] = shg[1]
        if sb["ffn"]:
            dh, g = _ffn_bwd(dm, df, saves[i], wfull[l], sps[l], sb["tag"])
        else:
            dh, g = _mix_bwd(dm, df, saves[i], wfull[l], sps[l], sel, sb["tag"])
        grads[l].update(g)
    sb = subs[0]
    grad_x, _, pb, shg = _norm_bwd(dx, (dh, xs[0], sb["pre_g"], sb["mod"], sb["row"]), None, "l0_mix_pre_norm_bwd")
    dmod[0][0], dmod[0][1] = pb[:, 0], pb[:, 1]
    grads[0]["mix_pre_g"] = shg[0]

    for l in range(depth):
        dm6 = jnp.concatenate(dmod[l], axis=-1)
        grads[l]["ada_b"] = jnp.sum(dm6, axis=0)
        dm16 = jnp.pad(dm6, ((0, 16 - Bl), (0, 0))).astype(MXU_DTYPE)
        grads[l]["ada_w"] = _matmul(acts[l], dm16, "TN", WIRE_DTYPE,f"l{l}_dw_ada")
    return loss, grad_x, grads


_WEIGHTS = ("ada_w", "ada_b", "mix_pre_g", "mix_post_g", "w_in", "ssd_conv_w", "ssd_conv_b", "ssd_dt_bias",
            "ssd_a_log", "ssd_d", "ssd_norm_g", "w_ssd_out", "sc_conv_w", "w_sc_out", "w_o", "ffn_pre_g",
            "ffn_post_g", "w_up", "ffn_conv_w", "ffn_conv_b", "w_down")
_INPUTS = ("x", "c") + _WEIGHTS + ("loss_target",) + tuple("m_" + n for n in _WEIGHTS) + tuple("v_" + n for n in _WEIGHTS)
_BIG = {"ada_w": 2, "w_in": 2, "w_ssd_out": 1, "w_sc_out": 1, "w_o": 1, "w_up": 2, "w_down": 1}
_CONV = ("ssd_conv_w", "sc_conv_w", "ffn_conv_w")
_SMALL = tuple(n for n in _WEIGHTS if n not in _BIG)
PACK_COLS, PACK_ROWS = 1024, 256


def _block(a, axis, j, n=N_CHIPS):
    w = a.shape[axis] // n
    return lax.slice_in_dim(a, j * w, (j + 1) * w, axis=axis)


def _step(a):
    x, c, target = a["x"], a["c"], a["loss_target"]
    depth = a["ada_w"].shape[0]
    dm = _dims(x, a["ssd_dt_bias"], a["ssd_norm_g"], a["ssd_conv_b"], a["ffn_conv_b"])
    chip = 2 * lax.axis_index("x") + lax.axis_index("y")

    shards = [a[n].astype(WIRE_DTYPE) for n in _BIG]
    axes = list(_BIG.values())
    kinds = [_gather_kind(s, ax) for s, ax in zip(shards, axes)]
    full = {n: w.astype(MXU_DTYPE) for n, w in zip(_BIG, _gather_weights(shards, axes, "gather_weights"))}
    conv_shapes = [a[n].shape for n in _CONV]
    gotc = _exchange_chips(_pack([a[n] for n in _CONV], F32, LANES, 8), False, "gather_conv_weights")
    piecesc = [_unpack(gotc[j], conv_shapes) for j in range(N_CHIPS)]
    fullc = {n: jnp.concatenate([piecesc[j][i] for j in range(N_CHIPS)], axis=2) for i, n in enumerate(_CONV)}

    wfull, small = [], []
    for l in range(depth):
        wf = {n: full[n][l] for n in _BIG if n != "w_in"}
        wf["w_in_p"] = _permute_w_in(full["w_in"][l], dm)
        wfull.append(wf)
        small.append({n: (fullc[n][l] if n in _CONV else a[n][l]) for n in _SMALL})

    loss_part, grad_x, grads = _local_step(dm, x, c, target, wfull, small)

    gfull = []
    for n, kind in zip(_BIG, kinds):
        per_layer = []
        for l in range(depth):
            g = _unpermute_w_in(grads[l]["w_in_p"], dm) if n == "w_in" else grads[l][n]
            if kind == "slots":
                g = jnp.moveaxis(g.reshape(g.shape[0], N_CHIPS, g.shape[1] // N_CHIPS), 1, 0)
            per_layer.append(g.astype(WIRE_DTYPE))
        gfull.append(per_layer)
    got = _scatter_grads(gfull, kinds, "scatter_grads")
    two_d = [(a[n].shape[0] * a[n].shape[1], a[n].shape[2]) for n in _BIG]
    mine = [_sum_slots(g.reshape((N_CHIPS,) + s), "sum_chip_grads_" + n) for g, s, n in zip(got, two_d, _BIG)]
    theirs = _sibling_swap(mine, "swap_core_grads")

    out = {}
    for i, n in enumerate(_BIG):
        shp = a[n].shape
        res = _adamw(a[n].reshape(two_d[i]), a["m_" + n].reshape(two_d[i]), a["v_" + n].reshape(two_d[i]),
                     mine[i], theirs[i], "adamw_" + n)
        out[n] = [r.reshape(shp) for r in res]

    gsmall = [jnp.stack([grads[l][n] for l in range(depth)]) for n in _SMALL]
    small_shapes = [g.shape for g in gsmall]
    summed = _sum_slots(_allgather_all(_pack(gsmall, F32, LANES, 8), "gather_small_grads"), "sum_small_grads")
    gs = dict(zip(_SMALL, _unpack(summed, small_shapes)))
    for n in _CONV:
        wcols = a[n].shape[2]
        gs[n] = lax.dynamic_slice_in_dim(gs[n], chip * wcols, wcols, axis=2)
    local_shapes = [a[n].shape for n in _SMALL]
    res = _adamw(_pack([a[n] for n in _SMALL], F32, LANES, 8), _pack([a["m_" + n] for n in _SMALL], F32, LANES, 8),
                 _pack([a["v_" + n] for n in _SMALL], F32, LANES, 8), _pack([gs[n] for n in _SMALL], F32, LANES, 8),
                 None, "adamw_small")
    res = [_unpack(r, local_shapes) for r in res]
    for i, n in enumerate(_SMALL):
        out[n] = [r[i] for r in res]

    loss = lax.psum(loss_part, ("x", "y", "c"))
    return (loss, grad_x) + tuple(out[n][k] for k in range(4) for n in _WEIGHTS)


def kernel(x, c, ada_w, ada_b, mix_pre_g, mix_post_g, w_in, ssd_conv_w, ssd_conv_b, ssd_dt_bias, ssd_a_log, ssd_d, ssd_norm_g, w_ssd_out, sc_conv_w, w_sc_out, w_o, ffn_pre_g, ffn_post_g, w_up, ffn_conv_w, ffn_conv_b, w_down, loss_target, m_ada_w, m_ada_b, m_mix_pre_g, m_mix_post_g, m_w_in, m_ssd_conv_w, m_ssd_conv_b, m_ssd_dt_bias, m_ssd_a_log, m_ssd_d, m_ssd_norm_g, m_w_ssd_out, m_sc_conv_w, m_w_sc_out, m_w_o, m_ffn_pre_g, m_ffn_post_g, m_w_up, m_ffn_conv_w, m_ffn_conv_b, m_w_down, v_ada_w, v_ada_b, v_mix_pre_g, v_mix_post_g, v_w_in, v_ssd_conv_w, v_ssd_conv_b, v_ssd_dt_bias, v_ssd_a_log, v_ssd_d, v_ssd_norm_g, v_w_ssd_out, v_sc_conv_w, v_w_sc_out, v_w_o, v_ffn_pre_g, v_ffn_post_g, v_w_up, v_ffn_conv_w, v_ffn_conv_b, v_w_down):
    return _step(dict(zip(_INPUTS, (
        x, c, ada_w, ada_b, mix_pre_g, mix_post_g, w_in, ssd_conv_w, ssd_conv_b, ssd_dt_bias, ssd_a_log, ssd_d, ssd_norm_g, w_ssd_out, sc_conv_w, w_sc_out, w_o, ffn_pre_g, ffn_post_g, w_up, ffn_conv_w, ffn_conv_b, w_down, loss_target, m_ada_w, m_ada_b, m_mix_pre_g, m_mix_post_g, m_w_in, m_ssd_conv_w, m_ssd_conv_b, m_ssd_dt_bias, m_ssd_a_log, m_ssd_d, m_ssd_norm_g, m_w_ssd_out, m_sc_conv_w, m_w_sc_out, m_w_o, m_ffn_pre_g, m_ffn_post_g, m_w_up, m_ffn_conv_w, m_ffn_conv_b, m_w_down, v_ada_w, v_ada_b, v_mix_pre_g, v_mix_post_g, v_w_in, v_ssd_conv_w, v_ssd_conv_b, v_ssd_dt_bias, v_ssd_a_log, v_ssd_d, v_ssd_norm_g, v_w_ssd_out, v_sc_conv_w, v_w_sc_out, v_w_o, v_ffn_pre_g, v_ffn_post_g, v_w_up, v_ffn_conv_w, v_ffn_conv_b, v_w_down))))
```

```python
import math

import jax
import jax.numpy as jnp
from jax import lax
from jax.experimental import pallas as pl
from jax.experimental.pallas import tpu as pltpu

F32 = jnp.float32
MXU_DTYPE = jnp.bfloat16
WIRE_DTYPE = jnp.bfloat16
HI = lax.Precision.HIGHEST
EPS = 1e-6
N_GROUPS = 4
CHUNK = 128
HEAD_DIM = 64
LANES = 128
HALO = 8
N_CHIPS = 4
N_DEV = 8
VMEM_LIMIT = 56 * 1024 * 1024
ADAM_LR, ADAM_B1, ADAM_B2, ADAM_EPS, ADAM_WD, ADAM_STEP = 0.001, 0.9, 0.999, 1e-08, 0.01, 10
MESH = pl.DeviceIdType.MESH

NN = (((1,), (0,)), ((), ()))
NT = (((1,), (1,)), ((), ()))
TN = (((0,), (0,)), ((), ()))


def _dg(a, b, dn, precision=None):
    return lax.dot_general(a, b, dn, precision=precision, preferred_element_type=F32)


def _tile(dim, pref, mult=LANES):
    t = (min(pref, dim) // mult) * mult
    while t >= mult:
        if dim % t == 0:
            return t
        t -= mult
    return dim


def _cp(sem):
    return pltpu.CompilerParams(dimension_semantics=sem, vmem_limit_bytes=VMEM_LIMIT)


def _sigmoid(x):
    return 1.0 / (1.0 + jnp.exp(-x))


def _softplus(x):
    return jnp.maximum(x, 0.0) + jnp.log1p(jnp.exp(-jnp.abs(x)))


def _rows8(v):
    v = v.reshape(1, -1).astype(F32)
    return jnp.pad(v, ((0, 7), (0, 0)))


def _matmul(a, b, mode, out_dtype, name, tm=1024, tn=1024, tk=1024):
    if mode == "NN":
        (M, K), N = a.shape, b.shape[1]
    elif mode == "NT":
        (M, K), N = a.shape, b.shape[0]
    else:
        (K, M), N = a.shape, b.shape[1]
    tm, tn, tk = _tile(M, tm), _tile(N, tn), _tile(K, tk)
    nk = K // tk
    dn = {"NN": NN, "NT": NT, "TN": TN}[mode]

    def body_one(a_ref, b_ref, o_ref):
        o_ref[...] = _dg(a_ref[...], b_ref[...], dn).astype(o_ref.dtype)

    def body_acc(a_ref, b_ref, o_ref, acc_ref):
        k = pl.program_id(2)

        @pl.when(k == 0)
        def _():
            acc_ref[...] = jnp.zeros_like(acc_ref)

        acc_ref[...] += _dg(a_ref[...], b_ref[...], dn)

        @pl.when(k == nk - 1)
        def _():
            o_ref[...] = acc_ref[...].astype(o_ref.dtype)

    a_spec = (pl.BlockSpec((tk, tm), lambda i, j, k: (k, i)) if mode == "TN"
              else pl.BlockSpec((tm, tk), lambda i, j, k: (i, k)))
    b_spec = (pl.BlockSpec((tn, tk), lambda i, j, k: (j, k)) if mode == "NT"
              else pl.BlockSpec((tk, tn), lambda i, j, k: (k, j)))
    return pl.pallas_call(
        body_one if nk == 1 else body_acc, name=name, grid=(M // tm, N // tn, nk),
        in_specs=[a_spec, b_spec],
        out_specs=pl.BlockSpec((tm, tn), lambda i, j, k: (i, j)),
        out_shape=jax.ShapeDtypeStruct((M, N), out_dtype),
        scratch_shapes=[] if nk == 1 else [pltpu.VMEM((tm, tn), F32)],
        compiler_params=_cp(("parallel", "parallel", "arbitrary")),
    )(a, b)


def _ada_fwd(c16, ada_w, ada_b, name):
    rows, D = c16.shape
    N6 = ada_w.shape[1]
    tn = _tile(N6, 1536)

    def body(c_ref, w_ref, b_ref, act_ref, mod_ref):
        c = c_ref[...]
        act = (c * _sigmoid(c)).astype(act_ref.dtype)
        act_ref[...] = act
        mod_ref[...] = _dg(act, w_ref[...], NN) + b_ref[...]

    return pl.pallas_call(
        body, name=name, grid=(N6 // tn,),
        in_specs=[pl.BlockSpec((rows, D), lambda j: (0, 0)),
                  pl.BlockSpec((D, tn), lambda j: (0, j)),
                  pl.BlockSpec((1, tn), lambda j: (0, j))],
        out_specs=[pl.BlockSpec((rows, D), lambda j: (0, 0)),
                   pl.BlockSpec((rows, tn), lambda j: (0, j))],
        out_shape=[jax.ShapeDtypeStruct((rows, D), MXU_DTYPE),
                   jax.ShapeDtypeStruct((rows, N6), F32)],
        compiler_params=_cp(("arbitrary",)),
    )(c16, ada_w, ada_b.reshape(1, N6))


def _norm_mod_rows(x, g, sc, sh):
    r = lax.rsqrt(jnp.mean(x * x, axis=-1, keepdims=True) + EPS)
    return ((x * r) * g) * (1.0 + sc) + sh


def _norm_mod_fwd(x, g, mod, row_sh, name, ts=512):
    Bl, S, D = x.shape
    ts = _tile(S, ts, 8)

    def body(x_ref, g_ref, mod_ref, h_ref):
        sh = mod_ref[row_sh:row_sh + 1, :]
        sc = mod_ref[row_sh + 1:row_sh + 2, :]
        h_ref[...] = _norm_mod_rows(x_ref[...], g_ref[...], sc, sh).astype(h_ref.dtype)

    tok = pl.BlockSpec((None, ts, D), lambda b, s: (b, s, 0))
    return pl.pallas_call(
        body, name=name, grid=(Bl, S // ts),
        in_specs=[tok, pl.BlockSpec((1, D), lambda b, s: (0, 0)),
                  pl.BlockSpec((None, 8, D), lambda b, s: (b, 0, 0))],
        out_specs=tok, out_shape=jax.ShapeDtypeStruct((Bl, S, D), MXU_DTYPE),
        compiler_params=_cp(("parallel", "parallel")),
    )(x, g.reshape(1, D), mod)


def _post_norm_fwd(xp, f, post_g, mod, row_gt, nxt, name, ts=512):
    Bl, S, D = xp.shape
    ts = _tile(S, ts, 8)
    has_next = nxt is not None

    def body(*refs):
        if has_next:
            xp_ref, f_ref, pg_ref, mod_ref, ng_ref, nmod_ref, x_ref, h_ref = refs
        else:
            xp_ref, f_ref, pg_ref, mod_ref, x_ref = refs
        f = f_ref[...]
        r = lax.rsqrt(jnp.mean(f * f, axis=-1, keepdims=True) + EPS)
        x = xp_ref[...] + mod_ref[row_gt:row_gt + 1, :] * ((f * r) * pg_ref[...])
        x_ref[...] = x
        if has_next:
            rs = nxt[2]
            h_ref[...] = _norm_mod_rows(x, ng_ref[...], nmod_ref[rs + 1:rs + 2, :], nmod_ref[rs:rs + 1, :]).astype(h_ref.dtype)

    tok = pl.BlockSpec((None, ts, D), lambda b, s: (b, s, 0))
    vec = pl.BlockSpec((1, D), lambda b, s: (0, 0))
    modspec = pl.BlockSpec((None, 8, D), lambda b, s: (b, 0, 0))
    ins = [xp, f, post_g.reshape(1, D), mod]
    in_specs = [tok, tok, vec, modspec]
    out_specs = [tok]
    out_shape = [jax.ShapeDtypeStruct((Bl, S, D), F32)]
    if has_next:
        ins += [nxt[0].reshape(1, D), nxt[1]]
        in_specs += [vec, modspec]
        out_specs += [tok]
        out_shape += [jax.ShapeDtypeStruct((Bl, S, D), MXU_DTYPE)]
    out = pl.pallas_call(
        body, name=name, grid=(Bl, S // ts), in_specs=in_specs, out_specs=out_specs, out_shape=out_shape,
        compiler_params=_cp(("parallel", "parallel")),
    )(*ins)
    return (out[0], out[1]) if has_next else (out[0], None)


def _loss_fwd_bwd(y, target, name, ts=512):
    Bl, S, D = y.shape
    ts = _tile(S, ts, 8)

    def body(y_ref, t_ref, dy_ref, l_ref):
        @pl.when((pl.program_id(0) == 0) & (pl.program_id(1) == 0))
        def _():
            l_ref[...] = jnp.zeros_like(l_ref)

        e = y_ref[...] - t_ref[...]
        dy_ref[...] = e * (1.0 / D)
        l_ref[...] += 0.5 * jnp.sum(jnp.mean(e * e, axis=-1, keepdims=True), axis=0, keepdims=True)

    tok = pl.BlockSpec((None, ts, D), lambda b, s: (b, s, 0))
    dy, l = pl.pallas_call(
        body, name=name, grid=(Bl, S // ts), in_specs=[tok, tok],
        out_specs=[tok, pl.BlockSpec((8, LANES), lambda b, s: (0, 0))],
        out_shape=[jax.ShapeDtypeStruct((Bl, S, D), F32), jax.ShapeDtypeStruct((8, LANES), F32)],
        compiler_params=_cp(("arbitrary", "arbitrary")),
    )(y, target)
    return dy, l[0, 0]


def _norm_bwd(dx_res, nxt, prv, name, ts=256):
    Bl, S, D = dx_res.shape
    ts = _tile(S, ts, 8)
    has_next, has_prev = nxt is not None, prv is not None

    def body(*refs):
        refs = list(refs)
        dxr_ref = refs.pop(0)
        if has_next:
            dh_ref, x_ref, g_ref, nmod_ref = refs[:4]
            refs = refs[4:]
        if has_prev:
            f_ref, pg_ref, pmod_ref = refs[:3]
            refs = refs[3:]
        dx_ref = refs.pop(0)
        if has_prev:
            df_ref = refs.pop(0)
        pb_ref, sh_ref = refs
        b, s = pl.program_id(0), pl.program_id(1)

        @pl.when(s == 0)
        def _():
            pb_ref[...] = jnp.zeros_like(pb_ref)

        @pl.when((b == 0) & (s == 0))
        def _():
            sh_ref[...] = jnp.zeros_like(sh_ref)

        dx = dxr_ref[...]
        if has_next:
            rs = nxt[4]
            x, dh, g = x_ref[...], dh_ref[...], g_ref[...]
            sc1 = 1.0 + nmod_ref[rs + 1:rs + 2, :]
            r = lax.rsqrt(jnp.mean(x * x, axis=-1, keepdims=True) + EPS)
            xn = x * r
            pb_ref[0:1, :] += jnp.sum(dh, axis=0, keepdims=True)
            pb_ref[1:2, :] += jnp.sum(dh * (xn * g), axis=0, keepdims=True)
            sh_ref[0:1, :] += jnp.sum(dh * sc1 * xn, axis=0, keepdims=True)
            dxn = dh * sc1 * g
            dx = dx + r * (dxn - xn * jnp.mean(dxn * xn, axis=-1, keepdims=True))
        dx_ref[...] = dx
        if has_prev:
            rg = prv[3]
            f, pg = f_ref[...], pg_ref[...]
            gt = pmod_ref[rg:rg + 1, :]
            r = lax.rsqrt(jnp.mean(f * f, axis=-1, keepdims=True) + EPS)
            fn = f * r
            pb_ref[2:3, :] += jnp.sum(dx * (fn * pg), axis=0, keepdims=True)
            drn = dx * gt
            sh_ref[1:2, :] += jnp.sum(drn * fn, axis=0, keepdims=True)
            dfn = drn * pg
            df_ref[...] = (r * (dfn - fn * jnp.mean(dfn * fn, axis=-1, keepdims=True))).astype(df_ref.dtype)

    tok = pl.BlockSpec((None, ts, D), lambda b, s: (b, s, 0))
    vec = pl.BlockSpec((1, D), lambda b, s: (0, 0))
    modspec = pl.BlockSpec((None, 8, D), lambda b, s: (b, 0, 0))
    ins, in_specs = [dx_res], [tok]
    if has_next:
        ins += [nxt[0], nxt[1], nxt[2].reshape(1, D), nxt[3]]
        in_specs += [tok, tok, vec, modspec]
    if has_prev:
        ins += [prv[0], prv[1].reshape(1, D), prv[2]]
        in_specs += [tok, vec, modspec]
    out_specs, out_shape = [tok], [jax.ShapeDtypeStruct((Bl, S, D), F32)]
    if has_prev:
        out_specs += [tok]
        out_shape += [jax.ShapeDtypeStruct((Bl, S, D), MXU_DTYPE)]
    out_specs += [modspec, pl.BlockSpec((8, D), lambda b, s: (0, 0))]
    out_shape += [jax.ShapeDtypeStruct((Bl, 8, D), F32), jax.ShapeDtypeStruct((8, D), F32)]
    out = pl.pallas_call(
        body, name=name, grid=(Bl, S // ts), in_specs=in_specs, out_specs=out_specs, out_shape=out_shape,
        compiler_params=_cp(("arbitrary", "arbitrary")),
    )(*ins)
    if has_prev:
        return out[0], out[1], out[2], out[3]
    return out[0], None, out[1], out[2]


def _shift_down(x, j):
    return x if j == 0 else pltpu.roll(x, j, axis=0)


def _shift_up(x, j):
    return x if j == 0 else pltpu.roll(x, x.shape[0] - j, axis=0)


def _conv(xall, w_ref, K):
    y = w_ref[K - 1:K, :] * xall
    for k in range(K - 1):
        y = y + w_ref[k:k + 1, :] * _shift_down(xall, K - 1 - k)
    return y


def _conv_t(dall, w_ref, K):
    y = w_ref[K - 1:K, :] * dall
    for k in range(K - 1):
        y = y + w_ref[k:k + 1, :] * _shift_up(dall, K - 1 - k)
    return y


def _conv_wgrad(acc_ref, row0, dtile, xall, K, ts):
    for k in range(K):
        xs = _shift_down(xall, K - 1 - k)[HALO:HALO + ts]
        acc_ref[row0 + k:row0 + k + 1, :] += jnp.sum(dtile * xs, axis=0, keepdims=True)


def _halo_specs(ts, W, nS, colblk):
    per = ts // HALO
    tile = pl.BlockSpec((None, ts, W), lambda b, c, s: (b, s, colblk(c)))
    prev = pl.BlockSpec((None, HALO, W), lambda b, c, s: (b, jnp.maximum(s * per - 1, 0), colblk(c)))
    nxt = pl.BlockSpec((None, HALO, W), lambda b, c, s: (b, jnp.minimum((s + 1) * per, nS * per - 1), colblk(c)))
    return tile, prev, nxt


def _masked(ref, keep):
    v = ref[...]
    return jnp.where(keep, v, jnp.zeros_like(v))


def _ssd_conv_fwd(proj, w8, b, off, CD, name, ts=512, W=1024):
    Bl, S, _ = proj.shape
    K = 4
    ts, W = _tile(S, ts, 8), _tile(math.gcd(CD, off), W)
    assert off % W == 0
    nS, nW, ob = S // ts, CD // W, off // W

    def body(x_ref, xp_ref, w_ref, b_ref, o_ref):
        s = pl.program_id(2)
        xall = jnp.concatenate([_masked(xp_ref, s > 0), x_ref[...]], axis=0)
        xc = _conv(xall, w_ref, K)[HALO:] + b_ref[...]
        o_ref[...] = xc * _sigmoid(xc)

    tile, prev, _ = _halo_specs(ts, W, nS, lambda c: ob + c)
    return pl.pallas_call(
        body, name=name, grid=(Bl, nW, nS),
        in_specs=[tile, prev, pl.BlockSpec((8, W), lambda b_, c, s: (0, c)), pl.BlockSpec((1, W), lambda b_, c, s: (0, c))],
        out_specs=pl.BlockSpec((None, ts, W), lambda b_, c, s: (b_, s, c)),
        out_shape=jax.ShapeDtypeStruct((Bl, S, CD), F32),
        compiler_params=_cp(("parallel", "parallel", "parallel")),
    )(proj, proj, w8, b.reshape(1, CD))


def _ssd_conv_bwd(dxa, proj, w8, b, off, CD, name, ts=256, W=1024):
    Bl, S, _ = proj.shape
    K = 4
    ts, W = _tile(S, ts, 8), _tile(math.gcd(CD, off), W)
    nS, nW, ob = S // ts, CD // W, off // W

    def body(d_ref, dn_ref, x_ref, xp_ref, xn_ref, w_ref, b_ref, dx_ref, dw_ref, db_ref):
        bb, s = pl.program_id(1), pl.program_id(2)

        @pl.when((bb == 0) & (s == 0))
        def _():
            dw_ref[...] = jnp.zeros_like(dw_ref)
            db_ref[...] = jnp.zeros_like(db_ref)

        last = s == nS - 1
        xall = jnp.concatenate([_masked(xp_ref, s > 0), x_ref[...], _masked(xn_ref, ~last)], axis=0)
        dall = jnp.concatenate([jnp.zeros((HALO, W), F32), d_ref[...], _masked(dn_ref, ~last)], axis=0)
        xc = _conv(xall, w_ref, K) + b_ref[...]
        sg = _sigmoid(xc)
        dxc = dall * (sg * (1.0 + xc * (1.0 - sg)))
        dx_ref[...] = _conv_t(dxc, w_ref, K)[HALO:HALO + ts].astype(dx_ref.dtype)
        dtile = dxc[HALO:HALO + ts]
        db_ref[0:1, :] += jnp.sum(dtile, axis=0, keepdims=True)
        _conv_wgrad(dw_ref, 0, dtile, xall, K, ts)

    per = ts // HALO
    dtile_s = pl.BlockSpec((None, ts, W), lambda c, b_, s: (b_, s, c))
    dnext_s = pl.BlockSpec((None, HALO, W), lambda c, b_, s: (b_, jnp.minimum((s + 1) * per, nS * per - 1), c))
    xtile_s = pl.BlockSpec((None, ts, W), lambda c, b_, s: (b_, s, ob + c))
    xprev_s = pl.BlockSpec((None, HALO, W), lambda c, b_, s: (b_, jnp.maximum(s * per - 1, 0), ob + c))
    xnext_s = pl.BlockSpec((None, HALO, W), lambda c, b_, s: (b_, jnp.minimum((s + 1) * per, nS * per - 1), ob + c))
    wspec = pl.BlockSpec((8, W), lambda c, b_, s: (0, c))
    return pl.pallas_call(
        body, name=name, grid=(nW, Bl, nS),
        in_specs=[dtile_s, dnext_s, xtile_s, xprev_s, xnext_s, wspec, pl.BlockSpec((1, W), lambda c, b_, s: (0, c))],
        out_specs=[dtile_s, wspec, wspec],
        out_shape=[jax.ShapeDtypeStruct((Bl, S, CD), MXU_DTYPE), jax.ShapeDtypeStruct((8, CD), F32),
                   jax.ShapeDtypeStruct((8, CD), F32)],
        compiler_params=_cp(("arbitrary", "arbitrary", "arbitrary")),
    )(dxa, dxa, proj, proj, proj, w8, b.reshape(1, CD))


def _sc_conv_fwd(proj, w8, offs, D, name, ts=512, W=1024):
    Bl, S, _ = proj.shape
    K = 3
    ts, W = _tile(S, ts, 8), _tile(D, W)
    nS, nW = S // ts, D // W
    ob, oc, oh = [o // W for o in offs]

    def body(b_ref, c_ref, cp_ref, h_ref, hp_ref, w_ref, o_ref):
        s = pl.program_id(2)
        keep = s > 0
        vall = jnp.concatenate([_masked(cp_ref, keep) * _masked(hp_ref, keep), c_ref[...] * h_ref[...]], axis=0)
        o_ref[...] = (b_ref[...] * _conv(vall, w_ref, K)[HALO:]).astype(o_ref.dtype)

    tb, _, _ = _halo_specs(ts, W, nS, lambda c: ob + c)
    tc, pc, _ = _halo_specs(ts, W, nS, lambda c: oc + c)
    th, ph, _ = _halo_specs(ts, W, nS, lambda c: oh + c)
    return pl.pallas_call(
        body, name=name, grid=(Bl, nW, nS),
        in_specs=[tb, tc, pc, th, ph, pl.BlockSpec((8, W), lambda b_, c, s: (0, c))],
        out_specs=pl.BlockSpec((None, ts, W), lambda b_, c, s: (b_, s, c)),
        out_shape=jax.ShapeDtypeStruct((Bl, S, D), MXU_DTYPE),
        compiler_params=_cp(("parallel", "parallel", "parallel")),
    )(proj, proj, proj, proj, proj, w8)


def _sc_conv_bwd(ds, proj, w8, offs, D, name, ts=256, W=1024):
    Bl, S, _ = proj.shape
    K = 3
    ts, W = _tile(S, ts, 8), _tile(D, W)
    nS, nW = S // ts, D // W
    ob, oc, oh = [o // W for o in offs]

    def body(d_ref, dn_ref, b_ref, bn_ref, c_ref, cp_ref, cn_ref, h_ref, hp_ref, hn_ref, w_ref,
             db_ref, dc_ref, dh_ref, dw_ref):
        bb, s = pl.program_id(1), pl.program_id(2)

        @pl.when((bb == 0) & (s == 0))
        def _():
            dw_ref[...] = jnp.zeros_like(dw_ref)

        first, last = s > 0, s < nS - 1
        zeros = jnp.zeros((HALO, W), F32)
        c_t, h_t = c_ref[...], h_ref[...]
        vall = jnp.concatenate([_masked(cp_ref, first) * _masked(hp_ref, first), c_t * h_t,
                                _masked(cn_ref, last) * _masked(hn_ref, last)], axis=0)
        dcv = jnp.concatenate([zeros, d_ref[...] * b_ref[...], _masked(dn_ref, last) * _masked(bn_ref, last)], axis=0)
        cv = _conv(vall, w_ref, K)[HALO:HALO + ts]
        db_ref[...] = (d_ref[...] * cv).astype(db_ref.dtype)
        dv = _conv_t(dcv, w_ref, K)[HALO:HALO + ts]
        dc_ref[...] = (dv * h_t).astype(dc_ref.dtype)
        dh_ref[...] = (dv * c_t).astype(dh_ref.dtype)
        _conv_wgrad(dw_ref, 0, dcv[HALO:HALO + ts], vall, K, ts)

    per = ts // HALO

    def specs(o):
        t = pl.BlockSpec((None, ts, W), lambda c, b_, s: (b_, s, o + c))
        p = pl.BlockSpec((None, HALO, W), lambda c, b_, s: (b_, jnp.maximum(s * per - 1, 0), o + c))
        n = pl.BlockSpec((None, HALO, W), lambda c, b_, s: (b_, jnp.minimum((s + 1) * per, nS * per - 1), o + c))
        return t, p, n

    dt_, _, dn_ = specs(0)
    bt, _, bn = specs(ob)
    ct, cp, cn = specs(oc)
    ht, hp, hn = specs(oh)
    wspec = pl.BlockSpec((8, W), lambda c, b_, s: (0, c))
    act = jax.ShapeDtypeStruct((Bl, S, D), MXU_DTYPE)
    return pl.pallas_call(
        body, name=name, grid=(nW, Bl, nS),
        in_specs=[dt_, dn_, bt, bn, ct, cp, cn, ht, hp, hn, wspec],
        out_specs=[dt_, dt_, dt_, wspec],
        out_shape=[act, act, act, jax.ShapeDtypeStruct((8, D), F32)],
        compiler_params=_cp(("arbitrary", "arbitrary", "arbitrary")),
    )(ds, ds, proj, proj, proj, proj, proj, proj, proj, proj, w8)


def _ffn_conv_fwd(up, w8, b, DFF, name, ts=512, W=1408):
    Bl, S, _ = up.shape
    K = 3
    ts, W = _tile(S, ts, 8), _tile(DFF, W)
    nS, nW = S // ts, DFF // W

    def body(g_ref, gp_ref, v_ref, vp_ref, wg_ref, wv_ref, bg_ref, bv_ref, o_ref):
        keep = pl.program_id(2) > 0
        ug = _conv(jnp.concatenate([_masked(gp_ref, keep), g_ref[...]], axis=0), wg_ref, K)[HALO:] + bg_ref[...]
        uv = _conv(jnp.concatenate([_masked(vp_ref, keep), v_ref[...]], axis=0), wv_ref, K)[HALO:] + bv_ref[...]
        o_ref[...] = (ug * _sigmoid(ug) * uv).astype(o_ref.dtype)

    tg, pg, _ = _halo_specs(ts, W, nS, lambda c: c)
    tv, pv, _ = _halo_specs(ts, W, nS, lambda c: nW + c)
    wg = pl.BlockSpec((8, W), lambda b_, c, s: (0, c))
    wv = pl.BlockSpec((8, W), lambda b_, c, s: (0, nW + c))
    bg = pl.BlockSpec((1, W), lambda b_, c, s: (0, c))
    bv = pl.BlockSpec((1, W), lambda b_, c, s: (0, nW + c))
    b2 = b.reshape(1, 2 * DFF)
    return pl.pallas_call(
        body, name=name, grid=(Bl, nW, nS),
        in_specs=[tg, pg, tv, pv, wg, wv, bg, bv],
        out_specs=pl.BlockSpec((None, ts, W), lambda b_, c, s: (b_, s, c)),
        out_shape=jax.ShapeDtypeStruct((Bl, S, DFF), MXU_DTYPE),
        compiler_params=_cp(("parallel", "parallel", "parallel")),
    )(up, up, up, up, w8, w8, b2, b2)


def _ffn_conv_bwd(da, up, w8, b, DFF, name, ts=256, W=1408):
    Bl, S, _ = up.shape
    K = 3
    ts, W = _tile(S, ts, 8), _tile(DFF, W)
    nS, nW = S // ts, DFF // W

    def body(d_ref, dn_ref, g_ref, gp_ref, gn_ref, v_ref, vp_ref, vn_ref, wg_ref, wv_ref, bg_ref, bv_ref,
             dg_ref, dv_ref, dwg_ref, dwv_ref, dbg_ref, dbv_ref):
        bb, s = pl.program_id(1), pl.program_id(2)

        @pl.when((bb == 0) & (s == 0))
        def _():
            for r in (dwg_ref, dwv_ref, dbg_ref, dbv_ref):
                r[...] = jnp.zeros_like(r)

        first, last = s > 0, s < nS - 1
        gall = jnp.concatenate([_masked(gp_ref, first), g_ref[...], _masked(gn_ref, last)], axis=0)
        vall = jnp.concatenate([_masked(vp_ref, first), v_ref[...], _masked(vn_ref, last)], axis=0)
        dall = jnp.concatenate([jnp.zeros((HALO, W), F32), d_ref[...], _masked(dn_ref, last)], axis=0)
        ug = _conv(gall, wg_ref, K) + bg_ref[...]
        uv = _conv(vall, wv_ref, K) + bv_ref[...]
        sg = _sigmoid(ug)
        dug = dall * uv * (sg * (1.0 + ug * (1.0 - sg)))
        duv = dall * (ug * sg)
        dg_ref[...] = _conv_t(dug, wg_ref, K)[HALO:HALO + ts].astype(dg_ref.dtype)
        dv_ref[...] = _conv_t(duv, wv_ref, K)[HALO:HALO + ts].astype(dv_ref.dtype)
        dug_t, duv_t = dug[HALO:HALO + ts], duv[HALO:HALO + ts]
        dbg_ref[0:1, :] += jnp.sum(dug_t, axis=0, keepdims=True)
        dbv_ref[0:1, :] += jnp.sum(duv_t, axis=0, keepdims=True)
        _conv_wgrad(dwg_ref, 0, dug_t, gall, K, ts)
        _conv_wgrad(dwv_ref, 0, duv_t, vall, K, ts)

    per = ts // HALO

    def specs(o):
        t = pl.BlockSpec((None, ts, W), lambda c, b_, s: (b_, s, o + c))
        p = pl.BlockSpec((None, HALO, W), lambda c, b_, s: (b_, jnp.maximum(s * per - 1, 0), o + c))
        n = pl.BlockSpec((None, HALO, W), lambda c, b_, s: (b_, jnp.minimum((s + 1) * per, nS * per - 1), o + c))
        return t, p, n

    dt_, _, dn_ = specs(0)
    gt, gp, gn = specs(0)
    vt, vp, vn = specs(nW)
    wg = pl.BlockSpec((8, W), lambda c, b_, s: (0, c))
    wv = pl.BlockSpec((8, W), lambda c, b_, s: (0, nW + c))
    bg = pl.BlockSpec((1, W), lambda c, b_, s: (0, c))
    bv = pl.BlockSpec((1, W), lambda c, b_, s: (0, nW + c))
    b2 = b.reshape(1, 2 * DFF)
    act = jax.ShapeDtypeStruct((Bl, S, DFF), MXU_DTYPE)
    small = jax.ShapeDtypeStruct((8, DFF), F32)
    dg, dv, dwg, dwv, dbg, dbv = pl.pallas_call(
        body, name=name, grid=(nW, Bl, nS),
        in_specs=[dt_, dn_, gt, gp, gn, vt, vp, vn, wg, wv, bg, bv],
        out_specs=[dt_, dt_, wg, wg, wg, wg],
        out_shape=[act, act, small, small, small, small],
        compiler_params=_cp(("arbitrary", "arbitrary", "arbitrary")),
    )(da, da, up, up, up, up, up, up, w8, w8, b2, b2)
    return dg, dv, jnp.concatenate([dwg, dwv], axis=1), jnp.concatenate([dbg, dbv], axis=1)


def _merge_fwd(proj, y_ssd, y_sc, off, D, name, ts=512):
    Bl, S, _ = proj.shape
    ts = _tile(S, ts, 8)
    og = off // D

    def body(g1_ref, g2_ref, a_ref, b_ref, o_ref):
        o_ref[...] = (_sigmoid(g1_ref[...]) * a_ref[...] + _sigmoid(g2_ref[...]) * b_ref[...]).astype(o_ref.dtype)

    tok = pl.BlockSpec((None, ts, D), lambda b, s: (b, s, 0))
    return pl.pallas_call(
        body, name=name, grid=(Bl, S // ts),
        in_specs=[pl.BlockSpec((None, ts, D), lambda b, s: (b, s, og)),
                  pl.BlockSpec((None, ts, D), lambda b, s: (b, s, og + 1)), tok, tok],
        out_specs=tok, out_shape=jax.ShapeDtypeStruct((Bl, S, D), MXU_DTYPE),
        compiler_params=_cp(("parallel", "parallel")),
    )(proj, proj, y_ssd, y_sc)


def _merge_bwd(dmixin, proj, y_ssd, y_sc, off, D, name, ts=512):
    Bl, S, _ = proj.shape
    ts = _tile(S, ts, 8)
    og = off // D

    def body(d_ref, g1_ref, g2_ref, a_ref, b_ref, da_ref, db_ref, dg1_ref, dg2_ref):
        d = d_ref[...]
        s1, s2 = _sigmoid(g1_ref[...]), _sigmoid(g2_ref[...])
        da_ref[...] = (d * s1).astype(da_ref.dtype)
        db_ref[...] = (d * s2).astype(db_ref.dtype)
        dg1_ref[...] = (d * a_ref[...] * (s1 * (1.0 - s1))).astype(dg1_ref.dtype)
        dg2_ref[...] = (d * b_ref[...] * (s2 * (1.0 - s2))).astype(dg2_ref.dtype)

    tok = pl.BlockSpec((None, ts, D), lambda b, s: (b, s, 0))
    act = jax.ShapeDtypeStruct((Bl, S, D), MXU_DTYPE)
    return pl.pallas_call(
        body, name=name, grid=(Bl, S // ts),
        in_specs=[tok, pl.BlockSpec((None, ts, D), lambda b, s: (b, s, og)),
                  pl.BlockSpec((None, ts, D), lambda b, s: (b, s, og + 1)), tok, tok],
        out_specs=[tok, tok, tok, tok], out_shape=[act, act, act, act],
        compiler_params=_cp(("parallel", "parallel")),
    )(dmixin, proj, proj, y_ssd, y_sc)


def _expand_cols(mat, R, lane):
    half = lane < HEAD_DIM
    return jnp.concatenate(
        [jnp.where(half, mat[:, 2 * q:2 * q + 1], mat[:, 2 * q + 1:2 * q + 2]) for q in range(R // 2)], axis=1)


def _head_rows(colvec, R, N):
    return jnp.concatenate([jnp.broadcast_to(colvec[r:r + 1, :], (HEAD_DIM, N)) for r in range(R)], axis=0)


def _ssd_common(dtr_ref, dtb_ref, alog_ref, sel, L):
    row = lax.broadcasted_iota(jnp.int32, (L, L), 0)
    col = lax.broadcasted_iota(jnp.int32, (L, L), 1)
    causal = row >= col
    eye = (row == col).astype(F32)
    dt_all = _softplus(dtr_ref[...] + dtb_ref[...])
    a_all = -jnp.exp(alog_ref[...])
    dtg = _dg(dt_all, sel, NN, HI)
    adtg = _dg(dt_all * a_all, sel, NN, HI)
    acs = _dg(causal.astype(F32), adtg, NN, HI)
    acs_t = _dg(eye, acs, NT, HI)
    return causal, row, col, dt_all, a_all, dtg, acs, acs_t


def _ssd_fwd(xbc_a, proj, dt_bias, a_log, d_skip, norm_g, sel, dm, name):
    Bl, S, DI, N, R, L, G = dm.Bl, dm.S, dm.DI, dm.N, dm.R, CHUNK, N_GROUPS
    RP = R * HEAD_DIM
    nc = S // L
    ob, ocm, odt = DI // N, DI // N + G, dm.ODT // LANES

    def body(xs_ref, bm_ref, cm_ref, z_ref, dtr_ref, dtb_ref, alog_ref, dsk_ref, ng_ref, sel_ref,
             y_ref, yn_ref, hp_ref, h_ref):
        @pl.when(pl.program_id(2) == 0)
        def _():
            h_ref[...] = jnp.zeros_like(h_ref)

        sel_g = sel_ref[...]
        causal, row, col, dt_all, a_all, dtg, acs, acs_t = _ssd_common(dtr_ref, dtb_ref, alog_ref, sel_g, L)
        lane = lax.broadcasted_iota(jnp.int32, (L, LANES), 1)
        lane1 = lax.broadcasted_iota(jnp.int32, (1, LANES), 1)
        xs = xs_ref[...]
        bmb, cmb = bm_ref[...].astype(MXU_DTYPE), cm_ref[...].astype(MXU_DTYPE)
        sg = _dg(cmb, bmb, NT)
        xdt = xs * _expand_cols(dtg, R, lane)
        xb = xdt.astype(MXU_DTYPE)
        parts = []
        for q in range(R // 2):
            x2 = xb[:, LANES * q:LANES * (q + 1)]
            ys = []
            for r in (2 * q, 2 * q + 1):
                dec = jnp.exp(jnp.where(causal, acs[:, r:r + 1] - acs_t[r:r + 1, :], -1e30))
                ys.append(_dg((sg * dec).astype(MXU_DTYPE), x2, NN))
            parts.append(jnp.where(lane < HEAD_DIM, ys[0], ys[1]))
        ydiag = jnp.concatenate(parts, axis=1)
        acs_last = acs[L - 1:L, :]
        h_cur = h_ref[...]
        hb = h_cur.astype(MXU_DTYPE)
        yoff = _dg(cmb, hb, NT) * _expand_cols(jnp.exp(acs), R, lane)
        st = _dg((xdt * _expand_cols(jnp.exp(acs_last - acs), R, lane)).astype(MXU_DTYPE), bmb, TN)
        hp_ref[...] = hb
        h_ref[...] = h_cur * _head_rows(jnp.exp(acs_t[:, L - 1:L]), R, N) + st
        dsk_g = _dg(jnp.broadcast_to(dsk_ref[...], (8, LANES)), sel_g, NN, HI)[0:1, :]
        y = ydiag + yoff + _expand_cols(dsk_g, R, lane1) * xs
        y_ref[...] = y
        z = z_ref[...]
        yg = y * (z * _sigmoid(z))
        rr = lax.rsqrt(jnp.mean(yg * yg, axis=-1, keepdims=True) + EPS)
        yn_ref[...] = (yg * rr * ng_ref[...]).astype(yn_ref.dtype)

    vec = pl.BlockSpec((1, LANES), lambda b, g, c: (0, 0))
    grp = pl.BlockSpec((None, L, RP), lambda b, g, c: (b, c, g))
    return pl.pallas_call(
        body, name=name, grid=(Bl, G, nc),
        in_specs=[grp,
                  pl.BlockSpec((None, L, N), lambda b, g, c: (b, c, ob + g)),
                  pl.BlockSpec((None, L, N), lambda b, g, c: (b, c, ocm + g)),
                  grp,
                  pl.BlockSpec((None, L, LANES), lambda b, g, c: (b, c, odt)),
                  vec, vec, vec,
                  pl.BlockSpec((1, RP), lambda b, g, c: (0, g)),
                  pl.BlockSpec((None, LANES, LANES), lambda b, g, c: (g, 0, 0))],
        out_specs=[grp, grp, pl.BlockSpec((None, None, None, RP, N), lambda b, g, c: (b, g, c, 0, 0))],
        out_shape=[jax.ShapeDtypeStruct((Bl, S, DI), F32), jax.ShapeDtypeStruct((Bl, S, DI), MXU_DTYPE),
                   jax.ShapeDtypeStruct((Bl, G, nc, RP, N), MXU_DTYPE)],
        scratch_shapes=[pltpu.VMEM((RP, N), F32)],
        compiler_params=_cp(("parallel", "parallel", "arbitrary")),
    )(xbc_a, xbc_a, xbc_a, proj, proj, dt_bias, a_log, d_skip, norm_g.reshape(1, DI), sel)


def _ssd_bwd(dyn, y, xbc_a, proj, hprev, dt_bias, a_log, d_skip, norm_g, sel, dm, name):
    Bl, S, DI, N, R, L, G = dm.Bl, dm.S, dm.DI, dm.N, dm.R, CHUNK, N_GROUPS
    RP = R * HEAD_DIM
    nc = S // L
    ob, ocm, odt = DI // N, DI // N + G, dm.ODT // LANES

    def body(dyn_ref, y_ref, z_ref, xs_ref, bm_ref, cm_ref, dtr_ref, hp_ref, dtb_ref, alog_ref, dsk_ref, ng_ref,
             sel_ref, dz_ref, dxs_ref, dbm_ref, dcm_ref, ddtr_ref, dpar_ref, dng_ref, dh_ref):
        b, c = pl.program_id(1), pl.program_id(2)

        @pl.when(c == 0)
        def _():
            dh_ref[...] = jnp.zeros_like(dh_ref)

        @pl.when((b == 0) & (c == 0))
        def _():
            dpar_ref[...] = jnp.zeros_like(dpar_ref)
            dng_ref[...] = jnp.zeros_like(dng_ref)

        sel_g = sel_ref[...]
        causal, row, col, dt_all, a_all, dtg, acs, acs_t = _ssd_common(dtr_ref, dtb_ref, alog_ref, sel_g, L)
        anti = col >= row
        lane = lax.broadcasted_iota(jnp.int32, (L, LANES), 1)
        lane1 = lax.broadcasted_iota(jnp.int32, (1, LANES), 1)
        et = (lax.shift_right_logical(lax.broadcasted_iota(jnp.int32, (RP, LANES), 0), 6)
              == lax.broadcasted_iota(jnp.int32, (RP, LANES), 1)).astype(F32)

        def headsum(v):
            return _dg(v, et, NN, HI)

        def to_all(v):
            return _dg(v, sel_g, NT, HI)

        xs, z, y, dyn = xs_ref[...], z_ref[...], y_ref[...], dyn_ref[...]
        bm, cm = bm_ref[...], cm_ref[...]
        bmb, cmb = bm.astype(MXU_DTYPE), cm.astype(MXU_DTYPE)
        hpb = hp_ref[...]
        ng = ng_ref[...]

        sz = _sigmoid(z)
        siluz = z * sz
        yg = y * siluz
        rr = lax.rsqrt(jnp.mean(yg * yg, axis=-1, keepdims=True) + EPS)
        yhat = yg * rr
        dng_ref[...] += jnp.sum(dyn * yhat, axis=0, keepdims=True)
        dyhat = dyn * ng
        dyg = rr * (dyhat - yhat * jnp.mean(dyhat * yhat, axis=-1, keepdims=True))
        dy = dyg * siluz
        dz_ref[...] = (dyg * y * (sz * (1.0 + z * (1.0 - sz)))).astype(dz_ref.dtype)

        dsk_g = _dg(jnp.broadcast_to(dsk_ref[...], (8, LANES)), sel_g, NN, HI)[0:1, :]
        dxs = dy * _expand_cols(dsk_g, R, lane1)
        dd_lane = headsum(jnp.sum(dy * xs, axis=0, keepdims=True))

        dt_exp = _expand_cols(dtg, R, lane)
        xdt = xs * dt_exp
        xb = xdt.astype(MXU_DTYPE)
        dyb = dy.astype(MXU_DTYPE)
        acs_last = acs[L - 1:L, :]
        e_a = jnp.exp(acs)
        dsd = jnp.exp(acs_last - acs)
        cd = jnp.exp(acs_last)
        cd_rows = _head_rows(jnp.exp(acs_t[:, L - 1:L]), R, N)

        q_ = _dg(cmb, hpb, NT)
        dq = dy * _expand_cols(e_a, R, lane)
        dqb = dq.astype(MXU_DTYPE)
        dcm = _dg(dqb, hpb, NN)
        dh_yoff = _dg(dqb, cmb, TN)
        d_a = headsum(dq * q_)

        dhn = dh_ref[...]
        dcd_lane = jnp.sum(_dg(dhn * hpb.astype(F32), et, TN, HI), axis=0, keepdims=True)
        d_a_last = dcd_lane * cd
        dh_ref[...] = dhn * cd_rows + dh_yoff
        dhnb = dhn.astype(MXU_DTYPE)

        ds_exp = _expand_cols(dsd, R, lane)
        e_ = _dg(bmb, dhnb, NT)
        dxdt = ds_exp * e_
        t2 = headsum(xdt * e_) * dsd
        d_a = d_a - t2
        d_a_last = d_a_last + jnp.sum(t2, axis=0, keepdims=True)
        dbm = _dg((xdt * ds_exp).astype(MXU_DTYPE), dhnb, NN)

        sg = _dg(cmb, bmb, NT)
        sg_t = _dg(bmb, cmb, NT)
        dsg = jnp.zeros((L, L), F32)
        dsg_t = jnp.zeros((L, L), F32)
        parts = []
        for q in range(R // 2):
            x2 = xb[:, LANES * q:LANES * (q + 1)]
            dy2 = dyb[:, LANES * q:LANES * (q + 1)]
            dxs2 = []
            for hh, r in enumerate((2 * q, 2 * q + 1)):
                mine = (lane < HEAD_DIM) if hh == 0 else (lane >= HEAD_DIM)
                dec = jnp.exp(jnp.where(causal, acs[:, r:r + 1] - acs_t[r:r + 1, :], -1e30))
                dec_t = jnp.exp(jnp.where(anti, acs_t[r:r + 1, :] - acs[:, r:r + 1], -1e30))
                dy2m = jnp.where(mine, dy2, jnp.zeros_like(dy2))
                dm_ = _dg(dy2m, x2, NT)
                dm_t = _dg(x2, dy2m, NT)
                m_t = sg_t * dec_t
                da_col = (jnp.sum(dm_ * (sg * dec), axis=1, keepdims=True)
                          - jnp.sum(dm_t * m_t, axis=1, keepdims=True))
                d_a = d_a + jnp.where(lane == r, da_col, 0.0)
                dsg = dsg + dm_ * dec
                dsg_t = dsg_t + dm_t * dec_t
                dxs2.append(_dg(m_t.astype(MXU_DTYPE), dy2, NN))
            parts.append(jnp.where(lane < HEAD_DIM, dxs2[0], dxs2[1]))
        dxdt = dxdt + jnp.concatenate(parts, axis=1)
        dcm = dcm + _dg(dsg.astype(MXU_DTYPE), bmb, NN)
        dbm = dbm + _dg(dsg_t.astype(MXU_DTYPE), cmb, NN)
        dcm_ref[...] = dcm
        dbm_ref[...] = dbm

        dxs_ref[...] = dxs + dxdt * dt_exp
        ddt = headsum(dxdt * xs)
        rowl = lax.broadcasted_iota(jnp.int32, (L, LANES), 0)
        d_a = d_a + jnp.where(rowl == L - 1, d_a_last, 0.0)
        dadt = _dg(anti.astype(F32), d_a, NN, HI)
        a_g = _dg(jnp.broadcast_to(a_all, (8, LANES)), sel_g, NN, HI)[0:1, :]
        ddt = ddt + dadt * a_g
        da_lane = jnp.sum(dadt * dtg, axis=0, keepdims=True)
        ddtr = to_all(ddt) * _sigmoid(dtr_ref[...] + dtb_ref[...])
        ddtr_ref[...] = ddtr
        dpar_ref[0:1, :] += jnp.sum(ddtr, axis=0, keepdims=True)
        small = to_all(jnp.concatenate([da_lane, dd_lane, jnp.zeros((6, LANES), F32)], axis=0))
        dpar_ref[1:2, :] += small[0:1, :] * a_all
        dpar_ref[2:3, :] += small[1:2, :]

    vec = pl.BlockSpec((1, LANES), lambda g, b, c: (0, 0))
    grp = pl.BlockSpec((None, L, RP), lambda g, b, c: (b, nc - 1 - c, g))
    bspec = pl.BlockSpec((None, L, N), lambda g, b, c: (b, nc - 1 - c, ob + g))
    cspec = pl.BlockSpec((None, L, N), lambda g, b, c: (b, nc - 1 - c, ocm + g))
    gn = pl.BlockSpec((None, L, N), lambda g, b, c: (b, nc - 1 - c, g))
    return pl.pallas_call(
        body, name=name, grid=(G, Bl, nc),
        in_specs=[grp, grp, grp, grp, bspec, cspec,
                  pl.BlockSpec((None, L, LANES), lambda g, b, c: (b, nc - 1 - c, odt)),
                  pl.BlockSpec((None, None, None, RP, N), lambda g, b, c: (b, g, nc - 1 - c, 0, 0)),
                  vec, vec, vec,
                  pl.BlockSpec((1, RP), lambda g, b, c: (0, g)),
                  pl.BlockSpec((None, LANES, LANES), lambda g, b, c: (g, 0, 0))],
        out_specs=[grp, grp, gn, gn,
                   pl.BlockSpec((None, None, L, LANES), lambda g, b, c: (b, g, nc - 1 - c, 0)),
                   pl.BlockSpec((None, 8, LANES), lambda g, b, c: (g, 0, 0)),
                   pl.BlockSpec((1, RP), lambda g, b, c: (0, g))],
        out_shape=[jax.ShapeDtypeStruct((Bl, S, DI), MXU_DTYPE), jax.ShapeDtypeStruct((Bl, S, DI), F32),
                   jax.ShapeDtypeStruct((Bl, S, G * N), F32), jax.ShapeDtypeStruct((Bl, S, G * N), F32),
                   jax.ShapeDtypeStruct((Bl, G, S, LANES), F32), jax.ShapeDtypeStruct((G, 8, LANES), F32),
                   jax.ShapeDtypeStruct((1, DI), F32)],
        scratch_shapes=[pltpu.VMEM((RP, N), F32)],
        compiler_params=_cp(("arbitrary", "arbitrary", "arbitrary")),
    )(dyn, y, proj, xbc_a, xbc_a, xbc_a, proj, hprev, dt_bias, a_log, d_skip, norm_g.reshape(1, DI), sel)


def _adam_update(w, m, v, g):
    c1 = 1.0 - ADAM_B1 ** ADAM_STEP
    c2 = 1.0 - ADAM_B2 ** ADAM_STEP
    nm = ADAM_B1 * m + (1.0 - ADAM_B1) * g
    nv = ADAM_B2 * v + (1.0 - ADAM_B2) * (g * g)
    return nm, nv, -ADAM_LR * ((nm / c1) / (jnp.sqrt(nv / c2) + ADAM_EPS) + ADAM_WD * w)


def _adamw(w, m, v, g, name, tr=128):
    rows, cols = w.shape
    tr = _tile(rows, tr, 8)

    def body(w_ref, m_ref, v_ref, ga_ref, g_ref, d_ref, nm_ref, nv_ref):
        g = ga_ref[...]
        g_ref[...] = g
        nm_ref[...], nv_ref[...], d_ref[...] = _adam_update(w_ref[...], m_ref[...], v_ref[...], g)

    blk = pl.BlockSpec((tr, cols), lambda i: (i, 0))
    shp = jax.ShapeDtypeStruct((rows, cols), F32)
    return pl.pallas_call(
        body, name=name, grid=(rows // tr,), in_specs=[blk] * 4, out_specs=[blk] * 4,
        out_shape=[shp] * 4, compiler_params=_cp(("parallel",)),
    )(w, m, v, g)


def _adamw_layers(w, m, v, g_mine, g_theirs, core, name, tr=128):
    _, rows, cols = w.shape
    tr = _tile(rows, tr, 8)

    def body(c_ref, w_ref, m_ref, v_ref, ga_ref, gb_ref, g_ref, d_ref, nm_ref, nv_ref):
        g = jnp.where(pl.program_id(0) == c_ref[0], ga_ref[...], gb_ref[...])
        g_ref[...] = g
        nm_ref[...], nv_ref[...], d_ref[...] = _adam_update(w_ref[...], m_ref[...], v_ref[...], g)

    lay = pl.BlockSpec((None, tr, cols), lambda l, i, c_ref: (l, i, 0))
    one = pl.BlockSpec((tr, cols), lambda l, i, c_ref: (i, 0))
    shp = jax.ShapeDtypeStruct(w.shape, F32)
    return pl.pallas_call(
        body, name=name,
        grid_spec=pltpu.PrefetchScalarGridSpec(num_scalar_prefetch=1, grid=(2, rows // tr),
                                               in_specs=[lay, lay, lay, one, one], out_specs=[lay] * 4),
        out_shape=[shp] * 4, compiler_params=_cp(("parallel", "parallel")),
    )(core, w, m, v, g_mine, g_theirs)


def _sum_slots(buf, name, tr=256):
    n, rows, cols = buf.shape
    tr = _tile(rows, tr, 8)

    def body(b_ref, o_ref):
        acc = b_ref[0].astype(F32)
        for k in range(1, n):
            acc = acc + b_ref[k].astype(F32)
        o_ref[...] = acc

    return pl.pallas_call(
        body, name=name, grid=(rows // tr,),
        in_specs=[pl.BlockSpec((n, tr, cols), lambda i: (0, i, 0))],
        out_specs=pl.BlockSpec((tr, cols), lambda i: (i, 0)),
        out_shape=jax.ShapeDtypeStruct((rows, cols), F32), compiler_params=_cp(("parallel",)),
    )(buf)


_ANY = pl.BlockSpec(memory_space=pl.ANY)


def _exchange_chips(src, per_dest, name):
    rows, cols = src.shape[-2:]

    def body(in_ref, out_ref, send_sems, recv_sems, local_sem):
        x, y, c = lax.axis_index("x"), lax.axis_index("y"), lax.axis_index("c")
        me = 2 * x + y
        chips = [(1 - x, y), (x, 1 - y), (1 - x, 1 - y)]

        def block(j):
            return in_ref.at[j] if per_dest else in_ref

        mine = pltpu.make_async_copy(block(me), out_ref.at[me], local_sem)
        mine.start()
        sends = []
        for k, (px, py) in enumerate(chips):
            cp = pltpu.make_async_remote_copy(
                src_ref=block(2 * px + py), dst_ref=out_ref.at[me], send_sem=send_sems.at[k],
                recv_sem=recv_sems.at[k], device_id=(px, py, c), device_id_type=MESH)
            cp.start()
            sends.append(cp)
        for k, (px, py) in enumerate(chips):
            pltpu.make_async_remote_copy(
                src_ref=block(me), dst_ref=out_ref.at[2 * px + py], send_sem=send_sems.at[k],
                recv_sem=recv_sems.at[k], device_id=(px, py, c), device_id_type=MESH).wait_recv()
        for cp in sends:
            cp.wait_send()
        mine.wait()

    return pl.pallas_call(
        body, name=name, in_specs=[_ANY], out_specs=_ANY,
        out_shape=jax.ShapeDtypeStruct((N_CHIPS, rows, cols), src.dtype),
        scratch_shapes=[pltpu.SemaphoreType.DMA((3,)), pltpu.SemaphoreType.DMA((3,)), pltpu.SemaphoreType.DMA(())],
    )(src)


def _shard_window(ref, kind, j, lead=()):
    if kind == "slots":
        return ref.at[(j,) + lead]
    r, c = ref.shape[-2] // (N_CHIPS if kind == "rows" else 1), ref.shape[-1] // (N_CHIPS if kind == "cols" else 1)
    full = tuple(slice(None) for _ in range(len(ref.shape) - 2 - len(lead)))
    if kind == "rows":
        return ref.at[lead + full + (pl.ds(pl.multiple_of(j * r, 16), r), slice(None))]
    return ref.at[lead + full + (slice(None), pl.ds(pl.multiple_of(j * c, LANES), c))]


def _gather_kind(shard, axis):
    if axis == 1:
        return "rows"
    return "cols" if shard.shape[2] % LANES == 0 else "slots"


def _gather_weights(shards, axes, name):
    kinds = [_gather_kind(s, ax) for s, ax in zip(shards, axes)]
    nw = len(shards)

    def out_shape(s, kind):
        d, r, c = s.shape
        shp = {"rows": (d, N_CHIPS * r, c), "cols": (d, r, N_CHIPS * c), "slots": (N_CHIPS, d, r, c)}[kind]
        return jax.ShapeDtypeStruct(shp, s.dtype)

    assert all(s.shape[0] == 2 for s in shards)

    def body(*refs):
        ins, outs = refs[:nw], refs[nw:2 * nw]
        ici_send, ici_recv, d2d_send, d2d_recv, local_sems = refs[2 * nw:]
        x, y, c = lax.axis_index("x"), lax.axis_index("y"), lax.axis_index("c")
        me = 2 * x + y
        chips = [(1 - x, y), (x, 1 - y), (1 - x, 1 - y)]
        sends = []
        for i in range(nw):
            own = pltpu.make_async_copy(ins[i], _shard_window(outs[i], kinds[i], me), local_sems.at[i])
            own.start()
            sends.append((own, False))
            for k, (px, py) in enumerate(chips):
                cp = pltpu.make_async_remote_copy(
                    src_ref=ins[i].at[c], dst_ref=_shard_window(outs[i], kinds[i], me, (c,)),
                    send_sem=ici_send.at[3 * i + k], recv_sem=ici_recv.at[3 * i + k],
                    device_id=(px, py, c), device_id_type=MESH)
                cp.start()
                sends.append((cp, True))
        for i in range(nw):
            for k, (px, py) in enumerate(chips):
                win = _shard_window(outs[i], kinds[i], 2 * px + py, (c,))
                pltpu.make_async_remote_copy(
                    src_ref=ins[i].at[c], dst_ref=win, send_sem=ici_send.at[3 * i + k], recv_sem=ici_recv.at[3 * i + k],
                    device_id=(px, py, c), device_id_type=MESH).wait_recv()
                fw = pltpu.make_async_remote_copy(
                    src_ref=win, dst_ref=win, send_sem=d2d_send.at[3 * i + k], recv_sem=d2d_recv.at[3 * i + k],
                    device_id=(x, y, 1 - c), device_id_type=MESH)
                fw.start()
                sends.append((fw, True))
        for i in range(nw):
            for k, (px, py) in enumerate(chips):
                win = _shard_window(outs[i], kinds[i], 2 * px + py, (1 - c,))
                pltpu.make_async_remote_copy(
                    src_ref=win, dst_ref=win, send_sem=d2d_send.at[3 * i + k], recv_sem=d2d_recv.at[3 * i + k],
                    device_id=(x, y, 1 - c), device_id_type=MESH).wait_recv()
        for cp, remote in sends:
            cp.wait_send() if remote else cp.wait()

    outs = pl.pallas_call(
        body, name=name, in_specs=[_ANY] * nw, out_specs=[_ANY] * nw,
        out_shape=[out_shape(s, k) for s, k in zip(shards, kinds)],
        scratch_shapes=[pltpu.SemaphoreType.DMA((3 * nw,)), pltpu.SemaphoreType.DMA((3 * nw,)),
                        pltpu.SemaphoreType.DMA((3 * nw,)), pltpu.SemaphoreType.DMA((3 * nw,)),
                        pltpu.SemaphoreType.DMA((nw,))],
    )(*shards)
    return [jnp.concatenate([o[j] for j in range(N_CHIPS)], axis=2) if k == "slots" else o for o, k in zip(outs, kinds)]


def _swap_other_layer(gst, name):
    nv = len(gst)

    def body(*refs):
        ins, outs, send_sems, recv_sems = refs[:nv], refs[nv:2 * nv], refs[2 * nv], refs[2 * nv + 1]
        x, y, c = lax.axis_index("x"), lax.axis_index("y"), lax.axis_index("c")
        cps = [pltpu.make_async_remote_copy(src_ref=ins[i].at[1 - c], dst_ref=outs[i], send_sem=send_sems.at[i],
                                            recv_sem=recv_sems.at[i], device_id=(x, y, 1 - c), device_id_type=MESH)
               for i in range(nv)]
        for cp in cps:
            cp.start()
        for cp in cps:
            cp.wait()

    return pl.pallas_call(
        body, name=name, in_specs=[_ANY] * nv, out_specs=[_ANY] * nv,
        out_shape=[jax.ShapeDtypeStruct(v.shape[1:], v.dtype) for v in gst],
        scratch_shapes=[pltpu.SemaphoreType.DMA((nv,)), pltpu.SemaphoreType.DMA((nv,))],
    )(*gst)


def _pair_sum(g, other, core, name, tr=256):
    _, rows, cols = g.shape
    tr = _tile(rows, tr, 16)

    def body(c_ref, g_ref, o_ref, s_ref):
        s_ref[...] = (g_ref[...].astype(F32) + o_ref[...].astype(F32)).astype(s_ref.dtype)

    blk = pl.BlockSpec((tr, cols), lambda i, c_ref: (i, 0))
    return pl.pallas_call(
        body, name=name,
        grid_spec=pltpu.PrefetchScalarGridSpec(
            num_scalar_prefetch=1, grid=(rows // tr,),
            in_specs=[pl.BlockSpec((None, tr, cols), lambda i, c_ref: (c_ref[0], i, 0)), blk], out_specs=blk),
        out_shape=jax.ShapeDtypeStruct((rows, cols), g.dtype), compiler_params=_cp(("parallel",)),
    )(core, g, other)


def _scatter_layer(parts, kinds, name):
    nw = len(parts)

    def shard_shape(g, kind):
        return g.shape[-2] // (N_CHIPS if kind == "rows" else 1), g.shape[-1] // (N_CHIPS if kind == "cols" else 1)

    def body(*refs):
        ins, outs = refs[:nw], refs[nw:2 * nw]
        send_sems, recv_sems, local_sems = refs[2 * nw:]
        x, y, c = lax.axis_index("x"), lax.axis_index("y"), lax.axis_index("c")
        me = 2 * x + y
        chips = [(1 - x, y), (x, 1 - y), (1 - x, 1 - y)]
        sends = []
        for i in range(nw):
            own = pltpu.make_async_copy(_shard_window(ins[i], kinds[i], me), outs[i].at[me], local_sems.at[i])
            own.start()
            sends.append((own, False))
            for k, (px, py) in enumerate(chips):
                cp = pltpu.make_async_remote_copy(
                    src_ref=_shard_window(ins[i], kinds[i], 2 * px + py), dst_ref=outs[i].at[me],
                    send_sem=send_sems.at[3 * i + k], recv_sem=recv_sems.at[3 * i + k],
                    device_id=(px, py, c), device_id_type=MESH)
                cp.start()
                sends.append((cp, True))
        for i in range(nw):
            for k, (px, py) in enumerate(chips):
                pltpu.make_async_remote_copy(
                    src_ref=_shard_window(ins[i], kinds[i], me), dst_ref=outs[i].at[2 * px + py],
                    send_sem=send_sems.at[3 * i + k], recv_sem=recv_sems.at[3 * i + k],
                    device_id=(px, py, c), device_id_type=MESH).wait_recv()
        for cp, remote in sends:
            cp.wait_send() if remote else cp.wait()

    return pl.pallas_call(
        body, name=name, in_specs=[_ANY] * nw, out_specs=[_ANY] * nw,
        out_shape=[jax.ShapeDtypeStruct((N_CHIPS,) + shard_shape(g, k), g.dtype) for g, k in zip(parts, kinds)],
        scratch_shapes=[pltpu.SemaphoreType.DMA((3 * nw,)), pltpu.SemaphoreType.DMA((3 * nw,)),
                        pltpu.SemaphoreType.DMA((nw,))],
    )(*parts)


def _sibling_swap(vs, name):
    nv = len(vs)

    def body(*refs):
        ins, outs, send_sems, recv_sems = refs[:nv], refs[nv:2 * nv], refs[2 * nv], refs[2 * nv + 1]
        x, y, c = lax.axis_index("x"), lax.axis_index("y"), lax.axis_index("c")
        cps = [pltpu.make_async_remote_copy(src_ref=ins[i], dst_ref=outs[i], send_sem=send_sems.at[i],
                                            recv_sem=recv_sems.at[i], device_id=(x, y, 1 - c), device_id_type=MESH)
               for i in range(nv)]
        for cp in cps:
            cp.start()
        for cp in cps:
            cp.wait()

    return pl.pallas_call(
        body, name=name, in_specs=[_ANY] * nv, out_specs=[_ANY] * nv,
        out_shape=[jax.ShapeDtypeStruct(v.shape, v.dtype) for v in vs],
        scratch_shapes=[pltpu.SemaphoreType.DMA((nv,)), pltpu.SemaphoreType.DMA((nv,))],
    )(*vs)


def _allgather_all(v, name):
    rows, cols = v.shape

    def body(in_ref, out_ref, send_sems, recv_sems, local_sem):
        x, y, c = lax.axis_index("x"), lax.axis_index("y"), lax.axis_index("c")
        me = 4 * x + 2 * y + c
        peers = []
        for k in range(1, N_DEV):
            peers.append(((1 - x) if k & 4 else x, (1 - y) if k & 2 else y, (1 - c) if k & 1 else c))
        mine = pltpu.make_async_copy(in_ref, out_ref.at[me], local_sem)
        mine.start()
        sends = []
        for k, peer in enumerate(peers):
            cp = pltpu.make_async_remote_copy(src_ref=in_ref, dst_ref=out_ref.at[me], send_sem=send_sems.at[k],
                                              recv_sem=recv_sems.at[k], device_id=peer, device_id_type=MESH)
            cp.start()
            sends.append(cp)
        for k, (px, py, pc) in enumerate(peers):
            pltpu.make_async_remote_copy(src_ref=in_ref, dst_ref=out_ref.at[4 * px + 2 * py + pc],
                                         send_sem=send_sems.at[k], recv_sem=recv_sems.at[k],
                                         device_id=(px, py, pc), device_id_type=MESH).wait_recv()
        for cp in sends:
            cp.wait_send()
        mine.wait()

    return pl.pallas_call(
        body, name=name, in_specs=[_ANY], out_specs=_ANY,
        out_shape=jax.ShapeDtypeStruct((N_DEV, rows, cols), v.dtype),
        scratch_shapes=[pltpu.SemaphoreType.DMA((N_DEV - 1,)), pltpu.SemaphoreType.DMA((N_DEV - 1,)),
                        pltpu.SemaphoreType.DMA(())],
    )(v)


def _pack(arrs, dtype, width, row_mult):
    flat = jnp.concatenate([a.reshape(-1).astype(dtype) for a in arrs])
    unit = width * row_mult
    total = -(-flat.shape[0] // unit) * unit
    return jnp.pad(flat, (0, total - flat.shape[0])).reshape(-1, width)


def _unpack(buf, shapes):
    flat = buf.reshape(-1)
    out, off = [], 0
    for shp in shapes:
        n = 1
        for d in shp:
            n *= d
        out.append(flat[off:off + n].reshape(shp))
        off += n
    return out


class _Dims:
    pass


def _dims(x, ssd_dt_bias, ssd_norm_g, ssd_conv_b, ffn_conv_b):
    dm = _Dims()
    dm.Bl, dm.S, dm.D = x.shape
    dm.H, dm.DI, dm.CD = ssd_dt_bias.shape[-1], ssd_norm_g.shape[-1], ssd_conv_b.shape[-1]
    dm.N = (dm.CD - dm.DI) // (2 * N_GROUPS)
    dm.R = dm.H // N_GROUPS
    dm.DFF = ffn_conv_b.shape[-1] // 2
    dm.OX = dm.DI
    dm.OB = dm.DI + dm.CD
    dm.OC = dm.OB + dm.D
    dm.OH = dm.OC + dm.D
    dm.OG = dm.OH + dm.D
    dm.ODT = dm.OG + 2 * dm.D
    dm.NP = dm.ODT + LANES
    assert dm.DI // dm.H == HEAD_DIM and dm.N == LANES and dm.R % 2 == 0 and dm.S % CHUNK == 0 and dm.H <= LANES
    return dm


def _permute_w_in(w, dm):
    o = dm.DI + dm.CD
    pad = jnp.zeros((w.shape[0], LANES - dm.H), w.dtype)
    return jnp.concatenate([w[:, :o], w[:, o + dm.H:], w[:, o:o + dm.H], pad], axis=1)


def _unpermute_w_in(dw, dm):
    o = dm.DI + dm.CD
    return jnp.concatenate([dw[:, :o], dw[:, dm.ODT:dm.ODT + dm.H], dw[:, o:dm.ODT]], axis=1)


def _lane_pad(v):
    return jnp.pad(v.reshape(1, -1).astype(F32), ((0, 0), (0, LANES - v.shape[-1])))


def _pad8(w):
    return jnp.pad(w.astype(F32), ((0, 8 - w.shape[0]), (0, 0)))


def _head_select(dm):
    j = jnp.arange(LANES)[None, :, None]
    r = jnp.arange(LANES)[None, None, :]
    g = jnp.arange(N_GROUPS)[:, None, None]
    return ((j == dm.R * g + r) & (r < dm.R)).astype(F32)


def _mix_fwd(dm, h, w, sp, sel, tag):
    Bl, S, D = dm.Bl, dm.S, dm.D
    T = Bl * S
    proj = _matmul(h.reshape(T, D), w["w_in_p"], "NN", F32, tag + "_in_proj", tn=1152).reshape(Bl, S, dm.NP)
    xbc_a = _ssd_conv_fwd(proj, sp["ssd_conv_w8"], sp["ssd_conv_b"], dm.OX, dm.CD, tag + "_ssd_conv")
    y, yn, hprev = _ssd_fwd(xbc_a, proj, sp["dt_bias"], sp["a_log"], sp["d_skip"], sp["ssd_norm_g"], sel, dm,
                            tag + "_ssd_scan")
    y_ssd = _matmul(yn.reshape(T, dm.DI), w["w_ssd_out"], "NN", F32, tag + "_ssd_out", tk=2048).reshape(Bl, S, D)
    s = _sc_conv_fwd(proj, sp["sc_conv_w8"], (dm.OB, dm.OC, dm.OH), D, tag + "_sc_conv")
    y_sc = _matmul(s.reshape(T, D), w["w_sc_out"], "NN", F32, tag + "_sc_out", tk=1024).reshape(Bl, S, D)
    mixin = _merge_fwd(proj, y_ssd, y_sc, dm.OG, D, tag + "_merge")
    mix = _matmul(mixin.reshape(T, D), w["w_o"], "NN", F32, tag + "_o", tk=1024).reshape(Bl, S, D)
    return mix, (h, proj, xbc_a, y, yn, hprev, y_ssd, y_sc, s, mixin)


def _mix_bwd(dm, dmix, saved, w, sp, sel, tag):
    Bl, S, D = dm.Bl, dm.S, dm.D
    T = Bl * S
    h, proj, xbc_a, y, yn, hprev, y_ssd, y_sc, s, mixin = saved
    dmix2 = dmix.reshape(T, D)
    g = {}
    g["w_o"] = _matmul(mixin.reshape(T, D), dmix2, "TN", WIRE_DTYPE,tag + "_dw_o", tk=1024)
    dmixin = _matmul(dmix2, w["w_o"], "NT", F32, tag + "_d_o", tk=1024).reshape(Bl, S, D)
    dy_ssd, dy_sc, dg1, dg2 = _merge_bwd(dmixin, proj, y_ssd, y_sc, dm.OG, D, tag + "_merge_bwd")
    g["w_sc_out"] = _matmul(s.reshape(T, D), dy_sc.reshape(T, D), "TN", WIRE_DTYPE,tag + "_dw_sc_out", tk=1024)
    ds = _matmul(dy_sc.reshape(T, D), w["w_sc_out"], "NT", F32, tag + "_d_sc_out", tk=1024).reshape(Bl, S, D)
    dscb, dscc, dsch, dscw = _sc_conv_bwd(ds, proj, sp["sc_conv_w8"], (dm.OB, dm.OC, dm.OH), D, tag + "_sc_conv_bwd")
    g["sc_conv_w"] = dscw[:3]
    g["w_ssd_out"] = _matmul(yn.reshape(T, dm.DI), dy_ssd.reshape(T, D), "TN", WIRE_DTYPE,tag + "_dw_ssd_out", tk=1024)
    dyn = _matmul(dy_ssd.reshape(T, D), w["w_ssd_out"], "NT", F32, tag + "_d_ssd_out", tk=1024).reshape(Bl, S, dm.DI)
    dz, dxs, dbm, dcm, ddtr_g, dpar, dng = _ssd_bwd(dyn, y, xbc_a, proj, hprev, sp["dt_bias"], sp["a_log"],
                                                    sp["d_skip"], sp["ssd_norm_g"], sel, dm, tag + "_ssd_scan_bwd")
    dpar = jnp.sum(dpar, axis=0)
    g["ssd_dt_bias"], g["ssd_a_log"], g["ssd_d"] = dpar[0, :dm.H], dpar[1, :dm.H], dpar[2, :dm.H]
    g["ssd_norm_g"] = dng[0]
    dxa = jnp.concatenate([dxs, dbm, dcm], axis=-1)
    dxbc, dcw, dcb = _ssd_conv_bwd(dxa, proj, sp["ssd_conv_w8"], sp["ssd_conv_b"], dm.OX, dm.CD, tag + "_ssd_conv_bwd")
    g["ssd_conv_w"], g["ssd_conv_b"] = dcw[:4], dcb[0]
    ddt = jnp.sum(ddtr_g, axis=1).astype(MXU_DTYPE)
    dproj = jnp.concatenate([dz, dxbc, dscb, dscc, dsch, dg1, dg2, ddt], axis=-1).reshape(T, dm.NP)
    g["w_in_p"] = _matmul(h.reshape(T, D), dproj, "TN", WIRE_DTYPE,tag + "_dw_in", tn=1152)
    dh = _matmul(dproj, w["w_in_p"], "NT", F32, tag + "_d_in", tk=1152).reshape(Bl, S, D)
    return dh, g


def _ffn_fwd(dm, h, w, sp, tag):
    Bl, S, D = dm.Bl, dm.S, dm.D
    T = Bl * S
    up = _matmul(h.reshape(T, D), w["w_up"], "NN", F32, tag + "_up", tn=1408).reshape(Bl, S, 2 * dm.DFF)
    a = _ffn_conv_fwd(up, sp["ffn_conv_w8"], sp["ffn_conv_b"], dm.DFF, tag + "_ffn_conv")
    f = _matmul(a.reshape(T, dm.DFF), w["w_down"], "NN", F32, tag + "_down", tk=2816).reshape(Bl, S, D)
    return f, (h, up, a)


def _ffn_bwd(dm, df, saved, w, sp, tag):
    Bl, S, D = dm.Bl, dm.S, dm.D
    T = Bl * S
    h, up, a = saved
    df2 = df.reshape(T, D)
    g = {}
    g["w_down"] = _matmul(a.reshape(T, dm.DFF), df2, "TN", WIRE_DTYPE,tag + "_dw_down", tm=1408)
    da = _matmul(df2, w["w_down"], "NT", F32, tag + "_d_down", tn=1408).reshape(Bl, S, dm.DFF)
    dg, dv, dcw, dcb = _ffn_conv_bwd(da, up, sp["ffn_conv_w8"], sp["ffn_conv_b"], dm.DFF, tag + "_ffn_conv_bwd")
    g["ffn_conv_w"], g["ffn_conv_b"] = dcw[:3], dcb[0]
    dup = jnp.concatenate([dg, dv], axis=-1).reshape(T, 2 * dm.DFF)
    g["w_up"] = _matmul(h.reshape(T, D), dup, "TN", WIRE_DTYPE,tag + "_dw_up", tn=1408)
    dh = _matmul(dup, w["w_up"], "NT", F32, tag + "_d_up", tk=1408).reshape(Bl, S, D)
    return dh, g


def _local_step(dm, x, c, target, wfull, small):
    Bl, S, D = dm.Bl, dm.S, dm.D
    depth = len(wfull)
    sel = _head_select(dm)
    c16 = jnp.pad(c.astype(F32), ((0, 16 - Bl), (0, 0)))
    sps, mods, acts = [], [], []
    for l in range(depth):
        sm = small[l]
        sps.append(dict(
            ssd_conv_w8=_pad8(sm["ssd_conv_w"]), ssd_conv_b=sm["ssd_conv_b"], dt_bias=_lane_pad(sm["ssd_dt_bias"]),
            a_log=_lane_pad(sm["ssd_a_log"]), d_skip=_lane_pad(sm["ssd_d"]), ssd_norm_g=sm["ssd_norm_g"],
            sc_conv_w8=_pad8(sm["sc_conv_w"]), ffn_conv_w8=_pad8(sm["ffn_conv_w"]), ffn_conv_b=sm["ffn_conv_b"]))
        act, mod = _ada_fwd(c16, wfull[l]["ada_w"], sm["ada_b"], f"l{l}_ada")
        acts.append(act)
        mods.append(jnp.pad(mod[:Bl].reshape(Bl, 6, D), ((0, 0), (0, 2), (0, 0))))

    def sub(i):
        l, ffn = i // 2, i % 2
        sm = small[l]
        return dict(l=l, ffn=ffn, pre_g=sm["ffn_pre_g" if ffn else "mix_pre_g"],
                    post_g=sm["ffn_post_g" if ffn else "mix_post_g"], mod=mods[l], row=3 * ffn,
                    tag=f"l{l}_{'ffn' if ffn else 'mix'}")

    nsub = 2 * depth
    subs = [sub(i) for i in range(nsub)]
    xs, fs, saves = [x], [], []
    h = _norm_mod_fwd(x, subs[0]["pre_g"], subs[0]["mod"], subs[0]["row"], "l0_mix_pre_norm")
    for i, sb in enumerate(subs):
        l = sb["l"]
        if sb["ffn"]:
            f, sv = _ffn_fwd(dm, h, wfull[l], sps[l], sb["tag"])
        else:
            f, sv = _mix_fwd(dm, h, wfull[l], sps[l], sel, sb["tag"])
        nxt = None
        if i + 1 < nsub:
            nb = subs[i + 1]
            nxt = (nb["pre_g"], nb["mod"], nb["row"])
        xn, h = _post_norm_fwd(xs[-1], f, sb["post_g"], sb["mod"], sb["row"] + 2, nxt, sb["tag"] + "_post_norm")
        xs.append(xn)
        fs.append(f)
        saves.append(sv)

    dy, loss = _loss_fwd_bwd(xs[-1], target, "loss")

    grads = [dict() for _ in range(depth)]
    dmod = [[None] * 6 for _ in range(depth)]
    dx, dh = dy, None
    for i in reversed(range(nsub)):
        sb = subs[i]
        l = sb["l"]
        nxt = None
        if i + 1 < nsub:
            nb = subs[i + 1]
            nxt = (dh, xs[i + 1], nb["pre_g"], nb["mod"], nb["row"])
        dx, df, pb, shg = _norm_bwd(dx, nxt, (fs[i], sb["post_g"], sb["mod"], sb["row"] + 2), sb["tag"] + "_post_norm_bwd")
        if nxt is not None:
            nb = subs[i + 1]
            dmod[nb["l"]][nb["row"]], dmod[nb["l"]][nb["row"] + 1] = pb[:, 0], pb[:, 1]
            grads[nb["l"]]["ffn_pre_g" if nb["ffn"] else "mix_pre_g"] = shg[0]
        dmod[l][sb["row"] + 2] = pb[:, 2]
        grads[l]["ffn_post_g" if sb["ffn"] else "mix_post_g"] = shg[1]
        if sb["ffn"]:
            dh, g = _ffn_bwd(dm, df, saves[i], wfull[l], sps[l], sb["tag"])
        else:
            dh, g = _mix_bwd(dm, df, saves[i], wfull[l], sps[l], sel, sb["tag"])
        grads[l].update(g)
    sb = subs[0]
    grad_x, _, pb, shg = _norm_bwd(dx, (dh, xs[0], sb["pre_g"], sb["mod"], sb["row"]), None, "l0_mix_pre_norm_bwd")
    dmod[0][0], dmod[0][1] = pb[:, 0], pb[:, 1]
    grads[0]["mix_pre_g"] = shg[0]

    for l in range(depth):
        dm6 = jnp.concatenate(dmod[l], axis=-1)
        grads[l]["ada_b"] = jnp.sum(dm6, axis=0)
        dm16 = jnp.pad(dm6, ((0, 16 - Bl), (0, 0))).astype(MXU_DTYPE)
        grads[l]["ada_w"] = _matmul(acts[l], dm16, "TN", WIRE_DTYPE,f"l{l}_dw_ada")
    return loss, grad_x, grads


_WEIGHTS = ("ada_w", "ada_b", "mix_pre_g", "mix_post_g", "w_in", "ssd_conv_w", "ssd_conv_b", "ssd_dt_bias",
            "ssd_a_log", "ssd_d", "ssd_norm_g", "w_ssd_out", "sc_conv_w", "w_sc_out", "w_o", "ffn_pre_g",
            "ffn_post_g", "w_up", "ffn_conv_w", "ffn_conv_b", "w_down")
_INPUTS = ("x", "c") + _WEIGHTS + ("loss_target",) + tuple("m_" + n for n in _WEIGHTS) + tuple("v_" + n for n in _WEIGHTS)
_BIG = {"ada_w": 2, "w_in": 2, "w_ssd_out": 1, "w_sc_out": 1, "w_o": 1, "w_up": 2, "w_down": 1}
_CONV = ("ssd_conv_w", "sc_conv_w", "ffn_conv_w")
_SMALL = tuple(n for n in _WEIGHTS if n not in _BIG)
PACK_COLS, PACK_ROWS = 1024, 256


def _block(a, axis, j, n=N_CHIPS):
    w = a.shape[axis] // n
    return lax.slice_in_dim(a, j * w, (j + 1) * w, axis=axis)


def _step(a):
    x, c, target = a["x"], a["c"], a["loss_target"]
    depth = a["ada_w"].shape[0]
    dm = _dims(x, a["ssd_dt_bias"], a["ssd_norm_g"], a["ssd_conv_b"], a["ffn_conv_b"])
    chip = 2 * lax.axis_index("x") + lax.axis_index("y")

    shards = [a[n].astype(WIRE_DTYPE) for n in _BIG]
    axes = list(_BIG.values())
    kinds = [_gather_kind(s, ax) for s, ax in zip(shards, axes)]
    full = {n: w.astype(MXU_DTYPE) for n, w in zip(_BIG, _gather_weights(shards, axes, "gather_weights"))}
    conv_shapes = [a[n].shape for n in _CONV]
    gotc = _exchange_chips(_pack([a[n] for n in _CONV], F32, LANES, 8), False, "gather_conv_weights")
    piecesc = [_unpack(gotc[j], conv_shapes) for j in range(N_CHIPS)]
    fullc = {n: jnp.concatenate([piecesc[j][i] for j in range(N_CHIPS)], axis=2) for i, n in enumerate(_CONV)}

    wfull, small = [], []
    for l in range(depth):
        wf = {n: full[n][l] for n in _BIG if n != "w_in"}
        wf["w_in_p"] = _permute_w_in(full["w_in"][l], dm)
        wfull.append(wf)
        small.append({n: (fullc[n][l] if n in _CONV else a[n][l]) for n in _SMALL})

    loss_part, grad_x, grads = _local_step(dm, x, c, target, wfull, small)

    core = lax.axis_index("c").astype(jnp.int32).reshape(1)
    gst = []
    for n, kind in zip(_BIG, kinds):
        per_layer = []
        for l in range(depth):
            g = _unpermute_w_in(grads[l]["w_in_p"], dm) if n == "w_in" else grads[l][n]
            if kind == "slots":
                g = jnp.moveaxis(g.reshape(g.shape[0], N_CHIPS, g.shape[1] // N_CHIPS), 1, 0)
                g = g.reshape(-1, g.shape[-1])
            per_layer.append(g.astype(WIRE_DTYPE))
        gst.append(jnp.stack(per_layer))
    other = _swap_other_layer(gst, "swap_layer_grads")
    parts = [_pair_sum(g, o, core, "pair_sum_" + n) for g, o, n in zip(gst, other, _BIG)]
    parts = [p.reshape(N_CHIPS, -1, p.shape[-1]) if k == "slots" else p for p, k in zip(parts, kinds)]
    got = _scatter_layer(parts, kinds, "scatter_grads")
    mine = [_sum_slots(g, "sum_chip_grads_" + n) for g, n in zip(got, _BIG)]
    theirs = _sibling_swap(mine, "swap_core_grads")

    out = {}
    for i, n in enumerate(_BIG):
        out[n] = _adamw_layers(a[n], a["m_" + n], a["v_" + n], mine[i], theirs[i], core, "adamw_" + n)

    gsmall = [jnp.stack([grads[l][n] for l in range(depth)]) for n in _SMALL]
    small_shapes = [g.shape for g in gsmall]
    summed = _sum_slots(_allgather_all(_pack(gsmall, F32, LANES, 8), "gather_small_grads"), "sum_small_grads")
    gs = dict(zip(_SMALL, _unpack(summed, small_shapes)))
    for n in _CONV:
        wcols = a[n].shape[2]
        gs[n] = lax.dynamic_slice_in_dim(gs[n], chip * wcols, wcols, axis=2)
    local_shapes = [a[n].shape for n in _SMALL]
    res = _adamw(_pack([a[n] for n in _SMALL], F32, LANES, 8), _pack([a["m_" + n] for n in _SMALL], F32, LANES, 8),
                 _pack([a["v_" + n] for n in _SMALL], F32, LANES, 8), _pack([gs[n] for n in _SMALL], F32, LANES, 8),
                 "adamw_small")
    res = [_unpack(r, local_shapes) for r in res]
    for i, n in enumerate(_SMALL):
        out[n] = [r[i] for r in res]

    loss = lax.psum(loss_part, ("x", "y", "c"))
    return (loss, grad_x) + tuple(out[n][k] for k in range(4) for n in _WEIGHTS)


def kernel(x, c, ada_w, ada_b, mix_pre_g, mix_post_g, w_in, ssd_conv_w, ssd_conv_b, ssd_dt_bias, ssd_a_log, ssd_d, ssd_norm_g, w_ssd_out, sc_conv_w, w_sc_out, w_o, ffn_pre_g, ffn_post_g, w_up, ffn_conv_w, ffn_conv_b, w_down, loss_target, m_ada_w, m_ada_b, m_mix_pre_g, m_mix_post_g, m_w_in, m_ssd_conv_w, m_ssd_conv_b, m_ssd_dt_bias, m_ssd_a_log, m_ssd_d, m_ssd_norm_g, m_w_ssd_out, m_sc_conv_w, m_w_sc_out, m_w_o, m_ffn_pre_g, m_ffn_post_g, m_w_up, m_ffn_conv_w, m_ffn_conv_b, m_w_down, v_ada_w, v_ada_b, v_mix_pre_g, v_mix_post_g, v_w_in, v_ssd_conv_w, v_ssd_conv_b, v_ssd_dt_bias, v_ssd_a_log, v_ssd_d, v_ssd_norm_g, v_w_ssd_out, v_sc_conv_w, v_w_sc_out, v_w_o, v_ffn_pre_g, v_ffn_post_g, v_w_up, v_ffn_conv_w, v_ffn_conv_b, v_w_down):
    return _step(dict(zip(_INPUTS, (
        x, c, ada_w, ada_b, mix_pre_g, mix_post_g, w_in, ssd_conv_w, ssd_conv_b, ssd_dt_bias, ssd_a_log, ssd_d, ssd_norm_g, w_ssd_out, sc_conv_w, w_sc_out, w_o, ffn_pre_g, ffn_post_g, w_up, ffn_conv_w, ffn_conv_b, w_down, loss_target, m_ada_w, m_ada_b, m_mix_pre_g, m_mix_post_g, m_w_in, m_ssd_conv_w, m_ssd_conv_b, m_ssd_dt_bias, m_ssd_a_log, m_ssd_d, m_ssd_norm_g, m_w_ssd_out, m_sc_conv_w, m_w_sc_out, m_w_o, m_ffn_pre_g, m_ffn_post_g, m_w_up, m_ffn_conv_w, m_ffn_conv_b, m_w_down, v_ada_w, v_ada_b, v_mix_pre_g, v_mix_post_g, v_w_in, v_ssd_conv_w, v_ssd_conv_b, v_ssd_dt_bias, v_ssd_a_log, v_ssd_d, v_ssd_norm_g, v_w_ssd_out, v_sc_conv_w, v_w_sc_out, v_w_o, v_ffn_pre_g, v_ffn_post_g, v_w_up, v_ffn_conv_w, v_ffn_conv_b, v_w_down))))
```

```python
import math

import jax
import jax.numpy as jnp
from jax import lax
from jax.experimental import pallas as pl
from jax.experimental.pallas import tpu as pltpu

F32 = jnp.float32
MXU_DTYPE = jnp.bfloat16
WIRE_DTYPE = jnp.bfloat16
HI = lax.Precision.HIGHEST
EPS = 1e-6
N_GROUPS = 4
CHUNK = 128
HEAD_DIM = 64
LANES = 128
HALO = 8
N_CHIPS = 4
N_DEV = 8
VMEM_LIMIT = 56 * 1024 * 1024
ADAM_LR, ADAM_B1, ADAM_B2, ADAM_EPS, ADAM_WD, ADAM_STEP = 0.001, 0.9, 0.999, 1e-08, 0.01, 10
MESH = pl.DeviceIdType.MESH

NN = (((1,), (0,)), ((), ()))
NT = (((1,), (1,)), ((), ()))
TN = (((0,), (0,)), ((), ()))


def _dg(a, b, dn, precision=None):
    return lax.dot_general(a, b, dn, precision=precision, preferred_element_type=F32)


def _tile(dim, pref, mult=LANES):
    t = (min(pref, dim) // mult) * mult
    while t >= mult:
        if dim % t == 0:
            return t
        t -= mult
    return dim


def _cp(sem):
    return pltpu.CompilerParams(dimension_semantics=sem, vmem_limit_bytes=VMEM_LIMIT)


def _sigmoid(x):
    return 1.0 / (1.0 + jnp.exp(-x))


def _softplus(x):
    return jnp.maximum(x, 0.0) + jnp.log1p(jnp.exp(-jnp.abs(x)))


def _matmul(a, b, mode, out_dtype, name, tm=1024, tn=1024, tk=1024):
    if mode == "NN":
        (M, K), N = a.shape, b.shape[1]
    elif mode == "NT":
        (M, K), N = a.shape, b.shape[0]
    else:
        (K, M), N = a.shape, b.shape[1]
    tm, tn, tk = _tile(M, tm), _tile(N, tn), _tile(K, tk)
    nk = K // tk
    dn = {"NN": NN, "NT": NT, "TN": TN}[mode]

    def body_one(a_ref, b_ref, o_ref):
        o_ref[...] = _dg(a_ref[...], b_ref[...], dn).astype(o_ref.dtype)

    def body_acc(a_ref, b_ref, o_ref, acc_ref):
        k = pl.program_id(2)

        @pl.when(k == 0)
        def _():
            acc_ref[...] = jnp.zeros_like(acc_ref)

        acc_ref[...] += _dg(a_ref[...], b_ref[...], dn)

        @pl.when(k == nk - 1)
        def _():
            o_ref[...] = acc_ref[...].astype(o_ref.dtype)

    a_spec = (pl.BlockSpec((tk, tm), lambda i, j, k: (k, i)) if mode == "TN"
              else pl.BlockSpec((tm, tk), lambda i, j, k: (i, k)))
    b_spec = (pl.BlockSpec((tn, tk), lambda i, j, k: (j, k)) if mode == "NT"
              else pl.BlockSpec((tk, tn), lambda i, j, k: (k, j)))
    return pl.pallas_call(
        body_one if nk == 1 else body_acc, name=name, grid=(M // tm, N // tn, nk),
        in_specs=[a_spec, b_spec],
        out_specs=pl.BlockSpec((tm, tn), lambda i, j, k: (i, j)),
        out_shape=jax.ShapeDtypeStruct((M, N), out_dtype),
        scratch_shapes=[] if nk == 1 else [pltpu.VMEM((tm, tn), F32)],
        compiler_params=_cp(("parallel", "parallel", "arbitrary")),
    )(a, b)


def _ada_fwd(c16, ada_w, ada_b, name):
    rows, D = c16.shape
    N6 = ada_w.shape[1]
    tn = _tile(N6, 1536)

    def body(c_ref, w_ref, b_ref, act_ref, mod_ref):
        c = c_ref[...]
        act = (c * _sigmoid(c)).astype(act_ref.dtype)
        act_ref[...] = act
        mod_ref[...] = _dg(act, w_ref[...], NN) + b_ref[...]

    return pl.pallas_call(
        body, name=name, grid=(N6 // tn,),
        in_specs=[pl.BlockSpec((rows, D), lambda j: (0, 0)),
                  pl.BlockSpec((D, tn), lambda j: (0, j)),
                  pl.BlockSpec((1, tn), lambda j: (0, j))],
        out_specs=[pl.BlockSpec((rows, D), lambda j: (0, 0)),
                   pl.BlockSpec((rows, tn), lambda j: (0, j))],
        out_shape=[jax.ShapeDtypeStruct((rows, D), MXU_DTYPE),
                   jax.ShapeDtypeStruct((rows, N6), F32)],
        compiler_params=_cp(("arbitrary",)),
    )(c16, ada_w, ada_b.reshape(1, N6))


def _norm_mod_rows(x, g, sc, sh):
    r = lax.rsqrt(jnp.mean(x * x, axis=-1, keepdims=True) + EPS)
    return ((x * r) * g) * (1.0 + sc) + sh


def _norm_mod_fwd(x, g, mod, row_sh, name, ts=512):
    Bl, S, D = x.shape
    ts = _tile(S, ts, 8)

    def body(x_ref, g_ref, mod_ref, h_ref):
        sh = mod_ref[row_sh:row_sh + 1, :]
        sc = mod_ref[row_sh + 1:row_sh + 2, :]
        h_ref[...] = _norm_mod_rows(x_ref[...], g_ref[...], sc, sh).astype(h_ref.dtype)

    tok = pl.BlockSpec((None, ts, D), lambda b, s: (b, s, 0))
    return pl.pallas_call(
        body, name=name, grid=(Bl, S // ts),
        in_specs=[tok, pl.BlockSpec((1, D), lambda b, s: (0, 0)),
                  pl.BlockSpec((None, 8, D), lambda b, s: (b, 0, 0))],
        out_specs=tok, out_shape=jax.ShapeDtypeStruct((Bl, S, D), MXU_DTYPE),
        compiler_params=_cp(("parallel", "parallel")),
    )(x, g.reshape(1, D), mod)


def _post_norm_fwd(xp, f, post_g, mod, row_gt, nxt, name, ts=512):
    Bl, S, D = xp.shape
    ts = _tile(S, ts, 8)
    has_next = nxt is not None

    def body(*refs):
        if has_next:
            xp_ref, f_ref, pg_ref, mod_ref, ng_ref, nmod_ref, x_ref, h_ref = refs
        else:
            xp_ref, f_ref, pg_ref, mod_ref, x_ref = refs
        f = f_ref[...]
        r = lax.rsqrt(jnp.mean(f * f, axis=-1, keepdims=True) + EPS)
        x = xp_ref[...] + mod_ref[row_gt:row_gt + 1, :] * ((f * r) * pg_ref[...])
        x_ref[...] = x
        if has_next:
            rs = nxt[2]
            h_ref[...] = _norm_mod_rows(x, ng_ref[...], nmod_ref[rs + 1:rs + 2, :], nmod_ref[rs:rs + 1, :]).astype(h_ref.dtype)

    tok = pl.BlockSpec((None, ts, D), lambda b, s: (b, s, 0))
    vec = pl.BlockSpec((1, D), lambda b, s: (0, 0))
    modspec = pl.BlockSpec((None, 8, D), lambda b, s: (b, 0, 0))
    ins = [xp, f, post_g.reshape(1, D), mod]
    in_specs = [tok, tok, vec, modspec]
    out_specs = [tok]
    out_shape = [jax.ShapeDtypeStruct((Bl, S, D), F32)]
    if has_next:
        ins += [nxt[0].reshape(1, D), nxt[1]]
        in_specs += [vec, modspec]
        out_specs += [tok]
        out_shape += [jax.ShapeDtypeStruct((Bl, S, D), MXU_DTYPE)]
    out = pl.pallas_call(
        body, name=name, grid=(Bl, S // ts), in_specs=in_specs, out_specs=out_specs, out_shape=out_shape,
        compiler_params=_cp(("parallel", "parallel")),
    )(*ins)
    return (out[0], out[1]) if has_next else (out[0], None)


def _loss_fwd_bwd(y, target, name, ts=512):
    Bl, S, D = y.shape
    ts = _tile(S, ts, 8)

    def body(y_ref, t_ref, dy_ref, l_ref):
        @pl.when((pl.program_id(0) == 0) & (pl.program_id(1) == 0))
        def _():
            l_ref[...] = jnp.zeros_like(l_ref)

        e = y_ref[...] - t_ref[...]
        dy_ref[...] = e * (1.0 / D)
        l_ref[...] += 0.5 * jnp.sum(jnp.mean(e * e, axis=-1, keepdims=True), axis=0, keepdims=True)

    tok = pl.BlockSpec((None, ts, D), lambda b, s: (b, s, 0))
    dy, l = pl.pallas_call(
        body, name=name, grid=(Bl, S // ts), in_specs=[tok, tok],
        out_specs=[tok, pl.BlockSpec((8, LANES), lambda b, s: (0, 0))],
        out_shape=[jax.ShapeDtypeStruct((Bl, S, D), F32), jax.ShapeDtypeStruct((8, LANES), F32)],
        compiler_params=_cp(("arbitrary", "arbitrary")),
    )(y, target)
    return dy, l[0, 0]


def _norm_bwd(dx_res, nxt, prv, name, ts=256):
    Bl, S, D = dx_res.shape
    ts = _tile(S, ts, 8)
    has_next, has_prev = nxt is not None, prv is not None

    def body(*refs):
        refs = list(refs)
        dxr_ref = refs.pop(0)
        if has_next:
            dh_ref, x_ref, g_ref, nmod_ref = refs[:4]
            refs = refs[4:]
        if has_prev:
            f_ref, pg_ref, pmod_ref = refs[:3]
            refs = refs[3:]
        dx_ref = refs.pop(0)
        if has_prev:
            df_ref = refs.pop(0)
        pb_ref, sh_ref = refs
        b, s = pl.program_id(0), pl.program_id(1)

        @pl.when(s == 0)
        def _():
            pb_ref[...] = jnp.zeros_like(pb_ref)

        @pl.when((b == 0) & (s == 0))
        def _():
            sh_ref[...] = jnp.zeros_like(sh_ref)

        dx = dxr_ref[...]
        if has_next:
            rs = nxt[4]
            x, dh, g = x_ref[...], dh_ref[...], g_ref[...]
            sc1 = 1.0 + nmod_ref[rs + 1:rs + 2, :]
            r = lax.rsqrt(jnp.mean(x * x, axis=-1, keepdims=True) + EPS)
            xn = x * r
            pb_ref[0:1, :] += jnp.sum(dh, axis=0, keepdims=True)
            pb_ref[1:2, :] += jnp.sum(dh * (xn * g), axis=0, keepdims=True)
            sh_ref[0:1, :] += jnp.sum(dh * sc1 * xn, axis=0, keepdims=True)
            dxn = dh * sc1 * g
            dx = dx + r * (dxn - xn * jnp.mean(dxn * xn, axis=-1, keepdims=True))
        dx_ref[...] = dx
        if has_prev:
            rg = prv[3]
            f, pg = f_ref[...], pg_ref[...]
            gt = pmod_ref[rg:rg + 1, :]
            r = lax.rsqrt(jnp.mean(f * f, axis=-1, keepdims=True) + EPS)
            fn = f * r
            pb_ref[2:3, :] += jnp.sum(dx * (fn * pg), axis=0, keepdims=True)
            drn = dx * gt
            sh_ref[1:2, :] += jnp.sum(drn * fn, axis=0, keepdims=True)
            dfn = drn * pg
            df_ref[...] = (r * (dfn - fn * jnp.mean(dfn * fn, axis=-1, keepdims=True))).astype(df_ref.dtype)

    tok = pl.BlockSpec((None, ts, D), lambda b, s: (b, s, 0))
    vec = pl.BlockSpec((1, D), lambda b, s: (0, 0))
    modspec = pl.BlockSpec((None, 8, D), lambda b, s: (b, 0, 0))
    ins, in_specs = [dx_res], [tok]
    if has_next:
        ins += [nxt[0], nxt[1], nxt[2].reshape(1, D), nxt[3]]
        in_specs += [tok, tok, vec, modspec]
    if has_prev:
        ins += [prv[0], prv[1].reshape(1, D), prv[2]]
        in_specs += [tok, vec, modspec]
    out_specs, out_shape = [tok], [jax.ShapeDtypeStruct((Bl, S, D), F32)]
    if has_prev:
        out_specs += [tok]
        out_shape += [jax.ShapeDtypeStruct((Bl, S, D), MXU_DTYPE)]
    out_specs += [modspec, pl.BlockSpec((8, D), lambda b, s: (0, 0))]
    out_shape += [jax.ShapeDtypeStruct((Bl, 8, D), F32), jax.ShapeDtypeStruct((8, D), F32)]
    out = pl.pallas_call(
        body, name=name, grid=(Bl, S // ts), in_specs=in_specs, out_specs=out_specs, out_shape=out_shape,
        compiler_params=_cp(("arbitrary", "arbitrary")),
    )(*ins)
    if has_prev:
        return out[0], out[1], out[2], out[3]
    return out[0], None, out[1], out[2]


def _shift_down(x, j):
    return x if j == 0 else pltpu.roll(x, j, axis=0)


def _shift_up(x, j):
    return x if j == 0 else pltpu.roll(x, x.shape[0] - j, axis=0)


def _conv(xall, w_ref, K):
    y = w_ref[K - 1:K, :] * xall
    for k in range(K - 1):
        y = y + w_ref[k:k + 1, :] * _shift_down(xall, K - 1 - k)
    return y


def _conv_t(dall, w_ref, K):
    y = w_ref[K - 1:K, :] * dall
    for k in range(K - 1):
        y = y + w_ref[k:k + 1, :] * _shift_up(dall, K - 1 - k)
    return y


def _conv_wgrad(acc_ref, row0, dtile, xall, K, ts):
    for k in range(K):
        xs = _shift_down(xall, K - 1 - k)[HALO:HALO + ts]
        acc_ref[row0 + k:row0 + k + 1, :] += jnp.sum(dtile * xs, axis=0, keepdims=True)


def _halo_specs(ts, W, nS, colblk):
    per = ts // HALO
    tile = pl.BlockSpec((None, ts, W), lambda b, c, s: (b, s, colblk(c)))
    prev = pl.BlockSpec((None, HALO, W), lambda b, c, s: (b, jnp.maximum(s * per - 1, 0), colblk(c)))
    nxt = pl.BlockSpec((None, HALO, W), lambda b, c, s: (b, jnp.minimum((s + 1) * per, nS * per - 1), colblk(c)))
    return tile, prev, nxt


def _masked(ref, keep):
    v = ref[...]
    return jnp.where(keep, v, jnp.zeros_like(v))


def _ssd_conv_fwd(proj, w8, b, off, CD, name, ts=512, W=1024):
    Bl, S, _ = proj.shape
    K = 4
    ts, W = _tile(S, ts, 8), _tile(math.gcd(CD, off), W)
    assert off % W == 0
    nS, nW, ob = S // ts, CD // W, off // W

    def body(x_ref, xp_ref, w_ref, b_ref, o_ref):
        s = pl.program_id(2)
        xall = jnp.concatenate([_masked(xp_ref, s > 0), x_ref[...]], axis=0)
        xc = _conv(xall, w_ref, K)[HALO:] + b_ref[...]
        o_ref[...] = xc * _sigmoid(xc)

    tile, prev, _ = _halo_specs(ts, W, nS, lambda c: ob + c)
    return pl.pallas_call(
        body, name=name, grid=(Bl, nW, nS),
        in_specs=[tile, prev, pl.BlockSpec((8, W), lambda b_, c, s: (0, c)), pl.BlockSpec((1, W), lambda b_, c, s: (0, c))],
        out_specs=pl.BlockSpec((None, ts, W), lambda b_, c, s: (b_, s, c)),
        out_shape=jax.ShapeDtypeStruct((Bl, S, CD), F32),
        compiler_params=_cp(("parallel", "parallel", "parallel")),
    )(proj, proj, w8, b.reshape(1, CD))


def _ssd_conv_bwd(dxa, proj, w8, b, off, CD, name, ts=256, W=1024):
    Bl, S, _ = proj.shape
    K = 4
    ts, W = _tile(S, ts, 8), _tile(math.gcd(CD, off), W)
    nS, nW, ob = S // ts, CD // W, off // W

    def body(d_ref, dn_ref, x_ref, xp_ref, xn_ref, w_ref, b_ref, dx_ref, dw_ref, db_ref):
        bb, s = pl.program_id(1), pl.program_id(2)

        @pl.when((bb == 0) & (s == 0))
        def _():
            dw_ref[...] = jnp.zeros_like(dw_ref)
            db_ref[...] = jnp.zeros_like(db_ref)

        last = s == nS - 1
        xall = jnp.concatenate([_masked(xp_ref, s > 0), x_ref[...], _masked(xn_ref, ~last)], axis=0)
        dall = jnp.concatenate([jnp.zeros((HALO, W), F32), d_ref[...], _masked(dn_ref, ~last)], axis=0)
        xc = _conv(xall, w_ref, K) + b_ref[...]
        sg = _sigmoid(xc)
        dxc = dall * (sg * (1.0 + xc * (1.0 - sg)))
        dx_ref[...] = _conv_t(dxc, w_ref, K)[HALO:HALO + ts].astype(dx_ref.dtype)
        dtile = dxc[HALO:HALO + ts]
        db_ref[0:1, :] += jnp.sum(dtile, axis=0, keepdims=True)
        _conv_wgrad(dw_ref, 0, dtile, xall, K, ts)

    per = ts // HALO
    dtile_s = pl.BlockSpec((None, ts, W), lambda c, b_, s: (b_, s, c))
    dnext_s = pl.BlockSpec((None, HALO, W), lambda c, b_, s: (b_, jnp.minimum((s + 1) * per, nS * per - 1), c))
    xtile_s = pl.BlockSpec((None, ts, W), lambda c, b_, s: (b_, s, ob + c))
    xprev_s = pl.BlockSpec((None, HALO, W), lambda c, b_, s: (b_, jnp.maximum(s * per - 1, 0), ob + c))
    xnext_s = pl.BlockSpec((None, HALO, W), lambda c, b_, s: (b_, jnp.minimum((s + 1) * per, nS * per - 1), ob + c))
    wspec = pl.BlockSpec((8, W), lambda c, b_, s: (0, c))
    return pl.pallas_call(
        body, name=name, grid=(nW, Bl, nS),
        in_specs=[dtile_s, dnext_s, xtile_s, xprev_s, xnext_s, wspec, pl.BlockSpec((1, W), lambda c, b_, s: (0, c))],
        out_specs=[dtile_s, wspec, wspec],
        out_shape=[jax.ShapeDtypeStruct((Bl, S, CD), MXU_DTYPE), jax.ShapeDtypeStruct((8, CD), F32),
                   jax.ShapeDtypeStruct((8, CD), F32)],
        compiler_params=_cp(("arbitrary", "arbitrary", "arbitrary")),
    )(dxa, dxa, proj, proj, proj, w8, b.reshape(1, CD))


def _sc_conv_fwd(proj, w8, offs, D, name, ts=512, W=1024):
    Bl, S, _ = proj.shape
    K = 3
    ts, W = _tile(S, ts, 8), _tile(D, W)
    nS, nW = S // ts, D // W
    ob, oc, oh = [o // W for o in offs]

    def body(b_ref, c_ref, cp_ref, h_ref, hp_ref, w_ref, o_ref):
        s = pl.program_id(2)
        keep = s > 0
        vall = jnp.concatenate([_masked(cp_ref, keep) * _masked(hp_ref, keep), c_ref[...] * h_ref[...]], axis=0)
        o_ref[...] = (b_ref[...] * _conv(vall, w_ref, K)[HALO:]).astype(o_ref.dtype)

    tb, _, _ = _halo_specs(ts, W, nS, lambda c: ob + c)
    tc, pc, _ = _halo_specs(ts, W, nS, lambda c: oc + c)
    th, ph, _ = _halo_specs(ts, W, nS, lambda c: oh + c)
    return pl.pallas_call(
        body, name=name, grid=(Bl, nW, nS),
        in_specs=[tb, tc, pc, th, ph, pl.BlockSpec((8, W), lambda b_, c, s: (0, c))],
        out_specs=pl.BlockSpec((None, ts, W), lambda b_, c, s: (b_, s, c)),
        out_shape=jax.ShapeDtypeStruct((Bl, S, D), MXU_DTYPE),
        compiler_params=_cp(("parallel", "parallel", "parallel")),
    )(proj, proj, proj, proj, proj, w8)


def _sc_conv_bwd(ds, proj, w8, offs, D, name, ts=256, W=1024):
    Bl, S, _ = proj.shape
    K = 3
    ts, W = _tile(S, ts, 8), _tile(D, W)
    nS, nW = S // ts, D // W
    ob, oc, oh = [o // W for o in offs]

    def body(d_ref, dn_ref, b_ref, bn_ref, c_ref, cp_ref, cn_ref, h_ref, hp_ref, hn_ref, w_ref,
             db_ref, dc_ref, dh_ref, dw_ref):
        bb, s = pl.program_id(1), pl.program_id(2)

        @pl.when((bb == 0) & (s == 0))
        def _():
            dw_ref[...] = jnp.zeros_like(dw_ref)

        first, last = s > 0, s < nS - 1
        zeros = jnp.zeros((HALO, W), F32)
        c_t, h_t = c_ref[...], h_ref[...]
        vall = jnp.concatenate([_masked(cp_ref, first) * _masked(hp_ref, first), c_t * h_t,
                                _masked(cn_ref, last) * _masked(hn_ref, last)], axis=0)
        dcv = jnp.concatenate([zeros, d_ref[...] * b_ref[...], _masked(dn_ref, last) * _masked(bn_ref, last)], axis=0)
        cv = _conv(vall, w_ref, K)[HALO:HALO + ts]
        db_ref[...] = (d_ref[...] * cv).astype(db_ref.dtype)
        dv = _conv_t(dcv, w_ref, K)[HALO:HALO + ts]
        dc_ref[...] = (dv * h_t).astype(dc_ref.dtype)
        dh_ref[...] = (dv * c_t).astype(dh_ref.dtype)
        _conv_wgrad(dw_ref, 0, dcv[HALO:HALO + ts], vall, K, ts)

    per = ts // HALO

    def specs(o):
        t = pl.BlockSpec((None, ts, W), lambda c, b_, s: (b_, s, o + c))
        p = pl.BlockSpec((None, HALO, W), lambda c, b_, s: (b_, jnp.maximum(s * per - 1, 0), o + c))
        n = pl.BlockSpec((None, HALO, W), lambda c, b_, s: (b_, jnp.minimum((s + 1) * per, nS * per - 1), o + c))
        return t, p, n

    dt_, _, dn_ = specs(0)
    bt, _, bn = specs(ob)
    ct, cp, cn = specs(oc)
    ht, hp, hn = specs(oh)
    wspec = pl.BlockSpec((8, W), lambda c, b_, s: (0, c))
    act = jax.ShapeDtypeStruct((Bl, S, D), MXU_DTYPE)
    return pl.pallas_call(
        body, name=name, grid=(nW, Bl, nS),
        in_specs=[dt_, dn_, bt, bn, ct, cp, cn, ht, hp, hn, wspec],
        out_specs=[dt_, dt_, dt_, wspec],
        out_shape=[act, act, act, jax.ShapeDtypeStruct((8, D), F32)],
        compiler_params=_cp(("arbitrary", "arbitrary", "arbitrary")),
    )(ds, ds, proj, proj, proj, proj, proj, proj, proj, proj, w8)


def _ffn_conv_fwd(up, w8, b, DFF, name, ts=512, W=1408):
    Bl, S, _ = up.shape
    K = 3
    ts, W = _tile(S, ts, 8), _tile(DFF, W)
    nS, nW = S // ts, DFF // W

    def body(g_ref, gp_ref, v_ref, vp_ref, wg_ref, wv_ref, bg_ref, bv_ref, o_ref):
        keep = pl.program_id(2) > 0
        ug = _conv(jnp.concatenate([_masked(gp_ref, keep), g_ref[...]], axis=0), wg_ref, K)[HALO:] + bg_ref[...]
        uv = _conv(jnp.concatenate([_masked(vp_ref, keep), v_ref[...]], axis=0), wv_ref, K)[HALO:] + bv_ref[...]
        o_ref[...] = (ug * _sigmoid(ug) * uv).astype(o_ref.dtype)

    tg, pg, _ = _halo_specs(ts, W, nS, lambda c: c)
    tv, pv, _ = _halo_specs(ts, W, nS, lambda c: nW + c)
    wg = pl.BlockSpec((8, W), lambda b_, c, s: (0, c))
    wv = pl.BlockSpec((8, W), lambda b_, c, s: (0, nW + c))
    bg = pl.BlockSpec((1, W), lambda b_, c, s: (0, c))
    bv = pl.BlockSpec((1, W), lambda b_, c, s: (0, nW + c))
    b2 = b.reshape(1, 2 * DFF)
    return pl.pallas_call(
        body, name=name, grid=(Bl, nW, nS),
        in_specs=[tg, pg, tv, pv, wg, wv, bg, bv],
        out_specs=pl.BlockSpec((None, ts, W), lambda b_, c, s: (b_, s, c)),
        out_shape=jax.ShapeDtypeStruct((Bl, S, DFF), MXU_DTYPE),
        compiler_params=_cp(("parallel", "parallel", "parallel")),
    )(up, up, up, up, w8, w8, b2, b2)


def _ffn_conv_bwd(da, up, w8, b, DFF, name, ts=256, W=1408):
    Bl, S, _ = up.shape
    K = 3
    ts, W = _tile(S, ts, 8), _tile(DFF, W)
    nS, nW = S // ts, DFF // W

    def body(d_ref, dn_ref, g_ref, gp_ref, gn_ref, v_ref, vp_ref, vn_ref, wg_ref, wv_ref, bg_ref, bv_ref,
             dg_ref, dv_ref, dwg_ref, dwv_ref, dbg_ref, dbv_ref):
        bb, s = pl.program_id(1), pl.program_id(2)

        @pl.when((bb == 0) & (s == 0))
        def _():
            for r in (dwg_ref, dwv_ref, dbg_ref, dbv_ref):
                r[...] = jnp.zeros_like(r)

        first, last = s > 0, s < nS - 1
        gall = jnp.concatenate([_masked(gp_ref, first), g_ref[...], _masked(gn_ref, last)], axis=0)
        vall = jnp.concatenate([_masked(vp_ref, first), v_ref[...], _masked(vn_ref, last)], axis=0)
        dall = jnp.concatenate([jnp.zeros((HALO, W), F32), d_ref[...], _masked(dn_ref, last)], axis=0)
        ug = _conv(gall, wg_ref, K) + bg_ref[...]
        uv = _conv(vall, wv_ref, K) + bv_ref[...]
        sg = _sigmoid(ug)
        dug = dall * uv * (sg * (1.0 + ug * (1.0 - sg)))
        duv = dall * (ug * sg)
        dg_ref[...] = _conv_t(dug, wg_ref, K)[HALO:HALO + ts].astype(dg_ref.dtype)
        dv_ref[...] = _conv_t(duv, wv_ref, K)[HALO:HALO + ts].astype(dv_ref.dtype)
        dug_t, duv_t = dug[HALO:HALO + ts], duv[HALO:HALO + ts]
        dbg_ref[0:1, :] += jnp.sum(dug_t, axis=0, keepdims=True)
        dbv_ref[0:1, :] += jnp.sum(duv_t, axis=0, keepdims=True)
        _conv_wgrad(dwg_ref, 0, dug_t, gall, K, ts)
        _conv_wgrad(dwv_ref, 0, duv_t, vall, K, ts)

    per = ts // HALO

    def specs(o):
        t = pl.BlockSpec((None, ts, W), lambda c, b_, s: (b_, s, o + c))
        p = pl.BlockSpec((None, HALO, W), lambda c, b_, s: (b_, jnp.maximum(s * per - 1, 0), o + c))
        n = pl.BlockSpec((None, HALO, W), lambda c, b_, s: (b_, jnp.minimum((s + 1) * per, nS * per - 1), o + c))
        return t, p, n

    dt_, _, dn_ = specs(0)
    gt, gp, gn = specs(0)
    vt, vp, vn = specs(nW)
    wg = pl.BlockSpec((8, W), lambda c, b_, s: (0, c))
    wv = pl.BlockSpec((8, W), lambda c, b_, s: (0, nW + c))
    bg = pl.BlockSpec((1, W), lambda c, b_, s: (0, c))
    bv = pl.BlockSpec((1, W), lambda c, b_, s: (0, nW + c))
    b2 = b.reshape(1, 2 * DFF)
    act = jax.ShapeDtypeStruct((Bl, S, DFF), MXU_DTYPE)
    small = jax.ShapeDtypeStruct((8, DFF), F32)
    dg, dv, dwg, dwv, dbg, dbv = pl.pallas_call(
        body, name=name, grid=(nW, Bl, nS),
        in_specs=[dt_, dn_, gt, gp, gn, vt, vp, vn, wg, wv, bg, bv],
        out_specs=[dt_, dt_, wg, wg, wg, wg],
        out_shape=[act, act, small, small, small, small],
        compiler_params=_cp(("arbitrary", "arbitrary", "arbitrary")),
    )(da, da, up, up, up, up, up, up, w8, w8, b2, b2)
    return dg, dv, jnp.concatenate([dwg, dwv], axis=1), jnp.concatenate([dbg, dbv], axis=1)


def _merge_fwd(proj, y_ssd, y_sc, off, D, name, ts=512):
    Bl, S, _ = proj.shape
    ts = _tile(S, ts, 8)
    og = off // D

    def body(g1_ref, g2_ref, a_ref, b_ref, o_ref):
        o_ref[...] = (_sigmoid(g1_ref[...]) * a_ref[...] + _sigmoid(g2_ref[...]) * b_ref[...]).astype(o_ref.dtype)

    tok = pl.BlockSpec((None, ts, D), lambda b, s: (b, s, 0))
    return pl.pallas_call(
        body, name=name, grid=(Bl, S // ts),
        in_specs=[pl.BlockSpec((None, ts, D), lambda b, s: (b, s, og)),
                  pl.BlockSpec((None, ts, D), lambda b, s: (b, s, og + 1)), tok, tok],
        out_specs=tok, out_shape=jax.ShapeDtypeStruct((Bl, S, D), MXU_DTYPE),
        compiler_params=_cp(("parallel", "parallel")),
    )(proj, proj, y_ssd, y_sc)


def _merge_bwd(dmixin, proj, y_ssd, y_sc, off, D, name, ts=512):
    Bl, S, _ = proj.shape
    ts = _tile(S, ts, 8)
    og = off // D

    def body(d_ref, g1_ref, g2_ref, a_ref, b_ref, da_ref, db_ref, dg1_ref, dg2_ref):
        d = d_ref[...]
        s1, s2 = _sigmoid(g1_ref[...]), _sigmoid(g2_ref[...])
        da_ref[...] = (d * s1).astype(da_ref.dtype)
        db_ref[...] = (d * s2).astype(db_ref.dtype)
        dg1_ref[...] = (d * a_ref[...] * (s1 * (1.0 - s1))).astype(dg1_ref.dtype)
        dg2_ref[...] = (d * b_ref[...] * (s2 * (1.0 - s2))).astype(dg2_ref.dtype)

    tok = pl.BlockSpec((None, ts, D), lambda b, s: (b, s, 0))
    act = jax.ShapeDtypeStruct((Bl, S, D), MXU_DTYPE)
    return pl.pallas_call(
        body, name=name, grid=(Bl, S // ts),
        in_specs=[tok, pl.BlockSpec((None, ts, D), lambda b, s: (b, s, og)),
                  pl.BlockSpec((None, ts, D), lambda b, s: (b, s, og + 1)), tok, tok],
        out_specs=[tok, tok, tok, tok], out_shape=[act, act, act, act],
        compiler_params=_cp(("parallel", "parallel")),
    )(dmixin, proj, proj, y_ssd, y_sc)


def _expand_cols(mat, R, lane):
    half = lane < HEAD_DIM
    return jnp.concatenate(
        [jnp.where(half, mat[:, 2 * q:2 * q + 1], mat[:, 2 * q + 1:2 * q + 2]) for q in range(R // 2)], axis=1)


def _head_rows(colvec, R, N):
    return jnp.concatenate([jnp.broadcast_to(colvec[r:r + 1, :], (HEAD_DIM, N)) for r in range(R)], axis=0)


def _ssd_prep(proj, dt_bias, a_log, selcat, selbig, dm, name):
    Bl, S, L, G = dm.Bl, dm.S, CHUNK, N_GROUPS
    nc = S // L
    odt = dm.ODT // LANES

    def body(dtr_ref, dtb_ref, alog_ref, selcat_ref, selbig_ref, dtg_ref, acs_ref, acst_ref):
        row = lax.broadcasted_iota(jnp.int32, (L, L), 0)
        col = lax.broadcasted_iota(jnp.int32, (L, L), 1)
        dt_all = _softplus(dtr_ref[...] + dtb_ref[...])
        acs_all = _dg((row >= col).astype(F32), dt_all * (-jnp.exp(alog_ref[...])), NN, HI)
        dtg_ref[...] = _dg(dt_all, selcat_ref[...], NN, HI)
        acs_ref[...] = _dg(acs_all, selcat_ref[...], NN, HI)
        acst_ref[...] = _dg(selbig_ref[...], acs_all, NT, HI)

    vec = pl.BlockSpec((1, LANES), lambda b, c: (0, 0))
    wide = pl.BlockSpec((None, L, G * LANES), lambda b, c: (b, c, 0))
    return pl.pallas_call(
        body, name=name, grid=(Bl, nc),
        in_specs=[pl.BlockSpec((None, L, LANES), lambda b, c: (b, c, odt)), vec, vec,
                  pl.BlockSpec((LANES, G * LANES), lambda b, c: (0, 0)),
                  pl.BlockSpec((G * 8, LANES), lambda b, c: (0, 0))],
        out_specs=[wide, wide, pl.BlockSpec((None, None, G * 8, L), lambda b, c: (b, c, 0, 0))],
        out_shape=[jax.ShapeDtypeStruct((Bl, S, G * LANES), F32), jax.ShapeDtypeStruct((Bl, S, G * LANES), F32),
                   jax.ShapeDtypeStruct((Bl, nc, G * 8, L), F32)],
        compiler_params=_cp(("parallel", "parallel")),
    )(proj, dt_bias, a_log, selcat, selbig)


def _ssd_post(d_a, ddt, dtg, proj, dt_bias, a_log, selcat, dm, name):
    Bl, S, L, G = dm.Bl, dm.S, CHUNK, N_GROUPS
    nc = S // L
    odt = dm.ODT // LANES

    def body(da_ref, ddt_ref, dtg_ref, dtr_ref, dtb_ref, alog_ref, selcat_ref, ddtr_ref, dpar_ref):
        @pl.when((pl.program_id(0) == 0) & (pl.program_id(1) == 0))
        def _():
            dpar_ref[...] = jnp.zeros_like(dpar_ref)

        row = lax.broadcasted_iota(jnp.int32, (L, L), 0)
        col = lax.broadcasted_iota(jnp.int32, (L, L), 1)
        selcat = selcat_ref[...]
        a_all = -jnp.exp(alog_ref[...])
        a4 = _dg(jnp.broadcast_to(a_all, (8, LANES)), selcat, NN, HI)[0:1, :]
        dadt = _dg((col >= row).astype(F32), da_ref[...], NN, HI)
        ddt4 = ddt_ref[...] + dadt * a4
        da4 = jnp.sum(dadt * dtg_ref[...], axis=0, keepdims=True)
        ddt_all = _dg(ddt4, selcat, NT, HI)
        da_all = _dg(jnp.broadcast_to(da4, (8, G * LANES)), selcat, NT, HI)[0:1, :]
        ddtr = ddt_all * _sigmoid(dtr_ref[...] + dtb_ref[...])
        ddtr_ref[...] = ddtr.astype(ddtr_ref.dtype)
        dpar_ref[0:1, :] += jnp.sum(ddtr, axis=0, keepdims=True)
        dpar_ref[1:2, :] += da_all * a_all

    vec = pl.BlockSpec((1, LANES), lambda b, c: (0, 0))
    wide = pl.BlockSpec((None, L, G * LANES), lambda b, c: (b, c, 0))
    return pl.pallas_call(
        body, name=name, grid=(Bl, nc),
        in_specs=[wide, wide, wide, pl.BlockSpec((None, L, LANES), lambda b, c: (b, c, odt)), vec, vec,
                  pl.BlockSpec((LANES, G * LANES), lambda b, c: (0, 0))],
        out_specs=[pl.BlockSpec((None, L, LANES), lambda b, c: (b, c, 0)), pl.BlockSpec((8, LANES), lambda b, c: (0, 0))],
        out_shape=[jax.ShapeDtypeStruct((Bl, S, LANES), MXU_DTYPE), jax.ShapeDtypeStruct((8, LANES), F32)],
        compiler_params=_cp(("arbitrary", "arbitrary")),
    )(d_a, ddt, dtg, proj, dt_bias, a_log, selcat)


def _scan_fwd(xbc_a, proj, dtg4, acs4, acs_t4, dsk_exp, norm_g, dm, name):
    Bl, S, DI, N, R, L, G = dm.Bl, dm.S, dm.DI, dm.N, dm.R, CHUNK, N_GROUPS
    RP = R * HEAD_DIM
    nc = S // L
    ob, ocm = DI // N, DI // N + G

    def body(xs_ref, bm_ref, cm_ref, z_ref, dtg_ref, acs_ref, acst_ref, dsk_ref, ng_ref, y_ref, yn_ref, hp_ref, h_ref):
        @pl.when(pl.program_id(2) == 0)
        def _():
            h_ref[...] = jnp.zeros_like(h_ref)

        causal = lax.broadcasted_iota(jnp.int32, (L, L), 0) >= lax.broadcasted_iota(jnp.int32, (L, L), 1)
        lane = lax.broadcasted_iota(jnp.int32, (L, LANES), 1)
        dtg, acs, acs_t = dtg_ref[...], acs_ref[...], acst_ref[...]
        xs = xs_ref[...]
        bmb, cmb = bm_ref[...].astype(MXU_DTYPE), cm_ref[...].astype(MXU_DTYPE)
        sg = _dg(cmb, bmb, NT)
        xdt = xs * _expand_cols(dtg, R, lane)
        xb = xdt.astype(MXU_DTYPE)
        parts = []
        for q in range(R // 2):
            x2 = xb[:, LANES * q:LANES * (q + 1)]
            ys = []
            for r in (2 * q, 2 * q + 1):
                dec = jnp.exp(jnp.where(causal, acs[:, r:r + 1] - acs_t[r:r + 1, :], -1e30))
                ys.append(_dg((sg * dec).astype(MXU_DTYPE), x2, NN))
            parts.append(jnp.where(lane < HEAD_DIM, ys[0], ys[1]))
        ydiag = jnp.concatenate(parts, axis=1)
        acs_last = acs[L - 1:L, :]
        h_cur = h_ref[...]
        hb = h_cur.astype(MXU_DTYPE)
        yoff = _dg(cmb, hb, NT) * _expand_cols(jnp.exp(acs), R, lane)
        st = _dg((xdt * _expand_cols(jnp.exp(acs_last - acs), R, lane)).astype(MXU_DTYPE), bmb, TN)
        hp_ref[...] = hb
        h_ref[...] = h_cur * _head_rows(jnp.exp(acs_t[:, L - 1:L]), R, N) + st
        y = ydiag + yoff + dsk_ref[...] * xs
        y_ref[...] = y
        z = z_ref[...]
        yg = y * (z * _sigmoid(z))
        rr = lax.rsqrt(jnp.mean(yg * yg, axis=-1, keepdims=True) + EPS)
        yn_ref[...] = (yg * rr * ng_ref[...]).astype(yn_ref.dtype)

    grp = pl.BlockSpec((None, L, RP), lambda b, g, c: (b, c, g))
    lanes = pl.BlockSpec((None, L, LANES), lambda b, g, c: (b, c, g))
    chan = pl.BlockSpec((1, RP), lambda b, g, c: (0, g))
    return pl.pallas_call(
        body, name=name, grid=(Bl, G, nc),
        in_specs=[grp,
                  pl.BlockSpec((None, L, N), lambda b, g, c: (b, c, ob + g)),
                  pl.BlockSpec((None, L, N), lambda b, g, c: (b, c, ocm + g)),
                  grp, lanes, lanes,
                  pl.BlockSpec((None, None, 8, L), lambda b, g, c: (b, c, g, 0)),
                  chan, chan],
        out_specs=[grp, grp, pl.BlockSpec((None, None, None, RP, N), lambda b, g, c: (b, g, c, 0, 0))],
        out_shape=[jax.ShapeDtypeStruct((Bl, S, DI), F32), jax.ShapeDtypeStruct((Bl, S, DI), MXU_DTYPE),
                   jax.ShapeDtypeStruct((Bl, G, nc, RP, N), MXU_DTYPE)],
        scratch_shapes=[pltpu.VMEM((RP, N), F32)],
        compiler_params=_cp(("parallel", "parallel", "arbitrary")),
    )(xbc_a, xbc_a, xbc_a, proj, dtg4, acs4, acs_t4, dsk_exp, norm_g.reshape(1, DI))


def _scan_bwd(dyn, y, xbc_a, proj, hprev, dtg4, acs4, acs_t4, dsk_exp, norm_g, dm, name):
    Bl, S, DI, N, R, L, G = dm.Bl, dm.S, dm.DI, dm.N, dm.R, CHUNK, N_GROUPS
    RP = R * HEAD_DIM
    nc = S // L
    ob, ocm = DI // N, DI // N + G

    def body(dyn_ref, y_ref, z_ref, xs_ref, bm_ref, cm_ref, hp_ref, dtg_ref, acs_ref, acst_ref, dsk_ref, ng_ref,
             dz_ref, dxs_ref, dbm_ref, dcm_ref, da_ref, ddt_ref, ddsk_ref, dng_ref, dh_ref):
        b, c = pl.program_id(1), pl.program_id(2)

        @pl.when(c == 0)
        def _():
            dh_ref[...] = jnp.zeros_like(dh_ref)

        @pl.when((b == 0) & (c == 0))
        def _():
            ddsk_ref[...] = jnp.zeros_like(ddsk_ref)
            dng_ref[...] = jnp.zeros_like(dng_ref)

        row = lax.broadcasted_iota(jnp.int32, (L, L), 0)
        col = lax.broadcasted_iota(jnp.int32, (L, L), 1)
        causal, anti = row >= col, col >= row
        lane = lax.broadcasted_iota(jnp.int32, (L, LANES), 1)
        etb = (lax.shift_right_logical(lax.broadcasted_iota(jnp.int32, (RP, LANES), 0), 6)
               == lax.broadcasted_iota(jnp.int32, (RP, LANES), 1)).astype(MXU_DTYPE)

        dtg, acs, acs_t = dtg_ref[...], acs_ref[...], acst_ref[...]
        xs, z, y, dyn = xs_ref[...], z_ref[...], y_ref[...], dyn_ref[...]
        bmb, cmb = bm_ref[...].astype(MXU_DTYPE), cm_ref[...].astype(MXU_DTYPE)
        hpb = hp_ref[...]
        ng = ng_ref[...]

        sz = _sigmoid(z)
        siluz = z * sz
        yg = y * siluz
        rr = lax.rsqrt(jnp.mean(yg * yg, axis=-1, keepdims=True) + EPS)
        yhat = yg * rr
        dng_ref[...] += jnp.sum(dyn * yhat, axis=0, keepdims=True)
        dyhat = dyn * ng
        dyg = rr * (dyhat - yhat * jnp.mean(dyhat * yhat, axis=-1, keepdims=True))
        dy = dyg * siluz
        dz_ref[...] = (dyg * y * (sz * (1.0 + z * (1.0 - sz)))).astype(dz_ref.dtype)

        dxs = dy * dsk_ref[...]
        ddsk_ref[...] += jnp.sum(dy * xs, axis=0, keepdims=True)

        dt_exp = _expand_cols(dtg, R, lane)
        xdt = xs * dt_exp
        xb = xdt.astype(MXU_DTYPE)
        dyb = dy.astype(MXU_DTYPE)
        acs_last = acs[L - 1:L, :]
        e_a = jnp.exp(acs)
        dsd = jnp.exp(acs_last - acs)
        cd = jnp.exp(acs_last)
        cd_rows = _head_rows(jnp.exp(acs_t[:, L - 1:L]), R, N)

        q_ = _dg(cmb, hpb, NT)
        dq = dy * _expand_cols(e_a, R, lane)
        dqb = dq.astype(MXU_DTYPE)
        dcm = _dg(dqb, hpb, NN)
        dh_yoff = _dg(dqb, cmb, TN)

        dhn = dh_ref[...]
        wprod = dhn * hpb.astype(F32)
        per_head = jnp.concatenate(
            [jnp.sum(wprod[HEAD_DIM * r:HEAD_DIM * (r + 1), :], axis=0, keepdims=True) for r in range(R)]
            + ([jnp.zeros((8 - R, N), F32)] if R < 8 else []), axis=0)
        dcd_col = jnp.sum(per_head, axis=1, keepdims=True)
        diag8 = lax.broadcasted_iota(jnp.int32, (8, LANES), 0) == lax.broadcasted_iota(jnp.int32, (8, LANES), 1)
        dcd_lane = jnp.sum(jnp.where(diag8, dcd_col, 0.0), axis=0, keepdims=True)
        d_a_last = dcd_lane * cd
        dh_ref[...] = dhn * cd_rows + dh_yoff
        dhnb = dhn.astype(MXU_DTYPE)

        ds_exp = _expand_cols(dsd, R, lane)
        e_ = _dg(bmb, dhnb, NT)
        dxdt = ds_exp * e_
        xds = xdt * ds_exp
        dbm = _dg(xds.astype(MXU_DTYPE), dhnb, NN)

        sg = _dg(cmb, bmb, NT)
        sg_t = _dg(bmb, cmb, NT)
        dsg = jnp.zeros((L, L), F32)
        dsg_t = jnp.zeros((L, L), F32)
        d_a = jnp.zeros((L, LANES), F32)
        parts = []
        for q in range(R // 2):
            x2 = xb[:, LANES * q:LANES * (q + 1)]
            dy2 = dyb[:, LANES * q:LANES * (q + 1)]
            dxs2 = []
            for hh, r in enumerate((2 * q, 2 * q + 1)):
                mine = (lane < HEAD_DIM) if hh == 0 else (lane >= HEAD_DIM)
                dec = jnp.exp(jnp.where(causal, acs[:, r:r + 1] - acs_t[r:r + 1, :], -1e30))
                dec_t = jnp.exp(jnp.where(anti, acs_t[r:r + 1, :] - acs[:, r:r + 1], -1e30))
                dy2m = jnp.where(mine, dy2, jnp.zeros_like(dy2))
                dm_ = _dg(dy2m, x2, NT)
                dm_t = _dg(x2, dy2m, NT)
                m_t = sg_t * dec_t
                da_col = (jnp.sum(dm_ * (sg * dec), axis=1, keepdims=True)
                          - jnp.sum(dm_t * m_t, axis=1, keepdims=True))
                d_a = d_a + jnp.where(lane == r, da_col, 0.0)
                dsg = dsg + dm_ * dec
                dsg_t = dsg_t + dm_t * dec_t
                dxs2.append(_dg(m_t.astype(MXU_DTYPE), dy2, NN))
            parts.append(jnp.where(lane < HEAD_DIM, dxs2[0], dxs2[1]))
        dxdt = dxdt + jnp.concatenate(parts, axis=1)
        dcm_ref[...] = dcm + _dg(dsg.astype(MXU_DTYPE), bmb, NN)
        dbm_ref[...] = dbm + _dg(dsg_t.astype(MXU_DTYPE), cmb, NN)
        dxs_ref[...] = dxs + dxdt * dt_exp

        stack = jnp.concatenate([dq * q_ - xds * e_, xds * e_, dxdt * xs], axis=0)
        hi = stack.astype(MXU_DTYPE)
        lo = (stack - hi.astype(F32)).astype(MXU_DTYPE)
        hs = _dg(hi, etb, NN) + _dg(lo, etb, NN)
        t2 = hs[L:2 * L, :]
        rowl = lax.broadcasted_iota(jnp.int32, (L, LANES), 0)
        d_a_last = d_a_last + jnp.sum(t2, axis=0, keepdims=True)
        da_ref[...] = d_a + hs[0:L, :] + jnp.where(rowl == L - 1, d_a_last, 0.0)
        ddt_ref[...] = hs[2 * L:3 * L, :]

    grp = pl.BlockSpec((None, L, RP), lambda g, b, c: (b, nc - 1 - c, g))
    lanes = pl.BlockSpec((None, L, LANES), lambda g, b, c: (b, nc - 1 - c, g))
    bspec = pl.BlockSpec((None, L, N), lambda g, b, c: (b, nc - 1 - c, ob + g))
    cspec = pl.BlockSpec((None, L, N), lambda g, b, c: (b, nc - 1 - c, ocm + g))
    gn = pl.BlockSpec((None, L, N), lambda g, b, c: (b, nc - 1 - c, g))
    chan = pl.BlockSpec((1, RP), lambda g, b, c: (0, g))
    wide = jax.ShapeDtypeStruct((Bl, S, G * LANES), F32)
    return pl.pallas_call(
        body, name=name, grid=(G, Bl, nc),
        in_specs=[grp, grp, grp, grp, bspec, cspec,
                  pl.BlockSpec((None, None, None, RP, N), lambda g, b, c: (b, g, nc - 1 - c, 0, 0)),
                  lanes, lanes, pl.BlockSpec((None, None, 8, L), lambda g, b, c: (b, nc - 1 - c, g, 0)),
                  chan, chan],
        out_specs=[grp, grp, gn, gn, lanes, lanes, chan, chan],
        out_shape=[jax.ShapeDtypeStruct((Bl, S, DI), MXU_DTYPE), jax.ShapeDtypeStruct((Bl, S, DI), F32),
                   jax.ShapeDtypeStruct((Bl, S, G * N), F32), jax.ShapeDtypeStruct((Bl, S, G * N), F32),
                   wide, wide, jax.ShapeDtypeStruct((1, DI), F32), jax.ShapeDtypeStruct((1, DI), F32)],
        scratch_shapes=[pltpu.VMEM((RP, N), F32)],
        compiler_params=_cp(("arbitrary", "arbitrary", "arbitrary")),
    )(dyn, y, proj, xbc_a, xbc_a, xbc_a, hprev, dtg4, acs4, acs_t4, dsk_exp, norm_g.reshape(1, DI))


def _adam_update(w, m, v, g):
    c1 = 1.0 - ADAM_B1 ** ADAM_STEP
    c2 = 1.0 - ADAM_B2 ** ADAM_STEP
    nm = ADAM_B1 * m + (1.0 - ADAM_B1) * g
    nv = ADAM_B2 * v + (1.0 - ADAM_B2) * (g * g)
    return nm, nv, -ADAM_LR * ((nm / c1) / (jnp.sqrt(nv / c2) + ADAM_EPS) + ADAM_WD * w)


def _adamw(w, m, v, g, name, tr=128):
    rows, cols = w.shape
    tr = _tile(rows, tr, 8)

    def body(w_ref, m_ref, v_ref, ga_ref, g_ref, d_ref, nm_ref, nv_ref):
        g = ga_ref[...]
        g_ref[...] = g
        nm_ref[...], nv_ref[...], d_ref[...] = _adam_update(w_ref[...], m_ref[...], v_ref[...], g)

    blk = pl.BlockSpec((tr, cols), lambda i: (i, 0))
    shp = jax.ShapeDtypeStruct((rows, cols), F32)
    return pl.pallas_call(
        body, name=name, grid=(rows // tr,), in_specs=[blk] * 4, out_specs=[blk] * 4,
        out_shape=[shp] * 4, compiler_params=_cp(("parallel",)),
    )(w, m, v, g)


def _adamw_layers(w, m, v, g_mine, g_theirs, core, name, tr=128):
    _, rows, cols = w.shape
    tr = _tile(rows, tr, 8)

    def body(c_ref, w_ref, m_ref, v_ref, ga_ref, gb_ref, g_ref, d_ref, nm_ref, nv_ref):
        g = jnp.where(pl.program_id(0) == c_ref[0], ga_ref[...], gb_ref[...])
        g_ref[...] = g
        nm_ref[...], nv_ref[...], d_ref[...] = _adam_update(w_ref[...], m_ref[...], v_ref[...], g)

    lay = pl.BlockSpec((None, tr, cols), lambda l, i, c_ref: (l, i, 0))
    one = pl.BlockSpec((tr, cols), lambda l, i, c_ref: (i, 0))
    shp = jax.ShapeDtypeStruct(w.shape, F32)
    return pl.pallas_call(
        body, name=name,
        grid_spec=pltpu.PrefetchScalarGridSpec(num_scalar_prefetch=1, grid=(2, rows // tr),
                                               in_specs=[lay, lay, lay, one, one], out_specs=[lay] * 4),
        out_shape=[shp] * 4, compiler_params=_cp(("parallel", "parallel")),
    )(core, w, m, v, g_mine, g_theirs)


def _sum_slots(buf, name, tr=256):
    n, rows, cols = buf.shape
    tr = _tile(rows, tr, 8)

    def body(b_ref, o_ref):
        acc = b_ref[0].astype(F32)
        for k in range(1, n):
            acc = acc + b_ref[k].astype(F32)
        o_ref[...] = acc

    return pl.pallas_call(
        body, name=name, grid=(rows // tr,),
        in_specs=[pl.BlockSpec((n, tr, cols), lambda i: (0, i, 0))],
        out_specs=pl.BlockSpec((tr, cols), lambda i: (i, 0)),
        out_shape=jax.ShapeDtypeStruct((rows, cols), F32), compiler_params=_cp(("parallel",)),
    )(buf)


_ANY = pl.BlockSpec(memory_space=pl.ANY)


def _exchange_chips(src, per_dest, name):
    rows, cols = src.shape[-2:]

    def body(in_ref, out_ref, send_sems, recv_sems, local_sem):
        x, y, c = lax.axis_index("x"), lax.axis_index("y"), lax.axis_index("c")
        me = 2 * x + y
        chips = [(1 - x, y), (x, 1 - y), (1 - x, 1 - y)]

        def block(j):
            return in_ref.at[j] if per_dest else in_ref

        mine = pltpu.make_async_copy(block(me), out_ref.at[me], local_sem)
        mine.start()
        sends = []
        for k, (px, py) in enumerate(chips):
            cp = pltpu.make_async_remote_copy(
                src_ref=block(2 * px + py), dst_ref=out_ref.at[me], send_sem=send_sems.at[k],
                recv_sem=recv_sems.at[k], device_id=(px, py, c), device_id_type=MESH)
            cp.start()
            sends.append(cp)
        for k, (px, py) in enumerate(chips):
            pltpu.make_async_remote_copy(
                src_ref=block(me), dst_ref=out_ref.at[2 * px + py], send_sem=send_sems.at[k],
                recv_sem=recv_sems.at[k], device_id=(px, py, c), device_id_type=MESH).wait_recv()
        for cp in sends:
            cp.wait_send()
        mine.wait()

    return pl.pallas_call(
        body, name=name, in_specs=[_ANY], out_specs=_ANY,
        out_shape=jax.ShapeDtypeStruct((N_CHIPS, rows, cols), src.dtype),
        scratch_shapes=[pltpu.SemaphoreType.DMA((3,)), pltpu.SemaphoreType.DMA((3,)), pltpu.SemaphoreType.DMA(())],
    )(src)


def _shard_window(ref, kind, j, lead=()):
    if kind == "slots":
        return ref.at[(j,) + lead]
    r, c = ref.shape[-2] // (N_CHIPS if kind == "rows" else 1), ref.shape[-1] // (N_CHIPS if kind == "cols" else 1)
    full = tuple(slice(None) for _ in range(len(ref.shape) - 2 - len(lead)))
    if kind == "rows":
        return ref.at[lead + full + (pl.ds(pl.multiple_of(j * r, 16), r), slice(None))]
    return ref.at[lead + full + (slice(None), pl.ds(pl.multiple_of(j * c, LANES), c))]


def _gather_kind(shard, axis):
    if axis == 1:
        return "rows"
    return "cols" if shard.shape[2] % LANES == 0 else "slots"


def _gather_weights(shards, axes, name):
    kinds = ["rows" if ax == 1 else "slots" for ax in axes]
    nw = len(shards)

    def out_shape(s, kind):
        d, r, c = s.shape
        shp = {"rows": (d, N_CHIPS * r, c), "cols": (d, r, N_CHIPS * c), "slots": (N_CHIPS, d, r, c)}[kind]
        return jax.ShapeDtypeStruct(shp, s.dtype)

    assert all(s.shape[0] == 2 for s in shards)

    def body(*refs):
        ins, outs = refs[:nw], refs[nw:2 * nw]
        ici_send, ici_recv, d2d_send, d2d_recv, local_sems = refs[2 * nw:]
        x, y, c = lax.axis_index("x"), lax.axis_index("y"), lax.axis_index("c")
        me = 2 * x + y
        chips = [(1 - x, y), (x, 1 - y), (1 - x, 1 - y)]
        sends = []
        for i in range(nw):
            own = pltpu.make_async_copy(ins[i], _shard_window(outs[i], kinds[i], me), local_sems.at[i])
            own.start()
            sends.append((own, False))
            for k, (px, py) in enumerate(chips):
                cp = pltpu.make_async_remote_copy(
                    src_ref=ins[i].at[c], dst_ref=_shard_window(outs[i], kinds[i], me, (c,)),
                    send_sem=ici_send.at[3 * i + k], recv_sem=ici_recv.at[3 * i + k],
                    device_id=(px, py, c), device_id_type=MESH)
                cp.start()
                sends.append((cp, True))
        for i in range(nw):
            for k, (px, py) in enumerate(chips):
                win = _shard_window(outs[i], kinds[i], 2 * px + py, (c,))
                pltpu.make_async_remote_copy(
                    src_ref=ins[i].at[c], dst_ref=win, send_sem=ici_send.at[3 * i + k], recv_sem=ici_recv.at[3 * i + k],
                    device_id=(px, py, c), device_id_type=MESH).wait_recv()
                fw = pltpu.make_async_remote_copy(
                    src_ref=win, dst_ref=win, send_sem=d2d_send.at[3 * i + k], recv_sem=d2d_recv.at[3 * i + k],
                    device_id=(x, y, 1 - c), device_id_type=MESH)
                fw.start()
                sends.append((fw, True))
        for i in range(nw):
            for k, (px, py) in enumerate(chips):
                win = _shard_window(outs[i], kinds[i], 2 * px + py, (1 - c,))
                pltpu.make_async_remote_copy(
                    src_ref=win, dst_ref=win, send_sem=d2d_send.at[3 * i + k], recv_sem=d2d_recv.at[3 * i + k],
                    device_id=(x, y, 1 - c), device_id_type=MESH).wait_recv()
        for cp, remote in sends:
            cp.wait_send() if remote else cp.wait()

    outs = pl.pallas_call(
        body, name=name, in_specs=[_ANY] * nw, out_specs=[_ANY] * nw,
        out_shape=[out_shape(s, k) for s, k in zip(shards, kinds)],
        scratch_shapes=[pltpu.SemaphoreType.DMA((3 * nw,)), pltpu.SemaphoreType.DMA((3 * nw,)),
                        pltpu.SemaphoreType.DMA((3 * nw,)), pltpu.SemaphoreType.DMA((3 * nw,)),
                        pltpu.SemaphoreType.DMA((nw,))],
    )(*shards)
    return [jnp.concatenate([o[j] for j in range(N_CHIPS)], axis=2) if k == "slots" else o for o, k in zip(outs, kinds)]


def _swap_other_layer(gst, name):
    nv = len(gst)

    def body(*refs):
        ins, outs, send_sems, recv_sems = refs[:nv], refs[nv:2 * nv], refs[2 * nv], refs[2 * nv + 1]
        x, y, c = lax.axis_index("x"), lax.axis_index("y"), lax.axis_index("c")
        cps = [pltpu.make_async_remote_copy(src_ref=ins[i].at[1 - c], dst_ref=outs[i], send_sem=send_sems.at[i],
                                            recv_sem=recv_sems.at[i], device_id=(x, y, 1 - c), device_id_type=MESH)
               for i in range(nv)]
        for cp in cps:
            cp.start()
        for cp in cps:
            cp.wait()

    return pl.pallas_call(
        body, name=name, in_specs=[_ANY] * nv, out_specs=[_ANY] * nv,
        out_shape=[jax.ShapeDtypeStruct(v.shape[1:], v.dtype) for v in gst],
        scratch_shapes=[pltpu.SemaphoreType.DMA((nv,)), pltpu.SemaphoreType.DMA((nv,))],
    )(*gst)


def _pair_sum(g, other, core, name, tr=256):
    _, rows, cols = g.shape
    tr = _tile(rows, tr, 16)

    def body(c_ref, g_ref, o_ref, s_ref):
        s_ref[...] = (g_ref[...].astype(F32) + o_ref[...].astype(F32)).astype(s_ref.dtype)

    blk = pl.BlockSpec((tr, cols), lambda i, c_ref: (i, 0))
    return pl.pallas_call(
        body, name=name,
        grid_spec=pltpu.PrefetchScalarGridSpec(
            num_scalar_prefetch=1, grid=(rows // tr,),
            in_specs=[pl.BlockSpec((None, tr, cols), lambda i, c_ref: (c_ref[0], i, 0)), blk], out_specs=blk),
        out_shape=jax.ShapeDtypeStruct((rows, cols), g.dtype), compiler_params=_cp(("parallel",)),
    )(core, g, other)


def _scatter_layer(parts, kinds, name):
    nw = len(parts)

    def shard_shape(g, kind):
        return g.shape[-2] // (N_CHIPS if kind == "rows" else 1), g.shape[-1] // (N_CHIPS if kind == "cols" else 1)

    def body(*refs):
        ins, outs = refs[:nw], refs[nw:2 * nw]
        send_sems, recv_sems, local_sems = refs[2 * nw:]
        x, y, c = lax.axis_index("x"), lax.axis_index("y"), lax.axis_index("c")
        me = 2 * x + y
        chips = [(1 - x, y), (x, 1 - y), (1 - x, 1 - y)]
        sends = []
        for i in range(nw):
            own = pltpu.make_async_copy(_shard_window(ins[i], kinds[i], me), outs[i].at[me], local_sems.at[i])
            own.start()
            sends.append((own, False))
            for k, (px, py) in enumerate(chips):
                cp = pltpu.make_async_remote_copy(
                    src_ref=_shard_window(ins[i], kinds[i], 2 * px + py), dst_ref=outs[i].at[me],
                    send_sem=send_sems.at[3 * i + k], recv_sem=recv_sems.at[3 * i + k],
                    device_id=(px, py, c), device_id_type=MESH)
                cp.start()
                sends.append((cp, True))
        for i in range(nw):
            for k, (px, py) in enumerate(chips):
                pltpu.make_async_remote_copy(
                    src_ref=_shard_window(ins[i], kinds[i], me), dst_ref=outs[i].at[2 * px + py],
                    send_sem=send_sems.at[3 * i + k], recv_sem=recv_sems.at[3 * i + k],
                    device_id=(px, py, c), device_id_type=MESH).wait_recv()
        for cp, remote in sends:
            cp.wait_send() if remote else cp.wait()

    return pl.pallas_call(
        body, name=name, in_specs=[_ANY] * nw, out_specs=[_ANY] * nw,
        out_shape=[jax.ShapeDtypeStruct((N_CHIPS,) + shard_shape(g, k), g.dtype) for g, k in zip(parts, kinds)],
        scratch_shapes=[pltpu.SemaphoreType.DMA((3 * nw,)), pltpu.SemaphoreType.DMA((3 * nw,)),
                        pltpu.SemaphoreType.DMA((nw,))],
    )(*parts)


def _sibling_swap(vs, name):
    nv = len(vs)

    def body(*refs):
        ins, outs, send_sems, recv_sems = refs[:nv], refs[nv:2 * nv], refs[2 * nv], refs[2 * nv + 1]
        x, y, c = lax.axis_index("x"), lax.axis_index("y"), lax.axis_index("c")
        cps = [pltpu.make_async_remote_copy(src_ref=ins[i], dst_ref=outs[i], send_sem=send_sems.at[i],
                                            recv_sem=recv_sems.at[i], device_id=(x, y, 1 - c), device_id_type=MESH)
               for i in range(nv)]
        for cp in cps:
            cp.start()
        for cp in cps:
            cp.wait()

    return pl.pallas_call(
        body, name=name, in_specs=[_ANY] * nv, out_specs=[_ANY] * nv,
        out_shape=[jax.ShapeDtypeStruct(v.shape, v.dtype) for v in vs],
        scratch_shapes=[pltpu.SemaphoreType.DMA((nv,)), pltpu.SemaphoreType.DMA((nv,))],
    )(*vs)


def _allgather_all(v, name):
    rows, cols = v.shape

    def body(in_ref, out_ref, send_sems, recv_sems, local_sem):
        x, y, c = lax.axis_index("x"), lax.axis_index("y"), lax.axis_index("c")
        me = 4 * x + 2 * y + c
        peers = []
        for k in range(1, N_DEV):
            peers.append(((1 - x) if k & 4 else x, (1 - y) if k & 2 else y, (1 - c) if k & 1 else c))
        mine = pltpu.make_async_copy(in_ref, out_ref.at[me], local_sem)
        mine.start()
        sends = []
        for k, peer in enumerate(peers):
            cp = pltpu.make_async_remote_copy(src_ref=in_ref, dst_ref=out_ref.at[me], send_sem=send_sems.at[k],
                                              recv_sem=recv_sems.at[k], device_id=peer, device_id_type=MESH)
            cp.start()
            sends.append(cp)
        for k, (px, py, pc) in enumerate(peers):
            pltpu.make_async_remote_copy(src_ref=in_ref, dst_ref=out_ref.at[4 * px + 2 * py + pc],
                                         send_sem=send_sems.at[k], recv_sem=recv_sems.at[k],
                                         device_id=(px, py, pc), device_id_type=MESH).wait_recv()
        for cp in sends:
            cp.wait_send()
        mine.wait()

    return pl.pallas_call(
        body, name=name, in_specs=[_ANY], out_specs=_ANY,
        out_shape=jax.ShapeDtypeStruct((N_DEV, rows, cols), v.dtype),
        scratch_shapes=[pltpu.SemaphoreType.DMA((N_DEV - 1,)), pltpu.SemaphoreType.DMA((N_DEV - 1,)),
                        pltpu.SemaphoreType.DMA(())],
    )(v)


def _pack(arrs, dtype, width, row_mult):
    flat = jnp.concatenate([a.reshape(-1).astype(dtype) for a in arrs])
    unit = width * row_mult
    total = -(-flat.shape[0] // unit) * unit
    return jnp.pad(flat, (0, total - flat.shape[0])).reshape(-1, width)


def _unpack(buf, shapes):
    flat = buf.reshape(-1)
    out, off = [], 0
    for shp in shapes:
        n = 1
        for d in shp:
            n *= d
        out.append(flat[off:off + n].reshape(shp))
        off += n
    return out


class _Dims:
    pass


def _dims(x, ssd_dt_bias, ssd_norm_g, ssd_conv_b, ffn_conv_b):
    dm = _Dims()
    dm.Bl, dm.S, dm.D = x.shape
    dm.H, dm.DI, dm.CD = ssd_dt_bias.shape[-1], ssd_norm_g.shape[-1], ssd_conv_b.shape[-1]
    dm.N = (dm.CD - dm.DI) // (2 * N_GROUPS)
    dm.R = dm.H // N_GROUPS
    dm.DFF = ffn_conv_b.shape[-1] // 2
    dm.OX = dm.DI
    dm.OB = dm.DI + dm.CD
    dm.OC = dm.OB + dm.D
    dm.OH = dm.OC + dm.D
    dm.OG = dm.OH + dm.D
    dm.ODT = dm.OG + 2 * dm.D
    dm.NP = dm.ODT + LANES
    assert dm.DI // dm.H == HEAD_DIM and dm.N == LANES and dm.R % 2 == 0 and dm.S % CHUNK == 0 and dm.H <= LANES
    return dm


def _permute_w_in(w, dm):
    o = dm.DI + dm.CD
    pad = jnp.zeros((w.shape[0], LANES - dm.H), w.dtype)
    return jnp.concatenate([w[:, :o], w[:, o + dm.H:], w[:, o:o + dm.H], pad], axis=1)


def _unpermute_w_in(dw, dm):
    o = dm.DI + dm.CD
    return jnp.concatenate([dw[:, :o], dw[:, dm.ODT:dm.ODT + dm.H], dw[:, o:dm.ODT]], axis=1)


def _lane_pad(v):
    return jnp.pad(v.reshape(1, -1).astype(F32), ((0, 0), (0, LANES - v.shape[-1])))


def _pad8(w):
    return jnp.pad(w.astype(F32), ((0, 8 - w.shape[0]), (0, 0)))


def _head_select(dm):
    j = jnp.arange(LANES)[None, :, None]
    r = jnp.arange(LANES)[None, None, :]
    g = jnp.arange(N_GROUPS)[:, None, None]
    sel = ((j == dm.R * g + r) & (r < dm.R)).astype(F32)
    selcat = jnp.transpose(sel, (1, 0, 2)).reshape(LANES, N_GROUPS * LANES)
    selbig = jnp.transpose(sel[:, :, :8], (0, 2, 1)).reshape(N_GROUPS * 8, LANES)
    return selcat, selbig


def _mix_fwd(dm, h, w, sp, sel, tag):
    Bl, S, D = dm.Bl, dm.S, dm.D
    T = Bl * S
    proj = _matmul(h.reshape(T, D), w["w_in_p"], "NN", F32, tag + "_in_proj", tn=1152).reshape(Bl, S, dm.NP)
    xbc_a = _ssd_conv_fwd(proj, sp["ssd_conv_w8"], sp["ssd_conv_b"], dm.OX, dm.CD, tag + "_ssd_conv")
    dtg, acs, acs_t = _ssd_prep(proj, sp["dt_bias"], sp["a_log"], sel[0], sel[1], dm, tag + "_ssd_prep")
    y, yn, hprev = _scan_fwd(xbc_a, proj, dtg, acs, acs_t, sp["dsk_exp"], sp["ssd_norm_g"], dm, tag + "_ssd_scan")
    y_ssd = _matmul(yn.reshape(T, dm.DI), w["w_ssd_out"], "NN", F32, tag + "_ssd_out", tk=2048).reshape(Bl, S, D)
    s = _sc_conv_fwd(proj, sp["sc_conv_w8"], (dm.OB, dm.OC, dm.OH), D, tag + "_sc_conv")
    y_sc = _matmul(s.reshape(T, D), w["w_sc_out"], "NN", F32, tag + "_sc_out", tk=1024).reshape(Bl, S, D)
    mixin = _merge_fwd(proj, y_ssd, y_sc, dm.OG, D, tag + "_merge")
    mix = _matmul(mixin.reshape(T, D), w["w_o"], "NN", F32, tag + "_o", tk=1024).reshape(Bl, S, D)
    return mix, (h, proj, xbc_a, y, yn, hprev, y_ssd, y_sc, s, mixin, dtg, acs, acs_t)


def _mix_bwd(dm, dmix, saved, w, sp, sel, tag):
    Bl, S, D = dm.Bl, dm.S, dm.D
    T = Bl * S
    h, proj, xbc_a, y, yn, hprev, y_ssd, y_sc, s, mixin, dtg, acs, acs_t = saved
    dmix2 = dmix.reshape(T, D)
    g = {}
    g["w_o"] = _matmul(mixin.reshape(T, D), dmix2, "TN", WIRE_DTYPE,tag + "_dw_o", tk=1024)
    dmixin = _matmul(dmix2, w["w_o"], "NT", F32, tag + "_d_o", tk=1024).reshape(Bl, S, D)
    dy_ssd, dy_sc, dg1, dg2 = _merge_bwd(dmixin, proj, y_ssd, y_sc, dm.OG, D, tag + "_merge_bwd")
    g["w_sc_out"] = _matmul(s.reshape(T, D), dy_sc.reshape(T, D), "TN", WIRE_DTYPE,tag + "_dw_sc_out", tk=1024)
    ds = _matmul(dy_sc.reshape(T, D), w["w_sc_out"], "NT", F32, tag + "_d_sc_out", tk=1024).reshape(Bl, S, D)
    dscb, dscc, dsch, dscw = _sc_conv_bwd(ds, proj, sp["sc_conv_w8"], (dm.OB, dm.OC, dm.OH), D, tag + "_sc_conv_bwd")
    g["sc_conv_w"] = dscw[:3]
    g["w_ssd_out"] = _matmul(yn.reshape(T, dm.DI), dy_ssd.reshape(T, D), "TN", WIRE_DTYPE,tag + "_dw_ssd_out", tk=1024)
    dyn = _matmul(dy_ssd.reshape(T, D), w["w_ssd_out"], "NT", F32, tag + "_d_ssd_out", tk=1024).reshape(Bl, S, dm.DI)
    dz, dxs, dbm, dcm, d_a, ddt, ddsk, dng = _scan_bwd(dyn, y, xbc_a, proj, hprev, dtg, acs, acs_t, sp["dsk_exp"],
                                                       sp["ssd_norm_g"], dm, tag + "_ssd_scan_bwd")
    ddtr, dpar = _ssd_post(d_a, ddt, dtg, proj, sp["dt_bias"], sp["a_log"], sel[0], dm, tag + "_ssd_post")
    g["ssd_dt_bias"], g["ssd_a_log"] = dpar[0, :dm.H], dpar[1, :dm.H]
    g["ssd_d"] = jnp.sum(ddsk.reshape(dm.H, HEAD_DIM), axis=-1)
    g["ssd_norm_g"] = dng[0]
    dxa = jnp.concatenate([dxs, dbm, dcm], axis=-1)
    dxbc, dcw, dcb = _ssd_conv_bwd(dxa, proj, sp["ssd_conv_w8"], sp["ssd_conv_b"], dm.OX, dm.CD, tag + "_ssd_conv_bwd")
    g["ssd_conv_w"], g["ssd_conv_b"] = dcw[:4], dcb[0]
    dproj = jnp.concatenate([dz, dxbc, dscb, dscc, dsch, dg1, dg2, ddtr], axis=-1).reshape(T, dm.NP)
    g["w_in_p"] = _matmul(h.reshape(T, D), dproj, "TN", WIRE_DTYPE,tag + "_dw_in", tn=1152)
    dh = _matmul(dproj, w["w_in_p"], "NT", F32, tag + "_d_in", tk=1152).reshape(Bl, S, D)
    return dh, g


def _ffn_fwd(dm, h, w, sp, tag):
    Bl, S, D = dm.Bl, dm.S, dm.D
    T = Bl * S
    up = _matmul(h.reshape(T, D), w["w_up"], "NN", F32, tag + "_up", tn=1408).reshape(Bl, S, 2 * dm.DFF)
    a = _ffn_conv_fwd(up, sp["ffn_conv_w8"], sp["ffn_conv_b"], dm.DFF, tag + "_ffn_conv")
    f = _matmul(a.reshape(T, dm.DFF), w["w_down"], "NN", F32, tag + "_down", tk=2816).reshape(Bl, S, D)
    return f, (h, up, a)


def _ffn_bwd(dm, df, saved, w, sp, tag):
    Bl, S, D = dm.Bl, dm.S, dm.D
    T = Bl * S
    h, up, a = saved
    df2 = df.reshape(T, D)
    g = {}
    g["w_down"] = _matmul(a.reshape(T, dm.DFF), df2, "TN", WIRE_DTYPE,tag + "_dw_down", tm=1408)
    da = _matmul(df2, w["w_down"], "NT", F32, tag + "_d_down", tn=1408).reshape(Bl, S, dm.DFF)
    dg, dv, dcw, dcb = _ffn_conv_bwd(da, up, sp["ffn_conv_w8"], sp["ffn_conv_b"], dm.DFF, tag + "_ffn_conv_bwd")
    g["ffn_conv_w"], g["ffn_conv_b"] = dcw[:3], dcb[0]
    dup = jnp.concatenate([dg, dv], axis=-1).reshape(T, 2 * dm.DFF)
    g["w_up"] = _matmul(h.reshape(T, D), dup, "TN", WIRE_DTYPE,tag + "_dw_up", tn=1408)
    dh = _matmul(dup, w["w_up"], "NT", F32, tag + "_d_up", tk=1408).reshape(Bl, S, D)
    return dh, g


def _local_step(dm, x, c, target, wfull, small):
    Bl, S, D = dm.Bl, dm.S, dm.D
    depth = len(wfull)
    sel = _head_select(dm)
    c16 = jnp.pad(c.astype(F32), ((0, 16 - Bl), (0, 0)))
    sps, mods, acts = [], [], []
    for l in range(depth):
        sm = small[l]
        sps.append(dict(
            ssd_conv_w8=_pad8(sm["ssd_conv_w"]), ssd_conv_b=sm["ssd_conv_b"], dt_bias=_lane_pad(sm["ssd_dt_bias"]),
            a_log=_lane_pad(sm["ssd_a_log"]), dsk_exp=jnp.repeat(sm["ssd_d"].astype(F32), HEAD_DIM).reshape(1, dm.DI),
            ssd_norm_g=sm["ssd_norm_g"],
            sc_conv_w8=_pad8(sm["sc_conv_w"]), ffn_conv_w8=_pad8(sm["ffn_conv_w"]), ffn_conv_b=sm["ffn_conv_b"]))
        act, mod = _ada_fwd(c16, wfull[l]["ada_w"], sm["ada_b"], f"l{l}_ada")
        acts.append(act)
        mods.append(jnp.pad(mod[:Bl].reshape(Bl, 6, D), ((0, 0), (0, 2), (0, 0))))

    def sub(i):
        l, ffn = i // 2, i % 2
        sm = small[l]
        return dict(l=l, ffn=ffn, pre_g=sm["ffn_pre_g" if ffn else "mix_pre_g"],
                    post_g=sm["ffn_post_g" if ffn else "mix_post_g"], mod=mods[l], row=3 * ffn,
                    tag=f"l{l}_{'ffn' if ffn else 'mix'}")

    nsub = 2 * depth
    subs = [sub(i) for i in range(nsub)]
    xs, fs, saves = [x], [], []
    h = _norm_mod_fwd(x, subs[0]["pre_g"], subs[0]["mod"], subs[0]["row"], "l0_mix_pre_norm")
    for i, sb in enumerate(subs):
        l = sb["l"]
        if sb["ffn"]:
            f, sv = _ffn_fwd(dm, h, wfull[l], sps[l], sb["tag"])
        else:
            f, sv = _mix_fwd(dm, h, wfull[l], sps[l], sel, sb["tag"])
        nxt = None
        if i + 1 < nsub:
            nb = subs[i + 1]
            nxt = (nb["pre_g"], nb["mod"], nb["row"])
        xn, h = _post_norm_fwd(xs[-1], f, sb["post_g"], sb["mod"], sb["row"] + 2, nxt, sb["tag"] + "_post_norm")
        xs.append(xn)
        fs.append(f)
        saves.append(sv)

    dy, loss = _loss_fwd_bwd(xs[-1], target, "loss")

    grads = [dict() for _ in range(depth)]
    dmod = [[None] * 6 for _ in range(depth)]
    dx, dh = dy, None
    for i in reversed(range(nsub)):
        sb = subs[i]
        l = sb["l"]
        nxt = None
        if i + 1 < nsub:
            nb = subs[i + 1]
            nxt = (dh, xs[i + 1], nb["pre_g"], nb["mod"], nb["row"])
        dx, df, pb, shg = _norm_bwd(dx, nxt, (fs[i], sb["post_g"], sb["mod"], sb["row"] + 2), sb["tag"] + "_post_norm_bwd")
        if nxt is not None:
            nb = subs[i + 1]
            dmod[nb["l"]][nb["row"]], dmod[nb["l"]][nb["row"] + 1] = pb[:, 0], pb[:, 1]
            grads[nb["l"]]["ffn_pre_g" if nb["ffn"] else "mix_pre_g"] = shg[0]
        dmod[l][sb["row"] + 2] = pb[:, 2]
        grads[l]["ffn_post_g" if sb["ffn"] else "mix_post_g"] = shg[1]
        if sb["ffn"]:
            dh, g = _ffn_bwd(dm, df, saves[i], wfull[l], sps[l], sb["tag"])
        else:
            dh, g = _mix_bwd(dm, df, saves[i], wfull[l], sps[l], sel, sb["tag"])
        grads[l].update(g)
    sb = subs[0]
    grad_x, _, pb, shg = _norm_bwd(dx, (dh, xs[0], sb["pre_g"], sb["mod"], sb["row"]), None, "l0_mix_pre_norm_bwd")
    dmod[0][0], dmod[0][1] = pb[:, 0], pb[:, 1]
    grads[0]["mix_pre_g"] = shg[0]

    for l in range(depth):
        dm6 = jnp.concatenate(dmod[l], axis=-1)
        grads[l]["ada_b"] = jnp.sum(dm6, axis=0)
        dm16 = jnp.pad(dm6, ((0, 16 - Bl), (0, 0))).astype(MXU_DTYPE)
        grads[l]["ada_w"] = _matmul(acts[l], dm16, "TN", WIRE_DTYPE,f"l{l}_dw_ada")
    return loss, grad_x, grads


_WEIGHTS = ("ada_w", "ada_b", "mix_pre_g", "mix_post_g", "w_in", "ssd_conv_w", "ssd_conv_b", "ssd_dt_bias",
            "ssd_a_log", "ssd_d", "ssd_norm_g", "w_ssd_out", "sc_conv_w", "w_sc_out", "w_o", "ffn_pre_g",
            "ffn_post_g", "w_up", "ffn_conv_w", "ffn_conv_b", "w_down")
_INPUTS = ("x", "c") + _WEIGHTS + ("loss_target",) + tuple("m_" + n for n in _WEIGHTS) + tuple("v_" + n for n in _WEIGHTS)
_BIG = {"ada_w": 2, "w_in": 2, "w_ssd_out": 1, "w_sc_out": 1, "w_o": 1, "w_up": 2, "w_down": 1}
_CONV = ("ssd_conv_w", "sc_conv_w", "ffn_conv_w")
_SMALL = tuple(n for n in _WEIGHTS if n not in _BIG)


def _step(a):
    x, c, target = a["x"], a["c"], a["loss_target"]
    depth = a["ada_w"].shape[0]
    dm = _dims(x, a["ssd_dt_bias"], a["ssd_norm_g"], a["ssd_conv_b"], a["ffn_conv_b"])
    chip = 2 * lax.axis_index("x") + lax.axis_index("y")

    shards = [a[n].astype(WIRE_DTYPE) for n in _BIG]
    axes = list(_BIG.values())
    kinds = [_gather_kind(s, ax) for s, ax in zip(shards, axes)]
    full = {n: w.astype(MXU_DTYPE) for n, w in zip(_BIG, _gather_weights(shards, axes, "gather_weights"))}
    conv_shapes = [a[n].shape for n in _CONV]
    gotc = _exchange_chips(_pack([a[n] for n in _CONV], F32, LANES, 8), False, "gather_conv_weights")
    piecesc = [_unpack(gotc[j], conv_shapes) for j in range(N_CHIPS)]
    fullc = {n: jnp.concatenate([piecesc[j][i] for j in range(N_CHIPS)], axis=2) for i, n in enumerate(_CONV)}

    wfull, small = [], []
    for l in range(depth):
        wf = {n: full[n][l] for n in _BIG if n != "w_in"}
        wf["w_in_p"] = _permute_w_in(full["w_in"][l], dm)
        wfull.append(wf)
        small.append({n: (fullc[n][l] if n in _CONV else a[n][l]) for n in _SMALL})

    loss_part, grad_x, grads = _local_step(dm, x, c, target, wfull, small)

    core = lax.axis_index("c").astype(jnp.int32).reshape(1)
    gst = []
    for n, kind in zip(_BIG, kinds):
        per_layer = []
        for l in range(depth):
            g = _unpermute_w_in(grads[l]["w_in_p"], dm) if n == "w_in" else grads[l][n]
            if kind == "slots":
                g = jnp.moveaxis(g.reshape(g.shape[0], N_CHIPS, g.shape[1] // N_CHIPS), 1, 0)
                g = g.reshape(-1, g.shape[-1])
            per_layer.append(g.astype(WIRE_DTYPE))
        gst.append(jnp.stack(per_layer))
    other = _swap_other_layer(gst, "swap_layer_grads")
    parts = [_pair_sum(g, o, core, "pair_sum_" + n) for g, o, n in zip(gst, other, _BIG)]
    parts = [p.reshape(N_CHIPS, -1, p.shape[-1]) if k == "slots" else p for p, k in zip(parts, kinds)]
    got = _scatter_layer(parts, kinds, "scatter_grads")
    mine = [_sum_slots(g, "sum_chip_grads_" + n) for g, n in zip(got, _BIG)]
    theirs = _sibling_swap(mine, "swap_core_grads")

    out = {}
    for i, n in enumerate(_BIG):
        out[n] = _adamw_layers(a[n], a["m_" + n], a["v_" + n], mine[i], theirs[i], core, "adamw_" + n)

    gsmall = [jnp.stack([grads[l][n] for l in range(depth)]) for n in _SMALL]
    small_shapes = [g.shape for g in gsmall]
    summed = _sum_slots(_allgather_all(_pack(gsmall, F32, LANES, 8), "gather_small_grads"), "sum_small_grads")
    gs = dict(zip(_SMALL, _unpack(summed, small_shapes)))
    for n in _CONV:
        wcols = a[n].shape[2]
        gs[n] = lax.dynamic_slice_in_dim(gs[n], chip * wcols, wcols, axis=2)
    local_shapes = [a[n].shape for n in _SMALL]
    res = _adamw(_pack([a[n] for n in _SMALL], F32, LANES, 8), _pack([a["m_" + n] for n in _SMALL], F32, LANES, 8),
                 _pack([a["v_" + n] for n in _SMALL], F32, LANES, 8), _pack([gs[n] for n in _SMALL], F32, LANES, 8),
                 "adamw_small")
    res = [_unpack(r, local_shapes) for r in res]
    for i, n in enumerate(_SMALL):
        out[n] = [r[i] for r in res]

    loss = lax.psum(loss_part, ("x", "y", "c"))
    return (loss, grad_x) + tuple(out[n][k] for k in range(4) for n in _WEIGHTS)


def kernel(x, c, ada_w, ada_b, mix_pre_g, mix_post_g, w_in, ssd_conv_w, ssd_conv_b, ssd_dt_bias, ssd_a_log, ssd_d, ssd_norm_g, w_ssd_out, sc_conv_w, w_sc_out, w_o, ffn_pre_g, ffn_post_g, w_up, ffn_conv_w, ffn_conv_b, w_down, loss_target, m_ada_w, m_ada_b, m_mix_pre_g, m_mix_post_g, m_w_in, m_ssd_conv_w, m_ssd_conv_b, m_ssd_dt_bias, m_ssd_a_log, m_ssd_d, m_ssd_norm_g, m_w_ssd_out, m_sc_conv_w, m_w_sc_out, m_w_o, m_ffn_pre_g, m_ffn_post_g, m_w_up, m_ffn_conv_w, m_ffn_conv_b, m_w_down, v_ada_w, v_ada_b, v_mix_pre_g, v_mix_post_g, v_w_in, v_ssd_conv_w, v_ssd_conv_b, v_ssd_dt_bias, v_ssd_a_log, v_ssd_d, v_ssd_norm_g, v_w_ssd_out, v_sc_conv_w, v_w_sc_out, v_w_o, v_ffn_pre_g, v_ffn_post_g, v_w_up, v_ffn_conv_w, v_ffn_conv_b, v_w_down):
    return _step(dict(zip(_INPUTS, (
        x, c, ada_w, ada_b, mix_pre_g, mix_post_g, w_in, ssd_conv_w, ssd_conv_b, ssd_dt_bias, ssd_a_log, ssd_d, ssd_norm_g, w_ssd_out, sc_conv_w, w_sc_out, w_o, ffn_pre_g, ffn_post_g, w_up, ffn_conv_w, ffn_conv_b, w_down, loss_target, m_ada_w, m_ada_b, m_mix_pre_g, m_mix_post_g, m_w_in, m_ssd_conv_w, m_ssd_conv_b, m_ssd_dt_bias, m_ssd_a_log, m_ssd_d, m_ssd_norm_g, m_w_ssd_out, m_sc_conv_w, m_w_sc_out, m_w_o, m_ffn_pre_g, m_ffn_post_g, m_w_up, m_ffn_conv_w, m_ffn_conv_b, m_w_down, v_ada_w, v_ada_b, v_mix_pre_g, v_mix_post_g, v_w_in, v_ssd_conv_w, v_ssd_conv_b, v_ssd_dt_bias, v_ssd_a_log, v_ssd_d, v_ssd_norm_g, v_w_ssd_out, v_sc_conv_w, v_w_sc_out, v_w_o, v_ffn_pre_g, v_ffn_post_g, v_w_up, v_ffn_conv_w, v_ffn_conv_b, v_w_down))))
```

```python
import math

import jax
import jax.numpy as jnp
from jax import lax
from jax.experimental import pallas as pl
from jax.experimental.pallas import tpu as pltpu

F32 = jnp.float32
MXU_DTYPE = jnp.bfloat16
WIRE_DTYPE = jnp.bfloat16
HI = lax.Precision.HIGHEST
EPS = 1e-6
N_GROUPS = 4
CHUNK = 128
HEAD_DIM = 64
LANES = 128
HALO = 8
N_CHIPS = 4
N_DEV = 8
VMEM_LIMIT = 56 * 1024 * 1024
ADAM_LR, ADAM_B1, ADAM_B2, ADAM_EPS, ADAM_WD, ADAM_STEP = 0.001, 0.9, 0.999, 1e-08, 0.01, 10
MESH = pl.DeviceIdType.MESH

NN = (((1,), (0,)), ((), ()))
NT = (((1,), (1,)), ((), ()))
TN = (((0,), (0,)), ((), ()))


def _dg(a, b, dn, precision=None):
    return lax.dot_general(a, b, dn, precision=precision, preferred_element_type=F32)


def _tile(dim, pref, mult=LANES):
    t = (min(pref, dim) // mult) * mult
    while t >= mult:
        if dim % t == 0:
            return t
        t -= mult
    return dim


def _cp(sem):
    return pltpu.CompilerParams(dimension_semantics=sem, vmem_limit_bytes=VMEM_LIMIT)


def _sigmoid(x):
    return 1.0 / (1.0 + jnp.exp(-x))


def _softplus(x):
    return jnp.maximum(x, 0.0) + jnp.log1p(jnp.exp(-jnp.abs(x)))


def _matmul(a, b, mode, out_dtype, name, tm=1024, tn=1024, tk=1024, slab=None):
    if mode == "NN":
        (M, K), N = a.shape, b.shape[1]
    elif mode == "NT":
        (M, K), N = a.shape, b.shape[0]
    else:
        (K, M), N = a.shape, b.shape[1]
    tm, tn, tk = _tile(M, tm), _tile(N, tn), _tile(K, tk)
    nk = K // tk
    dn = {"NN": NN, "NT": NT, "TN": TN}[mode]
    carry = slab is not None and slab[0] is not None

    def body_one(a_ref, b_ref, *rest):
        o_ref = rest[-1]
        o_ref[...] = _dg(a_ref[...], b_ref[...], dn).astype(o_ref.dtype)

    def body_acc(a_ref, b_ref, *rest):
        o_ref, acc_ref = rest[-2:]
        k = pl.program_id(2)

        @pl.when(k == 0)
        def _():
            acc_ref[...] = jnp.zeros_like(acc_ref)

        acc_ref[...] += _dg(a_ref[...], b_ref[...], dn)

        @pl.when(k == nk - 1)
        def _():
            o_ref[...] = acc_ref[...].astype(o_ref.dtype)

    a_spec = (pl.BlockSpec((tk, tm), lambda i, j, k: (k, i)) if mode == "TN"
              else pl.BlockSpec((tm, tk), lambda i, j, k: (i, k)))
    b_spec = (pl.BlockSpec((tn, tk), lambda i, j, k: (j, k)) if mode == "NT"
              else pl.BlockSpec((tk, tn), lambda i, j, k: (k, j)))
    if slab is None:
        out_spec = pl.BlockSpec((tm, tn), lambda i, j, k: (i, j))
        out_shape = jax.ShapeDtypeStruct((M, N), out_dtype)
    else:
        layer = slab[1]
        out_spec = pl.BlockSpec((None, tm, tn), lambda i, j, k: (layer, i, j))
        out_shape = jax.ShapeDtypeStruct((slab[2], M, N), out_dtype)
    return pl.pallas_call(
        body_one if nk == 1 else body_acc, name=name, grid=(M // tm, N // tn, nk),
        in_specs=[a_spec, b_spec] + ([_ANY] if carry else []),
        out_specs=out_spec, out_shape=out_shape,
        input_output_aliases={2: 0} if carry else {},
        scratch_shapes=[] if nk == 1 else [pltpu.VMEM((tm, tn), F32)],
        compiler_params=_cp(("parallel", "parallel", "arbitrary")),
    )(*((a, b, slab[0]) if carry else (a, b)))


def _ada_fwd(c16, ada_w, ada_b, name):
    rows, D = c16.shape
    N6 = ada_w.shape[1]
    tn = _tile(N6, 1536)

    def body(c_ref, w_ref, b_ref, act_ref, mod_ref):
        c = c_ref[...]
        act = (c * _sigmoid(c)).astype(act_ref.dtype)
        act_ref[...] = act
        mod_ref[...] = _dg(act, w_ref[...], NN) + b_ref[...]

    return pl.pallas_call(
        body, name=name, grid=(N6 // tn,),
        in_specs=[pl.BlockSpec((rows, D), lambda j: (0, 0)),
                  pl.BlockSpec((D, tn), lambda j: (0, j)),
                  pl.BlockSpec((1, tn), lambda j: (0, j))],
        out_specs=[pl.BlockSpec((rows, D), lambda j: (0, 0)),
                   pl.BlockSpec((rows, tn), lambda j: (0, j))],
        out_shape=[jax.ShapeDtypeStruct((rows, D), MXU_DTYPE),
                   jax.ShapeDtypeStruct((rows, N6), F32)],
        compiler_params=_cp(("arbitrary",)),
    )(c16, ada_w, ada_b.reshape(1, N6))


def _norm_mod_rows(x, g, sc, sh):
    r = lax.rsqrt(jnp.mean(x * x, axis=-1, keepdims=True) + EPS)
    return ((x * r) * g) * (1.0 + sc) + sh


def _norm_mod_fwd(x, g, mod, row_sh, name, ts=512):
    Bl, S, D = x.shape
    ts = _tile(S, ts, 8)

    def body(x_ref, g_ref, mod_ref, h_ref):
        sh = mod_ref[row_sh:row_sh + 1, :]
        sc = mod_ref[row_sh + 1:row_sh + 2, :]
        h_ref[...] = _norm_mod_rows(x_ref[...], g_ref[...], sc, sh).astype(h_ref.dtype)

    tok = pl.BlockSpec((None, ts, D), lambda b, s: (b, s, 0))
    return pl.pallas_call(
        body, name=name, grid=(Bl, S // ts),
        in_specs=[tok, pl.BlockSpec((1, D), lambda b, s: (0, 0)),
                  pl.BlockSpec((None, 8, D), lambda b, s: (b, 0, 0))],
        out_specs=tok, out_shape=jax.ShapeDtypeStruct((Bl, S, D), MXU_DTYPE),
        compiler_params=_cp(("parallel", "parallel")),
    )(x, g.reshape(1, D), mod)


def _post_norm_fwd(xp, f, post_g, mod, row_gt, nxt, name, ts=512):
    Bl, S, D = xp.shape
    ts = _tile(S, ts, 8)
    has_next = nxt is not None

    def body(*refs):
        if has_next:
            xp_ref, f_ref, pg_ref, mod_ref, ng_ref, nmod_ref, x_ref, h_ref = refs
        else:
            xp_ref, f_ref, pg_ref, mod_ref, x_ref = refs
        f = f_ref[...]
        r = lax.rsqrt(jnp.mean(f * f, axis=-1, keepdims=True) + EPS)
        x = xp_ref[...] + mod_ref[row_gt:row_gt + 1, :] * ((f * r) * pg_ref[...])
        x_ref[...] = x
        if has_next:
            rs = nxt[2]
            h_ref[...] = _norm_mod_rows(x, ng_ref[...], nmod_ref[rs + 1:rs + 2, :], nmod_ref[rs:rs + 1, :]).astype(h_ref.dtype)

    tok = pl.BlockSpec((None, ts, D), lambda b, s: (b, s, 0))
    vec = pl.BlockSpec((1, D), lambda b, s: (0, 0))
    modspec = pl.BlockSpec((None, 8, D), lambda b, s: (b, 0, 0))
    ins = [xp, f, post_g.reshape(1, D), mod]
    in_specs = [tok, tok, vec, modspec]
    out_specs = [tok]
    out_shape = [jax.ShapeDtypeStruct((Bl, S, D), F32)]
    if has_next:
        ins += [nxt[0].reshape(1, D), nxt[1]]
        in_specs += [vec, modspec]
        out_specs += [tok]
        out_shape += [jax.ShapeDtypeStruct((Bl, S, D), MXU_DTYPE)]
    out = pl.pallas_call(
        body, name=name, grid=(Bl, S // ts), in_specs=in_specs, out_specs=out_specs, out_shape=out_shape,
        compiler_params=_cp(("parallel", "parallel")),
    )(*ins)
    return (out[0], out[1]) if has_next else (out[0], None)


def _loss_fwd_bwd(y, target, name, ts=512):
    Bl, S, D = y.shape
    ts = _tile(S, ts, 8)

    def body(y_ref, t_ref, dy_ref, l_ref):
        @pl.when((pl.program_id(0) == 0) & (pl.program_id(1) == 0))
        def _():
            l_ref[...] = jnp.zeros_like(l_ref)

        e = y_ref[...] - t_ref[...]
        dy_ref[...] = e * (1.0 / D)
        l_ref[...] += 0.5 * jnp.sum(jnp.mean(e * e, axis=-1, keepdims=True), axis=0, keepdims=True)

    tok = pl.BlockSpec((None, ts, D), lambda b, s: (b, s, 0))
    dy, l = pl.pallas_call(
        body, name=name, grid=(Bl, S // ts), in_specs=[tok, tok],
        out_specs=[tok, pl.BlockSpec((8, LANES), lambda b, s: (0, 0))],
        out_shape=[jax.ShapeDtypeStruct((Bl, S, D), F32), jax.ShapeDtypeStruct((8, LANES), F32)],
        compiler_params=_cp(("arbitrary", "arbitrary")),
    )(y, target)
    return dy, l[0, 0]


def _norm_bwd(dx_res, nxt, prv, name, ts=256):
    Bl, S, D = dx_res.shape
    ts = _tile(S, ts, 8)
    has_next, has_prev = nxt is not None, prv is not None

    def body(*refs):
        refs = list(refs)
        dxr_ref = refs.pop(0)
        if has_next:
            dh_ref, x_ref, g_ref, nmod_ref = refs[:4]
            refs = refs[4:]
        if has_prev:
            f_ref, pg_ref, pmod_ref = refs[:3]
            refs = refs[3:]
        dx_ref = refs.pop(0)
        if has_prev:
            df_ref = refs.pop(0)
        pb_ref, sh_ref = refs
        b, s = pl.program_id(0), pl.program_id(1)

        @pl.when(s == 0)
        def _():
            pb_ref[...] = jnp.zeros_like(pb_ref)

        @pl.when((b == 0) & (s == 0))
        def _():
            sh_ref[...] = jnp.zeros_like(sh_ref)

        dx = dxr_ref[...]
        if has_next:
            rs = nxt[4]
            x, dh, g = x_ref[...], dh_ref[...], g_ref[...]
            sc1 = 1.0 + nmod_ref[rs + 1:rs + 2, :]
            r = lax.rsqrt(jnp.mean(x * x, axis=-1, keepdims=True) + EPS)
            xn = x * r
            pb_ref[0:1, :] += jnp.sum(dh, axis=0, keepdims=True)
            pb_ref[1:2, :] += jnp.sum(dh * (xn * g), axis=0, keepdims=True)
            sh_ref[0:1, :] += jnp.sum(dh * sc1 * xn, axis=0, keepdims=True)
            dxn = dh * sc1 * g
            dx = dx + r * (dxn - xn * jnp.mean(dxn * xn, axis=-1, keepdims=True))
        dx_ref[...] = dx
        if has_prev:
            rg = prv[3]
            f, pg = f_ref[...], pg_ref[...]
            gt = pmod_ref[rg:rg + 1, :]
            r = lax.rsqrt(jnp.mean(f * f, axis=-1, keepdims=True) + EPS)
            fn = f * r
            pb_ref[2:3, :] += jnp.sum(dx * (fn * pg), axis=0, keepdims=True)
            drn = dx * gt
            sh_ref[1:2, :] += jnp.sum(drn * fn, axis=0, keepdims=True)
            dfn = drn * pg
            df_ref[...] = (r * (dfn - fn * jnp.mean(dfn * fn, axis=-1, keepdims=True))).astype(df_ref.dtype)

    tok = pl.BlockSpec((None, ts, D), lambda b, s: (b, s, 0))
    vec = pl.BlockSpec((1, D), lambda b, s: (0, 0))
    modspec = pl.BlockSpec((None, 8, D), lambda b, s: (b, 0, 0))
    ins, in_specs = [dx_res], [tok]
    if has_next:
        ins += [nxt[0], nxt[1], nxt[2].reshape(1, D), nxt[3]]
        in_specs += [tok, tok, vec, modspec]
    if has_prev:
        ins += [prv[0], prv[1].reshape(1, D), prv[2]]
        in_specs += [tok, vec, modspec]
    out_specs, out_shape = [tok], [jax.ShapeDtypeStruct((Bl, S, D), F32)]
    if has_prev:
        out_specs += [tok]
        out_shape += [jax.ShapeDtypeStruct((Bl, S, D), MXU_DTYPE)]
    out_specs += [modspec, pl.BlockSpec((8, D), lambda b, s: (0, 0))]
    out_shape += [jax.ShapeDtypeStruct((Bl, 8, D), F32), jax.ShapeDtypeStruct((8, D), F32)]
    out = pl.pallas_call(
        body, name=name, grid=(Bl, S // ts), in_specs=in_specs, out_specs=out_specs, out_shape=out_shape,
        compiler_params=_cp(("arbitrary", "arbitrary")),
    )(*ins)
    if has_prev:
        return out[0], out[1], out[2], out[3]
    return out[0], None, out[1], out[2]


def _shift_down(x, j):
    return x if j == 0 else pltpu.roll(x, j, axis=0)


def _shift_up(x, j):
    return x if j == 0 else pltpu.roll(x, x.shape[0] - j, axis=0)


def _conv(xall, w_ref, K):
    y = w_ref[K - 1:K, :] * xall
    for k in range(K - 1):
        y = y + w_ref[k:k + 1, :] * _shift_down(xall, K - 1 - k)
    return y


def _conv_t(dall, xall, w_ref, dw_ref, K, ts):
    xt = xall[HALO:HALO + ts]
    y = w_ref[K - 1:K, :] * dall
    dw_ref[K - 1:K, :] += jnp.sum(dall[HALO:HALO + ts] * xt, axis=0, keepdims=True)
    for k in range(K - 1):
        sh = _shift_up(dall, K - 1 - k)
        y = y + w_ref[k:k + 1, :] * sh
        dw_ref[k:k + 1, :] += jnp.sum(sh[HALO:HALO + ts] * xt, axis=0, keepdims=True)
    return y


def _halo_specs(ts, W, nS, colblk):
    per = ts // HALO
    tile = pl.BlockSpec((None, ts, W), lambda b, c, s: (b, s, colblk(c)))
    prev = pl.BlockSpec((None, HALO, W), lambda b, c, s: (b, jnp.maximum(s * per - 1, 0), colblk(c)))
    nxt = pl.BlockSpec((None, HALO, W), lambda b, c, s: (b, jnp.minimum((s + 1) * per, nS * per - 1), colblk(c)))
    return tile, prev, nxt


def _masked(ref, keep):
    v = ref[...]
    return jnp.where(keep, v, jnp.zeros_like(v))


def _ssd_conv_fwd(proj, w8, b, off, CD, name, ts=512, W=1024):
    Bl, S, _ = proj.shape
    K = 4
    ts, W = _tile(S, ts, 8), _tile(math.gcd(CD, off), W)
    assert off % W == 0
    nS, nW, ob = S // ts, CD // W, off // W

    def body(x_ref, xp_ref, w_ref, b_ref, o_ref):
        s = pl.program_id(2)
        xall = jnp.concatenate([_masked(xp_ref, s > 0), x_ref[...]], axis=0)
        xc = _conv(xall, w_ref, K)[HALO:] + b_ref[...]
        o_ref[...] = xc * _sigmoid(xc)

    tile, prev, _ = _halo_specs(ts, W, nS, lambda c: ob + c)
    return pl.pallas_call(
        body, name=name, grid=(Bl, nW, nS),
        in_specs=[tile, prev, pl.BlockSpec((8, W), lambda b_, c, s: (0, c)), pl.BlockSpec((1, W), lambda b_, c, s: (0, c))],
        out_specs=pl.BlockSpec((None, ts, W), lambda b_, c, s: (b_, s, c)),
        out_shape=jax.ShapeDtypeStruct((Bl, S, CD), F32),
        compiler_params=_cp(("parallel", "parallel", "parallel")),
    )(proj, proj, w8, b.reshape(1, CD))


def _ssd_conv_bwd(dxa, proj, w8, b, off, CD, name, ts=256, W=1024):
    Bl, S, _ = proj.shape
    K = 4
    ts, W = _tile(S, ts, 8), _tile(math.gcd(CD, off), W)
    nS, nW, ob = S // ts, CD // W, off // W

    def body(d_ref, dn_ref, x_ref, xp_ref, xn_ref, w_ref, b_ref, dx_ref, dw_ref, db_ref):
        bb, s = pl.program_id(1), pl.program_id(2)

        @pl.when((bb == 0) & (s == 0))
        def _():
            dw_ref[...] = jnp.zeros_like(dw_ref)
            db_ref[...] = jnp.zeros_like(db_ref)

        last = s == nS - 1
        xall = jnp.concatenate([_masked(xp_ref, s > 0), x_ref[...], _masked(xn_ref, ~last)], axis=0)
        dall = jnp.concatenate([jnp.zeros((HALO, W), F32), d_ref[...], _masked(dn_ref, ~last)], axis=0)
        xc = _conv(xall, w_ref, K) + b_ref[...]
        sg = _sigmoid(xc)
        dxc = dall * (sg * (1.0 + xc * (1.0 - sg)))
        dx_ref[...] = _conv_t(dxc, xall, w_ref, dw_ref, K, ts)[HALO:HALO + ts].astype(dx_ref.dtype)
        db_ref[0:1, :] += jnp.sum(dxc[HALO:HALO + ts], axis=0, keepdims=True)

    per = ts // HALO
    dtile_s = pl.BlockSpec((None, ts, W), lambda c, b_, s: (b_, s, c))
    dnext_s = pl.BlockSpec((None, HALO, W), lambda c, b_, s: (b_, jnp.minimum((s + 1) * per, nS * per - 1), c))
    xtile_s = pl.BlockSpec((None, ts, W), lambda c, b_, s: (b_, s, ob + c))
    xprev_s = pl.BlockSpec((None, HALO, W), lambda c, b_, s: (b_, jnp.maximum(s * per - 1, 0), ob + c))
    xnext_s = pl.BlockSpec((None, HALO, W), lambda c, b_, s: (b_, jnp.minimum((s + 1) * per, nS * per - 1), ob + c))
    wspec = pl.BlockSpec((8, W), lambda c, b_, s: (0, c))
    return pl.pallas_call(
        body, name=name, grid=(nW, Bl, nS),
        in_specs=[dtile_s, dnext_s, xtile_s, xprev_s, xnext_s, wspec, pl.BlockSpec((1, W), lambda c, b_, s: (0, c))],
        out_specs=[dtile_s, wspec, wspec],
        out_shape=[jax.ShapeDtypeStruct((Bl, S, CD), MXU_DTYPE), jax.ShapeDtypeStruct((8, CD), F32),
                   jax.ShapeDtypeStruct((8, CD), F32)],
        compiler_params=_cp(("arbitrary", "arbitrary", "arbitrary")),
    )(dxa, dxa, proj, proj, proj, w8, b.reshape(1, CD))


def _sc_conv_fwd(proj, w8, offs, D, name, ts=512, W=1024):
    Bl, S, _ = proj.shape
    K = 3
    ts, W = _tile(S, ts, 8), _tile(D, W)
    nS, nW = S // ts, D // W
    ob, oc, oh = [o // W for o in offs]

    def body(b_ref, c_ref, cp_ref, h_ref, hp_ref, w_ref, o_ref):
        s = pl.program_id(2)
        keep = s > 0
        vall = jnp.concatenate([_masked(cp_ref, keep) * _masked(hp_ref, keep), c_ref[...] * h_ref[...]], axis=0)
        o_ref[...] = (b_ref[...] * _conv(vall, w_ref, K)[HALO:]).astype(o_ref.dtype)

    tb, _, _ = _halo_specs(ts, W, nS, lambda c: ob + c)
    tc, pc, _ = _halo_specs(ts, W, nS, lambda c: oc + c)
    th, ph, _ = _halo_specs(ts, W, nS, lambda c: oh + c)
    return pl.pallas_call(
        body, name=name, grid=(Bl, nW, nS),
        in_specs=[tb, tc, pc, th, ph, pl.BlockSpec((8, W), lambda b_, c, s: (0, c))],
        out_specs=pl.BlockSpec((None, ts, W), lambda b_, c, s: (b_, s, c)),
        out_shape=jax.ShapeDtypeStruct((Bl, S, D), MXU_DTYPE),
        compiler_params=_cp(("parallel", "parallel", "parallel")),
    )(proj, proj, proj, proj, proj, w8)


def _sc_conv_bwd(ds, proj, w8, offs, D, name, ts=256, W=1024):
    Bl, S, _ = proj.shape
    K = 3
    ts, W = _tile(S, ts, 8), _tile(D, W)
    nS, nW = S // ts, D // W
    ob, oc, oh = [o // W for o in offs]

    def body(d_ref, dn_ref, b_ref, bn_ref, c_ref, cp_ref, cn_ref, h_ref, hp_ref, hn_ref, w_ref,
             db_ref, dc_ref, dh_ref, dw_ref):
        bb, s = pl.program_id(1), pl.program_id(2)

        @pl.when((bb == 0) & (s == 0))
        def _():
            dw_ref[...] = jnp.zeros_like(dw_ref)

        first, last = s > 0, s < nS - 1
        zeros = jnp.zeros((HALO, W), F32)
        c_t, h_t = c_ref[...], h_ref[...]
        vall = jnp.concatenate([_masked(cp_ref, first) * _masked(hp_ref, first), c_t * h_t,
                                _masked(cn_ref, last) * _masked(hn_ref, last)], axis=0)
        dcv = jnp.concatenate([zeros, d_ref[...] * b_ref[...], _masked(dn_ref, last) * _masked(bn_ref, last)], axis=0)
        cv = _conv(vall, w_ref, K)[HALO:HALO + ts]
        db_ref[...] = (d_ref[...] * cv).astype(db_ref.dtype)
        dv = _conv_t(dcv, vall, w_ref, dw_ref, K, ts)[HALO:HALO + ts]
        dc_ref[...] = (dv * h_t).astype(dc_ref.dtype)
        dh_ref[...] = (dv * c_t).astype(dh_ref.dtype)

    per = ts // HALO

    def specs(o):
        t = pl.BlockSpec((None, ts, W), lambda c, b_, s: (b_, s, o + c))
        p = pl.BlockSpec((None, HALO, W), lambda c, b_, s: (b_, jnp.maximum(s * per - 1, 0), o + c))
        n = pl.BlockSpec((None, HALO, W), lambda c, b_, s: (b_, jnp.minimum((s + 1) * per, nS * per - 1), o + c))
        return t, p, n

    dt_, _, dn_ = specs(0)
    bt, _, bn = specs(ob)
    ct, cp, cn = specs(oc)
    ht, hp, hn = specs(oh)
    wspec = pl.BlockSpec((8, W), lambda c, b_, s: (0, c))
    act = jax.ShapeDtypeStruct((Bl, S, D), MXU_DTYPE)
    return pl.pallas_call(
        body, name=name, grid=(nW, Bl, nS),
        in_specs=[dt_, dn_, bt, bn, ct, cp, cn, ht, hp, hn, wspec],
        out_specs=[dt_, dt_, dt_, wspec],
        out_shape=[act, act, act, jax.ShapeDtypeStruct((8, D), F32)],
        compiler_params=_cp(("arbitrary", "arbitrary", "arbitrary")),
    )(ds, ds, proj, proj, proj, proj, proj, proj, proj, proj, w8)


def _ffn_conv_fwd(up, w8, b, DFF, name, ts=512, W=1408):
    Bl, S, _ = up.shape
    K = 3
    ts, W = _tile(S, ts, 8), _tile(DFF, W)
    nS, nW = S // ts, DFF // W

    def body(g_ref, gp_ref, v_ref, vp_ref, wg_ref, wv_ref, bg_ref, bv_ref, o_ref):
        keep = pl.program_id(2) > 0
        ug = _conv(jnp.concatenate([_masked(gp_ref, keep), g_ref[...]], axis=0), wg_ref, K)[HALO:] + bg_ref[...]
        uv = _conv(jnp.concatenate([_masked(vp_ref, keep), v_ref[...]], axis=0), wv_ref, K)[HALO:] + bv_ref[...]
        o_ref[...] = (ug * _sigmoid(ug) * uv).astype(o_ref.dtype)

    tg, pg, _ = _halo_specs(ts, W, nS, lambda c: c)
    tv, pv, _ = _halo_specs(ts, W, nS, lambda c: nW + c)
    wg = pl.BlockSpec((8, W), lambda b_, c, s: (0, c))
    wv = pl.BlockSpec((8, W), lambda b_, c, s: (0, nW + c))
    bg = pl.BlockSpec((1, W), lambda b_, c, s: (0, c))
    bv = pl.BlockSpec((1, W), lambda b_, c, s: (0, nW + c))
    b2 = b.reshape(1, 2 * DFF)
    return pl.pallas_call(
        body, name=name, grid=(Bl, nW, nS),
        in_specs=[tg, pg, tv, pv, wg, wv, bg, bv],
        out_specs=pl.BlockSpec((None, ts, W), lambda b_, c, s: (b_, s, c)),
        out_shape=jax.ShapeDtypeStruct((Bl, S, DFF), MXU_DTYPE),
        compiler_params=_cp(("parallel", "parallel", "parallel")),
    )(up, up, up, up, w8, w8, b2, b2)


def _ffn_conv_bwd(da, up, w8, b, DFF, name, ts=256, W=1408):
    Bl, S, _ = up.shape
    K = 3
    ts, W = _tile(S, ts, 8), _tile(DFF, W)
    nS, nW = S // ts, DFF // W

    def body(d_ref, dn_ref, g_ref, gp_ref, gn_ref, v_ref, vp_ref, vn_ref, wg_ref, wv_ref, bg_ref, bv_ref,
             dg_ref, dv_ref, dwg_ref, dwv_ref, dbg_ref, dbv_ref):
        bb, s = pl.program_id(1), pl.program_id(2)

        @pl.when((bb == 0) & (s == 0))
        def _():
            for r in (dwg_ref, dwv_ref, dbg_ref, dbv_ref):
                r[...] = jnp.zeros_like(r)

        first, last = s > 0, s < nS - 1
        gall = jnp.concatenate([_masked(gp_ref, first), g_ref[...], _masked(gn_ref, last)], axis=0)
        vall = jnp.concatenate([_masked(vp_ref, first), v_ref[...], _masked(vn_ref, last)], axis=0)
        dall = jnp.concatenate([jnp.zeros((HALO, W), F32), d_ref[...], _masked(dn_ref, last)], axis=0)
        ug = _conv(gall, wg_ref, K) + bg_ref[...]
        uv = _conv(vall, wv_ref, K) + bv_ref[...]
        sg = _sigmoid(ug)
        dug = dall * uv * (sg * (1.0 + ug * (1.0 - sg)))
        duv = dall * (ug * sg)
        dg_ref[...] = _conv_t(dug, gall, wg_ref, dwg_ref, K, ts)[HALO:HALO + ts].astype(dg_ref.dtype)
        dv_ref[...] = _conv_t(duv, vall, wv_ref, dwv_ref, K, ts)[HALO:HALO + ts].astype(dv_ref.dtype)
        dbg_ref[0:1, :] += jnp.sum(dug[HALO:HALO + ts], axis=0, keepdims=True)
        dbv_ref[0:1, :] += jnp.sum(duv[HALO:HALO + ts], axis=0, keepdims=True)

    per = ts // HALO

    def specs(o):
        t = pl.BlockSpec((None, ts, W), lambda c, b_, s: (b_, s, o + c))
        p = pl.BlockSpec((None, HALO, W), lambda c, b_, s: (b_, jnp.maximum(s * per - 1, 0), o + c))
        n = pl.BlockSpec((None, HALO, W), lambda c, b_, s: (b_, jnp.minimum((s + 1) * per, nS * per - 1), o + c))
        return t, p, n

    dt_, _, dn_ = specs(0)
    gt, gp, gn = specs(0)
    vt, vp, vn = specs(nW)
    wg = pl.BlockSpec((8, W), lambda c, b_, s: (0, c))
    wv = pl.BlockSpec((8, W), lambda c, b_, s: (0, nW + c))
    bg = pl.BlockSpec((1, W), lambda c, b_, s: (0, c))
    bv = pl.BlockSpec((1, W), lambda c, b_, s: (0, nW + c))
    b2 = b.reshape(1, 2 * DFF)
    act = jax.ShapeDtypeStruct((Bl, S, DFF), MXU_DTYPE)
    small = jax.ShapeDtypeStruct((8, DFF), F32)
    dg, dv, dwg, dwv, dbg, dbv = pl.pallas_call(
        body, name=name, grid=(nW, Bl, nS),
        in_specs=[dt_, dn_, gt, gp, gn, vt, vp, vn, wg, wv, bg, bv],
        out_specs=[dt_, dt_, wg, wg, wg, wg],
        out_shape=[act, act, small, small, small, small],
        compiler_params=_cp(("arbitrary", "arbitrary", "arbitrary")),
    )(da, da, up, up, up, up, up, up, w8, w8, b2, b2)
    return dg, dv, jnp.concatenate([dwg, dwv], axis=1), jnp.concatenate([dbg, dbv], axis=1)


def _merge_fwd(proj, y_ssd, y_sc, off, D, name, ts=512):
    Bl, S, _ = proj.shape
    ts = _tile(S, ts, 8)
    og = off // D

    def body(g1_ref, g2_ref, a_ref, b_ref, o_ref):
        o_ref[...] = (_sigmoid(g1_ref[...]) * a_ref[...] + _sigmoid(g2_ref[...]) * b_ref[...]).astype(o_ref.dtype)

    tok = pl.BlockSpec((None, ts, D), lambda b, s: (b, s, 0))
    return pl.pallas_call(
        body, name=name, grid=(Bl, S // ts),
        in_specs=[pl.BlockSpec((None, ts, D), lambda b, s: (b, s, og)),
                  pl.BlockSpec((None, ts, D), lambda b, s: (b, s, og + 1)), tok, tok],
        out_specs=tok, out_shape=jax.ShapeDtypeStruct((Bl, S, D), MXU_DTYPE),
        compiler_params=_cp(("parallel", "parallel")),
    )(proj, proj, y_ssd, y_sc)


def _merge_bwd(dmixin, proj, y_ssd, y_sc, off, D, name, ts=512):
    Bl, S, _ = proj.shape
    ts = _tile(S, ts, 8)
    og = off // D

    def body(d_ref, g1_ref, g2_ref, a_ref, b_ref, da_ref, db_ref, dg1_ref, dg2_ref):
        d = d_ref[...]
        s1, s2 = _sigmoid(g1_ref[...]), _sigmoid(g2_ref[...])
        da_ref[...] = (d * s1).astype(da_ref.dtype)
        db_ref[...] = (d * s2).astype(db_ref.dtype)
        dg1_ref[...] = (d * a_ref[...] * (s1 * (1.0 - s1))).astype(dg1_ref.dtype)
        dg2_ref[...] = (d * b_ref[...] * (s2 * (1.0 - s2))).astype(dg2_ref.dtype)

    tok = pl.BlockSpec((None, ts, D), lambda b, s: (b, s, 0))
    act = jax.ShapeDtypeStruct((Bl, S, D), MXU_DTYPE)
    return pl.pallas_call(
        body, name=name, grid=(Bl, S // ts),
        in_specs=[tok, pl.BlockSpec((None, ts, D), lambda b, s: (b, s, og)),
                  pl.BlockSpec((None, ts, D), lambda b, s: (b, s, og + 1)), tok, tok],
        out_specs=[tok, tok, tok, tok], out_shape=[act, act, act, act],
        compiler_params=_cp(("parallel", "parallel")),
    )(dmixin, proj, proj, y_ssd, y_sc)


def _two_term_dot(v, m01):
    hi = v.astype(MXU_DTYPE)
    lo = (v - hi.astype(F32)).astype(MXU_DTYPE)
    return _dg(hi, m01, NN) + _dg(lo, m01, NN)


def _head_to_channels(R):
    rp = R * HEAD_DIM
    return (lax.shift_right_logical(lax.broadcasted_iota(jnp.int32, (LANES, rp), 1), 6)
            == lax.broadcasted_iota(jnp.int32, (LANES, rp), 0)).astype(MXU_DTYPE)


def _expand_heads(mats, R):
    L = mats[0].shape[0]
    out = _two_term_dot(jnp.concatenate(mats, axis=0), _head_to_channels(R))
    return [out[i * L:(i + 1) * L, :] for i in range(len(mats))]


def _row_sums(v):
    return _two_term_dot(v, jnp.ones((v.shape[1], LANES), MXU_DTYPE))


def _lanes_to(v, width):
    return jnp.concatenate([v] * (width // LANES), axis=1)


def _head_rows(colvec, R, N):
    return jnp.concatenate([jnp.broadcast_to(colvec[r:r + 1, :], (HEAD_DIM, N)) for r in range(R)], axis=0)


def _ssd_prep(proj, dt_bias, a_log, selcat, selbig, dm, name):
    Bl, S, L, G = dm.Bl, dm.S, CHUNK, N_GROUPS
    nc = S // L
    odt = dm.ODT // LANES

    def body(dtr_ref, dtb_ref, alog_ref, selcat_ref, selbig_ref, dtg_ref, acs_ref, acst_ref):
        row = lax.broadcasted_iota(jnp.int32, (L, L), 0)
        col = lax.broadcasted_iota(jnp.int32, (L, L), 1)
        dt_all = _softplus(dtr_ref[...] + dtb_ref[...])
        acs_all = _dg((row >= col).astype(F32), dt_all * (-jnp.exp(alog_ref[...])), NN, HI)
        dtg_ref[...] = _dg(dt_all, selcat_ref[...], NN, HI)
        acs_ref[...] = _dg(acs_all, selcat_ref[...], NN, HI)
        acst_ref[...] = _dg(selbig_ref[...], acs_all, NT, HI)

    vec = pl.BlockSpec((1, LANES), lambda b, c: (0, 0))
    wide = pl.BlockSpec((None, L, G * LANES), lambda b, c: (b, c, 0))
    return pl.pallas_call(
        body, name=name, grid=(Bl, nc),
        in_specs=[pl.BlockSpec((None, L, LANES), lambda b, c: (b, c, odt)), vec, vec,
                  pl.BlockSpec((LANES, G * LANES), lambda b, c: (0, 0)),
                  pl.BlockSpec((G * 8, LANES), lambda b, c: (0, 0))],
        out_specs=[wide, wide, pl.BlockSpec((None, None, G * 8, L), lambda b, c: (b, c, 0, 0))],
        out_shape=[jax.ShapeDtypeStruct((Bl, S, G * LANES), F32), jax.ShapeDtypeStruct((Bl, S, G * LANES), F32),
                   jax.ShapeDtypeStruct((Bl, nc, G * 8, L), F32)],
        compiler_params=_cp(("parallel", "parallel")),
    )(proj, dt_bias, a_log, selcat, selbig)


def _ssd_post(d_a, ddt, dtg, proj, dt_bias, a_log, selcat, dm, name):
    Bl, S, L, G = dm.Bl, dm.S, CHUNK, N_GROUPS
    nc = S // L
    odt = dm.ODT // LANES

    def body(da_ref, ddt_ref, dtg_ref, dtr_ref, dtb_ref, alog_ref, selcat_ref, ddtr_ref, dpar_ref):
        @pl.when((pl.program_id(0) == 0) & (pl.program_id(1) == 0))
        def _():
            dpar_ref[...] = jnp.zeros_like(dpar_ref)

        row = lax.broadcasted_iota(jnp.int32, (L, L), 0)
        col = lax.broadcasted_iota(jnp.int32, (L, L), 1)
        selcat = selcat_ref[...]
        a_all = -jnp.exp(alog_ref[...])
        a4 = _dg(jnp.broadcast_to(a_all, (8, LANES)), selcat, NN, HI)[0:1, :]
        dadt = _dg((col >= row).astype(F32), da_ref[...], NN, HI)
        ddt4 = ddt_ref[...] + dadt * a4
        da4 = jnp.sum(dadt * dtg_ref[...], axis=0, keepdims=True)
        ddt_all = _dg(ddt4, selcat, NT, HI)
        da_all = _dg(jnp.broadcast_to(da4, (8, G * LANES)), selcat, NT, HI)[0:1, :]
        ddtr = ddt_all * _sigmoid(dtr_ref[...] + dtb_ref[...])
        ddtr_ref[...] = ddtr.astype(ddtr_ref.dtype)
        dpar_ref[0:1, :] += jnp.sum(ddtr, axis=0, keepdims=True)
        dpar_ref[1:2, :] += da_all * a_all

    vec = pl.BlockSpec((1, LANES), lambda b, c: (0, 0))
    wide = pl.BlockSpec((None, L, G * LANES), lambda b, c: (b, c, 0))
    return pl.pallas_call(
        body, name=name, grid=(Bl, nc),
        in_specs=[wide, wide, wide, pl.BlockSpec((None, L, LANES), lambda b, c: (b, c, odt)), vec, vec,
                  pl.BlockSpec((LANES, G * LANES), lambda b, c: (0, 0))],
        out_specs=[pl.BlockSpec((None, L, LANES), lambda b, c: (b, c, 0)), pl.BlockSpec((8, LANES), lambda b, c: (0, 0))],
        out_shape=[jax.ShapeDtypeStruct((Bl, S, LANES), MXU_DTYPE), jax.ShapeDtypeStruct((8, LANES), F32)],
        compiler_params=_cp(("arbitrary", "arbitrary")),
    )(d_a, ddt, dtg, proj, dt_bias, a_log, selcat)


def _scan_fwd(xbc_a, proj, dtg4, acs4, acs_t4, dsk_exp, norm_g, dm, name):
    Bl, S, DI, N, R, L, G = dm.Bl, dm.S, dm.DI, dm.N, dm.R, CHUNK, N_GROUPS
    RP = R * HEAD_DIM
    nc = S // L
    ob, ocm = DI // N, DI // N + G

    def body(xs_ref, bm_ref, cm_ref, z_ref, dtg_ref, acs_ref, acst_ref, dsk_ref, ng_ref, y_ref, yn_ref, hp_ref, h_ref):
        @pl.when(pl.program_id(2) == 0)
        def _():
            h_ref[...] = jnp.zeros_like(h_ref)

        causal = lax.broadcasted_iota(jnp.int32, (L, L), 0) >= lax.broadcasted_iota(jnp.int32, (L, L), 1)
        lane = lax.broadcasted_iota(jnp.int32, (L, LANES), 1)
        dtg, acs, acs_t = dtg_ref[...], acs_ref[...], acst_ref[...]
        xs = xs_ref[...]
        bmb, cmb = bm_ref[...].astype(MXU_DTYPE), cm_ref[...].astype(MXU_DTYPE)
        sg = _dg(cmb, bmb, NT)
        acs_last = acs[L - 1:L, :]
        dt_exp, ea_exp, ds_exp = _expand_heads([dtg, jnp.exp(acs), jnp.exp(acs_last - acs)], R)
        xdt = xs * dt_exp
        xb = xdt.astype(MXU_DTYPE)
        parts = []
        for q in range(R // 2):
            x2 = xb[:, LANES * q:LANES * (q + 1)]
            ys = []
            for r in (2 * q, 2 * q + 1):
                dec = jnp.exp(jnp.where(causal, acs[:, r:r + 1] - acs_t[r:r + 1, :], -1e30))
                ys.append(_dg((sg * dec).astype(MXU_DTYPE), x2, NN))
            parts.append(jnp.where(lane < HEAD_DIM, ys[0], ys[1]))
        ydiag = jnp.concatenate(parts, axis=1)
        h_cur = h_ref[...]
        hb = h_cur.astype(MXU_DTYPE)
        yoff = _dg(cmb, hb, NT) * ea_exp
        st = _dg((xdt * ds_exp).astype(MXU_DTYPE), bmb, TN)
        hp_ref[...] = hb
        h_ref[...] = h_cur * _head_rows(jnp.exp(acs_t[:, L - 1:L]), R, N) + st
        y = ydiag + yoff + dsk_ref[...] * xs
        y_ref[...] = y
        z = z_ref[...]
        yg = y * (z * _sigmoid(z))
        rr = lax.rsqrt(_row_sums(yg * yg) * (1.0 / RP) + EPS)
        yn_ref[...] = (yg * _lanes_to(rr, RP) * ng_ref[...]).astype(yn_ref.dtype)

    grp = pl.BlockSpec((None, L, RP), lambda b, g, c: (b, c, g))
    lanes = pl.BlockSpec((None, L, LANES), lambda b, g, c: (b, c, g))
    chan = pl.BlockSpec((1, RP), lambda b, g, c: (0, g))
    return pl.pallas_call(
        body, name=name, grid=(Bl, G, nc),
        in_specs=[grp,
                  pl.BlockSpec((None, L, N), lambda b, g, c: (b, c, ob + g)),
                  pl.BlockSpec((None, L, N), lambda b, g, c: (b, c, ocm + g)),
                  grp, lanes, lanes,
                  pl.BlockSpec((None, None, 8, L), lambda b, g, c: (b, c, g, 0)),
                  chan, chan],
        out_specs=[grp, grp, pl.BlockSpec((None, None, None, RP, N), lambda b, g, c: (b, g, c, 0, 0))],
        out_shape=[jax.ShapeDtypeStruct((Bl, S, DI), F32), jax.ShapeDtypeStruct((Bl, S, DI), MXU_DTYPE),
                   jax.ShapeDtypeStruct((Bl, G, nc, RP, N), MXU_DTYPE)],
        scratch_shapes=[pltpu.VMEM((RP, N), F32)],
        compiler_params=_cp(("parallel", "parallel", "arbitrary")),
    )(xbc_a, xbc_a, xbc_a, proj, dtg4, acs4, acs_t4, dsk_exp, norm_g.reshape(1, DI))


def _scan_bwd(dyn, y, xbc_a, proj, hprev, dtg4, acs4, acs_t4, dsk_exp, norm_g, dm, name):
    Bl, S, DI, N, R, L, G = dm.Bl, dm.S, dm.DI, dm.N, dm.R, CHUNK, N_GROUPS
    RP = R * HEAD_DIM
    nc = S // L
    ob, ocm = DI // N, DI // N + G

    def body(dyn_ref, y_ref, z_ref, xs_ref, bm_ref, cm_ref, hp_ref, dtg_ref, acs_ref, acst_ref, dsk_ref, ng_ref,
             dz_ref, dxs_ref, dbm_ref, dcm_ref, da_ref, ddt_ref, ddsk_ref, dng_ref, dh_ref):
        b, c = pl.program_id(1), pl.program_id(2)

        @pl.when(c == 0)
        def _():
            dh_ref[...] = jnp.zeros_like(dh_ref)

        @pl.when((b == 0) & (c == 0))
        def _():
            ddsk_ref[...] = jnp.zeros_like(ddsk_ref)
            dng_ref[...] = jnp.zeros_like(dng_ref)

        row = lax.broadcasted_iota(jnp.int32, (L, L), 0)
        col = lax.broadcasted_iota(jnp.int32, (L, L), 1)
        causal, anti = row >= col, col >= row
        lane = lax.broadcasted_iota(jnp.int32, (L, LANES), 1)
        etb = (lax.shift_right_logical(lax.broadcasted_iota(jnp.int32, (RP, LANES), 0), 6)
               == lax.broadcasted_iota(jnp.int32, (RP, LANES), 1)).astype(MXU_DTYPE)

        dtg, acs, acs_t = dtg_ref[...], acs_ref[...], acst_ref[...]
        xs, z, y, dyn = xs_ref[...], z_ref[...], y_ref[...], dyn_ref[...]
        bmb, cmb = bm_ref[...].astype(MXU_DTYPE), cm_ref[...].astype(MXU_DTYPE)
        hpb = hp_ref[...]
        ng = ng_ref[...]

        sz = _sigmoid(z)
        siluz = z * sz
        yg = y * siluz
        rr = _lanes_to(lax.rsqrt(_row_sums(yg * yg) * (1.0 / RP) + EPS), RP)
        yhat = yg * rr
        dng_ref[...] += jnp.sum(dyn * yhat, axis=0, keepdims=True)
        dyhat = dyn * ng
        dyg = rr * (dyhat - yhat * _lanes_to(_row_sums(dyhat * yhat) * (1.0 / RP), RP))
        dy = dyg * siluz
        dz_ref[...] = (dyg * y * (sz * (1.0 + z * (1.0 - sz)))).astype(dz_ref.dtype)

        dxs = dy * dsk_ref[...]
        ddsk_ref[...] += jnp.sum(dy * xs, axis=0, keepdims=True)

        acs_last = acs[L - 1:L, :]
        dt_exp, ea_exp, ds_exp = _expand_heads([dtg, jnp.exp(acs), jnp.exp(acs_last - acs)], R)
        xdt = xs * dt_exp
        xb = xdt.astype(MXU_DTYPE)
        dyb = dy.astype(MXU_DTYPE)
        cd = jnp.exp(acs_last)
        cd_rows = _head_rows(jnp.exp(acs_t[:, L - 1:L]), R, N)

        q_ = _dg(cmb, hpb, NT)
        dq = dy * ea_exp
        dqb = dq.astype(MXU_DTYPE)
        dcm = _dg(dqb, hpb, NN)
        dh_yoff = _dg(dqb, cmb, TN)

        dhn = dh_ref[...]
        wprod = dhn * hpb.astype(F32)
        per_head = jnp.concatenate(
            [jnp.sum(wprod[HEAD_DIM * r:HEAD_DIM * (r + 1), :], axis=0, keepdims=True) for r in range(R)]
            + ([jnp.zeros((8 - R, N), F32)] if R < 8 else []), axis=0)
        dcd_col = jnp.sum(per_head, axis=1, keepdims=True)
        diag8 = lax.broadcasted_iota(jnp.int32, (8, LANES), 0) == lax.broadcasted_iota(jnp.int32, (8, LANES), 1)
        dcd_lane = jnp.sum(jnp.where(diag8, dcd_col, 0.0), axis=0, keepdims=True)
        d_a_last = dcd_lane * cd
        dh_ref[...] = dhn * cd_rows + dh_yoff
        dhnb = dhn.astype(MXU_DTYPE)

        e_ = _dg(bmb, dhnb, NT)
        dxdt = ds_exp * e_
        xds = xdt * ds_exp
        dbm = _dg(xds.astype(MXU_DTYPE), dhnb, NN)

        sg = _dg(cmb, bmb, NT)
        sg_t = _dg(bmb, cmb, NT)
        dsg = jnp.zeros((L, L), F32)
        dsg_t = jnp.zeros((L, L), F32)
        d_a = jnp.zeros((L, LANES), F32)
        parts = []
        for q in range(R // 2):
            x2 = xb[:, LANES * q:LANES * (q + 1)]
            dy2 = dyb[:, LANES * q:LANES * (q + 1)]
            dxs2 = []
            for hh, r in enumerate((2 * q, 2 * q + 1)):
                mine = (lane < HEAD_DIM) if hh == 0 else (lane >= HEAD_DIM)
                diff = acs[:, r:r + 1] - acs_t[r:r + 1, :]
                dec = jnp.exp(jnp.where(causal, diff, -1e30))
                dec_t = jnp.exp(jnp.where(anti, -diff, -1e30))
                dy2m = jnp.where(mine, dy2, jnp.zeros_like(dy2))
                dm_ = _dg(dy2m, x2, NT)
                dm_t = _dg(x2, dy2m, NT)
                m_t = sg_t * dec_t
                d_a = d_a + jnp.where(lane == r, _row_sums(dm_ * (sg * dec) - dm_t * m_t), 0.0)
                dsg = dsg + dm_ * dec
                dsg_t = dsg_t + dm_t * dec_t
                dxs2.append(_dg(m_t.astype(MXU_DTYPE), dy2, NN))
            parts.append(jnp.where(lane < HEAD_DIM, dxs2[0], dxs2[1]))
        dxdt = dxdt + jnp.concatenate(parts, axis=1)
        dcm_ref[...] = dcm + _dg(dsg.astype(MXU_DTYPE), bmb, NN)
        dbm_ref[...] = dbm + _dg(dsg_t.astype(MXU_DTYPE), cmb, NN)
        dxs_ref[...] = dxs + dxdt * dt_exp

        hs = _two_term_dot(jnp.concatenate([dq * q_ - xds * e_, xds * e_, dxdt * xs], axis=0), etb)
        t2 = hs[L:2 * L, :]
        rowl = lax.broadcasted_iota(jnp.int32, (L, LANES), 0)
        d_a_last = d_a_last + jnp.sum(t2, axis=0, keepdims=True)
        da_ref[...] = d_a + hs[0:L, :] + jnp.where(rowl == L - 1, d_a_last, 0.0)
        ddt_ref[...] = hs[2 * L:3 * L, :]

    grp = pl.BlockSpec((None, L, RP), lambda g, b, c: (b, nc - 1 - c, g))
    lanes = pl.BlockSpec((None, L, LANES), lambda g, b, c: (b, nc - 1 - c, g))
    bspec = pl.BlockSpec((None, L, N), lambda g, b, c: (b, nc - 1 - c, ob + g))
    cspec = pl.BlockSpec((None, L, N), lambda g, b, c: (b, nc - 1 - c, ocm + g))
    gn = pl.BlockSpec((None, L, N), lambda g, b, c: (b, nc - 1 - c, g))
    chan = pl.BlockSpec((1, RP), lambda g, b, c: (0, g))
    wide = jax.ShapeDtypeStruct((Bl, S, G * LANES), F32)
    return pl.pallas_call(
        body, name=name, grid=(G, Bl, nc),
        in_specs=[grp, grp, grp, grp, bspec, cspec,
                  pl.BlockSpec((None, None, None, RP, N), lambda g, b, c: (b, g, nc - 1 - c, 0, 0)),
                  lanes, lanes, pl.BlockSpec((None, None, 8, L), lambda g, b, c: (b, nc - 1 - c, g, 0)),
                  chan, chan],
        out_specs=[grp, grp, gn, gn, lanes, lanes, chan, chan],
        out_shape=[jax.ShapeDtypeStruct((Bl, S, DI), MXU_DTYPE), jax.ShapeDtypeStruct((Bl, S, DI), F32),
                   jax.ShapeDtypeStruct((Bl, S, G * N), F32), jax.ShapeDtypeStruct((Bl, S, G * N), F32),
                   wide, wide, jax.ShapeDtypeStruct((1, DI), F32), jax.ShapeDtypeStruct((1, DI), F32)],
        scratch_shapes=[pltpu.VMEM((RP, N), F32)],
        compiler_params=_cp(("arbitrary", "arbitrary", "arbitrary")),
    )(dyn, y, proj, xbc_a, xbc_a, xbc_a, hprev, dtg4, acs4, acs_t4, dsk_exp, norm_g.reshape(1, DI))


def _adam_update(w, m, v, g):
    c1 = 1.0 - ADAM_B1 ** ADAM_STEP
    c2 = 1.0 - ADAM_B2 ** ADAM_STEP
    nm = ADAM_B1 * m + (1.0 - ADAM_B1) * g
    nv = ADAM_B2 * v + (1.0 - ADAM_B2) * (g * g)
    return nm, nv, -ADAM_LR * ((nm / c1) / (jnp.sqrt(nv / c2) + ADAM_EPS) + ADAM_WD * w)


def _adamw(w, m, v, g, name, tr=128):
    rows, cols = w.shape
    tr = _tile(rows, tr, 8)

    def body(w_ref, m_ref, v_ref, ga_ref, g_ref, d_ref, nm_ref, nv_ref):
        g = ga_ref[...]
        g_ref[...] = g
        nm_ref[...], nv_ref[...], d_ref[...] = _adam_update(w_ref[...], m_ref[...], v_ref[...], g)

    blk = pl.BlockSpec((tr, cols), lambda i: (i, 0))
    shp = jax.ShapeDtypeStruct((rows, cols), F32)
    return pl.pallas_call(
        body, name=name, grid=(rows // tr,), in_specs=[blk] * 4, out_specs=[blk] * 4,
        out_shape=[shp] * 4, compiler_params=_cp(("parallel",)),
    )(w, m, v, g)


def _adamw_layers(w, m, v, g_mine, g_theirs, core, name, tr=128):
    _, rows, cols = w.shape
    tr = _tile(rows, tr, 8)

    def body(c_ref, w_ref, m_ref, v_ref, ga_ref, gb_ref, g_ref, d_ref, nm_ref, nv_ref):
        g = jnp.where(pl.program_id(0) == c_ref[0], ga_ref[...], gb_ref[...])
        g_ref[...] = g
        nm_ref[...], nv_ref[...], d_ref[...] = _adam_update(w_ref[...], m_ref[...], v_ref[...], g)

    lay = pl.BlockSpec((None, tr, cols), lambda l, i, c_ref: (l, i, 0))
    one = pl.BlockSpec((tr, cols), lambda l, i, c_ref: (i, 0))
    shp = jax.ShapeDtypeStruct(w.shape, F32)
    return pl.pallas_call(
        body, name=name,
        grid_spec=pltpu.PrefetchScalarGridSpec(num_scalar_prefetch=1, grid=(2, rows // tr),
                                               in_specs=[lay, lay, lay, one, one], out_specs=[lay] * 4),
        out_shape=[shp] * 4, compiler_params=_cp(("parallel", "parallel")),
    )(core, w, m, v, g_mine, g_theirs)


def _sum_slots(buf, name, tr=256):
    n, rows, cols = buf.shape
    tr = _tile(rows, tr, 8)

    def body(b_ref, o_ref):
        acc = b_ref[0].astype(F32)
        for k in range(1, n):
            acc = acc + b_ref[k].astype(F32)
        o_ref[...] = acc

    return pl.pallas_call(
        body, name=name, grid=(rows // tr,),
        in_specs=[pl.BlockSpec((n, tr, cols), lambda i: (0, i, 0))],
        out_specs=pl.BlockSpec((tr, cols), lambda i: (i, 0)),
        out_shape=jax.ShapeDtypeStruct((rows, cols), F32), compiler_params=_cp(("parallel",)),
    )(buf)


_ANY = pl.BlockSpec(memory_space=pl.ANY)


def _exchange_chips(src, per_dest, name):
    rows, cols = src.shape[-2:]

    def body(in_ref, out_ref, send_sems, recv_sems, local_sem):
        x, y, c = lax.axis_index("x"), lax.axis_index("y"), lax.axis_index("c")
        me = 2 * x + y
        chips = [(1 - x, y), (x, 1 - y), (1 - x, 1 - y)]

        def block(j):
            return in_ref.at[j] if per_dest else in_ref

        mine = pltpu.make_async_copy(block(me), out_ref.at[me], local_sem)
        mine.start()
        sends = []
        for k, (px, py) in enumerate(chips):
            cp = pltpu.make_async_remote_copy(
                src_ref=block(2 * px + py), dst_ref=out_ref.at[me], send_sem=send_sems.at[k],
                recv_sem=recv_sems.at[k], device_id=(px, py, c), device_id_type=MESH)
            cp.start()
            sends.append(cp)
        for k, (px, py) in enumerate(chips):
            pltpu.make_async_remote_copy(
                src_ref=block(me), dst_ref=out_ref.at[2 * px + py], send_sem=send_sems.at[k],
                recv_sem=recv_sems.at[k], device_id=(px, py, c), device_id_type=MESH).wait_recv()
        for cp in sends:
            cp.wait_send()
        mine.wait()

    return pl.pallas_call(
        body, name=name, in_specs=[_ANY], out_specs=_ANY,
        out_shape=jax.ShapeDtypeStruct((N_CHIPS, rows, cols), src.dtype),
        scratch_shapes=[pltpu.SemaphoreType.DMA((3,)), pltpu.SemaphoreType.DMA((3,)), pltpu.SemaphoreType.DMA(())],
    )(src)


def _shard_window(ref, kind, j, lead=()):
    if kind == "slots":
        return ref.at[(j,) + lead]
    r, c = ref.shape[-2] // (N_CHIPS if kind == "rows" else 1), ref.shape[-1] // (N_CHIPS if kind == "cols" else 1)
    full = tuple(slice(None) for _ in range(len(ref.shape) - 2 - len(lead)))
    if kind == "rows":
        return ref.at[lead + full + (pl.ds(pl.multiple_of(j * r, 16), r), slice(None))]
    return ref.at[lead + full + (slice(None), pl.ds(pl.multiple_of(j * c, LANES), c))]


def _gather_kind(shard, axis):
    if axis == 1:
        return "rows"
    return "cols" if shard.shape[2] % LANES == 0 else "slots"


def _gather_weights(shards, axes, name):
    kinds = ["rows" if ax == 1 else "slots" for ax in axes]
    nw = len(shards)

    def out_shape(s, kind):
        d, r, c = s.shape
        shp = {"rows": (d, N_CHIPS * r, c), "cols": (d, r, N_CHIPS * c), "slots": (N_CHIPS, d, r, c)}[kind]
        return jax.ShapeDtypeStruct(shp, s.dtype)

    assert all(s.shape[0] == 2 for s in shards)

    def body(*refs):
        ins, outs = refs[:nw], refs[nw:2 * nw]
        ici_send, ici_recv, d2d_send, d2d_recv, local_sems = refs[2 * nw:]
        x, y, c = lax.axis_index("x"), lax.axis_index("y"), lax.axis_index("c")
        me = 2 * x + y
        chips = [(1 - x, y), (x, 1 - y), (1 - x, 1 - y)]
        sends = []

        def send(i):
            for k, (px, py) in enumerate(chips):
                cp = pltpu.make_async_remote_copy(
                    src_ref=ins[i].at[c], dst_ref=_shard_window(outs[i], kinds[i], me, (c,)),
                    send_sem=ici_send.at[3 * i + k], recv_sem=ici_recv.at[3 * i + k],
                    device_id=(px, py, c), device_id_type=MESH)
                cp.start()
                sends.append((cp, True))

        send(0)
        for i in range(nw):
            if i + 1 < nw:
                send(i + 1)
            for k, (px, py) in enumerate(chips):
                win = _shard_window(outs[i], kinds[i], 2 * px + py, (c,))
                pltpu.make_async_remote_copy(
                    src_ref=ins[i].at[c], dst_ref=win, send_sem=ici_send.at[3 * i + k], recv_sem=ici_recv.at[3 * i + k],
                    device_id=(px, py, c), device_id_type=MESH).wait_recv()
                fw = pltpu.make_async_remote_copy(
                    src_ref=win, dst_ref=win, send_sem=d2d_send.at[3 * i + k], recv_sem=d2d_recv.at[3 * i + k],
                    device_id=(x, y, 1 - c), device_id_type=MESH)
                fw.start()
                sends.append((fw, True))
        for i in range(nw):
            own = pltpu.make_async_copy(ins[i], _shard_window(outs[i], kinds[i], me), local_sems.at[i])
            own.start()
            sends.append((own, False))
        for i in range(nw):
            for k, (px, py) in enumerate(chips):
                win = _shard_window(outs[i], kinds[i], 2 * px + py, (1 - c,))
                pltpu.make_async_remote_copy(
                    src_ref=win, dst_ref=win, send_sem=d2d_send.at[3 * i + k], recv_sem=d2d_recv.at[3 * i + k],
                    device_id=(x, y, 1 - c), device_id_type=MESH).wait_recv()
        for cp, remote in sends:
            cp.wait_send() if remote else cp.wait()

    outs = pl.pallas_call(
        body, name=name, in_specs=[_ANY] * nw, out_specs=[_ANY] * nw,
        out_shape=[out_shape(s, k) for s, k in zip(shards, kinds)],
        scratch_shapes=[pltpu.SemaphoreType.DMA((3 * nw,)), pltpu.SemaphoreType.DMA((3 * nw,)),
                        pltpu.SemaphoreType.DMA((3 * nw,)), pltpu.SemaphoreType.DMA((3 * nw,)),
                        pltpu.SemaphoreType.DMA((nw,))],
    )(*shards)
    return [jnp.concatenate([o[j] for j in range(N_CHIPS)], axis=2) if k == "slots" else o for o, k in zip(outs, kinds)]


def _swap_other_layer(gst, name):
    nv = len(gst)

    def body(*refs):
        ins, outs, send_sems, recv_sems = refs[:nv], refs[nv:2 * nv], refs[2 * nv], refs[2 * nv + 1]
        x, y, c = lax.axis_index("x"), lax.axis_index("y"), lax.axis_index("c")
        cps = [pltpu.make_async_remote_copy(src_ref=ins[i].at[1 - c], dst_ref=outs[i], send_sem=send_sems.at[i],
                                            recv_sem=recv_sems.at[i], device_id=(x, y, 1 - c), device_id_type=MESH)
               for i in range(nv)]
        for cp in cps:
            cp.start()
        for cp in cps:
            cp.wait()

    return pl.pallas_call(
        body, name=name, in_specs=[_ANY] * nv, out_specs=[_ANY] * nv,
        out_shape=[jax.ShapeDtypeStruct(v.shape[1:], v.dtype) for v in gst],
        scratch_shapes=[pltpu.SemaphoreType.DMA((nv,)), pltpu.SemaphoreType.DMA((nv,))],
    )(*gst)


def _pair_sum(g, other, core, name, tr=256):
    _, rows, cols = g.shape
    tr = _tile(rows, tr, 16)

    def body(c_ref, g_ref, o_ref, s_ref):
        s_ref[...] = (g_ref[...].astype(F32) + o_ref[...].astype(F32)).astype(s_ref.dtype)

    blk = pl.BlockSpec((tr, cols), lambda i, c_ref: (i, 0))
    return pl.pallas_call(
        body, name=name,
        grid_spec=pltpu.PrefetchScalarGridSpec(
            num_scalar_prefetch=1, grid=(rows // tr,),
            in_specs=[pl.BlockSpec((None, tr, cols), lambda i, c_ref: (c_ref[0], i, 0)), blk], out_specs=blk),
        out_shape=jax.ShapeDtypeStruct((rows, cols), g.dtype), compiler_params=_cp(("parallel",)),
    )(core, g, other)


def _scatter_layer(parts, kinds, name):
    nw = len(parts)

    def shard_shape(g, kind):
        return g.shape[-2] // (N_CHIPS if kind == "rows" else 1), g.shape[-1] // (N_CHIPS if kind == "cols" else 1)

    def body(*refs):
        ins, outs = refs[:nw], refs[nw:2 * nw]
        send_sems, recv_sems, local_sems = refs[2 * nw:]
        x, y, c = lax.axis_index("x"), lax.axis_index("y"), lax.axis_index("c")
        me = 2 * x + y
        chips = [(1 - x, y), (x, 1 - y), (1 - x, 1 - y)]
        sends = []
        for i in range(nw):
            own = pltpu.make_async_copy(_shard_window(ins[i], kinds[i], me), outs[i].at[me], local_sems.at[i])
            own.start()
            sends.append((own, False))
            for k, (px, py) in enumerate(chips):
                cp = pltpu.make_async_remote_copy(
                    src_ref=_shard_window(ins[i], kinds[i], 2 * px + py), dst_ref=outs[i].at[me],
                    send_sem=send_sems.at[3 * i + k], recv_sem=recv_sems.at[3 * i + k],
                    device_id=(px, py, c), device_id_type=MESH)
                cp.start()
                sends.append((cp, True))
        for i in range(nw):
            for k, (px, py) in enumerate(chips):
                pltpu.make_async_remote_copy(
                    src_ref=_shard_window(ins[i], kinds[i], me), dst_ref=outs[i].at[2 * px + py],
                    send_sem=send_sems.at[3 * i + k], recv_sem=recv_sems.at[3 * i + k],
                    device_id=(px, py, c), device_id_type=MESH).wait_recv()
        for cp, remote in sends:
            cp.wait_send() if remote else cp.wait()

    return pl.pallas_call(
        body, name=name, in_specs=[_ANY] * nw, out_specs=[_ANY] * nw,
        out_shape=[jax.ShapeDtypeStruct((N_CHIPS,) + shard_shape(g, k), g.dtype) for g, k in zip(parts, kinds)],
        scratch_shapes=[pltpu.SemaphoreType.DMA((3 * nw,)), pltpu.SemaphoreType.DMA((3 * nw,)),
                        pltpu.SemaphoreType.DMA((nw,))],
    )(*parts)


def _sibling_swap(vs, name):
    nv = len(vs)

    def body(*refs):
        ins, outs, send_sems, recv_sems = refs[:nv], refs[nv:2 * nv], refs[2 * nv], refs[2 * nv + 1]
        x, y, c = lax.axis_index("x"), lax.axis_index("y"), lax.axis_index("c")
        cps = [pltpu.make_async_remote_copy(src_ref=ins[i], dst_ref=outs[i], send_sem=send_sems.at[i],
                                            recv_sem=recv_sems.at[i], device_id=(x, y, 1 - c), device_id_type=MESH)
               for i in range(nv)]
        for cp in cps:
            cp.start()
        for cp in cps:
            cp.wait()

    return pl.pallas_call(
        body, name=name, in_specs=[_ANY] * nv, out_specs=[_ANY] * nv,
        out_shape=[jax.ShapeDtypeStruct(v.shape, v.dtype) for v in vs],
        scratch_shapes=[pltpu.SemaphoreType.DMA((nv,)), pltpu.SemaphoreType.DMA((nv,))],
    )(*vs)


def _allgather_all(v, name):
    rows, cols = v.shape

    def body(in_ref, out_ref, send_sems, recv_sems, local_sem):
        x, y, c = lax.axis_index("x"), lax.axis_index("y"), lax.axis_index("c")
        me = 4 * x + 2 * y + c
        peers = []
        for k in range(1, N_DEV):
            peers.append(((1 - x) if k & 4 else x, (1 - y) if k & 2 else y, (1 - c) if k & 1 else c))
        mine = pltpu.make_async_copy(in_ref, out_ref.at[me], local_sem)
        mine.start()
        sends = []
        for k, peer in enumerate(peers):
            cp = pltpu.make_async_remote_copy(src_ref=in_ref, dst_ref=out_ref.at[me], send_sem=send_sems.at[k],
                                              recv_sem=recv_sems.at[k], device_id=peer, device_id_type=MESH)
            cp.start()
            sends.append(cp)
        for k, (px, py, pc) in enumerate(peers):
            pltpu.make_async_remote_copy(src_ref=in_ref, dst_ref=out_ref.at[4 * px + 2 * py + pc],
                                         send_sem=send_sems.at[k], recv_sem=recv_sems.at[k],
                                         device_id=(px, py, pc), device_id_type=MESH).wait_recv()
        for cp in sends:
            cp.wait_send()
        mine.wait()

    return pl.pallas_call(
        body, name=name, in_specs=[_ANY], out_specs=_ANY,
        out_shape=jax.ShapeDtypeStruct((N_DEV, rows, cols), v.dtype),
        scratch_shapes=[pltpu.SemaphoreType.DMA((N_DEV - 1,)), pltpu.SemaphoreType.DMA((N_DEV - 1,)),
                        pltpu.SemaphoreType.DMA(())],
    )(v)


def _pack(arrs, dtype, width, row_mult):
    flat = jnp.concatenate([a.reshape(-1).astype(dtype) for a in arrs])
    unit = width * row_mult
    total = -(-flat.shape[0] // unit) * unit
    return jnp.pad(flat, (0, total - flat.shape[0])).reshape(-1, width)


def _unpack(buf, shapes):
    flat = buf.reshape(-1)
    out, off = [], 0
    for shp in shapes:
        n = 1
        for d in shp:
            n *= d
        out.append(flat[off:off + n].reshape(shp))
        off += n
    return out


class _Dims:
    pass


def _dims(x, ssd_dt_bias, ssd_norm_g, ssd_conv_b, ffn_conv_b):
    dm = _Dims()
    dm.Bl, dm.S, dm.D = x.shape
    dm.H, dm.DI, dm.CD = ssd_dt_bias.shape[-1], ssd_norm_g.shape[-1], ssd_conv_b.shape[-1]
    dm.N = (dm.CD - dm.DI) // (2 * N_GROUPS)
    dm.R = dm.H // N_GROUPS
    dm.DFF = ffn_conv_b.shape[-1] // 2
    dm.OX = dm.DI
    dm.OB = dm.DI + dm.CD
    dm.OC = dm.OB + dm.D
    dm.OH = dm.OC + dm.D
    dm.OG = dm.OH + dm.D
    dm.ODT = dm.OG + 2 * dm.D
    dm.NP = dm.ODT + LANES
    assert dm.DI // dm.H == HEAD_DIM and dm.N == LANES and dm.R % 2 == 0 and dm.S % CHUNK == 0 and dm.H <= LANES
    return dm


def _permute_w_in(w, dm):
    o = dm.DI + dm.CD
    pad = jnp.zeros((w.shape[0], LANES - dm.H), w.dtype)
    return jnp.concatenate([w[:, :o], w[:, o + dm.H:], w[:, o:o + dm.H], pad], axis=1)


def _unpermute_w_in(dw, dm):
    o = dm.DI + dm.CD
    return jnp.concatenate([dw[..., :o], dw[..., dm.ODT:dm.ODT + dm.H], dw[..., o:dm.ODT]], axis=-1)


def _lane_pad(v):
    return jnp.pad(v.reshape(1, -1).astype(F32), ((0, 0), (0, LANES - v.shape[-1])))


def _pad8(w):
    return jnp.pad(w.astype(F32), ((0, 8 - w.shape[0]), (0, 0)))


def _head_select(dm):
    j = jnp.arange(LANES)[None, :, None]
    r = jnp.arange(LANES)[None, None, :]
    g = jnp.arange(N_GROUPS)[:, None, None]
    sel = ((j == dm.R * g + r) & (r < dm.R)).astype(F32)
    selcat = jnp.transpose(sel, (1, 0, 2)).reshape(LANES, N_GROUPS * LANES)
    selbig = jnp.transpose(sel[:, :, :8], (0, 2, 1)).reshape(N_GROUPS * 8, LANES)
    return selcat, selbig


def _mix_fwd(dm, h, w, sp, sel, tag):
    Bl, S, D = dm.Bl, dm.S, dm.D
    T = Bl * S
    proj = _matmul(h.reshape(T, D), w["w_in_p"], "NN", F32, tag + "_in_proj", tn=1152).reshape(Bl, S, dm.NP)
    xbc_a = _ssd_conv_fwd(proj, sp["ssd_conv_w8"], sp["ssd_conv_b"], dm.OX, dm.CD, tag + "_ssd_conv")
    dtg, acs, acs_t = _ssd_prep(proj, sp["dt_bias"], sp["a_log"], sel[0], sel[1], dm, tag + "_ssd_prep")
    y, yn, hprev = _scan_fwd(xbc_a, proj, dtg, acs, acs_t, sp["dsk_exp"], sp["ssd_norm_g"], dm, tag + "_ssd_scan")
    y_ssd = _matmul(yn.reshape(T, dm.DI), w["w_ssd_out"], "NN", F32, tag + "_ssd_out", tk=2048).reshape(Bl, S, D)
    s = _sc_conv_fwd(proj, sp["sc_conv_w8"], (dm.OB, dm.OC, dm.OH), D, tag + "_sc_conv")
    y_sc = _matmul(s.reshape(T, D), w["w_sc_out"], "NN", F32, tag + "_sc_out", tk=1024).reshape(Bl, S, D)
    mixin = _merge_fwd(proj, y_ssd, y_sc, dm.OG, D, tag + "_merge")
    mix = _matmul(mixin.reshape(T, D), w["w_o"], "NN", F32, tag + "_o", tk=1024).reshape(Bl, S, D)
    return mix, (h, proj, xbc_a, y, yn, hprev, y_ssd, y_sc, s, mixin, dtg, acs, acs_t)


def _mix_bwd(dm, dmix, saved, w, sp, sel, tag, gbig, slab):
    Bl, S, D = dm.Bl, dm.S, dm.D
    T = Bl * S
    h, proj, xbc_a, y, yn, hprev, y_ssd, y_sc, s, mixin, dtg, acs, acs_t = saved
    dmix2 = dmix.reshape(T, D)
    g = {}
    gbig["w_o"] = _matmul(mixin.reshape(T, D), dmix2, "TN", WIRE_DTYPE, tag + "_dw_o", slab=(gbig.get("w_o"),) + slab)
    dmixin = _matmul(dmix2, w["w_o"], "NT", F32, tag + "_d_o", tk=1024).reshape(Bl, S, D)
    dy_ssd, dy_sc, dg1, dg2 = _merge_bwd(dmixin, proj, y_ssd, y_sc, dm.OG, D, tag + "_merge_bwd")
    gbig["w_sc_out"] = _matmul(s.reshape(T, D), dy_sc.reshape(T, D), "TN", WIRE_DTYPE, tag + "_dw_sc_out",
                               slab=(gbig.get("w_sc_out"),) + slab)
    ds = _matmul(dy_sc.reshape(T, D), w["w_sc_out"], "NT", F32, tag + "_d_sc_out", tk=1024).reshape(Bl, S, D)
    dscb, dscc, dsch, dscw = _sc_conv_bwd(ds, proj, sp["sc_conv_w8"], (dm.OB, dm.OC, dm.OH), D, tag + "_sc_conv_bwd")
    g["sc_conv_w"] = dscw[:3]
    gbig["w_ssd_out"] = _matmul(yn.reshape(T, dm.DI), dy_ssd.reshape(T, D), "TN", WIRE_DTYPE, tag + "_dw_ssd_out",
                                slab=(gbig.get("w_ssd_out"),) + slab)
    dyn = _matmul(dy_ssd.reshape(T, D), w["w_ssd_out"], "NT", F32, tag + "_d_ssd_out", tk=1024).reshape(Bl, S, dm.DI)
    dz, dxs, dbm, dcm, d_a, ddt, ddsk, dng = _scan_bwd(dyn, y, xbc_a, proj, hprev, dtg, acs, acs_t, sp["dsk_exp"],
                                                       sp["ssd_norm_g"], dm, tag + "_ssd_scan_bwd")
    ddtr, dpar = _ssd_post(d_a, ddt, dtg, proj, sp["dt_bias"], sp["a_log"], sel[0], dm, tag + "_ssd_post")
    g["ssd_dt_bias"], g["ssd_a_log"] = dpar[0, :dm.H], dpar[1, :dm.H]
    g["ssd_d"] = jnp.sum(ddsk.reshape(dm.H, HEAD_DIM), axis=-1)
    g["ssd_norm_g"] = dng[0]
    dxa = jnp.concatenate([dxs, dbm, dcm], axis=-1)
    dxbc, dcw, dcb = _ssd_conv_bwd(dxa, proj, sp["ssd_conv_w8"], sp["ssd_conv_b"], dm.OX, dm.CD, tag + "_ssd_conv_bwd")
    g["ssd_conv_w"], g["ssd_conv_b"] = dcw[:4], dcb[0]
    dproj = jnp.concatenate([dz, dxbc, dscb, dscc, dsch, dg1, dg2, ddtr], axis=-1).reshape(T, dm.NP)
    gbig["w_in_p"] = _matmul(h.reshape(T, D), dproj, "TN", WIRE_DTYPE, tag + "_dw_in", tn=1152,
                             slab=(gbig.get("w_in_p"),) + slab)
    dh = _matmul(dproj, w["w_in_p"], "NT", F32, tag + "_d_in", tk=1152).reshape(Bl, S, D)
    return dh, g


def _ffn_fwd(dm, h, w, sp, tag):
    Bl, S, D = dm.Bl, dm.S, dm.D
    T = Bl * S
    up = _matmul(h.reshape(T, D), w["w_up"], "NN", F32, tag + "_up", tn=1408).reshape(Bl, S, 2 * dm.DFF)
    a = _ffn_conv_fwd(up, sp["ffn_conv_w8"], sp["ffn_conv_b"], dm.DFF, tag + "_ffn_conv")
    f = _matmul(a.reshape(T, dm.DFF), w["w_down"], "NN", F32, tag + "_down", tk=2816).reshape(Bl, S, D)
    return f, (h, up, a)


def _ffn_bwd(dm, df, saved, w, sp, tag, gbig, slab):
    Bl, S, D = dm.Bl, dm.S, dm.D
    T = Bl * S
    h, up, a = saved
    df2 = df.reshape(T, D)
    g = {}
    gbig["w_down"] = _matmul(a.reshape(T, dm.DFF), df2, "TN", WIRE_DTYPE, tag + "_dw_down", tm=1408,
                             slab=(gbig.get("w_down"),) + slab)
    da = _matmul(df2, w["w_down"], "NT", F32, tag + "_d_down", tn=1408).reshape(Bl, S, dm.DFF)
    dg, dv, dcw, dcb = _ffn_conv_bwd(da, up, sp["ffn_conv_w8"], sp["ffn_conv_b"], dm.DFF, tag + "_ffn_conv_bwd")
    g["ffn_conv_w"], g["ffn_conv_b"] = dcw[:3], dcb[0]
    dup = jnp.concatenate([dg, dv], axis=-1).reshape(T, 2 * dm.DFF)
    gbig["w_up"] = _matmul(h.reshape(T, D), dup, "TN", WIRE_DTYPE, tag + "_dw_up", tn=1408,
                           slab=(gbig.get("w_up"),) + slab)
    dh = _matmul(dup, w["w_up"], "NT", F32, tag + "_d_up", tk=1408).reshape(Bl, S, D)
    return dh, g


def _local_step(dm, x, c, target, wfull, small):
    Bl, S, D = dm.Bl, dm.S, dm.D
    depth = len(wfull)
    sel = _head_select(dm)
    c16 = jnp.pad(c.astype(F32), ((0, 16 - Bl), (0, 0)))
    sps, mods, acts = [], [], []
    for l in range(depth):
        sm = small[l]
        sps.append(dict(
            ssd_conv_w8=_pad8(sm["ssd_conv_w"]), ssd_conv_b=sm["ssd_conv_b"], dt_bias=_lane_pad(sm["ssd_dt_bias"]),
            a_log=_lane_pad(sm["ssd_a_log"]), dsk_exp=jnp.repeat(sm["ssd_d"].astype(F32), HEAD_DIM).reshape(1, dm.DI),
            ssd_norm_g=sm["ssd_norm_g"],
            sc_conv_w8=_pad8(sm["sc_conv_w"]), ffn_conv_w8=_pad8(sm["ffn_conv_w"]), ffn_conv_b=sm["ffn_conv_b"]))
        act, mod = _ada_fwd(c16, wfull[l]["ada_w"], sm["ada_b"], f"l{l}_ada")
        acts.append(act)
        mods.append(jnp.pad(mod[:Bl].reshape(Bl, 6, D), ((0, 0), (0, 2), (0, 0))))

    def sub(i):
        l, ffn = i // 2, i % 2
        sm = small[l]
        return dict(l=l, ffn=ffn, pre_g=sm["ffn_pre_g" if ffn else "mix_pre_g"],
                    post_g=sm["ffn_post_g" if ffn else "mix_post_g"], mod=mods[l], row=3 * ffn,
                    tag=f"l{l}_{'ffn' if ffn else 'mix'}")

    nsub = 2 * depth
    subs = [sub(i) for i in range(nsub)]
    xs, fs, saves = [x], [], []
    h = _norm_mod_fwd(x, subs[0]["pre_g"], subs[0]["mod"], subs[0]["row"], "l0_mix_pre_norm")
    for i, sb in enumerate(subs):
        l = sb["l"]
        if sb["ffn"]:
            f, sv = _ffn_fwd(dm, h, wfull[l], sps[l], sb["tag"])
        else:
            f, sv = _mix_fwd(dm, h, wfull[l], sps[l], sel, sb["tag"])
        nxt = None
        if i + 1 < nsub:
            nb = subs[i + 1]
            nxt = (nb["pre_g"], nb["mod"], nb["row"])
        xn, h = _post_norm_fwd(xs[-1], f, sb["post_g"], sb["mod"], sb["row"] + 2, nxt, sb["tag"] + "_post_norm")
        xs.append(xn)
        fs.append(f)
        saves.append(sv)

    dy, loss = _loss_fwd_bwd(xs[-1], target, "loss")

    grads = [dict() for _ in range(depth)]
    gbig = {}
    dmod = [[None] * 6 for _ in range(depth)]
    dx, dh = dy, None
    for i in reversed(range(nsub)):
        sb = subs[i]
        l = sb["l"]
        nxt = None
        if i + 1 < nsub:
            nb = subs[i + 1]
            nxt = (dh, xs[i + 1], nb["pre_g"], nb["mod"], nb["row"])
        dx, df, pb, shg = _norm_bwd(dx, nxt, (fs[i], sb["post_g"], sb["mod"], sb["row"] + 2), sb["tag"] + "_post_norm_bwd")
        if nxt is not None:
            nb = subs[i + 1]
            dmod[nb["l"]][nb["row"]], dmod[nb["l"]][nb["row"] + 1] = pb[:, 0], pb[:, 1]
            grads[nb["l"]]["ffn_pre_g" if nb["ffn"] else "mix_pre_g"] = shg[0]
        dmod[l][sb["row"] + 2] = pb[:, 2]
        grads[l]["ffn_post_g" if sb["ffn"] else "mix_post_g"] = shg[1]
        if sb["ffn"]:
            dh, g = _ffn_bwd(dm, df, saves[i], wfull[l], sps[l], sb["tag"], gbig, (l, depth))
        else:
            dh, g = _mix_bwd(dm, df, saves[i], wfull[l], sps[l], sel, sb["tag"], gbig, (l, depth))
        grads[l].update(g)
    sb = subs[0]
    grad_x, _, pb, shg = _norm_bwd(dx, (dh, xs[0], sb["pre_g"], sb["mod"], sb["row"]), None, "l0_mix_pre_norm_bwd")
    dmod[0][0], dmod[0][1] = pb[:, 0], pb[:, 1]
    grads[0]["mix_pre_g"] = shg[0]

    for l in range(depth):
        dm6 = jnp.concatenate(dmod[l], axis=-1)
        grads[l]["ada_b"] = jnp.sum(dm6, axis=0)
        dm16 = jnp.pad(dm6, ((0, 16 - Bl), (0, 0))).astype(MXU_DTYPE)
        gbig["ada_w"] = _matmul(acts[l], dm16, "TN", WIRE_DTYPE, f"l{l}_dw_ada", slab=(gbig.get("ada_w"), l, depth))
    return loss, grad_x, grads, gbig


_WEIGHTS = ("ada_w", "ada_b", "mix_pre_g", "mix_post_g", "w_in", "ssd_conv_w", "ssd_conv_b", "ssd_dt_bias",
            "ssd_a_log", "ssd_d", "ssd_norm_g", "w_ssd_out", "sc_conv_w", "w_sc_out", "w_o", "ffn_pre_g",
            "ffn_post_g", "w_up", "ffn_conv_w", "ffn_conv_b", "w_down")
_INPUTS = ("x", "c") + _WEIGHTS + ("loss_target",) + tuple("m_" + n for n in _WEIGHTS) + tuple("v_" + n for n in _WEIGHTS)
_BIG = {"ada_w": 2, "w_in": 2, "w_ssd_out": 1, "w_sc_out": 1, "w_o": 1, "w_up": 2, "w_down": 1}
_CONV = ("ssd_conv_w", "sc_conv_w", "ffn_conv_w")
_SMALL = tuple(n for n in _WEIGHTS if n not in _BIG)


def _step(a):
    x, c, target = a["x"], a["c"], a["loss_target"]
    depth = a["ada_w"].shape[0]
    dm = _dims(x, a["ssd_dt_bias"], a["ssd_norm_g"], a["ssd_conv_b"], a["ffn_conv_b"])
    chip = 2 * lax.axis_index("x") + lax.axis_index("y")

    shards = [a[n].astype(WIRE_DTYPE) for n in _BIG]
    axes = list(_BIG.values())
    kinds = [_gather_kind(s, ax) for s, ax in zip(shards, axes)]
    full = {n: w.astype(MXU_DTYPE) for n, w in zip(_BIG, _gather_weights(shards, axes, "gather_weights"))}
    conv_shapes = [a[n].shape for n in _CONV]
    gotc = _exchange_chips(_pack([a[n] for n in _CONV], F32, LANES, 8), False, "gather_conv_weights")
    piecesc = [_unpack(gotc[j], conv_shapes) for j in range(N_CHIPS)]
    fullc = {n: jnp.concatenate([piecesc[j][i] for j in range(N_CHIPS)], axis=2) for i, n in enumerate(_CONV)}

    wfull, small = [], []
    for l in range(depth):
        wf = {n: full[n][l] for n in _BIG if n != "w_in"}
        wf["w_in_p"] = _permute_w_in(full["w_in"][l], dm)
        wfull.append(wf)
        small.append({n: (fullc[n][l] if n in _CONV else a[n][l]) for n in _SMALL})

    loss_part, grad_x, grads, gbig = _local_step(dm, x, c, target, wfull, small)

    core = lax.axis_index("c").astype(jnp.int32).reshape(1)
    gst = []
    for n, kind in zip(_BIG, kinds):
        g = _unpermute_w_in(gbig["w_in_p"], dm) if n == "w_in" else gbig[n]
        if kind == "slots":
            g = jnp.moveaxis(g.reshape(depth, g.shape[1], N_CHIPS, g.shape[2] // N_CHIPS), 2, 1)
            g = g.reshape(depth, -1, g.shape[-1])
        gst.append(g)
    other = _swap_other_layer(gst, "swap_layer_grads")
    parts = [_pair_sum(g, o, core, "pair_sum_" + n) for g, o, n in zip(gst, other, _BIG)]
    parts = [p.reshape(N_CHIPS, -1, p.shape[-1]) if k == "slots" else p for p, k in zip(parts, kinds)]
    got = _scatter_layer(parts, kinds, "scatter_grads")
    mine = [_sum_slots(g, "sum_chip_grads_" + n) for g, n in zip(got, _BIG)]
    theirs = _sibling_swap(mine, "swap_core_grads")

    out = {}
    for i, n in enumerate(_BIG):
        out[n] = _adamw_layers(a[n], a["m_" + n], a["v_" + n], mine[i], theirs[i], core, "adamw_" + n)

    gsmall = [jnp.stack([grads[l][n] for l in range(depth)]) for n in _SMALL]
    small_shapes = [g.shape for g in gsmall]
    summed = _sum_slots(_allgather_all(_pack(gsmall, F32, LANES, 8), "gather_small_grads"), "sum_small_grads")
    gs = dict(zip(_SMALL, _unpack(summed, small_shapes)))
    for n in _CONV:
        wcols = a[n].shape[2]
        gs[n] = lax.dynamic_slice_in_dim(gs[n], chip * wcols, wcols, axis=2)
    local_shapes = [a[n].shape for n in _SMALL]
    res = _adamw(_pack([a[n] for n in _SMALL], F32, LANES, 8), _pack([a["m_" + n] for n in _SMALL], F32, LANES, 8),
                 _pack([a["v_" + n] for n in _SMALL], F32, LANES, 8), _pack([gs[n] for n in _SMALL], F32, LANES, 8),
                 "adamw_small")
    res = [_unpack(r, local_shapes) for r in res]
    for i, n in enumerate(_SMALL):
        out[n] = [r[i] for r in res]

    loss = lax.psum(loss_part, ("x", "y", "c"))
    return (loss, grad_x) + tuple(out[n][k] for k in range(4) for n in _WEIGHTS)


def kernel(x, c, ada_w, ada_b, mix_pre_g, mix_post_g, w_in, ssd_conv_w, ssd_conv_b, ssd_dt_bias, ssd_a_log, ssd_d, ssd_norm_g, w_ssd_out, sc_conv_w, w_sc_out, w_o, ffn_pre_g, ffn_post_g, w_up, ffn_conv_w, ffn_conv_b, w_down, loss_target, m_ada_w, m_ada_b, m_mix_pre_g, m_mix_post_g, m_w_in, m_ssd_conv_w, m_ssd_conv_b, m_ssd_dt_bias, m_ssd_a_log, m_ssd_d, m_ssd_norm_g, m_w_ssd_out, m_sc_conv_w, m_w_sc_out, m_w_o, m_ffn_pre_g, m_ffn_post_g, m_w_up, m_ffn_conv_w, m_ffn_conv_b, m_w_down, v_ada_w, v_ada_b, v_mix_pre_g, v_mix_post_g, v_w_in, v_ssd_conv_w, v_ssd_conv_b, v_ssd_dt_bias, v_ssd_a_log, v_ssd_d, v_ssd_norm_g, v_w_ssd_out, v_sc_conv_w, v_w_sc_out, v_w_o, v_ffn_pre_g, v_ffn_post_g, v_w_up, v_ffn_conv_w, v_ffn_conv_b, v_w_down):
    return _step(dict(zip(_INPUTS, (
        x, c, ada_w, ada_b, mix_pre_g, mix_post_g, w_in, ssd_conv_w, ssd_conv_b, ssd_dt_bias, ssd_a_log, ssd_d, ssd_norm_g, w_ssd_out, sc_conv_w, w_sc_out, w_o, ffn_pre_g, ffn_post_g, w_up, ffn_conv_w, ffn_conv_b, w_down, loss_target, m_ada_w, m_ada_b, m_mix_pre_g, m_mix_post_g, m_w_in, m_ssd_conv_w, m_ssd_conv_b, m_ssd_dt_bias, m_ssd_a_log, m_ssd_d, m_ssd_norm_g, m_w_ssd_out, m_sc_conv_w, m_w_sc_out, m_w_o, m_ffn_pre_g, m_ffn_post_g, m_w_up, m_ffn_conv_w, m_ffn_conv_b, m_w_down, v_ada_w, v_ada_b, v_mix_pre_g, v_mix_post_g, v_w_in, v_ssd_conv_w, v_ssd_conv_b, v_ssd_dt_bias, v_ssd_a_log, v_ssd_d, v_ssd_norm_g, v_w_ssd_out, v_sc_conv_w, v_w_sc_out, v_w_o, v_ffn_pre_g, v_ffn_post_g, v_w_up, v_ffn_conv_w, v_ffn_conv_b, v_w_down))))
```

```python
import math

import jax
import jax.numpy as jnp
from jax import lax
from jax.experimental import pallas as pl
from jax.experimental.pallas import tpu as pltpu

F32 = jnp.float32
MXU_DTYPE = jnp.bfloat16
WIRE_DTYPE = jnp.bfloat16
HI = lax.Precision.HIGHEST
EPS = 1e-6
N_GROUPS = 4
CHUNK = 128
HEAD_DIM = 64
LANES = 128
HALO = 8
N_CHIPS = 4
N_DEV = 8
VMEM_LIMIT = 56 * 1024 * 1024
ADAM_LR, ADAM_B1, ADAM_B2, ADAM_EPS, ADAM_WD, ADAM_STEP = 0.001, 0.9, 0.999, 1e-08, 0.01, 10
MESH = pl.DeviceIdType.MESH

NN = (((1,), (0,)), ((), ()))
NT = (((1,), (1,)), ((), ()))
TN = (((0,), (0,)), ((), ()))


def _dg(a, b, dn, precision=None):
    return lax.dot_general(a, b, dn, precision=precision, preferred_element_type=F32)


def _tile(dim, pref, mult=LANES):
    t = (min(pref, dim) // mult) * mult
    while t >= mult:
        if dim % t == 0:
            return t
        t -= mult
    return dim


def _cp(sem):
    return pltpu.CompilerParams(dimension_semantics=sem, vmem_limit_bytes=VMEM_LIMIT)


def _sigmoid(x):
    return 1.0 / (1.0 + jnp.exp(-x))


def _softplus(x):
    return jnp.maximum(x, 0.0) + jnp.log1p(jnp.exp(-jnp.abs(x)))


def _matmul(a, b, mode, out_dtype, name, tm=1024, tn=1024, tk=1024, slab=None):
    if mode == "NN":
        (M, K), N = a.shape, b.shape[1]
    elif mode == "NT":
        (M, K), N = a.shape, b.shape[0]
    else:
        (K, M), N = a.shape, b.shape[1]
    tm, tn, tk = _tile(M, tm), _tile(N, tn), _tile(K, tk)
    nk = K // tk
    dn = {"NN": NN, "NT": NT, "TN": TN}[mode]
    carry = slab is not None and slab[0] is not None

    def body_one(a_ref, b_ref, *rest):
        o_ref = rest[-1]
        o_ref[...] = _dg(a_ref[...], b_ref[...], dn).astype(o_ref.dtype)

    def body_acc(a_ref, b_ref, *rest):
        o_ref, acc_ref = rest[-2:]
        k = pl.program_id(2)

        @pl.when(k == 0)
        def _():
            acc_ref[...] = jnp.zeros_like(acc_ref)

        acc_ref[...] += _dg(a_ref[...], b_ref[...], dn)

        @pl.when(k == nk - 1)
        def _():
            o_ref[...] = acc_ref[...].astype(o_ref.dtype)

    a_spec = (pl.BlockSpec((tk, tm), lambda i, j, k: (k, i)) if mode == "TN"
              else pl.BlockSpec((tm, tk), lambda i, j, k: (i, k)))
    b_spec = (pl.BlockSpec((tn, tk), lambda i, j, k: (j, k)) if mode == "NT"
              else pl.BlockSpec((tk, tn), lambda i, j, k: (k, j)))
    if slab is None:
        out_spec = pl.BlockSpec((tm, tn), lambda i, j, k: (i, j))
        out_shape = jax.ShapeDtypeStruct((M, N), out_dtype)
    else:
        layer = slab[1]
        out_spec = pl.BlockSpec((None, tm, tn), lambda i, j, k: (layer, i, j))
        out_shape = jax.ShapeDtypeStruct((slab[2], M, N), out_dtype)
    return pl.pallas_call(
        body_one if nk == 1 else body_acc, name=name, grid=(M // tm, N // tn, nk),
        in_specs=[a_spec, b_spec] + ([_ANY] if carry else []),
        out_specs=out_spec, out_shape=out_shape,
        input_output_aliases={2: 0} if carry else {},
        scratch_shapes=[] if nk == 1 else [pltpu.VMEM((tm, tn), F32)],
        compiler_params=_cp(("parallel", "parallel", "arbitrary")),
    )(*((a, b, slab[0]) if carry else (a, b)))


def _ada_fwd(c16, ada_w, ada_b, name):
    rows, D = c16.shape
    N6 = ada_w.shape[1]
    tn = _tile(N6, 1536)

    def body(c_ref, w_ref, b_ref, act_ref, mod_ref):
        c = c_ref[...]
        act = (c * _sigmoid(c)).astype(act_ref.dtype)
        act_ref[...] = act
        mod_ref[...] = _dg(act, w_ref[...], NN) + b_ref[...]

    return pl.pallas_call(
        body, name=name, grid=(N6 // tn,),
        in_specs=[pl.BlockSpec((rows, D), lambda j: (0, 0)),
                  pl.BlockSpec((D, tn), lambda j: (0, j)),
                  pl.BlockSpec((1, tn), lambda j: (0, j))],
        out_specs=[pl.BlockSpec((rows, D), lambda j: (0, 0)),
                   pl.BlockSpec((rows, tn), lambda j: (0, j))],
        out_shape=[jax.ShapeDtypeStruct((rows, D), MXU_DTYPE),
                   jax.ShapeDtypeStruct((rows, N6), F32)],
        compiler_params=_cp(("arbitrary",)),
    )(c16, ada_w, ada_b.reshape(1, N6))


def _norm_mod_rows(x, g, sc, sh):
    r = lax.rsqrt(jnp.mean(x * x, axis=-1, keepdims=True) + EPS)
    return ((x * r) * g) * (1.0 + sc) + sh


def _norm_mod_fwd(x, g, mod, row_sh, name, ts=512):
    Bl, S, D = x.shape
    ts = _tile(S, ts, 8)

    def body(x_ref, g_ref, mod_ref, h_ref):
        sh = mod_ref[row_sh:row_sh + 1, :]
        sc = mod_ref[row_sh + 1:row_sh + 2, :]
        h_ref[...] = _norm_mod_rows(x_ref[...], g_ref[...], sc, sh).astype(h_ref.dtype)

    tok = pl.BlockSpec((None, ts, D), lambda b, s: (b, s, 0))
    return pl.pallas_call(
        body, name=name, grid=(Bl, S // ts),
        in_specs=[tok, pl.BlockSpec((1, D), lambda b, s: (0, 0)),
                  pl.BlockSpec((None, 8, D), lambda b, s: (b, 0, 0))],
        out_specs=tok, out_shape=jax.ShapeDtypeStruct((Bl, S, D), MXU_DTYPE),
        compiler_params=_cp(("parallel", "parallel")),
    )(x, g.reshape(1, D), mod)


def _post_norm_fwd(xp, f, post_g, mod, row_gt, nxt, name, ts=512):
    Bl, S, D = xp.shape
    ts = _tile(S, ts, 8)
    has_next = nxt is not None

    def body(*refs):
        if has_next:
            xp_ref, f_ref, pg_ref, mod_ref, ng_ref, nmod_ref, x_ref, h_ref = refs
        else:
            xp_ref, f_ref, pg_ref, mod_ref, x_ref = refs
        f = f_ref[...]
        r = lax.rsqrt(jnp.mean(f * f, axis=-1, keepdims=True) + EPS)
        x = xp_ref[...] + mod_ref[row_gt:row_gt + 1, :] * ((f * r) * pg_ref[...])
        x_ref[...] = x
        if has_next:
            rs = nxt[2]
            h_ref[...] = _norm_mod_rows(x, ng_ref[...], nmod_ref[rs + 1:rs + 2, :], nmod_ref[rs:rs + 1, :]).astype(h_ref.dtype)

    tok = pl.BlockSpec((None, ts, D), lambda b, s: (b, s, 0))
    vec = pl.BlockSpec((1, D), lambda b, s: (0, 0))
    modspec = pl.BlockSpec((None, 8, D), lambda b, s: (b, 0, 0))
    ins = [xp, f, post_g.reshape(1, D), mod]
    in_specs = [tok, tok, vec, modspec]
    out_specs = [tok]
    out_shape = [jax.ShapeDtypeStruct((Bl, S, D), F32)]
    if has_next:
        ins += [nxt[0].reshape(1, D), nxt[1]]
        in_specs += [vec, modspec]
        out_specs += [tok]
        out_shape += [jax.ShapeDtypeStruct((Bl, S, D), MXU_DTYPE)]
    out = pl.pallas_call(
        body, name=name, grid=(Bl, S // ts), in_specs=in_specs, out_specs=out_specs, out_shape=out_shape,
        compiler_params=_cp(("parallel", "parallel")),
    )(*ins)
    return (out[0], out[1]) if has_next else (out[0], None)


def _loss_fwd_bwd(y, target, name, ts=512):
    Bl, S, D = y.shape
    ts = _tile(S, ts, 8)

    def body(y_ref, t_ref, dy_ref, l_ref):
        @pl.when((pl.program_id(0) == 0) & (pl.program_id(1) == 0))
        def _():
            l_ref[...] = jnp.zeros_like(l_ref)

        e = y_ref[...] - t_ref[...]
        dy_ref[...] = e * (1.0 / D)
        l_ref[...] += 0.5 * jnp.sum(jnp.mean(e * e, axis=-1, keepdims=True), axis=0, keepdims=True)

    tok = pl.BlockSpec((None, ts, D), lambda b, s: (b, s, 0))
    dy, l = pl.pallas_call(
        body, name=name, grid=(Bl, S // ts), in_specs=[tok, tok],
        out_specs=[tok, pl.BlockSpec((8, LANES), lambda b, s: (0, 0))],
        out_shape=[jax.ShapeDtypeStruct((Bl, S, D), F32), jax.ShapeDtypeStruct((8, LANES), F32)],
        compiler_params=_cp(("arbitrary", "arbitrary")),
    )(y, target)
    return dy, l[0, 0]


def _norm_bwd(dx_res, nxt, prv, name, ts=256):
    Bl, S, D = dx_res.shape
    ts = _tile(S, ts, 8)
    has_next, has_prev = nxt is not None, prv is not None

    def body(*refs):
        refs = list(refs)
        dxr_ref = refs.pop(0)
        if has_next:
            dh_ref, x_ref, g_ref, nmod_ref = refs[:4]
            refs = refs[4:]
        if has_prev:
            f_ref, pg_ref, pmod_ref = refs[:3]
            refs = refs[3:]
        dx_ref = refs.pop(0)
        if has_prev:
            df_ref = refs.pop(0)
        pb_ref, sh_ref = refs
        b, s = pl.program_id(0), pl.program_id(1)

        @pl.when(s == 0)
        def _():
            pb_ref[...] = jnp.zeros_like(pb_ref)

        @pl.when((b == 0) & (s == 0))
        def _():
            sh_ref[...] = jnp.zeros_like(sh_ref)

        dx = dxr_ref[...]
        if has_next:
            rs = nxt[4]
            x, dh, g = x_ref[...], dh_ref[...], g_ref[...]
            sc1 = 1.0 + nmod_ref[rs + 1:rs + 2, :]
            r = lax.rsqrt(jnp.mean(x * x, axis=-1, keepdims=True) + EPS)
            xn = x * r
            pb_ref[0:1, :] += jnp.sum(dh, axis=0, keepdims=True)
            pb_ref[1:2, :] += jnp.sum(dh * (xn * g), axis=0, keepdims=True)
            sh_ref[0:1, :] += jnp.sum(dh * sc1 * xn, axis=0, keepdims=True)
            dxn = dh * sc1 * g
            dx = dx + r * (dxn - xn * jnp.mean(dxn * xn, axis=-1, keepdims=True))
        dx_ref[...] = dx
        if has_prev:
            rg = prv[3]
            f, pg = f_ref[...], pg_ref[...]
            gt = pmod_ref[rg:rg + 1, :]
            r = lax.rsqrt(jnp.mean(f * f, axis=-1, keepdims=True) + EPS)
            fn = f * r
            pb_ref[2:3, :] += jnp.sum(dx * (fn * pg), axis=0, keepdims=True)
            drn = dx * gt
            sh_ref[1:2, :] += jnp.sum(drn * fn, axis=0, keepdims=True)
            dfn = drn * pg
            df_ref[...] = (r * (dfn - fn * jnp.mean(dfn * fn, axis=-1, keepdims=True))).astype(df_ref.dtype)

    tok = pl.BlockSpec((None, ts, D), lambda b, s: (b, s, 0))
    vec = pl.BlockSpec((1, D), lambda b, s: (0, 0))
    modspec = pl.BlockSpec((None, 8, D), lambda b, s: (b, 0, 0))
    ins, in_specs = [dx_res], [tok]
    if has_next:
        ins += [nxt[0], nxt[1], nxt[2].reshape(1, D), nxt[3]]
        in_specs += [tok, tok, vec, modspec]
    if has_prev:
        ins += [prv[0], prv[1].reshape(1, D), prv[2]]
        in_specs += [tok, vec, modspec]
    out_specs, out_shape = [tok], [jax.ShapeDtypeStruct((Bl, S, D), F32)]
    if has_prev:
        out_specs += [tok]
        out_shape += [jax.ShapeDtypeStruct((Bl, S, D), MXU_DTYPE)]
    out_specs += [modspec, pl.BlockSpec((8, D), lambda b, s: (0, 0))]
    out_shape += [jax.ShapeDtypeStruct((Bl, 8, D), F32), jax.ShapeDtypeStruct((8, D), F32)]
    out = pl.pallas_call(
        body, name=name, grid=(Bl, S // ts), in_specs=in_specs, out_specs=out_specs, out_shape=out_shape,
        compiler_params=_cp(("arbitrary", "arbitrary")),
    )(*ins)
    if has_prev:
        return out[0], out[1], out[2], out[3]
    return out[0], None, out[1], out[2]


def _shift_down(x, j):
    return x if j == 0 else pltpu.roll(x, j, axis=0)


def _shift_up(x, j):
    return x if j == 0 else pltpu.roll(x, x.shape[0] - j, axis=0)


def _conv(xall, w_ref, K):
    y = w_ref[K - 1:K, :] * xall
    for k in range(K - 1):
        y = y + w_ref[k:k + 1, :] * _shift_down(xall, K - 1 - k)
    return y


def _conv_t(dall, xall, w, K, rows):
    xt = xall[HALO:HALO + rows]
    y = w[K - 1:K, :] * dall[HALO:HALO + rows]
    gw = [None] * K
    gw[K - 1] = jnp.sum(dall[HALO:HALO + rows] * xt, axis=0, keepdims=True)
    for k in range(K - 1):
        sh = _shift_up(dall, K - 1 - k)[HALO:HALO + rows]
        y = y + w[k:k + 1, :] * sh
        gw[k] = jnp.sum(sh * xt, axis=0, keepdims=True)
    return y, gw


ROW_CHUNK = 64


def _lane_chunks(W):
    lc = 2 * LANES if W % (2 * LANES) == 0 else LANES
    return [(j * lc, lc) for j in range(W // lc)]


def _fill_ext(ext_ref, prev, tile, nxt, ts):
    ext_ref[0:HALO, :] = prev
    ext_ref[HALO:HALO + ts, :] = tile
    ext_ref[HALO + ts:2 * HALO + ts, :] = nxt


def _halo_specs(ts, W, nS, colblk):
    per = ts // HALO
    tile = pl.BlockSpec((None, ts, W), lambda b, c, s: (b, s, colblk(c)))
    prev = pl.BlockSpec((None, HALO, W), lambda b, c, s: (b, jnp.maximum(s * per - 1, 0), colblk(c)))
    nxt = pl.BlockSpec((None, HALO, W), lambda b, c, s: (b, jnp.minimum((s + 1) * per, nS * per - 1), colblk(c)))
    return tile, prev, nxt


def _masked(ref, keep):
    v = ref[...]
    return jnp.where(keep, v, jnp.zeros_like(v))


def _ssd_conv_fwd(proj, w8, b, off, CD, name, ts=512, W=1024):
    Bl, S, _ = proj.shape
    K = 4
    ts, W = _tile(S, ts, 8), _tile(math.gcd(CD, off), W)
    assert off % W == 0
    nS, nW, ob = S // ts, CD // W, off // W

    rc = _tile(ts, ROW_CHUNK, 8)

    def body(x_ref, xp_ref, w_ref, b_ref, o_ref, ext_ref):
        ext_ref[0:HALO, :] = _masked(xp_ref, pl.program_id(2) > 0)
        ext_ref[HALO:HALO + ts, :] = x_ref[...]
        for l0, lc in _lane_chunks(W):
            w, bias = w_ref[:, l0:l0 + lc], b_ref[:, l0:l0 + lc]

            def chunk(i, carry):
                r0 = pl.multiple_of(i * rc, 8)
                xc = _conv(ext_ref[pl.ds(r0, rc + HALO), l0:l0 + lc], w, K)[HALO:] + bias
                o_ref[pl.ds(r0, rc), l0:l0 + lc] = xc * _sigmoid(xc)
                return carry

            lax.fori_loop(0, ts // rc, chunk, 0)

    tile, prev, _ = _halo_specs(ts, W, nS, lambda c: ob + c)
    return pl.pallas_call(
        body, name=name, grid=(Bl, nW, nS),
        in_specs=[tile, prev, pl.BlockSpec((8, W), lambda b_, c, s: (0, c)), pl.BlockSpec((1, W), lambda b_, c, s: (0, c))],
        out_specs=pl.BlockSpec((None, ts, W), lambda b_, c, s: (b_, s, c)),
        out_shape=jax.ShapeDtypeStruct((Bl, S, CD), F32),
        scratch_shapes=[pltpu.VMEM((ts + HALO, W), F32)],
        compiler_params=_cp(("parallel", "parallel", "parallel")),
    )(proj, proj, w8, b.reshape(1, CD))


def _ssd_conv_bwd(dxa, proj, w8, b, off, CD, name, ts=256, W=1024):
    Bl, S, _ = proj.shape
    K = 4
    ts, W = _tile(S, ts, 8), _tile(math.gcd(CD, off), W)
    nS, nW, ob = S // ts, CD // W, off // W

    rc = _tile(ts, ROW_CHUNK, 8)

    def body(d_ref, dn_ref, x_ref, xp_ref, xn_ref, w_ref, b_ref, dx_ref, dw_ref, db_ref, xext_ref, dext_ref):
        bb, s = pl.program_id(1), pl.program_id(2)

        @pl.when((bb == 0) & (s == 0))
        def _():
            dw_ref[...] = jnp.zeros_like(dw_ref)
            db_ref[...] = jnp.zeros_like(db_ref)

        last = s == nS - 1
        _fill_ext(xext_ref, _masked(xp_ref, s > 0), x_ref[...], _masked(xn_ref, ~last), ts)
        _fill_ext(dext_ref, jnp.zeros((HALO, W), F32), d_ref[...], _masked(dn_ref, ~last), ts)
        for l0, lc in _lane_chunks(W):
            w, bias = w_ref[:, l0:l0 + lc], b_ref[:, l0:l0 + lc]

            def chunk(i, acc):
                r0 = pl.multiple_of(i * rc, 8)
                xw = xext_ref[pl.ds(r0, rc + 2 * HALO), l0:l0 + lc]
                xc = _conv(xw, w, K) + bias
                sg = _sigmoid(xc)
                dxc = dext_ref[pl.ds(r0, rc + 2 * HALO), l0:l0 + lc] * (sg * (1.0 + xc * (1.0 - sg)))
                dx, gw = _conv_t(dxc, xw, w, K, rc)
                dx_ref[pl.ds(r0, rc), l0:l0 + lc] = dx.astype(dx_ref.dtype)
                gb = jnp.sum(dxc[HALO:HALO + rc], axis=0, keepdims=True)
                return tuple(a + g for a, g in zip(acc, gw + [gb]))

            acc = lax.fori_loop(0, ts // rc, chunk, tuple(jnp.zeros((1, lc), F32) for _ in range(K + 1)))
            for k in range(K):
                dw_ref[k:k + 1, l0:l0 + lc] += acc[k]
            db_ref[0:1, l0:l0 + lc] += acc[K]

    per = ts // HALO
    dtile_s = pl.BlockSpec((None, ts, W), lambda c, b_, s: (b_, s, c))
    dnext_s = pl.BlockSpec((None, HALO, W), lambda c, b_, s: (b_, jnp.minimum((s + 1) * per, nS * per - 1), c))
    xtile_s = pl.BlockSpec((None, ts, W), lambda c, b_, s: (b_, s, ob + c))
    xprev_s = pl.BlockSpec((None, HALO, W), lambda c, b_, s: (b_, jnp.maximum(s * per - 1, 0), ob + c))
    xnext_s = pl.BlockSpec((None, HALO, W), lambda c, b_, s: (b_, jnp.minimum((s + 1) * per, nS * per - 1), ob + c))
    wspec = pl.BlockSpec((8, W), lambda c, b_, s: (0, c))
    return pl.pallas_call(
        body, name=name, grid=(nW, Bl, nS),
        in_specs=[dtile_s, dnext_s, xtile_s, xprev_s, xnext_s, wspec, pl.BlockSpec((1, W), lambda c, b_, s: (0, c))],
        out_specs=[dtile_s, wspec, wspec],
        out_shape=[jax.ShapeDtypeStruct((Bl, S, CD), MXU_DTYPE), jax.ShapeDtypeStruct((8, CD), F32),
                   jax.ShapeDtypeStruct((8, CD), F32)],
        scratch_shapes=[pltpu.VMEM((ts + 2 * HALO, W), F32)] * 2,
        compiler_params=_cp(("arbitrary", "arbitrary", "arbitrary")),
    )(dxa, dxa, proj, proj, proj, w8, b.reshape(1, CD))


def _sc_conv_fwd(proj, w8, offs, D, name, ts=512, W=1024):
    Bl, S, _ = proj.shape
    K = 3
    ts, W = _tile(S, ts, 8), _tile(D, W)
    nS, nW = S // ts, D // W
    ob, oc, oh = [o // W for o in offs]

    def body(b_ref, c_ref, cp_ref, h_ref, hp_ref, w_ref, o_ref):
        s = pl.program_id(2)
        keep = s > 0
        vall = jnp.concatenate([_masked(cp_ref, keep) * _masked(hp_ref, keep), c_ref[...] * h_ref[...]], axis=0)
        o_ref[...] = (b_ref[...] * _conv(vall, w_ref, K)[HALO:]).astype(o_ref.dtype)

    tb, _, _ = _halo_specs(ts, W, nS, lambda c: ob + c)
    tc, pc, _ = _halo_specs(ts, W, nS, lambda c: oc + c)
    th, ph, _ = _halo_specs(ts, W, nS, lambda c: oh + c)
    return pl.pallas_call(
        body, name=name, grid=(Bl, nW, nS),
        in_specs=[tb, tc, pc, th, ph, pl.BlockSpec((8, W), lambda b_, c, s: (0, c))],
        out_specs=pl.BlockSpec((None, ts, W), lambda b_, c, s: (b_, s, c)),
        out_shape=jax.ShapeDtypeStruct((Bl, S, D), MXU_DTYPE),
        compiler_params=_cp(("parallel", "parallel", "parallel")),
    )(proj, proj, proj, proj, proj, w8)


def _sc_conv_bwd(ds, proj, w8, offs, D, name, ts=256, W=1024):
    Bl, S, _ = proj.shape
    K = 3
    ts, W = _tile(S, ts, 8), _tile(D, W)
    nS, nW = S // ts, D // W
    ob, oc, oh = [o // W for o in offs]

    def body(d_ref, dn_ref, b_ref, bn_ref, c_ref, cp_ref, cn_ref, h_ref, hp_ref, hn_ref, w_ref,
             db_ref, dc_ref, dh_ref, dw_ref):
        bb, s = pl.program_id(1), pl.program_id(2)

        @pl.when((bb == 0) & (s == 0))
        def _():
            dw_ref[...] = jnp.zeros_like(dw_ref)

        first, last = s > 0, s < nS - 1
        zeros = jnp.zeros((HALO, W), F32)
        c_t, h_t = c_ref[...], h_ref[...]
        vall = jnp.concatenate([_masked(cp_ref, first) * _masked(hp_ref, first), c_t * h_t,
                                _masked(cn_ref, last) * _masked(hn_ref, last)], axis=0)
        dcv = jnp.concatenate([zeros, d_ref[...] * b_ref[...], _masked(dn_ref, last) * _masked(bn_ref, last)], axis=0)
        cv = _conv(vall, w_ref, K)[HALO:HALO + ts]
        db_ref[...] = (d_ref[...] * cv).astype(db_ref.dtype)
        dv, gw = _conv_t(dcv, vall, w_ref, K, ts)
        dc_ref[...] = (dv * h_t).astype(dc_ref.dtype)
        dh_ref[...] = (dv * c_t).astype(dh_ref.dtype)
        for k in range(K):
            dw_ref[k:k + 1, :] += gw[k]

    per = ts // HALO

    def specs(o):
        t = pl.BlockSpec((None, ts, W), lambda c, b_, s: (b_, s, o + c))
        p = pl.BlockSpec((None, HALO, W), lambda c, b_, s: (b_, jnp.maximum(s * per - 1, 0), o + c))
        n = pl.BlockSpec((None, HALO, W), lambda c, b_, s: (b_, jnp.minimum((s + 1) * per, nS * per - 1), o + c))
        return t, p, n

    dt_, _, dn_ = specs(0)
    bt, _, bn = specs(ob)
    ct, cp, cn = specs(oc)
    ht, hp, hn = specs(oh)
    wspec = pl.BlockSpec((8, W), lambda c, b_, s: (0, c))
    act = jax.ShapeDtypeStruct((Bl, S, D), MXU_DTYPE)
    return pl.pallas_call(
        body, name=name, grid=(nW, Bl, nS),
        in_specs=[dt_, dn_, bt, bn, ct, cp, cn, ht, hp, hn, wspec],
        out_specs=[dt_, dt_, dt_, wspec],
        out_shape=[act, act, act, jax.ShapeDtypeStruct((8, D), F32)],
        compiler_params=_cp(("arbitrary", "arbitrary", "arbitrary")),
    )(ds, ds, proj, proj, proj, proj, proj, proj, proj, proj, w8)


def _ffn_conv_fwd(up, w8, b, DFF, name, ts=512, W=1408):
    Bl, S, _ = up.shape
    K = 3
    ts, W = _tile(S, ts, 8), _tile(DFF, W)
    nS, nW = S // ts, DFF // W

    rc = _tile(ts, ROW_CHUNK, 8)

    def body(g_ref, gp_ref, v_ref, vp_ref, wg_ref, wv_ref, bg_ref, bv_ref, o_ref, gext_ref, vext_ref):
        keep = pl.program_id(2) > 0
        gext_ref[0:HALO, :] = _masked(gp_ref, keep)
        gext_ref[HALO:HALO + ts, :] = g_ref[...]
        vext_ref[0:HALO, :] = _masked(vp_ref, keep)
        vext_ref[HALO:HALO + ts, :] = v_ref[...]
        for l0, lc in _lane_chunks(W):
            wg_, wv_ = wg_ref[:, l0:l0 + lc], wv_ref[:, l0:l0 + lc]
            bg_, bv_ = bg_ref[:, l0:l0 + lc], bv_ref[:, l0:l0 + lc]

            def chunk(i, carry):
                r0 = pl.multiple_of(i * rc, 8)
                ug = _conv(gext_ref[pl.ds(r0, rc + HALO), l0:l0 + lc], wg_, K)[HALO:] + bg_
                uv = _conv(vext_ref[pl.ds(r0, rc + HALO), l0:l0 + lc], wv_, K)[HALO:] + bv_
                o_ref[pl.ds(r0, rc), l0:l0 + lc] = (ug * _sigmoid(ug) * uv).astype(o_ref.dtype)
                return carry

            lax.fori_loop(0, ts // rc, chunk, 0)

    tg, pg, _ = _halo_specs(ts, W, nS, lambda c: c)
    tv, pv, _ = _halo_specs(ts, W, nS, lambda c: nW + c)
    wg = pl.BlockSpec((8, W), lambda b_, c, s: (0, c))
    wv = pl.BlockSpec((8, W), lambda b_, c, s: (0, nW + c))
    bg = pl.BlockSpec((1, W), lambda b_, c, s: (0, c))
    bv = pl.BlockSpec((1, W), lambda b_, c, s: (0, nW + c))
    b2 = b.reshape(1, 2 * DFF)
    return pl.pallas_call(
        body, name=name, grid=(Bl, nW, nS),
        in_specs=[tg, pg, tv, pv, wg, wv, bg, bv],
        out_specs=pl.BlockSpec((None, ts, W), lambda b_, c, s: (b_, s, c)),
        out_shape=jax.ShapeDtypeStruct((Bl, S, DFF), MXU_DTYPE),
        scratch_shapes=[pltpu.VMEM((ts + HALO, W), F32)] * 2,
        compiler_params=_cp(("parallel", "parallel", "parallel")),
    )(up, up, up, up, w8, w8, b2, b2)


def _ffn_conv_bwd(da, up, w8, b, DFF, name, ts=256, W=1408):
    Bl, S, _ = up.shape
    K = 3
    ts, W = _tile(S, ts, 8), _tile(DFF, W)
    nS, nW = S // ts, DFF // W

    rc = _tile(ts, ROW_CHUNK, 8)

    def body(d_ref, dn_ref, g_ref, gp_ref, gn_ref, v_ref, vp_ref, vn_ref, wg_ref, wv_ref, bg_ref, bv_ref,
             dg_ref, dv_ref, dwg_ref, dwv_ref, dbg_ref, dbv_ref, gext_ref, vext_ref, dext_ref):
        bb, s = pl.program_id(1), pl.program_id(2)

        @pl.when((bb == 0) & (s == 0))
        def _():
            for r in (dwg_ref, dwv_ref, dbg_ref, dbv_ref):
                r[...] = jnp.zeros_like(r)

        first, last = s > 0, s < nS - 1
        _fill_ext(gext_ref, _masked(gp_ref, first), g_ref[...], _masked(gn_ref, last), ts)
        _fill_ext(vext_ref, _masked(vp_ref, first), v_ref[...], _masked(vn_ref, last), ts)
        _fill_ext(dext_ref, jnp.zeros((HALO, W), F32), d_ref[...], _masked(dn_ref, last), ts)
        for l0, lc in _lane_chunks(W):
            wg_, wv_ = wg_ref[:, l0:l0 + lc], wv_ref[:, l0:l0 + lc]
            bg_, bv_ = bg_ref[:, l0:l0 + lc], bv_ref[:, l0:l0 + lc]

            def chunk(i, acc):
                r0 = pl.multiple_of(i * rc, 8)
                win = pl.ds(r0, rc + 2 * HALO)
                gw_, vw_, dw_ = gext_ref[win, l0:l0 + lc], vext_ref[win, l0:l0 + lc], dext_ref[win, l0:l0 + lc]
                ug = _conv(gw_, wg_, K) + bg_
                uv = _conv(vw_, wv_, K) + bv_
                sg = _sigmoid(ug)
                dug = dw_ * uv * (sg * (1.0 + ug * (1.0 - sg)))
                duv = dw_ * (ug * sg)
                dg, gg = _conv_t(dug, gw_, wg_, K, rc)
                dv, gv = _conv_t(duv, vw_, wv_, K, rc)
                dg_ref[pl.ds(r0, rc), l0:l0 + lc] = dg.astype(dg_ref.dtype)
                dv_ref[pl.ds(r0, rc), l0:l0 + lc] = dv.astype(dv_ref.dtype)
                new = gg + gv + [jnp.sum(dug[HALO:HALO + rc], axis=0, keepdims=True),
                                 jnp.sum(duv[HALO:HALO + rc], axis=0, keepdims=True)]
                return tuple(a + g for a, g in zip(acc, new))

            acc = lax.fori_loop(0, ts // rc, chunk, tuple(jnp.zeros((1, lc), F32) for _ in range(2 * K + 2)))
            for k in range(K):
                dwg_ref[k:k + 1, l0:l0 + lc] += acc[k]
                dwv_ref[k:k + 1, l0:l0 + lc] += acc[K + k]
            dbg_ref[0:1, l0:l0 + lc] += acc[2 * K]
            dbv_ref[0:1, l0:l0 + lc] += acc[2 * K + 1]

    per = ts // HALO

    def specs(o):
        t = pl.BlockSpec((None, ts, W), lambda c, b_, s: (b_, s, o + c))
        p = pl.BlockSpec((None, HALO, W), lambda c, b_, s: (b_, jnp.maximum(s * per - 1, 0), o + c))
        n = pl.BlockSpec((None, HALO, W), lambda c, b_, s: (b_, jnp.minimum((s + 1) * per, nS * per - 1), o + c))
        return t, p, n

    dt_, _, dn_ = specs(0)
    gt, gp, gn = specs(0)
    vt, vp, vn = specs(nW)
    wg = pl.BlockSpec((8, W), lambda c, b_, s: (0, c))
    wv = pl.BlockSpec((8, W), lambda c, b_, s: (0, nW + c))
    bg = pl.BlockSpec((1, W), lambda c, b_, s: (0, c))
    bv = pl.BlockSpec((1, W), lambda c, b_, s: (0, nW + c))
    b2 = b.reshape(1, 2 * DFF)
    act = jax.ShapeDtypeStruct((Bl, S, DFF), MXU_DTYPE)
    small = jax.ShapeDtypeStruct((8, DFF), F32)
    dg, dv, dwg, dwv, dbg, dbv = pl.pallas_call(
        body, name=name, grid=(nW, Bl, nS),
        in_specs=[dt_, dn_, gt, gp, gn, vt, vp, vn, wg, wv, bg, bv],
        out_specs=[dt_, dt_, wg, wg, wg, wg],
        out_shape=[act, act, small, small, small, small],
        scratch_shapes=[pltpu.VMEM((ts + 2 * HALO, W), F32)] * 3,
        compiler_params=_cp(("arbitrary", "arbitrary", "arbitrary")),
    )(da, da, up, up, up, up, up, up, w8, w8, b2, b2)
    return dg, dv, jnp.concatenate([dwg, dwv], axis=1), jnp.concatenate([dbg, dbv], axis=1)


def _merge_fwd(proj, y_ssd, y_sc, off, D, name, ts=512):
    Bl, S, _ = proj.shape
    ts = _tile(S, ts, 8)
    og = off // D

    def body(g1_ref, g2_ref, a_ref, b_ref, o_ref):
        o_ref[...] = (_sigmoid(g1_ref[...]) * a_ref[...] + _sigmoid(g2_ref[...]) * b_ref[...]).astype(o_ref.dtype)

    tok = pl.BlockSpec((None, ts, D), lambda b, s: (b, s, 0))
    return pl.pallas_call(
        body, name=name, grid=(Bl, S // ts),
        in_specs=[pl.BlockSpec((None, ts, D), lambda b, s: (b, s, og)),
                  pl.BlockSpec((None, ts, D), lambda b, s: (b, s, og + 1)), tok, tok],
        out_specs=tok, out_shape=jax.ShapeDtypeStruct((Bl, S, D), MXU_DTYPE),
        compiler_params=_cp(("parallel", "parallel")),
    )(proj, proj, y_ssd, y_sc)


def _merge_bwd(dmixin, proj, y_ssd, y_sc, off, D, name, ts=512):
    Bl, S, _ = proj.shape
    ts = _tile(S, ts, 8)
    og = off // D

    def body(d_ref, g1_ref, g2_ref, a_ref, b_ref, da_ref, db_ref, dg1_ref, dg2_ref):
        d = d_ref[...]
        s1, s2 = _sigmoid(g1_ref[...]), _sigmoid(g2_ref[...])
        da_ref[...] = (d * s1).astype(da_ref.dtype)
        db_ref[...] = (d * s2).astype(db_ref.dtype)
        dg1_ref[...] = (d * a_ref[...] * (s1 * (1.0 - s1))).astype(dg1_ref.dtype)
        dg2_ref[...] = (d * b_ref[...] * (s2 * (1.0 - s2))).astype(dg2_ref.dtype)

    tok = pl.BlockSpec((None, ts, D), lambda b, s: (b, s, 0))
    act = jax.ShapeDtypeStruct((Bl, S, D), MXU_DTYPE)
    return pl.pallas_call(
        body, name=name, grid=(Bl, S // ts),
        in_specs=[tok, pl.BlockSpec((None, ts, D), lambda b, s: (b, s, og)),
                  pl.BlockSpec((None, ts, D), lambda b, s: (b, s, og + 1)), tok, tok],
        out_specs=[tok, tok, tok, tok], out_shape=[act, act, act, act],
        compiler_params=_cp(("parallel", "parallel")),
    )(dmixin, proj, proj, y_ssd, y_sc)


def _two_term_dot(v, m01):
    hi = v.astype(MXU_DTYPE)
    lo = (v - hi.astype(F32)).astype(MXU_DTYPE)
    return _dg(hi, m01, NN) + _dg(lo, m01, NN)


def _head_to_channels(R):
    rp = R * HEAD_DIM
    return (lax.shift_right_logical(lax.broadcasted_iota(jnp.int32, (LANES, rp), 1), 6)
            == lax.broadcasted_iota(jnp.int32, (LANES, rp), 0)).astype(MXU_DTYPE)


def _expand_heads(mats, R):
    L = mats[0].shape[0]
    out = _two_term_dot(jnp.concatenate(mats, axis=0), _head_to_channels(R))
    return [out[i * L:(i + 1) * L, :] for i in range(len(mats))]


def _row_sums(v):
    return _two_term_dot(v, jnp.ones((v.shape[1], LANES), MXU_DTYPE))


def _lanes_to(v, width):
    return jnp.concatenate([v] * (width // LANES), axis=1)


def _head_rows(colvec, R, N):
    return jnp.concatenate([jnp.broadcast_to(colvec[r:r + 1, :], (HEAD_DIM, N)) for r in range(R)], axis=0)


def _ssd_prep(proj, dt_bias, a_log, selcat, selbig, dm, name):
    Bl, S, L, G = dm.Bl, dm.S, CHUNK, N_GROUPS
    nc = S // L
    odt = dm.ODT // LANES

    def body(dtr_ref, dtb_ref, alog_ref, selcat_ref, selbig_ref, dtg_ref, acs_ref, acst_ref):
        row = lax.broadcasted_iota(jnp.int32, (L, L), 0)
        col = lax.broadcasted_iota(jnp.int32, (L, L), 1)
        dt_all = _softplus(dtr_ref[...] + dtb_ref[...])
        acs_all = _dg((row >= col).astype(F32), dt_all * (-jnp.exp(alog_ref[...])), NN, HI)
        dtg_ref[...] = _dg(dt_all, selcat_ref[...], NN, HI)
        acs_ref[...] = _dg(acs_all, selcat_ref[...], NN, HI)
        acst_ref[...] = _dg(selbig_ref[...], acs_all, NT, HI)

    vec = pl.BlockSpec((1, LANES), lambda b, c: (0, 0))
    wide = pl.BlockSpec((None, L, G * LANES), lambda b, c: (b, c, 0))
    return pl.pallas_call(
        body, name=name, grid=(Bl, nc),
        in_specs=[pl.BlockSpec((None, L, LANES), lambda b, c: (b, c, odt)), vec, vec,
                  pl.BlockSpec((LANES, G * LANES), lambda b, c: (0, 0)),
                  pl.BlockSpec((G * 8, LANES), lambda b, c: (0, 0))],
        out_specs=[wide, wide, pl.BlockSpec((None, None, G * 8, L), lambda b, c: (b, c, 0, 0))],
        out_shape=[jax.ShapeDtypeStruct((Bl, S, G * LANES), F32), jax.ShapeDtypeStruct((Bl, S, G * LANES), F32),
                   jax.ShapeDtypeStruct((Bl, nc, G * 8, L), F32)],
        compiler_params=_cp(("parallel", "parallel")),
    )(proj, dt_bias, a_log, selcat, selbig)


def _ssd_post(d_a, ddt, dtg, proj, dt_bias, a_log, selcat, dm, name):
    Bl, S, L, G = dm.Bl, dm.S, CHUNK, N_GROUPS
    nc = S // L
    odt = dm.ODT // LANES

    def body(da_ref, ddt_ref, dtg_ref, dtr_ref, dtb_ref, alog_ref, selcat_ref, ddtr_ref, dpar_ref):
        @pl.when((pl.program_id(0) == 0) & (pl.program_id(1) == 0))
        def _():
            dpar_ref[...] = jnp.zeros_like(dpar_ref)

        row = lax.broadcasted_iota(jnp.int32, (L, L), 0)
        col = lax.broadcasted_iota(jnp.int32, (L, L), 1)
        selcat = selcat_ref[...]
        a_all = -jnp.exp(alog_ref[...])
        a4 = _dg(jnp.broadcast_to(a_all, (8, LANES)), selcat, NN, HI)[0:1, :]
        dadt = _dg((col >= row).astype(F32), da_ref[...], NN, HI)
        ddt4 = ddt_ref[...] + dadt * a4
        da4 = jnp.sum(dadt * dtg_ref[...], axis=0, keepdims=True)
        ddt_all = _dg(ddt4, selcat, NT, HI)
        da_all = _dg(jnp.broadcast_to(da4, (8, G * LANES)), selcat, NT, HI)[0:1, :]
        ddtr = ddt_all * _sigmoid(dtr_ref[...] + dtb_ref[...])
        ddtr_ref[...] = ddtr.astype(ddtr_ref.dtype)
        dpar_ref[0:1, :] += jnp.sum(ddtr, axis=0, keepdims=True)
        dpar_ref[1:2, :] += da_all * a_all

    vec = pl.BlockSpec((1, LANES), lambda b, c: (0, 0))
    wide = pl.BlockSpec((None, L, G * LANES), lambda b, c: (b, c, 0))
    return pl.pallas_call(
        body, name=name, grid=(Bl, nc),
        in_specs=[wide, wide, wide, pl.BlockSpec((None, L, LANES), lambda b, c: (b, c, odt)), vec, vec,
                  pl.BlockSpec((LANES, G * LANES), lambda b, c: (0, 0))],
        out_specs=[pl.BlockSpec((None, L, LANES), lambda b, c: (b, c, 0)), pl.BlockSpec((8, LANES), lambda b, c: (0, 0))],
        out_shape=[jax.ShapeDtypeStruct((Bl, S, LANES), MXU_DTYPE), jax.ShapeDtypeStruct((8, LANES), F32)],
        compiler_params=_cp(("arbitrary", "arbitrary")),
    )(d_a, ddt, dtg, proj, dt_bias, a_log, selcat)


def _scan_fwd(xbc_a, proj, dtg4, acs4, acs_t4, dsk_exp, norm_g, dm, name):
    Bl, S, DI, N, R, L, G = dm.Bl, dm.S, dm.DI, dm.N, dm.R, CHUNK, N_GROUPS
    RP = R * HEAD_DIM
    nc = S // L
    ob, ocm = DI // N, DI // N + G

    def body(xs_ref, bm_ref, cm_ref, z_ref, dtg_ref, acs_ref, acst_ref, dsk_ref, ng_ref, y_ref, yn_ref, hp_ref, h_ref):
        @pl.when(pl.program_id(2) == 0)
        def _():
            h_ref[...] = jnp.zeros_like(h_ref)

        causal = lax.broadcasted_iota(jnp.int32, (L, L), 0) >= lax.broadcasted_iota(jnp.int32, (L, L), 1)
        lane = lax.broadcasted_iota(jnp.int32, (L, LANES), 1)
        dtg, acs, acs_t = dtg_ref[...], acs_ref[...], acst_ref[...]
        xs = xs_ref[...]
        bmb, cmb = bm_ref[...].astype(MXU_DTYPE), cm_ref[...].astype(MXU_DTYPE)
        sg = _dg(cmb, bmb, NT)
        acs_last = acs[L - 1:L, :]
        dt_exp, ea_exp, ds_exp = _expand_heads([dtg, jnp.exp(acs), jnp.exp(acs_last - acs)], R)
        xdt = xs * dt_exp
        xb = xdt.astype(MXU_DTYPE)
        parts = []
        for q in range(R // 2):
            x2 = xb[:, LANES * q:LANES * (q + 1)]
            ys = []
            for r in (2 * q, 2 * q + 1):
                dec = jnp.exp(jnp.where(causal, acs[:, r:r + 1] - acs_t[r:r + 1, :], -1e30))
                ys.append(_dg((sg * dec).astype(MXU_DTYPE), x2, NN))
            parts.append(jnp.where(lane < HEAD_DIM, ys[0], ys[1]))
        ydiag = jnp.concatenate(parts, axis=1)
        h_cur = h_ref[...]
        hb = h_cur.astype(MXU_DTYPE)
        yoff = _dg(cmb, hb, NT) * ea_exp
        st = _dg((xdt * ds_exp).astype(MXU_DTYPE), bmb, TN)
        hp_ref[...] = hb
        h_ref[...] = h_cur * _head_rows(jnp.exp(acs_t[:, L - 1:L]), R, N) + st
        y = ydiag + yoff + dsk_ref[...] * xs
        y_ref[...] = y
        z = z_ref[...]
        yg = y * (z * _sigmoid(z))
        rr = lax.rsqrt(_row_sums(yg * yg) * (1.0 / RP) + EPS)
        yn_ref[...] = (yg * _lanes_to(rr, RP) * ng_ref[...]).astype(yn_ref.dtype)

    grp = pl.BlockSpec((None, L, RP), lambda b, g, c: (b, c, g))
    lanes = pl.BlockSpec((None, L, LANES), lambda b, g, c: (b, c, g))
    chan = pl.BlockSpec((1, RP), lambda b, g, c: (0, g))
    return pl.pallas_call(
        body, name=name, grid=(Bl, G, nc),
        in_specs=[grp,
                  pl.BlockSpec((None, L, N), lambda b, g, c: (b, c, ob + g)),
                  pl.BlockSpec((None, L, N), lambda b, g, c: (b, c, ocm + g)),
                  grp, lanes, lanes,
                  pl.BlockSpec((None, None, 8, L), lambda b, g, c: (b, c, g, 0)),
                  chan, chan],
        out_specs=[grp, grp, pl.BlockSpec((None, None, None, RP, N), lambda b, g, c: (b, g, c, 0, 0))],
        out_shape=[jax.ShapeDtypeStruct((Bl, S, DI), F32), jax.ShapeDtypeStruct((Bl, S, DI), MXU_DTYPE),
                   jax.ShapeDtypeStruct((Bl, G, nc, RP, N), MXU_DTYPE)],
        scratch_shapes=[pltpu.VMEM((RP, N), F32)],
        compiler_params=_cp(("parallel", "parallel", "arbitrary")),
    )(xbc_a, xbc_a, xbc_a, proj, dtg4, acs4, acs_t4, dsk_exp, norm_g.reshape(1, DI))


def _scan_bwd(dyn, y, xbc_a, proj, hprev, dtg4, acs4, acs_t4, dsk_exp, norm_g, dm, name):
    Bl, S, DI, N, R, L, G = dm.Bl, dm.S, dm.DI, dm.N, dm.R, CHUNK, N_GROUPS
    RP = R * HEAD_DIM
    nc = S // L
    ob, ocm = DI // N, DI // N + G

    def body(dyn_ref, y_ref, z_ref, xs_ref, bm_ref, cm_ref, hp_ref, dtg_ref, acs_ref, acst_ref, dsk_ref, ng_ref,
             dz_ref, dxs_ref, dbm_ref, dcm_ref, da_ref, ddt_ref, ddsk_ref, dng_ref, dh_ref):
        b, c = pl.program_id(1), pl.program_id(2)

        @pl.when(c == 0)
        def _():
            dh_ref[...] = jnp.zeros_like(dh_ref)

        @pl.when((b == 0) & (c == 0))
        def _():
            ddsk_ref[...] = jnp.zeros_like(ddsk_ref)
            dng_ref[...] = jnp.zeros_like(dng_ref)

        row = lax.broadcasted_iota(jnp.int32, (L, L), 0)
        col = lax.broadcasted_iota(jnp.int32, (L, L), 1)
        causal, anti = row >= col, col >= row
        lane = lax.broadcasted_iota(jnp.int32, (L, LANES), 1)
        etb = (lax.shift_right_logical(lax.broadcasted_iota(jnp.int32, (RP, LANES), 0), 6)
               == lax.broadcasted_iota(jnp.int32, (RP, LANES), 1)).astype(MXU_DTYPE)

        dtg, acs, acs_t = dtg_ref[...], acs_ref[...], acst_ref[...]
        xs, z, y, dyn = xs_ref[...], z_ref[...], y_ref[...], dyn_ref[...]
        bmb, cmb = bm_ref[...].astype(MXU_DTYPE), cm_ref[...].astype(MXU_DTYPE)
        hpb = hp_ref[...]
        ng = ng_ref[...]

        sz = _sigmoid(z)
        siluz = z * sz
        yg = y * siluz
        rr = _lanes_to(lax.rsqrt(_row_sums(yg * yg) * (1.0 / RP) + EPS), RP)
        yhat = yg * rr
        dng_ref[...] += jnp.sum(dyn * yhat, axis=0, keepdims=True)
        dyhat = dyn * ng
        dyg = rr * (dyhat - yhat * _lanes_to(_row_sums(dyhat * yhat) * (1.0 / RP), RP))
        dy = dyg * siluz
        dz_ref[...] = (dyg * y * (sz * (1.0 + z * (1.0 - sz)))).astype(dz_ref.dtype)

        dxs = dy * dsk_ref[...]
        ddsk_ref[...] += jnp.sum(dy * xs, axis=0, keepdims=True)

        acs_last = acs[L - 1:L, :]
        dt_exp, ea_exp, ds_exp = _expand_heads([dtg, jnp.exp(acs), jnp.exp(acs_last - acs)], R)
        xdt = xs * dt_exp
        xb = xdt.astype(MXU_DTYPE)
        dyb = dy.astype(MXU_DTYPE)
        cd = jnp.exp(acs_last)
        cd_rows = _head_rows(jnp.exp(acs_t[:, L - 1:L]), R, N)

        q_ = _dg(cmb, hpb, NT)
        dq = dy * ea_exp
        dqb = dq.astype(MXU_DTYPE)
        dcm = _dg(dqb, hpb, NN)
        dh_yoff = _dg(dqb, cmb, TN)

        dhn = dh_ref[...]
        wprod = dhn * hpb.astype(F32)
        per_head = jnp.concatenate(
            [jnp.sum(wprod[HEAD_DIM * r:HEAD_DIM * (r + 1), :], axis=0, keepdims=True) for r in range(R)]
            + ([jnp.zeros((8 - R, N), F32)] if R < 8 else []), axis=0)
        dcd_col = jnp.sum(per_head, axis=1, keepdims=True)
        diag8 = lax.broadcasted_iota(jnp.int32, (8, LANES), 0) == lax.broadcasted_iota(jnp.int32, (8, LANES), 1)
        dcd_lane = jnp.sum(jnp.where(diag8, dcd_col, 0.0), axis=0, keepdims=True)
        d_a_last = dcd_lane * cd
        dh_ref[...] = dhn * cd_rows + dh_yoff
        dhnb = dhn.astype(MXU_DTYPE)

        e_ = _dg(bmb, dhnb, NT)
        dxdt = ds_exp * e_
        xds = xdt * ds_exp
        dbm = _dg(xds.astype(MXU_DTYPE), dhnb, NN)

        sg = _dg(cmb, bmb, NT)
        sg_t = _dg(bmb, cmb, NT)
        dsg = jnp.zeros((L, L), F32)
        dsg_t = jnp.zeros((L, L), F32)
        d_a = jnp.zeros((L, LANES), F32)
        parts = []
        for q in range(R // 2):
            x2 = xb[:, LANES * q:LANES * (q + 1)]
            dy2 = dyb[:, LANES * q:LANES * (q + 1)]
            dxs2 = []
            for hh, r in enumerate((2 * q, 2 * q + 1)):
                mine = (lane < HEAD_DIM) if hh == 0 else (lane >= HEAD_DIM)
                diff = acs[:, r:r + 1] - acs_t[r:r + 1, :]
                dec = jnp.exp(jnp.where(causal, diff, -1e30))
                dec_t = jnp.exp(jnp.where(anti, -diff, -1e30))
                dy2m = jnp.where(mine, dy2, jnp.zeros_like(dy2))
                dm_ = _dg(dy2m, x2, NT)
                dm_t = _dg(x2, dy2m, NT)
                m_t = sg_t * dec_t
                da_col = jnp.sum(dm_ * (sg * dec) - dm_t * m_t, axis=1, keepdims=True)
                d_a = d_a + jnp.where(lane == r, da_col, 0.0)
                dsg = dsg + dm_ * dec
                dsg_t = dsg_t + dm_t * dec_t
                dxs2.append(_dg(m_t.astype(MXU_DTYPE), dy2, NN))
            parts.append(jnp.where(lane < HEAD_DIM, dxs2[0], dxs2[1]))
        dxdt = dxdt + jnp.concatenate(parts, axis=1)
        dcm_ref[...] = dcm + _dg(dsg.astype(MXU_DTYPE), bmb, NN)
        dbm_ref[...] = dbm + _dg(dsg_t.astype(MXU_DTYPE), cmb, NN)
        dxs_ref[...] = dxs + dxdt * dt_exp

        hs = _two_term_dot(jnp.concatenate([dq * q_ - xds * e_, xds * e_, dxdt * xs], axis=0), etb)
        t2 = hs[L:2 * L, :]
        rowl = lax.broadcasted_iota(jnp.int32, (L, LANES), 0)
        d_a_last = d_a_last + jnp.sum(t2, axis=0, keepdims=True)
        da_ref[...] = d_a + hs[0:L, :] + jnp.where(rowl == L - 1, d_a_last, 0.0)
        ddt_ref[...] = hs[2 * L:3 * L, :]

    grp = pl.BlockSpec((None, L, RP), lambda g, b, c: (b, nc - 1 - c, g))
    lanes = pl.BlockSpec((None, L, LANES), lambda g, b, c: (b, nc - 1 - c, g))
    bspec = pl.BlockSpec((None, L, N), lambda g, b, c: (b, nc - 1 - c, ob + g))
    cspec = pl.BlockSpec((None, L, N), lambda g, b, c: (b, nc - 1 - c, ocm + g))
    gn = pl.BlockSpec((None, L, N), lambda g, b, c: (b, nc - 1 - c, g))
    chan = pl.BlockSpec((1, RP), lambda g, b, c: (0, g))
    wide = jax.ShapeDtypeStruct((Bl, S, G * LANES), F32)
    return pl.pallas_call(
        body, name=name, grid=(G, Bl, nc),
        in_specs=[grp, grp, grp, grp, bspec, cspec,
                  pl.BlockSpec((None, None, None, RP, N), lambda g, b, c: (b, g, nc - 1 - c, 0, 0)),
                  lanes, lanes, pl.BlockSpec((None, None, 8, L), lambda g, b, c: (b, nc - 1 - c, g, 0)),
                  chan, chan],
        out_specs=[grp, grp, gn, gn, lanes, lanes, chan, chan],
        out_shape=[jax.ShapeDtypeStruct((Bl, S, DI), MXU_DTYPE), jax.ShapeDtypeStruct((Bl, S, DI), F32),
                   jax.ShapeDtypeStruct((Bl, S, G * N), F32), jax.ShapeDtypeStruct((Bl, S, G * N), F32),
                   wide, wide, jax.ShapeDtypeStruct((1, DI), F32), jax.ShapeDtypeStruct((1, DI), F32)],
        scratch_shapes=[pltpu.VMEM((RP, N), F32)],
        compiler_params=_cp(("arbitrary", "arbitrary", "arbitrary")),
    )(dyn, y, proj, xbc_a, xbc_a, xbc_a, hprev, dtg4, acs4, acs_t4, dsk_exp, norm_g.reshape(1, DI))


def _adam_update(w, m, v, g):
    c1 = 1.0 - ADAM_B1 ** ADAM_STEP
    c2 = 1.0 - ADAM_B2 ** ADAM_STEP
    nm = ADAM_B1 * m + (1.0 - ADAM_B1) * g
    nv = ADAM_B2 * v + (1.0 - ADAM_B2) * (g * g)
    return nm, nv, -ADAM_LR * ((nm / c1) / (jnp.sqrt(nv / c2) + ADAM_EPS) + ADAM_WD * w)


def _adamw(w, m, v, g, name, tr=128):
    rows, cols = w.shape
    tr = _tile(rows, tr, 8)

    def body(w_ref, m_ref, v_ref, ga_ref, g_ref, d_ref, nm_ref, nv_ref):
        g = ga_ref[...]
        g_ref[...] = g
        nm_ref[...], nv_ref[...], d_ref[...] = _adam_update(w_ref[...], m_ref[...], v_ref[...], g)

    blk = pl.BlockSpec((tr, cols), lambda i: (i, 0))
    shp = jax.ShapeDtypeStruct((rows, cols), F32)
    return pl.pallas_call(
        body, name=name, grid=(rows // tr,), in_specs=[blk] * 4, out_specs=[blk] * 4,
        out_shape=[shp] * 4, compiler_params=_cp(("parallel",)),
    )(w, m, v, g)


def _adamw_layers(w, m, v, g_mine, g_theirs, core, name, tr=128):
    _, rows, cols = w.shape
    tr = _tile(rows, tr, 8)

    def body(c_ref, w_ref, m_ref, v_ref, ga_ref, gb_ref, g_ref, d_ref, nm_ref, nv_ref):
        g = jnp.where(pl.program_id(0) == c_ref[0], ga_ref[...], gb_ref[...])
        g_ref[...] = g
        nm_ref[...], nv_ref[...], d_ref[...] = _adam_update(w_ref[...], m_ref[...], v_ref[...], g)

    lay = pl.BlockSpec((None, tr, cols), lambda l, i, c_ref: (l, i, 0))
    one = pl.BlockSpec((tr, cols), lambda l, i, c_ref: (i, 0))
    shp = jax.ShapeDtypeStruct(w.shape, F32)
    return pl.pallas_call(
        body, name=name,
        grid_spec=pltpu.PrefetchScalarGridSpec(num_scalar_prefetch=1, grid=(2, rows // tr),
                                               in_specs=[lay, lay, lay, one, one], out_specs=[lay] * 4),
        out_shape=[shp] * 4, compiler_params=_cp(("parallel", "parallel")),
    )(core, w, m, v, g_mine, g_theirs)


def _sum_slots(buf, name, tr=256):
    n, rows, cols = buf.shape
    tr = _tile(rows, tr, 8)

    def body(b_ref, o_ref):
        acc = b_ref[0].astype(F32)
        for k in range(1, n):
            acc = acc + b_ref[k].astype(F32)
        o_ref[...] = acc

    return pl.pallas_call(
        body, name=name, grid=(rows // tr,),
        in_specs=[pl.BlockSpec((n, tr, cols), lambda i: (0, i, 0))],
        out_specs=pl.BlockSpec((tr, cols), lambda i: (i, 0)),
        out_shape=jax.ShapeDtypeStruct((rows, cols), F32), compiler_params=_cp(("parallel",)),
    )(buf)


_ANY = pl.BlockSpec(memory_space=pl.ANY)


def _exchange_chips(src, per_dest, name):
    rows, cols = src.shape[-2:]

    def body(in_ref, out_ref, send_sems, recv_sems, local_sem):
        x, y, c = lax.axis_index("x"), lax.axis_index("y"), lax.axis_index("c")
        me = 2 * x + y
        chips = [(1 - x, y), (x, 1 - y), (1 - x, 1 - y)]

        def block(j):
            return in_ref.at[j] if per_dest else in_ref

        mine = pltpu.make_async_copy(block(me), out_ref.at[me], local_sem)
        mine.start()
        sends = []
        for k, (px, py) in enumerate(chips):
            cp = pltpu.make_async_remote_copy(
                src_ref=block(2 * px + py), dst_ref=out_ref.at[me], send_sem=send_sems.at[k],
                recv_sem=recv_sems.at[k], device_id=(px, py, c), device_id_type=MESH)
            cp.start()
            sends.append(cp)
        for k, (px, py) in enumerate(chips):
            pltpu.make_async_remote_copy(
                src_ref=block(me), dst_ref=out_ref.at[2 * px + py], send_sem=send_sems.at[k],
                recv_sem=recv_sems.at[k], device_id=(px, py, c), device_id_type=MESH).wait_recv()
        for cp in sends:
            cp.wait_send()
        mine.wait()

    return pl.pallas_call(
        body, name=name, in_specs=[_ANY], out_specs=_ANY,
        out_shape=jax.ShapeDtypeStruct((N_CHIPS, rows, cols), src.dtype),
        scratch_shapes=[pltpu.SemaphoreType.DMA((3,)), pltpu.SemaphoreType.DMA((3,)), pltpu.SemaphoreType.DMA(())],
    )(src)


def _shard_window(ref, kind, j, lead=()):
    if kind == "slots":
        return ref.at[(j,) + lead]
    r, c = ref.shape[-2] // (N_CHIPS if kind == "rows" else 1), ref.shape[-1] // (N_CHIPS if kind == "cols" else 1)
    full = tuple(slice(None) for _ in range(len(ref.shape) - 2 - len(lead)))
    if kind == "rows":
        return ref.at[lead + full + (pl.ds(pl.multiple_of(j * r, 16), r), slice(None))]
    return ref.at[lead + full + (slice(None), pl.ds(pl.multiple_of(j * c, LANES), c))]


def _gather_kind(shard, axis):
    if axis == 1:
        return "rows"
    return "cols" if shard.shape[2] % LANES == 0 else "slots"


def _gather_weights(shards, axes, name):
    kinds = ["rows" if ax == 1 else "slots" for ax in axes]
    nw = len(shards)

    def out_shape(s, kind):
        d, r, c = s.shape
        shp = {"rows": (d, N_CHIPS * r, c), "cols": (d, r, N_CHIPS * c), "slots": (N_CHIPS, d, r, c)}[kind]
        return jax.ShapeDtypeStruct(shp, s.dtype)

    assert all(s.shape[0] == 2 for s in shards)

    def body(*refs):
        ins, outs = refs[:nw], refs[nw:2 * nw]
        ici_send, ici_recv, d2d_send, d2d_recv, local_sems = refs[2 * nw:]
        x, y, c = lax.axis_index("x"), lax.axis_index("y"), lax.axis_index("c")
        me = 2 * x + y
        chips = [(1 - x, y), (x, 1 - y), (1 - x, 1 - y)]
        sends = []
        for i in range(nw):
            for k, (px, py) in enumerate(chips):
                cp = pltpu.make_async_remote_copy(
                    src_ref=ins[i].at[c], dst_ref=_shard_window(outs[i], kinds[i], me, (c,)),
                    send_sem=ici_send.at[3 * i + k], recv_sem=ici_recv.at[3 * i + k],
                    device_id=(px, py, c), device_id_type=MESH)
                cp.start()
                sends.append((cp, True))
        for i in range(nw):
            for k, (px, py) in enumerate(chips):
                win = _shard_window(outs[i], kinds[i], 2 * px + py, (c,))
                pltpu.make_async_remote_copy(
                    src_ref=ins[i].at[c], dst_ref=win, send_sem=ici_send.at[3 * i + k], recv_sem=ici_recv.at[3 * i + k],
                    device_id=(px, py, c), device_id_type=MESH).wait_recv()
                fw = pltpu.make_async_remote_copy(
                    src_ref=win, dst_ref=win, send_sem=d2d_send.at[3 * i + k], recv_sem=d2d_recv.at[3 * i + k],
                    device_id=(x, y, 1 - c), device_id_type=MESH)
                fw.start()
                sends.append((fw, True))
        for i in range(nw):
            own = pltpu.make_async_copy(ins[i], _shard_window(outs[i], kinds[i], me), local_sems.at[i])
            own.start()
            sends.append((own, False))
        for i in range(nw):
            for k, (px, py) in enumerate(chips):
                win = _shard_window(outs[i], kinds[i], 2 * px + py, (1 - c,))
                pltpu.make_async_remote_copy(
                    src_ref=win, dst_ref=win, send_sem=d2d_send.at[3 * i + k], recv_sem=d2d_recv.at[3 * i + k],
                    device_id=(x, y, 1 - c), device_id_type=MESH).wait_recv()
        for cp, remote in sends:
            cp.wait_send() if remote else cp.wait()

    outs = pl.pallas_call(
        body, name=name, in_specs=[_ANY] * nw, out_specs=[_ANY] * nw,
        out_shape=[out_shape(s, k) for s, k in zip(shards, kinds)],
        scratch_shapes=[pltpu.SemaphoreType.DMA((3 * nw,)), pltpu.SemaphoreType.DMA((3 * nw,)),
                        pltpu.SemaphoreType.DMA((3 * nw,)), pltpu.SemaphoreType.DMA((3 * nw,)),
                        pltpu.SemaphoreType.DMA((nw,))],
    )(*shards)
    return [jnp.concatenate([o[j] for j in range(N_CHIPS)], axis=2) if k == "slots" else o for o, k in zip(outs, kinds)]


def _swap_other_layer(gst, name):
    nv = len(gst)

    def body(*refs):
        ins, outs, send_sems, recv_sems = refs[:nv], refs[nv:2 * nv], refs[2 * nv], refs[2 * nv + 1]
        x, y, c = lax.axis_index("x"), lax.axis_index("y"), lax.axis_index("c")
        cps = [pltpu.make_async_remote_copy(src_ref=ins[i].at[1 - c], dst_ref=outs[i], send_sem=send_sems.at[i],
                                            recv_sem=recv_sems.at[i], device_id=(x, y, 1 - c), device_id_type=MESH)
               for i in range(nv)]
        for cp in cps:
            cp.start()
        for cp in cps:
            cp.wait()

    return pl.pallas_call(
        body, name=name, in_specs=[_ANY] * nv, out_specs=[_ANY] * nv,
        out_shape=[jax.ShapeDtypeStruct(v.shape[1:], v.dtype) for v in gst],
        scratch_shapes=[pltpu.SemaphoreType.DMA((nv,)), pltpu.SemaphoreType.DMA((nv,))],
    )(*gst)


def _pair_sum(g, other, core, name, tr=256):
    _, rows, cols = g.shape
    tr = _tile(rows, tr, 16)

    def body(c_ref, g_ref, o_ref, s_ref):
        s_ref[...] = (g_ref[...].astype(F32) + o_ref[...].astype(F32)).astype(s_ref.dtype)

    blk = pl.BlockSpec((tr, cols), lambda i, c_ref: (i, 0))
    return pl.pallas_call(
        body, name=name,
        grid_spec=pltpu.PrefetchScalarGridSpec(
            num_scalar_prefetch=1, grid=(rows // tr,),
            in_specs=[pl.BlockSpec((None, tr, cols), lambda i, c_ref: (c_ref[0], i, 0)), blk], out_specs=blk),
        out_shape=jax.ShapeDtypeStruct((rows, cols), g.dtype), compiler_params=_cp(("parallel",)),
    )(core, g, other)


def _scatter_layer(parts, kinds, name):
    nw = len(parts)

    def shard_shape(g, kind):
        return g.shape[-2] // (N_CHIPS if kind == "rows" else 1), g.shape[-1] // (N_CHIPS if kind == "cols" else 1)

    def body(*refs):
        ins, outs = refs[:nw], refs[nw:2 * nw]
        send_sems, recv_sems, local_sems = refs[2 * nw:]
        x, y, c = lax.axis_index("x"), lax.axis_index("y"), lax.axis_index("c")
        me = 2 * x + y
        chips = [(1 - x, y), (x, 1 - y), (1 - x, 1 - y)]
        sends = []
        for i in range(nw):
            own = pltpu.make_async_copy(_shard_window(ins[i], kinds[i], me), outs[i].at[me], local_sems.at[i])
            own.start()
            sends.append((own, False))
            for k, (px, py) in enumerate(chips):
                cp = pltpu.make_async_remote_copy(
                    src_ref=_shard_window(ins[i], kinds[i], 2 * px + py), dst_ref=outs[i].at[me],
                    send_sem=send_sems.at[3 * i + k], recv_sem=recv_sems.at[3 * i + k],
                    device_id=(px, py, c), device_id_type=MESH)
                cp.start()
                sends.append((cp, True))
        for i in range(nw):
            for k, (px, py) in enumerate(chips):
                pltpu.make_async_remote_copy(
                    src_ref=_shard_window(ins[i], kinds[i], me), dst_ref=outs[i].at[2 * px + py],
                    send_sem=send_sems.at[3 * i + k], recv_sem=recv_sems.at[3 * i + k],
                    device_id=(px, py, c), device_id_type=MESH).wait_recv()
        for cp, remote in sends:
            cp.wait_send() if remote else cp.wait()

    return pl.pallas_call(
        body, name=name, in_specs=[_ANY] * nw, out_specs=[_ANY] * nw,
        out_shape=[jax.ShapeDtypeStruct((N_CHIPS,) + shard_shape(g, k), g.dtype) for g, k in zip(parts, kinds)],
        scratch_shapes=[pltpu.SemaphoreType.DMA((3 * nw,)), pltpu.SemaphoreType.DMA((3 * nw,)),
                        pltpu.SemaphoreType.DMA((nw,))],
    )(*parts)


def _sibling_swap(vs, name):
    nv = len(vs)

    def body(*refs):
        ins, outs, send_sems, recv_sems = refs[:nv], refs[nv:2 * nv], refs[2 * nv], refs[2 * nv + 1]
        x, y, c = lax.axis_index("x"), lax.axis_index("y"), lax.axis_index("c")
        cps = [pltpu.make_async_remote_copy(src_ref=ins[i], dst_ref=outs[i], send_sem=send_sems.at[i],
                                            recv_sem=recv_sems.at[i], device_id=(x, y, 1 - c), device_id_type=MESH)
               for i in range(nv)]
        for cp in cps:
            cp.start()
        for cp in cps:
            cp.wait()

    return pl.pallas_call(
        body, name=name, in_specs=[_ANY] * nv, out_specs=[_ANY] * nv,
        out_shape=[jax.ShapeDtypeStruct(v.shape, v.dtype) for v in vs],
        scratch_shapes=[pltpu.SemaphoreType.DMA((nv,)), pltpu.SemaphoreType.DMA((nv,))],
    )(*vs)


def _allgather_all(v, name):
    rows, cols = v.shape

    def body(in_ref, out_ref, send_sems, recv_sems, local_sem):
        x, y, c = lax.axis_index("x"), lax.axis_index("y"), lax.axis_index("c")
        me = 4 * x + 2 * y + c
        peers = []
        for k in range(1, N_DEV):
            peers.append(((1 - x) if k & 4 else x, (1 - y) if k & 2 else y, (1 - c) if k & 1 else c))
        mine = pltpu.make_async_copy(in_ref, out_ref.at[me], local_sem)
        mine.start()
        sends = []
        for k, peer in enumerate(peers):
            cp = pltpu.make_async_remote_copy(src_ref=in_ref, dst_ref=out_ref.at[me], send_sem=send_sems.at[k],
                                              recv_sem=recv_sems.at[k], device_id=peer, device_id_type=MESH)
            cp.start()
            sends.append(cp)
        for k, (px, py, pc) in enumerate(peers):
            pltpu.make_async_remote_copy(src_ref=in_ref, dst_ref=out_ref.at[4 * px + 2 * py + pc],
                                         send_sem=send_sems.at[k], recv_sem=recv_sems.at[k],
                                         device_id=(px, py, pc), device_id_type=MESH).wait_recv()
        for cp in sends:
            cp.wait_send()
        mine.wait()

    return pl.pallas_call(
        body, name=name, in_specs=[_ANY], out_specs=_ANY,
        out_shape=jax.ShapeDtypeStruct((N_DEV, rows, cols), v.dtype),
        scratch_shapes=[pltpu.SemaphoreType.DMA((N_DEV - 1,)), pltpu.SemaphoreType.DMA((N_DEV - 1,)),
                        pltpu.SemaphoreType.DMA(())],
    )(v)


def _pack(arrs, dtype, width, row_mult):
    flat = jnp.concatenate([a.reshape(-1).astype(dtype) for a in arrs])
    unit = width * row_mult
    total = -(-flat.shape[0] // unit) * unit
    return jnp.pad(flat, (0, total - flat.shape[0])).reshape(-1, width)


def _unpack(buf, shapes):
    flat = buf.reshape(-1)
    out, off = [], 0
    for shp in shapes:
        n = 1
        for d in shp:
            n *= d
        out.append(flat[off:off + n].reshape(shp))
        off += n
    return out


class _Dims:
    pass


def _dims(x, ssd_dt_bias, ssd_norm_g, ssd_conv_b, ffn_conv_b):
    dm = _Dims()
    dm.Bl, dm.S, dm.D = x.shape
    dm.H, dm.DI, dm.CD = ssd_dt_bias.shape[-1], ssd_norm_g.shape[-1], ssd_conv_b.shape[-1]
    dm.N = (dm.CD - dm.DI) // (2 * N_GROUPS)
    dm.R = dm.H // N_GROUPS
    dm.DFF = ffn_conv_b.shape[-1] // 2
    dm.OX = dm.DI
    dm.OB = dm.DI + dm.CD
    dm.OC = dm.OB + dm.D
    dm.OH = dm.OC + dm.D
    dm.OG = dm.OH + dm.D
    dm.ODT = dm.OG + 2 * dm.D
    dm.NP = dm.ODT + LANES
    assert dm.DI // dm.H == HEAD_DIM and dm.N == LANES and dm.R % 2 == 0 and dm.S % CHUNK == 0 and dm.H <= LANES
    return dm


def _permute_w_in(w, dm):
    o = dm.DI + dm.CD
    pad = jnp.zeros((w.shape[0], LANES - dm.H), w.dtype)
    return jnp.concatenate([w[:, :o], w[:, o + dm.H:], w[:, o:o + dm.H], pad], axis=1)


def _unpermute_w_in(dw, dm):
    o = dm.DI + dm.CD
    return jnp.concatenate([dw[..., :o], dw[..., dm.ODT:dm.ODT + dm.H], dw[..., o:dm.ODT]], axis=-1)


def _lane_pad(v):
    return jnp.pad(v.reshape(1, -1).astype(F32), ((0, 0), (0, LANES - v.shape[-1])))


def _pad8(w):
    return jnp.pad(w.astype(F32), ((0, 8 - w.shape[0]), (0, 0)))


def _head_select(dm):
    j = jnp.arange(LANES)[None, :, None]
    r = jnp.arange(LANES)[None, None, :]
    g = jnp.arange(N_GROUPS)[:, None, None]
    sel = ((j == dm.R * g + r) & (r < dm.R)).astype(F32)
    selcat = jnp.transpose(sel, (1, 0, 2)).reshape(LANES, N_GROUPS * LANES)
    selbig = jnp.transpose(sel[:, :, :8], (0, 2, 1)).reshape(N_GROUPS * 8, LANES)
    return selcat, selbig


def _mix_fwd(dm, h, w, sp, sel, tag):
    Bl, S, D = dm.Bl, dm.S, dm.D
    T = Bl * S
    proj = _matmul(h.reshape(T, D), w["w_in_p"], "NN", F32, tag + "_in_proj", tn=1152).reshape(Bl, S, dm.NP)
    xbc_a = _ssd_conv_fwd(proj, sp["ssd_conv_w8"], sp["ssd_conv_b"], dm.OX, dm.CD, tag + "_ssd_conv")
    dtg, acs, acs_t = _ssd_prep(proj, sp["dt_bias"], sp["a_log"], sel[0], sel[1], dm, tag + "_ssd_prep")
    y, yn, hprev = _scan_fwd(xbc_a, proj, dtg, acs, acs_t, sp["dsk_exp"], sp["ssd_norm_g"], dm, tag + "_ssd_scan")
    y_ssd = _matmul(yn.reshape(T, dm.DI), w["w_ssd_out"], "NN", F32, tag + "_ssd_out", tk=2048).reshape(Bl, S, D)
    s = _sc_conv_fwd(proj, sp["sc_conv_w8"], (dm.OB, dm.OC, dm.OH), D, tag + "_sc_conv")
    y_sc = _matmul(s.reshape(T, D), w["w_sc_out"], "NN", F32, tag + "_sc_out", tk=1024).reshape(Bl, S, D)
    mixin = _merge_fwd(proj, y_ssd, y_sc, dm.OG, D, tag + "_merge")
    mix = _matmul(mixin.reshape(T, D), w["w_o"], "NN", F32, tag + "_o", tk=1024).reshape(Bl, S, D)
    return mix, (h, proj, xbc_a, y, yn, hprev, y_ssd, y_sc, s, mixin, dtg, acs, acs_t)


def _mix_bwd(dm, dmix, saved, w, sp, sel, tag, gbig, slab):
    Bl, S, D = dm.Bl, dm.S, dm.D
    T = Bl * S
    h, proj, xbc_a, y, yn, hprev, y_ssd, y_sc, s, mixin, dtg, acs, acs_t = saved
    dmix2 = dmix.reshape(T, D)
    g = {}
    gbig["w_o"] = _matmul(mixin.reshape(T, D), dmix2, "TN", WIRE_DTYPE, tag + "_dw_o", slab=(gbig.get("w_o"),) + slab)
    dmixin = _matmul(dmix2, w["w_o"], "NT", F32, tag + "_d_o", tk=1024).reshape(Bl, S, D)
    dy_ssd, dy_sc, dg1, dg2 = _merge_bwd(dmixin, proj, y_ssd, y_sc, dm.OG, D, tag + "_merge_bwd")
    gbig["w_sc_out"] = _matmul(s.reshape(T, D), dy_sc.reshape(T, D), "TN", WIRE_DTYPE, tag + "_dw_sc_out",
                               slab=(gbig.get("w_sc_out"),) + slab)
    ds = _matmul(dy_sc.reshape(T, D), w["w_sc_out"], "NT", F32, tag + "_d_sc_out", tk=1024).reshape(Bl, S, D)
    dscb, dscc, dsch, dscw = _sc_conv_bwd(ds, proj, sp["sc_conv_w8"], (dm.OB, dm.OC, dm.OH), D, tag + "_sc_conv_bwd")
    g["sc_conv_w"] = dscw[:3]
    gbig["w_ssd_out"] = _matmul(yn.reshape(T, dm.DI), dy_ssd.reshape(T, D), "TN", WIRE_DTYPE, tag + "_dw_ssd_out",
                                slab=(gbig.get("w_ssd_out"),) + slab)
    dyn = _matmul(dy_ssd.reshape(T, D), w["w_ssd_out"], "NT", F32, tag + "_d_ssd_out", tk=1024).reshape(Bl, S, dm.DI)
    dz, dxs, dbm, dcm, d_a, ddt, ddsk, dng = _scan_bwd(dyn, y, xbc_a, proj, hprev, dtg, acs, acs_t, sp["dsk_exp"],
                                                       sp["ssd_norm_g"], dm, tag + "_ssd_scan_bwd")
    ddtr, dpar = _ssd_post(d_a, ddt, dtg, proj, sp["dt_bias"], sp["a_log"], sel[0], dm, tag + "_ssd_post")
    g["ssd_dt_bias"], g["ssd_a_log"] = dpar[0, :dm.H], dpar[1, :dm.H]
    g["ssd_d"] = jnp.sum(ddsk.reshape(dm.H, HEAD_DIM), axis=-1)
    g["ssd_norm_g"] = dng[0]
    dxa = jnp.concatenate([dxs, dbm, dcm], axis=-1)
    dxbc, dcw, dcb = _ssd_conv_bwd(dxa, proj, sp["ssd_conv_w8"], sp["ssd_conv_b"], dm.OX, dm.CD, tag + "_ssd_conv_bwd")
    g["ssd_conv_w"], g["ssd_conv_b"] = dcw[:4], dcb[0]
    dproj = jnp.concatenate([dz, dxbc, dscb, dscc, dsch, dg1, dg2, ddtr], axis=-1).reshape(T, dm.NP)
    gbig["w_in_p"] = _matmul(h.reshape(T, D), dproj, "TN", WIRE_DTYPE, tag + "_dw_in", tn=1152,
                             slab=(gbig.get("w_in_p"),) + slab)
    dh = _matmul(dproj, w["w_in_p"], "NT", F32, tag + "_d_in", tk=1152).reshape(Bl, S, D)
    return dh, g


def _ffn_fwd(dm, h, w, sp, tag):
    Bl, S, D = dm.Bl, dm.S, dm.D
    T = Bl * S
    up = _matmul(h.reshape(T, D), w["w_up"], "NN", F32, tag + "_up", tn=1408).reshape(Bl, S, 2 * dm.DFF)
    a = _ffn_conv_fwd(up, sp["ffn_conv_w8"], sp["ffn_conv_b"], dm.DFF, tag + "_ffn_conv")
    f = _matmul(a.reshape(T, dm.DFF), w["w_down"], "NN", F32, tag + "_down", tk=2816).reshape(Bl, S, D)
    return f, (h, up, a)


def _ffn_bwd(dm, df, saved, w, sp, tag, gbig, slab):
    Bl, S, D = dm.Bl, dm.S, dm.D
    T = Bl * S
    h, up, a = saved
    df2 = df.reshape(T, D)
    g = {}
    gbig["w_down"] = _matmul(a.reshape(T, dm.DFF), df2, "TN", WIRE_DTYPE, tag + "_dw_down", tm=1408,
                             slab=(gbig.get("w_down"),) + slab)
    da = _matmul(df2, w["w_down"], "NT", F32, tag + "_d_down", tn=1408).reshape(Bl, S, dm.DFF)
    dg, dv, dcw, dcb = _ffn_conv_bwd(da, up, sp["ffn_conv_w8"], sp["ffn_conv_b"], dm.DFF, tag + "_ffn_conv_bwd")
    g["ffn_conv_w"], g["ffn_conv_b"] = dcw[:3], dcb[0]
    dup = jnp.concatenate([dg, dv], axis=-1).reshape(T, 2 * dm.DFF)
    gbig["w_up"] = _matmul(h.reshape(T, D), dup, "TN", WIRE_DTYPE, tag + "_dw_up", tn=1408,
                           slab=(gbig.get("w_up"),) + slab)
    dh = _matmul(dup, w["w_up"], "NT", F32, tag + "_d_up", tk=1408).reshape(Bl, S, D)
    return dh, g


def _local_step(dm, x, c, target, wfull, small):
    Bl, S, D = dm.Bl, dm.S, dm.D
    depth = len(wfull)
    sel = _head_select(dm)
    c16 = jnp.pad(c.astype(F32), ((0, 16 - Bl), (0, 0)))
    sps, mods, acts = [], [], []
    for l in range(depth):
        sm = small[l]
        sps.append(dict(
            ssd_conv_w8=_pad8(sm["ssd_conv_w"]), ssd_conv_b=sm["ssd_conv_b"], dt_bias=_lane_pad(sm["ssd_dt_bias"]),
            a_log=_lane_pad(sm["ssd_a_log"]), dsk_exp=jnp.repeat(sm["ssd_d"].astype(F32), HEAD_DIM).reshape(1, dm.DI),
            ssd_norm_g=sm["ssd_norm_g"],
            sc_conv_w8=_pad8(sm["sc_conv_w"]), ffn_conv_w8=_pad8(sm["ffn_conv_w"]), ffn_conv_b=sm["ffn_conv_b"]))
        act, mod = _ada_fwd(c16, wfull[l]["ada_w"], sm["ada_b"], f"l{l}_ada")
        acts.append(act)
        mods.append(jnp.pad(mod[:Bl].reshape(Bl, 6, D), ((0, 0), (0, 2), (0, 0))))

    def sub(i):
        l, ffn = i // 2, i % 2
        sm = small[l]
        return dict(l=l, ffn=ffn, pre_g=sm["ffn_pre_g" if ffn else "mix_pre_g"],
                    post_g=sm["ffn_post_g" if ffn else "mix_post_g"], mod=mods[l], row=3 * ffn,
                    tag=f"l{l}_{'ffn' if ffn else 'mix'}")

    nsub = 2 * depth
    subs = [sub(i) for i in range(nsub)]
    xs, fs, saves = [x], [], []
    h = _norm_mod_fwd(x, subs[0]["pre_g"], subs[0]["mod"], subs[0]["row"], "l0_mix_pre_norm")
    for i, sb in enumerate(subs):
        l = sb["l"]
        if sb["ffn"]:
            f, sv = _ffn_fwd(dm, h, wfull[l], sps[l], sb["tag"])
        else:
            f, sv = _mix_fwd(dm, h, wfull[l], sps[l], sel, sb["tag"])
        nxt = None
        if i + 1 < nsub:
            nb = subs[i + 1]
            nxt = (nb["pre_g"], nb["mod"], nb["row"])
        xn, h = _post_norm_fwd(xs[-1], f, sb["post_g"], sb["mod"], sb["row"] + 2, nxt, sb["tag"] + "_post_norm")
        xs.append(xn)
        fs.append(f)
        saves.append(sv)

    dy, loss = _loss_fwd_bwd(xs[-1], target, "loss")

    grads = [dict() for _ in range(depth)]
    gbig = {}
    dmod = [[None] * 6 for _ in range(depth)]
    dx, dh = dy, None
    for i in reversed(range(nsub)):
        sb = subs[i]
        l = sb["l"]
        nxt = None
        if i + 1 < nsub:
            nb = subs[i + 1]
            nxt = (dh, xs[i + 1], nb["pre_g"], nb["mod"], nb["row"])
        dx, df, pb, shg = _norm_bwd(dx, nxt, (fs[i], sb["post_g"], sb["mod"], sb["row"] + 2), sb["tag"] + "_post_norm_bwd")
        if nxt is not None:
            nb = subs[i + 1]
            dmod[nb["l"]][nb["row"]], dmod[nb["l"]][nb["row"] + 1] = pb[:, 0], pb[:, 1]
            grads[nb["l"]]["ffn_pre_g" if nb["ffn"] else "mix_pre_g"] = shg[0]
        dmod[l][sb["row"] + 2] = pb[:, 2]
        grads[l]["ffn_post_g" if sb["ffn"] else "mix_post_g"] = shg[1]
        if sb["ffn"]:
            dh, g = _ffn_bwd(dm, df, saves[i], wfull[l], sps[l], sb["tag"], gbig, (l, depth))
        else:
            dh, g = _mix_bwd(dm, df, saves[i], wfull[l], sps[l], sel, sb["tag"], gbig, (l, depth))
        grads[l].update(g)
    sb = subs[0]
    grad_x, _, pb, shg = _norm_bwd(dx, (dh, xs[0], sb["pre_g"], sb["mod"], sb["row"]), None, "l0_mix_pre_norm_bwd")
    dmod[0][0], dmod[0][1] = pb[:, 0], pb[:, 1]
    grads[0]["mix_pre_g"] = shg[0]

    for l in range(depth):
        dm6 = jnp.concatenate(dmod[l], axis=-1)
        grads[l]["ada_b"] = jnp.sum(dm6, axis=0)
        dm16 = jnp.pad(dm6, ((0, 16 - Bl), (0, 0))).astype(MXU_DTYPE)
        gbig["ada_w"] = _matmul(acts[l], dm16, "TN", WIRE_DTYPE, f"l{l}_dw_ada", slab=(gbig.get("ada_w"), l, depth))
    return loss, grad_x, grads, gbig


_WEIGHTS = ("ada_w", "ada_b", "mix_pre_g", "mix_post_g", "w_in", "ssd_conv_w", "ssd_conv_b", "ssd_dt_bias",
            "ssd_a_log", "ssd_d", "ssd_norm_g", "w_ssd_out", "sc_conv_w", "w_sc_out", "w_o", "ffn_pre_g",
            "ffn_post_g", "w_up", "ffn_conv_w", "ffn_conv_b", "w_down")
_INPUTS = ("x", "c") + _WEIGHTS + ("loss_target",) + tuple("m_" + n for n in _WEIGHTS) + tuple("v_" + n for n in _WEIGHTS)
_BIG = {"ada_w": 2, "w_in": 2, "w_ssd_out": 1, "w_sc_out": 1, "w_o": 1, "w_up": 2, "w_down": 1}
_CONV = ("ssd_conv_w", "sc_conv_w", "ffn_conv_w")
_SMALL = tuple(n for n in _WEIGHTS if n not in _BIG)


def _step(a):
    x, c, target = a["x"], a["c"], a["loss_target"]
    depth = a["ada_w"].shape[0]
    dm = _dims(x, a["ssd_dt_bias"], a["ssd_norm_g"], a["ssd_conv_b"], a["ffn_conv_b"])
    chip = 2 * lax.axis_index("x") + lax.axis_index("y")

    shards = [a[n].astype(WIRE_DTYPE) for n in _BIG]
    axes = list(_BIG.values())
    kinds = [_gather_kind(s, ax) for s, ax in zip(shards, axes)]
    full = {n: w.astype(MXU_DTYPE) for n, w in zip(_BIG, _gather_weights(shards, axes, "gather_weights"))}
    conv_shapes = [a[n].shape for n in _CONV]
    gotc = _exchange_chips(_pack([a[n] for n in _CONV], F32, LANES, 8), False, "gather_conv_weights")
    piecesc = [_unpack(gotc[j], conv_shapes) for j in range(N_CHIPS)]
    fullc = {n: jnp.concatenate([piecesc[j][i] for j in range(N_CHIPS)], axis=2) for i, n in enumerate(_CONV)}

    wfull, small = [], []
    for l in range(depth):
        wf = {n: full[n][l] for n in _BIG if n != "w_in"}
        wf["w_in_p"] = _permute_w_in(full["w_in"][l], dm)
        wfull.append(wf)
        small.append({n: (fullc[n][l] if n in _CONV else a[n][l]) for n in _SMALL})

    loss_part, grad_x, grads, gbig = _local_step(dm, x, c, target, wfull, small)

    core = lax.axis_index("c").astype(jnp.int32).reshape(1)
    gst = []
    for n, kind in zip(_BIG, kinds):
        g = _unpermute_w_in(gbig["w_in_p"], dm) if n == "w_in" else gbig[n]
        if kind == "slots":
            g = jnp.moveaxis(g.reshape(depth, g.shape[1], N_CHIPS, g.shape[2] // N_CHIPS), 2, 1)
            g = g.reshape(depth, -1, g.shape[-1])
        gst.append(g)
    other = _swap_other_layer(gst, "swap_layer_grads")
    parts = [_pair_sum(g, o, core, "pair_sum_" + n) for g, o, n in zip(gst, other, _BIG)]
    parts = [p.reshape(N_CHIPS, -1, p.shape[-1]) if k == "slots" else p for p, k in zip(parts, kinds)]
    got = _scatter_layer(parts, kinds, "scatter_grads")
    mine = [_sum_slots(g, "sum_chip_grads_" + n) for g, n in zip(got, _BIG)]
    theirs = _sibling_swap(mine, "swap_core_grads")

    out = {}
    for i, n in enumerate(_BIG):
        out[n] = _adamw_layers(a[n], a["m_" + n], a["v_" + n], mine[i], theirs[i], core, "adamw_" + n)

    gsmall = [jnp.stack([grads[l][n] for l in range(depth)]) for n in _SMALL]
    small_shapes = [g.shape for g in gsmall]
    summed = _sum_slots(_allgather_all(_pack(gsmall, F32, LANES, 8), "gather_small_grads"), "sum_small_grads")
    gs = dict(zip(_SMALL, _unpack(summed, small_shapes)))
    for n in _CONV:
        wcols = a[n].shape[2]
        gs[n] = lax.dynamic_slice_in_dim(gs[n], chip * wcols, wcols, axis=2)
    local_shapes = [a[n].shape for n in _SMALL]
    res = _adamw(_pack([a[n] for n in _SMALL], F32, LANES, 8), _pack([a["m_" + n] for n in _SMALL], F32, LANES, 8),
                 _pack([a["v_" + n] for n in _SMALL], F32, LANES, 8), _pack([gs[n] for n in _SMALL], F32, LANES, 8),
                 "adamw_small")
    res = [_unpack(r, local_shapes) for r in res]
    for i, n in enumerate(_SMALL):
        out[n] = [r[i] for r in res]

    loss = lax.psum(loss_part, ("x", "y", "c"))
    return (loss, grad_x) + tuple(out[n][k] for k in range(4) for n in _WEIGHTS)


def kernel(x, c, ada_w, ada_b, mix_pre_g, mix_post_g, w_in, ssd_conv_w, ssd_conv_b, ssd_dt_bias, ssd_a_log, ssd_d, ssd_norm_g, w_ssd_out, sc_conv_w, w_sc_out, w_o, ffn_pre_g, ffn_post_g, w_up, ffn_conv_w, ffn_conv_b, w_down, loss_target, m_ada_w, m_ada_b, m_mix_pre_g, m_mix_post_g, m_w_in, m_ssd_conv_w, m_ssd_conv_b, m_ssd_dt_bias, m_ssd_a_log, m_ssd_d, m_ssd_norm_g, m_w_ssd_out, m_sc_conv_w, m_w_sc_out, m_w_o, m_ffn_pre_g, m_ffn_post_g, m_w_up, m_ffn_conv_w, m_ffn_conv_b, m_w_down, v_ada_w, v_ada_b, v_mix_pre_g, v_mix_post_g, v_w_in, v_ssd_conv_w, v_ssd_conv_b, v_ssd_dt_bias, v_ssd_a_log, v_ssd_d, v_ssd_norm_g, v_w_ssd_out, v_sc_conv_w, v_w_sc_out, v_w_o, v_ffn_pre_g, v_ffn_post_g, v_w_up, v_ffn_conv_w, v_ffn_conv_b, v_w_down):
    return _step(dict(zip(_INPUTS, (
        x, c, ada_w, ada_b, mix_pre_g, mix_post_g, w_in, ssd_conv_w, ssd_conv_b, ssd_dt_bias, ssd_a_log, ssd_d, ssd_norm_g, w_ssd_out, sc_conv_w, w_sc_out, w_o, ffn_pre_g, ffn_post_g, w_up, ffn_conv_w, ffn_conv_b, w_down, loss_target, m_ada_w, m_ada_b, m_mix_pre_g, m_mix_post_g, m_w_in, m_ssd_conv_w, m_ssd_conv_b, m_ssd_dt_bias, m_ssd_a_log, m_ssd_d, m_ssd_norm_g, m_w_ssd_out, m_sc_conv_w, m_w_sc_out, m_w_o, m_ffn_pre_g, m_ffn_post_g, m_w_up, m_ffn_conv_w, m_ffn_conv_b, m_w_down, v_ada_w, v_ada_b, v_mix_pre_g, v_mix_post_g, v_w_in, v_ssd_conv_w, v_ssd_conv_b, v_ssd_dt_bias, v_ssd_a_log, v_ssd_d, v_ssd_norm_g, v_w_ssd_out, v_sc_conv_w, v_w_sc_out, v_w_o, v_ffn_pre_g, v_ffn_post_g, v_w_up, v_ffn_conv_w, v_ffn_conv_b, v_w_down))))
```

```python
import math

import jax
import jax.numpy as jnp
from jax import lax
from jax.experimental import pallas as pl
from jax.experimental.pallas import tpu as pltpu

F32 = jnp.float32
MXU_DTYPE = jnp.bfloat16
WIRE_DTYPE = jnp.bfloat16
HI = lax.Precision.HIGHEST
EPS = 1e-6
N_GROUPS = 4
CHUNK = 128
HEAD_DIM = 64
LANES = 128
HALO = 8
N_CHIPS = 4
N_DEV = 8
VMEM_LIMIT = 56 * 1024 * 1024
ADAM_LR, ADAM_B1, ADAM_B2, ADAM_EPS, ADAM_WD, ADAM_STEP = 0.001, 0.9, 0.999, 1e-08, 0.01, 10
MESH = pl.DeviceIdType.MESH

NN = (((1,), (0,)), ((), ()))
NT = (((1,), (1,)), ((), ()))
TN = (((0,), (0,)), ((), ()))


def _dg(a, b, dn, precision=None):
    return lax.dot_general(a, b, dn, precision=precision, preferred_element_type=F32)


def _tile(dim, pref, mult=LANES):
    t = (min(pref, dim) // mult) * mult
    while t >= mult:
        if dim % t == 0:
            return t
        t -= mult
    return dim


def _cp(sem):
    return pltpu.CompilerParams(dimension_semantics=sem, vmem_limit_bytes=VMEM_LIMIT)


def _sigmoid(x):
    return 1.0 / (1.0 + jnp.exp(-x))


def _softplus(x):
    return jnp.maximum(x, 0.0) + jnp.log1p(jnp.exp(-jnp.abs(x)))


def _matmul(a, b, mode, out_dtype, name, tm=1024, tn=1024, tk=1024, slab=None):
    if mode == "NN":
        (M, K), N = a.shape, b.shape[1]
    elif mode == "NT":
        (M, K), N = a.shape, b.shape[0]
    else:
        (K, M), N = a.shape, b.shape[1]
    tm, tn, tk = _tile(M, tm), _tile(N, tn), _tile(K, tk)
    nk = K // tk
    dn = {"NN": NN, "NT": NT, "TN": TN}[mode]
    carry = slab is not None and slab[0] is not None

    def body_one(a_ref, b_ref, *rest):
        o_ref = rest[-1]
        o_ref[...] = _dg(a_ref[...], b_ref[...], dn).astype(o_ref.dtype)

    def body_acc(a_ref, b_ref, *rest):
        o_ref, acc_ref = rest[-2:]
        k = pl.program_id(2)

        @pl.when(k == 0)
        def _():
            acc_ref[...] = jnp.zeros_like(acc_ref)

        acc_ref[...] += _dg(a_ref[...], b_ref[...], dn)

        @pl.when(k == nk - 1)
        def _():
            o_ref[...] = acc_ref[...].astype(o_ref.dtype)

    a_spec = (pl.BlockSpec((tk, tm), lambda i, j, k: (k, i)) if mode == "TN"
              else pl.BlockSpec((tm, tk), lambda i, j, k: (i, k)))
    b_spec = (pl.BlockSpec((tn, tk), lambda i, j, k: (j, k)) if mode == "NT"
              else pl.BlockSpec((tk, tn), lambda i, j, k: (k, j)))
    if slab is None:
        out_spec = pl.BlockSpec((tm, tn), lambda i, j, k: (i, j))
        out_shape = jax.ShapeDtypeStruct((M, N), out_dtype)
    else:
        layer = slab[1]
        out_spec = pl.BlockSpec((None, tm, tn), lambda i, j, k: (layer, i, j))
        out_shape = jax.ShapeDtypeStruct((slab[2], M, N), out_dtype)
    return pl.pallas_call(
        body_one if nk == 1 else body_acc, name=name, grid=(M // tm, N // tn, nk),
        in_specs=[a_spec, b_spec] + ([_ANY] if carry else []),
        out_specs=out_spec, out_shape=out_shape,
        input_output_aliases={2: 0} if carry else {},
        scratch_shapes=[] if nk == 1 else [pltpu.VMEM((tm, tn), F32)],
        compiler_params=_cp(("parallel", "parallel", "arbitrary")),
    )(*((a, b, slab[0]) if carry else (a, b)))


def _ada_fwd(c16, ada_w, ada_b, name):
    rows, D = c16.shape
    N6 = ada_w.shape[1]
    tn = _tile(N6, 1536)

    def body(c_ref, w_ref, b_ref, act_ref, mod_ref):
        c = c_ref[...]
        act = (c * _sigmoid(c)).astype(act_ref.dtype)
        act_ref[...] = act
        mod_ref[...] = _dg(act, w_ref[...], NN) + b_ref[...]

    return pl.pallas_call(
        body, name=name, grid=(N6 // tn,),
        in_specs=[pl.BlockSpec((rows, D), lambda j: (0, 0)),
                  pl.BlockSpec((D, tn), lambda j: (0, j)),
                  pl.BlockSpec((1, tn), lambda j: (0, j))],
        out_specs=[pl.BlockSpec((rows, D), lambda j: (0, 0)),
                   pl.BlockSpec((rows, tn), lambda j: (0, j))],
        out_shape=[jax.ShapeDtypeStruct((rows, D), MXU_DTYPE),
                   jax.ShapeDtypeStruct((rows, N6), F32)],
        compiler_params=_cp(("arbitrary",)),
    )(c16, ada_w, ada_b.reshape(1, N6))


def _norm_mod_rows(x, g, sc, sh):
    r = lax.rsqrt(jnp.mean(x * x, axis=-1, keepdims=True) + EPS)
    return ((x * r) * g) * (1.0 + sc) + sh


def _norm_mod_fwd(x, g, mod, row_sh, name, ts=512):
    Bl, S, D = x.shape
    ts = _tile(S, ts, 8)

    def body(x_ref, g_ref, mod_ref, h_ref):
        sh = mod_ref[row_sh:row_sh + 1, :]
        sc = mod_ref[row_sh + 1:row_sh + 2, :]
        h_ref[...] = _norm_mod_rows(x_ref[...], g_ref[...], sc, sh).astype(h_ref.dtype)

    tok = pl.BlockSpec((None, ts, D), lambda b, s: (b, s, 0))
    return pl.pallas_call(
        body, name=name, grid=(Bl, S // ts),
        in_specs=[tok, pl.BlockSpec((1, D), lambda b, s: (0, 0)),
                  pl.BlockSpec((None, 8, D), lambda b, s: (b, 0, 0))],
        out_specs=tok, out_shape=jax.ShapeDtypeStruct((Bl, S, D), MXU_DTYPE),
        compiler_params=_cp(("parallel", "parallel")),
    )(x, g.reshape(1, D), mod)


def _post_norm_fwd(xp, f, post_g, mod, row_gt, nxt, name, ts=512):
    Bl, S, D = xp.shape
    ts = _tile(S, ts, 8)
    has_next = nxt is not None

    def body(*refs):
        if has_next:
            xp_ref, f_ref, pg_ref, mod_ref, ng_ref, nmod_ref, x_ref, h_ref = refs
        else:
            xp_ref, f_ref, pg_ref, mod_ref, x_ref = refs
        f = f_ref[...]
        r = lax.rsqrt(jnp.mean(f * f, axis=-1, keepdims=True) + EPS)
        x = xp_ref[...] + mod_ref[row_gt:row_gt + 1, :] * ((f * r) * pg_ref[...])
        x_ref[...] = x
        if has_next:
            rs = nxt[2]
            h_ref[...] = _norm_mod_rows(x, ng_ref[...], nmod_ref[rs + 1:rs + 2, :], nmod_ref[rs:rs + 1, :]).astype(h_ref.dtype)

    tok = pl.BlockSpec((None, ts, D), lambda b, s: (b, s, 0))
    vec = pl.BlockSpec((1, D), lambda b, s: (0, 0))
    modspec = pl.BlockSpec((None, 8, D), lambda b, s: (b, 0, 0))
    ins = [xp, f, post_g.reshape(1, D), mod]
    in_specs = [tok, tok, vec, modspec]
    out_specs = [tok]
    out_shape = [jax.ShapeDtypeStruct((Bl, S, D), F32)]
    if has_next:
        ins += [nxt[0].reshape(1, D), nxt[1]]
        in_specs += [vec, modspec]
        out_specs += [tok]
        out_shape += [jax.ShapeDtypeStruct((Bl, S, D), MXU_DTYPE)]
    out = pl.pallas_call(
        body, name=name, grid=(Bl, S // ts), in_specs=in_specs, out_specs=out_specs, out_shape=out_shape,
        compiler_params=_cp(("parallel", "parallel")),
    )(*ins)
    return (out[0], out[1]) if has_next else (out[0], None)


def _loss_fwd_bwd(y, target, name, ts=512):
    Bl, S, D = y.shape
    ts = _tile(S, ts, 8)

    def body(y_ref, t_ref, dy_ref, l_ref):
        @pl.when((pl.program_id(0) == 0) & (pl.program_id(1) == 0))
        def _():
            l_ref[...] = jnp.zeros_like(l_ref)

        e = y_ref[...] - t_ref[...]
        dy_ref[...] = e * (1.0 / D)
        l_ref[...] += 0.5 * jnp.sum(jnp.mean(e * e, axis=-1, keepdims=True), axis=0, keepdims=True)

    tok = pl.BlockSpec((None, ts, D), lambda b, s: (b, s, 0))
    dy, l = pl.pallas_call(
        body, name=name, grid=(Bl, S // ts), in_specs=[tok, tok],
        out_specs=[tok, pl.BlockSpec((8, LANES), lambda b, s: (0, 0))],
        out_shape=[jax.ShapeDtypeStruct((Bl, S, D), F32), jax.ShapeDtypeStruct((8, LANES), F32)],
        compiler_params=_cp(("arbitrary", "arbitrary")),
    )(y, target)
    return dy, l[0, 0]


def _norm_bwd(dx_res, nxt, prv, name, ts=256):
    Bl, S, D = dx_res.shape
    ts = _tile(S, ts, 8)
    has_next, has_prev = nxt is not None, prv is not None

    def body(*refs):
        refs = list(refs)
        dxr_ref = refs.pop(0)
        if has_next:
            dh_ref, x_ref, g_ref, nmod_ref = refs[:4]
            refs = refs[4:]
        if has_prev:
            f_ref, pg_ref, pmod_ref = refs[:3]
            refs = refs[3:]
        dx_ref = refs.pop(0)
        if has_prev:
            df_ref = refs.pop(0)
        pb_ref, sh_ref = refs
        b, s = pl.program_id(0), pl.program_id(1)

        @pl.when(s == 0)
        def _():
            pb_ref[...] = jnp.zeros_like(pb_ref)

        @pl.when((b == 0) & (s == 0))
        def _():
            sh_ref[...] = jnp.zeros_like(sh_ref)

        dx = dxr_ref[...]
        if has_next:
            rs = nxt[4]
            x, dh, g = x_ref[...], dh_ref[...], g_ref[...]
            sc1 = 1.0 + nmod_ref[rs + 1:rs + 2, :]
            r = lax.rsqrt(jnp.mean(x * x, axis=-1, keepdims=True) + EPS)
            xn = x * r
            pb_ref[0:1, :] += jnp.sum(dh, axis=0, keepdims=True)
            pb_ref[1:2, :] += jnp.sum(dh * (xn * g), axis=0, keepdims=True)
            sh_ref[0:1, :] += jnp.sum(dh * sc1 * xn, axis=0, keepdims=True)
            dxn = dh * sc1 * g
            dx = dx + r * (dxn - xn * jnp.mean(dxn * xn, axis=-1, keepdims=True))
        dx_ref[...] = dx
        if has_prev:
            rg = prv[3]
            f, pg = f_ref[...], pg_ref[...]
            gt = pmod_ref[rg:rg + 1, :]
            r = lax.rsqrt(jnp.mean(f * f, axis=-1, keepdims=True) + EPS)
            fn = f * r
            pb_ref[2:3, :] += jnp.sum(dx * (fn * pg), axis=0, keepdims=True)
            drn = dx * gt
            sh_ref[1:2, :] += jnp.sum(drn * fn, axis=0, keepdims=True)
            dfn = drn * pg
            df_ref[...] = (r * (dfn - fn * jnp.mean(dfn * fn, axis=-1, keepdims=True))).astype(df_ref.dtype)

    tok = pl.BlockSpec((None, ts, D), lambda b, s: (b, s, 0))
    vec = pl.BlockSpec((1, D), lambda b, s: (0, 0))
    modspec = pl.BlockSpec((None, 8, D), lambda b, s: (b, 0, 0))
    ins, in_specs = [dx_res], [tok]
    if has_next:
        ins += [nxt[0], nxt[1], nxt[2].reshape(1, D), nxt[3]]
        in_specs += [tok, tok, vec, modspec]
    if has_prev:
        ins += [prv[0], prv[1].reshape(1, D), prv[2]]
        in_specs += [tok, vec, modspec]
    out_specs, out_shape = [tok], [jax.ShapeDtypeStruct((Bl, S, D), F32)]
    if has_prev:
        out_specs += [tok]
        out_shape += [jax.ShapeDtypeStruct((Bl, S, D), MXU_DTYPE)]
    out_specs += [modspec, pl.BlockSpec((8, D), lambda b, s: (0, 0))]
    out_shape += [jax.ShapeDtypeStruct((Bl, 8, D), F32), jax.ShapeDtypeStruct((8, D), F32)]
    out = pl.pallas_call(
        body, name=name, grid=(Bl, S // ts), in_specs=in_specs, out_specs=out_specs, out_shape=out_shape,
        compiler_params=_cp(("arbitrary", "arbitrary")),
    )(*ins)
    if has_prev:
        return out[0], out[1], out[2], out[3]
    return out[0], None, out[1], out[2]


def _shift_down(x, j):
    return x if j == 0 else pltpu.roll(x, j, axis=0)


def _shift_up(x, j):
    return x if j == 0 else pltpu.roll(x, x.shape[0] - j, axis=0)


def _conv(xall, w_ref, K):
    y = w_ref[K - 1:K, :] * xall
    for k in range(K - 1):
        y = y + w_ref[k:k + 1, :] * _shift_down(xall, K - 1 - k)
    return y


def _conv_t(dall, xall, w, K, rows):
    xt = xall[HALO:HALO + rows]
    y = w[K - 1:K, :] * dall[HALO:HALO + rows]
    gw = [None] * K
    gw[K - 1] = jnp.sum(dall[HALO:HALO + rows] * xt, axis=0, keepdims=True)
    for k in range(K - 1):
        sh = _shift_up(dall, K - 1 - k)[HALO:HALO + rows]
        y = y + w[k:k + 1, :] * sh
        gw[k] = jnp.sum(sh * xt, axis=0, keepdims=True)
    return y, gw


def _conv_t_rows(dall, w_ref, K):
    y = w_ref[K - 1:K, :] * dall
    for k in range(K - 1):
        y = y + w_ref[k:k + 1, :] * _shift_up(dall, K - 1 - k)
    return y


def _conv_wgrad(acc_ref, dtile, xall, K, ts):
    for k in range(K):
        xs = _shift_down(xall, K - 1 - k)[HALO:HALO + ts]
        acc_ref[k:k + 1, :] += jnp.sum(dtile * xs, axis=0, keepdims=True)


ROW_CHUNK = 64


def _lane_chunks(W):
    lc = 2 * LANES if W % (2 * LANES) == 0 else LANES
    return [(j * lc, lc) for j in range(W // lc)]


def _fill_ext(ext_ref, prev, tile, nxt, ts):
    ext_ref[0:HALO, :] = prev
    ext_ref[HALO:HALO + ts, :] = tile
    ext_ref[HALO + ts:2 * HALO + ts, :] = nxt


def _halo_specs(ts, W, nS, colblk):
    per = ts // HALO
    tile = pl.BlockSpec((None, ts, W), lambda b, c, s: (b, s, colblk(c)))
    prev = pl.BlockSpec((None, HALO, W), lambda b, c, s: (b, jnp.maximum(s * per - 1, 0), colblk(c)))
    nxt = pl.BlockSpec((None, HALO, W), lambda b, c, s: (b, jnp.minimum((s + 1) * per, nS * per - 1), colblk(c)))
    return tile, prev, nxt


def _masked(ref, keep):
    v = ref[...]
    return jnp.where(keep, v, jnp.zeros_like(v))


def _ssd_conv_fwd(proj, w8, b, off, CD, name, ts=512, W=1024):
    Bl, S, _ = proj.shape
    K = 4
    ts, W = _tile(S, ts, 8), _tile(math.gcd(CD, off), W)
    assert off % W == 0
    nS, nW, ob = S // ts, CD // W, off // W

    def body(x_ref, xp_ref, w_ref, b_ref, o_ref):
        xall = jnp.concatenate([_masked(xp_ref, pl.program_id(2) > 0), x_ref[...]], axis=0)
        xc = _conv(xall, w_ref, K)[HALO:] + b_ref[...]
        o_ref[...] = xc * _sigmoid(xc)

    tile, prev, _ = _halo_specs(ts, W, nS, lambda c: ob + c)
    return pl.pallas_call(
        body, name=name, grid=(Bl, nW, nS),
        in_specs=[tile, prev, pl.BlockSpec((8, W), lambda b_, c, s: (0, c)), pl.BlockSpec((1, W), lambda b_, c, s: (0, c))],
        out_specs=pl.BlockSpec((None, ts, W), lambda b_, c, s: (b_, s, c)),
        out_shape=jax.ShapeDtypeStruct((Bl, S, CD), F32),
        compiler_params=_cp(("parallel", "parallel", "parallel")),
    )(proj, proj, w8, b.reshape(1, CD))


def _ssd_conv_bwd(dxa, proj, w8, b, off, CD, name, ts=256, W=1024):
    Bl, S, _ = proj.shape
    K = 4
    ts, W = _tile(S, ts, 8), _tile(math.gcd(CD, off), W)
    nS, nW, ob = S // ts, CD // W, off // W

    rc = _tile(ts, ROW_CHUNK, 8)

    def body(d_ref, dn_ref, x_ref, xp_ref, xn_ref, w_ref, b_ref, dx_ref, dw_ref, db_ref, xext_ref, dext_ref):
        bb, s = pl.program_id(1), pl.program_id(2)

        @pl.when((bb == 0) & (s == 0))
        def _():
            dw_ref[...] = jnp.zeros_like(dw_ref)
            db_ref[...] = jnp.zeros_like(db_ref)

        last = s == nS - 1
        _fill_ext(xext_ref, _masked(xp_ref, s > 0), x_ref[...], _masked(xn_ref, ~last), ts)
        _fill_ext(dext_ref, jnp.zeros((HALO, W), F32), d_ref[...], _masked(dn_ref, ~last), ts)
        for l0, lc in _lane_chunks(W):
            w, bias = w_ref[:, l0:l0 + lc], b_ref[:, l0:l0 + lc]

            def chunk(i, acc):
                r0 = pl.multiple_of(i * rc, 8)
                xw = xext_ref[pl.ds(r0, rc + 2 * HALO), l0:l0 + lc]
                xc = _conv(xw, w, K) + bias
                sg = _sigmoid(xc)
                dxc = dext_ref[pl.ds(r0, rc + 2 * HALO), l0:l0 + lc] * (sg * (1.0 + xc * (1.0 - sg)))
                dx, gw = _conv_t(dxc, xw, w, K, rc)
                dx_ref[pl.ds(r0, rc), l0:l0 + lc] = dx.astype(dx_ref.dtype)
                gb = jnp.sum(dxc[HALO:HALO + rc], axis=0, keepdims=True)
                return tuple(a + g for a, g in zip(acc, gw + [gb]))

            acc = lax.fori_loop(0, ts // rc, chunk, tuple(jnp.zeros((1, lc), F32) for _ in range(K + 1)))
            for k in range(K):
                dw_ref[k:k + 1, l0:l0 + lc] += acc[k]
            db_ref[0:1, l0:l0 + lc] += acc[K]

    per = ts // HALO
    dtile_s = pl.BlockSpec((None, ts, W), lambda c, b_, s: (b_, s, c))
    dnext_s = pl.BlockSpec((None, HALO, W), lambda c, b_, s: (b_, jnp.minimum((s + 1) * per, nS * per - 1), c))
    xtile_s = pl.BlockSpec((None, ts, W), lambda c, b_, s: (b_, s, ob + c))
    xprev_s = pl.BlockSpec((None, HALO, W), lambda c, b_, s: (b_, jnp.maximum(s * per - 1, 0), ob + c))
    xnext_s = pl.BlockSpec((None, HALO, W), lambda c, b_, s: (b_, jnp.minimum((s + 1) * per, nS * per - 1), ob + c))
    wspec = pl.BlockSpec((8, W), lambda c, b_, s: (0, c))
    return pl.pallas_call(
        body, name=name, grid=(nW, Bl, nS),
        in_specs=[dtile_s, dnext_s, xtile_s, xprev_s, xnext_s, wspec, pl.BlockSpec((1, W), lambda c, b_, s: (0, c))],
        out_specs=[dtile_s, wspec, wspec],
        out_shape=[jax.ShapeDtypeStruct((Bl, S, CD), MXU_DTYPE), jax.ShapeDtypeStruct((8, CD), F32),
                   jax.ShapeDtypeStruct((8, CD), F32)],
        scratch_shapes=[pltpu.VMEM((ts + 2 * HALO, W), F32)] * 2,
        compiler_params=_cp(("arbitrary", "arbitrary", "arbitrary")),
    )(dxa, dxa, proj, proj, proj, w8, b.reshape(1, CD))


def _sc_conv_fwd(proj, w8, offs, D, name, ts=512, W=1024):
    Bl, S, _ = proj.shape
    K = 3
    ts, W = _tile(S, ts, 8), _tile(D, W)
    nS, nW = S // ts, D // W
    ob, oc, oh = [o // W for o in offs]

    def body(b_ref, c_ref, cp_ref, h_ref, hp_ref, w_ref, o_ref):
        s = pl.program_id(2)
        keep = s > 0
        vall = jnp.concatenate([_masked(cp_ref, keep) * _masked(hp_ref, keep), c_ref[...] * h_ref[...]], axis=0)
        o_ref[...] = (b_ref[...] * _conv(vall, w_ref, K)[HALO:]).astype(o_ref.dtype)

    tb, _, _ = _halo_specs(ts, W, nS, lambda c: ob + c)
    tc, pc, _ = _halo_specs(ts, W, nS, lambda c: oc + c)
    th, ph, _ = _halo_specs(ts, W, nS, lambda c: oh + c)
    return pl.pallas_call(
        body, name=name, grid=(Bl, nW, nS),
        in_specs=[tb, tc, pc, th, ph, pl.BlockSpec((8, W), lambda b_, c, s: (0, c))],
        out_specs=pl.BlockSpec((None, ts, W), lambda b_, c, s: (b_, s, c)),
        out_shape=jax.ShapeDtypeStruct((Bl, S, D), MXU_DTYPE),
        compiler_params=_cp(("parallel", "parallel", "parallel")),
    )(proj, proj, proj, proj, proj, w8)


def _sc_conv_bwd(ds, proj, w8, offs, D, name, ts=256, W=1024):
    Bl, S, _ = proj.shape
    K = 3
    ts, W = _tile(S, ts, 8), _tile(D, W)
    nS, nW = S // ts, D // W
    ob, oc, oh = [o // W for o in offs]

    def body(d_ref, dn_ref, b_ref, bn_ref, c_ref, cp_ref, cn_ref, h_ref, hp_ref, hn_ref, w_ref,
             db_ref, dc_ref, dh_ref, dw_ref):
        bb, s = pl.program_id(1), pl.program_id(2)

        @pl.when((bb == 0) & (s == 0))
        def _():
            dw_ref[...] = jnp.zeros_like(dw_ref)

        first, last = s > 0, s < nS - 1
        zeros = jnp.zeros((HALO, W), F32)
        c_t, h_t = c_ref[...], h_ref[...]
        vall = jnp.concatenate([_masked(cp_ref, first) * _masked(hp_ref, first), c_t * h_t,
                                _masked(cn_ref, last) * _masked(hn_ref, last)], axis=0)
        dcv = jnp.concatenate([zeros, d_ref[...] * b_ref[...], _masked(dn_ref, last) * _masked(bn_ref, last)], axis=0)
        cv = _conv(vall, w_ref, K)[HALO:HALO + ts]
        db_ref[...] = (d_ref[...] * cv).astype(db_ref.dtype)
        dv, gw = _conv_t(dcv, vall, w_ref, K, ts)
        dc_ref[...] = (dv * h_t).astype(dc_ref.dtype)
        dh_ref[...] = (dv * c_t).astype(dh_ref.dtype)
        for k in range(K):
            dw_ref[k:k + 1, :] += gw[k]

    per = ts // HALO

    def specs(o):
        t = pl.BlockSpec((None, ts, W), lambda c, b_, s: (b_, s, o + c))
        p = pl.BlockSpec((None, HALO, W), lambda c, b_, s: (b_, jnp.maximum(s * per - 1, 0), o + c))
        n = pl.BlockSpec((None, HALO, W), lambda c, b_, s: (b_, jnp.minimum((s + 1) * per, nS * per - 1), o + c))
        return t, p, n

    dt_, _, dn_ = specs(0)
    bt, _, bn = specs(ob)
    ct, cp, cn = specs(oc)
    ht, hp, hn = specs(oh)
    wspec = pl.BlockSpec((8, W), lambda c, b_, s: (0, c))
    act = jax.ShapeDtypeStruct((Bl, S, D), MXU_DTYPE)
    return pl.pallas_call(
        body, name=name, grid=(nW, Bl, nS),
        in_specs=[dt_, dn_, bt, bn, ct, cp, cn, ht, hp, hn, wspec],
        out_specs=[dt_, dt_, dt_, wspec],
        out_shape=[act, act, act, jax.ShapeDtypeStruct((8, D), F32)],
        compiler_params=_cp(("arbitrary", "arbitrary", "arbitrary")),
    )(ds, ds, proj, proj, proj, proj, proj, proj, proj, proj, w8)


def _ffn_conv_fwd(up, w8, b, DFF, name, ts=512, W=1408):
    Bl, S, _ = up.shape
    K = 3
    ts, W = _tile(S, ts, 8), _tile(DFF, W)
    nS, nW = S // ts, DFF // W

    def body(g_ref, gp_ref, v_ref, vp_ref, wg_ref, wv_ref, bg_ref, bv_ref, o_ref):
        keep = pl.program_id(2) > 0
        ug = _conv(jnp.concatenate([_masked(gp_ref, keep), g_ref[...]], axis=0), wg_ref, K)[HALO:] + bg_ref[...]
        uv = _conv(jnp.concatenate([_masked(vp_ref, keep), v_ref[...]], axis=0), wv_ref, K)[HALO:] + bv_ref[...]
        o_ref[...] = (ug * _sigmoid(ug) * uv).astype(o_ref.dtype)

    tg, pg, _ = _halo_specs(ts, W, nS, lambda c: c)
    tv, pv, _ = _halo_specs(ts, W, nS, lambda c: nW + c)
    wg = pl.BlockSpec((8, W), lambda b_, c, s: (0, c))
    wv = pl.BlockSpec((8, W), lambda b_, c, s: (0, nW + c))
    bg = pl.BlockSpec((1, W), lambda b_, c, s: (0, c))
    bv = pl.BlockSpec((1, W), lambda b_, c, s: (0, nW + c))
    b2 = b.reshape(1, 2 * DFF)
    return pl.pallas_call(
        body, name=name, grid=(Bl, nW, nS),
        in_specs=[tg, pg, tv, pv, wg, wv, bg, bv],
        out_specs=pl.BlockSpec((None, ts, W), lambda b_, c, s: (b_, s, c)),
        out_shape=jax.ShapeDtypeStruct((Bl, S, DFF), MXU_DTYPE),
        compiler_params=_cp(("parallel", "parallel", "parallel")),
    )(up, up, up, up, w8, w8, b2, b2)


def _ffn_conv_bwd(da, up, w8, b, DFF, name, ts=256, W=1408):
    Bl, S, _ = up.shape
    K = 3
    ts, W = _tile(S, ts, 8), _tile(DFF, W)
    nS, nW = S // ts, DFF // W

    def body(d_ref, dn_ref, g_ref, gp_ref, gn_ref, v_ref, vp_ref, vn_ref, wg_ref, wv_ref, bg_ref, bv_ref,
             dg_ref, dv_ref, dwg_ref, dwv_ref, dbg_ref, dbv_ref):
        bb, s = pl.program_id(1), pl.program_id(2)

        @pl.when((bb == 0) & (s == 0))
        def _():
            for r in (dwg_ref, dwv_ref, dbg_ref, dbv_ref):
                r[...] = jnp.zeros_like(r)

        first, last = s > 0, s < nS - 1
        gall = jnp.concatenate([_masked(gp_ref, first), g_ref[...], _masked(gn_ref, last)], axis=0)
        vall = jnp.concatenate([_masked(vp_ref, first), v_ref[...], _masked(vn_ref, last)], axis=0)
        dall = jnp.concatenate([jnp.zeros((HALO, W), F32), d_ref[...], _masked(dn_ref, last)], axis=0)
        ug = _conv(gall, wg_ref, K) + bg_ref[...]
        uv = _conv(vall, wv_ref, K) + bv_ref[...]
        sg = _sigmoid(ug)
        dug = dall * uv * (sg * (1.0 + ug * (1.0 - sg)))
        duv = dall * (ug * sg)
        dg_ref[...] = _conv_t_rows(dug, wg_ref, K)[HALO:HALO + ts].astype(dg_ref.dtype)
        dv_ref[...] = _conv_t_rows(duv, wv_ref, K)[HALO:HALO + ts].astype(dv_ref.dtype)
        dug_t, duv_t = dug[HALO:HALO + ts], duv[HALO:HALO + ts]
        dbg_ref[0:1, :] += jnp.sum(dug_t, axis=0, keepdims=True)
        dbv_ref[0:1, :] += jnp.sum(duv_t, axis=0, keepdims=True)
        _conv_wgrad(dwg_ref, dug_t, gall, K, ts)
        _conv_wgrad(dwv_ref, duv_t, vall, K, ts)

    per = ts // HALO

    def specs(o):
        t = pl.BlockSpec((None, ts, W), lambda c, b_, s: (b_, s, o + c))
        p = pl.BlockSpec((None, HALO, W), lambda c, b_, s: (b_, jnp.maximum(s * per - 1, 0), o + c))
        n = pl.BlockSpec((None, HALO, W), lambda c, b_, s: (b_, jnp.minimum((s + 1) * per, nS * per - 1), o + c))
        return t, p, n

    dt_, _, dn_ = specs(0)
    gt, gp, gn = specs(0)
    vt, vp, vn = specs(nW)
    wg = pl.BlockSpec((8, W), lambda c, b_, s: (0, c))
    wv = pl.BlockSpec((8, W), lambda c, b_, s: (0, nW + c))
    bg = pl.BlockSpec((1, W), lambda c, b_, s: (0, c))
    bv = pl.BlockSpec((1, W), lambda c, b_, s: (0, nW + c))
    b2 = b.reshape(1, 2 * DFF)
    act = jax.ShapeDtypeStruct((Bl, S, DFF), MXU_DTYPE)
    small = jax.ShapeDtypeStruct((8, DFF), F32)
    dg, dv, dwg, dwv, dbg, dbv = pl.pallas_call(
        body, name=name, grid=(nW, Bl, nS),
        in_specs=[dt_, dn_, gt, gp, gn, vt, vp, vn, wg, wv, bg, bv],
        out_specs=[dt_, dt_, wg, wg, wg, wg],
        out_shape=[act, act, small, small, small, small],
        compiler_params=_cp(("arbitrary", "arbitrary", "arbitrary")),
    )(da, da, up, up, up, up, up, up, w8, w8, b2, b2)
    return dg, dv, jnp.concatenate([dwg, dwv], axis=1), jnp.concatenate([dbg, dbv], axis=1)


def _merge_fwd(proj, y_ssd, y_sc, off, D, name, ts=512):
    Bl, S, _ = proj.shape
    ts = _tile(S, ts, 8)
    og = off // D

    def body(g1_ref, g2_ref, a_ref, b_ref, o_ref):
        o_ref[...] = (_sigmoid(g1_ref[...]) * a_ref[...] + _sigmoid(g2_ref[...]) * b_ref[...]).astype(o_ref.dtype)

    tok = pl.BlockSpec((None, ts, D), lambda b, s: (b, s, 0))
    return pl.pallas_call(
        body, name=name, grid=(Bl, S // ts),
        in_specs=[pl.BlockSpec((None, ts, D), lambda b, s: (b, s, og)),
                  pl.BlockSpec((None, ts, D), lambda b, s: (b, s, og + 1)), tok, tok],
        out_specs=tok, out_shape=jax.ShapeDtypeStruct((Bl, S, D), MXU_DTYPE),
        compiler_params=_cp(("parallel", "parallel")),
    )(proj, proj, y_ssd, y_sc)


def _merge_bwd(dmixin, proj, y_ssd, y_sc, off, D, name, ts=512):
    Bl, S, _ = proj.shape
    ts = _tile(S, ts, 8)
    og = off // D

    def body(d_ref, g1_ref, g2_ref, a_ref, b_ref, da_ref, db_ref, dg1_ref, dg2_ref):
        d = d_ref[...]
        s1, s2 = _sigmoid(g1_ref[...]), _sigmoid(g2_ref[...])
        da_ref[...] = (d * s1).astype(da_ref.dtype)
        db_ref[...] = (d * s2).astype(db_ref.dtype)
        dg1_ref[...] = (d * a_ref[...] * (s1 * (1.0 - s1))).astype(dg1_ref.dtype)
        dg2_ref[...] = (d * b_ref[...] * (s2 * (1.0 - s2))).astype(dg2_ref.dtype)

    tok = pl.BlockSpec((None, ts, D), lambda b, s: (b, s, 0))
    act = jax.ShapeDtypeStruct((Bl, S, D), MXU_DTYPE)
    return pl.pallas_call(
        body, name=name, grid=(Bl, S // ts),
        in_specs=[tok, pl.BlockSpec((None, ts, D), lambda b, s: (b, s, og)),
                  pl.BlockSpec((None, ts, D), lambda b, s: (b, s, og + 1)), tok, tok],
        out_specs=[tok, tok, tok, tok], out_shape=[act, act, act, act],
        compiler_params=_cp(("parallel", "parallel")),
    )(dmixin, proj, proj, y_ssd, y_sc)


def _two_term_dot(v, m01):
    hi = v.astype(MXU_DTYPE)
    lo = (v - hi.astype(F32)).astype(MXU_DTYPE)
    return _dg(hi, m01, NN) + _dg(lo, m01, NN)


def _head_to_channels(R):
    rp = R * HEAD_DIM
    return (lax.shift_right_logical(lax.broadcasted_iota(jnp.int32, (LANES, rp), 1), 6)
            == lax.broadcasted_iota(jnp.int32, (LANES, rp), 0)).astype(MXU_DTYPE)


def _expand_heads(mats, R):
    L = mats[0].shape[0]
    out = _two_term_dot(jnp.concatenate(mats, axis=0), _head_to_channels(R))
    return [out[i * L:(i + 1) * L, :] for i in range(len(mats))]


def _expand_cols(mat, R, lane):
    half = lane < HEAD_DIM
    return jnp.concatenate(
        [jnp.where(half, mat[:, 2 * q:2 * q + 1], mat[:, 2 * q + 1:2 * q + 2]) for q in range(R // 2)], axis=1)


def _row_sums(v):
    return _two_term_dot(v, jnp.ones((v.shape[1], LANES), MXU_DTYPE))


def _lanes_to(v, width):
    return jnp.concatenate([v] * (width // LANES), axis=1)


def _head_rows(colvec, R, N):
    return jnp.concatenate([jnp.broadcast_to(colvec[r:r + 1, :], (HEAD_DIM, N)) for r in range(R)], axis=0)


def _ssd_prep(proj, dt_bias, a_log, selcat, selbig, dm, name):
    Bl, S, L, G = dm.Bl, dm.S, CHUNK, N_GROUPS
    nc = S // L
    odt = dm.ODT // LANES

    def body(dtr_ref, dtb_ref, alog_ref, selcat_ref, selbig_ref, dtg_ref, acs_ref, acst_ref):
        row = lax.broadcasted_iota(jnp.int32, (L, L), 0)
        col = lax.broadcasted_iota(jnp.int32, (L, L), 1)
        dt_all = _softplus(dtr_ref[...] + dtb_ref[...])
        acs_all = _dg((row >= col).astype(F32), dt_all * (-jnp.exp(alog_ref[...])), NN, HI)
        dtg_ref[...] = _dg(dt_all, selcat_ref[...], NN, HI)
        acs_ref[...] = _dg(acs_all, selcat_ref[...], NN, HI)
        acst_ref[...] = _dg(selbig_ref[...], acs_all, NT, HI)

    vec = pl.BlockSpec((1, LANES), lambda b, c: (0, 0))
    wide = pl.BlockSpec((None, L, G * LANES), lambda b, c: (b, c, 0))
    return pl.pallas_call(
        body, name=name, grid=(Bl, nc),
        in_specs=[pl.BlockSpec((None, L, LANES), lambda b, c: (b, c, odt)), vec, vec,
                  pl.BlockSpec((LANES, G * LANES), lambda b, c: (0, 0)),
                  pl.BlockSpec((G * 8, LANES), lambda b, c: (0, 0))],
        out_specs=[wide, wide, pl.BlockSpec((None, None, G * 8, L), lambda b, c: (b, c, 0, 0))],
        out_shape=[jax.ShapeDtypeStruct((Bl, S, G * LANES), F32), jax.ShapeDtypeStruct((Bl, S, G * LANES), F32),
                   jax.ShapeDtypeStruct((Bl, nc, G * 8, L), F32)],
        compiler_params=_cp(("parallel", "parallel")),
    )(proj, dt_bias, a_log, selcat, selbig)


def _ssd_post(d_a, ddt, dtg, proj, dt_bias, a_log, selcat, dm, name):
    Bl, S, L, G = dm.Bl, dm.S, CHUNK, N_GROUPS
    nc = S // L
    odt = dm.ODT // LANES

    def body(da_ref, ddt_ref, dtg_ref, dtr_ref, dtb_ref, alog_ref, selcat_ref, ddtr_ref, dpar_ref):
        @pl.when((pl.program_id(0) == 0) & (pl.program_id(1) == 0))
        def _():
            dpar_ref[...] = jnp.zeros_like(dpar_ref)

        row = lax.broadcasted_iota(jnp.int32, (L, L), 0)
        col = lax.broadcasted_iota(jnp.int32, (L, L), 1)
        selcat = selcat_ref[...]
        a_all = -jnp.exp(alog_ref[...])
        a4 = _dg(jnp.broadcast_to(a_all, (8, LANES)), selcat, NN, HI)[0:1, :]
        dadt = _dg((col >= row).astype(F32), da_ref[...], NN, HI)
        ddt4 = ddt_ref[...] + dadt * a4
        da4 = jnp.sum(dadt * dtg_ref[...], axis=0, keepdims=True)
        ddt_all = _dg(ddt4, selcat, NT, HI)
        da_all = _dg(jnp.broadcast_to(da4, (8, G * LANES)), selcat, NT, HI)[0:1, :]
        ddtr = ddt_all * _sigmoid(dtr_ref[...] + dtb_ref[...])
        ddtr_ref[...] = ddtr.astype(ddtr_ref.dtype)
        dpar_ref[0:1, :] += jnp.sum(ddtr, axis=0, keepdims=True)
        dpar_ref[1:2, :] += da_all * a_all

    vec = pl.BlockSpec((1, LANES), lambda b, c: (0, 0))
    wide = pl.BlockSpec((None, L, G * LANES), lambda b, c: (b, c, 0))
    return pl.pallas_call(
        body, name=name, grid=(Bl, nc),
        in_specs=[wide, wide, wide, pl.BlockSpec((None, L, LANES), lambda b, c: (b, c, odt)), vec, vec,
                  pl.BlockSpec((LANES, G * LANES), lambda b, c: (0, 0))],
        out_specs=[pl.BlockSpec((None, L, LANES), lambda b, c: (b, c, 0)), pl.BlockSpec((8, LANES), lambda b, c: (0, 0))],
        out_shape=[jax.ShapeDtypeStruct((Bl, S, LANES), MXU_DTYPE), jax.ShapeDtypeStruct((8, LANES), F32)],
        compiler_params=_cp(("arbitrary", "arbitrary")),
    )(d_a, ddt, dtg, proj, dt_bias, a_log, selcat)


def _scan_fwd(xbc_a, proj, dtg4, acs4, acs_t4, dsk_exp, norm_g, dm, name):
    Bl, S, DI, N, R, L, G = dm.Bl, dm.S, dm.DI, dm.N, dm.R, CHUNK, N_GROUPS
    RP = R * HEAD_DIM
    nc = S // L
    ob, ocm = DI // N, DI // N + G

    def body(xs_ref, bm_ref, cm_ref, z_ref, dtg_ref, acs_ref, acst_ref, dsk_ref, ng_ref, y_ref, yn_ref, hp_ref, h_ref):
        @pl.when(pl.program_id(2) == 0)
        def _():
            h_ref[...] = jnp.zeros_like(h_ref)

        causal = lax.broadcasted_iota(jnp.int32, (L, L), 0) >= lax.broadcasted_iota(jnp.int32, (L, L), 1)
        lane = lax.broadcasted_iota(jnp.int32, (L, LANES), 1)
        dtg, acs, acs_t = dtg_ref[...], acs_ref[...], acst_ref[...]
        xs = xs_ref[...]
        bmb, cmb = bm_ref[...].astype(MXU_DTYPE), cm_ref[...].astype(MXU_DTYPE)
        sg = _dg(cmb, bmb, NT)
        acs_last = acs[L - 1:L, :]
        dt_exp, ea_exp, ds_exp = _expand_heads([dtg, jnp.exp(acs), jnp.exp(acs_last - acs)], R)
        xdt = xs * dt_exp
        xb = xdt.astype(MXU_DTYPE)
        parts = []
        for q in range(R // 2):
            x2 = xb[:, LANES * q:LANES * (q + 1)]
            ys = []
            for r in (2 * q, 2 * q + 1):
                dec = jnp.exp(jnp.where(causal, acs[:, r:r + 1] - acs_t[r:r + 1, :], -1e30))
                ys.append(_dg((sg * dec).astype(MXU_DTYPE), x2, NN))
            parts.append(jnp.where(lane < HEAD_DIM, ys[0], ys[1]))
        ydiag = jnp.concatenate(parts, axis=1)
        h_cur = h_ref[...]
        hb = h_cur.astype(MXU_DTYPE)
        yoff = _dg(cmb, hb, NT) * ea_exp
        st = _dg((xdt * ds_exp).astype(MXU_DTYPE), bmb, TN)
        hp_ref[...] = hb
        h_ref[...] = h_cur * _head_rows(jnp.exp(acs_t[:, L - 1:L]), R, N) + st
        y = ydiag + yoff + dsk_ref[...] * xs
        y_ref[...] = y
        z = z_ref[...]
        yg = y * (z * _sigmoid(z))
        rr = lax.rsqrt(_row_sums(yg * yg) * (1.0 / RP) + EPS)
        yn_ref[...] = (yg * _lanes_to(rr, RP) * ng_ref[...]).astype(yn_ref.dtype)

    grp = pl.BlockSpec((None, L, RP), lambda b, g, c: (b, c, g))
    lanes = pl.BlockSpec((None, L, LANES), lambda b, g, c: (b, c, g))
    chan = pl.BlockSpec((1, RP), lambda b, g, c: (0, g))
    return pl.pallas_call(
        body, name=name, grid=(Bl, G, nc),
        in_specs=[grp,
                  pl.BlockSpec((None, L, N), lambda b, g, c: (b, c, ob + g)),
                  pl.BlockSpec((None, L, N), lambda b, g, c: (b, c, ocm + g)),
                  grp, lanes, lanes,
                  pl.BlockSpec((None, None, 8, L), lambda b, g, c: (b, c, g, 0)),
                  chan, chan],
        out_specs=[grp, grp, pl.BlockSpec((None, None, None, RP, N), lambda b, g, c: (b, g, c, 0, 0))],
        out_shape=[jax.ShapeDtypeStruct((Bl, S, DI), F32), jax.ShapeDtypeStruct((Bl, S, DI), MXU_DTYPE),
                   jax.ShapeDtypeStruct((Bl, G, nc, RP, N), MXU_DTYPE)],
        scratch_shapes=[pltpu.VMEM((RP, N), F32)],
        compiler_params=_cp(("parallel", "parallel", "arbitrary")),
    )(xbc_a, xbc_a, xbc_a, proj, dtg4, acs4, acs_t4, dsk_exp, norm_g.reshape(1, DI))


def _scan_bwd(dyn, y, xbc_a, proj, hprev, dtg4, acs4, acs_t4, dsk_exp, norm_g, dm, name):
    Bl, S, DI, N, R, L, G = dm.Bl, dm.S, dm.DI, dm.N, dm.R, CHUNK, N_GROUPS
    RP = R * HEAD_DIM
    nc = S // L
    ob, ocm = DI // N, DI // N + G

    def body(dyn_ref, y_ref, z_ref, xs_ref, bm_ref, cm_ref, hp_ref, dtg_ref, acs_ref, acst_ref, dsk_ref, ng_ref,
             dz_ref, dxs_ref, dbm_ref, dcm_ref, da_ref, ddt_ref, ddsk_ref, dng_ref, dh_ref):
        b, c = pl.program_id(1), pl.program_id(2)

        @pl.when(c == 0)
        def _():
            dh_ref[...] = jnp.zeros_like(dh_ref)

        @pl.when((b == 0) & (c == 0))
        def _():
            ddsk_ref[...] = jnp.zeros_like(ddsk_ref)
            dng_ref[...] = jnp.zeros_like(dng_ref)

        row = lax.broadcasted_iota(jnp.int32, (L, L), 0)
        col = lax.broadcasted_iota(jnp.int32, (L, L), 1)
        causal, anti = row >= col, col >= row
        lane = lax.broadcasted_iota(jnp.int32, (L, LANES), 1)
        etb = (lax.shift_right_logical(lax.broadcasted_iota(jnp.int32, (RP, LANES), 0), 6)
               == lax.broadcasted_iota(jnp.int32, (RP, LANES), 1)).astype(MXU_DTYPE)

        dtg, acs, acs_t = dtg_ref[...], acs_ref[...], acst_ref[...]
        xs, z, y, dyn = xs_ref[...], z_ref[...], y_ref[...], dyn_ref[...]
        bmb, cmb = bm_ref[...].astype(MXU_DTYPE), cm_ref[...].astype(MXU_DTYPE)
        hpb = hp_ref[...]
        ng = ng_ref[...]

        sz = _sigmoid(z)
        siluz = z * sz
        yg = y * siluz
        rr = lax.rsqrt(jnp.mean(yg * yg, axis=-1, keepdims=True) + EPS)
        yhat = yg * rr
        dng_ref[...] += jnp.sum(dyn * yhat, axis=0, keepdims=True)
        dyhat = dyn * ng
        dyg = rr * (dyhat - yhat * jnp.mean(dyhat * yhat, axis=-1, keepdims=True))
        dy = dyg * siluz
        dz_ref[...] = (dyg * y * (sz * (1.0 + z * (1.0 - sz)))).astype(dz_ref.dtype)

        dxs = dy * dsk_ref[...]
        ddsk_ref[...] += jnp.sum(dy * xs, axis=0, keepdims=True)

        acs_last = acs[L - 1:L, :]
        dt_exp = _expand_cols(dtg, R, lane)
        ea_exp = _expand_cols(jnp.exp(acs), R, lane)
        ds_exp = _expand_cols(jnp.exp(acs_last - acs), R, lane)
        xdt = xs * dt_exp
        xb = xdt.astype(MXU_DTYPE)
        dyb = dy.astype(MXU_DTYPE)
        cd = jnp.exp(acs_last)
        cd_rows = _head_rows(jnp.exp(acs_t[:, L - 1:L]), R, N)

        q_ = _dg(cmb, hpb, NT)
        dq = dy * ea_exp
        dqb = dq.astype(MXU_DTYPE)
        dcm = _dg(dqb, hpb, NN)
        dh_yoff = _dg(dqb, cmb, TN)

        dhn = dh_ref[...]
        wprod = dhn * hpb.astype(F32)
        per_head = jnp.concatenate(
            [jnp.sum(wprod[HEAD_DIM * r:HEAD_DIM * (r + 1), :], axis=0, keepdims=True) for r in range(R)]
            + ([jnp.zeros((8 - R, N), F32)] if R < 8 else []), axis=0)
        dcd_col = jnp.sum(per_head, axis=1, keepdims=True)
        diag8 = lax.broadcasted_iota(jnp.int32, (8, LANES), 0) == lax.broadcasted_iota(jnp.int32, (8, LANES), 1)
        dcd_lane = jnp.sum(jnp.where(diag8, dcd_col, 0.0), axis=0, keepdims=True)
        d_a_last = dcd_lane * cd
        dh_ref[...] = dhn * cd_rows + dh_yoff
        dhnb = dhn.astype(MXU_DTYPE)

        e_ = _dg(bmb, dhnb, NT)
        dxdt = ds_exp * e_
        xds = xdt * ds_exp
        dbm = _dg(xds.astype(MXU_DTYPE), dhnb, NN)

        sg = _dg(cmb, bmb, NT)
        sg_t = _dg(bmb, cmb, NT)
        dsg = jnp.zeros((L, L), F32)
        dsg_t = jnp.zeros((L, L), F32)
        d_a = jnp.zeros((L, LANES), F32)
        parts = []
        for q in range(R // 2):
            x2 = xb[:, LANES * q:LANES * (q + 1)]
            dy2 = dyb[:, LANES * q:LANES * (q + 1)]
            dxs2 = []
            for hh, r in enumerate((2 * q, 2 * q + 1)):
                mine = (lane < HEAD_DIM) if hh == 0 else (lane >= HEAD_DIM)
                diff = acs[:, r:r + 1] - acs_t[r:r + 1, :]
                dec = jnp.exp(jnp.where(causal, diff, -1e30))
                dec_t = jnp.exp(jnp.where(anti, -diff, -1e30))
                dy2m = jnp.where(mine, dy2, jnp.zeros_like(dy2))
                dm_ = _dg(dy2m, x2, NT)
                dm_t = _dg(x2, dy2m, NT)
                m_t = sg_t * dec_t
                da_col = jnp.sum(dm_ * (sg * dec) - dm_t * m_t, axis=1, keepdims=True)
                d_a = d_a + jnp.where(lane == r, da_col, 0.0)
                dsg = dsg + dm_ * dec
                dsg_t = dsg_t + dm_t * dec_t
                dxs2.append(_dg(m_t.astype(MXU_DTYPE), dy2, NN))
            parts.append(jnp.where(lane < HEAD_DIM, dxs2[0], dxs2[1]))
        dxdt = dxdt + jnp.concatenate(parts, axis=1)
        dcm_ref[...] = dcm + _dg(dsg.astype(MXU_DTYPE), bmb, NN)
        dbm_ref[...] = dbm + _dg(dsg_t.astype(MXU_DTYPE), cmb, NN)
        dxs_ref[...] = dxs + dxdt * dt_exp

        hs = _two_term_dot(jnp.concatenate([dq * q_ - xds * e_, xds * e_, dxdt * xs], axis=0), etb)
        t2 = hs[L:2 * L, :]
        rowl = lax.broadcasted_iota(jnp.int32, (L, LANES), 0)
        d_a_last = d_a_last + jnp.sum(t2, axis=0, keepdims=True)
        da_ref[...] = d_a + hs[0:L, :] + jnp.where(rowl == L - 1, d_a_last, 0.0)
        ddt_ref[...] = hs[2 * L:3 * L, :]

    grp = pl.BlockSpec((None, L, RP), lambda g, b, c: (b, nc - 1 - c, g))
    lanes = pl.BlockSpec((None, L, LANES), lambda g, b, c: (b, nc - 1 - c, g))
    bspec = pl.BlockSpec((None, L, N), lambda g, b, c: (b, nc - 1 - c, ob + g))
    cspec = pl.BlockSpec((None, L, N), lambda g, b, c: (b, nc - 1 - c, ocm + g))
    gn = pl.BlockSpec((None, L, N), lambda g, b, c: (b, nc - 1 - c, g))
    chan = pl.BlockSpec((1, RP), lambda g, b, c: (0, g))
    wide = jax.ShapeDtypeStruct((Bl, S, G * LANES), F32)
    return pl.pallas_call(
        body, name=name, grid=(G, Bl, nc),
        in_specs=[grp, grp, grp, grp, bspec, cspec,
                  pl.BlockSpec((None, None, None, RP, N), lambda g, b, c: (b, g, nc - 1 - c, 0, 0)),
                  lanes, lanes, pl.BlockSpec((None, None, 8, L), lambda g, b, c: (b, nc - 1 - c, g, 0)),
                  chan, chan],
        out_specs=[grp, grp, gn, gn, lanes, lanes, chan, chan],
        out_shape=[jax.ShapeDtypeStruct((Bl, S, DI), MXU_DTYPE), jax.ShapeDtypeStruct((Bl, S, DI), F32),
                   jax.ShapeDtypeStruct((Bl, S, G * N), F32), jax.ShapeDtypeStruct((Bl, S, G * N), F32),
                   wide, wide, jax.ShapeDtypeStruct((1, DI), F32), jax.ShapeDtypeStruct((1, DI), F32)],
        scratch_shapes=[pltpu.VMEM((RP, N), F32)],
        compiler_params=_cp(("arbitrary", "arbitrary", "arbitrary")),
    )(dyn, y, proj, xbc_a, xbc_a, xbc_a, hprev, dtg4, acs4, acs_t4, dsk_exp, norm_g.reshape(1, DI))


def _adam_update(w, m, v, g):
    c1 = 1.0 - ADAM_B1 ** ADAM_STEP
    c2 = 1.0 - ADAM_B2 ** ADAM_STEP
    nm = ADAM_B1 * m + (1.0 - ADAM_B1) * g
    nv = ADAM_B2 * v + (1.0 - ADAM_B2) * (g * g)
    return nm, nv, -ADAM_LR * ((nm / c1) / (jnp.sqrt(nv / c2) + ADAM_EPS) + ADAM_WD * w)


def _adamw(w, m, v, g, name, tr=128):
    rows, cols = w.shape
    tr = _tile(rows, tr, 8)

    def body(w_ref, m_ref, v_ref, ga_ref, g_ref, d_ref, nm_ref, nv_ref):
        g = ga_ref[...]
        g_ref[...] = g
        nm_ref[...], nv_ref[...], d_ref[...] = _adam_update(w_ref[...], m_ref[...], v_ref[...], g)

    blk = pl.BlockSpec((tr, cols), lambda i: (i, 0))
    shp = jax.ShapeDtypeStruct((rows, cols), F32)
    return pl.pallas_call(
        body, name=name, grid=(rows // tr,), in_specs=[blk] * 4, out_specs=[blk] * 4,
        out_shape=[shp] * 4, compiler_params=_cp(("parallel",)),
    )(w, m, v, g)


def _adamw_layers(w, m, v, g_mine, g_theirs, core, name, tr=128):
    _, rows, cols = w.shape
    tr = _tile(rows, tr, 8)

    def body(c_ref, w_ref, m_ref, v_ref, ga_ref, gb_ref, g_ref, d_ref, nm_ref, nv_ref):
        g = jnp.where(pl.program_id(0) == c_ref[0], ga_ref[...], gb_ref[...])
        g_ref[...] = g
        nm_ref[...], nv_ref[...], d_ref[...] = _adam_update(w_ref[...], m_ref[...], v_ref[...], g)

    lay = pl.BlockSpec((None, tr, cols), lambda l, i, c_ref: (l, i, 0))
    one = pl.BlockSpec((tr, cols), lambda l, i, c_ref: (i, 0))
    shp = jax.ShapeDtypeStruct(w.shape, F32)
    return pl.pallas_call(
        body, name=name,
        grid_spec=pltpu.PrefetchScalarGridSpec(num_scalar_prefetch=1, grid=(2, rows // tr),
                                               in_specs=[lay, lay, lay, one, one], out_specs=[lay] * 4),
        out_shape=[shp] * 4, compiler_params=_cp(("parallel", "parallel")),
    )(core, w, m, v, g_mine, g_theirs)


def _sum_slots(buf, name, tr=256):
    n, rows, cols = buf.shape
    tr = _tile(rows, tr, 8)

    def body(b_ref, o_ref):
        acc = b_ref[0].astype(F32)
        for k in range(1, n):
            acc = acc + b_ref[k].astype(F32)
        o_ref[...] = acc

    return pl.pallas_call(
        body, name=name, grid=(rows // tr,),
        in_specs=[pl.BlockSpec((n, tr, cols), lambda i: (0, i, 0))],
        out_specs=pl.BlockSpec((tr, cols), lambda i: (i, 0)),
        out_shape=jax.ShapeDtypeStruct((rows, cols), F32), compiler_params=_cp(("parallel",)),
    )(buf)


_ANY = pl.BlockSpec(memory_space=pl.ANY)


def _exchange_chips(src, per_dest, name):
    rows, cols = src.shape[-2:]

    def body(in_ref, out_ref, send_sems, recv_sems, local_sem):
        x, y, c = lax.axis_index("x"), lax.axis_index("y"), lax.axis_index("c")
        me = 2 * x + y
        chips = [(1 - x, y), (x, 1 - y), (1 - x, 1 - y)]

        def block(j):
            return in_ref.at[j] if per_dest else in_ref

        mine = pltpu.make_async_copy(block(me), out_ref.at[me], local_sem)
        mine.start()
        sends = []
        for k, (px, py) in enumerate(chips):
            cp = pltpu.make_async_remote_copy(
                src_ref=block(2 * px + py), dst_ref=out_ref.at[me], send_sem=send_sems.at[k],
                recv_sem=recv_sems.at[k], device_id=(px, py, c), device_id_type=MESH)
            cp.start()
            sends.append(cp)
        for k, (px, py) in enumerate(chips):
            pltpu.make_async_remote_copy(
                src_ref=block(me), dst_ref=out_ref.at[2 * px + py], send_sem=send_sems.at[k],
                recv_sem=recv_sems.at[k], device_id=(px, py, c), device_id_type=MESH).wait_recv()
        for cp in sends:
            cp.wait_send()
        mine.wait()

    return pl.pallas_call(
        body, name=name, in_specs=[_ANY], out_specs=_ANY,
        out_shape=jax.ShapeDtypeStruct((N_CHIPS, rows, cols), src.dtype),
        scratch_shapes=[pltpu.SemaphoreType.DMA((3,)), pltpu.SemaphoreType.DMA((3,)), pltpu.SemaphoreType.DMA(())],
    )(src)


def _shard_window(ref, kind, j, lead=()):
    if kind == "slots":
        return ref.at[(j,) + lead]
    r, c = ref.shape[-2] // (N_CHIPS if kind == "rows" else 1), ref.shape[-1] // (N_CHIPS if kind == "cols" else 1)
    full = tuple(slice(None) for _ in range(len(ref.shape) - 2 - len(lead)))
    if kind == "rows":
        return ref.at[lead + full + (pl.ds(pl.multiple_of(j * r, 16), r), slice(None))]
    return ref.at[lead + full + (slice(None), pl.ds(pl.multiple_of(j * c, LANES), c))]


def _gather_kind(shard, axis):
    if axis == 1:
        return "rows"
    return "cols" if shard.shape[2] % LANES == 0 else "slots"


def _gather_weights(shards, axes, name):
    kinds = ["rows" if ax == 1 else "slots" for ax in axes]
    nw = len(shards)

    def out_shape(s, kind):
        d, r, c = s.shape
        shp = {"rows": (d, N_CHIPS * r, c), "cols": (d, r, N_CHIPS * c), "slots": (N_CHIPS, d, r, c)}[kind]
        return jax.ShapeDtypeStruct(shp, s.dtype)

    assert all(s.shape[0] == 2 for s in shards)

    def body(*refs):
        ins, outs = refs[:nw], refs[nw:2 * nw]
        ici_send, ici_recv, d2d_send, d2d_recv, local_sems = refs[2 * nw:]
        x, y, c = lax.axis_index("x"), lax.axis_index("y"), lax.axis_index("c")
        me = 2 * x + y
        chips = [(1 - x, y), (x, 1 - y), (1 - x, 1 - y)]
        sends = []
        for i in range(nw):
            if kinds[i] == "rows":
                own = pltpu.make_async_copy(ins[i], _shard_window(outs[i], kinds[i], me), local_sems.at[i])
                own.start()
                sends.append((own, False))
        for i in range(nw):
            for k, (px, py) in enumerate(chips):
                cp = pltpu.make_async_remote_copy(
                    src_ref=ins[i].at[c], dst_ref=_shard_window(outs[i], kinds[i], me, (c,)),
                    send_sem=ici_send.at[3 * i + k], recv_sem=ici_recv.at[3 * i + k],
                    device_id=(px, py, c), device_id_type=MESH)
                cp.start()
                sends.append((cp, True))
        for i in range(nw):
            for k, (px, py) in enumerate(chips):
                win = _shard_window(outs[i], kinds[i], 2 * px + py, (c,))
                pltpu.make_async_remote_copy(
                    src_ref=ins[i].at[c], dst_ref=win, send_sem=ici_send.at[3 * i + k], recv_sem=ici_recv.at[3 * i + k],
                    device_id=(px, py, c), device_id_type=MESH).wait_recv()
                fw = pltpu.make_async_remote_copy(
                    src_ref=win, dst_ref=win, send_sem=d2d_send.at[3 * i + k], recv_sem=d2d_recv.at[3 * i + k],
                    device_id=(x, y, 1 - c), device_id_type=MESH)
                fw.start()
                sends.append((fw, True))
        for i in range(nw):
            for k, (px, py) in enumerate(chips):
                win = _shard_window(outs[i], kinds[i], 2 * px + py, (1 - c,))
                pltpu.make_async_remote_copy(
                    src_ref=win, dst_ref=win, send_sem=d2d_send.at[3 * i + k], recv_sem=d2d_recv.at[3 * i + k],
                    device_id=(x, y, 1 - c), device_id_type=MESH).wait_recv()
        for cp, remote in sends:
            cp.wait_send() if remote else cp.wait()

    outs = pl.pallas_call(
        body, name=name, in_specs=[_ANY] * nw, out_specs=[_ANY] * nw,
        out_shape=[out_shape(s, k) for s, k in zip(shards, kinds)],
        scratch_shapes=[pltpu.SemaphoreType.DMA((3 * nw,)), pltpu.SemaphoreType.DMA((3 * nw,)),
                        pltpu.SemaphoreType.DMA((3 * nw,)), pltpu.SemaphoreType.DMA((3 * nw,)),
                        pltpu.SemaphoreType.DMA((nw,))],
    )(*shards)
    me = 2 * lax.axis_index("x") + lax.axis_index("y")
    return [jnp.concatenate([jnp.where(me == j, s, o[j]) for j in range(N_CHIPS)], axis=2) if k == "slots" else o
            for s, o, k in zip(shards, outs, kinds)]


def _swap_other_layer(gst, name):
    nv = len(gst)

    def body(*refs):
        ins, outs, send_sems, recv_sems = refs[:nv], refs[nv:2 * nv], refs[2 * nv], refs[2 * nv + 1]
        x, y, c = lax.axis_index("x"), lax.axis_index("y"), lax.axis_index("c")
        cps = [pltpu.make_async_remote_copy(src_ref=ins[i].at[1 - c], dst_ref=outs[i], send_sem=send_sems.at[i],
                                            recv_sem=recv_sems.at[i], device_id=(x, y, 1 - c), device_id_type=MESH)
               for i in range(nv)]
        for cp in cps:
            cp.start()
        for cp in cps:
            cp.wait()

    return pl.pallas_call(
        body, name=name, in_specs=[_ANY] * nv, out_specs=[_ANY] * nv,
        out_shape=[jax.ShapeDtypeStruct(v.shape[1:], v.dtype) for v in gst],
        scratch_shapes=[pltpu.SemaphoreType.DMA((nv,)), pltpu.SemaphoreType.DMA((nv,))],
    )(*gst)


def _pair_sum(g, other, core, name, tr=256):
    _, rows, cols = g.shape
    tr = _tile(rows, tr, 16)

    def body(c_ref, g_ref, o_ref, s_ref):
        s_ref[...] = (g_ref[...].astype(F32) + o_ref[...].astype(F32)).astype(s_ref.dtype)

    blk = pl.BlockSpec((tr, cols), lambda i, c_ref: (i, 0))
    return pl.pallas_call(
        body, name=name,
        grid_spec=pltpu.PrefetchScalarGridSpec(
            num_scalar_prefetch=1, grid=(rows // tr,),
            in_specs=[pl.BlockSpec((None, tr, cols), lambda i, c_ref: (c_ref[0], i, 0)), blk], out_specs=blk),
        out_shape=jax.ShapeDtypeStruct((rows, cols), g.dtype), compiler_params=_cp(("parallel",)),
    )(core, g, other)


def _scatter_layer(parts, kinds, name):
    nw = len(parts)

    def shard_shape(g, kind):
        return g.shape[-2] // (N_CHIPS if kind == "rows" else 1), g.shape[-1] // (N_CHIPS if kind == "cols" else 1)

    def body(*refs):
        ins, outs = refs[:nw], refs[nw:2 * nw]
        send_sems, recv_sems, local_sems = refs[2 * nw:]
        x, y, c = lax.axis_index("x"), lax.axis_index("y"), lax.axis_index("c")
        me = 2 * x + y
        chips = [(1 - x, y), (x, 1 - y), (1 - x, 1 - y)]
        sends = []
        for i in range(nw):
            own = pltpu.make_async_copy(_shard_window(ins[i], kinds[i], me), outs[i].at[me], local_sems.at[i])
            own.start()
            sends.append((own, False))
            for k, (px, py) in enumerate(chips):
                cp = pltpu.make_async_remote_copy(
                    src_ref=_shard_window(ins[i], kinds[i], 2 * px + py), dst_ref=outs[i].at[me],
                    send_sem=send_sems.at[3 * i + k], recv_sem=recv_sems.at[3 * i + k],
                    device_id=(px, py, c), device_id_type=MESH)
                cp.start()
                sends.append((cp, True))
        for i in range(nw):
            for k, (px, py) in enumerate(chips):
                pltpu.make_async_remote_copy(
                    src_ref=_shard_window(ins[i], kinds[i], me), dst_ref=outs[i].at[2 * px + py],
                    send_sem=send_sems.at[3 * i + k], recv_sem=recv_sems.at[3 * i + k],
                    device_id=(px, py, c), device_id_type=MESH).wait_recv()
        for cp, remote in sends:
            cp.wait_send() if remote else cp.wait()

    return pl.pallas_call(
        body, name=name, in_specs=[_ANY] * nw, out_specs=[_ANY] * nw,
        out_shape=[jax.ShapeDtypeStruct((N_CHIPS,) + shard_shape(g, k), g.dtype) for g, k in zip(parts, kinds)],
        scratch_shapes=[pltpu.SemaphoreType.DMA((3 * nw,)), pltpu.SemaphoreType.DMA((3 * nw,)),
                        pltpu.SemaphoreType.DMA((nw,))],
    )(*parts)


def _sibling_swap(vs, name):
    nv = len(vs)

    def body(*refs):
        ins, outs, send_sems, recv_sems = refs[:nv], refs[nv:2 * nv], refs[2 * nv], refs[2 * nv + 1]
        x, y, c = lax.axis_index("x"), lax.axis_index("y"), lax.axis_index("c")
        cps = [pltpu.make_async_remote_copy(src_ref=ins[i], dst_ref=outs[i], send_sem=send_sems.at[i],
                                            recv_sem=recv_sems.at[i], device_id=(x, y, 1 - c), device_id_type=MESH)
               for i in range(nv)]
        for cp in cps:
            cp.start()
        for cp in cps:
            cp.wait()

    return pl.pallas_call(
        body, name=name, in_specs=[_ANY] * nv, out_specs=[_ANY] * nv,
        out_shape=[jax.ShapeDtypeStruct(v.shape, v.dtype) for v in vs],
        scratch_shapes=[pltpu.SemaphoreType.DMA((nv,)), pltpu.SemaphoreType.DMA((nv,))],
    )(*vs)


def _allgather_all(v, name):
    rows, cols = v.shape

    def body(in_ref, out_ref, send_sems, recv_sems, local_sem):
        x, y, c = lax.axis_index("x"), lax.axis_index("y"), lax.axis_index("c")
        me = 4 * x + 2 * y + c
        peers = []
        for k in range(1, N_DEV):
            peers.append(((1 - x) if k & 4 else x, (1 - y) if k & 2 else y, (1 - c) if k & 1 else c))
        mine = pltpu.make_async_copy(in_ref, out_ref.at[me], local_sem)
        mine.start()
        sends = []
        for k, peer in enumerate(peers):
            cp = pltpu.make_async_remote_copy(src_ref=in_ref, dst_ref=out_ref.at[me], send_sem=send_sems.at[k],
                                              recv_sem=recv_sems.at[k], device_id=peer, device_id_type=MESH)
            cp.start()
            sends.append(cp)
        for k, (px, py, pc) in enumerate(peers):
            pltpu.make_async_remote_copy(src_ref=in_ref, dst_ref=out_ref.at[4 * px + 2 * py + pc],
                                         send_sem=send_sems.at[k], recv_sem=recv_sems.at[k],
                                         device_id=(px, py, pc), device_id_type=MESH).wait_recv()
        for cp in sends:
            cp.wait_send()
        mine.wait()

    return pl.pallas_call(
        body, name=name, in_specs=[_ANY], out_specs=_ANY,
        out_shape=jax.ShapeDtypeStruct((N_DEV, rows, cols), v.dtype),
        scratch_shapes=[pltpu.SemaphoreType.DMA((N_DEV - 1,)), pltpu.SemaphoreType.DMA((N_DEV - 1,)),
                        pltpu.SemaphoreType.DMA(())],
    )(v)


def _pack(arrs, dtype, width, row_mult):
    flat = jnp.concatenate([a.reshape(-1).astype(dtype) for a in arrs])
    unit = width * row_mult
    total = -(-flat.shape[0] // unit) * unit
    return jnp.pad(flat, (0, total - flat.shape[0])).reshape(-1, width)


def _unpack(buf, shapes):
    flat = buf.reshape(-1)
    out, off = [], 0
    for shp in shapes:
        n = 1
        for d in shp:
            n *= d
        out.append(flat[off:off + n].reshape(shp))
        off += n
    return out


class _Dims:
    pass


def _dims(x, ssd_dt_bias, ssd_norm_g, ssd_conv_b, ffn_conv_b):
    dm = _Dims()
    dm.Bl, dm.S, dm.D = x.shape
    dm.H, dm.DI, dm.CD = ssd_dt_bias.shape[-1], ssd_norm_g.shape[-1], ssd_conv_b.shape[-1]
    dm.N = (dm.CD - dm.DI) // (2 * N_GROUPS)
    dm.R = dm.H // N_GROUPS
    dm.DFF = ffn_conv_b.shape[-1] // 2
    dm.OX = dm.DI
    dm.OB = dm.DI + dm.CD
    dm.OC = dm.OB + dm.D
    dm.OH = dm.OC + dm.D
    dm.OG = dm.OH + dm.D
    dm.ODT = dm.OG + 2 * dm.D
    dm.NP = dm.ODT + LANES
    assert dm.DI // dm.H == HEAD_DIM and dm.N == LANES and dm.R % 2 == 0 and dm.S % CHUNK == 0 and dm.H <= LANES
    return dm


def _permute_w_in(w, dm):
    o = dm.DI + dm.CD
    pad = jnp.zeros((w.shape[0], LANES - dm.H), w.dtype)
    return jnp.concatenate([w[:, :o], w[:, o + dm.H:], w[:, o:o + dm.H], pad], axis=1)


def _unpermute_w_in(dw, dm):
    o = dm.DI + dm.CD
    return jnp.concatenate([dw[..., :o], dw[..., dm.ODT:dm.ODT + dm.H], dw[..., o:dm.ODT]], axis=-1)


def _lane_pad(v):
    return jnp.pad(v.reshape(1, -1).astype(F32), ((0, 0), (0, LANES - v.shape[-1])))


def _pad8(w):
    return jnp.pad(w.astype(F32), ((0, 8 - w.shape[0]), (0, 0)))


def _head_select(dm):
    j = jnp.arange(LANES)[None, :, None]
    r = jnp.arange(LANES)[None, None, :]
    g = jnp.arange(N_GROUPS)[:, None, None]
    sel = ((j == dm.R * g + r) & (r < dm.R)).astype(F32)
    selcat = jnp.transpose(sel, (1, 0, 2)).reshape(LANES, N_GROUPS * LANES)
    selbig = jnp.transpose(sel[:, :, :8], (0, 2, 1)).reshape(N_GROUPS * 8, LANES)
    return selcat, selbig


def _mix_fwd(dm, h, w, sp, sel, tag):
    Bl, S, D = dm.Bl, dm.S, dm.D
    T = Bl * S
    proj = _matmul(h.reshape(T, D), w["w_in_p"], "NN", F32, tag + "_in_proj", tn=1152).reshape(Bl, S, dm.NP)
    xbc_a = _ssd_conv_fwd(proj, sp["ssd_conv_w8"], sp["ssd_conv_b"], dm.OX, dm.CD, tag + "_ssd_conv")
    dtg, acs, acs_t = _ssd_prep(proj, sp["dt_bias"], sp["a_log"], sel[0], sel[1], dm, tag + "_ssd_prep")
    y, yn, hprev = _scan_fwd(xbc_a, proj, dtg, acs, acs_t, sp["dsk_exp"], sp["ssd_norm_g"], dm, tag + "_ssd_scan")
    y_ssd = _matmul(yn.reshape(T, dm.DI), w["w_ssd_out"], "NN", F32, tag + "_ssd_out", tk=2048).reshape(Bl, S, D)
    s = _sc_conv_fwd(proj, sp["sc_conv_w8"], (dm.OB, dm.OC, dm.OH), D, tag + "_sc_conv")
    y_sc = _matmul(s.reshape(T, D), w["w_sc_out"], "NN", F32, tag + "_sc_out", tk=1024).reshape(Bl, S, D)
    mixin = _merge_fwd(proj, y_ssd, y_sc, dm.OG, D, tag + "_merge")
    mix = _matmul(mixin.reshape(T, D), w["w_o"], "NN", F32, tag + "_o", tk=1024).reshape(Bl, S, D)
    return mix, (h, proj, xbc_a, y, yn, hprev, y_ssd, y_sc, s, mixin, dtg, acs, acs_t)


def _mix_bwd(dm, dmix, saved, w, sp, sel, tag, gbig, slab):
    Bl, S, D = dm.Bl, dm.S, dm.D
    T = Bl * S
    h, proj, xbc_a, y, yn, hprev, y_ssd, y_sc, s, mixin, dtg, acs, acs_t = saved
    dmix2 = dmix.reshape(T, D)
    g = {}
    gbig["w_o"] = _matmul(mixin.reshape(T, D), dmix2, "TN", WIRE_DTYPE, tag + "_dw_o", slab=(gbig.get("w_o"),) + slab)
    dmixin = _matmul(dmix2, w["w_o"], "NT", F32, tag + "_d_o", tk=1024).reshape(Bl, S, D)
    dy_ssd, dy_sc, dg1, dg2 = _merge_bwd(dmixin, proj, y_ssd, y_sc, dm.OG, D, tag + "_merge_bwd")
    gbig["w_sc_out"] = _matmul(s.reshape(T, D), dy_sc.reshape(T, D), "TN", WIRE_DTYPE, tag + "_dw_sc_out",
                               slab=(gbig.get("w_sc_out"),) + slab)
    ds = _matmul(dy_sc.reshape(T, D), w["w_sc_out"], "NT", F32, tag + "_d_sc_out", tk=1024).reshape(Bl, S, D)
    dscb, dscc, dsch, dscw = _sc_conv_bwd(ds, proj, sp["sc_conv_w8"], (dm.OB, dm.OC, dm.OH), D, tag + "_sc_conv_bwd")
    g["sc_conv_w"] = dscw[:3]
    gbig["w_ssd_out"] = _matmul(yn.reshape(T, dm.DI), dy_ssd.reshape(T, D), "TN", WIRE_DTYPE, tag + "_dw_ssd_out",
                                slab=(gbig.get("w_ssd_out"),) + slab)
    dyn = _matmul(dy_ssd.reshape(T, D), w["w_ssd_out"], "NT", F32, tag + "_d_ssd_out", tk=1024).reshape(Bl, S, dm.DI)
    dz, dxs, dbm, dcm, d_a, ddt, ddsk, dng = _scan_bwd(dyn, y, xbc_a, proj, hprev, dtg, acs, acs_t, sp["dsk_exp"],
                                                       sp["ssd_norm_g"], dm, tag + "_ssd_scan_bwd")
    ddtr, dpar = _ssd_post(d_a, ddt, dtg, proj, sp["dt_bias"], sp["a_log"], sel[0], dm, tag + "_ssd_post")
    g["ssd_dt_bias"], g["ssd_a_log"] = dpar[0, :dm.H], dpar[1, :dm.H]
    g["ssd_d"] = jnp.sum(ddsk.reshape(dm.H, HEAD_DIM), axis=-1)
    g["ssd_norm_g"] = dng[0]
    dxa = jnp.concatenate([dxs, dbm, dcm], axis=-1)
    dxbc, dcw, dcb = _ssd_conv_bwd(dxa, proj, sp["ssd_conv_w8"], sp["ssd_conv_b"], dm.OX, dm.CD, tag + "_ssd_conv_bwd")
    g["ssd_conv_w"], g["ssd_conv_b"] = dcw[:4], dcb[0]
    dproj = jnp.concatenate([dz, dxbc, dscb, dscc, dsch, dg1, dg2, ddtr], axis=-1).reshape(T, dm.NP)
    gbig["w_in_p"] = _matmul(h.reshape(T, D), dproj, "TN", WIRE_DTYPE, tag + "_dw_in", tn=1152,
                             slab=(gbig.get("w_in_p"),) + slab)
    dh = _matmul(dproj, w["w_in_p"], "NT", F32, tag + "_d_in", tk=1152).reshape(Bl, S, D)
    return dh, g


def _ffn_fwd(dm, h, w, sp, tag):
    Bl, S, D = dm.Bl, dm.S, dm.D
    T = Bl * S
    up = _matmul(h.reshape(T, D), w["w_up"], "NN", F32, tag + "_up", tn=1408).reshape(Bl, S, 2 * dm.DFF)
    a = _ffn_conv_fwd(up, sp["ffn_conv_w8"], sp["ffn_conv_b"], dm.DFF, tag + "_ffn_conv")
    f = _matmul(a.reshape(T, dm.DFF), w["w_down"], "NN", F32, tag + "_down", tk=2816).reshape(Bl, S, D)
    return f, (h, up, a)


def _ffn_bwd(dm, df, saved, w, sp, tag, gbig, slab):
    Bl, S, D = dm.Bl, dm.S, dm.D
    T = Bl * S
    h, up, a = saved
    df2 = df.reshape(T, D)
    g = {}
    gbig["w_down"] = _matmul(a.reshape(T, dm.DFF), df2, "TN", WIRE_DTYPE, tag + "_dw_down", tm=1408,
                             slab=(gbig.get("w_down"),) + slab)
    da = _matmul(df2, w["w_down"], "NT", F32, tag + "_d_down", tn=1408).reshape(Bl, S, dm.DFF)
    dg, dv, dcw, dcb = _ffn_conv_bwd(da, up, sp["ffn_conv_w8"], sp["ffn_conv_b"], dm.DFF, tag + "_ffn_conv_bwd")
    g["ffn_conv_w"], g["ffn_conv_b"] = dcw[:3], dcb[0]
    dup = jnp.concatenate([dg, dv], axis=-1).reshape(T, 2 * dm.DFF)
    gbig["w_up"] = _matmul(h.reshape(T, D), dup, "TN", WIRE_DTYPE, tag + "_dw_up", tn=1408,
                           slab=(gbig.get("w_up"),) + slab)
    dh = _matmul(dup, w["w_up"], "NT", F32, tag + "_d_up", tk=1408).reshape(Bl, S, D)
    return dh, g


def _local_step(dm, x, c, target, wfull, small):
    Bl, S, D = dm.Bl, dm.S, dm.D
    depth = len(wfull)
    sel = _head_select(dm)
    c16 = jnp.pad(c.astype(F32), ((0, 16 - Bl), (0, 0)))
    sps, mods, acts = [], [], []
    for l in range(depth):
        sm = small[l]
        sps.append(dict(
            ssd_conv_w8=_pad8(sm["ssd_conv_w"]), ssd_conv_b=sm["ssd_conv_b"], dt_bias=_lane_pad(sm["ssd_dt_bias"]),
            a_log=_lane_pad(sm["ssd_a_log"]), dsk_exp=jnp.repeat(sm["ssd_d"].astype(F32), HEAD_DIM).reshape(1, dm.DI),
            ssd_norm_g=sm["ssd_norm_g"],
            sc_conv_w8=_pad8(sm["sc_conv_w"]), ffn_conv_w8=_pad8(sm["ffn_conv_w"]), ffn_conv_b=sm["ffn_conv_b"]))
        act, mod = _ada_fwd(c16, wfull[l]["ada_w"], sm["ada_b"], f"l{l}_ada")
        acts.append(act)
        mods.append(jnp.pad(mod[:Bl].reshape(Bl, 6, D), ((0, 0), (0, 2), (0, 0))))

    def sub(i):
        l, ffn = i // 2, i % 2
        sm = small[l]
        return dict(l=l, ffn=ffn, pre_g=sm["ffn_pre_g" if ffn else "mix_pre_g"],
                    post_g=sm["ffn_post_g" if ffn else "mix_post_g"], mod=mods[l], row=3 * ffn,
                    tag=f"l{l}_{'ffn' if ffn else 'mix'}")

    nsub = 2 * depth
    subs = [sub(i) for i in range(nsub)]
    xs, fs, saves = [x], [], []
    h = _norm_mod_fwd(x, subs[0]["pre_g"], subs[0]["mod"], subs[0]["row"], "l0_mix_pre_norm")
    for i, sb in enumerate(subs):
        l = sb["l"]
        if sb["ffn"]:
            f, sv = _ffn_fwd(dm, h, wfull[l], sps[l], sb["tag"])
        else:
            f, sv = _mix_fwd(dm, h, wfull[l], sps[l], sel, sb["tag"])
        nxt = None
        if i + 1 < nsub:
            nb = subs[i + 1]
            nxt = (nb["pre_g"], nb["mod"], nb["row"])
        xn, h = _post_norm_fwd(xs[-1], f, sb["post_g"], sb["mod"], sb["row"] + 2, nxt, sb["tag"] + "_post_norm")
        xs.append(xn)
        fs.append(f)
        saves.append(sv)

    dy, loss = _loss_fwd_bwd(xs[-1], target, "loss")

    grads = [dict() for _ in range(depth)]
    gbig = {}
    dmod = [[None] * 6 for _ in range(depth)]
    dx, dh = dy, None
    for i in reversed(range(nsub)):
        sb = subs[i]
        l = sb["l"]
        nxt = None
        if i + 1 < nsub:
            nb = subs[i + 1]
            nxt = (dh, xs[i + 1], nb["pre_g"], nb["mod"], nb["row"])
        dx, df, pb, shg = _norm_bwd(dx, nxt, (fs[i], sb["post_g"], sb["mod"], sb["row"] + 2), sb["tag"] + "_post_norm_bwd")
        if nxt is not None:
            nb = subs[i + 1]
            dmod[nb["l"]][nb["row"]], dmod[nb["l"]][nb["row"] + 1] = pb[:, 0], pb[:, 1]
            grads[nb["l"]]["ffn_pre_g" if nb["ffn"] else "mix_pre_g"] = shg[0]
        dmod[l][sb["row"] + 2] = pb[:, 2]
        grads[l]["ffn_post_g" if sb["ffn"] else "mix_post_g"] = shg[1]
        if sb["ffn"]:
            dh, g = _ffn_bwd(dm, df, saves[i], wfull[l], sps[l], sb["tag"], gbig, (l, depth))
        else:
            dh, g = _mix_bwd(dm, df, saves[i], wfull[l], sps[l], sel, sb["tag"], gbig, (l, depth))
        grads[l].update(g)
    sb = subs[0]
    grad_x, _, pb, shg = _norm_bwd(dx, (dh, xs[0], sb["pre_g"], sb["mod"], sb["row"]), None, "l0_mix_pre_norm_bwd")
    dmod[0][0], dmod[0][1] = pb[:, 0], pb[:, 1]
    grads[0]["mix_pre_g"] = shg[0]

    for l in range(depth):
        dm6 = jnp.concatenate(dmod[l], axis=-1)
        grads[l]["ada_b"] = jnp.sum(dm6, axis=0)
        dm16 = jnp.pad(dm6, ((0, 16 - Bl), (0, 0))).astype(MXU_DTYPE)
        gbig["ada_w"] = _matmul(acts[l], dm16, "TN", WIRE_DTYPE, f"l{l}_dw_ada", slab=(gbig.get("ada_w"), l, depth))
    return loss, grad_x, grads, gbig


_WEIGHTS = ("ada_w", "ada_b", "mix_pre_g", "mix_post_g", "w_in", "ssd_conv_w", "ssd_conv_b", "ssd_dt_bias",
            "ssd_a_log", "ssd_d", "ssd_norm_g", "w_ssd_out", "sc_conv_w", "w_sc_out", "w_o", "ffn_pre_g",
            "ffn_post_g", "w_up", "ffn_conv_w", "ffn_conv_b", "w_down")
_INPUTS = ("x", "c") + _WEIGHTS + ("loss_target",) + tuple("m_" + n for n in _WEIGHTS) + tuple("v_" + n for n in _WEIGHTS)
_BIG = {"ada_w": 2, "w_in": 2, "w_ssd_out": 1, "w_sc_out": 1, "w_o": 1, "w_up": 2, "w_down": 1}
_CONV = ("ssd_conv_w", "sc_conv_w", "ffn_conv_w")
_SMALL = tuple(n for n in _WEIGHTS if n not in _BIG)


def _step(a):
    x, c, target = a["x"], a["c"], a["loss_target"]
    depth = a["ada_w"].shape[0]
    dm = _dims(x, a["ssd_dt_bias"], a["ssd_norm_g"], a["ssd_conv_b"], a["ffn_conv_b"])
    chip = 2 * lax.axis_index("x") + lax.axis_index("y")

    shards = [a[n].astype(WIRE_DTYPE) for n in _BIG]
    axes = list(_BIG.values())
    kinds = [_gather_kind(s, ax) for s, ax in zip(shards, axes)]
    full = {n: w.astype(MXU_DTYPE) for n, w in zip(_BIG, _gather_weights(shards, axes, "gather_weights"))}
    conv_shapes = [a[n].shape for n in _CONV]
    gotc = _exchange_chips(_pack([a[n] for n in _CONV], F32, LANES, 8), False, "gather_conv_weights")
    piecesc = [_unpack(gotc[j], conv_shapes) for j in range(N_CHIPS)]
    fullc = {n: jnp.concatenate([piecesc[j][i] for j in range(N_CHIPS)], axis=2) for i, n in enumerate(_CONV)}

    wfull, small = [], []
    for l in range(depth):
        wf = {n: full[n][l] for n in _BIG if n != "w_in"}
        wf["w_in_p"] = _permute_w_in(full["w_in"][l], dm)
        wfull.append(wf)
        small.append({n: (fullc[n][l] if n in _CONV else a[n][l]) for n in _SMALL})

    loss_part, grad_x, grads, gbig = _local_step(dm, x, c, target, wfull, small)

    core = lax.axis_index("c").astype(jnp.int32).reshape(1)
    gst = []
    for n, kind in zip(_BIG, kinds):
        g = _unpermute_w_in(gbig["w_in_p"], dm) if n == "w_in" else gbig[n]
        if kind == "slots":
            g = jnp.moveaxis(g.reshape(depth, g.shape[1], N_CHIPS, g.shape[2] // N_CHIPS), 2, 1)
            g = g.reshape(depth, -1, g.shape[-1])
        gst.append(g)
    other = _swap_other_layer(gst, "swap_layer_grads")
    parts = [_pair_sum(g, o, core, "pair_sum_" + n) for g, o, n in zip(gst, other, _BIG)]
    parts = [p.reshape(N_CHIPS, -1, p.shape[-1]) if k == "slots" else p for p, k in zip(parts, kinds)]
    got = _scatter_layer(parts, kinds, "scatter_grads")
    mine = [_sum_slots(g, "sum_chip_grads_" + n) for g, n in zip(got, _BIG)]
    theirs = _sibling_swap(mine, "swap_core_grads")

    out = {}
    for i, n in enumerate(_BIG):
        out[n] = _adamw_layers(a[n], a["m_" + n], a["v_" + n], mine[i], theirs[i], core, "adamw_" + n)

    gsmall = [jnp.stack([grads[l][n] for l in range(depth)]) for n in _SMALL]
    small_shapes = [g.shape for g in gsmall]
    summed = _sum_slots(_allgather_all(_pack(gsmall, F32, LANES, 8), "gather_small_grads"), "sum_small_grads")
    gs = dict(zip(_SMALL, _unpack(summed, small_shapes)))
    for n in _CONV:
        wcols = a[n].shape[2]
        gs[n] = lax.dynamic_slice_in_dim(gs[n], chip * wcols, wcols, axis=2)
    local_shapes = [a[n].shape for n in _SMALL]
    res = _adamw(_pack([a[n] for n in _SMALL], F32, LANES, 8), _pack([a["m_" + n] for n in _SMALL], F32, LANES, 8),
                 _pack([a["v_" + n] for n in _SMALL], F32, LANES, 8), _pack([gs[n] for n in _SMALL], F32, LANES, 8),
                 "adamw_small")
    res = [_unpack(r, local_shapes) for r in res]
    for i, n in enumerate(_SMALL):
        out[n] = [r[i] for r in res]

    loss = lax.psum(loss_part, ("x", "y", "c"))
    return (loss, grad_x) + tuple(out[n][k] for k in range(4) for n in _WEIGHTS)


def kernel(x, c, ada_w, ada_b, mix_pre_g, mix_post_g, w_in, ssd_conv_w, ssd_conv_b, ssd_dt_bias, ssd_a_log, ssd_d, ssd_norm_g, w_ssd_out, sc_conv_w, w_sc_out, w_o, ffn_pre_g, ffn_post_g, w_up, ffn_conv_w, ffn_conv_b, w_down, loss_target, m_ada_w, m_ada_b, m_mix_pre_g, m_mix_post_g, m_w_in, m_ssd_conv_w, m_ssd_conv_b, m_ssd_dt_bias, m_ssd_a_log, m_ssd_d, m_ssd_norm_g, m_w_ssd_out, m_sc_conv_w, m_w_sc_out, m_w_o, m_ffn_pre_g, m_ffn_post_g, m_w_up, m_ffn_conv_w, m_ffn_conv_b, m_w_down, v_ada_w, v_ada_b, v_mix_pre_g, v_mix_post_g, v_w_in, v_ssd_conv_w, v_ssd_conv_b, v_ssd_dt_bias, v_ssd_a_log, v_ssd_d, v_ssd_norm_g, v_w_ssd_out, v_sc_conv_w, v_w_sc_out, v_w_o, v_ffn_pre_g, v_ffn_post_g, v_w_up, v_ffn_conv_w, v_ffn_conv_b, v_w_down):
    return _step(dict(zip(_INPUTS, (
        x, c, ada_w, ada_b, mix_pre_g, mix_post_g, w_in, ssd_conv_w, ssd_conv_b, ssd_dt_bias, ssd_a_log, ssd_d, ssd_norm_g, w_ssd_out, sc_conv_w, w_sc_out, w_o, ffn_pre_g, ffn_post_g, w_up, ffn_conv_w, ffn_conv_b, w_down, loss_target, m_ada_w, m_ada_b, m_mix_pre_g, m_mix_post_g, m_w_in, m_ssd_conv_w, m_ssd_conv_b, m_ssd_dt_bias, m_ssd_a_log, m_ssd_d, m_ssd_norm_g, m_w_ssd_out, m_sc_conv_w, m_w_sc_out, m_w_o, m_ffn_pre_g, m_ffn_post_g, m_w_up, m_ffn_conv_w, m_ffn_conv_b, m_w_down, v_ada_w, v_ada_b, v_mix_pre_g, v_mix_post_g, v_w_in, v_ssd_conv_w, v_ssd_conv_b, v_ssd_dt_bias, v_ssd_a_log, v_ssd_d, v_ssd_norm_g, v_w_ssd_out, v_sc_conv_w, v_w_sc_out, v_w_o, v_ffn_pre_g, v_ffn_post_g, v_w_up, v_ffn_conv_w, v_ffn_conv_b, v_w_down))))
```

```python
import math

import jax
import jax.numpy as jnp
from jax import lax
from jax.experimental import pallas as pl
from jax.experimental.pallas import tpu as pltpu

F32 = jnp.float32
MXU_DTYPE = jnp.bfloat16
WIRE_DTYPE = jnp.bfloat16
HI = lax.Precision.HIGHEST
EPS = 1e-6
N_GROUPS = 4
CHUNK = 128
HEAD_DIM = 64
LANES = 128
HALO = 8
N_CHIPS = 4
N_DEV = 8
VMEM_LIMIT = 56 * 1024 * 1024
ADAM_LR, ADAM_B1, ADAM_B2, ADAM_EPS, ADAM_WD, ADAM_STEP = 0.001, 0.9, 0.999, 1e-08, 0.01, 10
MESH = pl.DeviceIdType.MESH

NN = (((1,), (0,)), ((), ()))
NT = (((1,), (1,)), ((), ()))
TN = (((0,), (0,)), ((), ()))


def _dg(a, b, dn, precision=None):
    return lax.dot_general(a, b, dn, precision=precision, preferred_element_type=F32)


def _tile(dim, pref, mult=LANES):
    t = (min(pref, dim) // mult) * mult
    while t >= mult:
        if dim % t == 0:
            return t
        t -= mult
    return dim


def _cp(sem):
    return pltpu.CompilerParams(dimension_semantics=sem, vmem_limit_bytes=VMEM_LIMIT)


def _sigmoid(x):
    return 1.0 / (1.0 + jnp.exp(-x))


def _softplus(x):
    return jnp.maximum(x, 0.0) + jnp.log1p(jnp.exp(-jnp.abs(x)))


def _matmul(a, b, mode, out_dtype, name, tm=1024, tn=1024, tk=1024, slab=None):
    if mode == "NN":
        (M, K), N = a.shape, b.shape[1]
    elif mode == "NT":
        (M, K), N = a.shape, b.shape[0]
    else:
        (K, M), N = a.shape, b.shape[1]
    tm, tn, tk = _tile(M, tm), _tile(N, tn), _tile(K, tk)
    nk = K // tk
    dn = {"NN": NN, "NT": NT, "TN": TN}[mode]
    carry = slab is not None and slab[0] is not None

    def body_one(a_ref, b_ref, *rest):
        o_ref = rest[-1]
        o_ref[...] = _dg(a_ref[...], b_ref[...], dn).astype(o_ref.dtype)

    def body_acc(a_ref, b_ref, *rest):
        o_ref, acc_ref = rest[-2:]
        k = pl.program_id(2)

        @pl.when(k == 0)
        def _():
            acc_ref[...] = jnp.zeros_like(acc_ref)

        acc_ref[...] += _dg(a_ref[...], b_ref[...], dn)

        @pl.when(k == nk - 1)
        def _():
            o_ref[...] = acc_ref[...].astype(o_ref.dtype)

    a_spec = (pl.BlockSpec((tk, tm), lambda i, j, k: (k, i)) if mode == "TN"
              else pl.BlockSpec((tm, tk), lambda i, j, k: (i, k)))
    b_spec = (pl.BlockSpec((tn, tk), lambda i, j, k: (j, k)) if mode == "NT"
              else pl.BlockSpec((tk, tn), lambda i, j, k: (k, j)))
    if slab is None:
        out_spec = pl.BlockSpec((tm, tn), lambda i, j, k: (i, j))
        out_shape = jax.ShapeDtypeStruct((M, N), out_dtype)
    else:
        layer = slab[1]
        out_spec = pl.BlockSpec((None, tm, tn), lambda i, j, k: (layer, i, j))
        out_shape = jax.ShapeDtypeStruct((slab[2], M, N), out_dtype)
    return pl.pallas_call(
        body_one if nk == 1 else body_acc, name=name, grid=(M // tm, N // tn, nk),
        in_specs=[a_spec, b_spec] + ([_ANY] if carry else []),
        out_specs=out_spec, out_shape=out_shape,
        input_output_aliases={2: 0} if carry else {},
        scratch_shapes=[] if nk == 1 else [pltpu.VMEM((tm, tn), F32)],
        compiler_params=_cp(("parallel", "parallel", "arbitrary")),
    )(*((a, b, slab[0]) if carry else (a, b)))


def _ada_fwd(c16, ada_w, ada_b, name):
    rows, D = c16.shape
    N6 = ada_w.shape[1]
    tn = _tile(N6, 1536)

    def body(c_ref, w_ref, b_ref, act_ref, mod_ref):
        c = c_ref[...]
        act = (c * _sigmoid(c)).astype(act_ref.dtype)
        act_ref[...] = act
        mod_ref[...] = _dg(act, w_ref[...], NN) + b_ref[...]

    return pl.pallas_call(
        body, name=name, grid=(N6 // tn,),
        in_specs=[pl.BlockSpec((rows, D), lambda j: (0, 0)),
                  pl.BlockSpec((D, tn), lambda j: (0, j)),
                  pl.BlockSpec((1, tn), lambda j: (0, j))],
        out_specs=[pl.BlockSpec((rows, D), lambda j: (0, 0)),
                   pl.BlockSpec((rows, tn), lambda j: (0, j))],
        out_shape=[jax.ShapeDtypeStruct((rows, D), MXU_DTYPE),
                   jax.ShapeDtypeStruct((rows, N6), F32)],
        compiler_params=_cp(("arbitrary",)),
    )(c16, ada_w, ada_b.reshape(1, N6))


def _norm_mod_rows(x, g, sc, sh):
    r = lax.rsqrt(jnp.mean(x * x, axis=-1, keepdims=True) + EPS)
    return ((x * r) * g) * (1.0 + sc) + sh


def _norm_mod_fwd(x, g, mod, row_sh, name, ts=512):
    Bl, S, D = x.shape
    ts = _tile(S, ts, 8)

    def body(x_ref, g_ref, mod_ref, h_ref):
        sh = mod_ref[row_sh:row_sh + 1, :]
        sc = mod_ref[row_sh + 1:row_sh + 2, :]
        h_ref[...] = _norm_mod_rows(x_ref[...], g_ref[...], sc, sh).astype(h_ref.dtype)

    tok = pl.BlockSpec((None, ts, D), lambda b, s: (b, s, 0))
    return pl.pallas_call(
        body, name=name, grid=(Bl, S // ts),
        in_specs=[tok, pl.BlockSpec((1, D), lambda b, s: (0, 0)),
                  pl.BlockSpec((None, 8, D), lambda b, s: (b, 0, 0))],
        out_specs=tok, out_shape=jax.ShapeDtypeStruct((Bl, S, D), MXU_DTYPE),
        compiler_params=_cp(("parallel", "parallel")),
    )(x, g.reshape(1, D), mod)


def _post_norm_fwd(xp, f, post_g, mod, row_gt, nxt, name, ts=512):
    Bl, S, D = xp.shape
    ts = _tile(S, ts, 8)
    has_next = nxt is not None

    def body(*refs):
        if has_next:
            xp_ref, f_ref, pg_ref, mod_ref, ng_ref, nmod_ref, x_ref, h_ref = refs
        else:
            xp_ref, f_ref, pg_ref, mod_ref, x_ref = refs
        f = f_ref[...]
        r = lax.rsqrt(jnp.mean(f * f, axis=-1, keepdims=True) + EPS)
        x = xp_ref[...] + mod_ref[row_gt:row_gt + 1, :] * ((f * r) * pg_ref[...])
        x_ref[...] = x
        if has_next:
            rs = nxt[2]
            h_ref[...] = _norm_mod_rows(x, ng_ref[...], nmod_ref[rs + 1:rs + 2, :], nmod_ref[rs:rs + 1, :]).astype(h_ref.dtype)

    tok = pl.BlockSpec((None, ts, D), lambda b, s: (b, s, 0))
    vec = pl.BlockSpec((1, D), lambda b, s: (0, 0))
    modspec = pl.BlockSpec((None, 8, D), lambda b, s: (b, 0, 0))
    ins = [xp, f, post_g.reshape(1, D), mod]
    in_specs = [tok, tok, vec, modspec]
    out_specs = [tok]
    out_shape = [jax.ShapeDtypeStruct((Bl, S, D), F32)]
    if has_next:
        ins += [nxt[0].reshape(1, D), nxt[1]]
        in_specs += [vec, modspec]
        out_specs += [tok]
        out_shape += [jax.ShapeDtypeStruct((Bl, S, D), MXU_DTYPE)]
    out = pl.pallas_call(
        body, name=name, grid=(Bl, S // ts), in_specs=in_specs, out_specs=out_specs, out_shape=out_shape,
        compiler_params=_cp(("parallel", "parallel")),
    )(*ins)
    return (out[0], out[1]) if has_next else (out[0], None)


def _loss_fwd_bwd(y, target, name, ts=512):
    Bl, S, D = y.shape
    ts = _tile(S, ts, 8)

    def body(y_ref, t_ref, dy_ref, l_ref):
        @pl.when((pl.program_id(0) == 0) & (pl.program_id(1) == 0))
        def _():
            l_ref[...] = jnp.zeros_like(l_ref)

        e = y_ref[...] - t_ref[...]
        dy_ref[...] = e * (1.0 / D)
        l_ref[...] += 0.5 * jnp.sum(jnp.mean(e * e, axis=-1, keepdims=True), axis=0, keepdims=True)

    tok = pl.BlockSpec((None, ts, D), lambda b, s: (b, s, 0))
    dy, l = pl.pallas_call(
        body, name=name, grid=(Bl, S // ts), in_specs=[tok, tok],
        out_specs=[tok, pl.BlockSpec((8, LANES), lambda b, s: (0, 0))],
        out_shape=[jax.ShapeDtypeStruct((Bl, S, D), F32), jax.ShapeDtypeStruct((8, LANES), F32)],
        compiler_params=_cp(("arbitrary", "arbitrary")),
    )(y, target)
    return dy, l[0, 0]


def _norm_bwd(dx_res, nxt, prv, name, ts=256):
    Bl, S, D = dx_res.shape
    ts = _tile(S, ts, 8)
    has_next, has_prev = nxt is not None, prv is not None

    def body(*refs):
        refs = list(refs)
        dxr_ref = refs.pop(0)
        if has_next:
            dh_ref, x_ref, g_ref, nmod_ref = refs[:4]
            refs = refs[4:]
        if has_prev:
            f_ref, pg_ref, pmod_ref = refs[:3]
            refs = refs[3:]
        dx_ref = refs.pop(0)
        if has_prev:
            df_ref = refs.pop(0)
        pb_ref, sh_ref = refs
        b, s = pl.program_id(0), pl.program_id(1)

        @pl.when(s == 0)
        def _():
            pb_ref[...] = jnp.zeros_like(pb_ref)

        @pl.when((b == 0) & (s == 0))
        def _():
            sh_ref[...] = jnp.zeros_like(sh_ref)

        dx = dxr_ref[...]
        if has_next:
            rs = nxt[4]
            x, dh, g = x_ref[...], dh_ref[...], g_ref[...]
            sc1 = 1.0 + nmod_ref[rs + 1:rs + 2, :]
            r = lax.rsqrt(jnp.mean(x * x, axis=-1, keepdims=True) + EPS)
            xn = x * r
            pb_ref[0:1, :] += jnp.sum(dh, axis=0, keepdims=True)
            pb_ref[1:2, :] += jnp.sum(dh * (xn * g), axis=0, keepdims=True)
            sh_ref[0:1, :] += jnp.sum(dh * sc1 * xn, axis=0, keepdims=True)
            dxn = dh * sc1 * g
            dx = dx + r * (dxn - xn * jnp.mean(dxn * xn, axis=-1, keepdims=True))
        dx_ref[...] = dx
        if has_prev:
            rg = prv[3]
            f, pg = f_ref[...], pg_ref[...]
            gt = pmod_ref[rg:rg + 1, :]
            r = lax.rsqrt(jnp.mean(f * f, axis=-1, keepdims=True) + EPS)
            fn = f * r
            pb_ref[2:3, :] += jnp.sum(dx * (fn * pg), axis=0, keepdims=True)
            drn = dx * gt
            sh_ref[1:2, :] += jnp.sum(drn * fn, axis=0, keepdims=True)
            dfn = drn * pg
            df_ref[...] = (r * (dfn - fn * jnp.mean(dfn * fn, axis=-1, keepdims=True))).astype(df_ref.dtype)

    tok = pl.BlockSpec((None, ts, D), lambda b, s: (b, s, 0))
    vec = pl.BlockSpec((1, D), lambda b, s: (0, 0))
    modspec = pl.BlockSpec((None, 8, D), lambda b, s: (b, 0, 0))
    ins, in_specs = [dx_res], [tok]
    if has_next:
        ins += [nxt[0], nxt[1], nxt[2].reshape(1, D), nxt[3]]
        in_specs += [tok, tok, vec, modspec]
    if has_prev:
        ins += [prv[0], prv[1].reshape(1, D), prv[2]]
        in_specs += [tok, vec, modspec]
    out_specs, out_shape = [tok], [jax.ShapeDtypeStruct((Bl, S, D), F32)]
    if has_prev:
        out_specs += [tok]
        out_shape += [jax.ShapeDtypeStruct((Bl, S, D), MXU_DTYPE)]
    out_specs += [modspec, pl.BlockSpec((8, D), lambda b, s: (0, 0))]
    out_shape += [jax.ShapeDtypeStruct((Bl, 8, D), F32), jax.ShapeDtypeStruct((8, D), F32)]
    out = pl.pallas_call(
        body, name=name, grid=(Bl, S // ts), in_specs=in_specs, out_specs=out_specs, out_shape=out_shape,
        compiler_params=_cp(("arbitrary", "arbitrary")),
    )(*ins)
    if has_prev:
        return out[0], out[1], out[2], out[3]
    return out[0], None, out[1], out[2]


def _shift_down(x, j):
    return x if j == 0 else pltpu.roll(x, j, axis=0)


def _shift_up(x, j):
    return x if j == 0 else pltpu.roll(x, x.shape[0] - j, axis=0)


def _conv(xall, w_ref, K):
    y = w_ref[K - 1:K, :] * xall
    for k in range(K - 1):
        y = y + w_ref[k:k + 1, :] * _shift_down(xall, K - 1 - k)
    return y


def _conv_t(dall, xall, w, K, rows):
    xt = xall[HALO:HALO + rows]
    y = w[K - 1:K, :] * dall[HALO:HALO + rows]
    gw = [None] * K
    gw[K - 1] = jnp.sum(dall[HALO:HALO + rows] * xt, axis=0, keepdims=True)
    for k in range(K - 1):
        sh = _shift_up(dall, K - 1 - k)[HALO:HALO + rows]
        y = y + w[k:k + 1, :] * sh
        gw[k] = jnp.sum(sh * xt, axis=0, keepdims=True)
    return y, gw


def _conv_t_rows(dall, w_ref, K):
    y = w_ref[K - 1:K, :] * dall
    for k in range(K - 1):
        y = y + w_ref[k:k + 1, :] * _shift_up(dall, K - 1 - k)
    return y


def _conv_wgrad(acc_ref, dtile, xall, K, ts):
    for k in range(K):
        xs = _shift_down(xall, K - 1 - k)[HALO:HALO + ts]
        acc_ref[k:k + 1, :] += jnp.sum(dtile * xs, axis=0, keepdims=True)


ROW_CHUNK = 64


def _lane_chunks(W):
    lc = 2 * LANES if W % (2 * LANES) == 0 else LANES
    return [(j * lc, lc) for j in range(W // lc)]


def _fill_ext(ext_ref, prev, tile, nxt, ts):
    ext_ref[0:HALO, :] = prev
    ext_ref[HALO:HALO + ts, :] = tile
    ext_ref[HALO + ts:2 * HALO + ts, :] = nxt


def _halo_specs(ts, W, nS, colblk):
    per = ts // HALO
    tile = pl.BlockSpec((None, ts, W), lambda b, c, s: (b, s, colblk(c)))
    prev = pl.BlockSpec((None, HALO, W), lambda b, c, s: (b, jnp.maximum(s * per - 1, 0), colblk(c)))
    nxt = pl.BlockSpec((None, HALO, W), lambda b, c, s: (b, jnp.minimum((s + 1) * per, nS * per - 1), colblk(c)))
    return tile, prev, nxt


def _masked(ref, keep):
    v = ref[...]
    return jnp.where(keep, v, jnp.zeros_like(v))


def _ssd_conv_fwd(proj, w8, b, off, CD, name, ts=512, W=1024):
    Bl, S, _ = proj.shape
    K = 4
    ts, W = _tile(S, ts, 8), _tile(math.gcd(CD, off), W)
    assert off % W == 0
    nS, nW, ob = S // ts, CD // W, off // W

    def body(x_ref, xp_ref, w_ref, b_ref, o_ref):
        xall = jnp.concatenate([_masked(xp_ref, pl.program_id(2) > 0), x_ref[...]], axis=0)
        xc = _conv(xall, w_ref, K)[HALO:] + b_ref[...]
        o_ref[...] = xc * _sigmoid(xc)

    tile, prev, _ = _halo_specs(ts, W, nS, lambda c: ob + c)
    return pl.pallas_call(
        body, name=name, grid=(Bl, nW, nS),
        in_specs=[tile, prev, pl.BlockSpec((8, W), lambda b_, c, s: (0, c)), pl.BlockSpec((1, W), lambda b_, c, s: (0, c))],
        out_specs=pl.BlockSpec((None, ts, W), lambda b_, c, s: (b_, s, c)),
        out_shape=jax.ShapeDtypeStruct((Bl, S, CD), F32),
        compiler_params=_cp(("parallel", "parallel", "parallel")),
    )(proj, proj, w8, b.reshape(1, CD))


def _ssd_conv_bwd(dxa, proj, w8, b, off, CD, dproj, name, ts=256, W=1024):
    Bl, S, _ = proj.shape
    K = 4
    ts, W = _tile(S, ts, 8), _tile(math.gcd(CD, off), W)
    nS, nW, ob = S // ts, CD // W, off // W

    rc = _tile(ts, ROW_CHUNK, 8)

    def body(d_ref, dn_ref, x_ref, xp_ref, xn_ref, w_ref, b_ref, _, dx_ref, dw_ref, db_ref, xext_ref, dext_ref):
        bb, s = pl.program_id(1), pl.program_id(2)

        @pl.when((bb == 0) & (s == 0))
        def _():
            dw_ref[...] = jnp.zeros_like(dw_ref)
            db_ref[...] = jnp.zeros_like(db_ref)

        last = s == nS - 1
        _fill_ext(xext_ref, _masked(xp_ref, s > 0), x_ref[...], _masked(xn_ref, ~last), ts)
        _fill_ext(dext_ref, jnp.zeros((HALO, W), F32), d_ref[...], _masked(dn_ref, ~last), ts)
        for l0, lc in _lane_chunks(W):
            w, bias = w_ref[:, l0:l0 + lc], b_ref[:, l0:l0 + lc]

            def chunk(i, acc):
                r0 = pl.multiple_of(i * rc, 8)
                xw = xext_ref[pl.ds(r0, rc + 2 * HALO), l0:l0 + lc]
                xc = _conv(xw, w, K) + bias
                sg = _sigmoid(xc)
                dxc = dext_ref[pl.ds(r0, rc + 2 * HALO), l0:l0 + lc] * (sg * (1.0 + xc * (1.0 - sg)))
                dx, gw = _conv_t(dxc, xw, w, K, rc)
                dx_ref[pl.ds(r0, rc), l0:l0 + lc] = dx.astype(dx_ref.dtype)
                gb = jnp.sum(dxc[HALO:HALO + rc], axis=0, keepdims=True)
                return tuple(a + g for a, g in zip(acc, gw + [gb]))

            acc = lax.fori_loop(0, ts // rc, chunk, tuple(jnp.zeros((1, lc), F32) for _ in range(K + 1)))
            for k in range(K):
                dw_ref[k:k + 1, l0:l0 + lc] += acc[k]
            db_ref[0:1, l0:l0 + lc] += acc[K]

    per = ts // HALO
    dtile_s = pl.BlockSpec((None, ts, W), lambda c, b_, s: (b_, s, c))
    dnext_s = pl.BlockSpec((None, HALO, W), lambda c, b_, s: (b_, jnp.minimum((s + 1) * per, nS * per - 1), c))
    xtile_s = pl.BlockSpec((None, ts, W), lambda c, b_, s: (b_, s, ob + c))
    xprev_s = pl.BlockSpec((None, HALO, W), lambda c, b_, s: (b_, jnp.maximum(s * per - 1, 0), ob + c))
    xnext_s = pl.BlockSpec((None, HALO, W), lambda c, b_, s: (b_, jnp.minimum((s + 1) * per, nS * per - 1), ob + c))
    wspec = pl.BlockSpec((8, W), lambda c, b_, s: (0, c))
    return pl.pallas_call(
        body, name=name, grid=(nW, Bl, nS),
        in_specs=[dtile_s, dnext_s, xtile_s, xprev_s, xnext_s, wspec, pl.BlockSpec((1, W), lambda c, b_, s: (0, c)),
                  _ANY],
        out_specs=[xtile_s, wspec, wspec],
        out_shape=[jax.ShapeDtypeStruct(dproj.shape, dproj.dtype), jax.ShapeDtypeStruct((8, CD), F32),
                   jax.ShapeDtypeStruct((8, CD), F32)],
        input_output_aliases={7: 0},
        scratch_shapes=[pltpu.VMEM((ts + 2 * HALO, W), F32)] * 2,
        compiler_params=_cp(("arbitrary", "arbitrary", "arbitrary")),
    )(dxa, dxa, proj, proj, proj, w8, b.reshape(1, CD), dproj)


def _sc_conv_fwd(proj, w8, offs, D, name, ts=512, W=1024):
    Bl, S, _ = proj.shape
    K = 3
    ts, W = _tile(S, ts, 8), _tile(D, W)
    nS, nW = S // ts, D // W
    ob, oc, oh = [o // W for o in offs]

    def body(b_ref, c_ref, cp_ref, h_ref, hp_ref, w_ref, o_ref):
        s = pl.program_id(2)
        keep = s > 0
        vall = jnp.concatenate([_masked(cp_ref, keep) * _masked(hp_ref, keep), c_ref[...] * h_ref[...]], axis=0)
        o_ref[...] = (b_ref[...] * _conv(vall, w_ref, K)[HALO:]).astype(o_ref.dtype)

    tb, _, _ = _halo_specs(ts, W, nS, lambda c: ob + c)
    tc, pc, _ = _halo_specs(ts, W, nS, lambda c: oc + c)
    th, ph, _ = _halo_specs(ts, W, nS, lambda c: oh + c)
    return pl.pallas_call(
        body, name=name, grid=(Bl, nW, nS),
        in_specs=[tb, tc, pc, th, ph, pl.BlockSpec((8, W), lambda b_, c, s: (0, c))],
        out_specs=pl.BlockSpec((None, ts, W), lambda b_, c, s: (b_, s, c)),
        out_shape=jax.ShapeDtypeStruct((Bl, S, D), MXU_DTYPE),
        compiler_params=_cp(("parallel", "parallel", "parallel")),
    )(proj, proj, proj, proj, proj, w8)


def _sc_conv_bwd(ds, proj, w8, offs, D, dproj, name, ts=256):
    Bl, S, _ = proj.shape
    K = 3
    ts, W = _tile(S, ts, 8), D
    nS, nW = S // ts, 1
    ob, oc, oh = [o // W for o in offs]
    assert offs[1] == offs[0] + D and offs[2] == offs[0] + 2 * D and offs[0] % (3 * D) == 0

    def body(d_ref, dn_ref, b_ref, bn_ref, c_ref, cp_ref, cn_ref, h_ref, hp_ref, hn_ref, w_ref, _,
             o_ref, dw_ref):
        db_ref, dc_ref, dh_ref = o_ref.at[:, 0:D], o_ref.at[:, D:2 * D], o_ref.at[:, 2 * D:3 * D]
        bb, s = pl.program_id(1), pl.program_id(2)

        @pl.when((bb == 0) & (s == 0))
        def _():
            dw_ref[...] = jnp.zeros_like(dw_ref)

        first, last = s > 0, s < nS - 1
        zeros = jnp.zeros((HALO, W), F32)
        c_t, h_t = c_ref[...], h_ref[...]
        vall = jnp.concatenate([_masked(cp_ref, first) * _masked(hp_ref, first), c_t * h_t,
                                _masked(cn_ref, last) * _masked(hn_ref, last)], axis=0)
        dcv = jnp.concatenate([zeros, d_ref[...] * b_ref[...], _masked(dn_ref, last) * _masked(bn_ref, last)], axis=0)
        cv = _conv(vall, w_ref, K)[HALO:HALO + ts]
        db_ref[...] = (d_ref[...] * cv).astype(db_ref.dtype)
        dv, gw = _conv_t(dcv, vall, w_ref, K, ts)
        dc_ref[...] = (dv * h_t).astype(dc_ref.dtype)
        dh_ref[...] = (dv * c_t).astype(dh_ref.dtype)
        for k in range(K):
            dw_ref[k:k + 1, :] += gw[k]

    per = ts // HALO

    def specs(o):
        t = pl.BlockSpec((None, ts, W), lambda c, b_, s: (b_, s, o + c))
        p = pl.BlockSpec((None, HALO, W), lambda c, b_, s: (b_, jnp.maximum(s * per - 1, 0), o + c))
        n = pl.BlockSpec((None, HALO, W), lambda c, b_, s: (b_, jnp.minimum((s + 1) * per, nS * per - 1), o + c))
        return t, p, n

    dt_, _, dn_ = specs(0)
    bt, _, bn = specs(ob)
    ct, cp, cn = specs(oc)
    ht, hp, hn = specs(oh)
    wspec = pl.BlockSpec((8, W), lambda c, b_, s: (0, c))
    o3 = offs[0] // (3 * D)
    return pl.pallas_call(
        body, name=name, grid=(nW, Bl, nS),
        in_specs=[dt_, dn_, bt, bn, ct, cp, cn, ht, hp, hn, wspec, _ANY],
        out_specs=[pl.BlockSpec((None, ts, 3 * D), lambda c, b_, s: (b_, s, o3)), wspec],
        out_shape=[jax.ShapeDtypeStruct(dproj.shape, dproj.dtype), jax.ShapeDtypeStruct((8, D), F32)],
        input_output_aliases={11: 0},
        compiler_params=_cp(("arbitrary", "arbitrary", "arbitrary")),
    )(ds, ds, proj, proj, proj, proj, proj, proj, proj, proj, w8, dproj)


def _ffn_conv_fwd(up, w8, b, DFF, name, ts=512, W=1408):
    Bl, S, _ = up.shape
    K = 3
    ts, W = _tile(S, ts, 8), _tile(DFF, W)
    nS, nW = S // ts, DFF // W

    def body(g_ref, gp_ref, v_ref, vp_ref, wg_ref, wv_ref, bg_ref, bv_ref, o_ref):
        keep = pl.program_id(2) > 0
        ug = _conv(jnp.concatenate([_masked(gp_ref, keep), g_ref[...]], axis=0), wg_ref, K)[HALO:] + bg_ref[...]
        uv = _conv(jnp.concatenate([_masked(vp_ref, keep), v_ref[...]], axis=0), wv_ref, K)[HALO:] + bv_ref[...]
        o_ref[...] = (ug * _sigmoid(ug) * uv).astype(o_ref.dtype)

    tg, pg, _ = _halo_specs(ts, W, nS, lambda c: c)
    tv, pv, _ = _halo_specs(ts, W, nS, lambda c: nW + c)
    wg = pl.BlockSpec((8, W), lambda b_, c, s: (0, c))
    wv = pl.BlockSpec((8, W), lambda b_, c, s: (0, nW + c))
    bg = pl.BlockSpec((1, W), lambda b_, c, s: (0, c))
    bv = pl.BlockSpec((1, W), lambda b_, c, s: (0, nW + c))
    b2 = b.reshape(1, 2 * DFF)
    return pl.pallas_call(
        body, name=name, grid=(Bl, nW, nS),
        in_specs=[tg, pg, tv, pv, wg, wv, bg, bv],
        out_specs=pl.BlockSpec((None, ts, W), lambda b_, c, s: (b_, s, c)),
        out_shape=jax.ShapeDtypeStruct((Bl, S, DFF), MXU_DTYPE),
        compiler_params=_cp(("parallel", "parallel", "parallel")),
    )(up, up, up, up, w8, w8, b2, b2)


def _ffn_conv_bwd(da, up, w8, b, DFF, name, ts=256, W=1408):
    Bl, S, _ = up.shape
    K = 3
    ts, W = _tile(S, ts, 8), _tile(DFF, W)
    nS, nW = S // ts, DFF // W

    def body(d_ref, dn_ref, g_ref, gp_ref, gn_ref, v_ref, vp_ref, vn_ref, wg_ref, wv_ref, bg_ref, bv_ref,
             dg_ref, dv_ref, dwg_ref, dwv_ref, dbg_ref, dbv_ref):
        bb, s = pl.program_id(1), pl.program_id(2)

        @pl.when((bb == 0) & (s == 0))
        def _():
            for r in (dwg_ref, dwv_ref, dbg_ref, dbv_ref):
                r[...] = jnp.zeros_like(r)

        first, last = s > 0, s < nS - 1
        gall = jnp.concatenate([_masked(gp_ref, first), g_ref[...], _masked(gn_ref, last)], axis=0)
        vall = jnp.concatenate([_masked(vp_ref, first), v_ref[...], _masked(vn_ref, last)], axis=0)
        dall = jnp.concatenate([jnp.zeros((HALO, W), F32), d_ref[...], _masked(dn_ref, last)], axis=0)
        ug = _conv(gall, wg_ref, K) + bg_ref[...]
        uv = _conv(vall, wv_ref, K) + bv_ref[...]
        sg = _sigmoid(ug)
        dug = dall * uv * (sg * (1.0 + ug * (1.0 - sg)))
        duv = dall * (ug * sg)
        dg_ref[...] = _conv_t_rows(dug, wg_ref, K)[HALO:HALO + ts].astype(dg_ref.dtype)
        dv_ref[...] = _conv_t_rows(duv, wv_ref, K)[HALO:HALO + ts].astype(dv_ref.dtype)
        dug_t, duv_t = dug[HALO:HALO + ts], duv[HALO:HALO + ts]
        dbg_ref[0:1, :] += jnp.sum(dug_t, axis=0, keepdims=True)
        dbv_ref[0:1, :] += jnp.sum(duv_t, axis=0, keepdims=True)
        _conv_wgrad(dwg_ref, dug_t, gall, K, ts)
        _conv_wgrad(dwv_ref, duv_t, vall, K, ts)

    per = ts // HALO

    def specs(o):
        t = pl.BlockSpec((None, ts, W), lambda c, b_, s: (b_, s, o + c))
        p = pl.BlockSpec((None, HALO, W), lambda c, b_, s: (b_, jnp.maximum(s * per - 1, 0), o + c))
        n = pl.BlockSpec((None, HALO, W), lambda c, b_, s: (b_, jnp.minimum((s + 1) * per, nS * per - 1), o + c))
        return t, p, n

    dt_, _, dn_ = specs(0)
    gt, gp, gn = specs(0)
    vt, vp, vn = specs(nW)
    wg = pl.BlockSpec((8, W), lambda c, b_, s: (0, c))
    wv = pl.BlockSpec((8, W), lambda c, b_, s: (0, nW + c))
    bg = pl.BlockSpec((1, W), lambda c, b_, s: (0, c))
    bv = pl.BlockSpec((1, W), lambda c, b_, s: (0, nW + c))
    b2 = b.reshape(1, 2 * DFF)
    act = jax.ShapeDtypeStruct((Bl, S, DFF), MXU_DTYPE)
    small = jax.ShapeDtypeStruct((8, DFF), F32)
    dg, dv, dwg, dwv, dbg, dbv = pl.pallas_call(
        body, name=name, grid=(nW, Bl, nS),
        in_specs=[dt_, dn_, gt, gp, gn, vt, vp, vn, wg, wv, bg, bv],
        out_specs=[dt_, dt_, wg, wg, wg, wg],
        out_shape=[act, act, small, small, small, small],
        compiler_params=_cp(("arbitrary", "arbitrary", "arbitrary")),
    )(da, da, up, up, up, up, up, up, w8, w8, b2, b2)
    return dg, dv, jnp.concatenate([dwg, dwv], axis=1), jnp.concatenate([dbg, dbv], axis=1)


def _merge_fwd(proj, y_ssd, y_sc, off, D, name, ts=512):
    Bl, S, _ = proj.shape
    ts = _tile(S, ts, 8)
    og = off // D

    def body(g1_ref, g2_ref, a_ref, b_ref, o_ref):
        o_ref[...] = (_sigmoid(g1_ref[...]) * a_ref[...] + _sigmoid(g2_ref[...]) * b_ref[...]).astype(o_ref.dtype)

    tok = pl.BlockSpec((None, ts, D), lambda b, s: (b, s, 0))
    return pl.pallas_call(
        body, name=name, grid=(Bl, S // ts),
        in_specs=[pl.BlockSpec((None, ts, D), lambda b, s: (b, s, og)),
                  pl.BlockSpec((None, ts, D), lambda b, s: (b, s, og + 1)), tok, tok],
        out_specs=tok, out_shape=jax.ShapeDtypeStruct((Bl, S, D), MXU_DTYPE),
        compiler_params=_cp(("parallel", "parallel")),
    )(proj, proj, y_ssd, y_sc)


def _merge_bwd(dmixin, proj, y_ssd, y_sc, off, D, dproj0, name, ts=512):
    Bl, S, NP = proj.shape
    ts = _tile(S, ts, 8)
    og = off // D
    assert off % (2 * D) == 0

    def body(d_ref, g1_ref, g2_ref, a_ref, b_ref, *rest):
        da_ref, db_ref, dg_ref = rest[-3:]
        dg1_ref, dg2_ref = dg_ref.at[:, 0:D], dg_ref.at[:, D:2 * D]
        d = d_ref[...]
        s1, s2 = _sigmoid(g1_ref[...]), _sigmoid(g2_ref[...])
        da_ref[...] = (d * s1).astype(da_ref.dtype)
        db_ref[...] = (d * s2).astype(db_ref.dtype)
        dg1_ref[...] = (d * a_ref[...] * (s1 * (1.0 - s1))).astype(dg1_ref.dtype)
        dg2_ref[...] = (d * b_ref[...] * (s2 * (1.0 - s2))).astype(dg2_ref.dtype)

    tok = pl.BlockSpec((None, ts, D), lambda b, s: (b, s, 0))
    act = jax.ShapeDtypeStruct((Bl, S, D), MXU_DTYPE)
    carry = dproj0 is not None
    return pl.pallas_call(
        body, name=name, grid=(Bl, S // ts),
        in_specs=[tok, pl.BlockSpec((None, ts, D), lambda b, s: (b, s, og)),
                  pl.BlockSpec((None, ts, D), lambda b, s: (b, s, og + 1)), tok, tok] + ([_ANY] if carry else []),
        out_specs=[tok, tok, pl.BlockSpec((None, ts, 2 * D), lambda b, s: (b, s, og // 2))],
        out_shape=[act, act, jax.ShapeDtypeStruct((Bl, S, NP), MXU_DTYPE)],
        input_output_aliases={5: 2} if carry else {},
        compiler_params=_cp(("parallel", "parallel")),
    )(*((dmixin, proj, proj, y_ssd, y_sc) + ((dproj0,) if carry else ())))


def _two_term_dot(v, m01):
    hi = v.astype(MXU_DTYPE)
    lo = (v - hi.astype(F32)).astype(MXU_DTYPE)
    return _dg(hi, m01, NN) + _dg(lo, m01, NN)


def _head_to_channels(R):
    rp = R * HEAD_DIM
    return (lax.shift_right_logical(lax.broadcasted_iota(jnp.int32, (LANES, rp), 1), 6)
            == lax.broadcasted_iota(jnp.int32, (LANES, rp), 0)).astype(MXU_DTYPE)


def _expand_heads(mats, R):
    L = mats[0].shape[0]
    out = _two_term_dot(jnp.concatenate(mats, axis=0), _head_to_channels(R))
    return [out[i * L:(i + 1) * L, :] for i in range(len(mats))]


def _expand_cols(mat, R, lane):
    half = lane < HEAD_DIM
    return jnp.concatenate(
        [jnp.where(half, mat[:, 2 * q:2 * q + 1], mat[:, 2 * q + 1:2 * q + 2]) for q in range(R // 2)], axis=1)


def _row_sums(v):
    return _two_term_dot(v, jnp.ones((v.shape[1], LANES), MXU_DTYPE))


def _lanes_to(v, width):
    return jnp.concatenate([v] * (width // LANES), axis=1)


def _head_rows(colvec, R, N):
    return jnp.concatenate([jnp.broadcast_to(colvec[r:r + 1, :], (HEAD_DIM, N)) for r in range(R)], axis=0)


def _ssd_prep(proj, dt_bias, a_log, selcat, selbig, dm, name):
    Bl, S, L, G = dm.Bl, dm.S, CHUNK, N_GROUPS
    nc = S // L
    odt = dm.ODT // LANES

    def body(dtr_ref, dtb_ref, alog_ref, selcat_ref, selbig_ref, dtg_ref, acs_ref, acst_ref):
        row = lax.broadcasted_iota(jnp.int32, (L, L), 0)
        col = lax.broadcasted_iota(jnp.int32, (L, L), 1)
        dt_all = _softplus(dtr_ref[...] + dtb_ref[...])
        acs_all = _dg((row >= col).astype(F32), dt_all * (-jnp.exp(alog_ref[...])), NN, HI)
        dtg_ref[...] = _dg(dt_all, selcat_ref[...], NN, HI)
        acs_ref[...] = _dg(acs_all, selcat_ref[...], NN, HI)
        acst_ref[...] = _dg(selbig_ref[...], acs_all, NT, HI)

    vec = pl.BlockSpec((1, LANES), lambda b, c: (0, 0))
    wide = pl.BlockSpec((None, L, G * LANES), lambda b, c: (b, c, 0))
    return pl.pallas_call(
        body, name=name, grid=(Bl, nc),
        in_specs=[pl.BlockSpec((None, L, LANES), lambda b, c: (b, c, odt)), vec, vec,
                  pl.BlockSpec((LANES, G * LANES), lambda b, c: (0, 0)),
                  pl.BlockSpec((G * 8, LANES), lambda b, c: (0, 0))],
        out_specs=[wide, wide, pl.BlockSpec((None, None, G * 8, L), lambda b, c: (b, c, 0, 0))],
        out_shape=[jax.ShapeDtypeStruct((Bl, S, G * LANES), F32), jax.ShapeDtypeStruct((Bl, S, G * LANES), F32),
                   jax.ShapeDtypeStruct((Bl, nc, G * 8, L), F32)],
        compiler_params=_cp(("parallel", "parallel")),
    )(proj, dt_bias, a_log, selcat, selbig)


def _ssd_post(d_a, ddt, dtg, proj, dt_bias, a_log, selcat, dproj, dm, name):
    Bl, S, L, G = dm.Bl, dm.S, CHUNK, N_GROUPS
    nc = S // L
    odt = dm.ODT // LANES

    def body(da_ref, ddt_ref, dtg_ref, dtr_ref, dtb_ref, alog_ref, selcat_ref, _, ddtr_ref, dpar_ref):
        @pl.when((pl.program_id(0) == 0) & (pl.program_id(1) == 0))
        def _():
            dpar_ref[...] = jnp.zeros_like(dpar_ref)

        row = lax.broadcasted_iota(jnp.int32, (L, L), 0)
        col = lax.broadcasted_iota(jnp.int32, (L, L), 1)
        selcat = selcat_ref[...]
        a_all = -jnp.exp(alog_ref[...])
        a4 = _dg(jnp.broadcast_to(a_all, (8, LANES)), selcat, NN, HI)[0:1, :]
        dadt = _dg((col >= row).astype(F32), da_ref[...], NN, HI)
        ddt4 = ddt_ref[...] + dadt * a4
        da4 = jnp.sum(dadt * dtg_ref[...], axis=0, keepdims=True)
        ddt_all = _dg(ddt4, selcat, NT, HI)
        da_all = _dg(jnp.broadcast_to(da4, (8, G * LANES)), selcat, NT, HI)[0:1, :]
        ddtr = ddt_all * _sigmoid(dtr_ref[...] + dtb_ref[...])
        ddtr_ref[...] = ddtr.astype(ddtr_ref.dtype)
        dpar_ref[0:1, :] += jnp.sum(ddtr, axis=0, keepdims=True)
        dpar_ref[1:2, :] += da_all * a_all

    vec = pl.BlockSpec((1, LANES), lambda b, c: (0, 0))
    wide = pl.BlockSpec((None, L, G * LANES), lambda b, c: (b, c, 0))
    return pl.pallas_call(
        body, name=name, grid=(Bl, nc),
        in_specs=[wide, wide, wide, pl.BlockSpec((None, L, LANES), lambda b, c: (b, c, odt)), vec, vec,
                  pl.BlockSpec((LANES, G * LANES), lambda b, c: (0, 0)), _ANY],
        out_specs=[pl.BlockSpec((None, L, LANES), lambda b, c: (b, c, odt)), pl.BlockSpec((8, LANES), lambda b, c: (0, 0))],
        out_shape=[jax.ShapeDtypeStruct(dproj.shape, dproj.dtype), jax.ShapeDtypeStruct((8, LANES), F32)],
        input_output_aliases={7: 0},
        compiler_params=_cp(("arbitrary", "arbitrary")),
    )(d_a, ddt, dtg, proj, dt_bias, a_log, selcat, dproj)


def _scan_fwd(xbc_a, proj, dtg4, acs4, acs_t4, dsk_exp, norm_g, dm, name):
    Bl, S, DI, N, R, L, G = dm.Bl, dm.S, dm.DI, dm.N, dm.R, CHUNK, N_GROUPS
    RP = R * HEAD_DIM
    nc = S // L
    ob, ocm = DI // N, DI // N + G

    def body(xs_ref, bm_ref, cm_ref, z_ref, dtg_ref, acs_ref, acst_ref, dsk_ref, ng_ref, y_ref, yn_ref, hp_ref, h_ref):
        @pl.when(pl.program_id(2) == 0)
        def _():
            h_ref[...] = jnp.zeros_like(h_ref)

        causal = lax.broadcasted_iota(jnp.int32, (L, L), 0) >= lax.broadcasted_iota(jnp.int32, (L, L), 1)
        lane = lax.broadcasted_iota(jnp.int32, (L, LANES), 1)
        dtg, acs, acs_t = dtg_ref[...], acs_ref[...], acst_ref[...]
        xs = xs_ref[...]
        bmb, cmb = bm_ref[...].astype(MXU_DTYPE), cm_ref[...].astype(MXU_DTYPE)
        sg = _dg(cmb, bmb, NT)
        acs_last = acs[L - 1:L, :]
        dt_exp, ea_exp, ds_exp = _expand_heads([dtg, jnp.exp(acs), jnp.exp(acs_last - acs)], R)
        xdt = xs * dt_exp
        xb = xdt.astype(MXU_DTYPE)
        parts = []
        for q in range(R // 2):
            x2 = xb[:, LANES * q:LANES * (q + 1)]
            ys = []
            for r in (2 * q, 2 * q + 1):
                dec = jnp.exp(jnp.where(causal, acs[:, r:r + 1] - acs_t[r:r + 1, :], -1e30))
                ys.append(_dg((sg * dec).astype(MXU_DTYPE), x2, NN))
            parts.append(jnp.where(lane < HEAD_DIM, ys[0], ys[1]))
        ydiag = jnp.concatenate(parts, axis=1)
        h_cur = h_ref[...]
        hb = h_cur.astype(MXU_DTYPE)
        yoff = _dg(cmb, hb, NT) * ea_exp
        st = _dg((xdt * ds_exp).astype(MXU_DTYPE), bmb, TN)
        hp_ref[...] = hb
        h_ref[...] = h_cur * _head_rows(jnp.exp(acs_t[:, L - 1:L]), R, N) + st
        y = ydiag + yoff + dsk_ref[...] * xs
        y_ref[...] = y
        z = z_ref[...]
        yg = y * (z * _sigmoid(z))
        rr = lax.rsqrt(_row_sums(yg * yg) * (1.0 / RP) + EPS)
        yn_ref[...] = (yg * _lanes_to(rr, RP) * ng_ref[...]).astype(yn_ref.dtype)

    oz = dm.OZ // RP
    grp = pl.BlockSpec((None, L, RP), lambda b, g, c: (b, c, g))
    lanes = pl.BlockSpec((None, L, LANES), lambda b, g, c: (b, c, g))
    chan = pl.BlockSpec((1, RP), lambda b, g, c: (0, g))
    return pl.pallas_call(
        body, name=name, grid=(Bl, G, nc),
        in_specs=[grp,
                  pl.BlockSpec((None, L, N), lambda b, g, c: (b, c, ob + g)),
                  pl.BlockSpec((None, L, N), lambda b, g, c: (b, c, ocm + g)),
                  pl.BlockSpec((None, L, RP), lambda b, g, c: (b, c, oz + g)), lanes, lanes,
                  pl.BlockSpec((None, None, 8, L), lambda b, g, c: (b, c, g, 0)),
                  chan, chan],
        out_specs=[grp, grp, pl.BlockSpec((None, None, None, RP, N), lambda b, g, c: (b, g, c, 0, 0))],
        out_shape=[jax.ShapeDtypeStruct((Bl, S, DI), F32), jax.ShapeDtypeStruct((Bl, S, DI), MXU_DTYPE),
                   jax.ShapeDtypeStruct((Bl, G, nc, RP, N), MXU_DTYPE)],
        scratch_shapes=[pltpu.VMEM((RP, N), F32)],
        compiler_params=_cp(("parallel", "parallel", "arbitrary")),
    )(xbc_a, xbc_a, xbc_a, proj, dtg4, acs4, acs_t4, dsk_exp, norm_g.reshape(1, DI))


def _scan_bwd(dyn, y, xbc_a, proj, hprev, dtg4, acs4, acs_t4, dsk_exp, norm_g, dproj, dm, name):
    Bl, S, DI, N, R, L, G = dm.Bl, dm.S, dm.DI, dm.N, dm.R, CHUNK, N_GROUPS
    RP = R * HEAD_DIM
    nc = S // L
    ob, ocm = DI // N, DI // N + G

    def body(dyn_ref, y_ref, z_ref, xs_ref, bm_ref, cm_ref, hp_ref, dtg_ref, acs_ref, acst_ref, dsk_ref, ng_ref,
             _, dz_ref, dxs_ref, dbm_ref, dcm_ref, da_ref, ddt_ref, ddsk_ref, dng_ref, dh_ref):
        b, c = pl.program_id(1), pl.program_id(2)

        @pl.when(c == 0)
        def _():
            dh_ref[...] = jnp.zeros_like(dh_ref)

        @pl.when((b == 0) & (c == 0))
        def _():
            ddsk_ref[...] = jnp.zeros_like(ddsk_ref)
            dng_ref[...] = jnp.zeros_like(dng_ref)

        row = lax.broadcasted_iota(jnp.int32, (L, L), 0)
        col = lax.broadcasted_iota(jnp.int32, (L, L), 1)
        causal, anti = row >= col, col >= row
        lane = lax.broadcasted_iota(jnp.int32, (L, LANES), 1)
        etb = (lax.shift_right_logical(lax.broadcasted_iota(jnp.int32, (RP, LANES), 0), 6)
               == lax.broadcasted_iota(jnp.int32, (RP, LANES), 1)).astype(MXU_DTYPE)

        dtg, acs, acs_t = dtg_ref[...], acs_ref[...], acst_ref[...]
        xs, z, y, dyn = xs_ref[...], z_ref[...], y_ref[...], dyn_ref[...]
        bmb, cmb = bm_ref[...].astype(MXU_DTYPE), cm_ref[...].astype(MXU_DTYPE)
        hpb = hp_ref[...]
        ng = ng_ref[...]

        sz = _sigmoid(z)
        siluz = z * sz
        yg = y * siluz
        rr = lax.rsqrt(jnp.mean(yg * yg, axis=-1, keepdims=True) + EPS)
        yhat = yg * rr
        dng_ref[...] += jnp.sum(dyn * yhat, axis=0, keepdims=True)
        dyhat = dyn * ng
        dyg = rr * (dyhat - yhat * jnp.mean(dyhat * yhat, axis=-1, keepdims=True))
        dy = dyg * siluz
        dz_ref[...] = (dyg * y * (sz * (1.0 + z * (1.0 - sz)))).astype(dz_ref.dtype)

        dxs = dy * dsk_ref[...]
        ddsk_ref[...] += jnp.sum(dy * xs, axis=0, keepdims=True)

        acs_last = acs[L - 1:L, :]
        dt_exp = _expand_cols(dtg, R, lane)
        ea_exp = _expand_cols(jnp.exp(acs), R, lane)
        ds_exp = _expand_cols(jnp.exp(acs_last - acs), R, lane)
        xdt = xs * dt_exp
        xb = xdt.astype(MXU_DTYPE)
        dyb = dy.astype(MXU_DTYPE)
        cd = jnp.exp(acs_last)
        cd_rows = _head_rows(jnp.exp(acs_t[:, L - 1:L]), R, N)

        q_ = _dg(cmb, hpb, NT)
        dq = dy * ea_exp
        dqb = dq.astype(MXU_DTYPE)
        dcm = _dg(dqb, hpb, NN)
        dh_yoff = _dg(dqb, cmb, TN)

        dhn = dh_ref[...]
        wprod = dhn * hpb.astype(F32)
        per_head = jnp.concatenate(
            [jnp.sum(wprod[HEAD_DIM * r:HEAD_DIM * (r + 1), :], axis=0, keepdims=True) for r in range(R)]
            + ([jnp.zeros((8 - R, N), F32)] if R < 8 else []), axis=0)
        dcd_col = jnp.sum(per_head, axis=1, keepdims=True)
        diag8 = lax.broadcasted_iota(jnp.int32, (8, LANES), 0) == lax.broadcasted_iota(jnp.int32, (8, LANES), 1)
        dcd_lane = jnp.sum(jnp.where(diag8, dcd_col, 0.0), axis=0, keepdims=True)
        d_a_last = dcd_lane * cd
        dh_ref[...] = dhn * cd_rows + dh_yoff
        dhnb = dhn.astype(MXU_DTYPE)

        e_ = _dg(bmb, dhnb, NT)
        dxdt = ds_exp * e_
        xds = xdt * ds_exp
        dbm = _dg(xds.astype(MXU_DTYPE), dhnb, NN)

        sg = _dg(cmb, bmb, NT)
        sg_t = _dg(bmb, cmb, NT)
        dsg = jnp.zeros((L, L), F32)
        dsg_t = jnp.zeros((L, L), F32)
        d_a = jnp.zeros((L, LANES), F32)
        parts = []
        for q in range(R // 2):
            x2 = xb[:, LANES * q:LANES * (q + 1)]
            dy2 = dyb[:, LANES * q:LANES * (q + 1)]
            dxs2 = []
            for hh, r in enumerate((2 * q, 2 * q + 1)):
                mine = (lane < HEAD_DIM) if hh == 0 else (lane >= HEAD_DIM)
                diff = acs[:, r:r + 1] - acs_t[r:r + 1, :]
                dec = jnp.exp(jnp.where(causal, diff, -1e30))
                dec_t = jnp.exp(jnp.where(anti, -diff, -1e30))
                dy2m = jnp.where(mine, dy2, jnp.zeros_like(dy2))
                dm_ = _dg(dy2m, x2, NT)
                dm_t = _dg(x2, dy2m, NT)
                m_t = sg_t * dec_t
                da_col = jnp.sum(dm_ * (sg * dec) - dm_t * m_t, axis=1, keepdims=True)
                d_a = d_a + jnp.where(lane == r, da_col, 0.0)
                dsg = dsg + dm_ * dec
                dsg_t = dsg_t + dm_t * dec_t
                dxs2.append(_dg(m_t.astype(MXU_DTYPE), dy2, NN))
            parts.append(jnp.where(lane < HEAD_DIM, dxs2[0], dxs2[1]))
        dxdt = dxdt + jnp.concatenate(parts, axis=1)
        dcm_ref[...] = dcm + _dg(dsg.astype(MXU_DTYPE), bmb, NN)
        dbm_ref[...] = dbm + _dg(dsg_t.astype(MXU_DTYPE), cmb, NN)
        dxs_ref[...] = dxs + dxdt * dt_exp

        hs = _two_term_dot(jnp.concatenate([dq * q_ - xds * e_, xds * e_, dxdt * xs], axis=0), etb)
        t2 = hs[L:2 * L, :]
        rowl = lax.broadcasted_iota(jnp.int32, (L, LANES), 0)
        d_a_last = d_a_last + jnp.sum(t2, axis=0, keepdims=True)
        da_ref[...] = d_a + hs[0:L, :] + jnp.where(rowl == L - 1, d_a_last, 0.0)
        ddt_ref[...] = hs[2 * L:3 * L, :]

    oz = dm.OZ // RP
    grp = pl.BlockSpec((None, L, RP), lambda g, b, c: (b, nc - 1 - c, g))
    zspec = pl.BlockSpec((None, L, RP), lambda g, b, c: (b, nc - 1 - c, oz + g))
    lanes = pl.BlockSpec((None, L, LANES), lambda g, b, c: (b, nc - 1 - c, g))
    bspec = pl.BlockSpec((None, L, N), lambda g, b, c: (b, nc - 1 - c, ob + g))
    cspec = pl.BlockSpec((None, L, N), lambda g, b, c: (b, nc - 1 - c, ocm + g))
    gn = pl.BlockSpec((None, L, N), lambda g, b, c: (b, nc - 1 - c, g))
    chan = pl.BlockSpec((1, RP), lambda g, b, c: (0, g))
    wide = jax.ShapeDtypeStruct((Bl, S, G * LANES), F32)
    return pl.pallas_call(
        body, name=name, grid=(G, Bl, nc),
        in_specs=[grp, grp, zspec, grp, bspec, cspec,
                  pl.BlockSpec((None, None, None, RP, N), lambda g, b, c: (b, g, nc - 1 - c, 0, 0)),
                  lanes, lanes, pl.BlockSpec((None, None, 8, L), lambda g, b, c: (b, nc - 1 - c, g, 0)),
                  chan, chan, _ANY],
        out_specs=[zspec, grp, gn, gn, lanes, lanes, chan, chan],
        out_shape=[jax.ShapeDtypeStruct(dproj.shape, dproj.dtype), jax.ShapeDtypeStruct((Bl, S, DI), F32),
                   jax.ShapeDtypeStruct((Bl, S, G * N), F32), jax.ShapeDtypeStruct((Bl, S, G * N), F32),
                   wide, wide, jax.ShapeDtypeStruct((1, DI), F32), jax.ShapeDtypeStruct((1, DI), F32)],
        input_output_aliases={12: 0},
        scratch_shapes=[pltpu.VMEM((RP, N), F32)],
        compiler_params=_cp(("arbitrary", "arbitrary", "arbitrary")),
    )(dyn, y, proj, xbc_a, xbc_a, xbc_a, hprev, dtg4, acs4, acs_t4, dsk_exp, norm_g.reshape(1, DI), dproj)


def _adam_update(w, m, v, g):
    c1 = 1.0 - ADAM_B1 ** ADAM_STEP
    c2 = 1.0 - ADAM_B2 ** ADAM_STEP
    nm = ADAM_B1 * m + (1.0 - ADAM_B1) * g
    nv = ADAM_B2 * v + (1.0 - ADAM_B2) * (g * g)
    return nm, nv, -ADAM_LR * ((nm / c1) / (jnp.sqrt(nv / c2) + ADAM_EPS) + ADAM_WD * w)


def _adamw(w, m, v, g, name, tr=128):
    rows, cols = w.shape
    tr = _tile(rows, tr, 8)

    def body(w_ref, m_ref, v_ref, ga_ref, g_ref, d_ref, nm_ref, nv_ref):
        g = ga_ref[...]
        g_ref[...] = g
        nm_ref[...], nv_ref[...], d_ref[...] = _adam_update(w_ref[...], m_ref[...], v_ref[...], g)

    blk = pl.BlockSpec((tr, cols), lambda i: (i, 0))
    shp = jax.ShapeDtypeStruct((rows, cols), F32)
    return pl.pallas_call(
        body, name=name, grid=(rows // tr,), in_specs=[blk] * 4, out_specs=[blk] * 4,
        out_shape=[shp] * 4, compiler_params=_cp(("parallel",)),
    )(w, m, v, g)


def _adamw_layers(w, m, v, g_mine, g_theirs, core, name, tr=128):
    _, rows, cols = w.shape
    tr = _tile(rows, tr, 8)

    def body(c_ref, w_ref, m_ref, v_ref, ga_ref, gb_ref, g_ref, d_ref, nm_ref, nv_ref):
        g = jnp.where(pl.program_id(0) == c_ref[0], ga_ref[...], gb_ref[...])
        g_ref[...] = g
        nm_ref[...], nv_ref[...], d_ref[...] = _adam_update(w_ref[...], m_ref[...], v_ref[...], g)

    lay = pl.BlockSpec((None, tr, cols), lambda l, i, c_ref: (l, i, 0))
    one = pl.BlockSpec((tr, cols), lambda l, i, c_ref: (i, 0))
    shp = jax.ShapeDtypeStruct(w.shape, F32)
    return pl.pallas_call(
        body, name=name,
        grid_spec=pltpu.PrefetchScalarGridSpec(num_scalar_prefetch=1, grid=(2, rows // tr),
                                               in_specs=[lay, lay, lay, one, one], out_specs=[lay] * 4),
        out_shape=[shp] * 4, compiler_params=_cp(("parallel", "parallel")),
    )(core, w, m, v, g_mine, g_theirs)


def _sum_slots(buf, name, tr=256):
    n, rows, cols = buf.shape
    tr = _tile(rows, tr, 8)

    def body(b_ref, o_ref):
        acc = b_ref[0].astype(F32)
        for k in range(1, n):
            acc = acc + b_ref[k].astype(F32)
        o_ref[...] = acc

    return pl.pallas_call(
        body, name=name, grid=(rows // tr,),
        in_specs=[pl.BlockSpec((n, tr, cols), lambda i: (0, i, 0))],
        out_specs=pl.BlockSpec((tr, cols), lambda i: (i, 0)),
        out_shape=jax.ShapeDtypeStruct((rows, cols), F32), compiler_params=_cp(("parallel",)),
    )(buf)


_ANY = pl.BlockSpec(memory_space=pl.ANY)


def _exchange_chips(src, per_dest, name):
    rows, cols = src.shape[-2:]

    def body(in_ref, out_ref, send_sems, recv_sems, local_sem):
        x, y, c = lax.axis_index("x"), lax.axis_index("y"), lax.axis_index("c")
        me = 2 * x + y
        chips = [(1 - x, y), (x, 1 - y), (1 - x, 1 - y)]

        def block(j):
            return in_ref.at[j] if per_dest else in_ref

        mine = pltpu.make_async_copy(block(me), out_ref.at[me], local_sem)
        mine.start()
        sends = []
        for k, (px, py) in enumerate(chips):
            cp = pltpu.make_async_remote_copy(
                src_ref=block(2 * px + py), dst_ref=out_ref.at[me], send_sem=send_sems.at[k],
                recv_sem=recv_sems.at[k], device_id=(px, py, c), device_id_type=MESH)
            cp.start()
            sends.append(cp)
        for k, (px, py) in enumerate(chips):
            pltpu.make_async_remote_copy(
                src_ref=block(me), dst_ref=out_ref.at[2 * px + py], send_sem=send_sems.at[k],
                recv_sem=recv_sems.at[k], device_id=(px, py, c), device_id_type=MESH).wait_recv()
        for cp in sends:
            cp.wait_send()
        mine.wait()

    return pl.pallas_call(
        body, name=name, in_specs=[_ANY], out_specs=_ANY,
        out_shape=jax.ShapeDtypeStruct((N_CHIPS, rows, cols), src.dtype),
        scratch_shapes=[pltpu.SemaphoreType.DMA((3,)), pltpu.SemaphoreType.DMA((3,)), pltpu.SemaphoreType.DMA(())],
    )(src)


def _shard_window(ref, kind, j, lead=()):
    if kind == "slots":
        return ref.at[(j,) + lead]
    r, c = ref.shape[-2] // (N_CHIPS if kind == "rows" else 1), ref.shape[-1] // (N_CHIPS if kind == "cols" else 1)
    full = tuple(slice(None) for _ in range(len(ref.shape) - 2 - len(lead)))
    if kind == "rows":
        return ref.at[lead + full + (pl.ds(pl.multiple_of(j * r, 16), r), slice(None))]
    return ref.at[lead + full + (slice(None), pl.ds(pl.multiple_of(j * c, LANES), c))]


def _gather_kind(shard, axis):
    if axis == 1:
        return "rows"
    return "cols" if shard.shape[2] % LANES == 0 else "slots"


def _gather_weights(shards, axes, name):
    kinds = ["rows" if ax == 1 else "slots" for ax in axes]
    nw = len(shards)

    def out_shape(s, kind):
        d, r, c = s.shape
        shp = {"rows": (d, N_CHIPS * r, c), "cols": (d, r, N_CHIPS * c), "slots": (N_CHIPS, d, r, c)}[kind]
        return jax.ShapeDtypeStruct(shp, s.dtype)

    assert all(s.shape[0] == 2 for s in shards)

    def body(*refs):
        ins, outs = refs[:nw], refs[nw:2 * nw]
        ici_send, ici_recv, d2d_send, d2d_recv, local_sems = refs[2 * nw:]
        x, y, c = lax.axis_index("x"), lax.axis_index("y"), lax.axis_index("c")
        me = 2 * x + y
        chips = [(1 - x, y), (x, 1 - y), (1 - x, 1 - y)]
        sends = []
        for i in range(nw):
            if kinds[i] == "rows":
                own = pltpu.make_async_copy(ins[i], _shard_window(outs[i], kinds[i], me), local_sems.at[i])
                own.start()
                sends.append((own, False))
        for i in range(nw):
            for k, (px, py) in enumerate(chips):
                cp = pltpu.make_async_remote_copy(
                    src_ref=ins[i].at[c], dst_ref=_shard_window(outs[i], kinds[i], me, (c,)),
                    send_sem=ici_send.at[3 * i + k], recv_sem=ici_recv.at[3 * i + k],
                    device_id=(px, py, c), device_id_type=MESH)
                cp.start()
                sends.append((cp, True))
        for i in range(nw):
            for k, (px, py) in enumerate(chips):
                win = _shard_window(outs[i], kinds[i], 2 * px + py, (c,))
                pltpu.make_async_remote_copy(
                    src_ref=ins[i].at[c], dst_ref=win, send_sem=ici_send.at[3 * i + k], recv_sem=ici_recv.at[3 * i + k],
                    device_id=(px, py, c), device_id_type=MESH).wait_recv()
                fw = pltpu.make_async_remote_copy(
                    src_ref=win, dst_ref=win, send_sem=d2d_send.at[3 * i + k], recv_sem=d2d_recv.at[3 * i + k],
                    device_id=(x, y, 1 - c), device_id_type=MESH)
                fw.start()
                sends.append((fw, True))
        for i in range(nw):
            for k, (px, py) in enumerate(chips):
                win = _shard_window(outs[i], kinds[i], 2 * px + py, (1 - c,))
                pltpu.make_async_remote_copy(
                    src_ref=win, dst_ref=win, send_sem=d2d_send.at[3 * i + k], recv_sem=d2d_recv.at[3 * i + k],
                    device_id=(x, y, 1 - c), device_id_type=MESH).wait_recv()
        for cp, remote in sends:
            cp.wait_send() if remote else cp.wait()

    outs = pl.pallas_call(
        body, name=name, in_specs=[_ANY] * nw, out_specs=[_ANY] * nw,
        out_shape=[out_shape(s, k) for s, k in zip(shards, kinds)],
        scratch_shapes=[pltpu.SemaphoreType.DMA((3 * nw,)), pltpu.SemaphoreType.DMA((3 * nw,)),
                        pltpu.SemaphoreType.DMA((3 * nw,)), pltpu.SemaphoreType.DMA((3 * nw,)),
                        pltpu.SemaphoreType.DMA((nw,))],
    )(*shards)
    me = 2 * lax.axis_index("x") + lax.axis_index("y")
    return [jnp.concatenate([jnp.where(me == j, s, o[j]) for j in range(N_CHIPS)], axis=2) if k == "slots" else o
            for s, o, k in zip(shards, outs, kinds)]


def _swap_other_layer(gst, name):
    nv = len(gst)

    def body(*refs):
        ins, outs, send_sems, recv_sems = refs[:nv], refs[nv:2 * nv], refs[2 * nv], refs[2 * nv + 1]
        x, y, c = lax.axis_index("x"), lax.axis_index("y"), lax.axis_index("c")
        cps = [pltpu.make_async_remote_copy(src_ref=ins[i].at[1 - c], dst_ref=outs[i], send_sem=send_sems.at[i],
                                            recv_sem=recv_sems.at[i], device_id=(x, y, 1 - c), device_id_type=MESH)
               for i in range(nv)]
        for cp in cps:
            cp.start()
        for cp in cps:
            cp.wait()

    return pl.pallas_call(
        body, name=name, in_specs=[_ANY] * nv, out_specs=[_ANY] * nv,
        out_shape=[jax.ShapeDtypeStruct(v.shape[1:], v.dtype) for v in gst],
        scratch_shapes=[pltpu.SemaphoreType.DMA((nv,)), pltpu.SemaphoreType.DMA((nv,))],
    )(*gst)


def _pair_sum(g, other, core, name, tr=256):
    _, rows, cols = g.shape
    tr = _tile(rows, tr, 16)

    def body(c_ref, g_ref, o_ref, s_ref):
        s_ref[...] = (g_ref[...].astype(F32) + o_ref[...].astype(F32)).astype(s_ref.dtype)

    blk = pl.BlockSpec((tr, cols), lambda i, c_ref: (i, 0))
    return pl.pallas_call(
        body, name=name,
        grid_spec=pltpu.PrefetchScalarGridSpec(
            num_scalar_prefetch=1, grid=(rows // tr,),
            in_specs=[pl.BlockSpec((None, tr, cols), lambda i, c_ref: (c_ref[0], i, 0)), blk], out_specs=blk),
        out_shape=jax.ShapeDtypeStruct((rows, cols), g.dtype), compiler_params=_cp(("parallel",)),
    )(core, g, other)


def _scatter_layer(parts, kinds, name):
    nw = len(parts)

    def shard_shape(g, kind):
        return g.shape[-2] // (N_CHIPS if kind == "rows" else 1), g.shape[-1] // (N_CHIPS if kind == "cols" else 1)

    def body(*refs):
        ins, outs = refs[:nw], refs[nw:2 * nw]
        send_sems, recv_sems, local_sems = refs[2 * nw:]
        x, y, c = lax.axis_index("x"), lax.axis_index("y"), lax.axis_index("c")
        me = 2 * x + y
        chips = [(1 - x, y), (x, 1 - y), (1 - x, 1 - y)]
        sends = []
        for i in range(nw):
            own = pltpu.make_async_copy(_shard_window(ins[i], kinds[i], me), outs[i].at[me], local_sems.at[i])
            own.start()
            sends.append((own, False))
            for k, (px, py) in enumerate(chips):
                cp = pltpu.make_async_remote_copy(
                    src_ref=_shard_window(ins[i], kinds[i], 2 * px + py), dst_ref=outs[i].at[me],
                    send_sem=send_sems.at[3 * i + k], recv_sem=recv_sems.at[3 * i + k],
                    device_id=(px, py, c), device_id_type=MESH)
                cp.start()
                sends.append((cp, True))
        for i in range(nw):
            for k, (px, py) in enumerate(chips):
                pltpu.make_async_remote_copy(
                    src_ref=_shard_window(ins[i], kinds[i], me), dst_ref=outs[i].at[2 * px + py],
                    send_sem=send_sems.at[3 * i + k], recv_sem=recv_sems.at[3 * i + k],
                    device_id=(px, py, c), device_id_type=MESH).wait_recv()
        for cp, remote in sends:
            cp.wait_send() if remote else cp.wait()

    return pl.pallas_call(
        body, name=name, in_specs=[_ANY] * nw, out_specs=[_ANY] * nw,
        out_shape=[jax.ShapeDtypeStruct((N_CHIPS,) + shard_shape(g, k), g.dtype) for g, k in zip(parts, kinds)],
        scratch_shapes=[pltpu.SemaphoreType.DMA((3 * nw,)), pltpu.SemaphoreType.DMA((3 * nw,)),
                        pltpu.SemaphoreType.DMA((nw,))],
    )(*parts)


def _sibling_swap(vs, name):
    nv = len(vs)

    def body(*refs):
        ins, outs, send_sems, recv_sems = refs[:nv], refs[nv:2 * nv], refs[2 * nv], refs[2 * nv + 1]
        x, y, c = lax.axis_index("x"), lax.axis_index("y"), lax.axis_index("c")
        cps = [pltpu.make_async_remote_copy(src_ref=ins[i], dst_ref=outs[i], send_sem=send_sems.at[i],
                                            recv_sem=recv_sems.at[i], device_id=(x, y, 1 - c), device_id_type=MESH)
               for i in range(nv)]
        for cp in cps:
            cp.start()
        for cp in cps:
            cp.wait()

    return pl.pallas_call(
        body, name=name, in_specs=[_ANY] * nv, out_specs=[_ANY] * nv,
        out_shape=[jax.ShapeDtypeStruct(v.shape, v.dtype) for v in vs],
        scratch_shapes=[pltpu.SemaphoreType.DMA((nv,)), pltpu.SemaphoreType.DMA((nv,))],
    )(*vs)


def _allgather_all(v, name):
    rows, cols = v.shape

    def body(in_ref, out_ref, send_sems, recv_sems, local_sem):
        x, y, c = lax.axis_index("x"), lax.axis_index("y"), lax.axis_index("c")
        me = 4 * x + 2 * y + c
        peers = []
        for k in range(1, N_DEV):
            peers.append(((1 - x) if k & 4 else x, (1 - y) if k & 2 else y, (1 - c) if k & 1 else c))
        mine = pltpu.make_async_copy(in_ref, out_ref.at[me], local_sem)
        mine.start()
        sends = []
        for k, peer in enumerate(peers):
            cp = pltpu.make_async_remote_copy(src_ref=in_ref, dst_ref=out_ref.at[me], send_sem=send_sems.at[k],
                                              recv_sem=recv_sems.at[k], device_id=peer, device_id_type=MESH)
            cp.start()
            sends.append(cp)
        for k, (px, py, pc) in enumerate(peers):
            pltpu.make_async_remote_copy(src_ref=in_ref, dst_ref=out_ref.at[4 * px + 2 * py + pc],
                                         send_sem=send_sems.at[k], recv_sem=recv_sems.at[k],
                                         device_id=(px, py, pc), device_id_type=MESH).wait_recv()
        for cp in sends:
            cp.wait_send()
        mine.wait()

    return pl.pallas_call(
        body, name=name, in_specs=[_ANY], out_specs=_ANY,
        out_shape=jax.ShapeDtypeStruct((N_DEV, rows, cols), v.dtype),
        scratch_shapes=[pltpu.SemaphoreType.DMA((N_DEV - 1,)), pltpu.SemaphoreType.DMA((N_DEV - 1,)),
                        pltpu.SemaphoreType.DMA(())],
    )(v)


def _pack(arrs, dtype, width, row_mult):
    flat = jnp.concatenate([a.reshape(-1).astype(dtype) for a in arrs])
    unit = width * row_mult
    total = -(-flat.shape[0] // unit) * unit
    return jnp.pad(flat, (0, total - flat.shape[0])).reshape(-1, width)


def _unpack(buf, shapes):
    flat = buf.reshape(-1)
    out, off = [], 0
    for shp in shapes:
        n = 1
        for d in shp:
            n *= d
        out.append(flat[off:off + n].reshape(shp))
        off += n
    return out


class _Dims:
    pass


def _dims(x, ssd_dt_bias, ssd_norm_g, ssd_conv_b, ffn_conv_b):
    dm = _Dims()
    dm.Bl, dm.S, dm.D = x.shape
    dm.H, dm.DI, dm.CD = ssd_dt_bias.shape[-1], ssd_norm_g.shape[-1], ssd_conv_b.shape[-1]
    dm.N = (dm.CD - dm.DI) // (2 * N_GROUPS)
    dm.R = dm.H // N_GROUPS
    dm.DFF = ffn_conv_b.shape[-1] // 2
    D = dm.D
    dm.OB, dm.OC, dm.OH, dm.OX = 0, D, 2 * D, 3 * D
    dm.OZ = dm.OX + dm.CD
    dm.OG = -(-(dm.OZ + dm.DI) // (2 * D)) * (2 * D)
    dm.GPAD = dm.OG - (dm.OZ + dm.DI)
    dm.ODT = dm.OG + 2 * D
    dm.NP = dm.ODT + LANES
    assert dm.OZ % (dm.R * HEAD_DIM) == 0
    assert dm.DI // dm.H == HEAD_DIM and dm.N == LANES and dm.R % 2 == 0 and dm.S % CHUNK == 0 and dm.H <= LANES
    return dm


def _permute_w_in(w, dm):
    o, sc = dm.DI + dm.CD, dm.DI + dm.CD + dm.H
    zeros = lambda n: jnp.zeros((w.shape[0], n), w.dtype)
    return jnp.concatenate([w[:, sc:sc + 3 * dm.D], w[:, dm.DI:o], w[:, :dm.DI], zeros(dm.GPAD), w[:, sc + 3 * dm.D:],
                            w[:, o:o + dm.H], zeros(LANES - dm.H)], axis=1)


def _unpermute_w_in(dw, dm):
    return jnp.concatenate([dw[..., dm.OZ:dm.OZ + dm.DI], dw[..., dm.OX:dm.OX + dm.CD], dw[..., dm.ODT:dm.ODT + dm.H],
                            dw[..., :3 * dm.D], dw[..., dm.OG:dm.OG + 2 * dm.D]], axis=-1)


def _lane_pad(v):
    return jnp.pad(v.reshape(1, -1).astype(F32), ((0, 0), (0, LANES - v.shape[-1])))


def _pad8(w):
    return jnp.pad(w.astype(F32), ((0, 8 - w.shape[0]), (0, 0)))


def _head_select(dm):
    j = jnp.arange(LANES)[None, :, None]
    r = jnp.arange(LANES)[None, None, :]
    g = jnp.arange(N_GROUPS)[:, None, None]
    sel = ((j == dm.R * g + r) & (r < dm.R)).astype(F32)
    selcat = jnp.transpose(sel, (1, 0, 2)).reshape(LANES, N_GROUPS * LANES)
    selbig = jnp.transpose(sel[:, :, :8], (0, 2, 1)).reshape(N_GROUPS * 8, LANES)
    return selcat, selbig


def _mix_fwd(dm, h, w, sp, sel, tag):
    Bl, S, D = dm.Bl, dm.S, dm.D
    T = Bl * S
    proj = _matmul(h.reshape(T, D), w["w_in_p"], "NN", F32, tag + "_in_proj", tn=1152).reshape(Bl, S, dm.NP)
    xbc_a = _ssd_conv_fwd(proj, sp["ssd_conv_w8"], sp["ssd_conv_b"], dm.OX, dm.CD, tag + "_ssd_conv")
    dtg, acs, acs_t = _ssd_prep(proj, sp["dt_bias"], sp["a_log"], sel[0], sel[1], dm, tag + "_ssd_prep")
    y, yn, hprev = _scan_fwd(xbc_a, proj, dtg, acs, acs_t, sp["dsk_exp"], sp["ssd_norm_g"], dm, tag + "_ssd_scan")
    y_ssd = _matmul(yn.reshape(T, dm.DI), w["w_ssd_out"], "NN", F32, tag + "_ssd_out", tk=2048).reshape(Bl, S, D)
    s = _sc_conv_fwd(proj, sp["sc_conv_w8"], (dm.OB, dm.OC, dm.OH), D, tag + "_sc_conv")
    y_sc = _matmul(s.reshape(T, D), w["w_sc_out"], "NN", F32, tag + "_sc_out", tk=1024).reshape(Bl, S, D)
    mixin = _merge_fwd(proj, y_ssd, y_sc, dm.OG, D, tag + "_merge")
    mix = _matmul(mixin.reshape(T, D), w["w_o"], "NN", F32, tag + "_o", tk=1024).reshape(Bl, S, D)
    return mix, (h, proj, xbc_a, y, yn, hprev, y_ssd, y_sc, s, mixin, dtg, acs, acs_t)


def _mix_bwd(dm, dmix, saved, w, sp, sel, tag, gbig, slab):
    Bl, S, D = dm.Bl, dm.S, dm.D
    T = Bl * S
    h, proj, xbc_a, y, yn, hprev, y_ssd, y_sc, s, mixin, dtg, acs, acs_t = saved
    dmix2 = dmix.reshape(T, D)
    g = {}
    gbig["w_o"] = _matmul(mixin.reshape(T, D), dmix2, "TN", WIRE_DTYPE, tag + "_dw_o", slab=(gbig.get("w_o"),) + slab)
    dmixin = _matmul(dmix2, w["w_o"], "NT", F32, tag + "_d_o", tk=1024).reshape(Bl, S, D)
    dproj0 = jnp.zeros((Bl, S, dm.NP), MXU_DTYPE) if dm.GPAD else None
    dy_ssd, dy_sc, dproj = _merge_bwd(dmixin, proj, y_ssd, y_sc, dm.OG, D, dproj0, tag + "_merge_bwd")
    gbig["w_sc_out"] = _matmul(s.reshape(T, D), dy_sc.reshape(T, D), "TN", WIRE_DTYPE, tag + "_dw_sc_out",
                               slab=(gbig.get("w_sc_out"),) + slab)
    ds = _matmul(dy_sc.reshape(T, D), w["w_sc_out"], "NT", F32, tag + "_d_sc_out", tk=1024).reshape(Bl, S, D)
    dproj, dscw = _sc_conv_bwd(ds, proj, sp["sc_conv_w8"], (dm.OB, dm.OC, dm.OH), D, dproj, tag + "_sc_conv_bwd")
    g["sc_conv_w"] = dscw[:3]
    gbig["w_ssd_out"] = _matmul(yn.reshape(T, dm.DI), dy_ssd.reshape(T, D), "TN", WIRE_DTYPE, tag + "_dw_ssd_out",
                                slab=(gbig.get("w_ssd_out"),) + slab)
    dyn = _matmul(dy_ssd.reshape(T, D), w["w_ssd_out"], "NT", F32, tag + "_d_ssd_out", tk=1024).reshape(Bl, S, dm.DI)
    dproj, dxs, dbm, dcm, d_a, ddt, ddsk, dng = _scan_bwd(dyn, y, xbc_a, proj, hprev, dtg, acs, acs_t, sp["dsk_exp"],
                                                          sp["ssd_norm_g"], dproj, dm, tag + "_ssd_scan_bwd")
    dproj, dpar = _ssd_post(d_a, ddt, dtg, proj, sp["dt_bias"], sp["a_log"], sel[0], dproj, dm, tag + "_ssd_post")
    g["ssd_dt_bias"], g["ssd_a_log"] = dpar[0, :dm.H], dpar[1, :dm.H]
    g["ssd_d"] = jnp.sum(ddsk.reshape(dm.H, HEAD_DIM), axis=-1)
    g["ssd_norm_g"] = dng[0]
    dxa = jnp.concatenate([dxs, dbm, dcm], axis=-1)
    dproj, dcw, dcb = _ssd_conv_bwd(dxa, proj, sp["ssd_conv_w8"], sp["ssd_conv_b"], dm.OX, dm.CD, dproj,
                                    tag + "_ssd_conv_bwd")
    g["ssd_conv_w"], g["ssd_conv_b"] = dcw[:4], dcb[0]
    dproj = dproj.reshape(T, dm.NP)
    gbig["w_in_p"] = _matmul(h.reshape(T, D), dproj, "TN", WIRE_DTYPE, tag + "_dw_in", tn=1152,
                             slab=(gbig.get("w_in_p"),) + slab)
    dh = _matmul(dproj, w["w_in_p"], "NT", F32, tag + "_d_in", tk=1152).reshape(Bl, S, D)
    return dh, g


def _ffn_fwd(dm, h, w, sp, tag):
    Bl, S, D = dm.Bl, dm.S, dm.D
    T = Bl * S
    up = _matmul(h.reshape(T, D), w["w_up"], "NN", F32, tag + "_up", tn=1408).reshape(Bl, S, 2 * dm.DFF)
    a = _ffn_conv_fwd(up, sp["ffn_conv_w8"], sp["ffn_conv_b"], dm.DFF, tag + "_ffn_conv")
    f = _matmul(a.reshape(T, dm.DFF), w["w_down"], "NN", F32, tag + "_down", tk=2816).reshape(Bl, S, D)
    return f, (h, up, a)


def _ffn_bwd(dm, df, saved, w, sp, tag, gbig, slab):
    Bl, S, D = dm.Bl, dm.S, dm.D
    T = Bl * S
    h, up, a = saved
    df2 = df.reshape(T, D)
    g = {}
    gbig["w_down"] = _matmul(a.reshape(T, dm.DFF), df2, "TN", WIRE_DTYPE, tag + "_dw_down", tm=1408,
                             slab=(gbig.get("w_down"),) + slab)
    da = _matmul(df2, w["w_down"], "NT", F32, tag + "_d_down", tn=1408).reshape(Bl, S, dm.DFF)
    dg, dv, dcw, dcb = _ffn_conv_bwd(da, up, sp["ffn_conv_w8"], sp["ffn_conv_b"], dm.DFF, tag + "_ffn_conv_bwd")
    g["ffn_conv_w"], g["ffn_conv_b"] = dcw[:3], dcb[0]
    dup = jnp.concatenate([dg, dv], axis=-1).reshape(T, 2 * dm.DFF)
    gbig["w_up"] = _matmul(h.reshape(T, D), dup, "TN", WIRE_DTYPE, tag + "_dw_up", tn=1408,
                           slab=(gbig.get("w_up"),) + slab)
    dh = _matmul(dup, w["w_up"], "NT", F32, tag + "_d_up", tk=1408).reshape(Bl, S, D)
    return dh, g


def _local_step(dm, x, c, target, wfull, small):
    Bl, S, D = dm.Bl, dm.S, dm.D
    depth = len(wfull)
    sel = _head_select(dm)
    c16 = jnp.pad(c.astype(F32), ((0, 16 - Bl), (0, 0)))
    sps, mods, acts = [], [], []
    for l in range(depth):
        sm = small[l]
        sps.append(dict(
            ssd_conv_w8=_pad8(sm["ssd_conv_w"]), ssd_conv_b=sm["ssd_conv_b"], dt_bias=_lane_pad(sm["ssd_dt_bias"]),
            a_log=_lane_pad(sm["ssd_a_log"]), dsk_exp=jnp.repeat(sm["ssd_d"].astype(F32), HEAD_DIM).reshape(1, dm.DI),
            ssd_norm_g=sm["ssd_norm_g"],
            sc_conv_w8=_pad8(sm["sc_conv_w"]), ffn_conv_w8=_pad8(sm["ffn_conv_w"]), ffn_conv_b=sm["ffn_conv_b"]))
        act, mod = _ada_fwd(c16, wfull[l]["ada_w"], sm["ada_b"], f"l{l}_ada")
        acts.append(act)
        mods.append(jnp.pad(mod[:Bl].reshape(Bl, 6, D), ((0, 0), (0, 2), (0, 0))))

    def sub(i):
        l, ffn = i // 2, i % 2
        sm = small[l]
        return dict(l=l, ffn=ffn, pre_g=sm["ffn_pre_g" if ffn else "mix_pre_g"],
                    post_g=sm["ffn_post_g" if ffn else "mix_post_g"], mod=mods[l], row=3 * ffn,
                    tag=f"l{l}_{'ffn' if ffn else 'mix'}")

    nsub = 2 * depth
    subs = [sub(i) for i in range(nsub)]
    xs, fs, saves = [x], [], []
    h = _norm_mod_fwd(x, subs[0]["pre_g"], subs[0]["mod"], subs[0]["row"], "l0_mix_pre_norm")
    for i, sb in enumerate(subs):
        l = sb["l"]
        if sb["ffn"]:
            f, sv = _ffn_fwd(dm, h, wfull[l], sps[l], sb["tag"])
        else:
            f, sv = _mix_fwd(dm, h, wfull[l], sps[l], sel, sb["tag"])
        nxt = None
        if i + 1 < nsub:
            nb = subs[i + 1]
            nxt = (nb["pre_g"], nb["mod"], nb["row"])
        xn, h = _post_norm_fwd(xs[-1], f, sb["post_g"], sb["mod"], sb["row"] + 2, nxt, sb["tag"] + "_post_norm")
        xs.append(xn)
        fs.append(f)
        saves.append(sv)

    dy, loss = _loss_fwd_bwd(xs[-1], target, "loss")

    grads = [dict() for _ in range(depth)]
    gbig = {}
    dmod = [[None] * 6 for _ in range(depth)]
    dx, dh = dy, None
    for i in reversed(range(nsub)):
        sb = subs[i]
        l = sb["l"]
        nxt = None
        if i + 1 < nsub:
            nb = subs[i + 1]
            nxt = (dh, xs[i + 1], nb["pre_g"], nb["mod"], nb["row"])
        dx, df, pb, shg = _norm_bwd(dx, nxt, (fs[i], sb["post_g"], sb["mod"], sb["row"] + 2), sb["tag"] + "_post_norm_bwd")
        if nxt is not None:
            nb = subs[i + 1]
            dmod[nb["l"]][nb["row"]], dmod[nb["l"]][nb["row"] + 1] = pb[:, 0], pb[:, 1]
            grads[nb["l"]]["ffn_pre_g" if nb["ffn"] else "mix_pre_g"] = shg[0]
        dmod[l][sb["row"] + 2] = pb[:, 2]
        grads[l]["ffn_post_g" if sb["ffn"] else "mix_post_g"] = shg[1]
        if sb["ffn"]:
            dh, g = _ffn_bwd(dm, df, saves[i], wfull[l], sps[l], sb["tag"], gbig, (l, depth))
        else:
            dh, g = _mix_bwd(dm, df, saves[i], wfull[l], sps[l], sel, sb["tag"], gbig, (l, depth))
        grads[l].update(g)
    sb = subs[0]
    grad_x, _, pb, shg = _norm_bwd(dx, (dh, xs[0], sb["pre_g"], sb["mod"], sb["row"]), None, "l0_mix_pre_norm_bwd")
    dmod[0][0], dmod[0][1] = pb[:, 0], pb[:, 1]
    grads[0]["mix_pre_g"] = shg[0]

    for l in range(depth):
        dm6 = jnp.concatenate(dmod[l], axis=-1)
        grads[l]["ada_b"] = jnp.sum(dm6, axis=0)
        dm16 = jnp.pad(dm6, ((0, 16 - Bl), (0, 0))).astype(MXU_DTYPE)
        gbig["ada_w"] = _matmul(acts[l], dm16, "TN", WIRE_DTYPE, f"l{l}_dw_ada", slab=(gbig.get("ada_w"), l, depth))
    return loss, grad_x, grads, gbig


_WEIGHTS = ("ada_w", "ada_b", "mix_pre_g", "mix_post_g", "w_in", "ssd_conv_w", "ssd_conv_b", "ssd_dt_bias",
            "ssd_a_log", "ssd_d", "ssd_norm_g", "w_ssd_out", "sc_conv_w", "w_sc_out", "w_o", "ffn_pre_g",
            "ffn_post_g", "w_up", "ffn_conv_w", "ffn_conv_b", "w_down")
_INPUTS = ("x", "c") + _WEIGHTS + ("loss_target",) + tuple("m_" + n for n in _WEIGHTS) + tuple("v_" + n for n in _WEIGHTS)
_BIG = {"ada_w": 2, "w_in": 2, "w_ssd_out": 1, "w_sc_out": 1, "w_o": 1, "w_up": 2, "w_down": 1}
_CONV = ("ssd_conv_w", "sc_conv_w", "ffn_conv_w")
_SMALL = tuple(n for n in _WEIGHTS if n not in _BIG)


def _step(a):
    x, c, target = a["x"], a["c"], a["loss_target"]
    depth = a["ada_w"].shape[0]
    dm = _dims(x, a["ssd_dt_bias"], a["ssd_norm_g"], a["ssd_conv_b"], a["ffn_conv_b"])
    chip = 2 * lax.axis_index("x") + lax.axis_index("y")

    shards = [a[n].astype(WIRE_DTYPE) for n in _BIG]
    axes = list(_BIG.values())
    kinds = [_gather_kind(s, ax) for s, ax in zip(shards, axes)]
    full = {n: w.astype(MXU_DTYPE) for n, w in zip(_BIG, _gather_weights(shards, axes, "gather_weights"))}
    conv_shapes = [a[n].shape for n in _CONV]
    gotc = _exchange_chips(_pack([a[n] for n in _CONV], F32, LANES, 8), False, "gather_conv_weights")
    piecesc = [_unpack(gotc[j], conv_shapes) for j in range(N_CHIPS)]
    fullc = {n: jnp.concatenate([piecesc[j][i] for j in range(N_CHIPS)], axis=2) for i, n in enumerate(_CONV)}

    wfull, small = [], []
    for l in range(depth):
        wf = {n: full[n][l] for n in _BIG if n != "w_in"}
        wf["w_in_p"] = _permute_w_in(full["w_in"][l], dm)
        wfull.append(wf)
        small.append({n: (fullc[n][l] if n in _CONV else a[n][l]) for n in _SMALL})

    loss_part, grad_x, grads, gbig = _local_step(dm, x, c, target, wfull, small)

    core = lax.axis_index("c").astype(jnp.int32).reshape(1)
    gst = []
    for n, kind in zip(_BIG, kinds):
        g = _unpermute_w_in(gbig["w_in_p"], dm) if n == "w_in" else gbig[n]
        if kind == "slots":
            g = jnp.moveaxis(g.reshape(depth, g.shape[1], N_CHIPS, g.shape[2] // N_CHIPS), 2, 1)
            g = g.reshape(depth, -1, g.shape[-1])
        gst.append(g)
    other = _swap_other_layer(gst, "swap_layer_grads")
    parts = [_pair_sum(g, o, core, "pair_sum_" + n) for g, o, n in zip(gst, other, _BIG)]
    parts = [p.reshape(N_CHIPS, -1, p.shape[-1]) if k == "slots" else p for p, k in zip(parts, kinds)]
    got = _scatter_layer(parts, kinds, "scatter_grads")
    mine = [_sum_slots(g, "sum_chip_grads_" + n) for g, n in zip(got, _BIG)]
    theirs = _sibling_swap(mine, "swap_core_grads")

    out = {}
    for i, n in enumerate(_BIG):
        out[n] = _adamw_layers(a[n], a["m_" + n], a["v_" + n], mine[i], theirs[i], core, "adamw_" + n)

    gsmall = [jnp.stack([grads[l][n] for l in range(depth)]) for n in _SMALL]
    small_shapes = [g.shape for g in gsmall]
    summed = _sum_slots(_allgather_all(_pack(gsmall, F32, LANES, 8), "gather_small_grads"), "sum_small_grads")
    gs = dict(zip(_SMALL, _unpack(summed, small_shapes)))
    for n in _CONV:
        wcols = a[n].shape[2]
        gs[n] = lax.dynamic_slice_in_dim(gs[n], chip * wcols, wcols, axis=2)
    local_shapes = [a[n].shape for n in _SMALL]
    res = _adamw(_pack([a[n] for n in _SMALL], F32, LANES, 8), _pack([a["m_" + n] for n in _SMALL], F32, LANES, 8),
                 _pack([a["v_" + n] for n in _SMALL], F32, LANES, 8), _pack([gs[n] for n in _SMALL], F32, LANES, 8),
                 "adamw_small")
    res = [_unpack(r, local_shapes) for r in res]
    for i, n in enumerate(_SMALL):
        out[n] = [r[i] for r in res]

    loss = lax.psum(loss_part, ("x", "y", "c"))
    return (loss, grad_x) + tuple(out[n][k] for k in range(4) for n in _WEIGHTS)


def kernel(x, c, ada_w, ada_b, mix_pre_g, mix_post_g, w_in, ssd_conv_w, ssd_conv_b, ssd_dt_bias, ssd_a_log, ssd_d, ssd_norm_g, w_ssd_out, sc_conv_w, w_sc_out, w_o, ffn_pre_g, ffn_post_g, w_up, ffn_conv_w, ffn_conv_b, w_down, loss_target, m_ada_w, m_ada_b, m_mix_pre_g, m_mix_post_g, m_w_in, m_ssd_conv_w, m_ssd_conv_b, m_ssd_dt_bias, m_ssd_a_log, m_ssd_d, m_ssd_norm_g, m_w_ssd_out, m_sc_conv_w, m_w_sc_out, m_w_o, m_ffn_pre_g, m_ffn_post_g, m_w_up, m_ffn_conv_w, m_ffn_conv_b, m_w_down, v_ada_w, v_ada_b, v_mix_pre_g, v_mix_post_g, v_w_in, v_ssd_conv_w, v_ssd_conv_b, v_ssd_dt_bias, v_ssd_a_log, v_ssd_d, v_ssd_norm_g, v_w_ssd_out, v_sc_conv_w, v_w_sc_out, v_w_o, v_ffn_pre_g, v_ffn_post_g, v_w_up, v_ffn_conv_w, v_ffn_conv_b, v_w_down):
    return _step(dict(zip(_INPUTS, (
        x, c, ada_w, ada_b, mix_pre_g, mix_post_g, w_in, ssd_conv_w, ssd_conv_b, ssd_dt_bias, ssd_a_log, ssd_d, ssd_norm_g, w_ssd_out, sc_conv_w, w_sc_out, w_o, ffn_pre_g, ffn_post_g, w_up, ffn_conv_w, ffn_conv_b, w_down, loss_target, m_ada_w, m_ada_b, m_mix_pre_g, m_mix_post_g, m_w_in, m_ssd_conv_w, m_ssd_conv_b, m_ssd_dt_bias, m_ssd_a_log, m_ssd_d, m_ssd_norm_g, m_w_ssd_out, m_sc_conv_w, m_w_sc_out, m_w_o, m_ffn_pre_g, m_ffn_post_g, m_w_up, m_ffn_conv_w, m_ffn_conv_b, m_w_down, v_ada_w, v_ada_b, v_mix_pre_g, v_mix_post_g, v_w_in, v_ssd_conv_w, v_ssd_conv_b, v_ssd_dt_bias, v_ssd_a_log, v_ssd_d, v_ssd_norm_g, v_w_ssd_out, v_sc_conv_w, v_w_sc_out, v_w_o, v_ffn_pre_g, v_ffn_post_g, v_w_up, v_ffn_conv_w, v_ffn_conv_b, v_w_down))))
```

```python
import math

import jax
import jax.numpy as jnp
from jax import lax
from jax.experimental import pallas as pl
from jax.experimental.pallas import tpu as pltpu

F32 = jnp.float32
MXU_DTYPE = jnp.bfloat16
WIRE_DTYPE = jnp.bfloat16
HI = lax.Precision.HIGHEST
EPS = 1e-6
N_GROUPS = 4
CHUNK = 128
HEAD_DIM = 64
LANES = 128
HALO = 8
N_CHIPS = 4
N_DEV = 8
VMEM_LIMIT = 56 * 1024 * 1024
ADAM_LR, ADAM_B1, ADAM_B2, ADAM_EPS, ADAM_WD, ADAM_STEP = 0.001, 0.9, 0.999, 1e-08, 0.01, 10
MESH = pl.DeviceIdType.MESH

NN = (((1,), (0,)), ((), ()))
NT = (((1,), (1,)), ((), ()))
TN = (((0,), (0,)), ((), ()))


def _dg(a, b, dn, precision=None):
    return lax.dot_general(a, b, dn, precision=precision, preferred_element_type=F32)


def _tile(dim, pref, mult=LANES):
    t = (min(pref, dim) // mult) * mult
    while t >= mult:
        if dim % t == 0:
            return t
        t -= mult
    return dim


def _cp(sem):
    return pltpu.CompilerParams(dimension_semantics=sem, vmem_limit_bytes=VMEM_LIMIT)


def _sigmoid(x):
    return 1.0 / (1.0 + jnp.exp(-x))


def _softplus(x):
    return jnp.maximum(x, 0.0) + jnp.log1p(jnp.exp(-jnp.abs(x)))


def _matmul(a, b, mode, out_dtype, name, tm=1024, tn=1024, tk=1024, slab=None):
    if mode == "NN":
        (M, K), N = a.shape, b.shape[1]
    elif mode == "NT":
        (M, K), N = a.shape, b.shape[0]
    else:
        (K, M), N = a.shape, b.shape[1]
    tm, tn, tk = _tile(M, tm), _tile(N, tn), _tile(K, tk)
    nk = K // tk
    dn = {"NN": NN, "NT": NT, "TN": TN}[mode]
    carry = slab is not None and slab[0] is not None

    def body_one(a_ref, b_ref, *rest):
        o_ref = rest[-1]
        o_ref[...] = _dg(a_ref[...], b_ref[...], dn).astype(o_ref.dtype)

    def body_acc(a_ref, b_ref, *rest):
        o_ref, acc_ref = rest[-2:]
        k = pl.program_id(2)

        @pl.when(k == 0)
        def _():
            acc_ref[...] = jnp.zeros_like(acc_ref)

        acc_ref[...] += _dg(a_ref[...], b_ref[...], dn)

        @pl.when(k == nk - 1)
        def _():
            o_ref[...] = acc_ref[...].astype(o_ref.dtype)

    a_spec = (pl.BlockSpec((tk, tm), lambda i, j, k: (k, i)) if mode == "TN"
              else pl.BlockSpec((tm, tk), lambda i, j, k: (i, k)))
    b_spec = (pl.BlockSpec((tn, tk), lambda i, j, k: (j, k)) if mode == "NT"
              else pl.BlockSpec((tk, tn), lambda i, j, k: (k, j)))
    if slab is None:
        out_spec = pl.BlockSpec((tm, tn), lambda i, j, k: (i, j))
        out_shape = jax.ShapeDtypeStruct((M, N), out_dtype)
    else:
        layer = slab[1]
        out_spec = pl.BlockSpec((None, tm, tn), lambda i, j, k: (layer, i, j))
        out_shape = jax.ShapeDtypeStruct((slab[2], M, N), out_dtype)
    return pl.pallas_call(
        body_one if nk == 1 else body_acc, name=name, grid=(M // tm, N // tn, nk),
        in_specs=[a_spec, b_spec] + ([_ANY] if carry else []),
        out_specs=out_spec, out_shape=out_shape,
        input_output_aliases={2: 0} if carry else {},
        scratch_shapes=[] if nk == 1 else [pltpu.VMEM((tm, tn), F32)],
        compiler_params=_cp(("parallel", "parallel", "arbitrary")),
    )(*((a, b, slab[0]) if carry else (a, b)))


def _ada_fwd(c16, ada_w, ada_b, name):
    rows, D = c16.shape
    N6 = ada_w.shape[1]
    tn = _tile(N6, 1536)

    def body(c_ref, w_ref, b_ref, act_ref, mod_ref):
        c = c_ref[...]
        act = (c * _sigmoid(c)).astype(act_ref.dtype)
        act_ref[...] = act
        mod_ref[...] = _dg(act, w_ref[...], NN) + b_ref[...]

    return pl.pallas_call(
        body, name=name, grid=(N6 // tn,),
        in_specs=[pl.BlockSpec((rows, D), lambda j: (0, 0)),
                  pl.BlockSpec((D, tn), lambda j: (0, j)),
                  pl.BlockSpec((1, tn), lambda j: (0, j))],
        out_specs=[pl.BlockSpec((rows, D), lambda j: (0, 0)),
                   pl.BlockSpec((rows, tn), lambda j: (0, j))],
        out_shape=[jax.ShapeDtypeStruct((rows, D), MXU_DTYPE),
                   jax.ShapeDtypeStruct((rows, N6), F32)],
        compiler_params=_cp(("arbitrary",)),
    )(c16, ada_w, ada_b.reshape(1, N6))


def _norm_mod_rows(x, g, sc, sh):
    r = lax.rsqrt(jnp.mean(x * x, axis=-1, keepdims=True) + EPS)
    return ((x * r) * g) * (1.0 + sc) + sh


def _norm_mod_fwd(x, g, mod, row_sh, name, ts=512):
    Bl, S, D = x.shape
    ts = _tile(S, ts, 8)

    def body(x_ref, g_ref, mod_ref, h_ref):
        sh = mod_ref[row_sh:row_sh + 1, :]
        sc = mod_ref[row_sh + 1:row_sh + 2, :]
        h_ref[...] = _norm_mod_rows(x_ref[...], g_ref[...], sc, sh).astype(h_ref.dtype)

    tok = pl.BlockSpec((None, ts, D), lambda b, s: (b, s, 0))
    return pl.pallas_call(
        body, name=name, grid=(Bl, S // ts),
        in_specs=[tok, pl.BlockSpec((1, D), lambda b, s: (0, 0)),
                  pl.BlockSpec((None, 8, D), lambda b, s: (b, 0, 0))],
        out_specs=tok, out_shape=jax.ShapeDtypeStruct((Bl, S, D), MXU_DTYPE),
        compiler_params=_cp(("parallel", "parallel")),
    )(x, g.reshape(1, D), mod)


def _post_norm_fwd(xp, f, post_g, mod, row_gt, nxt, name, ts=512):
    Bl, S, D = xp.shape
    ts = _tile(S, ts, 8)
    has_next = nxt is not None

    def body(*refs):
        if has_next:
            xp_ref, f_ref, pg_ref, mod_ref, ng_ref, nmod_ref, x_ref, h_ref = refs
        else:
            xp_ref, f_ref, pg_ref, mod_ref, x_ref = refs
        f = f_ref[...]
        r = lax.rsqrt(jnp.mean(f * f, axis=-1, keepdims=True) + EPS)
        x = xp_ref[...] + mod_ref[row_gt:row_gt + 1, :] * ((f * r) * pg_ref[...])
        x_ref[...] = x
        if has_next:
            rs = nxt[2]
            h_ref[...] = _norm_mod_rows(x, ng_ref[...], nmod_ref[rs + 1:rs + 2, :], nmod_ref[rs:rs + 1, :]).astype(h_ref.dtype)

    tok = pl.BlockSpec((None, ts, D), lambda b, s: (b, s, 0))
    vec = pl.BlockSpec((1, D), lambda b, s: (0, 0))
    modspec = pl.BlockSpec((None, 8, D), lambda b, s: (b, 0, 0))
    ins = [xp, f, post_g.reshape(1, D), mod]
    in_specs = [tok, tok, vec, modspec]
    out_specs = [tok]
    out_shape = [jax.ShapeDtypeStruct((Bl, S, D), F32)]
    if has_next:
        ins += [nxt[0].reshape(1, D), nxt[1]]
        in_specs += [vec, modspec]
        out_specs += [tok]
        out_shape += [jax.ShapeDtypeStruct((Bl, S, D), MXU_DTYPE)]
    out = pl.pallas_call(
        body, name=name, grid=(Bl, S // ts), in_specs=in_specs, out_specs=out_specs, out_shape=out_shape,
        compiler_params=_cp(("parallel", "parallel")),
    )(*ins)
    return (out[0], out[1]) if has_next else (out[0], None)


def _loss_fwd_bwd(y, target, name, ts=512):
    Bl, S, D = y.shape
    ts = _tile(S, ts, 8)

    def body(y_ref, t_ref, dy_ref, l_ref):
        @pl.when((pl.program_id(0) == 0) & (pl.program_id(1) == 0))
        def _():
            l_ref[...] = jnp.zeros_like(l_ref)

        e = y_ref[...] - t_ref[...]
        dy_ref[...] = e * (1.0 / D)
        l_ref[...] += 0.5 * jnp.sum(jnp.mean(e * e, axis=-1, keepdims=True), axis=0, keepdims=True)

    tok = pl.BlockSpec((None, ts, D), lambda b, s: (b, s, 0))
    dy, l = pl.pallas_call(
        body, name=name, grid=(Bl, S // ts), in_specs=[tok, tok],
        out_specs=[tok, pl.BlockSpec((8, LANES), lambda b, s: (0, 0))],
        out_shape=[jax.ShapeDtypeStruct((Bl, S, D), F32), jax.ShapeDtypeStruct((8, LANES), F32)],
        compiler_params=_cp(("arbitrary", "arbitrary")),
    )(y, target)
    return dy, l[0, 0]


def _norm_bwd(dx_res, nxt, prv, name, ts=256):
    Bl, S, D = dx_res.shape
    ts = _tile(S, ts, 8)
    has_next, has_prev = nxt is not None, prv is not None

    def body(*refs):
        refs = list(refs)
        dxr_ref = refs.pop(0)
        if has_next:
            dh_ref, x_ref, g_ref, nmod_ref = refs[:4]
            refs = refs[4:]
        if has_prev:
            f_ref, pg_ref, pmod_ref = refs[:3]
            refs = refs[3:]
        dx_ref = refs.pop(0)
        if has_prev:
            df_ref = refs.pop(0)
        pb_ref, sh_ref = refs
        b, s = pl.program_id(0), pl.program_id(1)

        @pl.when(s == 0)
        def _():
            pb_ref[...] = jnp.zeros_like(pb_ref)

        @pl.when((b == 0) & (s == 0))
        def _():
            sh_ref[...] = jnp.zeros_like(sh_ref)

        dx = dxr_ref[...]
        if has_next:
            rs = nxt[4]
            x, dh, g = x_ref[...], dh_ref[...], g_ref[...]
            sc1 = 1.0 + nmod_ref[rs + 1:rs + 2, :]
            r = lax.rsqrt(jnp.mean(x * x, axis=-1, keepdims=True) + EPS)
            xn = x * r
            pb_ref[0:1, :] += jnp.sum(dh, axis=0, keepdims=True)
            pb_ref[1:2, :] += jnp.sum(dh * (xn * g), axis=0, keepdims=True)
            sh_ref[0:1, :] += jnp.sum(dh * sc1 * xn, axis=0, keepdims=True)
            dxn = dh * sc1 * g
            dx = dx + r * (dxn - xn * jnp.mean(dxn * xn, axis=-1, keepdims=True))
        dx_ref[...] = dx
        if has_prev:
            rg = prv[3]
            f, pg = f_ref[...], pg_ref[...]
            gt = pmod_ref[rg:rg + 1, :]
            r = lax.rsqrt(jnp.mean(f * f, axis=-1, keepdims=True) + EPS)
            fn = f * r
            pb_ref[2:3, :] += jnp.sum(dx * (fn * pg), axis=0, keepdims=True)
            drn = dx * gt
            sh_ref[1:2, :] += jnp.sum(drn * fn, axis=0, keepdims=True)
            dfn = drn * pg
            df_ref[...] = (r * (dfn - fn * jnp.mean(dfn * fn, axis=-1, keepdims=True))).astype(df_ref.dtype)

    tok = pl.BlockSpec((None, ts, D), lambda b, s: (b, s, 0))
    vec = pl.BlockSpec((1, D), lambda b, s: (0, 0))
    modspec = pl.BlockSpec((None, 8, D), lambda b, s: (b, 0, 0))
    ins, in_specs = [dx_res], [tok]
    if has_next:
        ins += [nxt[0], nxt[1], nxt[2].reshape(1, D), nxt[3]]
        in_specs += [tok, tok, vec, modspec]
    if has_prev:
        ins += [prv[0], prv[1].reshape(1, D), prv[2]]
        in_specs += [tok, vec, modspec]
    out_specs, out_shape = [tok], [jax.ShapeDtypeStruct((Bl, S, D), F32)]
    if has_prev:
        out_specs += [tok]
        out_shape += [jax.ShapeDtypeStruct((Bl, S, D), MXU_DTYPE)]
    out_specs += [modspec, pl.BlockSpec((8, D), lambda b, s: (0, 0))]
    out_shape += [jax.ShapeDtypeStruct((Bl, 8, D), F32), jax.ShapeDtypeStruct((8, D), F32)]
    out = pl.pallas_call(
        body, name=name, grid=(Bl, S // ts), in_specs=in_specs, out_specs=out_specs, out_shape=out_shape,
        compiler_params=_cp(("arbitrary", "arbitrary")),
    )(*ins)
    if has_prev:
        return out[0], out[1], out[2], out[3]
    return out[0], None, out[1], out[2]


def _shift_down(x, j):
    return x if j == 0 else pltpu.roll(x, j, axis=0)


def _shift_up(x, j):
    return x if j == 0 else pltpu.roll(x, x.shape[0] - j, axis=0)


def _conv(xall, w_ref, K):
    y = w_ref[K - 1:K, :] * xall
    for k in range(K - 1):
        y = y + w_ref[k:k + 1, :] * _shift_down(xall, K - 1 - k)
    return y


def _conv_t(dall, xall, w, K, rows):
    xt = xall[HALO:HALO + rows]
    y = w[K - 1:K, :] * dall[HALO:HALO + rows]
    gw = [None] * K
    gw[K - 1] = jnp.sum(dall[HALO:HALO + rows] * xt, axis=0, keepdims=True)
    for k in range(K - 1):
        sh = _shift_up(dall, K - 1 - k)[HALO:HALO + rows]
        y = y + w[k:k + 1, :] * sh
        gw[k] = jnp.sum(sh * xt, axis=0, keepdims=True)
    return y, gw


def _conv_t_rows(dall, w_ref, K):
    y = w_ref[K - 1:K, :] * dall
    for k in range(K - 1):
        y = y + w_ref[k:k + 1, :] * _shift_up(dall, K - 1 - k)
    return y


def _conv_wgrad(acc_ref, dtile, xall, K, ts):
    for k in range(K):
        xs = _shift_down(xall, K - 1 - k)[HALO:HALO + ts]
        acc_ref[k:k + 1, :] += jnp.sum(dtile * xs, axis=0, keepdims=True)


ROW_CHUNK = 64


def _lane_chunks(W):
    lc = 2 * LANES if W % (2 * LANES) == 0 else LANES
    return [(j * lc, lc) for j in range(W // lc)]


def _fill_ext(ext_ref, prev, tile, nxt, ts):
    ext_ref[0:HALO, :] = prev
    ext_ref[HALO:HALO + ts, :] = tile
    ext_ref[HALO + ts:2 * HALO + ts, :] = nxt


def _halo_specs(ts, W, nS, colblk):
    per = ts // HALO
    tile = pl.BlockSpec((None, ts, W), lambda b, c, s: (b, s, colblk(c)))
    prev = pl.BlockSpec((None, HALO, W), lambda b, c, s: (b, jnp.maximum(s * per - 1, 0), colblk(c)))
    nxt = pl.BlockSpec((None, HALO, W), lambda b, c, s: (b, jnp.minimum((s + 1) * per, nS * per - 1), colblk(c)))
    return tile, prev, nxt


def _masked(ref, keep):
    v = ref[...]
    return jnp.where(keep, v, jnp.zeros_like(v))


def _ssd_conv_fwd(proj, w8, b, off, CD, name, ts=512, W=1024):
    Bl, S, _ = proj.shape
    K = 4
    ts, W = _tile(S, ts, 8), _tile(math.gcd(CD, off), W)
    assert off % W == 0
    nS, nW, ob = S // ts, CD // W, off // W

    def body(x_ref, xp_ref, w_ref, b_ref, o_ref):
        xall = jnp.concatenate([_masked(xp_ref, pl.program_id(2) > 0), x_ref[...]], axis=0)
        xc = _conv(xall, w_ref, K)[HALO:] + b_ref[...]
        o_ref[...] = xc * _sigmoid(xc)

    tile, prev, _ = _halo_specs(ts, W, nS, lambda c: ob + c)
    return pl.pallas_call(
        body, name=name, grid=(Bl, nW, nS),
        in_specs=[tile, prev, pl.BlockSpec((8, W), lambda b_, c, s: (0, c)), pl.BlockSpec((1, W), lambda b_, c, s: (0, c))],
        out_specs=pl.BlockSpec((None, ts, W), lambda b_, c, s: (b_, s, c)),
        out_shape=jax.ShapeDtypeStruct((Bl, S, CD), F32),
        compiler_params=_cp(("parallel", "parallel", "parallel")),
    )(proj, proj, w8, b.reshape(1, CD))


def _ssd_conv_bwd(dxa, proj, w8, b, off, CD, dproj, name, ts=256, W=1024):
    Bl, S, _ = proj.shape
    K = 4
    ts, W = _tile(S, ts, 8), _tile(math.gcd(CD, off), W)
    nS, nW, ob = S // ts, CD // W, off // W

    rc = _tile(ts, ROW_CHUNK, 8)

    def body(d_ref, dn_ref, x_ref, xp_ref, xn_ref, w_ref, b_ref, _, dx_ref, dw_ref, db_ref, xext_ref, dext_ref):
        bb, s = pl.program_id(1), pl.program_id(2)

        @pl.when((bb == 0) & (s == 0))
        def _():
            dw_ref[...] = jnp.zeros_like(dw_ref)
            db_ref[...] = jnp.zeros_like(db_ref)

        last = s == nS - 1
        _fill_ext(xext_ref, _masked(xp_ref, s > 0), x_ref[...], _masked(xn_ref, ~last), ts)
        _fill_ext(dext_ref, jnp.zeros((HALO, W), F32), d_ref[...], _masked(dn_ref, ~last), ts)
        for l0, lc in _lane_chunks(W):
            w, bias = w_ref[:, l0:l0 + lc], b_ref[:, l0:l0 + lc]

            def chunk(i, acc):
                r0 = pl.multiple_of(i * rc, 8)
                xw = xext_ref[pl.ds(r0, rc + 2 * HALO), l0:l0 + lc]
                xc = _conv(xw, w, K) + bias
                sg = _sigmoid(xc)
                dxc = dext_ref[pl.ds(r0, rc + 2 * HALO), l0:l0 + lc] * (sg * (1.0 + xc * (1.0 - sg)))
                dx, gw = _conv_t(dxc, xw, w, K, rc)
                dx_ref[pl.ds(r0, rc), l0:l0 + lc] = dx.astype(dx_ref.dtype)
                gb = jnp.sum(dxc[HALO:HALO + rc], axis=0, keepdims=True)
                return tuple(a + g for a, g in zip(acc, gw + [gb]))

            acc = lax.fori_loop(0, ts // rc, chunk, tuple(jnp.zeros((1, lc), F32) for _ in range(K + 1)))
            for k in range(K):
                dw_ref[k:k + 1, l0:l0 + lc] += acc[k]
            db_ref[0:1, l0:l0 + lc] += acc[K]

    per = ts // HALO
    dtile_s = pl.BlockSpec((None, ts, W), lambda c, b_, s: (b_, s, c))
    dnext_s = pl.BlockSpec((None, HALO, W), lambda c, b_, s: (b_, jnp.minimum((s + 1) * per, nS * per - 1), c))
    xtile_s = pl.BlockSpec((None, ts, W), lambda c, b_, s: (b_, s, ob + c))
    xprev_s = pl.BlockSpec((None, HALO, W), lambda c, b_, s: (b_, jnp.maximum(s * per - 1, 0), ob + c))
    xnext_s = pl.BlockSpec((None, HALO, W), lambda c, b_, s: (b_, jnp.minimum((s + 1) * per, nS * per - 1), ob + c))
    wspec = pl.BlockSpec((8, W), lambda c, b_, s: (0, c))
    return pl.pallas_call(
        body, name=name, grid=(nW, Bl, nS),
        in_specs=[dtile_s, dnext_s, xtile_s, xprev_s, xnext_s, wspec, pl.BlockSpec((1, W), lambda c, b_, s: (0, c)),
                  _ANY],
        out_specs=[xtile_s, wspec, wspec],
        out_shape=[jax.ShapeDtypeStruct(dproj.shape, dproj.dtype), jax.ShapeDtypeStruct((8, CD), F32),
                   jax.ShapeDtypeStruct((8, CD), F32)],
        input_output_aliases={7: 0},
        scratch_shapes=[pltpu.VMEM((ts + 2 * HALO, W), F32)] * 2,
        compiler_params=_cp(("arbitrary", "arbitrary", "arbitrary")),
    )(dxa, dxa, proj, proj, proj, w8, b.reshape(1, CD), dproj)


def _sc_conv_fwd(proj, w8, offs, D, name, ts=512, W=1024):
    Bl, S, _ = proj.shape
    K = 3
    ts, W = _tile(S, ts, 8), _tile(D, W)
    nS, nW = S // ts, D // W
    ob, oc, oh = [o // W for o in offs]

    def body(b_ref, c_ref, cp_ref, h_ref, hp_ref, w_ref, o_ref):
        s = pl.program_id(2)
        keep = s > 0
        vall = jnp.concatenate([_masked(cp_ref, keep) * _masked(hp_ref, keep), c_ref[...] * h_ref[...]], axis=0)
        o_ref[...] = (b_ref[...] * _conv(vall, w_ref, K)[HALO:]).astype(o_ref.dtype)

    tb, _, _ = _halo_specs(ts, W, nS, lambda c: ob + c)
    tc, pc, _ = _halo_specs(ts, W, nS, lambda c: oc + c)
    th, ph, _ = _halo_specs(ts, W, nS, lambda c: oh + c)
    return pl.pallas_call(
        body, name=name, grid=(Bl, nW, nS),
        in_specs=[tb, tc, pc, th, ph, pl.BlockSpec((8, W), lambda b_, c, s: (0, c))],
        out_specs=pl.BlockSpec((None, ts, W), lambda b_, c, s: (b_, s, c)),
        out_shape=jax.ShapeDtypeStruct((Bl, S, D), MXU_DTYPE),
        compiler_params=_cp(("parallel", "parallel", "parallel")),
    )(proj, proj, proj, proj, proj, w8)


def _sc_conv_bwd(ds, proj, w8, offs, D, dproj, name, ts=256):
    Bl, S, _ = proj.shape
    K = 3
    ts, W = _tile(S, ts, 8), D
    nS, nW = S // ts, 1
    ob, oc, oh = [o // W for o in offs]
    assert offs[1] == offs[0] + D and offs[2] == offs[0] + 2 * D and offs[0] % (3 * D) == 0

    def body(d_ref, dn_ref, b_ref, bn_ref, c_ref, cp_ref, cn_ref, h_ref, hp_ref, hn_ref, w_ref, _,
             o_ref, dw_ref):
        db_ref, dc_ref, dh_ref = o_ref.at[:, 0:D], o_ref.at[:, D:2 * D], o_ref.at[:, 2 * D:3 * D]
        bb, s = pl.program_id(1), pl.program_id(2)

        @pl.when((bb == 0) & (s == 0))
        def _():
            dw_ref[...] = jnp.zeros_like(dw_ref)

        first, last = s > 0, s < nS - 1
        zeros = jnp.zeros((HALO, W), F32)
        c_t, h_t = c_ref[...], h_ref[...]
        vall = jnp.concatenate([_masked(cp_ref, first) * _masked(hp_ref, first), c_t * h_t,
                                _masked(cn_ref, last) * _masked(hn_ref, last)], axis=0)
        dcv = jnp.concatenate([zeros, d_ref[...] * b_ref[...], _masked(dn_ref, last) * _masked(bn_ref, last)], axis=0)
        cv = _conv(vall, w_ref, K)[HALO:HALO + ts]
        db_ref[...] = (d_ref[...] * cv).astype(db_ref.dtype)
        dv, gw = _conv_t(dcv, vall, w_ref, K, ts)
        dc_ref[...] = (dv * h_t).astype(dc_ref.dtype)
        dh_ref[...] = (dv * c_t).astype(dh_ref.dtype)
        for k in range(K):
            dw_ref[k:k + 1, :] += gw[k]

    per = ts // HALO

    def specs(o):
        t = pl.BlockSpec((None, ts, W), lambda c, b_, s: (b_, s, o + c))
        p = pl.BlockSpec((None, HALO, W), lambda c, b_, s: (b_, jnp.maximum(s * per - 1, 0), o + c))
        n = pl.BlockSpec((None, HALO, W), lambda c, b_, s: (b_, jnp.minimum((s + 1) * per, nS * per - 1), o + c))
        return t, p, n

    dt_, _, dn_ = specs(0)
    bt, _, bn = specs(ob)
    ct, cp, cn = specs(oc)
    ht, hp, hn = specs(oh)
    wspec = pl.BlockSpec((8, W), lambda c, b_, s: (0, c))
    o3 = offs[0] // (3 * D)
    return pl.pallas_call(
        body, name=name, grid=(nW, Bl, nS),
        in_specs=[dt_, dn_, bt, bn, ct, cp, cn, ht, hp, hn, wspec, _ANY],
        out_specs=[pl.BlockSpec((None, ts, 3 * D), lambda c, b_, s: (b_, s, o3)), wspec],
        out_shape=[jax.ShapeDtypeStruct(dproj.shape, dproj.dtype), jax.ShapeDtypeStruct((8, D), F32)],
        input_output_aliases={11: 0},
        compiler_params=_cp(("arbitrary", "arbitrary", "arbitrary")),
    )(ds, ds, proj, proj, proj, proj, proj, proj, proj, proj, w8, dproj)


def _ffn_conv_fwd(up, w8, b, DFF, name, ts=512, W=1408):
    Bl, S, _ = up.shape
    K = 3
    ts, W = _tile(S, ts, 8), _tile(DFF, W)
    nS, nW = S // ts, DFF // W

    def body(g_ref, gp_ref, v_ref, vp_ref, wg_ref, wv_ref, bg_ref, bv_ref, o_ref):
        keep = pl.program_id(2) > 0
        ug = _conv(jnp.concatenate([_masked(gp_ref, keep), g_ref[...]], axis=0), wg_ref, K)[HALO:] + bg_ref[...]
        uv = _conv(jnp.concatenate([_masked(vp_ref, keep), v_ref[...]], axis=0), wv_ref, K)[HALO:] + bv_ref[...]
        o_ref[...] = (ug * _sigmoid(ug) * uv).astype(o_ref.dtype)

    tg, pg, _ = _halo_specs(ts, W, nS, lambda c: c)
    tv, pv, _ = _halo_specs(ts, W, nS, lambda c: nW + c)
    wg = pl.BlockSpec((8, W), lambda b_, c, s: (0, c))
    wv = pl.BlockSpec((8, W), lambda b_, c, s: (0, nW + c))
    bg = pl.BlockSpec((1, W), lambda b_, c, s: (0, c))
    bv = pl.BlockSpec((1, W), lambda b_, c, s: (0, nW + c))
    b2 = b.reshape(1, 2 * DFF)
    return pl.pallas_call(
        body, name=name, grid=(Bl, nW, nS),
        in_specs=[tg, pg, tv, pv, wg, wv, bg, bv],
        out_specs=pl.BlockSpec((None, ts, W), lambda b_, c, s: (b_, s, c)),
        out_shape=jax.ShapeDtypeStruct((Bl, S, DFF), MXU_DTYPE),
        compiler_params=_cp(("parallel", "parallel", "parallel")),
    )(up, up, up, up, w8, w8, b2, b2)


def _ffn_conv_bwd(da, up, w8, b, DFF, name, ts=256, W=1408):
    Bl, S, _ = up.shape
    K = 3
    ts, W = _tile(S, ts, 8), _tile(DFF, W)
    nS, nW = S // ts, DFF // W

    def body(d_ref, dn_ref, g_ref, gp_ref, gn_ref, v_ref, vp_ref, vn_ref, wg_ref, wv_ref, bg_ref, bv_ref,
             dg_ref, dv_ref, dwg_ref, dwv_ref, dbg_ref, dbv_ref):
        bb, s = pl.program_id(1), pl.program_id(2)

        @pl.when((bb == 0) & (s == 0))
        def _():
            for r in (dwg_ref, dwv_ref, dbg_ref, dbv_ref):
                r[...] = jnp.zeros_like(r)

        first, last = s > 0, s < nS - 1
        gall = jnp.concatenate([_masked(gp_ref, first), g_ref[...], _masked(gn_ref, last)], axis=0)
        vall = jnp.concatenate([_masked(vp_ref, first), v_ref[...], _masked(vn_ref, last)], axis=0)
        dall = jnp.concatenate([jnp.zeros((HALO, W), F32), d_ref[...], _masked(dn_ref, last)], axis=0)
        ug = _conv(gall, wg_ref, K) + bg_ref[...]
        uv = _conv(vall, wv_ref, K) + bv_ref[...]
        sg = _sigmoid(ug)
        dug = dall * uv * (sg * (1.0 + ug * (1.0 - sg)))
        duv = dall * (ug * sg)
        dg_ref[...] = _conv_t_rows(dug, wg_ref, K)[HALO:HALO + ts].astype(dg_ref.dtype)
        dv_ref[...] = _conv_t_rows(duv, wv_ref, K)[HALO:HALO + ts].astype(dv_ref.dtype)
        dug_t, duv_t = dug[HALO:HALO + ts], duv[HALO:HALO + ts]
        dbg_ref[0:1, :] += jnp.sum(dug_t, axis=0, keepdims=True)
        dbv_ref[0:1, :] += jnp.sum(duv_t, axis=0, keepdims=True)
        _conv_wgrad(dwg_ref, dug_t, gall, K, ts)
        _conv_wgrad(dwv_ref, duv_t, vall, K, ts)

    per = ts // HALO

    def specs(o):
        t = pl.BlockSpec((None, ts, W), lambda c, b_, s: (b_, s, o + c))
        p = pl.BlockSpec((None, HALO, W), lambda c, b_, s: (b_, jnp.maximum(s * per - 1, 0), o + c))
        n = pl.BlockSpec((None, HALO, W), lambda c, b_, s: (b_, jnp.minimum((s + 1) * per, nS * per - 1), o + c))
        return t, p, n

    dt_, _, dn_ = specs(0)
    gt, gp, gn = specs(0)
    vt, vp, vn = specs(nW)
    wg = pl.BlockSpec((8, W), lambda c, b_, s: (0, c))
    wv = pl.BlockSpec((8, W), lambda c, b_, s: (0, nW + c))
    bg = pl.BlockSpec((1, W), lambda c, b_, s: (0, c))
    bv = pl.BlockSpec((1, W), lambda c, b_, s: (0, nW + c))
    b2 = b.reshape(1, 2 * DFF)
    act = jax.ShapeDtypeStruct((Bl, S, DFF), MXU_DTYPE)
    small = jax.ShapeDtypeStruct((8, DFF), F32)
    dg, dv, dwg, dwv, dbg, dbv = pl.pallas_call(
        body, name=name, grid=(nW, Bl, nS),
        in_specs=[dt_, dn_, gt, gp, gn, vt, vp, vn, wg, wv, bg, bv],
        out_specs=[dt_, dt_, wg, wg, wg, wg],
        out_shape=[act, act, small, small, small, small],
        compiler_params=_cp(("arbitrary", "arbitrary", "arbitrary")),
    )(da, da, up, up, up, up, up, up, w8, w8, b2, b2)
    return dg, dv, jnp.concatenate([dwg, dwv], axis=1), jnp.concatenate([dbg, dbv], axis=1)


def _merge_fwd(proj, y_ssd, y_sc, off, D, name, ts=512):
    Bl, S, _ = proj.shape
    ts = _tile(S, ts, 8)
    og = off // D

    def body(g1_ref, g2_ref, a_ref, b_ref, o_ref):
        o_ref[...] = (_sigmoid(g1_ref[...]) * a_ref[...] + _sigmoid(g2_ref[...]) * b_ref[...]).astype(o_ref.dtype)

    tok = pl.BlockSpec((None, ts, D), lambda b, s: (b, s, 0))
    return pl.pallas_call(
        body, name=name, grid=(Bl, S // ts),
        in_specs=[pl.BlockSpec((None, ts, D), lambda b, s: (b, s, og)),
                  pl.BlockSpec((None, ts, D), lambda b, s: (b, s, og + 1)), tok, tok],
        out_specs=tok, out_shape=jax.ShapeDtypeStruct((Bl, S, D), MXU_DTYPE),
        compiler_params=_cp(("parallel", "parallel")),
    )(proj, proj, y_ssd, y_sc)


def _merge_bwd(dmixin, proj, y_ssd, y_sc, off, D, dproj0, name, ts=512):
    Bl, S, NP = proj.shape
    ts = _tile(S, ts, 8)
    og = off // D
    assert off % (2 * D) == 0

    def body(d_ref, g1_ref, g2_ref, a_ref, b_ref, *rest):
        da_ref, db_ref, dg_ref = rest[-3:]
        dg1_ref, dg2_ref = dg_ref.at[:, 0:D], dg_ref.at[:, D:2 * D]
        d = d_ref[...]
        s1, s2 = _sigmoid(g1_ref[...]), _sigmoid(g2_ref[...])
        da_ref[...] = (d * s1).astype(da_ref.dtype)
        db_ref[...] = (d * s2).astype(db_ref.dtype)
        dg1_ref[...] = (d * a_ref[...] * (s1 * (1.0 - s1))).astype(dg1_ref.dtype)
        dg2_ref[...] = (d * b_ref[...] * (s2 * (1.0 - s2))).astype(dg2_ref.dtype)

    tok = pl.BlockSpec((None, ts, D), lambda b, s: (b, s, 0))
    act = jax.ShapeDtypeStruct((Bl, S, D), MXU_DTYPE)
    carry = dproj0 is not None
    return pl.pallas_call(
        body, name=name, grid=(Bl, S // ts),
        in_specs=[tok, pl.BlockSpec((None, ts, D), lambda b, s: (b, s, og)),
                  pl.BlockSpec((None, ts, D), lambda b, s: (b, s, og + 1)), tok, tok] + ([_ANY] if carry else []),
        out_specs=[tok, tok, pl.BlockSpec((None, ts, 2 * D), lambda b, s: (b, s, og // 2))],
        out_shape=[act, act, jax.ShapeDtypeStruct((Bl, S, NP), MXU_DTYPE)],
        input_output_aliases={5: 2} if carry else {},
        compiler_params=_cp(("parallel", "parallel")),
    )(*((dmixin, proj, proj, y_ssd, y_sc) + ((dproj0,) if carry else ())))


def _exact_dot(a, b, dn, value):
    v = a if value == 0 else b
    m01 = (b if value == 0 else a).astype(MXU_DTYPE)
    hi = v.astype(MXU_DTYPE)
    r1 = v - hi.astype(F32)
    mid = r1.astype(MXU_DTYPE)
    lo = (r1 - mid.astype(F32)).astype(MXU_DTYPE)
    terms = [(t, m01) if value == 0 else (m01, t) for t in (hi, mid, lo)]
    return _dg(*terms[0], dn) + _dg(*terms[1], dn) + _dg(*terms[2], dn)


def _two_term_dot(v, m01):
    hi = v.astype(MXU_DTYPE)
    lo = (v - hi.astype(F32)).astype(MXU_DTYPE)
    return _dg(hi, m01, NN) + _dg(lo, m01, NN)


def _head_to_channels(R):
    rp = R * HEAD_DIM
    return (lax.shift_right_logical(lax.broadcasted_iota(jnp.int32, (LANES, rp), 1), 6)
            == lax.broadcasted_iota(jnp.int32, (LANES, rp), 0)).astype(MXU_DTYPE)


def _expand_heads(mats, R):
    L = mats[0].shape[0]
    out = _two_term_dot(jnp.concatenate(mats, axis=0), _head_to_channels(R))
    return [out[i * L:(i + 1) * L, :] for i in range(len(mats))]


def _expand_cols(mat, R, lane):
    half = lane < HEAD_DIM
    return jnp.concatenate(
        [jnp.where(half, mat[:, 2 * q:2 * q + 1], mat[:, 2 * q + 1:2 * q + 2]) for q in range(R // 2)], axis=1)


def _row_sums(v):
    return _two_term_dot(v, jnp.ones((v.shape[1], LANES), MXU_DTYPE))


def _lanes_to(v, width):
    return jnp.concatenate([v] * (width // LANES), axis=1)


def _head_rows(colvec, R, N):
    return jnp.concatenate([jnp.broadcast_to(colvec[r:r + 1, :], (HEAD_DIM, N)) for r in range(R)], axis=0)


def _ssd_prep(proj, dt_bias, a_log, selcat, selbig, dm, name):
    Bl, S, L, G = dm.Bl, dm.S, CHUNK, N_GROUPS
    nc = S // L
    odt = dm.ODT // LANES

    def body(dtr_ref, dtb_ref, alog_ref, selcat_ref, selbig_ref, dtg_ref, acs_ref, acst_ref):
        row = lax.broadcasted_iota(jnp.int32, (L, L), 0)
        col = lax.broadcasted_iota(jnp.int32, (L, L), 1)
        dt_all = _softplus(dtr_ref[...] + dtb_ref[...])
        acs_all = _exact_dot((row >= col).astype(F32), dt_all * (-jnp.exp(alog_ref[...])), NN, 1)
        dtg_ref[...] = _exact_dot(dt_all, selcat_ref[...], NN, 0)
        acs_ref[...] = _exact_dot(acs_all, selcat_ref[...], NN, 0)
        acst_ref[...] = _exact_dot(selbig_ref[...], acs_all, NT, 1)

    vec = pl.BlockSpec((1, LANES), lambda b, c: (0, 0))
    wide = pl.BlockSpec((None, L, G * LANES), lambda b, c: (b, c, 0))
    return pl.pallas_call(
        body, name=name, grid=(Bl, nc),
        in_specs=[pl.BlockSpec((None, L, LANES), lambda b, c: (b, c, odt)), vec, vec,
                  pl.BlockSpec((LANES, G * LANES), lambda b, c: (0, 0)),
                  pl.BlockSpec((G * 8, LANES), lambda b, c: (0, 0))],
        out_specs=[wide, wide, pl.BlockSpec((None, None, G * 8, L), lambda b, c: (b, c, 0, 0))],
        out_shape=[jax.ShapeDtypeStruct((Bl, S, G * LANES), F32), jax.ShapeDtypeStruct((Bl, S, G * LANES), F32),
                   jax.ShapeDtypeStruct((Bl, nc, G * 8, L), F32)],
        compiler_params=_cp(("parallel", "parallel")),
    )(proj, dt_bias, a_log, selcat, selbig)


def _ssd_post(d_a, ddt, dtg, proj, dt_bias, a_log, selcat, dproj, dm, name):
    Bl, S, L, G = dm.Bl, dm.S, CHUNK, N_GROUPS
    nc = S // L
    odt = dm.ODT // LANES

    def body(da_ref, ddt_ref, dtg_ref, dtr_ref, dtb_ref, alog_ref, selcat_ref, _, ddtr_ref, dpar_ref):
        @pl.when((pl.program_id(0) == 0) & (pl.program_id(1) == 0))
        def _():
            dpar_ref[...] = jnp.zeros_like(dpar_ref)

        row = lax.broadcasted_iota(jnp.int32, (L, L), 0)
        col = lax.broadcasted_iota(jnp.int32, (L, L), 1)
        selcat = selcat_ref[...]
        a_all = -jnp.exp(alog_ref[...])
        a4 = _exact_dot(jnp.broadcast_to(a_all, (8, LANES)), selcat, NN, 0)[0:1, :]
        dadt = _exact_dot((col >= row).astype(F32), da_ref[...], NN, 1)
        ddt4 = ddt_ref[...] + dadt * a4
        da4 = jnp.sum(dadt * dtg_ref[...], axis=0, keepdims=True)
        ddt_all = _exact_dot(ddt4, selcat, NT, 0)
        da_all = _exact_dot(jnp.broadcast_to(da4, (8, G * LANES)), selcat, NT, 0)[0:1, :]
        ddtr = ddt_all * _sigmoid(dtr_ref[...] + dtb_ref[...])
        ddtr_ref[...] = ddtr.astype(ddtr_ref.dtype)
        dpar_ref[0:1, :] += jnp.sum(ddtr, axis=0, keepdims=True)
        dpar_ref[1:2, :] += da_all * a_all

    vec = pl.BlockSpec((1, LANES), lambda b, c: (0, 0))
    wide = pl.BlockSpec((None, L, G * LANES), lambda b, c: (b, c, 0))
    return pl.pallas_call(
        body, name=name, grid=(Bl, nc),
        in_specs=[wide, wide, wide, pl.BlockSpec((None, L, LANES), lambda b, c: (b, c, odt)), vec, vec,
                  pl.BlockSpec((LANES, G * LANES), lambda b, c: (0, 0)), _ANY],
        out_specs=[pl.BlockSpec((None, L, LANES), lambda b, c: (b, c, odt)), pl.BlockSpec((8, LANES), lambda b, c: (0, 0))],
        out_shape=[jax.ShapeDtypeStruct(dproj.shape, dproj.dtype), jax.ShapeDtypeStruct((8, LANES), F32)],
        input_output_aliases={7: 0},
        compiler_params=_cp(("arbitrary", "arbitrary")),
    )(d_a, ddt, dtg, proj, dt_bias, a_log, selcat, dproj)


def _scan_fwd(xbc_a, proj, dtg4, acs4, acs_t4, dsk_exp, norm_g, dm, name):
    Bl, S, DI, N, R, L, G = dm.Bl, dm.S, dm.DI, dm.N, dm.R, CHUNK, N_GROUPS
    RP = R * HEAD_DIM
    nc = S // L
    ob, ocm = DI // N, DI // N + G

    def body(x_ref, z_ref, dtg_ref, acs_ref, acst_ref, dsk_ref, ng_ref, y_ref, yn_ref, hp_ref, h_ref):
        xs_ref, bm_ref, cm_ref = x_ref.at[:, 0:RP], x_ref.at[:, RP:RP + N], x_ref.at[:, RP + N:RP + 2 * N]
        @pl.when(pl.program_id(2) == 0)
        def _():
            h_ref[...] = jnp.zeros_like(h_ref)

        causal = lax.broadcasted_iota(jnp.int32, (L, L), 0) >= lax.broadcasted_iota(jnp.int32, (L, L), 1)
        lane = lax.broadcasted_iota(jnp.int32, (L, LANES), 1)
        dtg, acs, acs_t = dtg_ref[...], acs_ref[...], acst_ref[...]
        xs = xs_ref[...]
        bmb, cmb = bm_ref[...].astype(MXU_DTYPE), cm_ref[...].astype(MXU_DTYPE)
        sg = _dg(cmb, bmb, NT)
        acs_last = acs[L - 1:L, :]
        dt_exp, ea_exp, ds_exp = _expand_heads([dtg, jnp.exp(acs), jnp.exp(acs_last - acs)], R)
        xdt = xs * dt_exp
        xb = xdt.astype(MXU_DTYPE)
        parts = []
        for q in range(R // 2):
            x2 = xb[:, LANES * q:LANES * (q + 1)]
            ys = []
            for r in (2 * q, 2 * q + 1):
                dec = jnp.exp(jnp.where(causal, acs[:, r:r + 1] - acs_t[r:r + 1, :], -1e30))
                ys.append(_dg((sg * dec).astype(MXU_DTYPE), x2, NN))
            parts.append(jnp.where(lane < HEAD_DIM, ys[0], ys[1]))
        ydiag = jnp.concatenate(parts, axis=1)
        h_cur = h_ref[...]
        hb = h_cur.astype(MXU_DTYPE)
        yoff = _dg(cmb, hb, NT) * ea_exp
        st = _dg((xdt * ds_exp).astype(MXU_DTYPE), bmb, TN)
        hp_ref[...] = hb
        h_ref[...] = h_cur * _head_rows(jnp.exp(acs_t[:, L - 1:L]), R, N) + st
        y = ydiag + yoff + dsk_ref[...] * xs
        y_ref[...] = y
        z = z_ref[...]
        yg = y * (z * _sigmoid(z))
        rr = lax.rsqrt(_row_sums(yg * yg) * (1.0 / RP) + EPS)
        yn_ref[...] = (yg * _lanes_to(rr, RP) * ng_ref[...]).astype(yn_ref.dtype)

    oz = dm.OZ // RP
    grp = pl.BlockSpec((None, L, RP), lambda b, g, c: (b, c, g))
    lanes = pl.BlockSpec((None, L, LANES), lambda b, g, c: (b, c, g))
    chan = pl.BlockSpec((1, RP), lambda b, g, c: (0, g))
    return pl.pallas_call(
        body, name=name, grid=(Bl, G, nc),
        in_specs=[pl.BlockSpec((None, L, RP + 2 * N), lambda b, g, c: (b, c, g)),
                  pl.BlockSpec((None, L, RP), lambda b, g, c: (b, c, oz + g)), lanes, lanes,
                  pl.BlockSpec((None, None, 8, L), lambda b, g, c: (b, c, g, 0)),
                  chan, chan],
        out_specs=[grp, grp, pl.BlockSpec((None, None, None, RP, N), lambda b, g, c: (b, g, c, 0, 0))],
        out_shape=[jax.ShapeDtypeStruct((Bl, S, DI), F32), jax.ShapeDtypeStruct((Bl, S, DI), MXU_DTYPE),
                   jax.ShapeDtypeStruct((Bl, G, nc, RP, N), MXU_DTYPE)],
        scratch_shapes=[pltpu.VMEM((RP, N), F32)],
        compiler_params=_cp(("parallel", "parallel", "arbitrary")),
    )(xbc_a, proj, dtg4, acs4, acs_t4, dsk_exp, norm_g.reshape(1, DI))


def _scan_bwd(dyn, y, xbc_a, proj, hprev, dtg4, acs4, acs_t4, dsk_exp, norm_g, dproj, dm, name):
    Bl, S, DI, N, R, L, G = dm.Bl, dm.S, dm.DI, dm.N, dm.R, CHUNK, N_GROUPS
    RP = R * HEAD_DIM
    nc = S // L
    ob, ocm = DI // N, DI // N + G

    def body(dyn_ref, y_ref, z_ref, x_ref, hp_ref, dtg_ref, acs_ref, acst_ref, dsk_ref, ng_ref,
             _, dz_ref, dx_ref, da_ref, ddt_ref, ddsk_ref, dng_ref, dh_ref):
        xs_ref, bm_ref, cm_ref = x_ref.at[:, 0:RP], x_ref.at[:, RP:RP + N], x_ref.at[:, RP + N:RP + 2 * N]
        dxs_ref, dbm_ref, dcm_ref = dx_ref.at[:, 0:RP], dx_ref.at[:, RP:RP + N], dx_ref.at[:, RP + N:RP + 2 * N]
        b, c = pl.program_id(1), pl.program_id(2)

        @pl.when(c == 0)
        def _():
            dh_ref[...] = jnp.zeros_like(dh_ref)

        @pl.when((b == 0) & (c == 0))
        def _():
            ddsk_ref[...] = jnp.zeros_like(ddsk_ref)
            dng_ref[...] = jnp.zeros_like(dng_ref)

        row = lax.broadcasted_iota(jnp.int32, (L, L), 0)
        col = lax.broadcasted_iota(jnp.int32, (L, L), 1)
        causal, anti = row >= col, col >= row
        lane = lax.broadcasted_iota(jnp.int32, (L, LANES), 1)
        etb = (lax.shift_right_logical(lax.broadcasted_iota(jnp.int32, (RP, LANES), 0), 6)
               == lax.broadcasted_iota(jnp.int32, (RP, LANES), 1)).astype(MXU_DTYPE)

        dtg, acs, acs_t = dtg_ref[...], acs_ref[...], acst_ref[...]
        xs, z, y, dyn = xs_ref[...], z_ref[...], y_ref[...], dyn_ref[...]
        bmb, cmb = bm_ref[...].astype(MXU_DTYPE), cm_ref[...].astype(MXU_DTYPE)
        hpb = hp_ref[...]
        ng = ng_ref[...]

        sz = _sigmoid(z)
        siluz = z * sz
        yg = y * siluz
        rr = lax.rsqrt(jnp.mean(yg * yg, axis=-1, keepdims=True) + EPS)
        yhat = yg * rr
        dng_ref[...] += jnp.sum(dyn * yhat, axis=0, keepdims=True)
        dyhat = dyn * ng
        dyg = rr * (dyhat - yhat * jnp.mean(dyhat * yhat, axis=-1, keepdims=True))
        dy = dyg * siluz
        dz_ref[...] = (dyg * y * (sz * (1.0 + z * (1.0 - sz)))).astype(dz_ref.dtype)

        dxs = dy * dsk_ref[...]
        ddsk_ref[...] += jnp.sum(dy * xs, axis=0, keepdims=True)

        acs_last = acs[L - 1:L, :]
        dt_exp = _expand_cols(dtg, R, lane)
        ea_exp = _expand_cols(jnp.exp(acs), R, lane)
        ds_exp = _expand_cols(jnp.exp(acs_last - acs), R, lane)
        xdt = xs * dt_exp
        xb = xdt.astype(MXU_DTYPE)
        dyb = dy.astype(MXU_DTYPE)
        cd = jnp.exp(acs_last)
        cd_rows = _head_rows(jnp.exp(acs_t[:, L - 1:L]), R, N)

        q_ = _dg(cmb, hpb, NT)
        dq = dy * ea_exp
        dqb = dq.astype(MXU_DTYPE)
        dcm = _dg(dqb, hpb, NN)
        dh_yoff = _dg(dqb, cmb, TN)

        dhn = dh_ref[...]
        wprod = dhn * hpb.astype(F32)
        per_head = jnp.concatenate(
            [jnp.sum(wprod[HEAD_DIM * r:HEAD_DIM * (r + 1), :], axis=0, keepdims=True) for r in range(R)]
            + ([jnp.zeros((8 - R, N), F32)] if R < 8 else []), axis=0)
        dcd_col = jnp.sum(per_head, axis=1, keepdims=True)
        diag8 = lax.broadcasted_iota(jnp.int32, (8, LANES), 0) == lax.broadcasted_iota(jnp.int32, (8, LANES), 1)
        dcd_lane = jnp.sum(jnp.where(diag8, dcd_col, 0.0), axis=0, keepdims=True)
        d_a_last = dcd_lane * cd
        dh_ref[...] = dhn * cd_rows + dh_yoff
        dhnb = dhn.astype(MXU_DTYPE)

        e_ = _dg(bmb, dhnb, NT)
        dxdt = ds_exp * e_
        xds = xdt * ds_exp
        dbm = _dg(xds.astype(MXU_DTYPE), dhnb, NN)

        sg = _dg(cmb, bmb, NT)
        sg_t = _dg(bmb, cmb, NT)
        dsg = jnp.zeros((L, L), F32)
        dsg_t = jnp.zeros((L, L), F32)
        d_a = jnp.zeros((L, LANES), F32)
        parts = []
        for q in range(R // 2):
            x2 = xb[:, LANES * q:LANES * (q + 1)]
            dy2 = dyb[:, LANES * q:LANES * (q + 1)]
            dxs2 = []
            for hh, r in enumerate((2 * q, 2 * q + 1)):
                mine = (lane < HEAD_DIM) if hh == 0 else (lane >= HEAD_DIM)
                diff = acs[:, r:r + 1] - acs_t[r:r + 1, :]
                dec = jnp.exp(jnp.where(causal, diff, -1e30))
                dec_t = jnp.exp(jnp.where(anti, -diff, -1e30))
                dy2m = jnp.where(mine, dy2, jnp.zeros_like(dy2))
                dm_ = _dg(dy2m, x2, NT)
                dm_t = _dg(x2, dy2m, NT)
                m_t = sg_t * dec_t
                da_col = jnp.sum(dm_ * (sg * dec) - dm_t * m_t, axis=1, keepdims=True)
                d_a = d_a + jnp.where(lane == r, da_col, 0.0)
                dsg = dsg + dm_ * dec
                dsg_t = dsg_t + dm_t * dec_t
                dxs2.append(_dg(m_t.astype(MXU_DTYPE), dy2, NN))
            parts.append(jnp.where(lane < HEAD_DIM, dxs2[0], dxs2[1]))
        dxdt = dxdt + jnp.concatenate(parts, axis=1)
        dcm_ref[...] = dcm + _dg(dsg.astype(MXU_DTYPE), bmb, NN)
        dbm_ref[...] = dbm + _dg(dsg_t.astype(MXU_DTYPE), cmb, NN)
        dxs_ref[...] = dxs + dxdt * dt_exp

        hs = _two_term_dot(jnp.concatenate([dq * q_ - xds * e_, xds * e_, dxdt * xs], axis=0), etb)
        t2 = hs[L:2 * L, :]
        rowl = lax.broadcasted_iota(jnp.int32, (L, LANES), 0)
        d_a_last = d_a_last + jnp.sum(t2, axis=0, keepdims=True)
        da_ref[...] = d_a + hs[0:L, :] + jnp.where(rowl == L - 1, d_a_last, 0.0)
        ddt_ref[...] = hs[2 * L:3 * L, :]

    oz = dm.OZ // RP
    grp = pl.BlockSpec((None, L, RP), lambda g, b, c: (b, nc - 1 - c, g))
    zspec = pl.BlockSpec((None, L, RP), lambda g, b, c: (b, nc - 1 - c, oz + g))
    lanes = pl.BlockSpec((None, L, LANES), lambda g, b, c: (b, nc - 1 - c, g))
    xg = pl.BlockSpec((None, L, RP + 2 * N), lambda g, b, c: (b, nc - 1 - c, g))
    chan = pl.BlockSpec((1, RP), lambda g, b, c: (0, g))
    wide = jax.ShapeDtypeStruct((Bl, S, G * LANES), F32)
    return pl.pallas_call(
        body, name=name, grid=(G, Bl, nc),
        in_specs=[grp, grp, zspec, xg,
                  pl.BlockSpec((None, None, None, RP, N), lambda g, b, c: (b, g, nc - 1 - c, 0, 0)),
                  lanes, lanes, pl.BlockSpec((None, None, 8, L), lambda g, b, c: (b, nc - 1 - c, g, 0)),
                  chan, chan, _ANY],
        out_specs=[zspec, xg, lanes, lanes, chan, chan],
        out_shape=[jax.ShapeDtypeStruct(dproj.shape, dproj.dtype), jax.ShapeDtypeStruct(xbc_a.shape, F32),
                   wide, wide, jax.ShapeDtypeStruct((1, DI), F32), jax.ShapeDtypeStruct((1, DI), F32)],
        input_output_aliases={10: 0},
        scratch_shapes=[pltpu.VMEM((RP, N), F32)],
        compiler_params=_cp(("arbitrary", "arbitrary", "arbitrary")),
    )(dyn, y, proj, xbc_a, hprev, dtg4, acs4, acs_t4, dsk_exp, norm_g.reshape(1, DI), dproj)


def _adam_update(w, m, v, g):
    c1 = 1.0 - ADAM_B1 ** ADAM_STEP
    c2 = 1.0 - ADAM_B2 ** ADAM_STEP
    nm = ADAM_B1 * m + (1.0 - ADAM_B1) * g
    nv = ADAM_B2 * v + (1.0 - ADAM_B2) * (g * g)
    return nm, nv, -ADAM_LR * ((nm / c1) / (jnp.sqrt(nv / c2) + ADAM_EPS) + ADAM_WD * w)


def _adamw(w, m, v, g, name, tr=128):
    rows, cols = w.shape
    tr = _tile(rows, tr, 8)

    def body(w_ref, m_ref, v_ref, ga_ref, g_ref, d_ref, nm_ref, nv_ref):
        g = ga_ref[...]
        g_ref[...] = g
        nm_ref[...], nv_ref[...], d_ref[...] = _adam_update(w_ref[...], m_ref[...], v_ref[...], g)

    blk = pl.BlockSpec((tr, cols), lambda i: (i, 0))
    shp = jax.ShapeDtypeStruct((rows, cols), F32)
    return pl.pallas_call(
        body, name=name, grid=(rows // tr,), in_specs=[blk] * 4, out_specs=[blk] * 4,
        out_shape=[shp] * 4, compiler_params=_cp(("parallel",)),
    )(w, m, v, g)


def _adamw_layers(w, m, v, g_mine, g_theirs, core, name, tr=128):
    _, rows, cols = w.shape
    tr = _tile(rows, tr, 8)

    def body(c_ref, w_ref, m_ref, v_ref, ga_ref, gb_ref, g_ref, d_ref, nm_ref, nv_ref):
        g = jnp.where(pl.program_id(0) == c_ref[0], ga_ref[...], gb_ref[...])
        g_ref[...] = g
        nm_ref[...], nv_ref[...], d_ref[...] = _adam_update(w_ref[...], m_ref[...], v_ref[...], g)

    lay = pl.BlockSpec((None, tr, cols), lambda l, i, c_ref: (l, i, 0))
    one = pl.BlockSpec((tr, cols), lambda l, i, c_ref: (i, 0))
    shp = jax.ShapeDtypeStruct(w.shape, F32)
    return pl.pallas_call(
        body, name=name,
        grid_spec=pltpu.PrefetchScalarGridSpec(num_scalar_prefetch=1, grid=(2, rows // tr),
                                               in_specs=[lay, lay, lay, one, one], out_specs=[lay] * 4),
        out_shape=[shp] * 4, compiler_params=_cp(("parallel", "parallel")),
    )(core, w, m, v, g_mine, g_theirs)


def _sum_slots(buf, name, tr=256):
    n, rows, cols = buf.shape
    tr = _tile(rows, tr, 8)

    def body(b_ref, o_ref):
        acc = b_ref[0].astype(F32)
        for k in range(1, n):
            acc = acc + b_ref[k].astype(F32)
        o_ref[...] = acc

    return pl.pallas_call(
        body, name=name, grid=(rows // tr,),
        in_specs=[pl.BlockSpec((n, tr, cols), lambda i: (0, i, 0))],
        out_specs=pl.BlockSpec((tr, cols), lambda i: (i, 0)),
        out_shape=jax.ShapeDtypeStruct((rows, cols), F32), compiler_params=_cp(("parallel",)),
    )(buf)


_ANY = pl.BlockSpec(memory_space=pl.ANY)


def _exchange_chips(src, per_dest, name):
    rows, cols = src.shape[-2:]

    def body(in_ref, out_ref, send_sems, recv_sems, local_sem):
        x, y, c = lax.axis_index("x"), lax.axis_index("y"), lax.axis_index("c")
        me = 2 * x + y
        chips = [(1 - x, y), (x, 1 - y), (1 - x, 1 - y)]

        def block(j):
            return in_ref.at[j] if per_dest else in_ref

        mine = pltpu.make_async_copy(block(me), out_ref.at[me], local_sem)
        mine.start()
        sends = []
        for k, (px, py) in enumerate(chips):
            cp = pltpu.make_async_remote_copy(
                src_ref=block(2 * px + py), dst_ref=out_ref.at[me], send_sem=send_sems.at[k],
                recv_sem=recv_sems.at[k], device_id=(px, py, c), device_id_type=MESH)
            cp.start()
            sends.append(cp)
        for k, (px, py) in enumerate(chips):
            pltpu.make_async_remote_copy(
                src_ref=block(me), dst_ref=out_ref.at[2 * px + py], send_sem=send_sems.at[k],
                recv_sem=recv_sems.at[k], device_id=(px, py, c), device_id_type=MESH).wait_recv()
        for cp in sends:
            cp.wait_send()
        mine.wait()

    return pl.pallas_call(
        body, name=name, in_specs=[_ANY], out_specs=_ANY,
        out_shape=jax.ShapeDtypeStruct((N_CHIPS, rows, cols), src.dtype),
        scratch_shapes=[pltpu.SemaphoreType.DMA((3,)), pltpu.SemaphoreType.DMA((3,)), pltpu.SemaphoreType.DMA(())],
    )(src)


def _shard_window(ref, kind, j, lead=()):
    if kind == "slots":
        return ref.at[(j,) + lead]
    r, c = ref.shape[-2] // (N_CHIPS if kind == "rows" else 1), ref.shape[-1] // (N_CHIPS if kind == "cols" else 1)
    full = tuple(slice(None) for _ in range(len(ref.shape) - 2 - len(lead)))
    if kind == "rows":
        return ref.at[lead + full + (pl.ds(pl.multiple_of(j * r, 16), r), slice(None))]
    return ref.at[lead + full + (slice(None), pl.ds(pl.multiple_of(j * c, LANES), c))]


def _gather_kind(shard, axis):
    if axis == 1:
        return "rows"
    return "cols" if shard.shape[2] % LANES == 0 else "slots"


def _gather_weights(shards, axes, name):
    kinds = ["rows" if ax == 1 else "slots" for ax in axes]
    nw = len(shards)

    def out_shape(s, kind):
        d, r, c = s.shape
        shp = {"rows": (d, N_CHIPS * r, c), "cols": (d, r, N_CHIPS * c), "slots": (N_CHIPS, d, r, c)}[kind]
        return jax.ShapeDtypeStruct(shp, s.dtype)

    assert all(s.shape[0] == 2 for s in shards)

    def body(*refs):
        ins, outs = refs[:nw], refs[nw:2 * nw]
        ici_send, ici_recv, d2d_send, d2d_recv, local_sems = refs[2 * nw:]
        x, y, c = lax.axis_index("x"), lax.axis_index("y"), lax.axis_index("c")
        me = 2 * x + y
        chips = [(1 - x, y), (x, 1 - y), (1 - x, 1 - y)]
        sends = []
        for i in range(nw):
            if kinds[i] == "rows":
                own = pltpu.make_async_copy(ins[i], _shard_window(outs[i], kinds[i], me), local_sems.at[i])
                own.start()
                sends.append((own, False))
        for i in range(nw):
            for k, (px, py) in enumerate(chips):
                cp = pltpu.make_async_remote_copy(
                    src_ref=ins[i].at[c], dst_ref=_shard_window(outs[i], kinds[i], me, (c,)),
                    send_sem=ici_send.at[3 * i + k], recv_sem=ici_recv.at[3 * i + k],
                    device_id=(px, py, c), device_id_type=MESH)
                cp.start()
                sends.append((cp, True))
        for i in range(nw):
            for k, (px, py) in enumerate(chips):
                win = _shard_window(outs[i], kinds[i], 2 * px + py, (c,))
                pltpu.make_async_remote_copy(
                    src_ref=ins[i].at[c], dst_ref=win, send_sem=ici_send.at[3 * i + k], recv_sem=ici_recv.at[3 * i + k],
                    device_id=(px, py, c), device_id_type=MESH).wait_recv()
                fw = pltpu.make_async_remote_copy(
                    src_ref=win, dst_ref=win, send_sem=d2d_send.at[3 * i + k], recv_sem=d2d_recv.at[3 * i + k],
                    device_id=(x, y, 1 - c), device_id_type=MESH)
                fw.start()
                sends.append((fw, True))
        for i in range(nw):
            for k, (px, py) in enumerate(chips):
                win = _shard_window(outs[i], kinds[i], 2 * px + py, (1 - c,))
                pltpu.make_async_remote_copy(
                    src_ref=win, dst_ref=win, send_sem=d2d_send.at[3 * i + k], recv_sem=d2d_recv.at[3 * i + k],
                    device_id=(x, y, 1 - c), device_id_type=MESH).wait_recv()
        for cp, remote in sends:
            cp.wait_send() if remote else cp.wait()

    outs = pl.pallas_call(
        body, name=name, in_specs=[_ANY] * nw, out_specs=[_ANY] * nw,
        out_shape=[out_shape(s, k) for s, k in zip(shards, kinds)],
        scratch_shapes=[pltpu.SemaphoreType.DMA((3 * nw,)), pltpu.SemaphoreType.DMA((3 * nw,)),
                        pltpu.SemaphoreType.DMA((3 * nw,)), pltpu.SemaphoreType.DMA((3 * nw,)),
                        pltpu.SemaphoreType.DMA((nw,))],
    )(*shards)
    me = 2 * lax.axis_index("x") + lax.axis_index("y")
    return [jnp.concatenate([jnp.where(me == j, s, o[j]) for j in range(N_CHIPS)], axis=2) if k == "slots" else o
            for s, o, k in zip(shards, outs, kinds)]


def _swap_other_layer(gst, name):
    nv = len(gst)

    def body(*refs):
        ins, outs, send_sems, recv_sems = refs[:nv], refs[nv:2 * nv], refs[2 * nv], refs[2 * nv + 1]
        x, y, c = lax.axis_index("x"), lax.axis_index("y"), lax.axis_index("c")
        cps = [pltpu.make_async_remote_copy(src_ref=ins[i].at[1 - c], dst_ref=outs[i], send_sem=send_sems.at[i],
                                            recv_sem=recv_sems.at[i], device_id=(x, y, 1 - c), device_id_type=MESH)
               for i in range(nv)]
        for cp in cps:
            cp.start()
        for cp in cps:
            cp.wait()

    return pl.pallas_call(
        body, name=name, in_specs=[_ANY] * nv, out_specs=[_ANY] * nv,
        out_shape=[jax.ShapeDtypeStruct(v.shape[1:], v.dtype) for v in gst],
        scratch_shapes=[pltpu.SemaphoreType.DMA((nv,)), pltpu.SemaphoreType.DMA((nv,))],
    )(*gst)


def _pair_sum(g, other, core, name, tr=256):
    _, rows, cols = g.shape
    tr = _tile(rows, tr, 16)

    def body(c_ref, g_ref, o_ref, s_ref):
        s_ref[...] = (g_ref[...].astype(F32) + o_ref[...].astype(F32)).astype(s_ref.dtype)

    blk = pl.BlockSpec((tr, cols), lambda i, c_ref: (i, 0))
    return pl.pallas_call(
        body, name=name,
        grid_spec=pltpu.PrefetchScalarGridSpec(
            num_scalar_prefetch=1, grid=(rows // tr,),
            in_specs=[pl.BlockSpec((None, tr, cols), lambda i, c_ref: (c_ref[0], i, 0)), blk], out_specs=blk),
        out_shape=jax.ShapeDtypeStruct((rows, cols), g.dtype), compiler_params=_cp(("parallel",)),
    )(core, g, other)


def _scatter_layer(parts, kinds, name):
    nw = len(parts)

    def shard_shape(g, kind):
        return g.shape[-2] // (N_CHIPS if kind == "rows" else 1), g.shape[-1] // (N_CHIPS if kind == "cols" else 1)

    def body(*refs):
        ins, outs = refs[:nw], refs[nw:2 * nw]
        send_sems, recv_sems, local_sems = refs[2 * nw:]
        x, y, c = lax.axis_index("x"), lax.axis_index("y"), lax.axis_index("c")
        me = 2 * x + y
        chips = [(1 - x, y), (x, 1 - y), (1 - x, 1 - y)]
        sends = []
        for i in range(nw):
            own = pltpu.make_async_copy(_shard_window(ins[i], kinds[i], me), outs[i].at[me], local_sems.at[i])
            own.start()
            sends.append((own, False))
            for k, (px, py) in enumerate(chips):
                cp = pltpu.make_async_remote_copy(
                    src_ref=_shard_window(ins[i], kinds[i], 2 * px + py), dst_ref=outs[i].at[me],
                    send_sem=send_sems.at[3 * i + k], recv_sem=recv_sems.at[3 * i + k],
                    device_id=(px, py, c), device_id_type=MESH)
                cp.start()
                sends.append((cp, True))
        for i in range(nw):
            for k, (px, py) in enumerate(chips):
                pltpu.make_async_remote_copy(
                    src_ref=_shard_window(ins[i], kinds[i], me), dst_ref=outs[i].at[2 * px + py],
                    send_sem=send_sems.at[3 * i + k], recv_sem=recv_sems.at[3 * i + k],
                    device_id=(px, py, c), device_id_type=MESH).wait_recv()
        for cp, remote in sends:
            cp.wait_send() if remote else cp.wait()

    return pl.pallas_call(
        body, name=name, in_specs=[_ANY] * nw, out_specs=[_ANY] * nw,
        out_shape=[jax.ShapeDtypeStruct((N_CHIPS,) + shard_shape(g, k), g.dtype) for g, k in zip(parts, kinds)],
        scratch_shapes=[pltpu.SemaphoreType.DMA((3 * nw,)), pltpu.SemaphoreType.DMA((3 * nw,)),
                        pltpu.SemaphoreType.DMA((nw,))],
    )(*parts)


def _sibling_swap(vs, name):
    nv = len(vs)

    def body(*refs):
        ins, outs, send_sems, recv_sems = refs[:nv], refs[nv:2 * nv], refs[2 * nv], refs[2 * nv + 1]
        x, y, c = lax.axis_index("x"), lax.axis_index("y"), lax.axis_index("c")
        cps = [pltpu.make_async_remote_copy(src_ref=ins[i], dst_ref=outs[i], send_sem=send_sems.at[i],
                                            recv_sem=recv_sems.at[i], device_id=(x, y, 1 - c), device_id_type=MESH)
               for i in range(nv)]
        for cp in cps:
            cp.start()
        for cp in cps:
            cp.wait()

    return pl.pallas_call(
        body, name=name, in_specs=[_ANY] * nv, out_specs=[_ANY] * nv,
        out_shape=[jax.ShapeDtypeStruct(v.shape, v.dtype) for v in vs],
        scratch_shapes=[pltpu.SemaphoreType.DMA((nv,)), pltpu.SemaphoreType.DMA((nv,))],
    )(*vs)


def _allgather_all(v, name):
    rows, cols = v.shape

    def body(in_ref, out_ref, send_sems, recv_sems, local_sem):
        x, y, c = lax.axis_index("x"), lax.axis_index("y"), lax.axis_index("c")
        me = 4 * x + 2 * y + c
        peers = []
        for k in range(1, N_DEV):
            peers.append(((1 - x) if k & 4 else x, (1 - y) if k & 2 else y, (1 - c) if k & 1 else c))
        mine = pltpu.make_async_copy(in_ref, out_ref.at[me], local_sem)
        mine.start()
        sends = []
        for k, peer in enumerate(peers):
            cp = pltpu.make_async_remote_copy(src_ref=in_ref, dst_ref=out_ref.at[me], send_sem=send_sems.at[k],
                                              recv_sem=recv_sems.at[k], device_id=peer, device_id_type=MESH)
            cp.start()
            sends.append(cp)
        for k, (px, py, pc) in enumerate(peers):
            pltpu.make_async_remote_copy(src_ref=in_ref, dst_ref=out_ref.at[4 * px + 2 * py + pc],
                                         send_sem=send_sems.at[k], recv_sem=recv_sems.at[k],
                                         device_id=(px, py, pc), device_id_type=MESH).wait_recv()
        for cp in sends:
            cp.wait_send()
        mine.wait()

    return pl.pallas_call(
        body, name=name, in_specs=[_ANY], out_specs=_ANY,
        out_shape=jax.ShapeDtypeStruct((N_DEV, rows, cols), v.dtype),
        scratch_shapes=[pltpu.SemaphoreType.DMA((N_DEV - 1,)), pltpu.SemaphoreType.DMA((N_DEV - 1,)),
                        pltpu.SemaphoreType.DMA(())],
    )(v)


def _pack(arrs, dtype, width, row_mult):
    flat = jnp.concatenate([a.reshape(-1).astype(dtype) for a in arrs])
    unit = width * row_mult
    total = -(-flat.shape[0] // unit) * unit
    return jnp.pad(flat, (0, total - flat.shape[0])).reshape(-1, width)


def _unpack(buf, shapes):
    flat = buf.reshape(-1)
    out, off = [], 0
    for shp in shapes:
        n = 1
        for d in shp:
            n *= d
        out.append(flat[off:off + n].reshape(shp))
        off += n
    return out


class _Dims:
    pass


def _dims(x, ssd_dt_bias, ssd_norm_g, ssd_conv_b, ffn_conv_b):
    dm = _Dims()
    dm.Bl, dm.S, dm.D = x.shape
    dm.H, dm.DI, dm.CD = ssd_dt_bias.shape[-1], ssd_norm_g.shape[-1], ssd_conv_b.shape[-1]
    dm.N = (dm.CD - dm.DI) // (2 * N_GROUPS)
    dm.R = dm.H // N_GROUPS
    dm.DFF = ffn_conv_b.shape[-1] // 2
    D = dm.D
    dm.OB, dm.OC, dm.OH, dm.OX = 0, D, 2 * D, 3 * D
    dm.OZ = dm.OX + dm.CD
    dm.OG = -(-(dm.OZ + dm.DI) // (2 * D)) * (2 * D)
    dm.GPAD = dm.OG - (dm.OZ + dm.DI)
    dm.ODT = dm.OG + 2 * D
    dm.NP = dm.ODT + LANES
    assert dm.OZ % (dm.R * HEAD_DIM) == 0
    assert dm.DI // dm.H == HEAD_DIM and dm.N == LANES and dm.R % 2 == 0 and dm.S % CHUNK == 0 and dm.H <= LANES
    return dm


def _group_xbc(v, dm):
    rp, n = dm.R * HEAD_DIM, dm.N
    parts = []
    for g in range(N_GROUPS):
        parts += [v[..., g * rp:(g + 1) * rp], v[..., dm.DI + g * n:dm.DI + (g + 1) * n],
                  v[..., dm.DI + (N_GROUPS + g) * n:dm.DI + (N_GROUPS + g + 1) * n]]
    return jnp.concatenate(parts, axis=-1)


def _ungroup_xbc(v, dm):
    rp, n = dm.R * HEAD_DIM, dm.N
    gw = rp + 2 * n
    xs = [v[..., g * gw:g * gw + rp] for g in range(N_GROUPS)]
    bs = [v[..., g * gw + rp:g * gw + rp + n] for g in range(N_GROUPS)]
    cs = [v[..., g * gw + rp + n:(g + 1) * gw] for g in range(N_GROUPS)]
    return jnp.concatenate(xs + bs + cs, axis=-1)


def _permute_w_in(w, dm):
    o, sc = dm.DI + dm.CD, dm.DI + dm.CD + dm.H
    zeros = lambda n: jnp.zeros((w.shape[0], n), w.dtype)
    return jnp.concatenate([w[:, sc:sc + 3 * dm.D], _group_xbc(w[:, dm.DI:o], dm), w[:, :dm.DI], zeros(dm.GPAD),
                            w[:, sc + 3 * dm.D:], w[:, o:o + dm.H], zeros(LANES - dm.H)], axis=1)


def _unpermute_w_in(dw, dm):
    return jnp.concatenate([dw[..., dm.OZ:dm.OZ + dm.DI], _ungroup_xbc(dw[..., dm.OX:dm.OX + dm.CD], dm),
                            dw[..., dm.ODT:dm.ODT + dm.H], dw[..., :3 * dm.D], dw[..., dm.OG:dm.OG + 2 * dm.D]], axis=-1)


def _lane_pad(v):
    return jnp.pad(v.reshape(1, -1).astype(F32), ((0, 0), (0, LANES - v.shape[-1])))


def _pad8(w):
    return jnp.pad(w.astype(F32), ((0, 8 - w.shape[0]), (0, 0)))


def _head_select(dm):
    j = jnp.arange(LANES)[None, :, None]
    r = jnp.arange(LANES)[None, None, :]
    g = jnp.arange(N_GROUPS)[:, None, None]
    sel = ((j == dm.R * g + r) & (r < dm.R)).astype(F32)
    selcat = jnp.transpose(sel, (1, 0, 2)).reshape(LANES, N_GROUPS * LANES)
    selbig = jnp.transpose(sel[:, :, :8], (0, 2, 1)).reshape(N_GROUPS * 8, LANES)
    return selcat, selbig


def _mix_fwd(dm, h, w, sp, sel, tag):
    Bl, S, D = dm.Bl, dm.S, dm.D
    T = Bl * S
    proj = _matmul(h.reshape(T, D), w["w_in_p"], "NN", F32, tag + "_in_proj", tm=2048, tn=1152).reshape(Bl, S, dm.NP)
    xbc_a = _ssd_conv_fwd(proj, sp["ssd_conv_w8"], sp["ssd_conv_b"], dm.OX, dm.CD, tag + "_ssd_conv")
    dtg, acs, acs_t = _ssd_prep(proj, sp["dt_bias"], sp["a_log"], sel[0], sel[1], dm, tag + "_ssd_prep")
    y, yn, hprev = _scan_fwd(xbc_a, proj, dtg, acs, acs_t, sp["dsk_exp"], sp["ssd_norm_g"], dm, tag + "_ssd_scan")
    y_ssd = _matmul(yn.reshape(T, dm.DI), w["w_ssd_out"], "NN", F32, tag + "_ssd_out", tk=2048).reshape(Bl, S, D)
    s = _sc_conv_fwd(proj, sp["sc_conv_w8"], (dm.OB, dm.OC, dm.OH), D, tag + "_sc_conv")
    y_sc = _matmul(s.reshape(T, D), w["w_sc_out"], "NN", F32, tag + "_sc_out", tk=1024).reshape(Bl, S, D)
    mixin = _merge_fwd(proj, y_ssd, y_sc, dm.OG, D, tag + "_merge")
    mix = _matmul(mixin.reshape(T, D), w["w_o"], "NN", F32, tag + "_o", tk=1024).reshape(Bl, S, D)
    return mix, (h, proj, xbc_a, y, yn, hprev, y_ssd, y_sc, s, mixin, dtg, acs, acs_t)


def _mix_bwd(dm, dmix, saved, w, sp, sel, tag, gbig, slab):
    Bl, S, D = dm.Bl, dm.S, dm.D
    T = Bl * S
    h, proj, xbc_a, y, yn, hprev, y_ssd, y_sc, s, mixin, dtg, acs, acs_t = saved
    dmix2 = dmix.reshape(T, D)
    g = {}
    gbig["w_o"] = _matmul(mixin.reshape(T, D), dmix2, "TN", WIRE_DTYPE, tag + "_dw_o", slab=(gbig.get("w_o"),) + slab)
    dmixin = _matmul(dmix2, w["w_o"], "NT", F32, tag + "_d_o", tk=1024).reshape(Bl, S, D)
    dproj0 = jnp.zeros((Bl, S, dm.NP), MXU_DTYPE) if dm.GPAD else None
    dy_ssd, dy_sc, dproj = _merge_bwd(dmixin, proj, y_ssd, y_sc, dm.OG, D, dproj0, tag + "_merge_bwd")
    gbig["w_sc_out"] = _matmul(s.reshape(T, D), dy_sc.reshape(T, D), "TN", WIRE_DTYPE, tag + "_dw_sc_out",
                               slab=(gbig.get("w_sc_out"),) + slab)
    ds = _matmul(dy_sc.reshape(T, D), w["w_sc_out"], "NT", F32, tag + "_d_sc_out", tk=1024).reshape(Bl, S, D)
    dproj, dscw = _sc_conv_bwd(ds, proj, sp["sc_conv_w8"], (dm.OB, dm.OC, dm.OH), D, dproj, tag + "_sc_conv_bwd")
    g["sc_conv_w"] = dscw[:3]
    gbig["w_ssd_out"] = _matmul(yn.reshape(T, dm.DI), dy_ssd.reshape(T, D), "TN", WIRE_DTYPE, tag + "_dw_ssd_out",
                                slab=(gbig.get("w_ssd_out"),) + slab)
    dyn = _matmul(dy_ssd.reshape(T, D), w["w_ssd_out"], "NT", F32, tag + "_d_ssd_out", tk=1024).reshape(Bl, S, dm.DI)
    dproj, dxa, d_a, ddt, ddsk, dng = _scan_bwd(dyn, y, xbc_a, proj, hprev, dtg, acs, acs_t, sp["dsk_exp"],
                                                sp["ssd_norm_g"], dproj, dm, tag + "_ssd_scan_bwd")
    dproj, dpar = _ssd_post(d_a, ddt, dtg, proj, sp["dt_bias"], sp["a_log"], sel[0], dproj, dm, tag + "_ssd_post")
    g["ssd_dt_bias"], g["ssd_a_log"] = dpar[0, :dm.H], dpar[1, :dm.H]
    g["ssd_d"] = jnp.sum(ddsk.reshape(dm.H, HEAD_DIM), axis=-1)
    g["ssd_norm_g"] = dng[0]
    dproj, dcw, dcb = _ssd_conv_bwd(dxa, proj, sp["ssd_conv_w8"], sp["ssd_conv_b"], dm.OX, dm.CD, dproj,
                                    tag + "_ssd_conv_bwd")
    g["ssd_conv_w"], g["ssd_conv_b"] = _ungroup_xbc(dcw[:4], dm), _ungroup_xbc(dcb[0], dm)
    dproj = dproj.reshape(T, dm.NP)
    gbig["w_in_p"] = _matmul(h.reshape(T, D), dproj, "TN", WIRE_DTYPE, tag + "_dw_in", tn=1728,
                             slab=(gbig.get("w_in_p"),) + slab)
    dh = _matmul(dproj, w["w_in_p"], "NT", F32, tag + "_d_in", tm=2048, tk=1152).reshape(Bl, S, D)
    return dh, g


def _ffn_fwd(dm, h, w, sp, tag):
    Bl, S, D = dm.Bl, dm.S, dm.D
    T = Bl * S
    up = _matmul(h.reshape(T, D), w["w_up"], "NN", F32, tag + "_up", tm=2048, tn=1408).reshape(Bl, S, 2 * dm.DFF)
    a = _ffn_conv_fwd(up, sp["ffn_conv_w8"], sp["ffn_conv_b"], dm.DFF, tag + "_ffn_conv")
    f = _matmul(a.reshape(T, dm.DFF), w["w_down"], "NN", F32, tag + "_down", tk=2816).reshape(Bl, S, D)
    return f, (h, up, a)


def _ffn_bwd(dm, df, saved, w, sp, tag, gbig, slab):
    Bl, S, D = dm.Bl, dm.S, dm.D
    T = Bl * S
    h, up, a = saved
    df2 = df.reshape(T, D)
    g = {}
    gbig["w_down"] = _matmul(a.reshape(T, dm.DFF), df2, "TN", WIRE_DTYPE, tag + "_dw_down", tm=1408,
                             slab=(gbig.get("w_down"),) + slab)
    da = _matmul(df2, w["w_down"], "NT", F32, tag + "_d_down", tn=1408).reshape(Bl, S, dm.DFF)
    dg, dv, dcw, dcb = _ffn_conv_bwd(da, up, sp["ffn_conv_w8"], sp["ffn_conv_b"], dm.DFF, tag + "_ffn_conv_bwd")
    g["ffn_conv_w"], g["ffn_conv_b"] = dcw[:3], dcb[0]
    dup = jnp.concatenate([dg, dv], axis=-1).reshape(T, 2 * dm.DFF)
    gbig["w_up"] = _matmul(h.reshape(T, D), dup, "TN", WIRE_DTYPE, tag + "_dw_up", tn=1408,
                           slab=(gbig.get("w_up"),) + slab)
    dh = _matmul(dup, w["w_up"], "NT", F32, tag + "_d_up", tm=2048, tk=1408).reshape(Bl, S, D)
    return dh, g


def _local_step(dm, x, c, target, wfull, small):
    Bl, S, D = dm.Bl, dm.S, dm.D
    depth = len(wfull)
    sel = _head_select(dm)
    c16 = jnp.pad(c.astype(F32), ((0, 16 - Bl), (0, 0)))
    sps, mods, acts = [], [], []
    for l in range(depth):
        sm = small[l]
        sps.append(dict(
            ssd_conv_w8=_pad8(_group_xbc(sm["ssd_conv_w"], dm)), ssd_conv_b=_group_xbc(sm["ssd_conv_b"], dm),
            dt_bias=_lane_pad(sm["ssd_dt_bias"]),
            a_log=_lane_pad(sm["ssd_a_log"]), dsk_exp=jnp.repeat(sm["ssd_d"].astype(F32), HEAD_DIM).reshape(1, dm.DI),
            ssd_norm_g=sm["ssd_norm_g"],
            sc_conv_w8=_pad8(sm["sc_conv_w"]), ffn_conv_w8=_pad8(sm["ffn_conv_w"]), ffn_conv_b=sm["ffn_conv_b"]))
        act, mod = _ada_fwd(c16, wfull[l]["ada_w"], sm["ada_b"], f"l{l}_ada")
        acts.append(act)
        mods.append(jnp.pad(mod[:Bl].reshape(Bl, 6, D), ((0, 0), (0, 2), (0, 0))))

    def sub(i):
        l, ffn = i // 2, i % 2
        sm = small[l]
        return dict(l=l, ffn=ffn, pre_g=sm["ffn_pre_g" if ffn else "mix_pre_g"],
                    post_g=sm["ffn_post_g" if ffn else "mix_post_g"], mod=mods[l], row=3 * ffn,
                    tag=f"l{l}_{'ffn' if ffn else 'mix'}")

    nsub = 2 * depth
    subs = [sub(i) for i in range(nsub)]
    xs, fs, saves = [x], [], []
    h = _norm_mod_fwd(x, subs[0]["pre_g"], subs[0]["mod"], subs[0]["row"], "l0_mix_pre_norm")
    for i, sb in enumerate(subs):
        l = sb["l"]
        if sb["ffn"]:
            f, sv = _ffn_fwd(dm, h, wfull[l], sps[l], sb["tag"])
        else:
            f, sv = _mix_fwd(dm, h, wfull[l], sps[l], sel, sb["tag"])
        nxt = None
        if i + 1 < nsub:
            nb = subs[i + 1]
            nxt = (nb["pre_g"], nb["mod"], nb["row"])
        xn, h = _post_norm_fwd(xs[-1], f, sb["post_g"], sb["mod"], sb["row"] + 2, nxt, sb["tag"] + "_post_norm")
        xs.append(xn)
        fs.append(f)
        saves.append(sv)

    dy, loss = _loss_fwd_bwd(xs[-1], target, "loss")

    grads = [dict() for _ in range(depth)]
    gbig = {}
    dmod = [[None] * 6 for _ in range(depth)]
    dx, dh = dy, None
    for i in reversed(range(nsub)):
        sb = subs[i]
        l = sb["l"]
        nxt = None
        if i + 1 < nsub:
            nb = subs[i + 1]
            nxt = (dh, xs[i + 1], nb["pre_g"], nb["mod"], nb["row"])
        dx, df, pb, shg = _norm_bwd(dx, nxt, (fs[i], sb["post_g"], sb["mod"], sb["row"] + 2), sb["tag"] + "_post_norm_bwd")
        if nxt is not None:
            nb = subs[i + 1]
            dmod[nb["l"]][nb["row"]], dmod[nb["l"]][nb["row"] + 1] = pb[:, 0], pb[:, 1]
            grads[nb["l"]]["ffn_pre_g" if nb["ffn"] else "mix_pre_g"] = shg[0]
        dmod[l][sb["row"] + 2] = pb[:, 2]
        grads[l]["ffn_post_g" if sb["ffn"] else "mix_post_g"] = shg[1]
        if sb["ffn"]:
            dh, g = _ffn_bwd(dm, df, saves[i], wfull[l], sps[l], sb["tag"], gbig, (l, depth))
        else:
            dh, g = _mix_bwd(dm, df, saves[i], wfull[l], sps[l], sel, sb["tag"], gbig, (l, depth))
        grads[l].update(g)
    sb = subs[0]
    grad_x, _, pb, shg = _norm_bwd(dx, (dh, xs[0], sb["pre_g"], sb["mod"], sb["row"]), None, "l0_mix_pre_norm_bwd")
    dmod[0][0], dmod[0][1] = pb[:, 0], pb[:, 1]
    grads[0]["mix_pre_g"] = shg[0]

    for l in range(depth):
        dm6 = jnp.concatenate(dmod[l], axis=-1)
        grads[l]["ada_b"] = jnp.sum(dm6, axis=0)
        dm16 = jnp.pad(dm6, ((0, 16 - Bl), (0, 0))).astype(MXU_DTYPE)
        gbig["ada_w"] = _matmul(acts[l], dm16, "TN", WIRE_DTYPE, f"l{l}_dw_ada", slab=(gbig.get("ada_w"), l, depth))
    return loss, grad_x, grads, gbig


_WEIGHTS = ("ada_w", "ada_b", "mix_pre_g", "mix_post_g", "w_in", "ssd_conv_w", "ssd_conv_b", "ssd_dt_bias",
            "ssd_a_log", "ssd_d", "ssd_norm_g", "w_ssd_out", "sc_conv_w", "w_sc_out", "w_o", "ffn_pre_g",
            "ffn_post_g", "w_up", "ffn_conv_w", "ffn_conv_b", "w_down")
_INPUTS = ("x", "c") + _WEIGHTS + ("loss_target",) + tuple("m_" + n for n in _WEIGHTS) + tuple("v_" + n for n in _WEIGHTS)
_BIG = {"ada_w": 2, "w_in": 2, "w_ssd_out": 1, "w_sc_out": 1, "w_o": 1, "w_up": 2, "w_down": 1}
_CONV = ("ssd_conv_w", "sc_conv_w", "ffn_conv_w")
_SMALL = tuple(n for n in _WEIGHTS if n not in _BIG)


def _step(a):
    x, c, target = a["x"], a["c"], a["loss_target"]
    depth = a["ada_w"].shape[0]
    dm = _dims(x, a["ssd_dt_bias"], a["ssd_norm_g"], a["ssd_conv_b"], a["ffn_conv_b"])
    chip = 2 * lax.axis_index("x") + lax.axis_index("y")

    shards = [a[n].astype(WIRE_DTYPE) for n in _BIG]
    axes = list(_BIG.values())
    kinds = [_gather_kind(s, ax) for s, ax in zip(shards, axes)]
    full = {n: w.astype(MXU_DTYPE) for n, w in zip(_BIG, _gather_weights(shards, axes, "gather_weights"))}
    conv_shapes = [a[n].shape for n in _CONV]
    gotc = _exchange_chips(_pack([a[n] for n in _CONV], F32, LANES, 8), False, "gather_conv_weights")
    piecesc = [_unpack(gotc[j], conv_shapes) for j in range(N_CHIPS)]
    fullc = {n: jnp.concatenate([piecesc[j][i] for j in range(N_CHIPS)], axis=2) for i, n in enumerate(_CONV)}

    wfull, small = [], []
    for l in range(depth):
        wf = {n: full[n][l] for n in _BIG if n != "w_in"}
        wf["w_in_p"] = _permute_w_in(full["w_in"][l], dm)
        wfull.append(wf)
        small.append({n: (fullc[n][l] if n in _CONV else a[n][l]) for n in _SMALL})

    loss_part, grad_x, grads, gbig = _local_step(dm, x, c, target, wfull, small)

    core = lax.axis_index("c").astype(jnp.int32).reshape(1)
    gst = []
    for n, kind in zip(_BIG, kinds):
        g = _unpermute_w_in(gbig["w_in_p"], dm) if n == "w_in" else gbig[n]
        if kind == "slots":
            g = jnp.moveaxis(g.reshape(depth, g.shape[1], N_CHIPS, g.shape[2] // N_CHIPS), 2, 1)
            g = g.reshape(depth, -1, g.shape[-1])
        gst.append(g)
    other = _swap_other_layer(gst, "swap_layer_grads")
    parts = [_pair_sum(g, o, core, "pair_sum_" + n) for g, o, n in zip(gst, other, _BIG)]
    parts = [p.reshape(N_CHIPS, -1, p.shape[-1]) if k == "slots" else p for p, k in zip(parts, kinds)]
    got = _scatter_layer(parts, kinds, "scatter_grads")
    mine = [_sum_slots(g, "sum_chip_grads_" + n) for g, n in zip(got, _BIG)]
    theirs = _sibling_swap(mine, "swap_core_grads")

    out = {}
    for i, n in enumerate(_BIG):
        out[n] = _adamw_layers(a[n], a["m_" + n], a["v_" + n], mine[i], theirs[i], core, "adamw_" + n)

    gsmall = [jnp.stack([grads[l][n] for l in range(depth)]) for n in _SMALL]
    small_shapes = [g.shape for g in gsmall]
    summed = _sum_slots(_allgather_all(_pack(gsmall, F32, LANES, 8), "gather_small_grads"), "sum_small_grads")
    gs = dict(zip(_SMALL, _unpack(summed, small_shapes)))
    for n in _CONV:
        wcols = a[n].shape[2]
        gs[n] = lax.dynamic_slice_in_dim(gs[n], chip * wcols, wcols, axis=2)
    local_shapes = [a[n].shape for n in _SMALL]
    res = _adamw(_pack([a[n] for n in _SMALL], F32, LANES, 8), _pack([a["m_" + n] for n in _SMALL], F32, LANES, 8),
                 _pack([a["v_" + n] for n in _SMALL], F32, LANES, 8), _pack([gs[n] for n in _SMALL], F32, LANES, 8),
                 "adamw_small")
    res = [_unpack(r, local_shapes) for r in res]
    for i, n in enumerate(_SMALL):
        out[n] = [r[i] for r in res]

    loss = lax.psum(loss_part, ("x", "y", "c"))
    return (loss, grad_x) + tuple(out[n][k] for k in range(4) for n in _WEIGHTS)


def kernel(x, c, ada_w, ada_b, mix_pre_g, mix_post_g, w_in, ssd_conv_w, ssd_conv_b, ssd_dt_bias, ssd_a_log, ssd_d, ssd_norm_g, w_ssd_out, sc_conv_w, w_sc_out, w_o, ffn_pre_g, ffn_post_g, w_up, ffn_conv_w, ffn_conv_b, w_down, loss_target, m_ada_w, m_ada_b, m_mix_pre_g, m_mix_post_g, m_w_in, m_ssd_conv_w, m_ssd_conv_b, m_ssd_dt_bias, m_ssd_a_log, m_ssd_d, m_ssd_norm_g, m_w_ssd_out, m_sc_conv_w, m_w_sc_out, m_w_o, m_ffn_pre_g, m_ffn_post_g, m_w_up, m_ffn_conv_w, m_ffn_conv_b, m_w_down, v_ada_w, v_ada_b, v_mix_pre_g, v_mix_post_g, v_w_in, v_ssd_conv_w, v_ssd_conv_b, v_ssd_dt_bias, v_ssd_a_log, v_ssd_d, v_ssd_norm_g, v_w_ssd_out, v_sc_conv_w, v_w_sc_out, v_w_o, v_ffn_pre_g, v_ffn_post_g, v_w_up, v_ffn_conv_w, v_ffn_conv_b, v_w_down):
    return _step(dict(zip(_INPUTS, (
        x, c, ada_w, ada_b, mix_pre_g, mix_post_g, w_in, ssd_conv_w, ssd_conv_b, ssd_dt_bias, ssd_a_log, ssd_d, ssd_norm_g, w_ssd_out, sc_conv_w, w_sc_out, w_o, ffn_pre_g, ffn_post_g, w_up, ffn_conv_w, ffn_conv_b, w_down, loss_target, m_ada_w, m_ada_b, m_mix_pre_g, m_mix_post_g, m_w_in, m_ssd_conv_w, m_ssd_conv_b, m_ssd_dt_bias, m_ssd_a_log, m_ssd_d, m_ssd_norm_g, m_w_ssd_out, m_sc_conv_w, m_w_sc_out, m_w_o, m_ffn_pre_g, m_ffn_post_g, m_w_up, m_ffn_conv_w, m_ffn_conv_b, m_w_down, v_ada_w, v_ada_b, v_mix_pre_g, v_mix_post_g, v_w_in, v_ssd_conv_w, v_ssd_conv_b, v_ssd_dt_bias, v_ssd_a_log, v_ssd_d, v_ssd_norm_g, v_w_ssd_out, v_sc_conv_w, v_w_sc_out, v_w_o, v_ffn_pre_g, v_ffn_post_g, v_w_up, v_ffn_conv_w, v_ffn_conv_b, v_w_down))))
```

```python
import math

import jax
import jax.numpy as jnp
from jax import lax
from jax.experimental import pallas as pl
from jax.experimental.pallas import tpu as pltpu

F32 = jnp.float32
MXU_DTYPE = jnp.bfloat16
WIRE_DTYPE = jnp.bfloat16
EPS = 1e-6
N_GROUPS = 4
CHUNK = 128
HEAD_DIM = 64
LANES = 128
HALO = 8
N_CHIPS = 4
N_DEV = 8
VMEM_LIMIT = 56 * 1024 * 1024
ADAM_LR, ADAM_B1, ADAM_B2, ADAM_EPS, ADAM_WD, ADAM_STEP = 0.001, 0.9, 0.999, 1e-08, 0.01, 10
MESH = pl.DeviceIdType.MESH

NN = (((1,), (0,)), ((), ()))
NT = (((1,), (1,)), ((), ()))
TN = (((0,), (0,)), ((), ()))


def _dg(a, b, dn, precision=None):
    return lax.dot_general(a, b, dn, precision=precision, preferred_element_type=F32)


def _tile(dim, pref, mult=LANES):
    t = (min(pref, dim) // mult) * mult
    while t >= mult:
        if dim % t == 0:
            return t
        t -= mult
    return dim


def _cp(sem):
    return pltpu.CompilerParams(dimension_semantics=sem, vmem_limit_bytes=VMEM_LIMIT)


def _sigmoid(x):
    return 1.0 / (1.0 + jnp.exp(-x))


def _softplus(x):
    return jnp.maximum(x, 0.0) + jnp.log1p(jnp.exp(-jnp.abs(x)))


def _matmul(a, b, mode, out_dtype, name, tm=1024, tn=1024, tk=1024, slab=None):
    if mode == "NN":
        (M, K), N = a.shape, b.shape[1]
    elif mode == "NT":
        (M, K), N = a.shape, b.shape[0]
    else:
        (K, M), N = a.shape, b.shape[1]
    tm, tn, tk = _tile(M, tm), _tile(N, tn), _tile(K, tk)
    nk = K // tk
    dn = {"NN": NN, "NT": NT, "TN": TN}[mode]
    carry = slab is not None and slab[0] is not None

    def body_one(a_ref, b_ref, *rest):
        o_ref = rest[-1]
        o_ref[...] = _dg(a_ref[...], b_ref[...], dn).astype(o_ref.dtype)

    def body_acc(a_ref, b_ref, *rest):
        o_ref, acc_ref = rest[-2:]
        k = pl.program_id(2)

        @pl.when(k == 0)
        def _():
            acc_ref[...] = jnp.zeros_like(acc_ref)

        acc_ref[...] += _dg(a_ref[...], b_ref[...], dn)

        @pl.when(k == nk - 1)
        def _():
            o_ref[...] = acc_ref[...].astype(o_ref.dtype)

    a_spec = (pl.BlockSpec((tk, tm), lambda i, j, k: (k, i)) if mode == "TN"
              else pl.BlockSpec((tm, tk), lambda i, j, k: (i, k)))
    b_spec = (pl.BlockSpec((tn, tk), lambda i, j, k: (j, k)) if mode == "NT"
              else pl.BlockSpec((tk, tn), lambda i, j, k: (k, j)))
    if slab is None:
        out_spec = pl.BlockSpec((tm, tn), lambda i, j, k: (i, j))
        out_shape = jax.ShapeDtypeStruct((M, N), out_dtype)
    else:
        layer = slab[1]
        out_spec = pl.BlockSpec((None, tm, tn), lambda i, j, k: (layer, i, j))
        out_shape = jax.ShapeDtypeStruct((slab[2], M, N), out_dtype)
    return pl.pallas_call(
        body_one if nk == 1 else body_acc, name=name, grid=(M // tm, N // tn, nk),
        in_specs=[a_spec, b_spec] + ([_ANY] if carry else []),
        out_specs=out_spec, out_shape=out_shape,
        input_output_aliases={2: 0} if carry else {},
        scratch_shapes=[] if nk == 1 else [pltpu.VMEM((tm, tn), F32)],
        compiler_params=_cp(("parallel", "parallel", "arbitrary")),
    )(*((a, b, slab[0]) if carry else (a, b)))


def _ada_fwd(c16, ada_w, ada_b, name):
    rows, D = c16.shape
    N6 = ada_w.shape[1]
    tn = _tile(N6, 1536)

    def body(c_ref, w_ref, b_ref, act_ref, mod_ref):
        c = c_ref[...]
        act = (c * _sigmoid(c)).astype(act_ref.dtype)
        act_ref[...] = act
        mod_ref[...] = _dg(act, w_ref[...], NN) + b_ref[...]

    return pl.pallas_call(
        body, name=name, grid=(N6 // tn,),
        in_specs=[pl.BlockSpec((rows, D), lambda j: (0, 0)),
                  pl.BlockSpec((D, tn), lambda j: (0, j)),
                  pl.BlockSpec((1, tn), lambda j: (0, j))],
        out_specs=[pl.BlockSpec((rows, D), lambda j: (0, 0)),
                   pl.BlockSpec((rows, tn), lambda j: (0, j))],
        out_shape=[jax.ShapeDtypeStruct((rows, D), MXU_DTYPE),
                   jax.ShapeDtypeStruct((rows, N6), F32)],
        compiler_params=_cp(("arbitrary",)),
    )(c16, ada_w, ada_b.reshape(1, N6))


def _norm_mod_rows(x, g, sc, sh):
    r = lax.rsqrt(jnp.mean(x * x, axis=-1, keepdims=True) + EPS)
    return ((x * r) * g) * (1.0 + sc) + sh


def _norm_mod_fwd(x, g, mod, row_sh, name, ts=512):
    Bl, S, D = x.shape
    ts = _tile(S, ts, 8)

    def body(x_ref, g_ref, mod_ref, h_ref):
        sh = mod_ref[row_sh:row_sh + 1, :]
        sc = mod_ref[row_sh + 1:row_sh + 2, :]
        h_ref[...] = _norm_mod_rows(x_ref[...], g_ref[...], sc, sh).astype(h_ref.dtype)

    tok = pl.BlockSpec((None, ts, D), lambda b, s: (b, s, 0))
    return pl.pallas_call(
        body, name=name, grid=(Bl, S // ts),
        in_specs=[tok, pl.BlockSpec((1, D), lambda b, s: (0, 0)),
                  pl.BlockSpec((None, 8, D), lambda b, s: (b, 0, 0))],
        out_specs=tok, out_shape=jax.ShapeDtypeStruct((Bl, S, D), MXU_DTYPE),
        compiler_params=_cp(("parallel", "parallel")),
    )(x, g.reshape(1, D), mod)


def _post_norm_fwd(xp, f, post_g, mod, row_gt, nxt, name, ts=512):
    Bl, S, D = xp.shape
    ts = _tile(S, ts, 8)
    has_next = nxt is not None

    def body(*refs):
        if has_next:
            xp_ref, f_ref, pg_ref, mod_ref, ng_ref, nmod_ref, x_ref, h_ref = refs
        else:
            xp_ref, f_ref, pg_ref, mod_ref, x_ref = refs
        f = f_ref[...]
        r = lax.rsqrt(jnp.mean(f * f, axis=-1, keepdims=True) + EPS)
        x = xp_ref[...] + mod_ref[row_gt:row_gt + 1, :] * ((f * r) * pg_ref[...])
        x_ref[...] = x
        if has_next:
            rs = nxt[2]
            h_ref[...] = _norm_mod_rows(x, ng_ref[...], nmod_ref[rs + 1:rs + 2, :], nmod_ref[rs:rs + 1, :]).astype(h_ref.dtype)

    tok = pl.BlockSpec((None, ts, D), lambda b, s: (b, s, 0))
    vec = pl.BlockSpec((1, D), lambda b, s: (0, 0))
    modspec = pl.BlockSpec((None, 8, D), lambda b, s: (b, 0, 0))
    ins = [xp, f, post_g.reshape(1, D), mod]
    in_specs = [tok, tok, vec, modspec]
    out_specs = [tok]
    out_shape = [jax.ShapeDtypeStruct((Bl, S, D), F32)]
    if has_next:
        ins += [nxt[0].reshape(1, D), nxt[1]]
        in_specs += [vec, modspec]
        out_specs += [tok]
        out_shape += [jax.ShapeDtypeStruct((Bl, S, D), MXU_DTYPE)]
    out = pl.pallas_call(
        body, name=name, grid=(Bl, S // ts), in_specs=in_specs, out_specs=out_specs, out_shape=out_shape,
        compiler_params=_cp(("parallel", "parallel")),
    )(*ins)
    return (out[0], out[1]) if has_next else (out[0], None)


def _loss_fwd_bwd(y, target, name, ts=512):
    Bl, S, D = y.shape
    ts = _tile(S, ts, 8)

    def body(y_ref, t_ref, dy_ref, l_ref):
        @pl.when((pl.program_id(0) == 0) & (pl.program_id(1) == 0))
        def _():
            l_ref[...] = jnp.zeros_like(l_ref)

        e = y_ref[...] - t_ref[...]
        dy_ref[...] = e * (1.0 / D)
        l_ref[...] += 0.5 * jnp.sum(jnp.mean(e * e, axis=-1, keepdims=True), axis=0, keepdims=True)

    tok = pl.BlockSpec((None, ts, D), lambda b, s: (b, s, 0))
    dy, l = pl.pallas_call(
        body, name=name, grid=(Bl, S // ts), in_specs=[tok, tok],
        out_specs=[tok, pl.BlockSpec((8, LANES), lambda b, s: (0, 0))],
        out_shape=[jax.ShapeDtypeStruct((Bl, S, D), F32), jax.ShapeDtypeStruct((8, LANES), F32)],
        compiler_params=_cp(("arbitrary", "arbitrary")),
    )(y, target)
    return dy, l[0, 0]


def _norm_bwd(dx_res, nxt, prv, name, ts=256):
    Bl, S, D = dx_res.shape
    ts = _tile(S, ts, 8)
    has_next, has_prev = nxt is not None, prv is not None

    def body(*refs):
        refs = list(refs)
        dxr_ref = refs.pop(0)
        if has_next:
            dh_ref, x_ref, g_ref, nmod_ref = refs[:4]
            refs = refs[4:]
        if has_prev:
            f_ref, pg_ref, pmod_ref = refs[:3]
            refs = refs[3:]
        dx_ref = refs.pop(0)
        if has_prev:
            df_ref = refs.pop(0)
        pb_ref, sh_ref = refs
        b, s = pl.program_id(0), pl.program_id(1)

        @pl.when(s == 0)
        def _():
            pb_ref[...] = jnp.zeros_like(pb_ref)

        @pl.when((b == 0) & (s == 0))
        def _():
            sh_ref[...] = jnp.zeros_like(sh_ref)

        dx = dxr_ref[...]
        if has_next:
            rs = nxt[4]
            x, dh, g = x_ref[...], dh_ref[...], g_ref[...]
            sc1 = 1.0 + nmod_ref[rs + 1:rs + 2, :]
            r = lax.rsqrt(jnp.mean(x * x, axis=-1, keepdims=True) + EPS)
            xn = x * r
            pb_ref[0:1, :] += jnp.sum(dh, axis=0, keepdims=True)
            pb_ref[1:2, :] += jnp.sum(dh * (xn * g), axis=0, keepdims=True)
            sh_ref[0:1, :] += jnp.sum(dh * sc1 * xn, axis=0, keepdims=True)
            dxn = dh * sc1 * g
            dx = dx + r * (dxn - xn * jnp.mean(dxn * xn, axis=-1, keepdims=True))
        dx_ref[...] = dx
        if has_prev:
            rg = prv[3]
            f, pg = f_ref[...], pg_ref[...]
            gt = pmod_ref[rg:rg + 1, :]
            r = lax.rsqrt(jnp.mean(f * f, axis=-1, keepdims=True) + EPS)
            fn = f * r
            pb_ref[2:3, :] += jnp.sum(dx * (fn * pg), axis=0, keepdims=True)
            drn = dx * gt
            sh_ref[1:2, :] += jnp.sum(drn * fn, axis=0, keepdims=True)
            dfn = drn * pg
            df_ref[...] = (r * (dfn - fn * jnp.mean(dfn * fn, axis=-1, keepdims=True))).astype(df_ref.dtype)

    tok = pl.BlockSpec((None, ts, D), lambda b, s: (b, s, 0))
    vec = pl.BlockSpec((1, D), lambda b, s: (0, 0))
    modspec = pl.BlockSpec((None, 8, D), lambda b, s: (b, 0, 0))
    ins, in_specs = [dx_res], [tok]
    if has_next:
        ins += [nxt[0], nxt[1], nxt[2].reshape(1, D), nxt[3]]
        in_specs += [tok, tok, vec, modspec]
    if has_prev:
        ins += [prv[0], prv[1].reshape(1, D), prv[2]]
        in_specs += [tok, vec, modspec]
    out_specs, out_shape = [tok], [jax.ShapeDtypeStruct((Bl, S, D), F32)]
    if has_prev:
        out_specs += [tok]
        out_shape += [jax.ShapeDtypeStruct((Bl, S, D), MXU_DTYPE)]
    out_specs += [modspec, pl.BlockSpec((8, D), lambda b, s: (0, 0))]
    out_shape += [jax.ShapeDtypeStruct((Bl, 8, D), F32), jax.ShapeDtypeStruct((8, D), F32)]
    out = pl.pallas_call(
        body, name=name, grid=(Bl, S // ts), in_specs=in_specs, out_specs=out_specs, out_shape=out_shape,
        compiler_params=_cp(("arbitrary", "arbitrary")),
    )(*ins)
    if has_prev:
        return out[0], out[1], out[2], out[3]
    return out[0], None, out[1], out[2]


def _shift_down(x, j):
    return x if j == 0 else pltpu.roll(x, j, axis=0)


def _shift_up(x, j):
    return x if j == 0 else pltpu.roll(x, x.shape[0] - j, axis=0)


def _conv(xall, w_ref, K):
    y = w_ref[K - 1:K, :] * xall
    for k in range(K - 1):
        y = y + w_ref[k:k + 1, :] * _shift_down(xall, K - 1 - k)
    return y


def _conv_t(dall, xall, w, K, rows):
    xt = xall[HALO:HALO + rows]
    y = w[K - 1:K, :] * dall[HALO:HALO + rows]
    gw = [None] * K
    gw[K - 1] = jnp.sum(dall[HALO:HALO + rows] * xt, axis=0, keepdims=True)
    for k in range(K - 1):
        sh = _shift_up(dall, K - 1 - k)[HALO:HALO + rows]
        y = y + w[k:k + 1, :] * sh
        gw[k] = jnp.sum(sh * xt, axis=0, keepdims=True)
    return y, gw


def _conv_t_rows(dall, w_ref, K):
    y = w_ref[K - 1:K, :] * dall
    for k in range(K - 1):
        y = y + w_ref[k:k + 1, :] * _shift_up(dall, K - 1 - k)
    return y


def _conv_wgrad(acc_ref, dtile, xall, K, ts):
    for k in range(K):
        xs = _shift_down(xall, K - 1 - k)[HALO:HALO + ts]
        acc_ref[k:k + 1, :] += jnp.sum(dtile * xs, axis=0, keepdims=True)


ROW_CHUNK = 64


def _lane_chunks(W):
    lc = 2 * LANES if W % (2 * LANES) == 0 else LANES
    return [(j * lc, lc) for j in range(W // lc)]


def _fill_ext(ext_ref, prev, tile, nxt, ts):
    ext_ref[0:HALO, :] = prev
    ext_ref[HALO:HALO + ts, :] = tile
    ext_ref[HALO + ts:2 * HALO + ts, :] = nxt


def _halo_specs(ts, W, nS, colblk):
    per = ts // HALO
    tile = pl.BlockSpec((None, ts, W), lambda b, c, s: (b, s, colblk(c)))
    prev = pl.BlockSpec((None, HALO, W), lambda b, c, s: (b, jnp.maximum(s * per - 1, 0), colblk(c)))
    nxt = pl.BlockSpec((None, HALO, W), lambda b, c, s: (b, jnp.minimum((s + 1) * per, nS * per - 1), colblk(c)))
    return tile, prev, nxt


def _masked(ref, keep):
    v = ref[...]
    return jnp.where(keep, v, jnp.zeros_like(v))


def _ssd_conv_fwd(proj, w8, b, off, CD, name, ts=512, W=1024):
    Bl, S, _ = proj.shape
    K = 4
    ts, W = _tile(S, ts, 8), _tile(math.gcd(CD, off), W)
    assert off % W == 0
    nS, nW, ob = S // ts, CD // W, off // W

    def body(x_ref, xp_ref, w_ref, b_ref, o_ref):
        xall = jnp.concatenate([_masked(xp_ref, pl.program_id(2) > 0), x_ref[...]], axis=0)
        xc = _conv(xall, w_ref, K)[HALO:] + b_ref[...]
        o_ref[...] = xc * _sigmoid(xc)

    tile, prev, _ = _halo_specs(ts, W, nS, lambda c: ob + c)
    return pl.pallas_call(
        body, name=name, grid=(Bl, nW, nS),
        in_specs=[tile, prev, pl.BlockSpec((8, W), lambda b_, c, s: (0, c)), pl.BlockSpec((1, W), lambda b_, c, s: (0, c))],
        out_specs=pl.BlockSpec((None, ts, W), lambda b_, c, s: (b_, s, c)),
        out_shape=jax.ShapeDtypeStruct((Bl, S, CD), F32),
        compiler_params=_cp(("parallel", "parallel", "parallel")),
    )(proj, proj, w8, b.reshape(1, CD))


def _ssd_conv_bwd(dxa, proj, w8, b, off, CD, dproj, name, ts=256, W=1024):
    Bl, S, _ = proj.shape
    K = 4
    ts, W = _tile(S, ts, 8), _tile(math.gcd(CD, off), W)
    nS, nW, ob = S // ts, CD // W, off // W

    rc = _tile(ts, ROW_CHUNK, 8)

    def body(d_ref, dn_ref, x_ref, xp_ref, xn_ref, w_ref, b_ref, _, dx_ref, dw_ref, db_ref, xext_ref, dext_ref):
        bb, s = pl.program_id(1), pl.program_id(2)

        @pl.when((bb == 0) & (s == 0))
        def _():
            dw_ref[...] = jnp.zeros_like(dw_ref)
            db_ref[...] = jnp.zeros_like(db_ref)

        last = s == nS - 1
        _fill_ext(xext_ref, _masked(xp_ref, s > 0), x_ref[...], _masked(xn_ref, ~last), ts)
        _fill_ext(dext_ref, jnp.zeros((HALO, W), F32), d_ref[...], _masked(dn_ref, ~last), ts)
        for l0, lc in _lane_chunks(W):
            w, bias = w_ref[:, l0:l0 + lc], b_ref[:, l0:l0 + lc]

            def chunk(i, acc):
                r0 = pl.multiple_of(i * rc, 8)
                xw = xext_ref[pl.ds(r0, rc + 2 * HALO), l0:l0 + lc]
                xc = _conv(xw, w, K) + bias
                sg = _sigmoid(xc)
                dxc = dext_ref[pl.ds(r0, rc + 2 * HALO), l0:l0 + lc] * (sg * (1.0 + xc * (1.0 - sg)))
                dx, gw = _conv_t(dxc, xw, w, K, rc)
                dx_ref[pl.ds(r0, rc), l0:l0 + lc] = dx.astype(dx_ref.dtype)
                gb = jnp.sum(dxc[HALO:HALO + rc], axis=0, keepdims=True)
                return tuple(a + g for a, g in zip(acc, gw + [gb]))

            acc = lax.fori_loop(0, ts // rc, chunk, tuple(jnp.zeros((1, lc), F32) for _ in range(K + 1)))
            for k in range(K):
                dw_ref[k:k + 1, l0:l0 + lc] += acc[k]
            db_ref[0:1, l0:l0 + lc] += acc[K]

    per = ts // HALO
    dtile_s = pl.BlockSpec((None, ts, W), lambda c, b_, s: (b_, s, c))
    dnext_s = pl.BlockSpec((None, HALO, W), lambda c, b_, s: (b_, jnp.minimum((s + 1) * per, nS * per - 1), c))
    xtile_s = pl.BlockSpec((None, ts, W), lambda c, b_, s: (b_, s, ob + c))
    xprev_s = pl.BlockSpec((None, HALO, W), lambda c, b_, s: (b_, jnp.maximum(s * per - 1, 0), ob + c))
    xnext_s = pl.BlockSpec((None, HALO, W), lambda c, b_, s: (b_, jnp.minimum((s + 1) * per, nS * per - 1), ob + c))
    wspec = pl.BlockSpec((8, W), lambda c, b_, s: (0, c))
    return pl.pallas_call(
        body, name=name, grid=(nW, Bl, nS),
        in_specs=[dtile_s, dnext_s, xtile_s, xprev_s, xnext_s, wspec, pl.BlockSpec((1, W), lambda c, b_, s: (0, c)),
                  _ANY],
        out_specs=[xtile_s, wspec, wspec],
        out_shape=[jax.ShapeDtypeStruct(dproj.shape, dproj.dtype), jax.ShapeDtypeStruct((8, CD), F32),
                   jax.ShapeDtypeStruct((8, CD), F32)],
        input_output_aliases={7: 0},
        scratch_shapes=[pltpu.VMEM((ts + 2 * HALO, W), F32)] * 2,
        compiler_params=_cp(("arbitrary", "arbitrary", "arbitrary")),
    )(dxa, dxa, proj, proj, proj, w8, b.reshape(1, CD), dproj)


def _sc_conv_fwd(proj, w8, offs, D, name, ts=512, W=1024):
    Bl, S, _ = proj.shape
    K = 3
    ts, W = _tile(S, ts, 8), _tile(D, W)
    nS, nW = S // ts, D // W
    ob, oc, oh = [o // W for o in offs]

    def body(b_ref, c_ref, cp_ref, h_ref, hp_ref, w_ref, o_ref):
        s = pl.program_id(2)
        keep = s > 0
        vall = jnp.concatenate([_masked(cp_ref, keep) * _masked(hp_ref, keep), c_ref[...] * h_ref[...]], axis=0)
        o_ref[...] = (b_ref[...] * _conv(vall, w_ref, K)[HALO:]).astype(o_ref.dtype)

    tb, _, _ = _halo_specs(ts, W, nS, lambda c: ob + c)
    tc, pc, _ = _halo_specs(ts, W, nS, lambda c: oc + c)
    th, ph, _ = _halo_specs(ts, W, nS, lambda c: oh + c)
    return pl.pallas_call(
        body, name=name, grid=(Bl, nW, nS),
        in_specs=[tb, tc, pc, th, ph, pl.BlockSpec((8, W), lambda b_, c, s: (0, c))],
        out_specs=pl.BlockSpec((None, ts, W), lambda b_, c, s: (b_, s, c)),
        out_shape=jax.ShapeDtypeStruct((Bl, S, D), MXU_DTYPE),
        compiler_params=_cp(("parallel", "parallel", "parallel")),
    )(proj, proj, proj, proj, proj, w8)


def _sc_conv_bwd(ds, proj, w8, offs, D, dproj, name, ts=256):
    Bl, S, _ = proj.shape
    K = 3
    ts, W = _tile(S, ts, 8), D
    nS, nW = S // ts, 1
    ob, oc, oh = [o // W for o in offs]
    assert offs[1] == offs[0] + D and offs[2] == offs[0] + 2 * D and offs[0] % (3 * D) == 0

    def body(d_ref, dn_ref, b_ref, bn_ref, c_ref, cp_ref, cn_ref, h_ref, hp_ref, hn_ref, w_ref, _,
             o_ref, dw_ref):
        db_ref, dc_ref, dh_ref = o_ref.at[:, 0:D], o_ref.at[:, D:2 * D], o_ref.at[:, 2 * D:3 * D]
        bb, s = pl.program_id(1), pl.program_id(2)

        @pl.when((bb == 0) & (s == 0))
        def _():
            dw_ref[...] = jnp.zeros_like(dw_ref)

        first, last = s > 0, s < nS - 1
        zeros = jnp.zeros((HALO, W), F32)
        c_t, h_t = c_ref[...], h_ref[...]
        vall = jnp.concatenate([_masked(cp_ref, first) * _masked(hp_ref, first), c_t * h_t,
                                _masked(cn_ref, last) * _masked(hn_ref, last)], axis=0)
        dcv = jnp.concatenate([zeros, d_ref[...] * b_ref[...], _masked(dn_ref, last) * _masked(bn_ref, last)], axis=0)
        cv = _conv(vall, w_ref, K)[HALO:HALO + ts]
        db_ref[...] = (d_ref[...] * cv).astype(db_ref.dtype)
        dv, gw = _conv_t(dcv, vall, w_ref, K, ts)
        dc_ref[...] = (dv * h_t).astype(dc_ref.dtype)
        dh_ref[...] = (dv * c_t).astype(dh_ref.dtype)
        for k in range(K):
            dw_ref[k:k + 1, :] += gw[k]

    per = ts // HALO

    def specs(o):
        t = pl.BlockSpec((None, ts, W), lambda c, b_, s: (b_, s, o + c))
        p = pl.BlockSpec((None, HALO, W), lambda c, b_, s: (b_, jnp.maximum(s * per - 1, 0), o + c))
        n = pl.BlockSpec((None, HALO, W), lambda c, b_, s: (b_, jnp.minimum((s + 1) * per, nS * per - 1), o + c))
        return t, p, n

    dt_, _, dn_ = specs(0)
    bt, _, bn = specs(ob)
    ct, cp, cn = specs(oc)
    ht, hp, hn = specs(oh)
    wspec = pl.BlockSpec((8, W), lambda c, b_, s: (0, c))
    o3 = offs[0] // (3 * D)
    return pl.pallas_call(
        body, name=name, grid=(nW, Bl, nS),
        in_specs=[dt_, dn_, bt, bn, ct, cp, cn, ht, hp, hn, wspec, _ANY],
        out_specs=[pl.BlockSpec((None, ts, 3 * D), lambda c, b_, s: (b_, s, o3)), wspec],
        out_shape=[jax.ShapeDtypeStruct(dproj.shape, dproj.dtype), jax.ShapeDtypeStruct((8, D), F32)],
        input_output_aliases={11: 0},
        compiler_params=_cp(("arbitrary", "arbitrary", "arbitrary")),
    )(ds, ds, proj, proj, proj, proj, proj, proj, proj, proj, w8, dproj)


FFN_BLOCK = 1408


def _ffn_block(DFF):
    return _tile(DFF, FFN_BLOCK)


def _interleave_up(w, DFF):
    W = _ffn_block(DFF)
    lead = w.shape[:-1]
    return jnp.swapaxes(w.reshape(lead + (2, DFF // W, W)), -3, -2).reshape(lead + (2 * DFF,))


def _deinterleave_up(w, DFF):
    W = _ffn_block(DFF)
    lead = w.shape[:-1]
    return jnp.swapaxes(w.reshape(lead + (DFF // W, 2, W)), -3, -2).reshape(lead + (2 * DFF,))


def _ffn_conv_fwd(up, w8, b, DFF, name, ts=512):
    Bl, S, _ = up.shape
    K = 3
    ts, W = _tile(S, ts, 8), _ffn_block(DFF)
    nS, nW = S // ts, DFF // W

    def body(g_ref, gp_ref, v_ref, vp_ref, wg_ref, wv_ref, bg_ref, bv_ref, o_ref):
        keep = pl.program_id(2) > 0
        ug = _conv(jnp.concatenate([_masked(gp_ref, keep), g_ref[...]], axis=0), wg_ref, K)[HALO:] + bg_ref[...]
        uv = _conv(jnp.concatenate([_masked(vp_ref, keep), v_ref[...]], axis=0), wv_ref, K)[HALO:] + bv_ref[...]
        o_ref[...] = (ug * _sigmoid(ug) * uv).astype(o_ref.dtype)

    tg, pg, _ = _halo_specs(ts, W, nS, lambda c: 2 * c)
    tv, pv, _ = _halo_specs(ts, W, nS, lambda c: 2 * c + 1)
    wg = pl.BlockSpec((8, W), lambda b_, c, s: (0, c))
    wv = pl.BlockSpec((8, W), lambda b_, c, s: (0, nW + c))
    bg = pl.BlockSpec((1, W), lambda b_, c, s: (0, c))
    bv = pl.BlockSpec((1, W), lambda b_, c, s: (0, nW + c))
    b2 = b.reshape(1, 2 * DFF)
    return pl.pallas_call(
        body, name=name, grid=(Bl, nW, nS),
        in_specs=[tg, pg, tv, pv, wg, wv, bg, bv],
        out_specs=pl.BlockSpec((None, ts, W), lambda b_, c, s: (b_, s, c)),
        out_shape=jax.ShapeDtypeStruct((Bl, S, DFF), MXU_DTYPE),
        compiler_params=_cp(("parallel", "parallel", "parallel")),
    )(up, up, up, up, w8, w8, b2, b2)


def _ffn_conv_bwd(da, up, w8, b, DFF, name, ts=256):
    Bl, S, _ = up.shape
    K = 3
    ts, W = _tile(S, ts, 8), _ffn_block(DFF)
    nS, nW = S // ts, DFF // W

    def body(d_ref, dn_ref, g_ref, gp_ref, gn_ref, v_ref, vp_ref, vn_ref, wg_ref, wv_ref, bg_ref, bv_ref,
             dup_ref, dwg_ref, dwv_ref, dbg_ref, dbv_ref):
        dg_ref, dv_ref = dup_ref.at[:, 0:W], dup_ref.at[:, W:2 * W]
        bb, s = pl.program_id(1), pl.program_id(2)

        @pl.when((bb == 0) & (s == 0))
        def _():
            for r in (dwg_ref, dwv_ref, dbg_ref, dbv_ref):
                r[...] = jnp.zeros_like(r)

        first, last = s > 0, s < nS - 1
        gall = jnp.concatenate([_masked(gp_ref, first), g_ref[...], _masked(gn_ref, last)], axis=0)
        vall = jnp.concatenate([_masked(vp_ref, first), v_ref[...], _masked(vn_ref, last)], axis=0)
        dall = jnp.concatenate([jnp.zeros((HALO, W), F32), d_ref[...], _masked(dn_ref, last)], axis=0)
        ug = _conv(gall, wg_ref, K) + bg_ref[...]
        uv = _conv(vall, wv_ref, K) + bv_ref[...]
        sg = _sigmoid(ug)
        dug = dall * uv * (sg * (1.0 + ug * (1.0 - sg)))
        duv = dall * (ug * sg)
        dg_ref[...] = _conv_t_rows(dug, wg_ref, K)[HALO:HALO + ts].astype(dg_ref.dtype)
        dv_ref[...] = _conv_t_rows(duv, wv_ref, K)[HALO:HALO + ts].astype(dv_ref.dtype)
        dug_t, duv_t = dug[HALO:HALO + ts], duv[HALO:HALO + ts]
        dbg_ref[0:1, :] += jnp.sum(dug_t, axis=0, keepdims=True)
        dbv_ref[0:1, :] += jnp.sum(duv_t, axis=0, keepdims=True)
        _conv_wgrad(dwg_ref, dug_t, gall, K, ts)
        _conv_wgrad(dwv_ref, duv_t, vall, K, ts)

    per = ts // HALO

    def specs(blk):
        t = pl.BlockSpec((None, ts, W), lambda c, b_, s: (b_, s, blk(c)))
        p = pl.BlockSpec((None, HALO, W), lambda c, b_, s: (b_, jnp.maximum(s * per - 1, 0), blk(c)))
        n = pl.BlockSpec((None, HALO, W), lambda c, b_, s: (b_, jnp.minimum((s + 1) * per, nS * per - 1), blk(c)))
        return t, p, n

    dt_, _, dn_ = specs(lambda c: c)
    gt, gp, gn = specs(lambda c: 2 * c)
    vt, vp, vn = specs(lambda c: 2 * c + 1)
    wg = pl.BlockSpec((8, W), lambda c, b_, s: (0, c))
    wv = pl.BlockSpec((8, W), lambda c, b_, s: (0, nW + c))
    bg = pl.BlockSpec((1, W), lambda c, b_, s: (0, c))
    bv = pl.BlockSpec((1, W), lambda c, b_, s: (0, nW + c))
    b2 = b.reshape(1, 2 * DFF)
    small = jax.ShapeDtypeStruct((8, DFF), F32)
    dup, dwg, dwv, dbg, dbv = pl.pallas_call(
        body, name=name, grid=(nW, Bl, nS),
        in_specs=[dt_, dn_, gt, gp, gn, vt, vp, vn, wg, wv, bg, bv],
        out_specs=[pl.BlockSpec((None, ts, 2 * W), lambda c, b_, s: (b_, s, c)), wg, wg, wg, wg],
        out_shape=[jax.ShapeDtypeStruct((Bl, S, 2 * DFF), MXU_DTYPE), small, small, small, small],
        compiler_params=_cp(("arbitrary", "arbitrary", "arbitrary")),
    )(da, da, up, up, up, up, up, up, w8, w8, b2, b2)
    return dup, jnp.concatenate([dwg, dwv], axis=1), jnp.concatenate([dbg, dbv], axis=1)


def _merge_fwd(proj, y_ssd, y_sc, off, D, name, ts=512):
    Bl, S, _ = proj.shape
    ts = _tile(S, ts, 8)
    og = off // D

    def body(g1_ref, g2_ref, a_ref, b_ref, o_ref):
        o_ref[...] = (_sigmoid(g1_ref[...]) * a_ref[...] + _sigmoid(g2_ref[...]) * b_ref[...]).astype(o_ref.dtype)

    tok = pl.BlockSpec((None, ts, D), lambda b, s: (b, s, 0))
    return pl.pallas_call(
        body, name=name, grid=(Bl, S // ts),
        in_specs=[pl.BlockSpec((None, ts, D), lambda b, s: (b, s, og)),
                  pl.BlockSpec((None, ts, D), lambda b, s: (b, s, og + 1)), tok, tok],
        out_specs=tok, out_shape=jax.ShapeDtypeStruct((Bl, S, D), MXU_DTYPE),
        compiler_params=_cp(("parallel", "parallel")),
    )(proj, proj, y_ssd, y_sc)


def _merge_bwd(dmixin, proj, y_ssd, y_sc, off, D, dproj0, name, ts=512):
    Bl, S, NP = proj.shape
    ts = _tile(S, ts, 8)
    og = off // D
    assert off % (2 * D) == 0

    def body(d_ref, g1_ref, g2_ref, a_ref, b_ref, *rest):
        da_ref, db_ref, dg_ref = rest[-3:]
        dg1_ref, dg2_ref = dg_ref.at[:, 0:D], dg_ref.at[:, D:2 * D]
        d = d_ref[...]
        s1, s2 = _sigmoid(g1_ref[...]), _sigmoid(g2_ref[...])
        da_ref[...] = (d * s1).astype(da_ref.dtype)
        db_ref[...] = (d * s2).astype(db_ref.dtype)
        dg1_ref[...] = (d * a_ref[...] * (s1 * (1.0 - s1))).astype(dg1_ref.dtype)
        dg2_ref[...] = (d * b_ref[...] * (s2 * (1.0 - s2))).astype(dg2_ref.dtype)

    tok = pl.BlockSpec((None, ts, D), lambda b, s: (b, s, 0))
    act = jax.ShapeDtypeStruct((Bl, S, D), MXU_DTYPE)
    carry = dproj0 is not None
    return pl.pallas_call(
        body, name=name, grid=(Bl, S // ts),
        in_specs=[tok, pl.BlockSpec((None, ts, D), lambda b, s: (b, s, og)),
                  pl.BlockSpec((None, ts, D), lambda b, s: (b, s, og + 1)), tok, tok] + ([_ANY] if carry else []),
        out_specs=[tok, tok, pl.BlockSpec((None, ts, 2 * D), lambda b, s: (b, s, og // 2))],
        out_shape=[act, act, jax.ShapeDtypeStruct((Bl, S, NP), MXU_DTYPE)],
        input_output_aliases={5: 2} if carry else {},
        compiler_params=_cp(("parallel", "parallel")),
    )(*((dmixin, proj, proj, y_ssd, y_sc) + ((dproj0,) if carry else ())))


def _exact_dot(a, b, dn, value):
    v = a if value == 0 else b
    m01 = (b if value == 0 else a).astype(MXU_DTYPE)
    hi = v.astype(MXU_DTYPE)
    r1 = v - hi.astype(F32)
    mid = r1.astype(MXU_DTYPE)
    lo = (r1 - mid.astype(F32)).astype(MXU_DTYPE)
    terms = [(t, m01) if value == 0 else (m01, t) for t in (hi, mid, lo)]
    return _dg(*terms[0], dn) + _dg(*terms[1], dn) + _dg(*terms[2], dn)


def _two_term_dot(v, m01):
    hi = v.astype(MXU_DTYPE)
    lo = (v - hi.astype(F32)).astype(MXU_DTYPE)
    return _dg(hi, m01, NN) + _dg(lo, m01, NN)


def _head_to_channels(R):
    rp = R * HEAD_DIM
    return (lax.shift_right_logical(lax.broadcasted_iota(jnp.int32, (LANES, rp), 1), 6)
            == lax.broadcasted_iota(jnp.int32, (LANES, rp), 0)).astype(MXU_DTYPE)


def _expand_heads(mats, R):
    L = mats[0].shape[0]
    out = _two_term_dot(jnp.concatenate(mats, axis=0), _head_to_channels(R))
    return [out[i * L:(i + 1) * L, :] for i in range(len(mats))]


def _expand_cols(mat, R, lane):
    half = lane < HEAD_DIM
    return jnp.concatenate(
        [jnp.where(half, mat[:, 2 * q:2 * q + 1], mat[:, 2 * q + 1:2 * q + 2]) for q in range(R // 2)], axis=1)


def _row_sums(v):
    return _two_term_dot(v, jnp.ones((v.shape[1], LANES), MXU_DTYPE))


def _lanes_to(v, width):
    return jnp.concatenate([v] * (width // LANES), axis=1)


def _head_rows(colvec, R, N):
    return jnp.concatenate([jnp.broadcast_to(colvec[r:r + 1, :], (HEAD_DIM, N)) for r in range(R)], axis=0)


def _ssd_prep(proj, dt_bias, a_log, selcat, selbig, dm, name):
    Bl, S, L, G = dm.Bl, dm.S, CHUNK, N_GROUPS
    nc = S // L
    odt = dm.ODT // LANES

    def body(dtr_ref, dtb_ref, alog_ref, selcat_ref, selbig_ref, dtg_ref, acs_ref, acst_ref):
        row = lax.broadcasted_iota(jnp.int32, (L, L), 0)
        col = lax.broadcasted_iota(jnp.int32, (L, L), 1)
        dt_all = _softplus(dtr_ref[...] + dtb_ref[...])
        acs_all = _exact_dot((row >= col).astype(F32), dt_all * (-jnp.exp(alog_ref[...])), NN, 1)
        dtg_ref[...] = _exact_dot(dt_all, selcat_ref[...], NN, 0)
        acs_ref[...] = _exact_dot(acs_all, selcat_ref[...], NN, 0)
        acst_ref[...] = _exact_dot(selbig_ref[...], acs_all, NT, 1)

    vec = pl.BlockSpec((1, LANES), lambda b, c: (0, 0))
    wide = pl.BlockSpec((None, L, G * LANES), lambda b, c: (b, c, 0))
    return pl.pallas_call(
        body, name=name, grid=(Bl, nc),
        in_specs=[pl.BlockSpec((None, L, LANES), lambda b, c: (b, c, odt)), vec, vec,
                  pl.BlockSpec((LANES, G * LANES), lambda b, c: (0, 0)),
                  pl.BlockSpec((G * 8, LANES), lambda b, c: (0, 0))],
        out_specs=[wide, wide, pl.BlockSpec((None, None, G * 8, L), lambda b, c: (b, c, 0, 0))],
        out_shape=[jax.ShapeDtypeStruct((Bl, S, G * LANES), F32), jax.ShapeDtypeStruct((Bl, S, G * LANES), F32),
                   jax.ShapeDtypeStruct((Bl, nc, G * 8, L), F32)],
        compiler_params=_cp(("parallel", "parallel")),
    )(proj, dt_bias, a_log, selcat, selbig)


def _ssd_post(d_a, ddt, dtg, proj, dt_bias, a_log, selcat, dproj, dm, name):
    Bl, S, L, G = dm.Bl, dm.S, CHUNK, N_GROUPS
    nc = S // L
    odt = dm.ODT // LANES

    def body(da_ref, ddt_ref, dtg_ref, dtr_ref, dtb_ref, alog_ref, selcat_ref, _, ddtr_ref, dpar_ref):
        @pl.when((pl.program_id(0) == 0) & (pl.program_id(1) == 0))
        def _():
            dpar_ref[...] = jnp.zeros_like(dpar_ref)

        row = lax.broadcasted_iota(jnp.int32, (L, L), 0)
        col = lax.broadcasted_iota(jnp.int32, (L, L), 1)
        selcat = selcat_ref[...]
        a_all = -jnp.exp(alog_ref[...])
        a4 = _exact_dot(jnp.broadcast_to(a_all, (8, LANES)), selcat, NN, 0)[0:1, :]
        dadt = _exact_dot((col >= row).astype(F32), da_ref[...], NN, 1)
        ddt4 = ddt_ref[...] + dadt * a4
        da4 = jnp.sum(dadt * dtg_ref[...], axis=0, keepdims=True)
        ddt_all = _exact_dot(ddt4, selcat, NT, 0)
        da_all = _exact_dot(jnp.broadcast_to(da4, (8, G * LANES)), selcat, NT, 0)[0:1, :]
        ddtr = ddt_all * _sigmoid(dtr_ref[...] + dtb_ref[...])
        ddtr_ref[...] = ddtr.astype(ddtr_ref.dtype)
        dpar_ref[0:1, :] += jnp.sum(ddtr, axis=0, keepdims=True)
        dpar_ref[1:2, :] += da_all * a_all

    vec = pl.BlockSpec((1, LANES), lambda b, c: (0, 0))
    wide = pl.BlockSpec((None, L, G * LANES), lambda b, c: (b, c, 0))
    return pl.pallas_call(
        body, name=name, grid=(Bl, nc),
        in_specs=[wide, wide, wide, pl.BlockSpec((None, L, LANES), lambda b, c: (b, c, odt)), vec, vec,
                  pl.BlockSpec((LANES, G * LANES), lambda b, c: (0, 0)), _ANY],
        out_specs=[pl.BlockSpec((None, L, LANES), lambda b, c: (b, c, odt)), pl.BlockSpec((8, LANES), lambda b, c: (0, 0))],
        out_shape=[jax.ShapeDtypeStruct(dproj.shape, dproj.dtype), jax.ShapeDtypeStruct((8, LANES), F32)],
        input_output_aliases={7: 0},
        compiler_params=_cp(("arbitrary", "arbitrary")),
    )(d_a, ddt, dtg, proj, dt_bias, a_log, selcat, dproj)


def _scan_fwd(xbc_a, proj, dtg4, acs4, acs_t4, dsk_exp, norm_g, dm, name):
    Bl, S, DI, N, R, L, G = dm.Bl, dm.S, dm.DI, dm.N, dm.R, CHUNK, N_GROUPS
    RP = R * HEAD_DIM
    nc = S // L

    def body(x_ref, z_ref, dtg_ref, acs_ref, acst_ref, dsk_ref, ng_ref, y_ref, yn_ref, hp_ref, h_ref):
        xs_ref, bm_ref, cm_ref = x_ref.at[:, 0:RP], x_ref.at[:, RP:RP + N], x_ref.at[:, RP + N:RP + 2 * N]
        @pl.when(pl.program_id(2) == 0)
        def _():
            h_ref[...] = jnp.zeros_like(h_ref)

        causal = lax.broadcasted_iota(jnp.int32, (L, L), 0) >= lax.broadcasted_iota(jnp.int32, (L, L), 1)
        lane = lax.broadcasted_iota(jnp.int32, (L, LANES), 1)
        dtg, acs, acs_t = dtg_ref[...], acs_ref[...], acst_ref[...]
        xs = xs_ref[...]
        bmb, cmb = bm_ref[...].astype(MXU_DTYPE), cm_ref[...].astype(MXU_DTYPE)
        sg = _dg(cmb, bmb, NT)
        acs_last = acs[L - 1:L, :]
        dt_exp, ea_exp, ds_exp = _expand_heads([dtg, jnp.exp(acs), jnp.exp(acs_last - acs)], R)
        xdt = xs * dt_exp
        xb = xdt.astype(MXU_DTYPE)
        parts = []
        for q in range(R // 2):
            x2 = xb[:, LANES * q:LANES * (q + 1)]
            ys = []
            for r in (2 * q, 2 * q + 1):
                dec = jnp.exp(jnp.where(causal, acs[:, r:r + 1] - acs_t[r:r + 1, :], -1e30))
                ys.append(_dg((sg * dec).astype(MXU_DTYPE), x2, NN))
            parts.append(jnp.where(lane < HEAD_DIM, ys[0], ys[1]))
        ydiag = jnp.concatenate(parts, axis=1)
        h_cur = h_ref[...]
        hb = h_cur.astype(MXU_DTYPE)
        yoff = _dg(cmb, hb, NT) * ea_exp
        st = _dg((xdt * ds_exp).astype(MXU_DTYPE), bmb, TN)
        hp_ref[...] = hb
        h_ref[...] = h_cur * _head_rows(jnp.exp(acs_t[:, L - 1:L]), R, N) + st
        y = ydiag + yoff + dsk_ref[...] * xs
        y_ref[...] = y
        z = z_ref[...]
        yg = y * (z * _sigmoid(z))
        rr = lax.rsqrt(_row_sums(yg * yg) * (1.0 / RP) + EPS)
        yn_ref[...] = (yg * _lanes_to(rr, RP) * ng_ref[...]).astype(yn_ref.dtype)

    oz = dm.OZ // RP
    grp = pl.BlockSpec((None, L, RP), lambda b, g, c: (b, c, g))
    lanes = pl.BlockSpec((None, L, LANES), lambda b, g, c: (b, c, g))
    chan = pl.BlockSpec((1, RP), lambda b, g, c: (0, g))
    return pl.pallas_call(
        body, name=name, grid=(Bl, G, nc),
        in_specs=[pl.BlockSpec((None, L, RP + 2 * N), lambda b, g, c: (b, c, g)),
                  pl.BlockSpec((None, L, RP), lambda b, g, c: (b, c, oz + g)), lanes, lanes,
                  pl.BlockSpec((None, None, 8, L), lambda b, g, c: (b, c, g, 0)),
                  chan, chan],
        out_specs=[grp, grp, pl.BlockSpec((None, None, None, RP, N), lambda b, g, c: (b, g, c, 0, 0))],
        out_shape=[jax.ShapeDtypeStruct((Bl, S, DI), F32), jax.ShapeDtypeStruct((Bl, S, DI), MXU_DTYPE),
                   jax.ShapeDtypeStruct((Bl, G, nc, RP, N), MXU_DTYPE)],
        scratch_shapes=[pltpu.VMEM((RP, N), F32)],
        compiler_params=_cp(("parallel", "parallel", "arbitrary")),
    )(xbc_a, proj, dtg4, acs4, acs_t4, dsk_exp, norm_g.reshape(1, DI))


def _scan_bwd(dyn, y, xbc_a, proj, hprev, dtg4, acs4, acs_t4, dsk_exp, norm_g, dproj, dm, name):
    Bl, S, DI, N, R, L, G = dm.Bl, dm.S, dm.DI, dm.N, dm.R, CHUNK, N_GROUPS
    RP = R * HEAD_DIM
    nc = S // L

    def body(dyn_ref, y_ref, z_ref, x_ref, hp_ref, dtg_ref, acs_ref, acst_ref, dsk_ref, ng_ref,
             _, dz_ref, dx_ref, da_ref, ddt_ref, ddsk_ref, dng_ref, dh_ref):
        xs_ref, bm_ref, cm_ref = x_ref.at[:, 0:RP], x_ref.at[:, RP:RP + N], x_ref.at[:, RP + N:RP + 2 * N]
        dxs_ref, dbm_ref, dcm_ref = dx_ref.at[:, 0:RP], dx_ref.at[:, RP:RP + N], dx_ref.at[:, RP + N:RP + 2 * N]
        b, c = pl.program_id(1), pl.program_id(2)

        @pl.when(c == 0)
        def _():
            dh_ref[...] = jnp.zeros_like(dh_ref)

        @pl.when((b == 0) & (c == 0))
        def _():
            ddsk_ref[...] = jnp.zeros_like(ddsk_ref)
            dng_ref[...] = jnp.zeros_like(dng_ref)

        row = lax.broadcasted_iota(jnp.int32, (L, L), 0)
        col = lax.broadcasted_iota(jnp.int32, (L, L), 1)
        causal, anti = row >= col, col >= row
        lane = lax.broadcasted_iota(jnp.int32, (L, LANES), 1)
        etb = (lax.shift_right_logical(lax.broadcasted_iota(jnp.int32, (RP, LANES), 0), 6)
               == lax.broadcasted_iota(jnp.int32, (RP, LANES), 1)).astype(MXU_DTYPE)

        dtg, acs, acs_t = dtg_ref[...], acs_ref[...], acst_ref[...]
        xs, z, y, dyn = xs_ref[...], z_ref[...], y_ref[...], dyn_ref[...]
        bmb, cmb = bm_ref[...].astype(MXU_DTYPE), cm_ref[...].astype(MXU_DTYPE)
        hpb = hp_ref[...]
        ng = ng_ref[...]

        sz = _sigmoid(z)
        siluz = z * sz
        yg = y * siluz
        rr = lax.rsqrt(jnp.mean(yg * yg, axis=-1, keepdims=True) + EPS)
        yhat = yg * rr
        dng_ref[...] += jnp.sum(dyn * yhat, axis=0, keepdims=True)
        dyhat = dyn * ng
        dyg = rr * (dyhat - yhat * jnp.mean(dyhat * yhat, axis=-1, keepdims=True))
        dy = dyg * siluz
        dz_ref[...] = (dyg * y * (sz * (1.0 + z * (1.0 - sz)))).astype(dz_ref.dtype)

        dxs = dy * dsk_ref[...]
        ddsk_ref[...] += jnp.sum(dy * xs, axis=0, keepdims=True)

        acs_last = acs[L - 1:L, :]
        dt_exp = _expand_cols(dtg, R, lane)
        ea_exp = _expand_cols(jnp.exp(acs), R, lane)
        ds_exp = _expand_cols(jnp.exp(acs_last - acs), R, lane)
        xdt = xs * dt_exp
        xb = xdt.astype(MXU_DTYPE)
        dyb = dy.astype(MXU_DTYPE)
        cd = jnp.exp(acs_last)
        cd_rows = _head_rows(jnp.exp(acs_t[:, L - 1:L]), R, N)

        q_ = _dg(cmb, hpb, NT)
        dq = dy * ea_exp
        dqb = dq.astype(MXU_DTYPE)
        dcm = _dg(dqb, hpb, NN)
        dh_yoff = _dg(dqb, cmb, TN)

        dhn = dh_ref[...]
        wprod = dhn * hpb.astype(F32)
        per_head = jnp.concatenate(
            [jnp.sum(wprod[HEAD_DIM * r:HEAD_DIM * (r + 1), :], axis=0, keepdims=True) for r in range(R)]
            + ([jnp.zeros((8 - R, N), F32)] if R < 8 else []), axis=0)
        dcd_col = jnp.sum(per_head, axis=1, keepdims=True)
        diag8 = lax.broadcasted_iota(jnp.int32, (8, LANES), 0) == lax.broadcasted_iota(jnp.int32, (8, LANES), 1)
        dcd_lane = jnp.sum(jnp.where(diag8, dcd_col, 0.0), axis=0, keepdims=True)
        d_a_last = dcd_lane * cd
        dh_ref[...] = dhn * cd_rows + dh_yoff
        dhnb = dhn.astype(MXU_DTYPE)

        e_ = _dg(bmb, dhnb, NT)
        dxdt = ds_exp * e_
        xds = xdt * ds_exp
        dbm = _dg(xds.astype(MXU_DTYPE), dhnb, NN)

        sg = _dg(cmb, bmb, NT)
        sg_t = _dg(bmb, cmb, NT)
        dsg = jnp.zeros((L, L), F32)
        dsg_t = jnp.zeros((L, L), F32)
        d_a = jnp.zeros((L, LANES), F32)
        parts = []
        for q in range(R // 2):
            x2 = xb[:, LANES * q:LANES * (q + 1)]
            dy2 = dyb[:, LANES * q:LANES * (q + 1)]
            dxs2 = []
            for hh, r in enumerate((2 * q, 2 * q + 1)):
                mine = (lane < HEAD_DIM) if hh == 0 else (lane >= HEAD_DIM)
                diff = acs[:, r:r + 1] - acs_t[r:r + 1, :]
                dec = jnp.exp(jnp.where(causal, diff, -1e30))
                dec_t = jnp.exp(jnp.where(anti, -diff, -1e30))
                dy2m = jnp.where(mine, dy2, jnp.zeros_like(dy2))
                dm_ = _dg(dy2m, x2, NT)
                dm_t = _dg(x2, dy2m, NT)
                m_t = sg_t * dec_t
                da_col = jnp.sum(dm_ * (sg * dec) - dm_t * m_t, axis=1, keepdims=True)
                d_a = d_a + jnp.where(lane == r, da_col, 0.0)
                dsg = dsg + dm_ * dec
                dsg_t = dsg_t + dm_t * dec_t
                dxs2.append(_dg(m_t.astype(MXU_DTYPE), dy2, NN))
            parts.append(jnp.where(lane < HEAD_DIM, dxs2[0], dxs2[1]))
        dxdt = dxdt + jnp.concatenate(parts, axis=1)
        dcm_ref[...] = dcm + _dg(dsg.astype(MXU_DTYPE), bmb, NN)
        dbm_ref[...] = dbm + _dg(dsg_t.astype(MXU_DTYPE), cmb, NN)
        dxs_ref[...] = dxs + dxdt * dt_exp

        hs = _two_term_dot(jnp.concatenate([dq * q_ - xds * e_, xds * e_, dxdt * xs], axis=0), etb)
        t2 = hs[L:2 * L, :]
        rowl = lax.broadcasted_iota(jnp.int32, (L, LANES), 0)
        d_a_last = d_a_last + jnp.sum(t2, axis=0, keepdims=True)
        da_ref[...] = d_a + hs[0:L, :] + jnp.where(rowl == L - 1, d_a_last, 0.0)
        ddt_ref[...] = hs[2 * L:3 * L, :]

    oz = dm.OZ // RP
    grp = pl.BlockSpec((None, L, RP), lambda g, b, c: (b, nc - 1 - c, g))
    zspec = pl.BlockSpec((None, L, RP), lambda g, b, c: (b, nc - 1 - c, oz + g))
    lanes = pl.BlockSpec((None, L, LANES), lambda g, b, c: (b, nc - 1 - c, g))
    xg = pl.BlockSpec((None, L, RP + 2 * N), lambda g, b, c: (b, nc - 1 - c, g))
    chan = pl.BlockSpec((1, RP), lambda g, b, c: (0, g))
    wide = jax.ShapeDtypeStruct((Bl, S, G * LANES), F32)
    return pl.pallas_call(
        body, name=name, grid=(G, Bl, nc),
        in_specs=[grp, grp, zspec, xg,
                  pl.BlockSpec((None, None, None, RP, N), lambda g, b, c: (b, g, nc - 1 - c, 0, 0)),
                  lanes, lanes, pl.BlockSpec((None, None, 8, L), lambda g, b, c: (b, nc - 1 - c, g, 0)),
                  chan, chan, _ANY],
        out_specs=[zspec, xg, lanes, lanes, chan, chan],
        out_shape=[jax.ShapeDtypeStruct(dproj.shape, dproj.dtype), jax.ShapeDtypeStruct(xbc_a.shape, F32),
                   wide, wide, jax.ShapeDtypeStruct((1, DI), F32), jax.ShapeDtypeStruct((1, DI), F32)],
        input_output_aliases={10: 0},
        scratch_shapes=[pltpu.VMEM((RP, N), F32)],
        compiler_params=_cp(("arbitrary", "arbitrary", "arbitrary")),
    )(dyn, y, proj, xbc_a, hprev, dtg4, acs4, acs_t4, dsk_exp, norm_g.reshape(1, DI), dproj)


def _adam_update(w, m, v, g):
    c1 = 1.0 - ADAM_B1 ** ADAM_STEP
    c2 = 1.0 - ADAM_B2 ** ADAM_STEP
    nm = ADAM_B1 * m + (1.0 - ADAM_B1) * g
    nv = ADAM_B2 * v + (1.0 - ADAM_B2) * (g * g)
    return nm, nv, -ADAM_LR * ((nm / c1) / (jnp.sqrt(nv / c2) + ADAM_EPS) + ADAM_WD * w)


def _adamw(w, m, v, g, name, tr=128):
    rows, cols = w.shape
    tr = _tile(rows, tr, 8)

    def body(w_ref, m_ref, v_ref, ga_ref, g_ref, d_ref, nm_ref, nv_ref):
        g = ga_ref[...]
        g_ref[...] = g
        nm_ref[...], nv_ref[...], d_ref[...] = _adam_update(w_ref[...], m_ref[...], v_ref[...], g)

    blk = pl.BlockSpec((tr, cols), lambda i: (i, 0))
    shp = jax.ShapeDtypeStruct((rows, cols), F32)
    return pl.pallas_call(
        body, name=name, grid=(rows // tr,), in_specs=[blk] * 4, out_specs=[blk] * 4,
        out_shape=[shp] * 4, compiler_params=_cp(("parallel",)),
    )(w, m, v, g)


def _adamw_layers(w, m, v, g_mine, g_theirs, core, name, tr=128):
    _, rows, cols = w.shape
    tr = _tile(rows, tr, 8)

    def body(c_ref, w_ref, m_ref, v_ref, ga_ref, gb_ref, g_ref, d_ref, nm_ref, nv_ref):
        g = jnp.where(pl.program_id(0) == c_ref[0], ga_ref[...], gb_ref[...])
        g_ref[...] = g
        nm_ref[...], nv_ref[...], d_ref[...] = _adam_update(w_ref[...], m_ref[...], v_ref[...], g)

    lay = pl.BlockSpec((None, tr, cols), lambda l, i, c_ref: (l, i, 0))
    one = pl.BlockSpec((tr, cols), lambda l, i, c_ref: (i, 0))
    shp = jax.ShapeDtypeStruct(w.shape, F32)
    return pl.pallas_call(
        body, name=name,
        grid_spec=pltpu.PrefetchScalarGridSpec(num_scalar_prefetch=1, grid=(2, rows // tr),
                                               in_specs=[lay, lay, lay, one, one], out_specs=[lay] * 4),
        out_shape=[shp] * 4, compiler_params=_cp(("parallel", "parallel")),
    )(core, w, m, v, g_mine, g_theirs)


def _sum_slots(buf, name, tr=256):
    n, rows, cols = buf.shape
    tr = _tile(rows, tr, 8)

    def body(b_ref, o_ref):
        acc = b_ref[0].astype(F32)
        for k in range(1, n):
            acc = acc + b_ref[k].astype(F32)
        o_ref[...] = acc

    return pl.pallas_call(
        body, name=name, grid=(rows // tr,),
        in_specs=[pl.BlockSpec((n, tr, cols), lambda i: (0, i, 0))],
        out_specs=pl.BlockSpec((tr, cols), lambda i: (i, 0)),
        out_shape=jax.ShapeDtypeStruct((rows, cols), F32), compiler_params=_cp(("parallel",)),
    )(buf)


_ANY = pl.BlockSpec(memory_space=pl.ANY)


def _exchange_chips(src, per_dest, name):
    rows, cols = src.shape[-2:]

    def body(in_ref, out_ref, send_sems, recv_sems, local_sem):
        x, y, c = lax.axis_index("x"), lax.axis_index("y"), lax.axis_index("c")
        me = 2 * x + y
        chips = [(1 - x, y), (x, 1 - y), (1 - x, 1 - y)]

        def block(j):
            return in_ref.at[j] if per_dest else in_ref

        mine = pltpu.make_async_copy(block(me), out_ref.at[me], local_sem)
        mine.start()
        sends = []
        for k, (px, py) in enumerate(chips):
            cp = pltpu.make_async_remote_copy(
                src_ref=block(2 * px + py), dst_ref=out_ref.at[me], send_sem=send_sems.at[k],
                recv_sem=recv_sems.at[k], device_id=(px, py, c), device_id_type=MESH)
            cp.start()
            sends.append(cp)
        for k, (px, py) in enumerate(chips):
            pltpu.make_async_remote_copy(
                src_ref=block(me), dst_ref=out_ref.at[2 * px + py], send_sem=send_sems.at[k],
                recv_sem=recv_sems.at[k], device_id=(px, py, c), device_id_type=MESH).wait_recv()
        for cp in sends:
            cp.wait_send()
        mine.wait()

    return pl.pallas_call(
        body, name=name, in_specs=[_ANY], out_specs=_ANY,
        out_shape=jax.ShapeDtypeStruct((N_CHIPS, rows, cols), src.dtype),
        scratch_shapes=[pltpu.SemaphoreType.DMA((3,)), pltpu.SemaphoreType.DMA((3,)), pltpu.SemaphoreType.DMA(())],
    )(src)


def _shard_window(ref, kind, j, lead=()):
    if kind == "slots":
        return ref.at[(j,) + lead]
    r, c = ref.shape[-2] // (N_CHIPS if kind == "rows" else 1), ref.shape[-1] // (N_CHIPS if kind == "cols" else 1)
    full = tuple(slice(None) for _ in range(len(ref.shape) - 2 - len(lead)))
    if kind == "rows":
        return ref.at[lead + full + (pl.ds(pl.multiple_of(j * r, 16), r), slice(None))]
    return ref.at[lead + full + (slice(None), pl.ds(pl.multiple_of(j * c, LANES), c))]


def _gather_kind(shard, axis):
    if axis == 1:
        return "rows"
    return "cols" if shard.shape[2] % LANES == 0 else "slots"


def _gather_weights(shards, axes, name):
    kinds = ["rows" if ax == 1 else "slots" for ax in axes]
    nw = len(shards)

    def out_shape(s, kind):
        d, r, c = s.shape
        shp = {"rows": (d, N_CHIPS * r, c), "cols": (d, r, N_CHIPS * c), "slots": (N_CHIPS, d, r, c)}[kind]
        return jax.ShapeDtypeStruct(shp, s.dtype)

    assert all(s.shape[0] == 2 for s in shards)

    def body(*refs):
        ins, outs = refs[:nw], refs[nw:2 * nw]
        ici_send, ici_recv, d2d_send, d2d_recv, local_sems = refs[2 * nw:]
        x, y, c = lax.axis_index("x"), lax.axis_index("y"), lax.axis_index("c")
        me = 2 * x + y
        chips = [(1 - x, y), (x, 1 - y), (1 - x, 1 - y)]
        sends = []
        for i in range(nw):
            if kinds[i] == "rows":
                own = pltpu.make_async_copy(ins[i], _shard_window(outs[i], kinds[i], me), local_sems.at[i])
                own.start()
                sends.append((own, False))
        for i in range(nw):
            for k, (px, py) in enumerate(chips):
                cp = pltpu.make_async_remote_copy(
                    src_ref=ins[i].at[c], dst_ref=_shard_window(outs[i], kinds[i], me, (c,)),
                    send_sem=ici_send.at[3 * i + k], recv_sem=ici_recv.at[3 * i + k],
                    device_id=(px, py, c), device_id_type=MESH)
                cp.start()
                sends.append((cp, True))
        for i in range(nw):
            for k, (px, py) in enumerate(chips):
                win = _shard_window(outs[i], kinds[i], 2 * px + py, (c,))
                pltpu.make_async_remote_copy(
                    src_ref=ins[i].at[c], dst_ref=win, send_sem=ici_send.at[3 * i + k], recv_sem=ici_recv.at[3 * i + k],
                    device_id=(px, py, c), device_id_type=MESH).wait_recv()
                fw = pltpu.make_async_remote_copy(
                    src_ref=win, dst_ref=win, send_sem=d2d_send.at[3 * i + k], recv_sem=d2d_recv.at[3 * i + k],
                    device_id=(x, y, 1 - c), device_id_type=MESH)
                fw.start()
                sends.append((fw, True))
        for i in range(nw):
            for k, (px, py) in enumerate(chips):
                win = _shard_window(outs[i], kinds[i], 2 * px + py, (1 - c,))
                pltpu.make_async_remote_copy(
                    src_ref=win, dst_ref=win, send_sem=d2d_send.at[3 * i + k], recv_sem=d2d_recv.at[3 * i + k],
                    device_id=(x, y, 1 - c), device_id_type=MESH).wait_recv()
        for cp, remote in sends:
            cp.wait_send() if remote else cp.wait()

    outs = pl.pallas_call(
        body, name=name, in_specs=[_ANY] * nw, out_specs=[_ANY] * nw,
        out_shape=[out_shape(s, k) for s, k in zip(shards, kinds)],
        scratch_shapes=[pltpu.SemaphoreType.DMA((3 * nw,)), pltpu.SemaphoreType.DMA((3 * nw,)),
                        pltpu.SemaphoreType.DMA((3 * nw,)), pltpu.SemaphoreType.DMA((3 * nw,)),
                        pltpu.SemaphoreType.DMA((nw,))],
    )(*shards)
    me = 2 * lax.axis_index("x") + lax.axis_index("y")
    return [jnp.concatenate([jnp.where(me == j, s, o[j]) for j in range(N_CHIPS)], axis=2) if k == "slots" else o
            for s, o, k in zip(shards, outs, kinds)]


def _swap_other_layer(gst, name):
    nv = len(gst)

    def body(*refs):
        ins, outs, send_sems, recv_sems = refs[:nv], refs[nv:2 * nv], refs[2 * nv], refs[2 * nv + 1]
        x, y, c = lax.axis_index("x"), lax.axis_index("y"), lax.axis_index("c")
        cps = [pltpu.make_async_remote_copy(src_ref=ins[i].at[1 - c], dst_ref=outs[i], send_sem=send_sems.at[i],
                                            recv_sem=recv_sems.at[i], device_id=(x, y, 1 - c), device_id_type=MESH)
               for i in range(nv)]
        for cp in cps:
            cp.start()
        for cp in cps:
            cp.wait()

    return pl.pallas_call(
        body, name=name, in_specs=[_ANY] * nv, out_specs=[_ANY] * nv,
        out_shape=[jax.ShapeDtypeStruct(v.shape[1:], v.dtype) for v in gst],
        scratch_shapes=[pltpu.SemaphoreType.DMA((nv,)), pltpu.SemaphoreType.DMA((nv,))],
    )(*gst)


def _pair_sum(g, other, core, name, tr=256):
    _, rows, cols = g.shape
    tr = _tile(rows, tr, 16)

    def body(c_ref, g_ref, o_ref, s_ref):
        s_ref[...] = (g_ref[...].astype(F32) + o_ref[...].astype(F32)).astype(s_ref.dtype)

    blk = pl.BlockSpec((tr, cols), lambda i, c_ref: (i, 0))
    return pl.pallas_call(
        body, name=name,
        grid_spec=pltpu.PrefetchScalarGridSpec(
            num_scalar_prefetch=1, grid=(rows // tr,),
            in_specs=[pl.BlockSpec((None, tr, cols), lambda i, c_ref: (c_ref[0], i, 0)), blk], out_specs=blk),
        out_shape=jax.ShapeDtypeStruct((rows, cols), g.dtype), compiler_params=_cp(("parallel",)),
    )(core, g, other)


def _scatter_layer(parts, kinds, name):
    nw = len(parts)

    def shard_shape(g, kind):
        return g.shape[-2] // (N_CHIPS if kind == "rows" else 1), g.shape[-1] // (N_CHIPS if kind == "cols" else 1)

    def body(*refs):
        ins, outs = refs[:nw], refs[nw:2 * nw]
        send_sems, recv_sems, local_sems = refs[2 * nw:]
        x, y, c = lax.axis_index("x"), lax.axis_index("y"), lax.axis_index("c")
        me = 2 * x + y
        chips = [(1 - x, y), (x, 1 - y), (1 - x, 1 - y)]
        sends = []
        for i in range(nw):
            own = pltpu.make_async_copy(_shard_window(ins[i], kinds[i], me), outs[i].at[me], local_sems.at[i])
            own.start()
            sends.append((own, False))
            for k, (px, py) in enumerate(chips):
                cp = pltpu.make_async_remote_copy(
                    src_ref=_shard_window(ins[i], kinds[i], 2 * px + py), dst_ref=outs[i].at[me],
                    send_sem=send_sems.at[3 * i + k], recv_sem=recv_sems.at[3 * i + k],
                    device_id=(px, py, c), device_id_type=MESH)
                cp.start()
                sends.append((cp, True))
        for i in range(nw):
            for k, (px, py) in enumerate(chips):
                pltpu.make_async_remote_copy(
                    src_ref=_shard_window(ins[i], kinds[i], me), dst_ref=outs[i].at[2 * px + py],
                    send_sem=send_sems.at[3 * i + k], recv_sem=recv_sems.at[3 * i + k],
                    device_id=(px, py, c), device_id_type=MESH).wait_recv()
        for cp, remote in sends:
            cp.wait_send() if remote else cp.wait()

    return pl.pallas_call(
        body, name=name, in_specs=[_ANY] * nw, out_specs=[_ANY] * nw,
        out_shape=[jax.ShapeDtypeStruct((N_CHIPS,) + shard_shape(g, k), g.dtype) for g, k in zip(parts, kinds)],
        scratch_shapes=[pltpu.SemaphoreType.DMA((3 * nw,)), pltpu.SemaphoreType.DMA((3 * nw,)),
                        pltpu.SemaphoreType.DMA((nw,))],
    )(*parts)


def _sibling_swap(vs, name):
    nv = len(vs)

    def body(*refs):
        ins, outs, send_sems, recv_sems = refs[:nv], refs[nv:2 * nv], refs[2 * nv], refs[2 * nv + 1]
        x, y, c = lax.axis_index("x"), lax.axis_index("y"), lax.axis_index("c")
        cps = [pltpu.make_async_remote_copy(src_ref=ins[i], dst_ref=outs[i], send_sem=send_sems.at[i],
                                            recv_sem=recv_sems.at[i], device_id=(x, y, 1 - c), device_id_type=MESH)
               for i in range(nv)]
        for cp in cps:
            cp.start()
        for cp in cps:
            cp.wait()

    return pl.pallas_call(
        body, name=name, in_specs=[_ANY] * nv, out_specs=[_ANY] * nv,
        out_shape=[jax.ShapeDtypeStruct(v.shape, v.dtype) for v in vs],
        scratch_shapes=[pltpu.SemaphoreType.DMA((nv,)), pltpu.SemaphoreType.DMA((nv,))],
    )(*vs)


def _allgather_all(v, name):
    rows, cols = v.shape

    def body(in_ref, out_ref, send_sems, recv_sems, local_sem):
        x, y, c = lax.axis_index("x"), lax.axis_index("y"), lax.axis_index("c")
        me = 4 * x + 2 * y + c
        peers = []
        for k in range(1, N_DEV):
            peers.append(((1 - x) if k & 4 else x, (1 - y) if k & 2 else y, (1 - c) if k & 1 else c))
        mine = pltpu.make_async_copy(in_ref, out_ref.at[me], local_sem)
        mine.start()
        sends = []
        for k, peer in enumerate(peers):
            cp = pltpu.make_async_remote_copy(src_ref=in_ref, dst_ref=out_ref.at[me], send_sem=send_sems.at[k],
                                              recv_sem=recv_sems.at[k], device_id=peer, device_id_type=MESH)
            cp.start()
            sends.append(cp)
        for k, (px, py, pc) in enumerate(peers):
            pltpu.make_async_remote_copy(src_ref=in_ref, dst_ref=out_ref.at[4 * px + 2 * py + pc],
                                         send_sem=send_sems.at[k], recv_sem=recv_sems.at[k],
                                         device_id=(px, py, pc), device_id_type=MESH).wait_recv()
        for cp in sends:
            cp.wait_send()
        mine.wait()

    return pl.pallas_call(
        body, name=name, in_specs=[_ANY], out_specs=_ANY,
        out_shape=jax.ShapeDtypeStruct((N_DEV, rows, cols), v.dtype),
        scratch_shapes=[pltpu.SemaphoreType.DMA((N_DEV - 1,)), pltpu.SemaphoreType.DMA((N_DEV - 1,)),
                        pltpu.SemaphoreType.DMA(())],
    )(v)


def _pack(arrs, dtype, width, row_mult):
    flat = jnp.concatenate([a.reshape(-1).astype(dtype) for a in arrs])
    unit = width * row_mult
    total = -(-flat.shape[0] // unit) * unit
    return jnp.pad(flat, (0, total - flat.shape[0])).reshape(-1, width)


def _unpack(buf, shapes):
    flat = buf.reshape(-1)
    out, off = [], 0
    for shp in shapes:
        n = 1
        for d in shp:
            n *= d
        out.append(flat[off:off + n].reshape(shp))
        off += n
    return out


class _Dims:
    pass


def _dims(x, ssd_dt_bias, ssd_norm_g, ssd_conv_b, ffn_conv_b):
    dm = _Dims()
    dm.Bl, dm.S, dm.D = x.shape
    dm.H, dm.DI, dm.CD = ssd_dt_bias.shape[-1], ssd_norm_g.shape[-1], ssd_conv_b.shape[-1]
    dm.N = (dm.CD - dm.DI) // (2 * N_GROUPS)
    dm.R = dm.H // N_GROUPS
    dm.DFF = ffn_conv_b.shape[-1] // 2
    D = dm.D
    dm.OB, dm.OC, dm.OH, dm.OX = 0, D, 2 * D, 3 * D
    dm.OZ = dm.OX + dm.CD
    dm.OG = -(-(dm.OZ + dm.DI) // (2 * D)) * (2 * D)
    dm.GPAD = dm.OG - (dm.OZ + dm.DI)
    dm.ODT = dm.OG + 2 * D
    dm.NP = dm.ODT + LANES
    assert dm.OZ % (dm.R * HEAD_DIM) == 0
    assert dm.DI // dm.H == HEAD_DIM and dm.N == LANES and dm.R % 2 == 0 and dm.S % CHUNK == 0 and dm.H <= LANES
    return dm


def _group_xbc(v, dm):
    rp, n = dm.R * HEAD_DIM, dm.N
    parts = []
    for g in range(N_GROUPS):
        parts += [v[..., g * rp:(g + 1) * rp], v[..., dm.DI + g * n:dm.DI + (g + 1) * n],
                  v[..., dm.DI + (N_GROUPS + g) * n:dm.DI + (N_GROUPS + g + 1) * n]]
    return jnp.concatenate(parts, axis=-1)


def _ungroup_xbc(v, dm):
    rp, n = dm.R * HEAD_DIM, dm.N
    gw = rp + 2 * n
    xs = [v[..., g * gw:g * gw + rp] for g in range(N_GROUPS)]
    bs = [v[..., g * gw + rp:g * gw + rp + n] for g in range(N_GROUPS)]
    cs = [v[..., g * gw + rp + n:(g + 1) * gw] for g in range(N_GROUPS)]
    return jnp.concatenate(xs + bs + cs, axis=-1)


def _permute_w_in(w, dm):
    o, sc = dm.DI + dm.CD, dm.DI + dm.CD + dm.H
    zeros = lambda n: jnp.zeros((w.shape[0], n), w.dtype)
    return jnp.concatenate([w[:, sc:sc + 3 * dm.D], _group_xbc(w[:, dm.DI:o], dm), w[:, :dm.DI], zeros(dm.GPAD),
                            w[:, sc + 3 * dm.D:], w[:, o:o + dm.H], zeros(LANES - dm.H)], axis=1)


def _unpermute_w_in(dw, dm):
    return jnp.concatenate([dw[..., dm.OZ:dm.OZ + dm.DI], _ungroup_xbc(dw[..., dm.OX:dm.OX + dm.CD], dm),
                            dw[..., dm.ODT:dm.ODT + dm.H], dw[..., :3 * dm.D], dw[..., dm.OG:dm.OG + 2 * dm.D]], axis=-1)


def _lane_pad(v):
    return jnp.pad(v.reshape(1, -1).astype(F32), ((0, 0), (0, LANES - v.shape[-1])))


def _pad8(w):
    return jnp.pad(w.astype(F32), ((0, 8 - w.shape[0]), (0, 0)))


def _head_select(dm):
    j = jnp.arange(LANES)[None, :, None]
    r = jnp.arange(LANES)[None, None, :]
    g = jnp.arange(N_GROUPS)[:, None, None]
    sel = ((j == dm.R * g + r) & (r < dm.R)).astype(F32)
    selcat = jnp.transpose(sel, (1, 0, 2)).reshape(LANES, N_GROUPS * LANES)
    selbig = jnp.transpose(sel[:, :, :8], (0, 2, 1)).reshape(N_GROUPS * 8, LANES)
    return selcat, selbig


def _mix_fwd(dm, h, w, sp, sel, tag):
    Bl, S, D = dm.Bl, dm.S, dm.D
    T = Bl * S
    proj = _matmul(h.reshape(T, D), w["w_in_p"], "NN", F32, tag + "_in_proj", tm=2048, tn=1152).reshape(Bl, S, dm.NP)
    xbc_a = _ssd_conv_fwd(proj, sp["ssd_conv_w8"], sp["ssd_conv_b"], dm.OX, dm.CD, tag + "_ssd_conv")
    dtg, acs, acs_t = _ssd_prep(proj, sp["dt_bias"], sp["a_log"], sel[0], sel[1], dm, tag + "_ssd_prep")
    y, yn, hprev = _scan_fwd(xbc_a, proj, dtg, acs, acs_t, sp["dsk_exp"], sp["ssd_norm_g"], dm, tag + "_ssd_scan")
    y_ssd = _matmul(yn.reshape(T, dm.DI), w["w_ssd_out"], "NN", F32, tag + "_ssd_out", tk=2048).reshape(Bl, S, D)
    s = _sc_conv_fwd(proj, sp["sc_conv_w8"], (dm.OB, dm.OC, dm.OH), D, tag + "_sc_conv")
    y_sc = _matmul(s.reshape(T, D), w["w_sc_out"], "NN", F32, tag + "_sc_out", tk=1024).reshape(Bl, S, D)
    mixin = _merge_fwd(proj, y_ssd, y_sc, dm.OG, D, tag + "_merge")
    mix = _matmul(mixin.reshape(T, D), w["w_o"], "NN", F32, tag + "_o", tk=1024).reshape(Bl, S, D)
    return mix, (h, proj, xbc_a, y, yn, hprev, y_ssd, y_sc, s, mixin, dtg, acs, acs_t)


def _mix_bwd(dm, dmix, saved, w, sp, sel, tag, gbig, slab):
    Bl, S, D = dm.Bl, dm.S, dm.D
    T = Bl * S
    h, proj, xbc_a, y, yn, hprev, y_ssd, y_sc, s, mixin, dtg, acs, acs_t = saved
    dmix2 = dmix.reshape(T, D)
    g = {}
    gbig["w_o"] = _matmul(mixin.reshape(T, D), dmix2, "TN", WIRE_DTYPE, tag + "_dw_o", slab=(gbig.get("w_o"),) + slab)
    dmixin = _matmul(dmix2, w["w_o"], "NT", F32, tag + "_d_o", tk=1024).reshape(Bl, S, D)
    dproj0 = jnp.zeros((Bl, S, dm.NP), MXU_DTYPE) if dm.GPAD else None
    dy_ssd, dy_sc, dproj = _merge_bwd(dmixin, proj, y_ssd, y_sc, dm.OG, D, dproj0, tag + "_merge_bwd")
    gbig["w_sc_out"] = _matmul(s.reshape(T, D), dy_sc.reshape(T, D), "TN", WIRE_DTYPE, tag + "_dw_sc_out",
                               slab=(gbig.get("w_sc_out"),) + slab)
    ds = _matmul(dy_sc.reshape(T, D), w["w_sc_out"], "NT", F32, tag + "_d_sc_out", tk=1024).reshape(Bl, S, D)
    dproj, dscw = _sc_conv_bwd(ds, proj, sp["sc_conv_w8"], (dm.OB, dm.OC, dm.OH), D, dproj, tag + "_sc_conv_bwd")
    g["sc_conv_w"] = dscw[:3]
    gbig["w_ssd_out"] = _matmul(yn.reshape(T, dm.DI), dy_ssd.reshape(T, D), "TN", WIRE_DTYPE, tag + "_dw_ssd_out",
                                slab=(gbig.get("w_ssd_out"),) + slab)
    dyn = _matmul(dy_ssd.reshape(T, D), w["w_ssd_out"], "NT", F32, tag + "_d_ssd_out", tk=1024).reshape(Bl, S, dm.DI)
    dproj, dxa, d_a, ddt, ddsk, dng = _scan_bwd(dyn, y, xbc_a, proj, hprev, dtg, acs, acs_t, sp["dsk_exp"],
                                                sp["ssd_norm_g"], dproj, dm, tag + "_ssd_scan_bwd")
    dproj, dpar = _ssd_post(d_a, ddt, dtg, proj, sp["dt_bias"], sp["a_log"], sel[0], dproj, dm, tag + "_ssd_post")
    g["ssd_dt_bias"], g["ssd_a_log"] = dpar[0, :dm.H], dpar[1, :dm.H]
    g["ssd_d"] = jnp.sum(ddsk.reshape(dm.H, HEAD_DIM), axis=-1)
    g["ssd_norm_g"] = dng[0]
    dproj, dcw, dcb = _ssd_conv_bwd(dxa, proj, sp["ssd_conv_w8"], sp["ssd_conv_b"], dm.OX, dm.CD, dproj,
                                    tag + "_ssd_conv_bwd")
    g["ssd_conv_w"], g["ssd_conv_b"] = _ungroup_xbc(dcw[:4], dm), _ungroup_xbc(dcb[0], dm)
    dproj = dproj.reshape(T, dm.NP)
    gbig["w_in_p"] = _matmul(h.reshape(T, D), dproj, "TN", WIRE_DTYPE, tag + "_dw_in", tn=1728,
                             slab=(gbig.get("w_in_p"),) + slab)
    dh = _matmul(dproj, w["w_in_p"], "NT", F32, tag + "_d_in", tm=2048, tk=1152).reshape(Bl, S, D)
    return dh, g


def _ffn_fwd(dm, h, w, sp, tag):
    Bl, S, D = dm.Bl, dm.S, dm.D
    T = Bl * S
    up = _matmul(h.reshape(T, D), w["w_up"], "NN", F32, tag + "_up", tm=2048, tn=1408).reshape(Bl, S, 2 * dm.DFF)
    a = _ffn_conv_fwd(up, sp["ffn_conv_w8"], sp["ffn_conv_b"], dm.DFF, tag + "_ffn_conv")
    f = _matmul(a.reshape(T, dm.DFF), w["w_down"], "NN", F32, tag + "_down", tk=2816).reshape(Bl, S, D)
    return f, (h, up, a)


def _ffn_bwd(dm, df, saved, w, sp, tag, gbig, slab):
    Bl, S, D = dm.Bl, dm.S, dm.D
    T = Bl * S
    h, up, a = saved
    df2 = df.reshape(T, D)
    g = {}
    gbig["w_down"] = _matmul(a.reshape(T, dm.DFF), df2, "TN", WIRE_DTYPE, tag + "_dw_down", tm=1408,
                             slab=(gbig.get("w_down"),) + slab)
    da = _matmul(df2, w["w_down"], "NT", F32, tag + "_d_down", tn=1408).reshape(Bl, S, dm.DFF)
    dup, dcw, dcb = _ffn_conv_bwd(da, up, sp["ffn_conv_w8"], sp["ffn_conv_b"], dm.DFF, tag + "_ffn_conv_bwd")
    g["ffn_conv_w"], g["ffn_conv_b"] = dcw[:3], dcb[0]
    dup = dup.reshape(T, 2 * dm.DFF)
    gbig["w_up"] = _matmul(h.reshape(T, D), dup, "TN", WIRE_DTYPE, tag + "_dw_up", tn=1408,
                           slab=(gbig.get("w_up"),) + slab)
    dh = _matmul(dup, w["w_up"], "NT", F32, tag + "_d_up", tm=2048, tk=1408).reshape(Bl, S, D)
    return dh, g


def _local_step(dm, x, c, target, wfull, small):
    Bl, S, D = dm.Bl, dm.S, dm.D
    depth = len(wfull)
    sel = _head_select(dm)
    c16 = jnp.pad(c.astype(F32), ((0, 16 - Bl), (0, 0)))
    sps, mods, acts = [], [], []
    for l in range(depth):
        sm = small[l]
        sps.append(dict(
            ssd_conv_w8=_pad8(_group_xbc(sm["ssd_conv_w"], dm)), ssd_conv_b=_group_xbc(sm["ssd_conv_b"], dm),
            dt_bias=_lane_pad(sm["ssd_dt_bias"]),
            a_log=_lane_pad(sm["ssd_a_log"]), dsk_exp=jnp.repeat(sm["ssd_d"].astype(F32), HEAD_DIM).reshape(1, dm.DI),
            ssd_norm_g=sm["ssd_norm_g"],
            sc_conv_w8=_pad8(sm["sc_conv_w"]), ffn_conv_w8=_pad8(sm["ffn_conv_w"]), ffn_conv_b=sm["ffn_conv_b"]))
        act, mod = _ada_fwd(c16, wfull[l]["ada_w"], sm["ada_b"], f"l{l}_ada")
        acts.append(act)
        mods.append(jnp.pad(mod[:Bl].reshape(Bl, 6, D), ((0, 0), (0, 2), (0, 0))))

    def sub(i):
        l, ffn = i // 2, i % 2
        sm = small[l]
        return dict(l=l, ffn=ffn, pre_g=sm["ffn_pre_g" if ffn else "mix_pre_g"],
                    post_g=sm["ffn_post_g" if ffn else "mix_post_g"], mod=mods[l], row=3 * ffn,
                    tag=f"l{l}_{'ffn' if ffn else 'mix'}")

    nsub = 2 * depth
    subs = [sub(i) for i in range(nsub)]
    xs, fs, saves = [x], [], []
    h = _norm_mod_fwd(x, subs[0]["pre_g"], subs[0]["mod"], subs[0]["row"], "l0_mix_pre_norm")
    for i, sb in enumerate(subs):
        l = sb["l"]
        if sb["ffn"]:
            f, sv = _ffn_fwd(dm, h, wfull[l], sps[l], sb["tag"])
        else:
            f, sv = _mix_fwd(dm, h, wfull[l], sps[l], sel, sb["tag"])
        nxt = None
        if i + 1 < nsub:
            nb = subs[i + 1]
            nxt = (nb["pre_g"], nb["mod"], nb["row"])
        xn, h = _post_norm_fwd(xs[-1], f, sb["post_g"], sb["mod"], sb["row"] + 2, nxt, sb["tag"] + "_post_norm")
        xs.append(xn)
        fs.append(f)
        saves.append(sv)

    dy, loss = _loss_fwd_bwd(xs[-1], target, "loss")

    grads = [dict() for _ in range(depth)]
    gbig = {}
    dmod = [[None] * 6 for _ in range(depth)]
    dx, dh = dy, None
    for i in reversed(range(nsub)):
        sb = subs[i]
        l = sb["l"]
        nxt = None
        if i + 1 < nsub:
            nb = subs[i + 1]
            nxt = (dh, xs[i + 1], nb["pre_g"], nb["mod"], nb["row"])
        dx, df, pb, shg = _norm_bwd(dx, nxt, (fs[i], sb["post_g"], sb["mod"], sb["row"] + 2), sb["tag"] + "_post_norm_bwd")
        if nxt is not None:
            nb = subs[i + 1]
            dmod[nb["l"]][nb["row"]], dmod[nb["l"]][nb["row"] + 1] = pb[:, 0], pb[:, 1]
            grads[nb["l"]]["ffn_pre_g" if nb["ffn"] else "mix_pre_g"] = shg[0]
        dmod[l][sb["row"] + 2] = pb[:, 2]
        grads[l]["ffn_post_g" if sb["ffn"] else "mix_post_g"] = shg[1]
        if sb["ffn"]:
            dh, g = _ffn_bwd(dm, df, saves[i], wfull[l], sps[l], sb["tag"], gbig, (l, depth))
        else:
            dh, g = _mix_bwd(dm, df, saves[i], wfull[l], sps[l], sel, sb["tag"], gbig, (l, depth))
        grads[l].update(g)
    sb = subs[0]
    grad_x, _, pb, shg = _norm_bwd(dx, (dh, xs[0], sb["pre_g"], sb["mod"], sb["row"]), None, "l0_mix_pre_norm_bwd")
    dmod[0][0], dmod[0][1] = pb[:, 0], pb[:, 1]
    grads[0]["mix_pre_g"] = shg[0]

    for l in range(depth):
        dm6 = jnp.concatenate(dmod[l], axis=-1)
        grads[l]["ada_b"] = jnp.sum(dm6, axis=0)
        dm16 = jnp.pad(dm6, ((0, 16 - Bl), (0, 0))).astype(MXU_DTYPE)
        gbig["ada_w"] = _matmul(acts[l], dm16, "TN", WIRE_DTYPE, f"l{l}_dw_ada", slab=(gbig.get("ada_w"), l, depth))
    return loss, grad_x, grads, gbig


_WEIGHTS = ("ada_w", "ada_b", "mix_pre_g", "mix_post_g", "w_in", "ssd_conv_w", "ssd_conv_b", "ssd_dt_bias",
            "ssd_a_log", "ssd_d", "ssd_norm_g", "w_ssd_out", "sc_conv_w", "w_sc_out", "w_o", "ffn_pre_g",
            "ffn_post_g", "w_up", "ffn_conv_w", "ffn_conv_b", "w_down")
_INPUTS = ("x", "c") + _WEIGHTS + ("loss_target",) + tuple("m_" + n for n in _WEIGHTS) + tuple("v_" + n for n in _WEIGHTS)
_BIG = {"ada_w": 2, "w_in": 2, "w_ssd_out": 1, "w_sc_out": 1, "w_o": 1, "w_up": 2, "w_down": 1}
_CONV = ("ssd_conv_w", "sc_conv_w", "ffn_conv_w")
_SMALL = tuple(n for n in _WEIGHTS if n not in _BIG)


def _step(a):
    x, c, target = a["x"], a["c"], a["loss_target"]
    depth = a["ada_w"].shape[0]
    dm = _dims(x, a["ssd_dt_bias"], a["ssd_norm_g"], a["ssd_conv_b"], a["ffn_conv_b"])
    chip = 2 * lax.axis_index("x") + lax.axis_index("y")

    shards = [a[n].astype(WIRE_DTYPE) for n in _BIG]
    axes = list(_BIG.values())
    kinds = [_gather_kind(s, ax) for s, ax in zip(shards, axes)]
    full = {n: w.astype(MXU_DTYPE) for n, w in zip(_BIG, _gather_weights(shards, axes, "gather_weights"))}
    conv_shapes = [a[n].shape for n in _CONV]
    gotc = _exchange_chips(_pack([a[n] for n in _CONV], F32, LANES, 8), False, "gather_conv_weights")
    piecesc = [_unpack(gotc[j], conv_shapes) for j in range(N_CHIPS)]
    fullc = {n: jnp.concatenate([piecesc[j][i] for j in range(N_CHIPS)], axis=2) for i, n in enumerate(_CONV)}

    wfull, small = [], []
    for l in range(depth):
        wf = {n: full[n][l] for n in _BIG if n != "w_in"}
        wf["w_in_p"] = _permute_w_in(full["w_in"][l], dm)
        wf["w_up"] = _interleave_up(wf["w_up"], dm.DFF)
        wfull.append(wf)
        small.append({n: (fullc[n][l] if n in _CONV else a[n][l]) for n in _SMALL})

    loss_part, grad_x, grads, gbig = _local_step(dm, x, c, target, wfull, small)

    core = lax.axis_index("c").astype(jnp.int32).reshape(1)
    gst = []
    for n, kind in zip(_BIG, kinds):
        g = _unpermute_w_in(gbig["w_in_p"], dm) if n == "w_in" else gbig[n]
        if n == "w_up":
            g = _deinterleave_up(g, dm.DFF)
        if kind == "slots":
            g = jnp.moveaxis(g.reshape(depth, g.shape[1], N_CHIPS, g.shape[2] // N_CHIPS), 2, 1)
            g = g.reshape(depth, -1, g.shape[-1])
        gst.append(g)
    other = _swap_other_layer(gst, "swap_layer_grads")
    parts = [_pair_sum(g, o, core, "pair_sum_" + n) for g, o, n in zip(gst, other, _BIG)]
    parts = [p.reshape(N_CHIPS, -1, p.shape[-1]) if k == "slots" else p for p, k in zip(parts, kinds)]
    got = _scatter_layer(parts, kinds, "scatter_grads")
    mine = [_sum_slots(g, "sum_chip_grads_" + n) for g, n in zip(got, _BIG)]
    theirs = _sibling_swap(mine, "swap_core_grads")

    out = {}
    for i, n in enumerate(_BIG):
        out[n] = _adamw_layers(a[n], a["m_" + n], a["v_" + n], mine[i], theirs[i], core, "adamw_" + n)

    gsmall = [jnp.stack([grads[l][n] for l in range(depth)]) for n in _SMALL]
    small_shapes = [g.shape for g in gsmall]
    summed = _sum_slots(_allgather_all(_pack(gsmall, F32, LANES, 8), "gather_small_grads"), "sum_small_grads")
    gs = dict(zip(_SMALL, _unpack(summed, small_shapes)))
    for n in _CONV:
        wcols = a[n].shape[2]
        gs[n] = lax.dynamic_slice_in_dim(gs[n], chip * wcols, wcols, axis=2)
    local_shapes = [a[n].shape for n in _SMALL]
    res = _adamw(_pack([a[n] for n in _SMALL], F32, LANES, 8), _pack([a["m_" + n] for n in _SMALL], F32, LANES, 8),
                 _pack([a["v_" + n] for n in _SMALL], F32, LANES, 8), _pack([gs[n] for n in _SMALL], F32, LANES, 8),
                 "adamw_small")
    res = [_unpack(r, local_shapes) for r in res]
    for i, n in enumerate(_SMALL):
        out[n] = [r[i] for r in res]

    loss = lax.psum(loss_part, ("x", "y", "c"))
    return (loss, grad_x) + tuple(out[n][k] for k in range(4) for n in _WEIGHTS)


def kernel(x, c, ada_w, ada_b, mix_pre_g, mix_post_g, w_in, ssd_conv_w, ssd_conv_b, ssd_dt_bias, ssd_a_log, ssd_d, ssd_norm_g, w_ssd_out, sc_conv_w, w_sc_out, w_o, ffn_pre_g, ffn_post_g, w_up, ffn_conv_w, ffn_conv_b, w_down, loss_target, m_ada_w, m_ada_b, m_mix_pre_g, m_mix_post_g, m_w_in, m_ssd_conv_w, m_ssd_conv_b, m_ssd_dt_bias, m_ssd_a_log, m_ssd_d, m_ssd_norm_g, m_w_ssd_out, m_sc_conv_w, m_w_sc_out, m_w_o, m_ffn_pre_g, m_ffn_post_g, m_w_up, m_ffn_conv_w, m_ffn_conv_b, m_w_down, v_ada_w, v_ada_b, v_mix_pre_g, v_mix_post_g, v_w_in, v_ssd_conv_w, v_ssd_conv_b, v_ssd_dt_bias, v_ssd_a_log, v_ssd_d, v_ssd_norm_g, v_w_ssd_out, v_sc_conv_w, v_w_sc_out, v_w_o, v_ffn_pre_g, v_ffn_post_g, v_w_up, v_ffn_conv_w, v_ffn_conv_b, v_w_down):
    return _step(dict(zip(_INPUTS, (
        x, c, ada_w, ada_b, mix_pre_g, mix_post_g, w_in, ssd_conv_w, ssd_conv_b, ssd_dt_bias, ssd_a_log, ssd_d, ssd_norm_g, w_ssd_out, sc_conv_w, w_sc_out, w_o, ffn_pre_g, ffn_post_g, w_up, ffn_conv_w, ffn_conv_b, w_down, loss_target, m_ada_w, m_ada_b, m_mix_pre_g, m_mix_post_g, m_w_in, m_ssd_conv_w, m_ssd_conv_b, m_ssd_dt_bias, m_ssd_a_log, m_ssd_d, m_ssd_norm_g, m_w_ssd_out, m_sc_conv_w, m_w_sc_out, m_w_o, m_ffn_pre_g, m_ffn_post_g, m_w_up, m_ffn_conv_w, m_ffn_conv_b, m_w_down, v_ada_w, v_ada_b, v_mix_pre_g, v_mix_post_g, v_w_in, v_ssd_conv_w, v_ssd_conv_b, v_ssd_dt_bias, v_ssd_a_log, v_ssd_d, v_ssd_norm_g, v_w_ssd_out, v_sc_conv_w, v_w_sc_out, v_w_o, v_ffn_pre_g, v_ffn_post_g, v_w_up, v_ffn_conv_w, v_ffn_conv_b, v_w_down))))
```

```python
import math

import jax
import jax.numpy as jnp
from jax import lax
from jax.experimental import pallas as pl
from jax.experimental.pallas import tpu as pltpu

F32 = jnp.float32
MXU_DTYPE = jnp.bfloat16
WIRE_DTYPE = jnp.bfloat16
EPS = 1e-6
N_GROUPS = 4
CHUNK = 128
HEAD_DIM = 64
LANES = 128
HALO = 8
N_CHIPS = 4
N_DEV = 8
VMEM_LIMIT = 56 * 1024 * 1024
ADAM_LR, ADAM_B1, ADAM_B2, ADAM_EPS, ADAM_WD, ADAM_STEP = 0.001, 0.9, 0.999, 1e-08, 0.01, 10
MESH = pl.DeviceIdType.MESH

NN = (((1,), (0,)), ((), ()))
NT = (((1,), (1,)), ((), ()))
TN = (((0,), (0,)), ((), ()))


def _dg(a, b, dn, precision=None):
    return lax.dot_general(a, b, dn, precision=precision, preferred_element_type=F32)


def _tile(dim, pref, mult=LANES):
    t = (min(pref, dim) // mult) * mult
    while t >= mult:
        if dim % t == 0:
            return t
        t -= mult
    return dim


def _cp(sem):
    return pltpu.CompilerParams(dimension_semantics=sem, vmem_limit_bytes=VMEM_LIMIT)


def _sigmoid(x):
    return 1.0 / (1.0 + jnp.exp(-x))


def _softplus(x):
    return jnp.maximum(x, 0.0) + jnp.log1p(jnp.exp(-jnp.abs(x)))


def _matmul(a, b, mode, out_dtype, name, tm=1024, tn=1024, tk=1024, slab=None):
    if mode == "NN":
        (M, K), N = a.shape, b.shape[1]
    elif mode == "NT":
        (M, K), N = a.shape, b.shape[0]
    else:
        (K, M), N = a.shape, b.shape[1]
    tm, tn, tk = _tile(M, tm), _tile(N, tn), _tile(K, tk)
    nk = K // tk
    dn = {"NN": NN, "NT": NT, "TN": TN}[mode]
    carry = slab is not None and slab[0] is not None

    def body_one(a_ref, b_ref, *rest):
        o_ref = rest[-1]
        o_ref[...] = _dg(a_ref[...], b_ref[...], dn).astype(o_ref.dtype)

    def body_acc(a_ref, b_ref, *rest):
        o_ref, acc_ref = rest[-2:]
        k = pl.program_id(2)

        @pl.when(k == 0)
        def _():
            acc_ref[...] = jnp.zeros_like(acc_ref)

        acc_ref[...] += _dg(a_ref[...], b_ref[...], dn)

        @pl.when(k == nk - 1)
        def _():
            o_ref[...] = acc_ref[...].astype(o_ref.dtype)

    a_spec = (pl.BlockSpec((tk, tm), lambda i, j, k: (k, i)) if mode == "TN"
              else pl.BlockSpec((tm, tk), lambda i, j, k: (i, k)))
    b_spec = (pl.BlockSpec((tn, tk), lambda i, j, k: (j, k)) if mode == "NT"
              else pl.BlockSpec((tk, tn), lambda i, j, k: (k, j)))
    if slab is None:
        out_spec = pl.BlockSpec((tm, tn), lambda i, j, k: (i, j))
        out_shape = jax.ShapeDtypeStruct((M, N), out_dtype)
    else:
        layer = slab[1]
        out_spec = pl.BlockSpec((None, tm, tn), lambda i, j, k: (layer, i, j))
        out_shape = jax.ShapeDtypeStruct((slab[2], M, N), out_dtype)
    return pl.pallas_call(
        body_one if nk == 1 else body_acc, name=name, grid=(M // tm, N // tn, nk),
        in_specs=[a_spec, b_spec] + ([_ANY] if carry else []),
        out_specs=out_spec, out_shape=out_shape,
        input_output_aliases={2: 0} if carry else {},
        scratch_shapes=[] if nk == 1 else [pltpu.VMEM((tm, tn), F32)],
        compiler_params=_cp(("parallel", "parallel", "arbitrary")),
    )(*((a, b, slab[0]) if carry else (a, b)))


def _ada_fwd(c16, ada_w, ada_b, name):
    rows, D = c16.shape
    N6 = ada_w.shape[1]
    tn = _tile(N6, 1536)

    def body(c_ref, w_ref, b_ref, act_ref, mod_ref):
        c = c_ref[...]
        act = (c * _sigmoid(c)).astype(act_ref.dtype)
        act_ref[...] = act
        mod_ref[...] = _dg(act, w_ref[...], NN) + b_ref[...]

    return pl.pallas_call(
        body, name=name, grid=(N6 // tn,),
        in_specs=[pl.BlockSpec((rows, D), lambda j: (0, 0)),
                  pl.BlockSpec((D, tn), lambda j: (0, j)),
                  pl.BlockSpec((1, tn), lambda j: (0, j))],
        out_specs=[pl.BlockSpec((rows, D), lambda j: (0, 0)),
                   pl.BlockSpec((rows, tn), lambda j: (0, j))],
        out_shape=[jax.ShapeDtypeStruct((rows, D), MXU_DTYPE),
                   jax.ShapeDtypeStruct((rows, N6), F32)],
        compiler_params=_cp(("arbitrary",)),
    )(c16, ada_w, ada_b.reshape(1, N6))


def _norm_mod_rows(x, g, sc, sh):
    r = lax.rsqrt(jnp.mean(x * x, axis=-1, keepdims=True) + EPS)
    return ((x * r) * g) * (1.0 + sc) + sh


def _norm_mod_fwd(x, g, mod, row_sh, name, ts=512):
    Bl, S, D = x.shape
    ts = _tile(S, ts, 8)

    def body(x_ref, g_ref, mod_ref, h_ref):
        sh = mod_ref[row_sh:row_sh + 1, :]
        sc = mod_ref[row_sh + 1:row_sh + 2, :]
        h_ref[...] = _norm_mod_rows(x_ref[...], g_ref[...], sc, sh).astype(h_ref.dtype)

    tok = pl.BlockSpec((None, ts, D), lambda b, s: (b, s, 0))
    return pl.pallas_call(
        body, name=name, grid=(Bl, S // ts),
        in_specs=[tok, pl.BlockSpec((1, D), lambda b, s: (0, 0)),
                  pl.BlockSpec((None, 8, D), lambda b, s: (b, 0, 0))],
        out_specs=tok, out_shape=jax.ShapeDtypeStruct((Bl, S, D), MXU_DTYPE),
        compiler_params=_cp(("parallel", "parallel")),
    )(x, g.reshape(1, D), mod)


def _post_norm_fwd(xp, f, post_g, mod, row_gt, nxt, name, ts=512):
    Bl, S, D = xp.shape
    ts = _tile(S, ts, 8)
    has_next = nxt is not None

    def body(*refs):
        if has_next:
            xp_ref, f_ref, pg_ref, mod_ref, ng_ref, nmod_ref, x_ref, h_ref = refs
        else:
            xp_ref, f_ref, pg_ref, mod_ref, x_ref = refs
        f = f_ref[...]
        r = lax.rsqrt(jnp.mean(f * f, axis=-1, keepdims=True) + EPS)
        x = xp_ref[...] + mod_ref[row_gt:row_gt + 1, :] * ((f * r) * pg_ref[...])
        x_ref[...] = x
        if has_next:
            rs = nxt[2]
            h_ref[...] = _norm_mod_rows(x, ng_ref[...], nmod_ref[rs + 1:rs + 2, :], nmod_ref[rs:rs + 1, :]).astype(h_ref.dtype)

    tok = pl.BlockSpec((None, ts, D), lambda b, s: (b, s, 0))
    vec = pl.BlockSpec((1, D), lambda b, s: (0, 0))
    modspec = pl.BlockSpec((None, 8, D), lambda b, s: (b, 0, 0))
    ins = [xp, f, post_g.reshape(1, D), mod]
    in_specs = [tok, tok, vec, modspec]
    out_specs = [tok]
    out_shape = [jax.ShapeDtypeStruct((Bl, S, D), F32)]
    if has_next:
        ins += [nxt[0].reshape(1, D), nxt[1]]
        in_specs += [vec, modspec]
        out_specs += [tok]
        out_shape += [jax.ShapeDtypeStruct((Bl, S, D), MXU_DTYPE)]
    out = pl.pallas_call(
        body, name=name, grid=(Bl, S // ts), in_specs=in_specs, out_specs=out_specs, out_shape=out_shape,
        compiler_params=_cp(("parallel", "parallel")),
    )(*ins)
    return (out[0], out[1]) if has_next else (out[0], None)


def _loss_fwd_bwd(y, target, name, ts=512):
    Bl, S, D = y.shape
    ts = _tile(S, ts, 8)

    def body(y_ref, t_ref, dy_ref, l_ref):
        @pl.when((pl.program_id(0) == 0) & (pl.program_id(1) == 0))
        def _():
            l_ref[...] = jnp.zeros_like(l_ref)

        e = y_ref[...] - t_ref[...]
        dy_ref[...] = e * (1.0 / D)
        l_ref[...] += 0.5 * jnp.sum(jnp.mean(e * e, axis=-1, keepdims=True), axis=0, keepdims=True)

    tok = pl.BlockSpec((None, ts, D), lambda b, s: (b, s, 0))
    dy, l = pl.pallas_call(
        body, name=name, grid=(Bl, S // ts), in_specs=[tok, tok],
        out_specs=[tok, pl.BlockSpec((8, LANES), lambda b, s: (0, 0))],
        out_shape=[jax.ShapeDtypeStruct((Bl, S, D), F32), jax.ShapeDtypeStruct((8, LANES), F32)],
        compiler_params=_cp(("arbitrary", "arbitrary")),
    )(y, target)
    return dy, l[0, 0]


def _norm_bwd(dx_res, nxt, prv, name, ts=256):
    Bl, S, D = dx_res.shape
    ts = _tile(S, ts, 8)
    has_next, has_prev = nxt is not None, prv is not None

    def body(*refs):
        refs = list(refs)
        dxr_ref = refs.pop(0)
        if has_next:
            dh_ref, x_ref, g_ref, nmod_ref = refs[:4]
            refs = refs[4:]
        if has_prev:
            f_ref, pg_ref, pmod_ref = refs[:3]
            refs = refs[3:]
        dx_ref = refs.pop(0)
        if has_prev:
            df_ref = refs.pop(0)
        pb_ref, sh_ref = refs
        b, s = pl.program_id(0), pl.program_id(1)

        @pl.when(s == 0)
        def _():
            pb_ref[...] = jnp.zeros_like(pb_ref)

        @pl.when((b == 0) & (s == 0))
        def _():
            sh_ref[...] = jnp.zeros_like(sh_ref)

        dx = dxr_ref[...]
        if has_next:
            rs = nxt[4]
            x, dh, g = x_ref[...], dh_ref[...], g_ref[...]
            sc1 = 1.0 + nmod_ref[rs + 1:rs + 2, :]
            r = lax.rsqrt(jnp.mean(x * x, axis=-1, keepdims=True) + EPS)
            xn = x * r
            pb_ref[0:1, :] += jnp.sum(dh, axis=0, keepdims=True)
            pb_ref[1:2, :] += jnp.sum(dh * (xn * g), axis=0, keepdims=True)
            sh_ref[0:1, :] += jnp.sum(dh * sc1 * xn, axis=0, keepdims=True)
            dxn = dh * sc1 * g
            dx = dx + r * (dxn - xn * jnp.mean(dxn * xn, axis=-1, keepdims=True))
        dx_ref[...] = dx
        if has_prev:
            rg = prv[3]
            f, pg = f_ref[...], pg_ref[...]
            gt = pmod_ref[rg:rg + 1, :]
            r = lax.rsqrt(jnp.mean(f * f, axis=-1, keepdims=True) + EPS)
            fn = f * r
            pb_ref[2:3, :] += jnp.sum(dx * (fn * pg), axis=0, keepdims=True)
            drn = dx * gt
            sh_ref[1:2, :] += jnp.sum(drn * fn, axis=0, keepdims=True)
            dfn = drn * pg
            df_ref[...] = (r * (dfn - fn * jnp.mean(dfn * fn, axis=-1, keepdims=True))).astype(df_ref.dtype)

    tok = pl.BlockSpec((None, ts, D), lambda b, s: (b, s, 0))
    vec = pl.BlockSpec((1, D), lambda b, s: (0, 0))
    modspec = pl.BlockSpec((None, 8, D), lambda b, s: (b, 0, 0))
    ins, in_specs = [dx_res], [tok]
    if has_next:
        ins += [nxt[0], nxt[1], nxt[2].reshape(1, D), nxt[3]]
        in_specs += [tok, tok, vec, modspec]
    if has_prev:
        ins += [prv[0], prv[1].reshape(1, D), prv[2]]
        in_specs += [tok, vec, modspec]
    out_specs, out_shape = [tok], [jax.ShapeDtypeStruct((Bl, S, D), F32)]
    if has_prev:
        out_specs += [tok]
        out_shape += [jax.ShapeDtypeStruct((Bl, S, D), MXU_DTYPE)]
    out_specs += [modspec, pl.BlockSpec((8, D), lambda b, s: (0, 0))]
    out_shape += [jax.ShapeDtypeStruct((Bl, 8, D), F32), jax.ShapeDtypeStruct((8, D), F32)]
    out = pl.pallas_call(
        body, name=name, grid=(Bl, S // ts), in_specs=in_specs, out_specs=out_specs, out_shape=out_shape,
        compiler_params=_cp(("arbitrary", "arbitrary")),
    )(*ins)
    if has_prev:
        return out[0], out[1], out[2], out[3]
    return out[0], None, out[1], out[2]


def _shift_down(x, j):
    return x if j == 0 else pltpu.roll(x, j, axis=0)


def _shift_up(x, j):
    return x if j == 0 else pltpu.roll(x, x.shape[0] - j, axis=0)


def _conv(xall, w_ref, K):
    y = w_ref[K - 1:K, :] * xall
    for k in range(K - 1):
        y = y + w_ref[k:k + 1, :] * _shift_down(xall, K - 1 - k)
    return y


def _conv_t(dall, xall, w, K, rows):
    xt = xall[HALO:HALO + rows]
    y = w[K - 1:K, :] * dall[HALO:HALO + rows]
    gw = [None] * K
    gw[K - 1] = jnp.sum(dall[HALO:HALO + rows] * xt, axis=0, keepdims=True)
    for k in range(K - 1):
        sh = _shift_up(dall, K - 1 - k)[HALO:HALO + rows]
        y = y + w[k:k + 1, :] * sh
        gw[k] = jnp.sum(sh * xt, axis=0, keepdims=True)
    return y, gw


def _conv_t_rows(dall, w_ref, K):
    y = w_ref[K - 1:K, :] * dall
    for k in range(K - 1):
        y = y + w_ref[k:k + 1, :] * _shift_up(dall, K - 1 - k)
    return y


def _conv_wgrad(acc_ref, dtile, xall, K, ts):
    for k in range(K):
        xs = _shift_down(xall, K - 1 - k)[HALO:HALO + ts]
        acc_ref[k:k + 1, :] += jnp.sum(dtile * xs, axis=0, keepdims=True)


ROW_CHUNK = 64


def _lane_chunks(W):
    lc = 2 * LANES if W % (2 * LANES) == 0 else LANES
    return [(j * lc, lc) for j in range(W // lc)]


def _fill_ext(ext_ref, prev, tile, nxt, ts):
    ext_ref[0:HALO, :] = prev
    ext_ref[HALO:HALO + ts, :] = tile
    ext_ref[HALO + ts:2 * HALO + ts, :] = nxt


def _halo_specs(ts, W, nS, colblk):
    per = ts // HALO
    tile = pl.BlockSpec((None, ts, W), lambda b, c, s: (b, s, colblk(c)))
    prev = pl.BlockSpec((None, HALO, W), lambda b, c, s: (b, jnp.maximum(s * per - 1, 0), colblk(c)))
    nxt = pl.BlockSpec((None, HALO, W), lambda b, c, s: (b, jnp.minimum((s + 1) * per, nS * per - 1), colblk(c)))
    return tile, prev, nxt


def _masked(ref, keep):
    v = ref[...]
    return jnp.where(keep, v, jnp.zeros_like(v))


def _ssd_conv_fwd(proj, w8, b, off, CD, name, ts=512, W=1024):
    Bl, S, _ = proj.shape
    K = 4
    ts, W = _tile(S, ts, 8), _tile(math.gcd(CD, off), W)
    assert off % W == 0
    nS, nW, ob = S // ts, CD // W, off // W

    def body(x_ref, xp_ref, w_ref, b_ref, o_ref):
        xall = jnp.concatenate([_masked(xp_ref, pl.program_id(2) > 0), x_ref[...]], axis=0)
        xc = _conv(xall, w_ref, K)[HALO:] + b_ref[...]
        o_ref[...] = xc * _sigmoid(xc)

    tile, prev, _ = _halo_specs(ts, W, nS, lambda c: ob + c)
    return pl.pallas_call(
        body, name=name, grid=(Bl, nW, nS),
        in_specs=[tile, prev, pl.BlockSpec((8, W), lambda b_, c, s: (0, c)), pl.BlockSpec((1, W), lambda b_, c, s: (0, c))],
        out_specs=pl.BlockSpec((None, ts, W), lambda b_, c, s: (b_, s, c)),
        out_shape=jax.ShapeDtypeStruct((Bl, S, CD), F32),
        compiler_params=_cp(("parallel", "parallel", "parallel")),
    )(proj, proj, w8, b.reshape(1, CD))


def _ssd_conv_bwd(dxa, proj, w8, b, off, CD, dproj, name, ts=256, W=1024):
    Bl, S, _ = proj.shape
    K = 4
    ts, W = _tile(S, ts, 8), _tile(math.gcd(CD, off), W)
    nS, nW, ob = S // ts, CD // W, off // W

    rc = _tile(ts, ROW_CHUNK, 8)

    def body(d_ref, dn_ref, x_ref, xp_ref, xn_ref, w_ref, b_ref, _, dx_ref, dw_ref, db_ref, xext_ref, dext_ref):
        bb, s = pl.program_id(1), pl.program_id(2)

        @pl.when((bb == 0) & (s == 0))
        def _():
            dw_ref[...] = jnp.zeros_like(dw_ref)
            db_ref[...] = jnp.zeros_like(db_ref)

        last = s == nS - 1
        _fill_ext(xext_ref, _masked(xp_ref, s > 0), x_ref[...], _masked(xn_ref, ~last), ts)
        _fill_ext(dext_ref, jnp.zeros((HALO, W), F32), d_ref[...], _masked(dn_ref, ~last), ts)
        for l0, lc in _lane_chunks(W):
            w, bias = w_ref[:, l0:l0 + lc], b_ref[:, l0:l0 + lc]

            def chunk(i, acc):
                r0 = pl.multiple_of(i * rc, 8)
                xw = xext_ref[pl.ds(r0, rc + 2 * HALO), l0:l0 + lc]
                xc = _conv(xw, w, K) + bias
                sg = _sigmoid(xc)
                dxc = dext_ref[pl.ds(r0, rc + 2 * HALO), l0:l0 + lc] * (sg * (1.0 + xc * (1.0 - sg)))
                dx, gw = _conv_t(dxc, xw, w, K, rc)
                dx_ref[pl.ds(r0, rc), l0:l0 + lc] = dx.astype(dx_ref.dtype)
                gb = jnp.sum(dxc[HALO:HALO + rc], axis=0, keepdims=True)
                return tuple(a + g for a, g in zip(acc, gw + [gb]))

            acc = lax.fori_loop(0, ts // rc, chunk, tuple(jnp.zeros((1, lc), F32) for _ in range(K + 1)))
            for k in range(K):
                dw_ref[k:k + 1, l0:l0 + lc] += acc[k]
            db_ref[0:1, l0:l0 + lc] += acc[K]

    per = ts // HALO
    dtile_s = pl.BlockSpec((None, ts, W), lambda c, b_, s: (b_, s, c))
    dnext_s = pl.BlockSpec((None, HALO, W), lambda c, b_, s: (b_, jnp.minimum((s + 1) * per, nS * per - 1), c))
    xtile_s = pl.BlockSpec((None, ts, W), lambda c, b_, s: (b_, s, ob + c))
    xprev_s = pl.BlockSpec((None, HALO, W), lambda c, b_, s: (b_, jnp.maximum(s * per - 1, 0), ob + c))
    xnext_s = pl.BlockSpec((None, HALO, W), lambda c, b_, s: (b_, jnp.minimum((s + 1) * per, nS * per - 1), ob + c))
    wspec = pl.BlockSpec((8, W), lambda c, b_, s: (0, c))
    return pl.pallas_call(
        body, name=name, grid=(nW, Bl, nS),
        in_specs=[dtile_s, dnext_s, xtile_s, xprev_s, xnext_s, wspec, pl.BlockSpec((1, W), lambda c, b_, s: (0, c)),
                  _ANY],
        out_specs=[xtile_s, wspec, wspec],
        out_shape=[jax.ShapeDtypeStruct(dproj.shape, dproj.dtype), jax.ShapeDtypeStruct((8, CD), F32),
                   jax.ShapeDtypeStruct((8, CD), F32)],
        input_output_aliases={7: 0},
        scratch_shapes=[pltpu.VMEM((ts + 2 * HALO, W), F32)] * 2,
        compiler_params=_cp(("arbitrary", "arbitrary", "arbitrary")),
    )(dxa, dxa, proj, proj, proj, w8, b.reshape(1, CD), dproj)


def _sc_conv_fwd(proj, w8, offs, D, name, ts=512, W=1024):
    Bl, S, _ = proj.shape
    K = 3
    ts, W = _tile(S, ts, 8), _tile(D, W)
    nS, nW = S // ts, D // W
    ob, oc, oh = [o // W for o in offs]

    def body(b_ref, c_ref, cp_ref, h_ref, hp_ref, w_ref, o_ref):
        s = pl.program_id(2)
        keep = s > 0
        vall = jnp.concatenate([_masked(cp_ref, keep) * _masked(hp_ref, keep), c_ref[...] * h_ref[...]], axis=0)
        o_ref[...] = (b_ref[...] * _conv(vall, w_ref, K)[HALO:]).astype(o_ref.dtype)

    tb, _, _ = _halo_specs(ts, W, nS, lambda c: ob + c)
    tc, pc, _ = _halo_specs(ts, W, nS, lambda c: oc + c)
    th, ph, _ = _halo_specs(ts, W, nS, lambda c: oh + c)
    return pl.pallas_call(
        body, name=name, grid=(Bl, nW, nS),
        in_specs=[tb, tc, pc, th, ph, pl.BlockSpec((8, W), lambda b_, c, s: (0, c))],
        out_specs=pl.BlockSpec((None, ts, W), lambda b_, c, s: (b_, s, c)),
        out_shape=jax.ShapeDtypeStruct((Bl, S, D), MXU_DTYPE),
        compiler_params=_cp(("parallel", "parallel", "parallel")),
    )(proj, proj, proj, proj, proj, w8)


def _sc_conv_bwd(ds, proj, w8, offs, D, dproj, name, ts=256):
    Bl, S, _ = proj.shape
    K = 3
    ts, W = _tile(S, ts, 8), D
    nS, nW = S // ts, 1
    ob, oc, oh = [o // W for o in offs]
    assert offs[1] == offs[0] + D and offs[2] == offs[0] + 2 * D and offs[0] % (3 * D) == 0

    def body(d_ref, dn_ref, b_ref, bn_ref, c_ref, cp_ref, cn_ref, h_ref, hp_ref, hn_ref, w_ref, _,
             o_ref, dw_ref):
        db_ref, dc_ref, dh_ref = o_ref.at[:, 0:D], o_ref.at[:, D:2 * D], o_ref.at[:, 2 * D:3 * D]
        bb, s = pl.program_id(1), pl.program_id(2)

        @pl.when((bb == 0) & (s == 0))
        def _():
            dw_ref[...] = jnp.zeros_like(dw_ref)

        first, last = s > 0, s < nS - 1
        zeros = jnp.zeros((HALO, W), F32)
        c_t, h_t = c_ref[...], h_ref[...]
        vall = jnp.concatenate([_masked(cp_ref, first) * _masked(hp_ref, first), c_t * h_t,
                                _masked(cn_ref, last) * _masked(hn_ref, last)], axis=0)
        dcv = jnp.concatenate([zeros, d_ref[...] * b_ref[...], _masked(dn_ref, last) * _masked(bn_ref, last)], axis=0)
        cv = _conv(vall, w_ref, K)[HALO:HALO + ts]
        db_ref[...] = (d_ref[...] * cv).astype(db_ref.dtype)
        dv, gw = _conv_t(dcv, vall, w_ref, K, ts)
        dc_ref[...] = (dv * h_t).astype(dc_ref.dtype)
        dh_ref[...] = (dv * c_t).astype(dh_ref.dtype)
        for k in range(K):
            dw_ref[k:k + 1, :] += gw[k]

    per = ts // HALO

    def specs(o):
        t = pl.BlockSpec((None, ts, W), lambda c, b_, s: (b_, s, o + c))
        p = pl.BlockSpec((None, HALO, W), lambda c, b_, s: (b_, jnp.maximum(s * per - 1, 0), o + c))
        n = pl.BlockSpec((None, HALO, W), lambda c, b_, s: (b_, jnp.minimum((s + 1) * per, nS * per - 1), o + c))
        return t, p, n

    dt_, _, dn_ = specs(0)
    bt, _, bn = specs(ob)
    ct, cp, cn = specs(oc)
    ht, hp, hn = specs(oh)
    wspec = pl.BlockSpec((8, W), lambda c, b_, s: (0, c))
    o3 = offs[0] // (3 * D)
    return pl.pallas_call(
        body, name=name, grid=(nW, Bl, nS),
        in_specs=[dt_, dn_, bt, bn, ct, cp, cn, ht, hp, hn, wspec, _ANY],
        out_specs=[pl.BlockSpec((None, ts, 3 * D), lambda c, b_, s: (b_, s, o3)), wspec],
        out_shape=[jax.ShapeDtypeStruct(dproj.shape, dproj.dtype), jax.ShapeDtypeStruct((8, D), F32)],
        input_output_aliases={11: 0},
        compiler_params=_cp(("arbitrary", "arbitrary", "arbitrary")),
    )(ds, ds, proj, proj, proj, proj, proj, proj, proj, proj, w8, dproj)


def _ffn_conv_fwd(up, w8, b, DFF, name, ts=512, W=1408):
    Bl, S, _ = up.shape
    K = 3
    ts, W = _tile(S, ts, 8), _tile(DFF, W)
    nS, nW = S // ts, DFF // W

    def body(g_ref, gp_ref, v_ref, vp_ref, wg_ref, wv_ref, bg_ref, bv_ref, o_ref):
        keep = pl.program_id(2) > 0
        ug = _conv(jnp.concatenate([_masked(gp_ref, keep), g_ref[...]], axis=0), wg_ref, K)[HALO:] + bg_ref[...]
        uv = _conv(jnp.concatenate([_masked(vp_ref, keep), v_ref[...]], axis=0), wv_ref, K)[HALO:] + bv_ref[...]
        o_ref[...] = (ug * _sigmoid(ug) * uv).astype(o_ref.dtype)

    tg, pg, _ = _halo_specs(ts, W, nS, lambda c: c)
    tv, pv, _ = _halo_specs(ts, W, nS, lambda c: nW + c)
    wg = pl.BlockSpec((8, W), lambda b_, c, s: (0, c))
    wv = pl.BlockSpec((8, W), lambda b_, c, s: (0, nW + c))
    bg = pl.BlockSpec((1, W), lambda b_, c, s: (0, c))
    bv = pl.BlockSpec((1, W), lambda b_, c, s: (0, nW + c))
    b2 = b.reshape(1, 2 * DFF)
    return pl.pallas_call(
        body, name=name, grid=(Bl, nW, nS),
        in_specs=[tg, pg, tv, pv, wg, wv, bg, bv],
        out_specs=pl.BlockSpec((None, ts, W), lambda b_, c, s: (b_, s, c)),
        out_shape=jax.ShapeDtypeStruct((Bl, S, DFF), MXU_DTYPE),
        compiler_params=_cp(("parallel", "parallel", "parallel")),
    )(up, up, up, up, w8, w8, b2, b2)


def _ffn_conv_bwd(da, up, w8, b, DFF, name, ts=256):
    Bl, S, _ = up.shape
    K = 3
    ts, W = _tile(S, ts, 8), DFF
    nS = S // ts
    rc = _tile(ts, ROW_CHUNK, 8)

    def body(d_ref, dn_ref, g_ref, gp_ref, gn_ref, v_ref, vp_ref, vn_ref, w_ref, b_ref,
             dup_ref, dw_ref, db_ref, gext_ref, vext_ref, dext_ref):
        bb, s = pl.program_id(0), pl.program_id(1)

        @pl.when((bb == 0) & (s == 0))
        def _():
            dw_ref[...] = jnp.zeros_like(dw_ref)
            db_ref[...] = jnp.zeros_like(db_ref)

        first, last = s > 0, s < nS - 1
        _fill_ext(gext_ref, _masked(gp_ref, first), g_ref[...], _masked(gn_ref, last), ts)
        _fill_ext(vext_ref, _masked(vp_ref, first), v_ref[...], _masked(vn_ref, last), ts)
        _fill_ext(dext_ref, jnp.zeros((HALO, W), F32), d_ref[...], _masked(dn_ref, last), ts)
        for l0, lc in _lane_chunks(W):
            gl, vl = slice(l0, l0 + lc), slice(DFF + l0, DFF + l0 + lc)
            wg_, wv_, bg_, bv_ = w_ref[:, gl], w_ref[:, vl], b_ref[:, gl], b_ref[:, vl]

            def chunk(i, acc):
                r0 = pl.multiple_of(i * rc, 8)
                win = pl.ds(r0, rc + 2 * HALO)
                gw_, vw_, dw_ = gext_ref[win, gl], vext_ref[win, gl], dext_ref[win, gl]
                ug = _conv(gw_, wg_, K) + bg_
                uv = _conv(vw_, wv_, K) + bv_
                sg = _sigmoid(ug)
                dug = dw_ * uv * (sg * (1.0 + ug * (1.0 - sg)))
                duv = dw_ * (ug * sg)
                dg, gg = _conv_t(dug, gw_, wg_, K, rc)
                dv, gv = _conv_t(duv, vw_, wv_, K, rc)
                dup_ref[pl.ds(r0, rc), gl] = dg.astype(dup_ref.dtype)
                dup_ref[pl.ds(r0, rc), vl] = dv.astype(dup_ref.dtype)
                new = gg + gv + [jnp.sum(dug[HALO:HALO + rc], axis=0, keepdims=True),
                                 jnp.sum(duv[HALO:HALO + rc], axis=0, keepdims=True)]
                return tuple(a + g for a, g in zip(acc, new))

            acc = lax.fori_loop(0, ts // rc, chunk, tuple(jnp.zeros((1, lc), F32) for _ in range(2 * K + 2)))
            for k in range(K):
                dw_ref[k:k + 1, gl] += acc[k]
                dw_ref[k:k + 1, vl] += acc[K + k]
            db_ref[0:1, gl] += acc[2 * K]
            db_ref[0:1, vl] += acc[2 * K + 1]

    per = ts // HALO

    def specs(blk):
        t = pl.BlockSpec((None, ts, W), lambda b_, s: (b_, s, blk))
        p = pl.BlockSpec((None, HALO, W), lambda b_, s: (b_, jnp.maximum(s * per - 1, 0), blk))
        n = pl.BlockSpec((None, HALO, W), lambda b_, s: (b_, jnp.minimum((s + 1) * per, nS * per - 1), blk))
        return t, p, n

    dt_, _, dn_ = specs(0)
    gt, gp, gn = specs(0)
    vt, vp, vn = specs(1)
    small = pl.BlockSpec((8, 2 * DFF), lambda b_, s: (0, 0))
    return pl.pallas_call(
        body, name=name, grid=(Bl, nS),
        in_specs=[dt_, dn_, gt, gp, gn, vt, vp, vn, small, pl.BlockSpec((1, 2 * DFF), lambda b_, s: (0, 0))],
        out_specs=[pl.BlockSpec((None, ts, 2 * DFF), lambda b_, s: (b_, s, 0)), small, small],
        out_shape=[jax.ShapeDtypeStruct((Bl, S, 2 * DFF), MXU_DTYPE), jax.ShapeDtypeStruct((8, 2 * DFF), F32),
                   jax.ShapeDtypeStruct((8, 2 * DFF), F32)],
        scratch_shapes=[pltpu.VMEM((ts + 2 * HALO, W), F32)] * 3,
        compiler_params=_cp(("arbitrary", "arbitrary")),
    )(da, da, up, up, up, up, up, up, w8, b.reshape(1, 2 * DFF))


def _merge_fwd(proj, y_ssd, y_sc, off, D, name, ts=512):
    Bl, S, _ = proj.shape
    ts = _tile(S, ts, 8)
    og = off // D

    def body(g1_ref, g2_ref, a_ref, b_ref, o_ref):
        o_ref[...] = (_sigmoid(g1_ref[...]) * a_ref[...] + _sigmoid(g2_ref[...]) * b_ref[...]).astype(o_ref.dtype)

    tok = pl.BlockSpec((None, ts, D), lambda b, s: (b, s, 0))
    return pl.pallas_call(
        body, name=name, grid=(Bl, S // ts),
        in_specs=[pl.BlockSpec((None, ts, D), lambda b, s: (b, s, og)),
                  pl.BlockSpec((None, ts, D), lambda b, s: (b, s, og + 1)), tok, tok],
        out_specs=tok, out_shape=jax.ShapeDtypeStruct((Bl, S, D), MXU_DTYPE),
        compiler_params=_cp(("parallel", "parallel")),
    )(proj, proj, y_ssd, y_sc)


def _merge_bwd(dmixin, proj, y_ssd, y_sc, off, D, dproj0, name, ts=512):
    Bl, S, NP = proj.shape
    ts = _tile(S, ts, 8)
    og = off // D
    assert off % (2 * D) == 0

    def body(d_ref, g1_ref, g2_ref, a_ref, b_ref, *rest):
        da_ref, db_ref, dg_ref = rest[-3:]
        dg1_ref, dg2_ref = dg_ref.at[:, 0:D], dg_ref.at[:, D:2 * D]
        d = d_ref[...]
        s1, s2 = _sigmoid(g1_ref[...]), _sigmoid(g2_ref[...])
        da_ref[...] = (d * s1).astype(da_ref.dtype)
        db_ref[...] = (d * s2).astype(db_ref.dtype)
        dg1_ref[...] = (d * a_ref[...] * (s1 * (1.0 - s1))).astype(dg1_ref.dtype)
        dg2_ref[...] = (d * b_ref[...] * (s2 * (1.0 - s2))).astype(dg2_ref.dtype)

    tok = pl.BlockSpec((None, ts, D), lambda b, s: (b, s, 0))
    act = jax.ShapeDtypeStruct((Bl, S, D), MXU_DTYPE)
    carry = dproj0 is not None
    return pl.pallas_call(
        body, name=name, grid=(Bl, S // ts),
        in_specs=[tok, pl.BlockSpec((None, ts, D), lambda b, s: (b, s, og)),
                  pl.BlockSpec((None, ts, D), lambda b, s: (b, s, og + 1)), tok, tok] + ([_ANY] if carry else []),
        out_specs=[tok, tok, pl.BlockSpec((None, ts, 2 * D), lambda b, s: (b, s, og // 2))],
        out_shape=[act, act, jax.ShapeDtypeStruct((Bl, S, NP), MXU_DTYPE)],
        input_output_aliases={5: 2} if carry else {},
        compiler_params=_cp(("parallel", "parallel")),
    )(*((dmixin, proj, proj, y_ssd, y_sc) + ((dproj0,) if carry else ())))


def _exact_dot(a, b, dn, value):
    v = a if value == 0 else b
    m01 = (b if value == 0 else a).astype(MXU_DTYPE)
    hi = v.astype(MXU_DTYPE)
    r1 = v - hi.astype(F32)
    mid = r1.astype(MXU_DTYPE)
    lo = (r1 - mid.astype(F32)).astype(MXU_DTYPE)
    terms = [(t, m01) if value == 0 else (m01, t) for t in (hi, mid, lo)]
    return _dg(*terms[0], dn) + _dg(*terms[1], dn) + _dg(*terms[2], dn)


def _two_term_dot(v, m01):
    hi = v.astype(MXU_DTYPE)
    lo = (v - hi.astype(F32)).astype(MXU_DTYPE)
    return _dg(hi, m01, NN) + _dg(lo, m01, NN)


def _head_to_channels(R):
    rp = R * HEAD_DIM
    return (lax.shift_right_logical(lax.broadcasted_iota(jnp.int32, (LANES, rp), 1), 6)
            == lax.broadcasted_iota(jnp.int32, (LANES, rp), 0)).astype(MXU_DTYPE)


def _expand_heads(mats, R):
    L = mats[0].shape[0]
    out = _two_term_dot(jnp.concatenate(mats, axis=0), _head_to_channels(R))
    return [out[i * L:(i + 1) * L, :] for i in range(len(mats))]


def _expand_cols(mat, R, lane):
    half = lane < HEAD_DIM
    return jnp.concatenate(
        [jnp.where(half, mat[:, 2 * q:2 * q + 1], mat[:, 2 * q + 1:2 * q + 2]) for q in range(R // 2)], axis=1)


def _row_sums(v):
    return _two_term_dot(v, jnp.ones((v.shape[1], LANES), MXU_DTYPE))


def _lanes_to(v, width):
    return jnp.concatenate([v] * (width // LANES), axis=1)


def _head_rows(colvec, R, N):
    return jnp.concatenate([jnp.broadcast_to(colvec[r:r + 1, :], (HEAD_DIM, N)) for r in range(R)], axis=0)


def _ssd_prep(proj, dt_bias, a_log, selcat, selbig, dm, name):
    Bl, S, L, G = dm.Bl, dm.S, CHUNK, N_GROUPS
    nc = S // L
    odt = dm.ODT // LANES

    def body(dtr_ref, dtb_ref, alog_ref, selcat_ref, selbig_ref, dtg_ref, acs_ref, acst_ref):
        row = lax.broadcasted_iota(jnp.int32, (L, L), 0)
        col = lax.broadcasted_iota(jnp.int32, (L, L), 1)
        dt_all = _softplus(dtr_ref[...] + dtb_ref[...])
        acs_all = _exact_dot((row >= col).astype(F32), dt_all * (-jnp.exp(alog_ref[...])), NN, 1)
        dtg_ref[...] = _exact_dot(dt_all, selcat_ref[...], NN, 0)
        acs_ref[...] = _exact_dot(acs_all, selcat_ref[...], NN, 0)
        acst_ref[...] = _exact_dot(selbig_ref[...], acs_all, NT, 1)

    vec = pl.BlockSpec((1, LANES), lambda b, c: (0, 0))
    wide = pl.BlockSpec((None, L, G * LANES), lambda b, c: (b, c, 0))
    return pl.pallas_call(
        body, name=name, grid=(Bl, nc),
        in_specs=[pl.BlockSpec((None, L, LANES), lambda b, c: (b, c, odt)), vec, vec,
                  pl.BlockSpec((LANES, G * LANES), lambda b, c: (0, 0)),
                  pl.BlockSpec((G * 8, LANES), lambda b, c: (0, 0))],
        out_specs=[wide, wide, pl.BlockSpec((None, None, G * 8, L), lambda b, c: (b, c, 0, 0))],
        out_shape=[jax.ShapeDtypeStruct((Bl, S, G * LANES), F32), jax.ShapeDtypeStruct((Bl, S, G * LANES), F32),
                   jax.ShapeDtypeStruct((Bl, nc, G * 8, L), F32)],
        compiler_params=_cp(("parallel", "parallel")),
    )(proj, dt_bias, a_log, selcat, selbig)


def _ssd_post(d_a, ddt, dtg, proj, dt_bias, a_log, selcat, dproj, dm, name):
    Bl, S, L, G = dm.Bl, dm.S, CHUNK, N_GROUPS
    nc = S // L
    odt = dm.ODT // LANES

    def body(da_ref, ddt_ref, dtg_ref, dtr_ref, dtb_ref, alog_ref, selcat_ref, _, ddtr_ref, dpar_ref):
        @pl.when((pl.program_id(0) == 0) & (pl.program_id(1) == 0))
        def _():
            dpar_ref[...] = jnp.zeros_like(dpar_ref)

        row = lax.broadcasted_iota(jnp.int32, (L, L), 0)
        col = lax.broadcasted_iota(jnp.int32, (L, L), 1)
        selcat = selcat_ref[...]
        a_all = -jnp.exp(alog_ref[...])
        a4 = _exact_dot(jnp.broadcast_to(a_all, (8, LANES)), selcat, NN, 0)[0:1, :]
        dadt = _exact_dot((col >= row).astype(F32), da_ref[...], NN, 1)
        ddt4 = ddt_ref[...] + dadt * a4
        da4 = jnp.sum(dadt * dtg_ref[...], axis=0, keepdims=True)
        ddt_all = _exact_dot(ddt4, selcat, NT, 0)
        da_all = _exact_dot(jnp.broadcast_to(da4, (8, G * LANES)), selcat, NT, 0)[0:1, :]
        ddtr = ddt_all * _sigmoid(dtr_ref[...] + dtb_ref[...])
        ddtr_ref[...] = ddtr.astype(ddtr_ref.dtype)
        dpar_ref[0:1, :] += jnp.sum(ddtr, axis=0, keepdims=True)
        dpar_ref[1:2, :] += da_all * a_all

    vec = pl.BlockSpec((1, LANES), lambda b, c: (0, 0))
    wide = pl.BlockSpec((None, L, G * LANES), lambda b, c: (b, c, 0))
    return pl.pallas_call(
        body, name=name, grid=(Bl, nc),
        in_specs=[wide, wide, wide, pl.BlockSpec((None, L, LANES), lambda b, c: (b, c, odt)), vec, vec,
                  pl.BlockSpec((LANES, G * LANES), lambda b, c: (0, 0)), _ANY],
        out_specs=[pl.BlockSpec((None, L, LANES), lambda b, c: (b, c, odt)), pl.BlockSpec((8, LANES), lambda b, c: (0, 0))],
        out_shape=[jax.ShapeDtypeStruct(dproj.shape, dproj.dtype), jax.ShapeDtypeStruct((8, LANES), F32)],
        input_output_aliases={7: 0},
        compiler_params=_cp(("arbitrary", "arbitrary")),
    )(d_a, ddt, dtg, proj, dt_bias, a_log, selcat, dproj)


def _scan_fwd(xbc_a, proj, dtg4, acs4, acs_t4, dsk_exp, norm_g, dm, name):
    Bl, S, DI, N, R, L, G = dm.Bl, dm.S, dm.DI, dm.N, dm.R, CHUNK, N_GROUPS
    RP = R * HEAD_DIM
    nc = S // L

    def body(x_ref, z_ref, dtg_ref, acs_ref, acst_ref, dsk_ref, ng_ref, y_ref, yn_ref, hp_ref, h_ref):
        xs_ref, bm_ref, cm_ref = x_ref.at[:, 0:RP], x_ref.at[:, RP:RP + N], x_ref.at[:, RP + N:RP + 2 * N]
        @pl.when(pl.program_id(2) == 0)
        def _():
            h_ref[...] = jnp.zeros_like(h_ref)

        causal = lax.broadcasted_iota(jnp.int32, (L, L), 0) >= lax.broadcasted_iota(jnp.int32, (L, L), 1)
        lane = lax.broadcasted_iota(jnp.int32, (L, LANES), 1)
        dtg, acs, acs_t = dtg_ref[...], acs_ref[...], acst_ref[...]
        xs = xs_ref[...]
        bmb, cmb = bm_ref[...].astype(MXU_DTYPE), cm_ref[...].astype(MXU_DTYPE)
        sg = _dg(cmb, bmb, NT)
        acs_last = acs[L - 1:L, :]
        dt_exp, ea_exp, ds_exp = _expand_heads([dtg, jnp.exp(acs), jnp.exp(acs_last - acs)], R)
        xdt = xs * dt_exp
        xb = xdt.astype(MXU_DTYPE)
        parts = []
        for q in range(R // 2):
            x2 = xb[:, LANES * q:LANES * (q + 1)]
            ys = []
            for r in (2 * q, 2 * q + 1):
                dec = jnp.exp(jnp.where(causal, acs[:, r:r + 1] - acs_t[r:r + 1, :], -1e30))
                ys.append(_dg((sg * dec).astype(MXU_DTYPE), x2, NN))
            parts.append(jnp.where(lane < HEAD_DIM, ys[0], ys[1]))
        ydiag = jnp.concatenate(parts, axis=1)
        h_cur = h_ref[...]
        hb = h_cur.astype(MXU_DTYPE)
        yoff = _dg(cmb, hb, NT) * ea_exp
        st = _dg((xdt * ds_exp).astype(MXU_DTYPE), bmb, TN)
        hp_ref[...] = hb
        h_ref[...] = h_cur * _head_rows(jnp.exp(acs_t[:, L - 1:L]), R, N) + st
        y = ydiag + yoff + dsk_ref[...] * xs
        y_ref[...] = y
        z = z_ref[...]
        yg = y * (z * _sigmoid(z))
        rr = lax.rsqrt(_row_sums(yg * yg) * (1.0 / RP) + EPS)
        yn_ref[...] = (yg * _lanes_to(rr, RP) * ng_ref[...]).astype(yn_ref.dtype)

    oz = dm.OZ // RP
    grp = pl.BlockSpec((None, L, RP), lambda b, g, c: (b, c, g))
    lanes = pl.BlockSpec((None, L, LANES), lambda b, g, c: (b, c, g))
    chan = pl.BlockSpec((1, RP), lambda b, g, c: (0, g))
    return pl.pallas_call(
        body, name=name, grid=(Bl, G, nc),
        in_specs=[pl.BlockSpec((None, L, RP + 2 * N), lambda b, g, c: (b, c, g)),
                  pl.BlockSpec((None, L, RP), lambda b, g, c: (b, c, oz + g)), lanes, lanes,
                  pl.BlockSpec((None, None, 8, L), lambda b, g, c: (b, c, g, 0)),
                  chan, chan],
        out_specs=[grp, grp, pl.BlockSpec((None, None, None, RP, N), lambda b, g, c: (b, g, c, 0, 0))],
        out_shape=[jax.ShapeDtypeStruct((Bl, S, DI), F32), jax.ShapeDtypeStruct((Bl, S, DI), MXU_DTYPE),
                   jax.ShapeDtypeStruct((Bl, G, nc, RP, N), MXU_DTYPE)],
        scratch_shapes=[pltpu.VMEM((RP, N), F32)],
        compiler_params=_cp(("parallel", "parallel", "arbitrary")),
    )(xbc_a, proj, dtg4, acs4, acs_t4, dsk_exp, norm_g.reshape(1, DI))


def _scan_bwd(dyn, y, xbc_a, proj, hprev, dtg4, acs4, acs_t4, dsk_exp, norm_g, dproj, dm, name):
    Bl, S, DI, N, R, L, G = dm.Bl, dm.S, dm.DI, dm.N, dm.R, CHUNK, N_GROUPS
    RP = R * HEAD_DIM
    nc = S // L

    def body(dyn_ref, y_ref, z_ref, x_ref, hp_ref, dtg_ref, acs_ref, acst_ref, dsk_ref, ng_ref,
             _, dz_ref, dx_ref, da_ref, ddt_ref, ddsk_ref, dng_ref, dh_ref):
        xs_ref, bm_ref, cm_ref = x_ref.at[:, 0:RP], x_ref.at[:, RP:RP + N], x_ref.at[:, RP + N:RP + 2 * N]
        dxs_ref, dbm_ref, dcm_ref = dx_ref.at[:, 0:RP], dx_ref.at[:, RP:RP + N], dx_ref.at[:, RP + N:RP + 2 * N]
        b, c = pl.program_id(1), pl.program_id(2)

        @pl.when(c == 0)
        def _():
            dh_ref[...] = jnp.zeros_like(dh_ref)

        @pl.when((b == 0) & (c == 0))
        def _():
            ddsk_ref[...] = jnp.zeros_like(ddsk_ref)
            dng_ref[...] = jnp.zeros_like(dng_ref)

        row = lax.broadcasted_iota(jnp.int32, (L, L), 0)
        col = lax.broadcasted_iota(jnp.int32, (L, L), 1)
        causal, anti = row >= col, col >= row
        lane = lax.broadcasted_iota(jnp.int32, (L, LANES), 1)
        etb = (lax.shift_right_logical(lax.broadcasted_iota(jnp.int32, (RP, LANES), 0), 6)
               == lax.broadcasted_iota(jnp.int32, (RP, LANES), 1)).astype(MXU_DTYPE)

        dtg, acs, acs_t = dtg_ref[...], acs_ref[...], acst_ref[...]
        xs, z, y, dyn = xs_ref[...], z_ref[...], y_ref[...], dyn_ref[...]
        bmb, cmb = bm_ref[...].astype(MXU_DTYPE), cm_ref[...].astype(MXU_DTYPE)
        hpb = hp_ref[...]
        ng = ng_ref[...]

        sz = _sigmoid(z)
        siluz = z * sz
        yg = y * siluz
        rr = lax.rsqrt(jnp.mean(yg * yg, axis=-1, keepdims=True) + EPS)
        yhat = yg * rr
        dng_ref[...] += jnp.sum(dyn * yhat, axis=0, keepdims=True)
        dyhat = dyn * ng
        dyg = rr * (dyhat - yhat * jnp.mean(dyhat * yhat, axis=-1, keepdims=True))
        dy = dyg * siluz
        dz_ref[...] = (dyg * y * (sz * (1.0 + z * (1.0 - sz)))).astype(dz_ref.dtype)

        dxs = dy * dsk_ref[...]
        ddsk_ref[...] += jnp.sum(dy * xs, axis=0, keepdims=True)

        acs_last = acs[L - 1:L, :]
        dt_exp = _expand_cols(dtg, R, lane)
        ea_exp = _expand_cols(jnp.exp(acs), R, lane)
        ds_exp = _expand_cols(jnp.exp(acs_last - acs), R, lane)
        xdt = xs * dt_exp
        xb = xdt.astype(MXU_DTYPE)
        dyb = dy.astype(MXU_DTYPE)
        cd = jnp.exp(acs_last)
        cd_rows = _head_rows(jnp.exp(acs_t[:, L - 1:L]), R, N)

        q_ = _dg(cmb, hpb, NT)
        dq = dy * ea_exp
        dqb = dq.astype(MXU_DTYPE)
        dcm = _dg(dqb, hpb, NN)
        dh_yoff = _dg(dqb, cmb, TN)

        dhn = dh_ref[...]
        wprod = dhn * hpb.astype(F32)
        per_head = jnp.concatenate(
            [jnp.sum(wprod[HEAD_DIM * r:HEAD_DIM * (r + 1), :], axis=0, keepdims=True) for r in range(R)]
            + ([jnp.zeros((8 - R, N), F32)] if R < 8 else []), axis=0)
        dcd_col = jnp.sum(per_head, axis=1, keepdims=True)
        diag8 = lax.broadcasted_iota(jnp.int32, (8, LANES), 0) == lax.broadcasted_iota(jnp.int32, (8, LANES), 1)
        dcd_lane = jnp.sum(jnp.where(diag8, dcd_col, 0.0), axis=0, keepdims=True)
        d_a_last = dcd_lane * cd
        dh_ref[...] = dhn * cd_rows + dh_yoff
        dhnb = dhn.astype(MXU_DTYPE)

        e_ = _dg(bmb, dhnb, NT)
        dxdt = ds_exp * e_
        xds = xdt * ds_exp
        dbm = _dg(xds.astype(MXU_DTYPE), dhnb, NN)

        sg = _dg(cmb, bmb, NT)
        sg_t = _dg(bmb, cmb, NT)
        dsg = jnp.zeros((L, L), F32)
        dsg_t = jnp.zeros((L, L), F32)
        d_a = jnp.zeros((L, LANES), F32)
        parts = []
        for q in range(R // 2):
            x2 = xb[:, LANES * q:LANES * (q + 1)]
            dy2 = dyb[:, LANES * q:LANES * (q + 1)]
            dxs2 = []
            for hh, r in enumerate((2 * q, 2 * q + 1)):
                mine = (lane < HEAD_DIM) if hh == 0 else (lane >= HEAD_DIM)
                diff = acs[:, r:r + 1] - acs_t[r:r + 1, :]
                dec = jnp.exp(jnp.where(causal, diff, -1e30))
                dec_t = jnp.exp(jnp.where(anti, -diff, -1e30))
                dy2m = jnp.where(mine, dy2, jnp.zeros_like(dy2))
                dm_ = _dg(dy2m, x2, NT)
                dm_t = _dg(x2, dy2m, NT)
                m_t = sg_t * dec_t
                da_col = jnp.sum(dm_ * (sg * dec) - dm_t * m_t, axis=1, keepdims=True)
                d_a = d_a + jnp.where(lane == r, da_col, 0.0)
                dsg = dsg + dm_ * dec
                dsg_t = dsg_t + dm_t * dec_t
                dxs2.append(_dg(m_t.astype(MXU_DTYPE), dy2, NN))
            parts.append(jnp.where(lane < HEAD_DIM, dxs2[0], dxs2[1]))
        dxdt = dxdt + jnp.concatenate(parts, axis=1)
        dcm_ref[...] = dcm + _dg(dsg.astype(MXU_DTYPE), bmb, NN)
        dbm_ref[...] = dbm + _dg(dsg_t.astype(MXU_DTYPE), cmb, NN)
        dxs_ref[...] = dxs + dxdt * dt_exp

        hs = _two_term_dot(jnp.concatenate([dq * q_ - xds * e_, xds * e_, dxdt * xs], axis=0), etb)
        t2 = hs[L:2 * L, :]
        rowl = lax.broadcasted_iota(jnp.int32, (L, LANES), 0)
        d_a_last = d_a_last + jnp.sum(t2, axis=0, keepdims=True)
        da_ref[...] = d_a + hs[0:L, :] + jnp.where(rowl == L - 1, d_a_last, 0.0)
        ddt_ref[...] = hs[2 * L:3 * L, :]

    oz = dm.OZ // RP
    grp = pl.BlockSpec((None, L, RP), lambda g, b, c: (b, nc - 1 - c, g))
    zspec = pl.BlockSpec((None, L, RP), lambda g, b, c: (b, nc - 1 - c, oz + g))
    lanes = pl.BlockSpec((None, L, LANES), lambda g, b, c: (b, nc - 1 - c, g))
    xg = pl.BlockSpec((None, L, RP + 2 * N), lambda g, b, c: (b, nc - 1 - c, g))
    chan = pl.BlockSpec((1, RP), lambda g, b, c: (0, g))
    wide = jax.ShapeDtypeStruct((Bl, S, G * LANES), F32)
    return pl.pallas_call(
        body, name=name, grid=(G, Bl, nc),
        in_specs=[grp, grp, zspec, xg,
                  pl.BlockSpec((None, None, None, RP, N), lambda g, b, c: (b, g, nc - 1 - c, 0, 0)),
                  lanes, lanes, pl.BlockSpec((None, None, 8, L), lambda g, b, c: (b, nc - 1 - c, g, 0)),
                  chan, chan, _ANY],
        out_specs=[zspec, xg, lanes, lanes, chan, chan],
        out_shape=[jax.ShapeDtypeStruct(dproj.shape, dproj.dtype), jax.ShapeDtypeStruct(xbc_a.shape, F32),
                   wide, wide, jax.ShapeDtypeStruct((1, DI), F32), jax.ShapeDtypeStruct((1, DI), F32)],
        input_output_aliases={10: 0},
        scratch_shapes=[pltpu.VMEM((RP, N), F32)],
        compiler_params=_cp(("arbitrary", "arbitrary", "arbitrary")),
    )(dyn, y, proj, xbc_a, hprev, dtg4, acs4, acs_t4, dsk_exp, norm_g.reshape(1, DI), dproj)


def _adam_update(w, m, v, g):
    c1 = 1.0 - ADAM_B1 ** ADAM_STEP
    c2 = 1.0 - ADAM_B2 ** ADAM_STEP
    nm = ADAM_B1 * m + (1.0 - ADAM_B1) * g
    nv = ADAM_B2 * v + (1.0 - ADAM_B2) * (g * g)
    return nm, nv, -ADAM_LR * ((nm / c1) / (jnp.sqrt(nv / c2) + ADAM_EPS) + ADAM_WD * w)


def _adamw(w, m, v, g, name, tr=128):
    rows, cols = w.shape
    tr = _tile(rows, tr, 8)

    def body(w_ref, m_ref, v_ref, ga_ref, g_ref, d_ref, nm_ref, nv_ref):
        g = ga_ref[...]
        g_ref[...] = g
        nm_ref[...], nv_ref[...], d_ref[...] = _adam_update(w_ref[...], m_ref[...], v_ref[...], g)

    blk = pl.BlockSpec((tr, cols), lambda i: (i, 0))
    shp = jax.ShapeDtypeStruct((rows, cols), F32)
    return pl.pallas_call(
        body, name=name, grid=(rows // tr,), in_specs=[blk] * 4, out_specs=[blk] * 4,
        out_shape=[shp] * 4, compiler_params=_cp(("parallel",)),
    )(w, m, v, g)


def _adamw_layers(w, m, v, g_mine, g_theirs, core, name, tr=128):
    _, rows, cols = w.shape
    tr = _tile(rows, tr, 8)

    def body(c_ref, w_ref, m_ref, v_ref, ga_ref, gb_ref, g_ref, d_ref, nm_ref, nv_ref):
        g = jnp.where(pl.program_id(0) == c_ref[0], ga_ref[...], gb_ref[...])
        g_ref[...] = g
        nm_ref[...], nv_ref[...], d_ref[...] = _adam_update(w_ref[...], m_ref[...], v_ref[...], g)

    lay = pl.BlockSpec((None, tr, cols), lambda l, i, c_ref: (l, i, 0))
    one = pl.BlockSpec((tr, cols), lambda l, i, c_ref: (i, 0))
    shp = jax.ShapeDtypeStruct(w.shape, F32)
    return pl.pallas_call(
        body, name=name,
        grid_spec=pltpu.PrefetchScalarGridSpec(num_scalar_prefetch=1, grid=(2, rows // tr),
                                               in_specs=[lay, lay, lay, one, one], out_specs=[lay] * 4),
        out_shape=[shp] * 4, compiler_params=_cp(("parallel", "parallel")),
    )(core, w, m, v, g_mine, g_theirs)


def _sum_slots(buf, name, tr=256):
    n, rows, cols = buf.shape
    tr = _tile(rows, tr, 8)

    def body(b_ref, o_ref):
        acc = b_ref[0].astype(F32)
        for k in range(1, n):
            acc = acc + b_ref[k].astype(F32)
        o_ref[...] = acc

    return pl.pallas_call(
        body, name=name, grid=(rows // tr,),
        in_specs=[pl.BlockSpec((n, tr, cols), lambda i: (0, i, 0))],
        out_specs=pl.BlockSpec((tr, cols), lambda i: (i, 0)),
        out_shape=jax.ShapeDtypeStruct((rows, cols), F32), compiler_params=_cp(("parallel",)),
    )(buf)


_ANY = pl.BlockSpec(memory_space=pl.ANY)


def _exchange_chips(src, per_dest, name):
    rows, cols = src.shape[-2:]

    def body(in_ref, out_ref, send_sems, recv_sems, local_sem):
        x, y, c = lax.axis_index("x"), lax.axis_index("y"), lax.axis_index("c")
        me = 2 * x + y
        chips = [(1 - x, y), (x, 1 - y), (1 - x, 1 - y)]

        def block(j):
            return in_ref.at[j] if per_dest else in_ref

        mine = pltpu.make_async_copy(block(me), out_ref.at[me], local_sem)
        mine.start()
        sends = []
        for k, (px, py) in enumerate(chips):
            cp = pltpu.make_async_remote_copy(
                src_ref=block(2 * px + py), dst_ref=out_ref.at[me], send_sem=send_sems.at[k],
                recv_sem=recv_sems.at[k], device_id=(px, py, c), device_id_type=MESH)
            cp.start()
            sends.append(cp)
        for k, (px, py) in enumerate(chips):
            pltpu.make_async_remote_copy(
                src_ref=block(me), dst_ref=out_ref.at[2 * px + py], send_sem=send_sems.at[k],
                recv_sem=recv_sems.at[k], device_id=(px, py, c), device_id_type=MESH).wait_recv()
        for cp in sends:
            cp.wait_send()
        mine.wait()

    return pl.pallas_call(
        body, name=name, in_specs=[_ANY], out_specs=_ANY,
        out_shape=jax.ShapeDtypeStruct((N_CHIPS, rows, cols), src.dtype),
        scratch_shapes=[pltpu.SemaphoreType.DMA((3,)), pltpu.SemaphoreType.DMA((3,)), pltpu.SemaphoreType.DMA(())],
    )(src)


def _shard_window(ref, kind, j, lead=()):
    if kind == "slots":
        return ref.at[(j,) + lead]
    r, c = ref.shape[-2] // (N_CHIPS if kind == "rows" else 1), ref.shape[-1] // (N_CHIPS if kind == "cols" else 1)
    full = tuple(slice(None) for _ in range(len(ref.shape) - 2 - len(lead)))
    if kind == "rows":
        return ref.at[lead + full + (pl.ds(pl.multiple_of(j * r, 16), r), slice(None))]
    return ref.at[lead + full + (slice(None), pl.ds(pl.multiple_of(j * c, LANES), c))]


def _gather_kind(shard, axis):
    if axis == 1:
        return "rows"
    return "cols" if shard.shape[2] % LANES == 0 else "slots"


def _gather_weights(shards, axes, name):
    kinds = ["rows" if ax == 1 else "slots" for ax in axes]
    nw = len(shards)

    def out_shape(s, kind):
        d, r, c = s.shape
        shp = {"rows": (d, N_CHIPS * r, c), "cols": (d, r, N_CHIPS * c), "slots": (N_CHIPS, d, r, c)}[kind]
        return jax.ShapeDtypeStruct(shp, s.dtype)

    assert all(s.shape[0] == 2 for s in shards)

    def body(*refs):
        ins, outs = refs[:nw], refs[nw:2 * nw]
        ici_send, ici_recv, d2d_send, d2d_recv, local_sems = refs[2 * nw:]
        x, y, c = lax.axis_index("x"), lax.axis_index("y"), lax.axis_index("c")
        me = 2 * x + y
        chips = [(1 - x, y), (x, 1 - y), (1 - x, 1 - y)]
        sends = []
        for i in range(nw):
            if kinds[i] == "rows":
                own = pltpu.make_async_copy(ins[i], _shard_window(outs[i], kinds[i], me), local_sems.at[i])
                own.start()
                sends.append((own, False))
        for i in range(nw):
            for k, (px, py) in enumerate(chips):
                cp = pltpu.make_async_remote_copy(
                    src_ref=ins[i].at[c], dst_ref=_shard_window(outs[i], kinds[i], me, (c,)),
                    send_sem=ici_send.at[3 * i + k], recv_sem=ici_recv.at[3 * i + k],
                    device_id=(px, py, c), device_id_type=MESH)
                cp.start()
                sends.append((cp, True))
        for i in range(nw):
            for k, (px, py) in enumerate(chips):
                win = _shard_window(outs[i], kinds[i], 2 * px + py, (c,))
                pltpu.make_async_remote_copy(
                    src_ref=ins[i].at[c], dst_ref=win, send_sem=ici_send.at[3 * i + k], recv_sem=ici_recv.at[3 * i + k],
                    device_id=(px, py, c), device_id_type=MESH).wait_recv()
                fw = pltpu.make_async_remote_copy(
                    src_ref=win, dst_ref=win, send_sem=d2d_send.at[3 * i + k], recv_sem=d2d_recv.at[3 * i + k],
                    device_id=(x, y, 1 - c), device_id_type=MESH)
                fw.start()
                sends.append((fw, True))
        for i in range(nw):
            for k, (px, py) in enumerate(chips):
                win = _shard_window(outs[i], kinds[i], 2 * px + py, (1 - c,))
                pltpu.make_async_remote_copy(
                    src_ref=win, dst_ref=win, send_sem=d2d_send.at[3 * i + k], recv_sem=d2d_recv.at[3 * i + k],
                    device_id=(x, y, 1 - c), device_id_type=MESH).wait_recv()
        for cp, remote in sends:
            cp.wait_send() if remote else cp.wait()

    outs = pl.pallas_call(
        body, name=name, in_specs=[_ANY] * nw, out_specs=[_ANY] * nw,
        out_shape=[out_shape(s, k) for s, k in zip(shards, kinds)],
        scratch_shapes=[pltpu.SemaphoreType.DMA((3 * nw,)), pltpu.SemaphoreType.DMA((3 * nw,)),
                        pltpu.SemaphoreType.DMA((3 * nw,)), pltpu.SemaphoreType.DMA((3 * nw,)),
                        pltpu.SemaphoreType.DMA((nw,))],
    )(*shards)
    me = 2 * lax.axis_index("x") + lax.axis_index("y")
    return [jnp.concatenate([jnp.where(me == j, s, o[j]) for j in range(N_CHIPS)], axis=2) if k == "slots" else o
            for s, o, k in zip(shards, outs, kinds)]


def _swap_other_layer(gst, name):
    nv = len(gst)

    def body(*refs):
        ins, outs, send_sems, recv_sems = refs[:nv], refs[nv:2 * nv], refs[2 * nv], refs[2 * nv + 1]
        x, y, c = lax.axis_index("x"), lax.axis_index("y"), lax.axis_index("c")
        cps = [pltpu.make_async_remote_copy(src_ref=ins[i].at[1 - c], dst_ref=outs[i], send_sem=send_sems.at[i],
                                            recv_sem=recv_sems.at[i], device_id=(x, y, 1 - c), device_id_type=MESH)
               for i in range(nv)]
        for cp in cps:
            cp.start()
        for cp in cps:
            cp.wait()

    return pl.pallas_call(
        body, name=name, in_specs=[_ANY] * nv, out_specs=[_ANY] * nv,
        out_shape=[jax.ShapeDtypeStruct(v.shape[1:], v.dtype) for v in gst],
        scratch_shapes=[pltpu.SemaphoreType.DMA((nv,)), pltpu.SemaphoreType.DMA((nv,))],
    )(*gst)


def _pair_sum(g, other, core, name, tr=256):
    _, rows, cols = g.shape
    tr = _tile(rows, tr, 16)

    def body(c_ref, g_ref, o_ref, s_ref):
        s_ref[...] = (g_ref[...].astype(F32) + o_ref[...].astype(F32)).astype(s_ref.dtype)

    blk = pl.BlockSpec((tr, cols), lambda i, c_ref: (i, 0))
    return pl.pallas_call(
        body, name=name,
        grid_spec=pltpu.PrefetchScalarGridSpec(
            num_scalar_prefetch=1, grid=(rows // tr,),
            in_specs=[pl.BlockSpec((None, tr, cols), lambda i, c_ref: (c_ref[0], i, 0)), blk], out_specs=blk),
        out_shape=jax.ShapeDtypeStruct((rows, cols), g.dtype), compiler_params=_cp(("parallel",)),
    )(core, g, other)


def _scatter_layer(parts, kinds, name):
    nw = len(parts)

    def shard_shape(g, kind):
        return g.shape[-2] // (N_CHIPS if kind == "rows" else 1), g.shape[-1] // (N_CHIPS if kind == "cols" else 1)

    def body(*refs):
        ins, outs = refs[:nw], refs[nw:2 * nw]
        send_sems, recv_sems, local_sems = refs[2 * nw:]
        x, y, c = lax.axis_index("x"), lax.axis_index("y"), lax.axis_index("c")
        me = 2 * x + y
        chips = [(1 - x, y), (x, 1 - y), (1 - x, 1 - y)]
        sends = []
        for i in range(nw):
            own = pltpu.make_async_copy(_shard_window(ins[i], kinds[i], me), outs[i].at[me], local_sems.at[i])
            own.start()
            sends.append((own, False))
            for k, (px, py) in enumerate(chips):
                cp = pltpu.make_async_remote_copy(
                    src_ref=_shard_window(ins[i], kinds[i], 2 * px + py), dst_ref=outs[i].at[me],
                    send_sem=send_sems.at[3 * i + k], recv_sem=recv_sems.at[3 * i + k],
                    device_id=(px, py, c), device_id_type=MESH)
                cp.start()
                sends.append((cp, True))
        for i in range(nw):
            for k, (px, py) in enumerate(chips):
                pltpu.make_async_remote_copy(
                    src_ref=_shard_window(ins[i], kinds[i], me), dst_ref=outs[i].at[2 * px + py],
                    send_sem=send_sems.at[3 * i + k], recv_sem=recv_sems.at[3 * i + k],
                    device_id=(px, py, c), device_id_type=MESH).wait_recv()
        for cp, remote in sends:
            cp.wait_send() if remote else cp.wait()

    return pl.pallas_call(
        body, name=name, in_specs=[_ANY] * nw, out_specs=[_ANY] * nw,
        out_shape=[jax.ShapeDtypeStruct((N_CHIPS,) + shard_shape(g, k), g.dtype) for g, k in zip(parts, kinds)],
        scratch_shapes=[pltpu.SemaphoreType.DMA((3 * nw,)), pltpu.SemaphoreType.DMA((3 * nw,)),
                        pltpu.SemaphoreType.DMA((nw,))],
    )(*parts)


def _sibling_swap(vs, name):
    nv = len(vs)

    def body(*refs):
        ins, outs, send_sems, recv_sems = refs[:nv], refs[nv:2 * nv], refs[2 * nv], refs[2 * nv + 1]
        x, y, c = lax.axis_index("x"), lax.axis_index("y"), lax.axis_index("c")
        cps = [pltpu.make_async_remote_copy(src_ref=ins[i], dst_ref=outs[i], send_sem=send_sems.at[i],
                                            recv_sem=recv_sems.at[i], device_id=(x, y, 1 - c), device_id_type=MESH)
               for i in range(nv)]
        for cp in cps:
            cp.start()
        for cp in cps:
            cp.wait()

    return pl.pallas_call(
        body, name=name, in_specs=[_ANY] * nv, out_specs=[_ANY] * nv,
        out_shape=[jax.ShapeDtypeStruct(v.shape, v.dtype) for v in vs],
        scratch_shapes=[pltpu.SemaphoreType.DMA((nv,)), pltpu.SemaphoreType.DMA((nv,))],
    )(*vs)


def _allgather_all(v, name):
    rows, cols = v.shape

    def body(in_ref, out_ref, send_sems, recv_sems, local_sem):
        x, y, c = lax.axis_index("x"), lax.axis_index("y"), lax.axis_index("c")
        me = 4 * x + 2 * y + c
        peers = []
        for k in range(1, N_DEV):
            peers.append(((1 - x) if k & 4 else x, (1 - y) if k & 2 else y, (1 - c) if k & 1 else c))
        mine = pltpu.make_async_copy(in_ref, out_ref.at[me], local_sem)
        mine.start()
        sends = []
        for k, peer in enumerate(peers):
            cp = pltpu.make_async_remote_copy(src_ref=in_ref, dst_ref=out_ref.at[me], send_sem=send_sems.at[k],
                                              recv_sem=recv_sems.at[k], device_id=peer, device_id_type=MESH)
            cp.start()
            sends.append(cp)
        for k, (px, py, pc) in enumerate(peers):
            pltpu.make_async_remote_copy(src_ref=in_ref, dst_ref=out_ref.at[4 * px + 2 * py + pc],
                                         send_sem=send_sems.at[k], recv_sem=recv_sems.at[k],
                                         device_id=(px, py, pc), device_id_type=MESH).wait_recv()
        for cp in sends:
            cp.wait_send()
        mine.wait()

    return pl.pallas_call(
        body, name=name, in_specs=[_ANY], out_specs=_ANY,
        out_shape=jax.ShapeDtypeStruct((N_DEV, rows, cols), v.dtype),
        scratch_shapes=[pltpu.SemaphoreType.DMA((N_DEV - 1,)), pltpu.SemaphoreType.DMA((N_DEV - 1,)),
                        pltpu.SemaphoreType.DMA(())],
    )(v)


def _pack(arrs, dtype, width, row_mult):
    flat = jnp.concatenate([a.reshape(-1).astype(dtype) for a in arrs])
    unit = width * row_mult
    total = -(-flat.shape[0] // unit) * unit
    return jnp.pad(flat, (0, total - flat.shape[0])).reshape(-1, width)


def _unpack(buf, shapes):
    flat = buf.reshape(-1)
    out, off = [], 0
    for shp in shapes:
        n = 1
        for d in shp:
            n *= d
        out.append(flat[off:off + n].reshape(shp))
        off += n
    return out


class _Dims:
    pass


def _dims(x, ssd_dt_bias, ssd_norm_g, ssd_conv_b, ffn_conv_b):
    dm = _Dims()
    dm.Bl, dm.S, dm.D = x.shape
    dm.H, dm.DI, dm.CD = ssd_dt_bias.shape[-1], ssd_norm_g.shape[-1], ssd_conv_b.shape[-1]
    dm.N = (dm.CD - dm.DI) // (2 * N_GROUPS)
    dm.R = dm.H // N_GROUPS
    dm.DFF = ffn_conv_b.shape[-1] // 2
    D = dm.D
    dm.OB, dm.OC, dm.OH, dm.OX = 0, D, 2 * D, 3 * D
    dm.OZ = dm.OX + dm.CD
    dm.OG = -(-(dm.OZ + dm.DI) // (2 * D)) * (2 * D)
    dm.GPAD = dm.OG - (dm.OZ + dm.DI)
    dm.ODT = dm.OG + 2 * D
    dm.NP = dm.ODT + LANES
    assert dm.OZ % (dm.R * HEAD_DIM) == 0
    assert dm.DI // dm.H == HEAD_DIM and dm.N == LANES and dm.R % 2 == 0 and dm.S % CHUNK == 0 and dm.H <= LANES
    return dm


def _group_xbc(v, dm):
    rp, n = dm.R * HEAD_DIM, dm.N
    parts = []
    for g in range(N_GROUPS):
        parts += [v[..., g * rp:(g + 1) * rp], v[..., dm.DI + g * n:dm.DI + (g + 1) * n],
                  v[..., dm.DI + (N_GROUPS + g) * n:dm.DI + (N_GROUPS + g + 1) * n]]
    return jnp.concatenate(parts, axis=-1)


def _ungroup_xbc(v, dm):
    rp, n = dm.R * HEAD_DIM, dm.N
    gw = rp + 2 * n
    xs = [v[..., g * gw:g * gw + rp] for g in range(N_GROUPS)]
    bs = [v[..., g * gw + rp:g * gw + rp + n] for g in range(N_GROUPS)]
    cs = [v[..., g * gw + rp + n:(g + 1) * gw] for g in range(N_GROUPS)]
    return jnp.concatenate(xs + bs + cs, axis=-1)


def _permute_w_in(w, dm):
    o, sc = dm.DI + dm.CD, dm.DI + dm.CD + dm.H
    zeros = lambda n: jnp.zeros((w.shape[0], n), w.dtype)
    return jnp.concatenate([w[:, sc:sc + 3 * dm.D], _group_xbc(w[:, dm.DI:o], dm), w[:, :dm.DI], zeros(dm.GPAD),
                            w[:, sc + 3 * dm.D:], w[:, o:o + dm.H], zeros(LANES - dm.H)], axis=1)


def _unpermute_w_in(dw, dm):
    return jnp.concatenate([dw[..., dm.OZ:dm.OZ + dm.DI], _ungroup_xbc(dw[..., dm.OX:dm.OX + dm.CD], dm),
                            dw[..., dm.ODT:dm.ODT + dm.H], dw[..., :3 * dm.D], dw[..., dm.OG:dm.OG + 2 * dm.D]], axis=-1)


def _lane_pad(v):
    return jnp.pad(v.reshape(1, -1).astype(F32), ((0, 0), (0, LANES - v.shape[-1])))


def _pad8(w):
    return jnp.pad(w.astype(F32), ((0, 8 - w.shape[0]), (0, 0)))


def _head_select(dm):
    j = jnp.arange(LANES)[None, :, None]
    r = jnp.arange(LANES)[None, None, :]
    g = jnp.arange(N_GROUPS)[:, None, None]
    sel = ((j == dm.R * g + r) & (r < dm.R)).astype(F32)
    selcat = jnp.transpose(sel, (1, 0, 2)).reshape(LANES, N_GROUPS * LANES)
    selbig = jnp.transpose(sel[:, :, :8], (0, 2, 1)).reshape(N_GROUPS * 8, LANES)
    return selcat, selbig


def _mix_fwd(dm, h, w, sp, sel, tag):
    Bl, S, D = dm.Bl, dm.S, dm.D
    T = Bl * S
    proj = _matmul(h.reshape(T, D), w["w_in_p"], "NN", F32, tag + "_in_proj", tm=2048, tn=1152).reshape(Bl, S, dm.NP)
    xbc_a = _ssd_conv_fwd(proj, sp["ssd_conv_w8"], sp["ssd_conv_b"], dm.OX, dm.CD, tag + "_ssd_conv")
    dtg, acs, acs_t = _ssd_prep(proj, sp["dt_bias"], sp["a_log"], sel[0], sel[1], dm, tag + "_ssd_prep")
    y, yn, hprev = _scan_fwd(xbc_a, proj, dtg, acs, acs_t, sp["dsk_exp"], sp["ssd_norm_g"], dm, tag + "_ssd_scan")
    y_ssd = _matmul(yn.reshape(T, dm.DI), w["w_ssd_out"], "NN", F32, tag + "_ssd_out", tk=2048).reshape(Bl, S, D)
    s = _sc_conv_fwd(proj, sp["sc_conv_w8"], (dm.OB, dm.OC, dm.OH), D, tag + "_sc_conv")
    y_sc = _matmul(s.reshape(T, D), w["w_sc_out"], "NN", F32, tag + "_sc_out", tk=1024).reshape(Bl, S, D)
    mixin = _merge_fwd(proj, y_ssd, y_sc, dm.OG, D, tag + "_merge")
    mix = _matmul(mixin.reshape(T, D), w["w_o"], "NN", F32, tag + "_o", tk=1024).reshape(Bl, S, D)
    return mix, (h, proj, xbc_a, y, yn, hprev, y_ssd, y_sc, s, mixin, dtg, acs, acs_t)


def _mix_bwd(dm, dmix, saved, w, sp, sel, tag, gbig, slab):
    Bl, S, D = dm.Bl, dm.S, dm.D
    T = Bl * S
    h, proj, xbc_a, y, yn, hprev, y_ssd, y_sc, s, mixin, dtg, acs, acs_t = saved
    dmix2 = dmix.reshape(T, D)
    g = {}
    gbig["w_o"] = _matmul(mixin.reshape(T, D), dmix2, "TN", WIRE_DTYPE, tag + "_dw_o", slab=(gbig.get("w_o"),) + slab)
    dmixin = _matmul(dmix2, w["w_o"], "NT", F32, tag + "_d_o", tk=1024).reshape(Bl, S, D)
    dproj0 = jnp.zeros((Bl, S, dm.NP), MXU_DTYPE) if dm.GPAD else None
    dy_ssd, dy_sc, dproj = _merge_bwd(dmixin, proj, y_ssd, y_sc, dm.OG, D, dproj0, tag + "_merge_bwd")
    gbig["w_sc_out"] = _matmul(s.reshape(T, D), dy_sc.reshape(T, D), "TN", WIRE_DTYPE, tag + "_dw_sc_out",
                               slab=(gbig.get("w_sc_out"),) + slab)
    ds = _matmul(dy_sc.reshape(T, D), w["w_sc_out"], "NT", F32, tag + "_d_sc_out", tk=1024).reshape(Bl, S, D)
    dproj, dscw = _sc_conv_bwd(ds, proj, sp["sc_conv_w8"], (dm.OB, dm.OC, dm.OH), D, dproj, tag + "_sc_conv_bwd")
    g["sc_conv_w"] = dscw[:3]
    gbig["w_ssd_out"] = _matmul(yn.reshape(T, dm.DI), dy_ssd.reshape(T, D), "TN", WIRE_DTYPE, tag + "_dw_ssd_out",
                                slab=(gbig.get("w_ssd_out"),) + slab)
    dyn = _matmul(dy_ssd.reshape(T, D), w["w_ssd_out"], "NT", F32, tag + "_d_ssd_out", tk=1024).reshape(Bl, S, dm.DI)
    dproj, dxa, d_a, ddt, ddsk, dng = _scan_bwd(dyn, y, xbc_a, proj, hprev, dtg, acs, acs_t, sp["dsk_exp"],
                                                sp["ssd_norm_g"], dproj, dm, tag + "_ssd_scan_bwd")
    dproj, dpar = _ssd_post(d_a, ddt, dtg, proj, sp["dt_bias"], sp["a_log"], sel[0], dproj, dm, tag + "_ssd_post")
    g["ssd_dt_bias"], g["ssd_a_log"] = dpar[0, :dm.H], dpar[1, :dm.H]
    g["ssd_d"] = jnp.sum(ddsk.reshape(dm.H, HEAD_DIM), axis=-1)
    g["ssd_norm_g"] = dng[0]
    dproj, dcw, dcb = _ssd_conv_bwd(dxa, proj, sp["ssd_conv_w8"], sp["ssd_conv_b"], dm.OX, dm.CD, dproj,
                                    tag + "_ssd_conv_bwd")
    g["ssd_conv_w"], g["ssd_conv_b"] = _ungroup_xbc(dcw[:4], dm), _ungroup_xbc(dcb[0], dm)
    dproj = dproj.reshape(T, dm.NP)
    gbig["w_in_p"] = _matmul(h.reshape(T, D), dproj, "TN", WIRE_DTYPE, tag + "_dw_in", tn=1728,
                             slab=(gbig.get("w_in_p"),) + slab)
    dh = _matmul(dproj, w["w_in_p"], "NT", F32, tag + "_d_in", tm=2048, tk=1152).reshape(Bl, S, D)
    return dh, g


def _ffn_fwd(dm, h, w, sp, tag):
    Bl, S, D = dm.Bl, dm.S, dm.D
    T = Bl * S
    up = _matmul(h.reshape(T, D), w["w_up"], "NN", F32, tag + "_up", tm=2048, tn=1408).reshape(Bl, S, 2 * dm.DFF)
    a = _ffn_conv_fwd(up, sp["ffn_conv_w8"], sp["ffn_conv_b"], dm.DFF, tag + "_ffn_conv")
    f = _matmul(a.reshape(T, dm.DFF), w["w_down"], "NN", F32, tag + "_down", tk=2816).reshape(Bl, S, D)
    return f, (h, up, a)


def _ffn_bwd(dm, df, saved, w, sp, tag, gbig, slab):
    Bl, S, D = dm.Bl, dm.S, dm.D
    T = Bl * S
    h, up, a = saved
    df2 = df.reshape(T, D)
    g = {}
    gbig["w_down"] = _matmul(a.reshape(T, dm.DFF), df2, "TN", WIRE_DTYPE, tag + "_dw_down", tm=1408,
                             slab=(gbig.get("w_down"),) + slab)
    da = _matmul(df2, w["w_down"], "NT", F32, tag + "_d_down", tn=1408).reshape(Bl, S, dm.DFF)
    dup, dcw, dcb = _ffn_conv_bwd(da, up, sp["ffn_conv_w8"], sp["ffn_conv_b"], dm.DFF, tag + "_ffn_conv_bwd")
    g["ffn_conv_w"], g["ffn_conv_b"] = dcw[:3], dcb[0]
    dup = dup.reshape(T, 2 * dm.DFF)
    gbig["w_up"] = _matmul(h.reshape(T, D), dup, "TN", WIRE_DTYPE, tag + "_dw_up", tn=1408,
                           slab=(gbig.get("w_up"),) + slab)
    dh = _matmul(dup, w["w_up"], "NT", F32, tag + "_d_up", tm=2048, tk=1408).reshape(Bl, S, D)
    return dh, g


def _local_step(dm, x, c, target, wfull, small):
    Bl, S, D = dm.Bl, dm.S, dm.D
    depth = len(wfull)
    sel = _head_select(dm)
    c16 = jnp.pad(c.astype(F32), ((0, 16 - Bl), (0, 0)))
    sps, mods, acts = [], [], []
    for l in range(depth):
        sm = small[l]
        sps.append(dict(
            ssd_conv_w8=_pad8(_group_xbc(sm["ssd_conv_w"], dm)), ssd_conv_b=_group_xbc(sm["ssd_conv_b"], dm),
            dt_bias=_lane_pad(sm["ssd_dt_bias"]),
            a_log=_lane_pad(sm["ssd_a_log"]), dsk_exp=jnp.repeat(sm["ssd_d"].astype(F32), HEAD_DIM).reshape(1, dm.DI),
            ssd_norm_g=sm["ssd_norm_g"],
            sc_conv_w8=_pad8(sm["sc_conv_w"]), ffn_conv_w8=_pad8(sm["ffn_conv_w"]), ffn_conv_b=sm["ffn_conv_b"]))
        act, mod = _ada_fwd(c16, wfull[l]["ada_w"], sm["ada_b"], f"l{l}_ada")
        acts.append(act)
        mods.append(jnp.pad(mod[:Bl].reshape(Bl, 6, D), ((0, 0), (0, 2), (0, 0))))

    def sub(i):
        l, ffn = i // 2, i % 2
        sm = small[l]
        return dict(l=l, ffn=ffn, pre_g=sm["ffn_pre_g" if ffn else "mix_pre_g"],
                    post_g=sm["ffn_post_g" if ffn else "mix_post_g"], mod=mods[l], row=3 * ffn,
                    tag=f"l{l}_{'ffn' if ffn else 'mix'}")

    nsub = 2 * depth
    subs = [sub(i) for i in range(nsub)]
    xs, fs, saves = [x], [], []
    h = _norm_mod_fwd(x, subs[0]["pre_g"], subs[0]["mod"], subs[0]["row"], "l0_mix_pre_norm")
    for i, sb in enumerate(subs):
        l = sb["l"]
        if sb["ffn"]:
            f, sv = _ffn_fwd(dm, h, wfull[l], sps[l], sb["tag"])
        else:
            f, sv = _mix_fwd(dm, h, wfull[l], sps[l], sel, sb["tag"])
        nxt = None
        if i + 1 < nsub:
            nb = subs[i + 1]
            nxt = (nb["pre_g"], nb["mod"], nb["row"])
        xn, h = _post_norm_fwd(xs[-1], f, sb["post_g"], sb["mod"], sb["row"] + 2, nxt, sb["tag"] + "_post_norm")
        xs.append(xn)
        fs.append(f)
        saves.append(sv)

    dy, loss = _loss_fwd_bwd(xs[-1], target, "loss")

    grads = [dict() for _ in range(depth)]
    gbig = {}
    dmod = [[None] * 6 for _ in range(depth)]
    dx, dh = dy, None
    for i in reversed(range(nsub)):
        sb = subs[i]
        l = sb["l"]
        nxt = None
        if i + 1 < nsub:
            nb = subs[i + 1]
            nxt = (dh, xs[i + 1], nb["pre_g"], nb["mod"], nb["row"])
        dx, df, pb, shg = _norm_bwd(dx, nxt, (fs[i], sb["post_g"], sb["mod"], sb["row"] + 2), sb["tag"] + "_post_norm_bwd")
        if nxt is not None:
            nb = subs[i + 1]
            dmod[nb["l"]][nb["row"]], dmod[nb["l"]][nb["row"] + 1] = pb[:, 0], pb[:, 1]
            grads[nb["l"]]["ffn_pre_g" if nb["ffn"] else "mix_pre_g"] = shg[0]
        dmod[l][sb["row"] + 2] = pb[:, 2]
        grads[l]["ffn_post_g" if sb["ffn"] else "mix_post_g"] = shg[1]
        if sb["ffn"]:
            dh, g = _ffn_bwd(dm, df, saves[i], wfull[l], sps[l], sb["tag"], gbig, (l, depth))
        else:
            dh, g = _mix_bwd(dm, df, saves[i], wfull[l], sps[l], sel, sb["tag"], gbig, (l, depth))
        grads[l].update(g)
    sb = subs[0]
    grad_x, _, pb, shg = _norm_bwd(dx, (dh, xs[0], sb["pre_g"], sb["mod"], sb["row"]), None, "l0_mix_pre_norm_bwd")
    dmod[0][0], dmod[0][1] = pb[:, 0], pb[:, 1]
    grads[0]["mix_pre_g"] = shg[0]

    for l in range(depth):
        dm6 = jnp.concatenate(dmod[l], axis=-1)
        grads[l]["ada_b"] = jnp.sum(dm6, axis=0)
        dm16 = jnp.pad(dm6, ((0, 16 - Bl), (0, 0))).astype(MXU_DTYPE)
        gbig["ada_w"] = _matmul(acts[l], dm16, "TN", WIRE_DTYPE, f"l{l}_dw_ada", slab=(gbig.get("ada_w"), l, depth))
    return loss, grad_x, grads, gbig


_WEIGHTS = ("ada_w", "ada_b", "mix_pre_g", "mix_post_g", "w_in", "ssd_conv_w", "ssd_conv_b", "ssd_dt_bias",
            "ssd_a_log", "ssd_d", "ssd_norm_g", "w_ssd_out", "sc_conv_w", "w_sc_out", "w_o", "ffn_pre_g",
            "ffn_post_g", "w_up", "ffn_conv_w", "ffn_conv_b", "w_down")
_INPUTS = ("x", "c") + _WEIGHTS + ("loss_target",) + tuple("m_" + n for n in _WEIGHTS) + tuple("v_" + n for n in _WEIGHTS)
_BIG = {"ada_w": 2, "w_in": 2, "w_ssd_out": 1, "w_sc_out": 1, "w_o": 1, "w_up": 2, "w_down": 1}
_CONV = ("ssd_conv_w", "sc_conv_w", "ffn_conv_w")
_SMALL = tuple(n for n in _WEIGHTS if n not in _BIG)


def _step(a):
    x, c, target = a["x"], a["c"], a["loss_target"]
    depth = a["ada_w"].shape[0]
    dm = _dims(x, a["ssd_dt_bias"], a["ssd_norm_g"], a["ssd_conv_b"], a["ffn_conv_b"])
    chip = 2 * lax.axis_index("x") + lax.axis_index("y")

    shards = [a[n].astype(WIRE_DTYPE) for n in _BIG]
    axes = list(_BIG.values())
    kinds = [_gather_kind(s, ax) for s, ax in zip(shards, axes)]
    full = {n: w.astype(MXU_DTYPE) for n, w in zip(_BIG, _gather_weights(shards, axes, "gather_weights"))}
    conv_shapes = [a[n].shape for n in _CONV]
    gotc = _exchange_chips(_pack([a[n] for n in _CONV], F32, LANES, 8), False, "gather_conv_weights")
    piecesc = [_unpack(gotc[j], conv_shapes) for j in range(N_CHIPS)]
    fullc = {n: jnp.concatenate([piecesc[j][i] for j in range(N_CHIPS)], axis=2) for i, n in enumerate(_CONV)}

    wfull, small = [], []
    for l in range(depth):
        wf = {n: full[n][l] for n in _BIG if n != "w_in"}
        wf["w_in_p"] = _permute_w_in(full["w_in"][l], dm)
        wfull.append(wf)
        small.append({n: (fullc[n][l] if n in _CONV else a[n][l]) for n in _SMALL})

    loss_part, grad_x, grads, gbig = _local_step(dm, x, c, target, wfull, small)

    core = lax.axis_index("c").astype(jnp.int32).reshape(1)
    gst = []
    for n, kind in zip(_BIG, kinds):
        g = _unpermute_w_in(gbig["w_in_p"], dm) if n == "w_in" else gbig[n]
        if kind == "slots":
            g = jnp.moveaxis(g.reshape(depth, g.shape[1], N_CHIPS, g.shape[2] // N_CHIPS), 2, 1)
            g = g.reshape(depth, -1, g.shape[-1])
        gst.append(g)
    other = _swap_other_layer(gst, "swap_layer_grads")
    parts = [_pair_sum(g, o, core, "pair_sum_" + n) for g, o, n in zip(gst, other, _BIG)]
    parts = [p.reshape(N_CHIPS, -1, p.shape[-1]) if k == "slots" else p for p, k in zip(parts, kinds)]
    got = _scatter_layer(parts, kinds, "scatter_grads")
    mine = [_sum_slots(g, "sum_chip_grads_" + n) for g, n in zip(got, _BIG)]
    theirs = _sibling_swap(mine, "swap_core_grads")

    out = {}
    for i, n in enumerate(_BIG):
        out[n] = _adamw_layers(a[n], a["m_" + n], a["v_" + n], mine[i], theirs[i], core, "adamw_" + n)

    gsmall = [jnp.stack([grads[l][n] for l in range(depth)]) for n in _SMALL]
    small_shapes = [g.shape for g in gsmall]
    summed = _sum_slots(_allgather_all(_pack(gsmall, F32, LANES, 8), "gather_small_grads"), "sum_small_grads")
    gs = dict(zip(_SMALL, _unpack(summed, small_shapes)))
    for n in _CONV:
        wcols = a[n].shape[2]
        gs[n] = lax.dynamic_slice_in_dim(gs[n], chip * wcols, wcols, axis=2)
    local_shapes = [a[n].shape for n in _SMALL]
    res = _adamw(_pack([a[n] for n in _SMALL], F32, LANES, 8), _pack([a["m_" + n] for n in _SMALL], F32, LANES, 8),
                 _pack([a["v_" + n] for n in _SMALL], F32, LANES, 8), _pack([gs[n] for n in _SMALL], F32, LANES, 8),
                 "adamw_small")
    res = [_unpack(r, local_shapes) for r in res]
    for i, n in enumerate(_SMALL):
        out[n] = [r[i] for r in res]

    loss = lax.psum(loss_part, ("x", "y", "c"))
    return (loss, grad_x) + tuple(out[n][k] for k in range(4) for n in _WEIGHTS)


def kernel(x, c, ada_w, ada_b, mix_pre_g, mix_post_g, w_in, ssd_conv_w, ssd_conv_b, ssd_dt_bias, ssd_a_log, ssd_d, ssd_norm_g, w_ssd_out, sc_conv_w, w_sc_out, w_o, ffn_pre_g, ffn_post_g, w_up, ffn_conv_w, ffn_conv_b, w_down, loss_target, m_ada_w, m_ada_b, m_mix_pre_g, m_mix_post_g, m_w_in, m_ssd_conv_w, m_ssd_conv_b, m_ssd_dt_bias, m_ssd_a_log, m_ssd_d, m_ssd_norm_g, m_w_ssd_out, m_sc_conv_w, m_w_sc_out, m_w_o, m_ffn_pre_g, m_ffn_post_g, m_w_up, m_ffn_conv_w, m_ffn_conv_b, m_w_down, v_ada_w, v_ada_b, v_mix_pre_g, v_mix_post_g, v_w_in, v_ssd_conv_w, v_ssd_conv_b, v_ssd_dt_bias, v_ssd_a_log, v_ssd_d, v_ssd_norm_g, v_w_ssd_out, v_sc_conv_w, v_w_sc_out, v_w_o, v_ffn_pre_g, v_ffn_post_g, v_w_up, v_ffn_conv_w, v_ffn_conv_b, v_w_down):
    return _step(dict(zip(_INPUTS, (
        x, c, ada_w, ada_b, mix_pre_g, mix_post_g, w_in, ssd_conv_w, ssd_conv_b, ssd_dt_bias, ssd_a_log, ssd_d, ssd_norm_g, w_ssd_out, sc_conv_w, w_sc_out, w_o, ffn_pre_g, ffn_post_g, w_up, ffn_conv_w, ffn_conv_b, w_down, loss_target, m_ada_w, m_ada_b, m_mix_pre_g, m_mix_post_g, m_w_in, m_ssd_conv_w, m_ssd_conv_b, m_ssd_dt_bias, m_ssd_a_log, m_ssd_d, m_ssd_norm_g, m_w_ssd_out, m_sc_conv_w, m_w_sc_out, m_w_o, m_ffn_pre_g, m_ffn_post_g, m_w_up, m_ffn_conv_w, m_ffn_conv_b, m_w_down, v_ada_w, v_ada_b, v_mix_pre_g, v_mix_post_g, v_w_in, v_ssd_conv_w, v_ssd_conv_b, v_ssd_dt_bias, v_ssd_a_log, v_ssd_d, v_ssd_norm_g, v_w_ssd_out, v_sc_conv_w, v_w_sc_out, v_w_o, v_ffn_pre_g, v_ffn_post_g, v_w_up, v_ffn_conv_w, v_ffn_conv_b, v_w_down))))
```

```python
import math

import jax
import jax.numpy as jnp
from jax import lax
from jax.experimental import pallas as pl
from jax.experimental.pallas import tpu as pltpu

F32 = jnp.float32
MXU_DTYPE = jnp.bfloat16
WIRE_DTYPE = jnp.bfloat16
EPS = 1e-6
N_GROUPS = 4
CHUNK = 128
HEAD_DIM = 64
LANES = 128
HALO = 8
N_CHIPS = 4
N_DEV = 8
VMEM_LIMIT = 56 * 1024 * 1024
ADAM_LR, ADAM_B1, ADAM_B2, ADAM_EPS, ADAM_WD, ADAM_STEP = 0.001, 0.9, 0.999, 1e-08, 0.01, 10
MESH = pl.DeviceIdType.MESH

NN = (((1,), (0,)), ((), ()))
NT = (((1,), (1,)), ((), ()))
TN = (((0,), (0,)), ((), ()))


def _dg(a, b, dn, precision=None):
    return lax.dot_general(a, b, dn, precision=precision, preferred_element_type=F32)


def _tile(dim, pref, mult=LANES):
    t = (min(pref, dim) // mult) * mult
    while t >= mult:
        if dim % t == 0:
            return t
        t -= mult
    return dim


def _cp(sem):
    return pltpu.CompilerParams(dimension_semantics=sem, vmem_limit_bytes=VMEM_LIMIT)


def _sigmoid(x):
    return 1.0 / (1.0 + jnp.exp(-x))


def _softplus(x):
    return jnp.maximum(x, 0.0) + jnp.log1p(jnp.exp(-jnp.abs(x)))


def _matmul(a, b, mode, out_dtype, name, tm=1024, tn=1024, tk=1024, slab=None):
    if mode == "NN":
        (M, K), N = a.shape, b.shape[1]
    elif mode == "NT":
        (M, K), N = a.shape, b.shape[0]
    else:
        (K, M), N = a.shape, b.shape[1]
    tm, tn, tk = _tile(M, tm), _tile(N, tn), _tile(K, tk)
    nk = K // tk
    dn = {"NN": NN, "NT": NT, "TN": TN}[mode]
    carry = slab is not None and slab[0] is not None

    def body_one(a_ref, b_ref, *rest):
        o_ref = rest[-1]
        o_ref[...] = _dg(a_ref[...], b_ref[...], dn).astype(o_ref.dtype)

    def body_acc(a_ref, b_ref, *rest):
        o_ref, acc_ref = rest[-2:]
        k = pl.program_id(2)

        @pl.when(k == 0)
        def _():
            acc_ref[...] = jnp.zeros_like(acc_ref)

        acc_ref[...] += _dg(a_ref[...], b_ref[...], dn)

        @pl.when(k == nk - 1)
        def _():
            o_ref[...] = acc_ref[...].astype(o_ref.dtype)

    a_spec = (pl.BlockSpec((tk, tm), lambda i, j, k: (k, i)) if mode == "TN"
              else pl.BlockSpec((tm, tk), lambda i, j, k: (i, k)))
    b_spec = (pl.BlockSpec((tn, tk), lambda i, j, k: (j, k)) if mode == "NT"
              else pl.BlockSpec((tk, tn), lambda i, j, k: (k, j)))
    if slab is None:
        out_spec = pl.BlockSpec((tm, tn), lambda i, j, k: (i, j))
        out_shape = jax.ShapeDtypeStruct((M, N), out_dtype)
    else:
        layer = slab[1]
        out_spec = pl.BlockSpec((None, tm, tn), lambda i, j, k: (layer, i, j))
        out_shape = jax.ShapeDtypeStruct((slab[2], M, N), out_dtype)
    return pl.pallas_call(
        body_one if nk == 1 else body_acc, name=name, grid=(M // tm, N // tn, nk),
        in_specs=[a_spec, b_spec] + ([_ANY] if carry else []),
        out_specs=out_spec, out_shape=out_shape,
        input_output_aliases={2: 0} if carry else {},
        scratch_shapes=[] if nk == 1 else [pltpu.VMEM((tm, tn), F32)],
        compiler_params=_cp(("parallel", "parallel", "arbitrary")),
    )(*((a, b, slab[0]) if carry else (a, b)))


def _ada_fwd(c16, ada_w, ada_b, name):
    rows, D = c16.shape
    N6 = ada_w.shape[1]
    tn = _tile(N6, 1536)

    def body(c_ref, w_ref, b_ref, act_ref, mod_ref):
        c = c_ref[...]
        act = (c * _sigmoid(c)).astype(act_ref.dtype)
        act_ref[...] = act
        mod_ref[...] = _dg(act, w_ref[...], NN) + b_ref[...]

    return pl.pallas_call(
        body, name=name, grid=(N6 // tn,),
        in_specs=[pl.BlockSpec((rows, D), lambda j: (0, 0)),
                  pl.BlockSpec((D, tn), lambda j: (0, j)),
                  pl.BlockSpec((1, tn), lambda j: (0, j))],
        out_specs=[pl.BlockSpec((rows, D), lambda j: (0, 0)),
                   pl.BlockSpec((rows, tn), lambda j: (0, j))],
        out_shape=[jax.ShapeDtypeStruct((rows, D), MXU_DTYPE),
                   jax.ShapeDtypeStruct((rows, N6), F32)],
        compiler_params=_cp(("arbitrary",)),
    )(c16, ada_w, ada_b.reshape(1, N6))


def _norm_mod_rows(x, g, sc, sh):
    r = lax.rsqrt(jnp.mean(x * x, axis=-1, keepdims=True) + EPS)
    return ((x * r) * g) * (1.0 + sc) + sh


def _norm_mod_fwd(x, g, mod, row_sh, name, ts=512):
    Bl, S, D = x.shape
    ts = _tile(S, ts, 8)

    def body(x_ref, g_ref, mod_ref, h_ref):
        sh = mod_ref[row_sh:row_sh + 1, :]
        sc = mod_ref[row_sh + 1:row_sh + 2, :]
        h_ref[...] = _norm_mod_rows(x_ref[...], g_ref[...], sc, sh).astype(h_ref.dtype)

    tok = pl.BlockSpec((None, ts, D), lambda b, s: (b, s, 0))
    return pl.pallas_call(
        body, name=name, grid=(Bl, S // ts),
        in_specs=[tok, pl.BlockSpec((1, D), lambda b, s: (0, 0)),
                  pl.BlockSpec((None, 8, D), lambda b, s: (b, 0, 0))],
        out_specs=tok, out_shape=jax.ShapeDtypeStruct((Bl, S, D), MXU_DTYPE),
        compiler_params=_cp(("parallel", "parallel")),
    )(x, g.reshape(1, D), mod)


def _post_norm_fwd(xp, f, post_g, mod, row_gt, nxt, name, ts=512):
    Bl, S, D = xp.shape
    ts = _tile(S, ts, 8)
    has_next = nxt is not None

    def body(*refs):
        if has_next:
            xp_ref, f_ref, pg_ref, mod_ref, ng_ref, nmod_ref, x_ref, h_ref = refs
        else:
            xp_ref, f_ref, pg_ref, mod_ref, x_ref = refs
        f = f_ref[...]
        r = lax.rsqrt(jnp.mean(f * f, axis=-1, keepdims=True) + EPS)
        x = xp_ref[...] + mod_ref[row_gt:row_gt + 1, :] * ((f * r) * pg_ref[...])
        x_ref[...] = x
        if has_next:
            rs = nxt[2]
            h_ref[...] = _norm_mod_rows(x, ng_ref[...], nmod_ref[rs + 1:rs + 2, :], nmod_ref[rs:rs + 1, :]).astype(h_ref.dtype)

    tok = pl.BlockSpec((None, ts, D), lambda b, s: (b, s, 0))
    vec = pl.BlockSpec((1, D), lambda b, s: (0, 0))
    modspec = pl.BlockSpec((None, 8, D), lambda b, s: (b, 0, 0))
    ins = [xp, f, post_g.reshape(1, D), mod]
    in_specs = [tok, tok, vec, modspec]
    out_specs = [tok]
    out_shape = [jax.ShapeDtypeStruct((Bl, S, D), F32)]
    if has_next:
        ins += [nxt[0].reshape(1, D), nxt[1]]
        in_specs += [vec, modspec]
        out_specs += [tok]
        out_shape += [jax.ShapeDtypeStruct((Bl, S, D), MXU_DTYPE)]
    out = pl.pallas_call(
        body, name=name, grid=(Bl, S // ts), in_specs=in_specs, out_specs=out_specs, out_shape=out_shape,
        compiler_params=_cp(("parallel", "parallel")),
    )(*ins)
    return (out[0], out[1]) if has_next else (out[0], None)


def _loss_fwd_bwd(y, target, name, ts=512):
    Bl, S, D = y.shape
    ts = _tile(S, ts, 8)

    def body(y_ref, t_ref, dy_ref, l_ref):
        @pl.when((pl.program_id(0) == 0) & (pl.program_id(1) == 0))
        def _():
            l_ref[...] = jnp.zeros_like(l_ref)

        e = y_ref[...] - t_ref[...]
        dy_ref[...] = e * (1.0 / D)
        l_ref[...] += 0.5 * jnp.sum(jnp.mean(e * e, axis=-1, keepdims=True), axis=0, keepdims=True)

    tok = pl.BlockSpec((None, ts, D), lambda b, s: (b, s, 0))
    dy, l = pl.pallas_call(
        body, name=name, grid=(Bl, S // ts), in_specs=[tok, tok],
        out_specs=[tok, pl.BlockSpec((8, LANES), lambda b, s: (0, 0))],
        out_shape=[jax.ShapeDtypeStruct((Bl, S, D), F32), jax.ShapeDtypeStruct((8, LANES), F32)],
        compiler_params=_cp(("arbitrary", "arbitrary")),
    )(y, target)
    return dy, l[0, 0]


def _norm_bwd(dx_res, nxt, prv, name, ts=256):
    Bl, S, D = dx_res.shape
    ts = _tile(S, ts, 8)
    has_next, has_prev = nxt is not None, prv is not None

    def body(*refs):
        refs = list(refs)
        dxr_ref = refs.pop(0)
        if has_next:
            dh_ref, x_ref, g_ref, nmod_ref = refs[:4]
            refs = refs[4:]
        if has_prev:
            f_ref, pg_ref, pmod_ref = refs[:3]
            refs = refs[3:]
        dx_ref = refs.pop(0)
        if has_prev:
            df_ref = refs.pop(0)
        pb_ref, sh_ref = refs
        b, s = pl.program_id(0), pl.program_id(1)

        @pl.when(s == 0)
        def _():
            pb_ref[...] = jnp.zeros_like(pb_ref)

        @pl.when((b == 0) & (s == 0))
        def _():
            sh_ref[...] = jnp.zeros_like(sh_ref)

        dx = dxr_ref[...]
        if has_next:
            rs = nxt[4]
            x, dh, g = x_ref[...], dh_ref[...], g_ref[...]
            sc1 = 1.0 + nmod_ref[rs + 1:rs + 2, :]
            r = lax.rsqrt(jnp.mean(x * x, axis=-1, keepdims=True) + EPS)
            xn = x * r
            pb_ref[0:1, :] += jnp.sum(dh, axis=0, keepdims=True)
            pb_ref[1:2, :] += jnp.sum(dh * (xn * g), axis=0, keepdims=True)
            sh_ref[0:1, :] += jnp.sum(dh * sc1 * xn, axis=0, keepdims=True)
            dxn = dh * sc1 * g
            dx = dx + r * (dxn - xn * jnp.mean(dxn * xn, axis=-1, keepdims=True))
        dx_ref[...] = dx
        if has_prev:
            rg = prv[3]
            f, pg = f_ref[...], pg_ref[...]
            gt = pmod_ref[rg:rg + 1, :]
            r = lax.rsqrt(jnp.mean(f * f, axis=-1, keepdims=True) + EPS)
            fn = f * r
            pb_ref[2:3, :] += jnp.sum(dx * (fn * pg), axis=0, keepdims=True)
            drn = dx * gt
            sh_ref[1:2, :] += jnp.sum(drn * fn, axis=0, keepdims=True)
            dfn = drn * pg
            df_ref[...] = (r * (dfn - fn * jnp.mean(dfn * fn, axis=-1, keepdims=True))).astype(df_ref.dtype)

    tok = pl.BlockSpec((None, ts, D), lambda b, s: (b, s, 0))
    vec = pl.BlockSpec((1, D), lambda b, s: (0, 0))
    modspec = pl.BlockSpec((None, 8, D), lambda b, s: (b, 0, 0))
    ins, in_specs = [dx_res], [tok]
    if has_next:
        ins += [nxt[0], nxt[1], nxt[2].reshape(1, D), nxt[3]]
        in_specs += [tok, tok, vec, modspec]
    if has_prev:
        ins += [prv[0], prv[1].reshape(1, D), prv[2]]
        in_specs += [tok, vec, modspec]
    out_specs, out_shape = [tok], [jax.ShapeDtypeStruct((Bl, S, D), F32)]
    if has_prev:
        out_specs += [tok]
        out_shape += [jax.ShapeDtypeStruct((Bl, S, D), MXU_DTYPE)]
    out_specs += [modspec, pl.BlockSpec((8, D), lambda b, s: (0, 0))]
    out_shape += [jax.ShapeDtypeStruct((Bl, 8, D), F32), jax.ShapeDtypeStruct((8, D), F32)]
    out = pl.pallas_call(
        body, name=name, grid=(Bl, S // ts), in_specs=in_specs, out_specs=out_specs, out_shape=out_shape,
        compiler_params=_cp(("arbitrary", "arbitrary")),
    )(*ins)
    if has_prev:
        return out[0], out[1], out[2], out[3]
    return out[0], None, out[1], out[2]


def _shift_down(x, j):
    return x if j == 0 else pltpu.roll(x, j, axis=0)


def _shift_up(x, j):
    return x if j == 0 else pltpu.roll(x, x.shape[0] - j, axis=0)


def _conv(xall, w_ref, K):
    y = w_ref[K - 1:K, :] * xall
    for k in range(K - 1):
        y = y + w_ref[k:k + 1, :] * _shift_down(xall, K - 1 - k)
    return y


def _conv_t(dall, xall, w, K, rows):
    xt = xall[HALO:HALO + rows]
    y = w[K - 1:K, :] * dall[HALO:HALO + rows]
    gw = [None] * K
    gw[K - 1] = jnp.sum(dall[HALO:HALO + rows] * xt, axis=0, keepdims=True)
    for k in range(K - 1):
        sh = _shift_up(dall, K - 1 - k)[HALO:HALO + rows]
        y = y + w[k:k + 1, :] * sh
        gw[k] = jnp.sum(sh * xt, axis=0, keepdims=True)
    return y, gw


def _conv_t_rows(dall, w_ref, K):
    y = w_ref[K - 1:K, :] * dall
    for k in range(K - 1):
        y = y + w_ref[k:k + 1, :] * _shift_up(dall, K - 1 - k)
    return y


def _conv_wgrad(acc_ref, dtile, xall, K, ts):
    for k in range(K):
        xs = _shift_down(xall, K - 1 - k)[HALO:HALO + ts]
        acc_ref[k:k + 1, :] += jnp.sum(dtile * xs, axis=0, keepdims=True)


ROW_CHUNK = 128


def _lane_chunks(W):
    return [(j * LANES, LANES) for j in range(W // LANES)]


def _fill_ext(ext_ref, prev, tile, nxt, ts):
    ext_ref[0:HALO, :] = prev
    ext_ref[HALO:HALO + ts, :] = tile
    ext_ref[HALO + ts:2 * HALO + ts, :] = nxt


def _halo_specs(ts, W, nS, colblk):
    per = ts // HALO
    tile = pl.BlockSpec((None, ts, W), lambda b, c, s: (b, s, colblk(c)))
    prev = pl.BlockSpec((None, HALO, W), lambda b, c, s: (b, jnp.maximum(s * per - 1, 0), colblk(c)))
    nxt = pl.BlockSpec((None, HALO, W), lambda b, c, s: (b, jnp.minimum((s + 1) * per, nS * per - 1), colblk(c)))
    return tile, prev, nxt


def _masked(ref, keep):
    v = ref[...]
    return jnp.where(keep, v, jnp.zeros_like(v))


def _ssd_conv_fwd(proj, w8, b, off, CD, name, ts=512, W=1024):
    Bl, S, _ = proj.shape
    K = 4
    ts, W = _tile(S, ts, 8), _tile(math.gcd(CD, off), W)
    assert off % W == 0
    nS, nW, ob = S // ts, CD // W, off // W

    def body(x_ref, xp_ref, w_ref, b_ref, o_ref):
        xall = jnp.concatenate([_masked(xp_ref, pl.program_id(2) > 0), x_ref[...]], axis=0)
        xc = _conv(xall, w_ref, K)[HALO:] + b_ref[...]
        o_ref[...] = xc * _sigmoid(xc)

    tile, prev, _ = _halo_specs(ts, W, nS, lambda c: ob + c)
    return pl.pallas_call(
        body, name=name, grid=(Bl, nW, nS),
        in_specs=[tile, prev, pl.BlockSpec((8, W), lambda b_, c, s: (0, c)), pl.BlockSpec((1, W), lambda b_, c, s: (0, c))],
        out_specs=pl.BlockSpec((None, ts, W), lambda b_, c, s: (b_, s, c)),
        out_shape=jax.ShapeDtypeStruct((Bl, S, CD), F32),
        compiler_params=_cp(("parallel", "parallel", "parallel")),
    )(proj, proj, w8, b.reshape(1, CD))


def _ssd_conv_bwd(dxa, proj, w8, b, off, CD, dproj, name, ts=256, W=1024):
    Bl, S, _ = proj.shape
    K = 4
    ts, W = _tile(S, ts, 8), _tile(math.gcd(CD, off), W)
    nS, nW, ob = S // ts, CD // W, off // W

    rc = _tile(ts, ROW_CHUNK, 8)

    def body(d_ref, dn_ref, x_ref, xp_ref, xn_ref, w_ref, b_ref, _, dx_ref, dw_ref, db_ref, xext_ref, dext_ref):
        bb, s = pl.program_id(1), pl.program_id(2)

        @pl.when((bb == 0) & (s == 0))
        def _():
            dw_ref[...] = jnp.zeros_like(dw_ref)
            db_ref[...] = jnp.zeros_like(db_ref)

        last = s == nS - 1
        _fill_ext(xext_ref, _masked(xp_ref, s > 0), x_ref[...], _masked(xn_ref, ~last), ts)
        _fill_ext(dext_ref, jnp.zeros((HALO, W), F32), d_ref[...], _masked(dn_ref, ~last), ts)
        for l0, lc in _lane_chunks(W):
            w, bias = w_ref[:, l0:l0 + lc], b_ref[:, l0:l0 + lc]

            def chunk(i, acc):
                r0 = pl.multiple_of(i * rc, 8)
                xw = xext_ref[pl.ds(r0, rc + 2 * HALO), l0:l0 + lc]
                xc = _conv(xw, w, K) + bias
                sg = _sigmoid(xc)
                dxc = dext_ref[pl.ds(r0, rc + 2 * HALO), l0:l0 + lc] * (sg * (1.0 + xc * (1.0 - sg)))
                dx, gw = _conv_t(dxc, xw, w, K, rc)
                dx_ref[pl.ds(r0, rc), l0:l0 + lc] = dx.astype(dx_ref.dtype)
                gb = jnp.sum(dxc[HALO:HALO + rc], axis=0, keepdims=True)
                return tuple(a + g for a, g in zip(acc, gw + [gb]))

            acc = lax.fori_loop(0, ts // rc, chunk, tuple(jnp.zeros((1, lc), F32) for _ in range(K + 1)))
            for k in range(K):
                dw_ref[k:k + 1, l0:l0 + lc] += acc[k]
            db_ref[0:1, l0:l0 + lc] += acc[K]

    per = ts // HALO
    dtile_s = pl.BlockSpec((None, ts, W), lambda c, b_, s: (b_, s, c))
    dnext_s = pl.BlockSpec((None, HALO, W), lambda c, b_, s: (b_, jnp.minimum((s + 1) * per, nS * per - 1), c))
    xtile_s = pl.BlockSpec((None, ts, W), lambda c, b_, s: (b_, s, ob + c))
    xprev_s = pl.BlockSpec((None, HALO, W), lambda c, b_, s: (b_, jnp.maximum(s * per - 1, 0), ob + c))
    xnext_s = pl.BlockSpec((None, HALO, W), lambda c, b_, s: (b_, jnp.minimum((s + 1) * per, nS * per - 1), ob + c))
    wspec = pl.BlockSpec((8, W), lambda c, b_, s: (0, c))
    return pl.pallas_call(
        body, name=name, grid=(nW, Bl, nS),
        in_specs=[dtile_s, dnext_s, xtile_s, xprev_s, xnext_s, wspec, pl.BlockSpec((1, W), lambda c, b_, s: (0, c)),
                  _ANY],
        out_specs=[xtile_s, wspec, wspec],
        out_shape=[jax.ShapeDtypeStruct(dproj.shape, dproj.dtype), jax.ShapeDtypeStruct((8, CD), F32),
                   jax.ShapeDtypeStruct((8, CD), F32)],
        input_output_aliases={7: 0},
        scratch_shapes=[pltpu.VMEM((ts + 2 * HALO, W), F32)] * 2,
        compiler_params=_cp(("arbitrary", "arbitrary", "arbitrary")),
    )(dxa, dxa, proj, proj, proj, w8, b.reshape(1, CD), dproj)


def _sc_conv_fwd(proj, w8, offs, D, name, ts=512, W=1024):
    Bl, S, _ = proj.shape
    K = 3
    ts, W = _tile(S, ts, 8), _tile(D, W)
    nS, nW = S // ts, D // W
    ob, oc, oh = [o // W for o in offs]

    def body(b_ref, c_ref, cp_ref, h_ref, hp_ref, w_ref, o_ref):
        s = pl.program_id(2)
        keep = s > 0
        vall = jnp.concatenate([_masked(cp_ref, keep) * _masked(hp_ref, keep), c_ref[...] * h_ref[...]], axis=0)
        o_ref[...] = (b_ref[...] * _conv(vall, w_ref, K)[HALO:]).astype(o_ref.dtype)

    tb, _, _ = _halo_specs(ts, W, nS, lambda c: ob + c)
    tc, pc, _ = _halo_specs(ts, W, nS, lambda c: oc + c)
    th, ph, _ = _halo_specs(ts, W, nS, lambda c: oh + c)
    return pl.pallas_call(
        body, name=name, grid=(Bl, nW, nS),
        in_specs=[tb, tc, pc, th, ph, pl.BlockSpec((8, W), lambda b_, c, s: (0, c))],
        out_specs=pl.BlockSpec((None, ts, W), lambda b_, c, s: (b_, s, c)),
        out_shape=jax.ShapeDtypeStruct((Bl, S, D), MXU_DTYPE),
        compiler_params=_cp(("parallel", "parallel", "parallel")),
    )(proj, proj, proj, proj, proj, w8)


def _sc_conv_bwd(ds, proj, w8, offs, D, dproj, name, ts=256):
    Bl, S, _ = proj.shape
    K = 3
    ts, W = _tile(S, ts, 8), D
    nS, nW = S // ts, 1
    ob, oc, oh = [o // W for o in offs]
    assert offs[1] == offs[0] + D and offs[2] == offs[0] + 2 * D and offs[0] % (3 * D) == 0

    def body(d_ref, dn_ref, b_ref, bn_ref, c_ref, cp_ref, cn_ref, h_ref, hp_ref, hn_ref, w_ref, _,
             o_ref, dw_ref):
        db_ref, dc_ref, dh_ref = o_ref.at[:, 0:D], o_ref.at[:, D:2 * D], o_ref.at[:, 2 * D:3 * D]
        bb, s = pl.program_id(1), pl.program_id(2)

        @pl.when((bb == 0) & (s == 0))
        def _():
            dw_ref[...] = jnp.zeros_like(dw_ref)

        first, last = s > 0, s < nS - 1
        zeros = jnp.zeros((HALO, W), F32)
        c_t, h_t = c_ref[...], h_ref[...]
        vall = jnp.concatenate([_masked(cp_ref, first) * _masked(hp_ref, first), c_t * h_t,
                                _masked(cn_ref, last) * _masked(hn_ref, last)], axis=0)
        dcv = jnp.concatenate([zeros, d_ref[...] * b_ref[...], _masked(dn_ref, last) * _masked(bn_ref, last)], axis=0)
        cv = _conv(vall, w_ref, K)[HALO:HALO + ts]
        db_ref[...] = (d_ref[...] * cv).astype(db_ref.dtype)
        dv, gw = _conv_t(dcv, vall, w_ref, K, ts)
        dc_ref[...] = (dv * h_t).astype(dc_ref.dtype)
        dh_ref[...] = (dv * c_t).astype(dh_ref.dtype)
        for k in range(K):
            dw_ref[k:k + 1, :] += gw[k]

    per = ts // HALO

    def specs(o):
        t = pl.BlockSpec((None, ts, W), lambda c, b_, s: (b_, s, o + c))
        p = pl.BlockSpec((None, HALO, W), lambda c, b_, s: (b_, jnp.maximum(s * per - 1, 0), o + c))
        n = pl.BlockSpec((None, HALO, W), lambda c, b_, s: (b_, jnp.minimum((s + 1) * per, nS * per - 1), o + c))
        return t, p, n

    dt_, _, dn_ = specs(0)
    bt, _, bn = specs(ob)
    ct, cp, cn = specs(oc)
    ht, hp, hn = specs(oh)
    wspec = pl.BlockSpec((8, W), lambda c, b_, s: (0, c))
    o3 = offs[0] // (3 * D)
    return pl.pallas_call(
        body, name=name, grid=(nW, Bl, nS),
        in_specs=[dt_, dn_, bt, bn, ct, cp, cn, ht, hp, hn, wspec, _ANY],
        out_specs=[pl.BlockSpec((None, ts, 3 * D), lambda c, b_, s: (b_, s, o3)), wspec],
        out_shape=[jax.ShapeDtypeStruct(dproj.shape, dproj.dtype), jax.ShapeDtypeStruct((8, D), F32)],
        input_output_aliases={11: 0},
        compiler_params=_cp(("arbitrary", "arbitrary", "arbitrary")),
    )(ds, ds, proj, proj, proj, proj, proj, proj, proj, proj, w8, dproj)


def _ffn_conv_fwd(up, w8, b, DFF, name, ts=512, W=1408):
    Bl, S, _ = up.shape
    K = 3
    ts, W = _tile(S, ts, 8), _tile(DFF, W)
    nS, nW = S // ts, DFF // W

    def body(g_ref, gp_ref, v_ref, vp_ref, wg_ref, wv_ref, bg_ref, bv_ref, o_ref):
        keep = pl.program_id(2) > 0
        ug = _conv(jnp.concatenate([_masked(gp_ref, keep), g_ref[...]], axis=0), wg_ref, K)[HALO:] + bg_ref[...]
        uv = _conv(jnp.concatenate([_masked(vp_ref, keep), v_ref[...]], axis=0), wv_ref, K)[HALO:] + bv_ref[...]
        o_ref[...] = (ug * _sigmoid(ug) * uv).astype(o_ref.dtype)

    tg, pg, _ = _halo_specs(ts, W, nS, lambda c: c)
    tv, pv, _ = _halo_specs(ts, W, nS, lambda c: nW + c)
    wg = pl.BlockSpec((8, W), lambda b_, c, s: (0, c))
    wv = pl.BlockSpec((8, W), lambda b_, c, s: (0, nW + c))
    bg = pl.BlockSpec((1, W), lambda b_, c, s: (0, c))
    bv = pl.BlockSpec((1, W), lambda b_, c, s: (0, nW + c))
    b2 = b.reshape(1, 2 * DFF)
    return pl.pallas_call(
        body, name=name, grid=(Bl, nW, nS),
        in_specs=[tg, pg, tv, pv, wg, wv, bg, bv],
        out_specs=pl.BlockSpec((None, ts, W), lambda b_, c, s: (b_, s, c)),
        out_shape=jax.ShapeDtypeStruct((Bl, S, DFF), MXU_DTYPE),
        compiler_params=_cp(("parallel", "parallel", "parallel")),
    )(up, up, up, up, w8, w8, b2, b2)


def _ffn_conv_bwd(da, up, w8, b, DFF, name, ts=256):
    Bl, S, _ = up.shape
    K = 3
    ts, W = _tile(S, ts, 8), DFF
    nS = S // ts
    rc = _tile(ts, ROW_CHUNK, 8)

    def body(d_ref, dn_ref, g_ref, gp_ref, gn_ref, v_ref, vp_ref, vn_ref, w_ref, b_ref,
             dup_ref, dw_ref, db_ref, gext_ref, vext_ref, dext_ref):
        bb, s = pl.program_id(0), pl.program_id(1)

        @pl.when((bb == 0) & (s == 0))
        def _():
            dw_ref[...] = jnp.zeros_like(dw_ref)
            db_ref[...] = jnp.zeros_like(db_ref)

        first, last = s > 0, s < nS - 1
        _fill_ext(gext_ref, _masked(gp_ref, first), g_ref[...], _masked(gn_ref, last), ts)
        _fill_ext(vext_ref, _masked(vp_ref, first), v_ref[...], _masked(vn_ref, last), ts)
        _fill_ext(dext_ref, jnp.zeros((HALO, W), F32), d_ref[...], _masked(dn_ref, last), ts)
        for l0, lc in _lane_chunks(W):
            gl, vl = slice(l0, l0 + lc), slice(DFF + l0, DFF + l0 + lc)
            wg_, wv_, bg_, bv_ = w_ref[:, gl], w_ref[:, vl], b_ref[:, gl], b_ref[:, vl]

            def chunk(i, acc):
                r0 = pl.multiple_of(i * rc, 8)
                win = pl.ds(r0, rc + 2 * HALO)
                gw_, vw_, dw_ = gext_ref[win, gl], vext_ref[win, gl], dext_ref[win, gl]
                ug = _conv(gw_, wg_, K) + bg_
                uv = _conv(vw_, wv_, K) + bv_
                sg = _sigmoid(ug)
                dug = dw_ * uv * (sg * (1.0 + ug * (1.0 - sg)))
                duv = dw_ * (ug * sg)
                dg, gg = _conv_t(dug, gw_, wg_, K, rc)
                dv, gv = _conv_t(duv, vw_, wv_, K, rc)
                dup_ref[pl.ds(r0, rc), gl] = dg.astype(dup_ref.dtype)
                dup_ref[pl.ds(r0, rc), vl] = dv.astype(dup_ref.dtype)
                new = gg + gv + [jnp.sum(dug[HALO:HALO + rc], axis=0, keepdims=True),
                                 jnp.sum(duv[HALO:HALO + rc], axis=0, keepdims=True)]
                return tuple(a + g for a, g in zip(acc, new))

            acc = lax.fori_loop(0, ts // rc, chunk, tuple(jnp.zeros((1, lc), F32) for _ in range(2 * K + 2)))
            for k in range(K):
                dw_ref[k:k + 1, gl] += acc[k]
                dw_ref[k:k + 1, vl] += acc[K + k]
            db_ref[0:1, gl] += acc[2 * K]
            db_ref[0:1, vl] += acc[2 * K + 1]

    per = ts // HALO

    def specs(blk):
        t = pl.BlockSpec((None, ts, W), lambda b_, s: (b_, s, blk))
        p = pl.BlockSpec((None, HALO, W), lambda b_, s: (b_, jnp.maximum(s * per - 1, 0), blk))
        n = pl.BlockSpec((None, HALO, W), lambda b_, s: (b_, jnp.minimum((s + 1) * per, nS * per - 1), blk))
        return t, p, n

    dt_, _, dn_ = specs(0)
    gt, gp, gn = specs(0)
    vt, vp, vn = specs(1)
    small = pl.BlockSpec((8, 2 * DFF), lambda b_, s: (0, 0))
    return pl.pallas_call(
        body, name=name, grid=(Bl, nS),
        in_specs=[dt_, dn_, gt, gp, gn, vt, vp, vn, small, pl.BlockSpec((1, 2 * DFF), lambda b_, s: (0, 0))],
        out_specs=[pl.BlockSpec((None, ts, 2 * DFF), lambda b_, s: (b_, s, 0)), small, small],
        out_shape=[jax.ShapeDtypeStruct((Bl, S, 2 * DFF), MXU_DTYPE), jax.ShapeDtypeStruct((8, 2 * DFF), F32),
                   jax.ShapeDtypeStruct((8, 2 * DFF), F32)],
        scratch_shapes=[pltpu.VMEM((ts + 2 * HALO, W), F32)] * 3,
        compiler_params=_cp(("arbitrary", "arbitrary")),
    )(da, da, up, up, up, up, up, up, w8, b.reshape(1, 2 * DFF))


def _merge_fwd(proj, y_ssd, y_sc, off, D, name, ts=512):
    Bl, S, _ = proj.shape
    ts = _tile(S, ts, 8)
    og = off // D

    def body(g1_ref, g2_ref, a_ref, b_ref, o_ref):
        o_ref[...] = (_sigmoid(g1_ref[...]) * a_ref[...] + _sigmoid(g2_ref[...]) * b_ref[...]).astype(o_ref.dtype)

    tok = pl.BlockSpec((None, ts, D), lambda b, s: (b, s, 0))
    return pl.pallas_call(
        body, name=name, grid=(Bl, S // ts),
        in_specs=[pl.BlockSpec((None, ts, D), lambda b, s: (b, s, og)),
                  pl.BlockSpec((None, ts, D), lambda b, s: (b, s, og + 1)), tok, tok],
        out_specs=tok, out_shape=jax.ShapeDtypeStruct((Bl, S, D), MXU_DTYPE),
        compiler_params=_cp(("parallel", "parallel")),
    )(proj, proj, y_ssd, y_sc)


def _merge_bwd(dmixin, proj, y_ssd, y_sc, off, D, dproj0, name, ts=512):
    Bl, S, NP = proj.shape
    ts = _tile(S, ts, 8)
    og = off // D
    assert off % (2 * D) == 0

    def body(d_ref, g1_ref, g2_ref, a_ref, b_ref, *rest):
        da_ref, db_ref, dg_ref = rest[-3:]
        dg1_ref, dg2_ref = dg_ref.at[:, 0:D], dg_ref.at[:, D:2 * D]
        d = d_ref[...]
        s1, s2 = _sigmoid(g1_ref[...]), _sigmoid(g2_ref[...])
        da_ref[...] = (d * s1).astype(da_ref.dtype)
        db_ref[...] = (d * s2).astype(db_ref.dtype)
        dg1_ref[...] = (d * a_ref[...] * (s1 * (1.0 - s1))).astype(dg1_ref.dtype)
        dg2_ref[...] = (d * b_ref[...] * (s2 * (1.0 - s2))).astype(dg2_ref.dtype)

    tok = pl.BlockSpec((None, ts, D), lambda b, s: (b, s, 0))
    act = jax.ShapeDtypeStruct((Bl, S, D), MXU_DTYPE)
    carry = dproj0 is not None
    return pl.pallas_call(
        body, name=name, grid=(Bl, S // ts),
        in_specs=[tok, pl.BlockSpec((None, ts, D), lambda b, s: (b, s, og)),
                  pl.BlockSpec((None, ts, D), lambda b, s: (b, s, og + 1)), tok, tok] + ([_ANY] if carry else []),
        out_specs=[tok, tok, pl.BlockSpec((None, ts, 2 * D), lambda b, s: (b, s, og // 2))],
        out_shape=[act, act, jax.ShapeDtypeStruct((Bl, S, NP), MXU_DTYPE)],
        input_output_aliases={5: 2} if carry else {},
        compiler_params=_cp(("parallel", "parallel")),
    )(*((dmixin, proj, proj, y_ssd, y_sc) + ((dproj0,) if carry else ())))


def _exact_dot(a, b, dn, value):
    v = a if value == 0 else b
    m01 = (b if value == 0 else a).astype(MXU_DTYPE)
    hi = v.astype(MXU_DTYPE)
    r1 = v - hi.astype(F32)
    mid = r1.astype(MXU_DTYPE)
    lo = (r1 - mid.astype(F32)).astype(MXU_DTYPE)
    terms = [(t, m01) if value == 0 else (m01, t) for t in (hi, mid, lo)]
    return _dg(*terms[0], dn) + _dg(*terms[1], dn) + _dg(*terms[2], dn)


def _two_term_dot(v, m01):
    hi = v.astype(MXU_DTYPE)
    lo = (v - hi.astype(F32)).astype(MXU_DTYPE)
    return _dg(hi, m01, NN) + _dg(lo, m01, NN)


def _head_to_channels(R):
    rp = R * HEAD_DIM
    return (lax.shift_right_logical(lax.broadcasted_iota(jnp.int32, (LANES, rp), 1), 6)
            == lax.broadcasted_iota(jnp.int32, (LANES, rp), 0)).astype(MXU_DTYPE)


def _expand_heads(mats, R):
    L = mats[0].shape[0]
    out = _two_term_dot(jnp.concatenate(mats, axis=0), _head_to_channels(R))
    return [out[i * L:(i + 1) * L, :] for i in range(len(mats))]


def _expand_cols(mat, R, lane):
    half = lane < HEAD_DIM
    return jnp.concatenate(
        [jnp.where(half, mat[:, 2 * q:2 * q + 1], mat[:, 2 * q + 1:2 * q + 2]) for q in range(R // 2)], axis=1)


def _row_sums(v):
    return _two_term_dot(v, jnp.ones((v.shape[1], LANES), MXU_DTYPE))


def _lanes_to(v, width):
    return jnp.concatenate([v] * (width // LANES), axis=1)


def _head_rows(colvec, R, N):
    return jnp.concatenate([jnp.broadcast_to(colvec[r:r + 1, :], (HEAD_DIM, N)) for r in range(R)], axis=0)


def _ssd_prep(proj, dt_bias, a_log, selcat, selbig, dm, name):
    Bl, S, L, G = dm.Bl, dm.S, CHUNK, N_GROUPS
    nc = S // L
    odt = dm.ODT // LANES

    def body(dtr_ref, dtb_ref, alog_ref, selcat_ref, selbig_ref, dtg_ref, acs_ref, acst_ref):
        row = lax.broadcasted_iota(jnp.int32, (L, L), 0)
        col = lax.broadcasted_iota(jnp.int32, (L, L), 1)
        dt_all = _softplus(dtr_ref[...] + dtb_ref[...])
        acs_all = _exact_dot((row >= col).astype(F32), dt_all * (-jnp.exp(alog_ref[...])), NN, 1)
        dtg_ref[...] = _exact_dot(dt_all, selcat_ref[...], NN, 0)
        acs_ref[...] = _exact_dot(acs_all, selcat_ref[...], NN, 0)
        acst_ref[...] = _exact_dot(selbig_ref[...], acs_all, NT, 1)

    vec = pl.BlockSpec((1, LANES), lambda b, c: (0, 0))
    wide = pl.BlockSpec((None, L, G * LANES), lambda b, c: (b, c, 0))
    return pl.pallas_call(
        body, name=name, grid=(Bl, nc),
        in_specs=[pl.BlockSpec((None, L, LANES), lambda b, c: (b, c, odt)), vec, vec,
                  pl.BlockSpec((LANES, G * LANES), lambda b, c: (0, 0)),
                  pl.BlockSpec((G * 8, LANES), lambda b, c: (0, 0))],
        out_specs=[wide, wide, pl.BlockSpec((None, None, G * 8, L), lambda b, c: (b, c, 0, 0))],
        out_shape=[jax.ShapeDtypeStruct((Bl, S, G * LANES), F32), jax.ShapeDtypeStruct((Bl, S, G * LANES), F32),
                   jax.ShapeDtypeStruct((Bl, nc, G * 8, L), F32)],
        compiler_params=_cp(("parallel", "parallel")),
    )(proj, dt_bias, a_log, selcat, selbig)


def _ssd_post(d_a, ddt, dtg, proj, dt_bias, a_log, selcat, dproj, dm, name):
    Bl, S, L, G = dm.Bl, dm.S, CHUNK, N_GROUPS
    nc = S // L
    odt = dm.ODT // LANES

    def body(da_ref, ddt_ref, dtg_ref, dtr_ref, dtb_ref, alog_ref, selcat_ref, _, ddtr_ref, dpar_ref):
        @pl.when((pl.program_id(0) == 0) & (pl.program_id(1) == 0))
        def _():
            dpar_ref[...] = jnp.zeros_like(dpar_ref)

        row = lax.broadcasted_iota(jnp.int32, (L, L), 0)
        col = lax.broadcasted_iota(jnp.int32, (L, L), 1)
        selcat = selcat_ref[...]
        a_all = -jnp.exp(alog_ref[...])
        a4 = _exact_dot(jnp.broadcast_to(a_all, (8, LANES)), selcat, NN, 0)[0:1, :]
        dadt = _exact_dot((col >= row).astype(F32), da_ref[...], NN, 1)
        ddt4 = ddt_ref[...] + dadt * a4
        da4 = jnp.sum(dadt * dtg_ref[...], axis=0, keepdims=True)
        ddt_all = _exact_dot(ddt4, selcat, NT, 0)
        da_all = _exact_dot(jnp.broadcast_to(da4, (8, G * LANES)), selcat, NT, 0)[0:1, :]
        ddtr = ddt_all * _sigmoid(dtr_ref[...] + dtb_ref[...])
        ddtr_ref[...] = ddtr.astype(ddtr_ref.dtype)
        dpar_ref[0:1, :] += jnp.sum(ddtr, axis=0, keepdims=True)
        dpar_ref[1:2, :] += da_all * a_all

    vec = pl.BlockSpec((1, LANES), lambda b, c: (0, 0))
    wide = pl.BlockSpec((None, L, G * LANES), lambda b, c: (b, c, 0))
    return pl.pallas_call(
        body, name=name, grid=(Bl, nc),
        in_specs=[wide, wide, wide, pl.BlockSpec((None, L, LANES), lambda b, c: (b, c, odt)), vec, vec,
                  pl.BlockSpec((LANES, G * LANES), lambda b, c: (0, 0)), _ANY],
        out_specs=[pl.BlockSpec((None, L, LANES), lambda b, c: (b, c, odt)), pl.BlockSpec((8, LANES), lambda b, c: (0, 0))],
        out_shape=[jax.ShapeDtypeStruct(dproj.shape, dproj.dtype), jax.ShapeDtypeStruct((8, LANES), F32)],
        input_output_aliases={7: 0},
        compiler_params=_cp(("arbitrary", "arbitrary")),
    )(d_a, ddt, dtg, proj, dt_bias, a_log, selcat, dproj)


def _scan_fwd(xbc_a, proj, dtg4, acs4, acs_t4, dsk_exp, norm_g, dm, name):
    Bl, S, DI, N, R, L, G = dm.Bl, dm.S, dm.DI, dm.N, dm.R, CHUNK, N_GROUPS
    RP = R * HEAD_DIM
    nc = S // L

    def body(x_ref, z_ref, dtg_ref, acs_ref, acst_ref, dsk_ref, ng_ref, y_ref, yn_ref, hp_ref, h_ref):
        xs_ref, bm_ref, cm_ref = x_ref.at[:, 0:RP], x_ref.at[:, RP:RP + N], x_ref.at[:, RP + N:RP + 2 * N]
        @pl.when(pl.program_id(2) == 0)
        def _():
            h_ref[...] = jnp.zeros_like(h_ref)

        causal = lax.broadcasted_iota(jnp.int32, (L, L), 0) >= lax.broadcasted_iota(jnp.int32, (L, L), 1)
        lane = lax.broadcasted_iota(jnp.int32, (L, LANES), 1)
        dtg, acs, acs_t = dtg_ref[...], acs_ref[...], acst_ref[...]
        xs = xs_ref[...]
        bmb, cmb = bm_ref[...].astype(MXU_DTYPE), cm_ref[...].astype(MXU_DTYPE)
        sg = _dg(cmb, bmb, NT)
        acs_last = acs[L - 1:L, :]
        dt_exp, ea_exp, ds_exp = _expand_heads([dtg, jnp.exp(acs), jnp.exp(acs_last - acs)], R)
        xdt = xs * dt_exp
        xb = xdt.astype(MXU_DTYPE)
        parts = []
        for q in range(R // 2):
            x2 = xb[:, LANES * q:LANES * (q + 1)]
            ys = []
            for r in (2 * q, 2 * q + 1):
                dec = jnp.exp(jnp.where(causal, acs[:, r:r + 1] - acs_t[r:r + 1, :], -1e30))
                ys.append(_dg((sg * dec).astype(MXU_DTYPE), x2, NN))
            parts.append(jnp.where(lane < HEAD_DIM, ys[0], ys[1]))
        ydiag = jnp.concatenate(parts, axis=1)
        h_cur = h_ref[...]
        hb = h_cur.astype(MXU_DTYPE)
        yoff = _dg(cmb, hb, NT) * ea_exp
        st = _dg((xdt * ds_exp).astype(MXU_DTYPE), bmb, TN)
        hp_ref[...] = hb
        h_ref[...] = h_cur * _head_rows(jnp.exp(acs_t[:, L - 1:L]), R, N) + st
        y = ydiag + yoff + dsk_ref[...] * xs
        y_ref[...] = y
        z = z_ref[...]
        yg = y * (z * _sigmoid(z))
        rr = lax.rsqrt(_row_sums(yg * yg) * (1.0 / RP) + EPS)
        yn_ref[...] = (yg * _lanes_to(rr, RP) * ng_ref[...]).astype(yn_ref.dtype)

    oz = dm.OZ // RP
    grp = pl.BlockSpec((None, L, RP), lambda b, g, c: (b, c, g))
    lanes = pl.BlockSpec((None, L, LANES), lambda b, g, c: (b, c, g))
    chan = pl.BlockSpec((1, RP), lambda b, g, c: (0, g))
    return pl.pallas_call(
        body, name=name, grid=(Bl, G, nc),
        in_specs=[pl.BlockSpec((None, L, RP + 2 * N), lambda b, g, c: (b, c, g)),
                  pl.BlockSpec((None, L, RP), lambda b, g, c: (b, c, oz + g)), lanes, lanes,
                  pl.BlockSpec((None, None, 8, L), lambda b, g, c: (b, c, g, 0)),
                  chan, chan],
        out_specs=[grp, grp, pl.BlockSpec((None, None, None, RP, N), lambda b, g, c: (b, g, c, 0, 0))],
        out_shape=[jax.ShapeDtypeStruct((Bl, S, DI), F32), jax.ShapeDtypeStruct((Bl, S, DI), MXU_DTYPE),
                   jax.ShapeDtypeStruct((Bl, G, nc, RP, N), MXU_DTYPE)],
        scratch_shapes=[pltpu.VMEM((RP, N), F32)],
        compiler_params=_cp(("parallel", "parallel", "arbitrary")),
    )(xbc_a, proj, dtg4, acs4, acs_t4, dsk_exp, norm_g.reshape(1, DI))


def _scan_bwd(dyn, y, xbc_a, proj, hprev, dtg4, acs4, acs_t4, dsk_exp, norm_g, dproj, dm, name):
    Bl, S, DI, N, R, L, G = dm.Bl, dm.S, dm.DI, dm.N, dm.R, CHUNK, N_GROUPS
    RP = R * HEAD_DIM
    nc = S // L

    def body(dyn_ref, y_ref, z_ref, x_ref, hp_ref, dtg_ref, acs_ref, acst_ref, dsk_ref, ng_ref,
             _, dz_ref, dx_ref, da_ref, ddt_ref, ddsk_ref, dng_ref, dh_ref):
        xs_ref, bm_ref, cm_ref = x_ref.at[:, 0:RP], x_ref.at[:, RP:RP + N], x_ref.at[:, RP + N:RP + 2 * N]
        dxs_ref, dbm_ref, dcm_ref = dx_ref.at[:, 0:RP], dx_ref.at[:, RP:RP + N], dx_ref.at[:, RP + N:RP + 2 * N]
        b, c = pl.program_id(1), pl.program_id(2)

        @pl.when(c == 0)
        def _():
            dh_ref[...] = jnp.zeros_like(dh_ref)

        @pl.when((b == 0) & (c == 0))
        def _():
            ddsk_ref[...] = jnp.zeros_like(ddsk_ref)
            dng_ref[...] = jnp.zeros_like(dng_ref)

        row = lax.broadcasted_iota(jnp.int32, (L, L), 0)
        col = lax.broadcasted_iota(jnp.int32, (L, L), 1)
        causal, anti = row >= col, col >= row
        lane = lax.broadcasted_iota(jnp.int32, (L, LANES), 1)
        etb = (lax.shift_right_logical(lax.broadcasted_iota(jnp.int32, (RP, LANES), 0), 6)
               == lax.broadcasted_iota(jnp.int32, (RP, LANES), 1)).astype(MXU_DTYPE)

        dtg, acs, acs_t = dtg_ref[...], acs_ref[...], acst_ref[...]
        xs, z, y, dyn = xs_ref[...], z_ref[...], y_ref[...], dyn_ref[...]
        bmb, cmb = bm_ref[...].astype(MXU_DTYPE), cm_ref[...].astype(MXU_DTYPE)
        hpb = hp_ref[...]
        ng = ng_ref[...]

        sz = _sigmoid(z)
        siluz = z * sz
        yg = y * siluz
        rr = lax.rsqrt(jnp.mean(yg * yg, axis=-1, keepdims=True) + EPS)
        yhat = yg * rr
        dng_ref[...] += jnp.sum(dyn * yhat, axis=0, keepdims=True)
        dyhat = dyn * ng
        dyg = rr * (dyhat - yhat * jnp.mean(dyhat * yhat, axis=-1, keepdims=True))
        dy = dyg * siluz
        dz_ref[...] = (dyg * y * (sz * (1.0 + z * (1.0 - sz)))).astype(dz_ref.dtype)

        dxs = dy * dsk_ref[...]
        ddsk_ref[...] += jnp.sum(dy * xs, axis=0, keepdims=True)

        acs_last = acs[L - 1:L, :]
        dt_exp = _expand_cols(dtg, R, lane)
        ea_exp = _expand_cols(jnp.exp(acs), R, lane)
        ds_exp = _expand_cols(jnp.exp(acs_last - acs), R, lane)
        xdt = xs * dt_exp
        xb = xdt.astype(MXU_DTYPE)
        dyb = dy.astype(MXU_DTYPE)
        cd = jnp.exp(acs_last)
        cd_rows = _head_rows(jnp.exp(acs_t[:, L - 1:L]), R, N)

        q_ = _dg(cmb, hpb, NT)
        dq = dy * ea_exp
        dqb = dq.astype(MXU_DTYPE)
        dcm = _dg(dqb, hpb, NN)
        dh_yoff = _dg(dqb, cmb, TN)

        dhn = dh_ref[...]
        wprod = dhn * hpb.astype(F32)
        per_head = jnp.concatenate(
            [jnp.sum(wprod[HEAD_DIM * r:HEAD_DIM * (r + 1), :], axis=0, keepdims=True) for r in range(R)]
            + ([jnp.zeros((8 - R, N), F32)] if R < 8 else []), axis=0)
        dcd_col = jnp.sum(per_head, axis=1, keepdims=True)
        diag8 = lax.broadcasted_iota(jnp.int32, (8, LANES), 0) == lax.broadcasted_iota(jnp.int32, (8, LANES), 1)
        dcd_lane = jnp.sum(jnp.where(diag8, dcd_col, 0.0), axis=0, keepdims=True)
        d_a_last = dcd_lane * cd
        dh_ref[...] = dhn * cd_rows + dh_yoff
        dhnb = dhn.astype(MXU_DTYPE)

        e_ = _dg(bmb, dhnb, NT)
        dxdt = ds_exp * e_
        xds = xdt * ds_exp
        dbm = _dg(xds.astype(MXU_DTYPE), dhnb, NN)

        sg = _dg(cmb, bmb, NT)
        sg_t = _dg(bmb, cmb, NT)
        dsg = jnp.zeros((L, L), F32)
        dsg_t = jnp.zeros((L, L), F32)
        d_a = jnp.zeros((L, LANES), F32)
        parts = []
        for q in range(R // 2):
            x2 = xb[:, LANES * q:LANES * (q + 1)]
            dy2 = dyb[:, LANES * q:LANES * (q + 1)]
            dxs2 = []
            for hh, r in enumerate((2 * q, 2 * q + 1)):
                mine = (lane < HEAD_DIM) if hh == 0 else (lane >= HEAD_DIM)
                diff = acs[:, r:r + 1] - acs_t[r:r + 1, :]
                dec = jnp.exp(jnp.where(causal, diff, -1e30))
                dec_t = jnp.exp(jnp.where(anti, -diff, -1e30))
                dy2m = jnp.where(mine, dy2, jnp.zeros_like(dy2))
                dm_ = _dg(dy2m, x2, NT)
                dm_t = _dg(x2, dy2m, NT)
                m_t = sg_t * dec_t
                da_col = jnp.sum(dm_ * (sg * dec) - dm_t * m_t, axis=1, keepdims=True)
                d_a = d_a + jnp.where(lane == r, da_col, 0.0)
                dsg = dsg + dm_ * dec
                dsg_t = dsg_t + dm_t * dec_t
                dxs2.append(_dg(m_t.astype(MXU_DTYPE), dy2, NN))
            parts.append(jnp.where(lane < HEAD_DIM, dxs2[0], dxs2[1]))
        dxdt = dxdt + jnp.concatenate(parts, axis=1)
        dcm_ref[...] = dcm + _dg(dsg.astype(MXU_DTYPE), bmb, NN)
        dbm_ref[...] = dbm + _dg(dsg_t.astype(MXU_DTYPE), cmb, NN)
        dxs_ref[...] = dxs + dxdt * dt_exp

        hs = _two_term_dot(jnp.concatenate([dq * q_ - xds * e_, xds * e_, dxdt * xs], axis=0), etb)
        t2 = hs[L:2 * L, :]
        rowl = lax.broadcasted_iota(jnp.int32, (L, LANES), 0)
        d_a_last = d_a_last + jnp.sum(t2, axis=0, keepdims=True)
        da_ref[...] = d_a + hs[0:L, :] + jnp.where(rowl == L - 1, d_a_last, 0.0)
        ddt_ref[...] = hs[2 * L:3 * L, :]

    oz = dm.OZ // RP
    grp = pl.BlockSpec((None, L, RP), lambda g, b, c: (b, nc - 1 - c, g))
    zspec = pl.BlockSpec((None, L, RP), lambda g, b, c: (b, nc - 1 - c, oz + g))
    lanes = pl.BlockSpec((None, L, LANES), lambda g, b, c: (b, nc - 1 - c, g))
    xg = pl.BlockSpec((None, L, RP + 2 * N), lambda g, b, c: (b, nc - 1 - c, g))
    chan = pl.BlockSpec((1, RP), lambda g, b, c: (0, g))
    wide = jax.ShapeDtypeStruct((Bl, S, G * LANES), F32)
    return pl.pallas_call(
        body, name=name, grid=(G, Bl, nc),
        in_specs=[grp, grp, zspec, xg,
                  pl.BlockSpec((None, None, None, RP, N), lambda g, b, c: (b, g, nc - 1 - c, 0, 0)),
                  lanes, lanes, pl.BlockSpec((None, None, 8, L), lambda g, b, c: (b, nc - 1 - c, g, 0)),
                  chan, chan, _ANY],
        out_specs=[zspec, xg, lanes, lanes, chan, chan],
        out_shape=[jax.ShapeDtypeStruct(dproj.shape, dproj.dtype), jax.ShapeDtypeStruct(xbc_a.shape, F32),
                   wide, wide, jax.ShapeDtypeStruct((1, DI), F32), jax.ShapeDtypeStruct((1, DI), F32)],
        input_output_aliases={10: 0},
        scratch_shapes=[pltpu.VMEM((RP, N), F32)],
        compiler_params=_cp(("arbitrary", "arbitrary", "arbitrary")),
    )(dyn, y, proj, xbc_a, hprev, dtg4, acs4, acs_t4, dsk_exp, norm_g.reshape(1, DI), dproj)


def _adam_update(w, m, v, g):
    c1 = 1.0 - ADAM_B1 ** ADAM_STEP
    c2 = 1.0 - ADAM_B2 ** ADAM_STEP
    nm = ADAM_B1 * m + (1.0 - ADAM_B1) * g
    nv = ADAM_B2 * v + (1.0 - ADAM_B2) * (g * g)
    return nm, nv, -ADAM_LR * ((nm / c1) / (jnp.sqrt(nv / c2) + ADAM_EPS) + ADAM_WD * w)


def _adamw(w, m, v, g, name, tr=128):
    rows, cols = w.shape
    tr = _tile(rows, tr, 8)

    def body(w_ref, m_ref, v_ref, ga_ref, g_ref, d_ref, nm_ref, nv_ref):
        g = ga_ref[...]
        g_ref[...] = g
        nm_ref[...], nv_ref[...], d_ref[...] = _adam_update(w_ref[...], m_ref[...], v_ref[...], g)

    blk = pl.BlockSpec((tr, cols), lambda i: (i, 0))
    shp = jax.ShapeDtypeStruct((rows, cols), F32)
    return pl.pallas_call(
        body, name=name, grid=(rows // tr,), in_specs=[blk] * 4, out_specs=[blk] * 4,
        out_shape=[shp] * 4, compiler_params=_cp(("parallel",)),
    )(w, m, v, g)


def _adamw_layers(w, m, v, g_mine, g_theirs, core, name, tr=128):
    _, rows, cols = w.shape
    tr = _tile(rows, tr, 8)

    def body(c_ref, w_ref, m_ref, v_ref, ga_ref, gb_ref, g_ref, d_ref, nm_ref, nv_ref):
        g = jnp.where(pl.program_id(0) == c_ref[0], ga_ref[...], gb_ref[...])
        g_ref[...] = g
        nm_ref[...], nv_ref[...], d_ref[...] = _adam_update(w_ref[...], m_ref[...], v_ref[...], g)

    lay = pl.BlockSpec((None, tr, cols), lambda l, i, c_ref: (l, i, 0))
    one = pl.BlockSpec((tr, cols), lambda l, i, c_ref: (i, 0))
    shp = jax.ShapeDtypeStruct(w.shape, F32)
    return pl.pallas_call(
        body, name=name,
        grid_spec=pltpu.PrefetchScalarGridSpec(num_scalar_prefetch=1, grid=(2, rows // tr),
                                               in_specs=[lay, lay, lay, one, one], out_specs=[lay] * 4),
        out_shape=[shp] * 4, compiler_params=_cp(("parallel", "parallel")),
    )(core, w, m, v, g_mine, g_theirs)


def _sum_slots(buf, name, tr=256):
    n, rows, cols = buf.shape
    tr = _tile(rows, tr, 8)

    def body(b_ref, o_ref):
        acc = b_ref[0].astype(F32)
        for k in range(1, n):
            acc = acc + b_ref[k].astype(F32)
        o_ref[...] = acc

    return pl.pallas_call(
        body, name=name, grid=(rows // tr,),
        in_specs=[pl.BlockSpec((n, tr, cols), lambda i: (0, i, 0))],
        out_specs=pl.BlockSpec((tr, cols), lambda i: (i, 0)),
        out_shape=jax.ShapeDtypeStruct((rows, cols), F32), compiler_params=_cp(("parallel",)),
    )(buf)


_ANY = pl.BlockSpec(memory_space=pl.ANY)


def _exchange_chips(src, per_dest, name):
    rows, cols = src.shape[-2:]

    def body(in_ref, out_ref, send_sems, recv_sems, local_sem):
        x, y, c = lax.axis_index("x"), lax.axis_index("y"), lax.axis_index("c")
        me = 2 * x + y
        chips = [(1 - x, y), (x, 1 - y), (1 - x, 1 - y)]

        def block(j):
            return in_ref.at[j] if per_dest else in_ref

        mine = pltpu.make_async_copy(block(me), out_ref.at[me], local_sem)
        mine.start()
        sends = []
        for k, (px, py) in enumerate(chips):
            cp = pltpu.make_async_remote_copy(
                src_ref=block(2 * px + py), dst_ref=out_ref.at[me], send_sem=send_sems.at[k],
                recv_sem=recv_sems.at[k], device_id=(px, py, c), device_id_type=MESH)
            cp.start()
            sends.append(cp)
        for k, (px, py) in enumerate(chips):
            pltpu.make_async_remote_copy(
                src_ref=block(me), dst_ref=out_ref.at[2 * px + py], send_sem=send_sems.at[k],
                recv_sem=recv_sems.at[k], device_id=(px, py, c), device_id_type=MESH).wait_recv()
        for cp in sends:
            cp.wait_send()
        mine.wait()

    return pl.pallas_call(
        body, name=name, in_specs=[_ANY], out_specs=_ANY,
        out_shape=jax.ShapeDtypeStruct((N_CHIPS, rows, cols), src.dtype),
        scratch_shapes=[pltpu.SemaphoreType.DMA((3,)), pltpu.SemaphoreType.DMA((3,)), pltpu.SemaphoreType.DMA(())],
    )(src)


def _shard_window(ref, kind, j, lead=()):
    if kind == "slots":
        return ref.at[(j,) + lead]
    r, c = ref.shape[-2] // (N_CHIPS if kind == "rows" else 1), ref.shape[-1] // (N_CHIPS if kind == "cols" else 1)
    full = tuple(slice(None) for _ in range(len(ref.shape) - 2 - len(lead)))
    if kind == "rows":
        return ref.at[lead + full + (pl.ds(pl.multiple_of(j * r, 16), r), slice(None))]
    return ref.at[lead + full + (slice(None), pl.ds(pl.multiple_of(j * c, LANES), c))]


def _gather_kind(shard, axis):
    if axis == 1:
        return "rows"
    return "cols" if shard.shape[2] % LANES == 0 else "slots"


def _gather_weights(shards, axes, name):
    kinds = ["rows" if ax == 1 else "slots" for ax in axes]
    nw = len(shards)

    def out_shape(s, kind):
        d, r, c = s.shape
        shp = {"rows": (d, N_CHIPS * r, c), "cols": (d, r, N_CHIPS * c), "slots": (N_CHIPS, d, r, c)}[kind]
        return jax.ShapeDtypeStruct(shp, s.dtype)

    assert all(s.shape[0] == 2 for s in shards)

    def body(*refs):
        ins, outs = refs[:nw], refs[nw:2 * nw]
        ici_send, ici_recv, d2d_send, d2d_recv, local_sems = refs[2 * nw:]
        x, y, c = lax.axis_index("x"), lax.axis_index("y"), lax.axis_index("c")
        me = 2 * x + y
        chips = [(1 - x, y), (x, 1 - y), (1 - x, 1 - y)]
        sends = []
        for i in range(nw):
            if kinds[i] == "rows":
                own = pltpu.make_async_copy(ins[i], _shard_window(outs[i], kinds[i], me), local_sems.at[i])
                own.start()
                sends.append((own, False))
        for i in range(nw):
            for k, (px, py) in enumerate(chips):
                cp = pltpu.make_async_remote_copy(
                    src_ref=ins[i].at[c], dst_ref=_shard_window(outs[i], kinds[i], me, (c,)),
                    send_sem=ici_send.at[3 * i + k], recv_sem=ici_recv.at[3 * i + k],
                    device_id=(px, py, c), device_id_type=MESH)
                cp.start()
                sends.append((cp, True))
        for i in range(nw):
            for k, (px, py) in enumerate(chips):
                win = _shard_window(outs[i], kinds[i], 2 * px + py, (c,))
                pltpu.make_async_remote_copy(
                    src_ref=ins[i].at[c], dst_ref=win, send_sem=ici_send.at[3 * i + k], recv_sem=ici_recv.at[3 * i + k],
                    device_id=(px, py, c), device_id_type=MESH).wait_recv()
                fw = pltpu.make_async_remote_copy(
                    src_ref=win, dst_ref=win, send_sem=d2d_send.at[3 * i + k], recv_sem=d2d_recv.at[3 * i + k],
                    device_id=(x, y, 1 - c), device_id_type=MESH)
                fw.start()
                sends.append((fw, True))
        for i in range(nw):
            for k, (px, py) in enumerate(chips):
                win = _shard_window(outs[i], kinds[i], 2 * px + py, (1 - c,))
                pltpu.make_async_remote_copy(
                    src_ref=win, dst_ref=win, send_sem=d2d_send.at[3 * i + k], recv_sem=d2d_recv.at[3 * i + k],
                    device_id=(x, y, 1 - c), device_id_type=MESH).wait_recv()
        for cp, remote in sends:
            cp.wait_send() if remote else cp.wait()

    outs = pl.pallas_call(
        body, name=name, in_specs=[_ANY] * nw, out_specs=[_ANY] * nw,
        out_shape=[out_shape(s, k) for s, k in zip(shards, kinds)],
        scratch_shapes=[pltpu.SemaphoreType.DMA((3 * nw,)), pltpu.SemaphoreType.DMA((3 * nw,)),
                        pltpu.SemaphoreType.DMA((3 * nw,)), pltpu.SemaphoreType.DMA((3 * nw,)),
                        pltpu.SemaphoreType.DMA((nw,))],
    )(*shards)
    me = 2 * lax.axis_index("x") + lax.axis_index("y")
    return [jnp.concatenate([jnp.where(me == j, s, o[j]) for j in range(N_CHIPS)], axis=2) if k == "slots" else o
            for s, o, k in zip(shards, outs, kinds)]


def _swap_other_layer(gst, name):
    nv = len(gst)

    def body(*refs):
        ins, outs, send_sems, recv_sems = refs[:nv], refs[nv:2 * nv], refs[2 * nv], refs[2 * nv + 1]
        x, y, c = lax.axis_index("x"), lax.axis_index("y"), lax.axis_index("c")
        cps = [pltpu.make_async_remote_copy(src_ref=ins[i].at[1 - c], dst_ref=outs[i], send_sem=send_sems.at[i],
                                            recv_sem=recv_sems.at[i], device_id=(x, y, 1 - c), device_id_type=MESH)
               for i in range(nv)]
        for cp in cps:
            cp.start()
        for cp in cps:
            cp.wait()

    return pl.pallas_call(
        body, name=name, in_specs=[_ANY] * nv, out_specs=[_ANY] * nv,
        out_shape=[jax.ShapeDtypeStruct(v.shape[1:], v.dtype) for v in gst],
        scratch_shapes=[pltpu.SemaphoreType.DMA((nv,)), pltpu.SemaphoreType.DMA((nv,))],
    )(*gst)


def _pair_sum(g, other, core, name, tr=256):
    _, rows, cols = g.shape
    tr = _tile(rows, tr, 16)

    def body(c_ref, g_ref, o_ref, s_ref):
        s_ref[...] = (g_ref[...].astype(F32) + o_ref[...].astype(F32)).astype(s_ref.dtype)

    blk = pl.BlockSpec((tr, cols), lambda i, c_ref: (i, 0))
    return pl.pallas_call(
        body, name=name,
        grid_spec=pltpu.PrefetchScalarGridSpec(
            num_scalar_prefetch=1, grid=(rows // tr,),
            in_specs=[pl.BlockSpec((None, tr, cols), lambda i, c_ref: (c_ref[0], i, 0)), blk], out_specs=blk),
        out_shape=jax.ShapeDtypeStruct((rows, cols), g.dtype), compiler_params=_cp(("parallel",)),
    )(core, g, other)


def _scatter_layer(parts, kinds, name):
    nw = len(parts)

    def shard_shape(g, kind):
        return g.shape[-2] // (N_CHIPS if kind == "rows" else 1), g.shape[-1] // (N_CHIPS if kind == "cols" else 1)

    def body(*refs):
        ins, outs = refs[:nw], refs[nw:2 * nw]
        send_sems, recv_sems, local_sems = refs[2 * nw:]
        x, y, c = lax.axis_index("x"), lax.axis_index("y"), lax.axis_index("c")
        me = 2 * x + y
        chips = [(1 - x, y), (x, 1 - y), (1 - x, 1 - y)]
        sends = []
        for i in range(nw):
            own = pltpu.make_async_copy(_shard_window(ins[i], kinds[i], me), outs[i].at[me], local_sems.at[i])
            own.start()
            sends.append((own, False))
            for k, (px, py) in enumerate(chips):
                cp = pltpu.make_async_remote_copy(
                    src_ref=_shard_window(ins[i], kinds[i], 2 * px + py), dst_ref=outs[i].at[me],
                    send_sem=send_sems.at[3 * i + k], recv_sem=recv_sems.at[3 * i + k],
                    device_id=(px, py, c), device_id_type=MESH)
                cp.start()
                sends.append((cp, True))
        for i in range(nw):
            for k, (px, py) in enumerate(chips):
                pltpu.make_async_remote_copy(
                    src_ref=_shard_window(ins[i], kinds[i], me), dst_ref=outs[i].at[2 * px + py],
                    send_sem=send_sems.at[3 * i + k], recv_sem=recv_sems.at[3 * i + k],
                    device_id=(px, py, c), device_id_type=MESH).wait_recv()
        for cp, remote in sends:
            cp.wait_send() if remote else cp.wait()

    return pl.pallas_call(
        body, name=name, in_specs=[_ANY] * nw, out_specs=[_ANY] * nw,
        out_shape=[jax.ShapeDtypeStruct((N_CHIPS,) + shard_shape(g, k), g.dtype) for g, k in zip(parts, kinds)],
        scratch_shapes=[pltpu.SemaphoreType.DMA((3 * nw,)), pltpu.SemaphoreType.DMA((3 * nw,)),
                        pltpu.SemaphoreType.DMA((nw,))],
    )(*parts)


def _sibling_swap(vs, name):
    nv = len(vs)

    def body(*refs):
        ins, outs, send_sems, recv_sems = refs[:nv], refs[nv:2 * nv], refs[2 * nv], refs[2 * nv + 1]
        x, y, c = lax.axis_index("x"), lax.axis_index("y"), lax.axis_index("c")
        cps = [pltpu.make_async_remote_copy(src_ref=ins[i], dst_ref=outs[i], send_sem=send_sems.at[i],
                                            recv_sem=recv_sems.at[i], device_id=(x, y, 1 - c), device_id_type=MESH)
               for i in range(nv)]
        for cp in cps:
            cp.start()
        for cp in cps:
            cp.wait()

    return pl.pallas_call(
        body, name=name, in_specs=[_ANY] * nv, out_specs=[_ANY] * nv,
        out_shape=[jax.ShapeDtypeStruct(v.shape, v.dtype) for v in vs],
        scratch_shapes=[pltpu.SemaphoreType.DMA((nv,)), pltpu.SemaphoreType.DMA((nv,))],
    )(*vs)


def _allgather_all(v, name):
    rows, cols = v.shape

    def body(in_ref, out_ref, send_sems, recv_sems, local_sem):
        x, y, c = lax.axis_index("x"), lax.axis_index("y"), lax.axis_index("c")
        me = 4 * x + 2 * y + c
        peers = []
        for k in range(1, N_DEV):
            peers.append(((1 - x) if k & 4 else x, (1 - y) if k & 2 else y, (1 - c) if k & 1 else c))
        mine = pltpu.make_async_copy(in_ref, out_ref.at[me], local_sem)
        mine.start()
        sends = []
        for k, peer in enumerate(peers):
            cp = pltpu.make_async_remote_copy(src_ref=in_ref, dst_ref=out_ref.at[me], send_sem=send_sems.at[k],
                                              recv_sem=recv_sems.at[k], device_id=peer, device_id_type=MESH)
            cp.start()
            sends.append(cp)
        for k, (px, py, pc) in enumerate(peers):
            pltpu.make_async_remote_copy(src_ref=in_ref, dst_ref=out_ref.at[4 * px + 2 * py + pc],
                                         send_sem=send_sems.at[k], recv_sem=recv_sems.at[k],
                                         device_id=(px, py, pc), device_id_type=MESH).wait_recv()
        for cp in sends:
            cp.wait_send()
        mine.wait()

    return pl.pallas_call(
        body, name=name, in_specs=[_ANY], out_specs=_ANY,
        out_shape=jax.ShapeDtypeStruct((N_DEV, rows, cols), v.dtype),
        scratch_shapes=[pltpu.SemaphoreType.DMA((N_DEV - 1,)), pltpu.SemaphoreType.DMA((N_DEV - 1,)),
                        pltpu.SemaphoreType.DMA(())],
    )(v)


def _pack(arrs, dtype, width, row_mult):
    flat = jnp.concatenate([a.reshape(-1).astype(dtype) for a in arrs])
    unit = width * row_mult
    total = -(-flat.shape[0] // unit) * unit
    return jnp.pad(flat, (0, total - flat.shape[0])).reshape(-1, width)


def _unpack(buf, shapes):
    flat = buf.reshape(-1)
    out, off = [], 0
    for shp in shapes:
        n = 1
        for d in shp:
            n *= d
        out.append(flat[off:off + n].reshape(shp))
        off += n
    return out


class _Dims:
    pass


def _dims(x, ssd_dt_bias, ssd_norm_g, ssd_conv_b, ffn_conv_b):
    dm = _Dims()
    dm.Bl, dm.S, dm.D = x.shape
    dm.H, dm.DI, dm.CD = ssd_dt_bias.shape[-1], ssd_norm_g.shape[-1], ssd_conv_b.shape[-1]
    dm.N = (dm.CD - dm.DI) // (2 * N_GROUPS)
    dm.R = dm.H // N_GROUPS
    dm.DFF = ffn_conv_b.shape[-1] // 2
    D = dm.D
    dm.OB, dm.OC, dm.OH, dm.OX = 0, D, 2 * D, 3 * D
    dm.OZ = dm.OX + dm.CD
    dm.OG = -(-(dm.OZ + dm.DI) // (2 * D)) * (2 * D)
    dm.GPAD = dm.OG - (dm.OZ + dm.DI)
    dm.ODT = dm.OG + 2 * D
    dm.NP = dm.ODT + LANES
    assert dm.OZ % (dm.R * HEAD_DIM) == 0
    assert dm.DI // dm.H == HEAD_DIM and dm.N == LANES and dm.R % 2 == 0 and dm.S % CHUNK == 0 and dm.H <= LANES
    return dm


def _group_xbc(v, dm):
    rp, n = dm.R * HEAD_DIM, dm.N
    parts = []
    for g in range(N_GROUPS):
        parts += [v[..., g * rp:(g + 1) * rp], v[..., dm.DI + g * n:dm.DI + (g + 1) * n],
                  v[..., dm.DI + (N_GROUPS + g) * n:dm.DI + (N_GROUPS + g + 1) * n]]
    return jnp.concatenate(parts, axis=-1)


def _ungroup_xbc(v, dm):
    rp, n = dm.R * HEAD_DIM, dm.N
    gw = rp + 2 * n
    xs = [v[..., g * gw:g * gw + rp] for g in range(N_GROUPS)]
    bs = [v[..., g * gw + rp:g * gw + rp + n] for g in range(N_GROUPS)]
    cs = [v[..., g * gw + rp + n:(g + 1) * gw] for g in range(N_GROUPS)]
    return jnp.concatenate(xs + bs + cs, axis=-1)


def _permute_w_in(w, dm):
    o, sc = dm.DI + dm.CD, dm.DI + dm.CD + dm.H
    zeros = lambda n: jnp.zeros((w.shape[0], n), w.dtype)
    return jnp.concatenate([w[:, sc:sc + 3 * dm.D], _group_xbc(w[:, dm.DI:o], dm), w[:, :dm.DI], zeros(dm.GPAD),
                            w[:, sc + 3 * dm.D:], w[:, o:o + dm.H], zeros(LANES - dm.H)], axis=1)


def _unpermute_w_in(dw, dm):
    return jnp.concatenate([dw[..., dm.OZ:dm.OZ + dm.DI], _ungroup_xbc(dw[..., dm.OX:dm.OX + dm.CD], dm),
                            dw[..., dm.ODT:dm.ODT + dm.H], dw[..., :3 * dm.D], dw[..., dm.OG:dm.OG + 2 * dm.D]], axis=-1)


def _lane_pad(v):
    return jnp.pad(v.reshape(1, -1).astype(F32), ((0, 0), (0, LANES - v.shape[-1])))


def _pad8(w):
    return jnp.pad(w.astype(F32), ((0, 8 - w.shape[0]), (0, 0)))


def _head_select(dm):
    j = jnp.arange(LANES)[None, :, None]
    r = jnp.arange(LANES)[None, None, :]
    g = jnp.arange(N_GROUPS)[:, None, None]
    sel = ((j == dm.R * g + r) & (r < dm.R)).astype(F32)
    selcat = jnp.transpose(sel, (1, 0, 2)).reshape(LANES, N_GROUPS * LANES)
    selbig = jnp.transpose(sel[:, :, :8], (0, 2, 1)).reshape(N_GROUPS * 8, LANES)
    return selcat, selbig


def _mix_fwd(dm, h, w, sp, sel, tag):
    Bl, S, D = dm.Bl, dm.S, dm.D
    T = Bl * S
    proj = _matmul(h.reshape(T, D), w["w_in_p"], "NN", F32, tag + "_in_proj", tm=2048, tn=1152).reshape(Bl, S, dm.NP)
    xbc_a = _ssd_conv_fwd(proj, sp["ssd_conv_w8"], sp["ssd_conv_b"], dm.OX, dm.CD, tag + "_ssd_conv")
    dtg, acs, acs_t = _ssd_prep(proj, sp["dt_bias"], sp["a_log"], sel[0], sel[1], dm, tag + "_ssd_prep")
    y, yn, hprev = _scan_fwd(xbc_a, proj, dtg, acs, acs_t, sp["dsk_exp"], sp["ssd_norm_g"], dm, tag + "_ssd_scan")
    y_ssd = _matmul(yn.reshape(T, dm.DI), w["w_ssd_out"], "NN", F32, tag + "_ssd_out", tk=2048).reshape(Bl, S, D)
    s = _sc_conv_fwd(proj, sp["sc_conv_w8"], (dm.OB, dm.OC, dm.OH), D, tag + "_sc_conv")
    y_sc = _matmul(s.reshape(T, D), w["w_sc_out"], "NN", F32, tag + "_sc_out", tk=1024).reshape(Bl, S, D)
    mixin = _merge_fwd(proj, y_ssd, y_sc, dm.OG, D, tag + "_merge")
    mix = _matmul(mixin.reshape(T, D), w["w_o"], "NN", F32, tag + "_o", tk=1024).reshape(Bl, S, D)
    return mix, (h, proj, xbc_a, y, yn, hprev, y_ssd, y_sc, s, mixin, dtg, acs, acs_t)


def _mix_bwd(dm, dmix, saved, w, sp, sel, tag, gbig, slab):
    Bl, S, D = dm.Bl, dm.S, dm.D
    T = Bl * S
    h, proj, xbc_a, y, yn, hprev, y_ssd, y_sc, s, mixin, dtg, acs, acs_t = saved
    dmix2 = dmix.reshape(T, D)
    g = {}
    gbig["w_o"] = _matmul(mixin.reshape(T, D), dmix2, "TN", WIRE_DTYPE, tag + "_dw_o", slab=(gbig.get("w_o"),) + slab)
    dmixin = _matmul(dmix2, w["w_o"], "NT", F32, tag + "_d_o", tk=1024).reshape(Bl, S, D)
    dproj0 = jnp.zeros((Bl, S, dm.NP), MXU_DTYPE) if dm.GPAD else None
    dy_ssd, dy_sc, dproj = _merge_bwd(dmixin, proj, y_ssd, y_sc, dm.OG, D, dproj0, tag + "_merge_bwd")
    gbig["w_sc_out"] = _matmul(s.reshape(T, D), dy_sc.reshape(T, D), "TN", WIRE_DTYPE, tag + "_dw_sc_out",
                               slab=(gbig.get("w_sc_out"),) + slab)
    ds = _matmul(dy_sc.reshape(T, D), w["w_sc_out"], "NT", F32, tag + "_d_sc_out", tk=1024).reshape(Bl, S, D)
    dproj, dscw = _sc_conv_bwd(ds, proj, sp["sc_conv_w8"], (dm.OB, dm.OC, dm.OH), D, dproj, tag + "_sc_conv_bwd")
    g["sc_conv_w"] = dscw[:3]
    gbig["w_ssd_out"] = _matmul(yn.reshape(T, dm.DI), dy_ssd.reshape(T, D), "TN", WIRE_DTYPE, tag + "_dw_ssd_out",
                                slab=(gbig.get("w_ssd_out"),) + slab)
    dyn = _matmul(dy_ssd.reshape(T, D), w["w_ssd_out"], "NT", F32, tag + "_d_ssd_out", tk=1024).reshape(Bl, S, dm.DI)
    dproj, dxa, d_a, ddt, ddsk, dng = _scan_bwd(dyn, y, xbc_a, proj, hprev, dtg, acs, acs_t, sp["dsk_exp"],
                                                sp["ssd_norm_g"], dproj, dm, tag + "_ssd_scan_bwd")
    dproj, dpar = _ssd_post(d_a, ddt, dtg, proj, sp["dt_bias"], sp["a_log"], sel[0], dproj, dm, tag + "_ssd_post")
    g["ssd_dt_bias"], g["ssd_a_log"] = dpar[0, :dm.H], dpar[1, :dm.H]
    g["ssd_d"] = jnp.sum(ddsk.reshape(dm.H, HEAD_DIM), axis=-1)
    g["ssd_norm_g"] = dng[0]
    dproj, dcw, dcb = _ssd_conv_bwd(dxa, proj, sp["ssd_conv_w8"], sp["ssd_conv_b"], dm.OX, dm.CD, dproj,
                                    tag + "_ssd_conv_bwd")
    g["ssd_conv_w"], g["ssd_conv_b"] = _ungroup_xbc(dcw[:4], dm), _ungroup_xbc(dcb[0], dm)
    dproj = dproj.reshape(T, dm.NP)
    gbig["w_in_p"] = _matmul(h.reshape(T, D), dproj, "TN", WIRE_DTYPE, tag + "_dw_in", tn=1728,
                             slab=(gbig.get("w_in_p"),) + slab)
    dh = _matmul(dproj, w["w_in_p"], "NT", F32, tag + "_d_in", tm=2048, tk=1152).reshape(Bl, S, D)
    return dh, g


def _ffn_fwd(dm, h, w, sp, tag):
    Bl, S, D = dm.Bl, dm.S, dm.D
    T = Bl * S
    up = _matmul(h.reshape(T, D), w["w_up"], "NN", F32, tag + "_up", tm=2048, tn=1408).reshape(Bl, S, 2 * dm.DFF)
    a = _ffn_conv_fwd(up, sp["ffn_conv_w8"], sp["ffn_conv_b"], dm.DFF, tag + "_ffn_conv")
    f = _matmul(a.reshape(T, dm.DFF), w["w_down"], "NN", F32, tag + "_down", tk=2816).reshape(Bl, S, D)
    return f, (h, up, a)


def _ffn_bwd(dm, df, saved, w, sp, tag, gbig, slab):
    Bl, S, D = dm.Bl, dm.S, dm.D
    T = Bl * S
    h, up, a = saved
    df2 = df.reshape(T, D)
    g = {}
    gbig["w_down"] = _matmul(a.reshape(T, dm.DFF), df2, "TN", WIRE_DTYPE, tag + "_dw_down", tm=1408,
                             slab=(gbig.get("w_down"),) + slab)
    da = _matmul(df2, w["w_down"], "NT", F32, tag + "_d_down", tn=1408).reshape(Bl, S, dm.DFF)
    dup, dcw, dcb = _ffn_conv_bwd(da, up, sp["ffn_conv_w8"], sp["ffn_conv_b"], dm.DFF, tag + "_ffn_conv_bwd")
    g["ffn_conv_w"], g["ffn_conv_b"] = dcw[:3], dcb[0]
    dup = dup.reshape(T, 2 * dm.DFF)
    gbig["w_up"] = _matmul(h.reshape(T, D), dup, "TN", WIRE_DTYPE, tag + "_dw_up", tn=1408,
                           slab=(gbig.get("w_up"),) + slab)
    dh = _matmul(dup, w["w_up"], "NT", F32, tag + "_d_up", tm=2048, tk=1408).reshape(Bl, S, D)
    return dh, g


def _local_step(dm, x, c, target, wfull, small):
    Bl, S, D = dm.Bl, dm.S, dm.D
    depth = len(wfull)
    sel = _head_select(dm)
    c16 = jnp.pad(c.astype(F32), ((0, 16 - Bl), (0, 0)))
    sps, mods, acts = [], [], []
    for l in range(depth):
        sm = small[l]
        sps.append(dict(
            ssd_conv_w8=_pad8(_group_xbc(sm["ssd_conv_w"], dm)), ssd_conv_b=_group_xbc(sm["ssd_conv_b"], dm),
            dt_bias=_lane_pad(sm["ssd_dt_bias"]),
            a_log=_lane_pad(sm["ssd_a_log"]), dsk_exp=jnp.repeat(sm["ssd_d"].astype(F32), HEAD_DIM).reshape(1, dm.DI),
            ssd_norm_g=sm["ssd_norm_g"],
            sc_conv_w8=_pad8(sm["sc_conv_w"]), ffn_conv_w8=_pad8(sm["ffn_conv_w"]), ffn_conv_b=sm["ffn_conv_b"]))
        act, mod = _ada_fwd(c16, wfull[l]["ada_w"], sm["ada_b"], f"l{l}_ada")
        acts.append(act)
        mods.append(jnp.pad(mod[:Bl].reshape(Bl, 6, D), ((0, 0), (0, 2), (0, 0))))

    def sub(i):
        l, ffn = i // 2, i % 2
        sm = small[l]
        return dict(l=l, ffn=ffn, pre_g=sm["ffn_pre_g" if ffn else "mix_pre_g"],
                    post_g=sm["ffn_post_g" if ffn else "mix_post_g"], mod=mods[l], row=3 * ffn,
                    tag=f"l{l}_{'ffn' if ffn else 'mix'}")

    nsub = 2 * depth
    subs = [sub(i) for i in range(nsub)]
    xs, fs, saves = [x], [], []
    h = _norm_mod_fwd(x, subs[0]["pre_g"], subs[0]["mod"], subs[0]["row"], "l0_mix_pre_norm")
    for i, sb in enumerate(subs):
        l = sb["l"]
        if sb["ffn"]:
            f, sv = _ffn_fwd(dm, h, wfull[l], sps[l], sb["tag"])
        else:
            f, sv = _mix_fwd(dm, h, wfull[l], sps[l], sel, sb["tag"])
        nxt = None
        if i + 1 < nsub:
            nb = subs[i + 1]
            nxt = (nb["pre_g"], nb["mod"], nb["row"])
        xn, h = _post_norm_fwd(xs[-1], f, sb["post_g"], sb["mod"], sb["row"] + 2, nxt, sb["tag"] + "_post_norm")
        xs.append(xn)
        fs.append(f)
        saves.append(sv)

    dy, loss = _loss_fwd_bwd(xs[-1], target, "loss")

    grads = [dict() for _ in range(depth)]
    gbig = {}
    dmod = [[None] * 6 for _ in range(depth)]
    dx, dh = dy, None
    for i in reversed(range(nsub)):
        sb = subs[i]
        l = sb["l"]
        nxt = None
        if i + 1 < nsub:
            nb = subs[i + 1]
            nxt = (dh, xs[i + 1], nb["pre_g"], nb["mod"], nb["row"])
        dx, df, pb, shg = _norm_bwd(dx, nxt, (fs[i], sb["post_g"], sb["mod"], sb["row"] + 2), sb["tag"] + "_post_norm_bwd")
        if nxt is not None:
            nb = subs[i + 1]
            dmod[nb["l"]][nb["row"]], dmod[nb["l"]][nb["row"] + 1] = pb[:, 0], pb[:, 1]
            grads[nb["l"]]["ffn_pre_g" if nb["ffn"] else "mix_pre_g"] = shg[0]
        dmod[l][sb["row"] + 2] = pb[:, 2]
        grads[l]["ffn_post_g" if sb["ffn"] else "mix_post_g"] = shg[1]
        if sb["ffn"]:
            dh, g = _ffn_bwd(dm, df, saves[i], wfull[l], sps[l], sb["tag"], gbig, (l, depth))
        else:
            dh, g = _mix_bwd(dm, df, saves[i], wfull[l], sps[l], sel, sb["tag"], gbig, (l, depth))
        grads[l].update(g)
    sb = subs[0]
    grad_x, _, pb, shg = _norm_bwd(dx, (dh, xs[0], sb["pre_g"], sb["mod"], sb["row"]), None, "l0_mix_pre_norm_bwd")
    dmod[0][0], dmod[0][1] = pb[:, 0], pb[:, 1]
    grads[0]["mix_pre_g"] = shg[0]

    for l in range(depth):
        dm6 = jnp.concatenate(dmod[l], axis=-1)
        grads[l]["ada_b"] = jnp.sum(dm6, axis=0)
        dm16 = jnp.pad(dm6, ((0, 16 - Bl), (0, 0))).astype(MXU_DTYPE)
        gbig["ada_w"] = _matmul(acts[l], dm16, "TN", WIRE_DTYPE, f"l{l}_dw_ada", slab=(gbig.get("ada_w"), l, depth))
    return loss, grad_x, grads, gbig


_WEIGHTS = ("ada_w", "ada_b", "mix_pre_g", "mix_post_g", "w_in", "ssd_conv_w", "ssd_conv_b", "ssd_dt_bias",
            "ssd_a_log", "ssd_d", "ssd_norm_g", "w_ssd_out", "sc_conv_w", "w_sc_out", "w_o", "ffn_pre_g",
            "ffn_post_g", "w_up", "ffn_conv_w", "ffn_conv_b", "w_down")
_INPUTS = ("x", "c") + _WEIGHTS + ("loss_target",) + tuple("m_" + n for n in _WEIGHTS) + tuple("v_" + n for n in _WEIGHTS)
_BIG = {"ada_w": 2, "w_in": 2, "w_ssd_out": 1, "w_sc_out": 1, "w_o": 1, "w_up": 2, "w_down": 1}
_CONV = ("ssd_conv_w", "sc_conv_w", "ffn_conv_w")
_SMALL = tuple(n for n in _WEIGHTS if n not in _BIG)


def _step(a):
    x, c, target = a["x"], a["c"], a["loss_target"]
    depth = a["ada_w"].shape[0]
    dm = _dims(x, a["ssd_dt_bias"], a["ssd_norm_g"], a["ssd_conv_b"], a["ffn_conv_b"])
    chip = 2 * lax.axis_index("x") + lax.axis_index("y")

    shards = [a[n].astype(WIRE_DTYPE) for n in _BIG]
    axes = list(_BIG.values())
    kinds = [_gather_kind(s, ax) for s, ax in zip(shards, axes)]
    full = {n: w.astype(MXU_DTYPE) for n, w in zip(_BIG, _gather_weights(shards, axes, "gather_weights"))}
    conv_shapes = [a[n].shape for n in _CONV]
    gotc = _exchange_chips(_pack([a[n] for n in _CONV], F32, LANES, 8), False, "gather_conv_weights")
    piecesc = [_unpack(gotc[j], conv_shapes) for j in range(N_CHIPS)]
    fullc = {n: jnp.concatenate([piecesc[j][i] for j in range(N_CHIPS)], axis=2) for i, n in enumerate(_CONV)}

    wfull, small = [], []
    for l in range(depth):
        wf = {n: full[n][l] for n in _BIG if n != "w_in"}
        wf["w_in_p"] = _permute_w_in(full["w_in"][l], dm)
        wfull.append(wf)
        small.append({n: (fullc[n][l] if n in _CONV else a[n][l]) for n in _SMALL})

    loss_part, grad_x, grads, gbig = _local_step(dm, x, c, target, wfull, small)

    core = lax.axis_index("c").astype(jnp.int32).reshape(1)
    gst = []
    for n, kind in zip(_BIG, kinds):
        g = _unpermute_w_in(gbig["w_in_p"], dm) if n == "w_in" else gbig[n]
        if kind == "slots":
            g = jnp.moveaxis(g.reshape(depth, g.shape[1], N_CHIPS, g.shape[2] // N_CHIPS), 2, 1)
            g = g.reshape(depth, -1, g.shape[-1])
        gst.append(g)
    other = _swap_other_layer(gst, "swap_layer_grads")
    parts = [_pair_sum(g, o, core, "pair_sum_" + n) for g, o, n in zip(gst, other, _BIG)]
    parts = [p.reshape(N_CHIPS, -1, p.shape[-1]) if k == "slots" else p for p, k in zip(parts, kinds)]
    got = _scatter_layer(parts, kinds, "scatter_grads")
    mine = [_sum_slots(g, "sum_chip_grads_" + n) for g, n in zip(got, _BIG)]
    theirs = _sibling_swap(mine, "swap_core_grads")

    out = {}
    for i, n in enumerate(_BIG):
        out[n] = _adamw_layers(a[n], a["m_" + n], a["v_" + n], mine[i], theirs[i], core, "adamw_" + n)

    gsmall = [jnp.stack([grads[l][n] for l in range(depth)]) for n in _SMALL]
    small_shapes = [g.shape for g in gsmall]
    summed = _sum_slots(_allgather_all(_pack(gsmall, F32, LANES, 8), "gather_small_grads"), "sum_small_grads")
    gs = dict(zip(_SMALL, _unpack(summed, small_shapes)))
    for n in _CONV:
        wcols = a[n].shape[2]
        gs[n] = lax.dynamic_slice_in_dim(gs[n], chip * wcols, wcols, axis=2)
    local_shapes = [a[n].shape for n in _SMALL]
    res = _adamw(_pack([a[n] for n in _SMALL], F32, LANES, 8), _pack([a["m_" + n] for n in _SMALL], F32, LANES, 8),
                 _pack([a["v_" + n] for n in _SMALL], F32, LANES, 8), _pack([gs[n] for n in _SMALL], F32, LANES, 8),
                 "adamw_small")
    res = [_unpack(r, local_shapes) for r in res]
    for i, n in enumerate(_SMALL):
        out[n] = [r[i] for r in res]

    loss = lax.psum(loss_part, ("x", "y", "c"))
    return (loss, grad_x) + tuple(out[n][k] for k in range(4) for n in _WEIGHTS)


def kernel(x, c, ada_w, ada_b, mix_pre_g, mix_post_g, w_in, ssd_conv_w, ssd_conv_b, ssd_dt_bias, ssd_a_log, ssd_d, ssd_norm_g, w_ssd_out, sc_conv_w, w_sc_out, w_o, ffn_pre_g, ffn_post_g, w_up, ffn_conv_w, ffn_conv_b, w_down, loss_target, m_ada_w, m_ada_b, m_mix_pre_g, m_mix_post_g, m_w_in, m_ssd_conv_w, m_ssd_conv_b, m_ssd_dt_bias, m_ssd_a_log, m_ssd_d, m_ssd_norm_g, m_w_ssd_out, m_sc_conv_w, m_w_sc_out, m_w_o, m_ffn_pre_g, m_ffn_post_g, m_w_up, m_ffn_conv_w, m_ffn_conv_b, m_w_down, v_ada_w, v_ada_b, v_mix_pre_g, v_mix_post_g, v_w_in, v_ssd_conv_w, v_ssd_conv_b, v_ssd_dt_bias, v_ssd_a_log, v_ssd_d, v_ssd_norm_g, v_w_ssd_out, v_sc_conv_w, v_w_sc_out, v_w_o, v_ffn_pre_g, v_ffn_post_g, v_w_up, v_ffn_conv_w, v_ffn_conv_b, v_w_down):
    return _step(dict(zip(_INPUTS, (
        x, c, ada_w, ada_b, mix_pre_g, mix_post_g, w_in, ssd_conv_w, ssd_conv_b, ssd_dt_bias, ssd_a_log, ssd_d, ssd_norm_g, w_ssd_out, sc_conv_w, w_sc_out, w_o, ffn_pre_g, ffn_post_g, w_up, ffn_conv_w, ffn_conv_b, w_down, loss_target, m_ada_w, m_ada_b, m_mix_pre_g, m_mix_post_g, m_w_in, m_ssd_conv_w, m_ssd_conv_b, m_ssd_dt_bias, m_ssd_a_log, m_ssd_d, m_ssd_norm_g, m_w_ssd_out, m_sc_conv_w, m_w_sc_out, m_w_o, m_ffn_pre_g, m_ffn_post_g, m_w_up, m_ffn_conv_w, m_ffn_conv_b, m_w_down, v_ada_w, v_ada_b, v_mix_pre_g, v_mix_post_g, v_w_in, v_ssd_conv_w, v_ssd_conv_b, v_ssd_dt_bias, v_ssd_a_log, v_ssd_d, v_ssd_norm_g, v_w_ssd_out, v_sc_conv_w, v_w_sc_out, v_w_o, v_ffn_pre_g, v_ffn_post_g, v_w_up, v_ffn_conv_w, v_ffn_conv_b, v_w_down))))
```

```python
import math

import jax
import jax.numpy as jnp
from jax import lax
from jax.experimental import pallas as pl
from jax.experimental.pallas import tpu as pltpu

F32 = jnp.float32
MXU_DTYPE = jnp.bfloat16
WIRE_DTYPE = jnp.bfloat16
EPS = 1e-6
N_GROUPS = 4
CHUNK = 128
HEAD_DIM = 64
LANES = 128
HALO = 8
N_CHIPS = 4
N_DEV = 8
VMEM_LIMIT = 56 * 1024 * 1024
ADAM_LR, ADAM_B1, ADAM_B2, ADAM_EPS, ADAM_WD, ADAM_STEP = 0.001, 0.9, 0.999, 1e-08, 0.01, 10
MESH = pl.DeviceIdType.MESH

NN = (((1,), (0,)), ((), ()))
NT = (((1,), (1,)), ((), ()))
TN = (((0,), (0,)), ((), ()))


def _dg(a, b, dn, precision=None):
    return lax.dot_general(a, b, dn, precision=precision, preferred_element_type=F32)


def _tile(dim, pref, mult=LANES):
    t = (min(pref, dim) // mult) * mult
    while t >= mult:
        if dim % t == 0:
            return t
        t -= mult
    return dim


def _cp(sem):
    return pltpu.CompilerParams(dimension_semantics=sem, vmem_limit_bytes=VMEM_LIMIT)


def _sigmoid(x):
    return 1.0 / (1.0 + jnp.exp(-x))


def _softplus(x):
    return jnp.maximum(x, 0.0) + jnp.log1p(jnp.exp(-jnp.abs(x)))


def _matmul(a, b, mode, out_dtype, name, tm=1024, tn=1024, tk=1024, slab=None):
    if mode == "NN":
        (M, K), N = a.shape, b.shape[1]
    elif mode == "NT":
        (M, K), N = a.shape, b.shape[0]
    else:
        (K, M), N = a.shape, b.shape[1]
    tm, tn, tk = _tile(M, tm), _tile(N, tn), _tile(K, tk)
    nk = K // tk
    dn = {"NN": NN, "NT": NT, "TN": TN}[mode]
    carry = slab is not None and slab[0] is not None

    def body_one(a_ref, b_ref, *rest):
        o_ref = rest[-1]
        o_ref[...] = _dg(a_ref[...], b_ref[...], dn).astype(o_ref.dtype)

    def body_acc(a_ref, b_ref, *rest):
        o_ref, acc_ref = rest[-2:]
        k = pl.program_id(2)

        @pl.when(k == 0)
        def _():
            acc_ref[...] = jnp.zeros_like(acc_ref)

        acc_ref[...] += _dg(a_ref[...], b_ref[...], dn)

        @pl.when(k == nk - 1)
        def _():
            o_ref[...] = acc_ref[...].astype(o_ref.dtype)

    a_spec = (pl.BlockSpec((tk, tm), lambda i, j, k: (k, i)) if mode == "TN"
              else pl.BlockSpec((tm, tk), lambda i, j, k: (i, k)))
    b_spec = (pl.BlockSpec((tn, tk), lambda i, j, k: (j, k)) if mode == "NT"
              else pl.BlockSpec((tk, tn), lambda i, j, k: (k, j)))
    if slab is None:
        out_spec = pl.BlockSpec((tm, tn), lambda i, j, k: (i, j))
        out_shape = jax.ShapeDtypeStruct((M, N), out_dtype)
    else:
        layer = slab[1]
        out_spec = pl.BlockSpec((None, tm, tn), lambda i, j, k: (layer, i, j))
        out_shape = jax.ShapeDtypeStruct((slab[2], M, N), out_dtype)
    return pl.pallas_call(
        body_one if nk == 1 else body_acc, name=name, grid=(M // tm, N // tn, nk),
        in_specs=[a_spec, b_spec] + ([_ANY] if carry else []),
        out_specs=out_spec, out_shape=out_shape,
        input_output_aliases={2: 0} if carry else {},
        scratch_shapes=[] if nk == 1 else [pltpu.VMEM((tm, tn), F32)],
        compiler_params=_cp(("parallel", "parallel", "arbitrary")),
    )(*((a, b, slab[0]) if carry else (a, b)))


def _ada_fwd(c16, ada_w, ada_b, name):
    rows, D = c16.shape
    N6 = ada_w.shape[1]
    tn = _tile(N6, 1536)

    def body(c_ref, w_ref, b_ref, act_ref, mod_ref):
        c = c_ref[...]
        act = (c * _sigmoid(c)).astype(act_ref.dtype)
        act_ref[...] = act
        mod_ref[...] = _dg(act, w_ref[...], NN) + b_ref[...]

    return pl.pallas_call(
        body, name=name, grid=(N6 // tn,),
        in_specs=[pl.BlockSpec((rows, D), lambda j: (0, 0)),
                  pl.BlockSpec((D, tn), lambda j: (0, j)),
                  pl.BlockSpec((1, tn), lambda j: (0, j))],
        out_specs=[pl.BlockSpec((rows, D), lambda j: (0, 0)),
                   pl.BlockSpec((rows, tn), lambda j: (0, j))],
        out_shape=[jax.ShapeDtypeStruct((rows, D), MXU_DTYPE),
                   jax.ShapeDtypeStruct((rows, N6), F32)],
        compiler_params=_cp(("arbitrary",)),
    )(c16, ada_w, ada_b.reshape(1, N6))


def _norm_mod_rows(x, g, sc, sh):
    r = lax.rsqrt(jnp.mean(x * x, axis=-1, keepdims=True) + EPS)
    return ((x * r) * g) * (1.0 + sc) + sh


def _norm_mod_fwd(x, g, mod, row_sh, name, ts=512):
    Bl, S, D = x.shape
    ts = _tile(S, ts, 8)

    def body(x_ref, g_ref, mod_ref, h_ref):
        sh = mod_ref[row_sh:row_sh + 1, :]
        sc = mod_ref[row_sh + 1:row_sh + 2, :]
        h_ref[...] = _norm_mod_rows(x_ref[...], g_ref[...], sc, sh).astype(h_ref.dtype)

    tok = pl.BlockSpec((None, ts, D), lambda b, s: (b, s, 0))
    return pl.pallas_call(
        body, name=name, grid=(Bl, S // ts),
        in_specs=[tok, pl.BlockSpec((1, D), lambda b, s: (0, 0)),
                  pl.BlockSpec((None, 8, D), lambda b, s: (b, 0, 0))],
        out_specs=tok, out_shape=jax.ShapeDtypeStruct((Bl, S, D), MXU_DTYPE),
        compiler_params=_cp(("parallel", "parallel")),
    )(x, g.reshape(1, D), mod)


def _post_norm_fwd(xp, f, post_g, mod, row_gt, nxt, name, ts=512):
    Bl, S, D = xp.shape
    ts = _tile(S, ts, 8)
    has_next = nxt is not None

    def body(*refs):
        if has_next:
            xp_ref, f_ref, pg_ref, mod_ref, ng_ref, nmod_ref, x_ref, h_ref = refs
        else:
            xp_ref, f_ref, pg_ref, mod_ref, x_ref = refs
        f = f_ref[...]
        r = lax.rsqrt(jnp.mean(f * f, axis=-1, keepdims=True) + EPS)
        x = xp_ref[...] + mod_ref[row_gt:row_gt + 1, :] * ((f * r) * pg_ref[...])
        x_ref[...] = x
        if has_next:
            rs = nxt[2]
            h_ref[...] = _norm_mod_rows(x, ng_ref[...], nmod_ref[rs + 1:rs + 2, :], nmod_ref[rs:rs + 1, :]).astype(h_ref.dtype)

    tok = pl.BlockSpec((None, ts, D), lambda b, s: (b, s, 0))
    vec = pl.BlockSpec((1, D), lambda b, s: (0, 0))
    modspec = pl.BlockSpec((None, 8, D), lambda b, s: (b, 0, 0))
    ins = [xp, f, post_g.reshape(1, D), mod]
    in_specs = [tok, tok, vec, modspec]
    out_specs = [tok]
    out_shape = [jax.ShapeDtypeStruct((Bl, S, D), F32)]
    if has_next:
        ins += [nxt[0].reshape(1, D), nxt[1]]
        in_specs += [vec, modspec]
        out_specs += [tok]
        out_shape += [jax.ShapeDtypeStruct((Bl, S, D), MXU_DTYPE)]
    out = pl.pallas_call(
        body, name=name, grid=(Bl, S // ts), in_specs=in_specs, out_specs=out_specs, out_shape=out_shape,
        compiler_params=_cp(("parallel", "parallel")),
    )(*ins)
    return (out[0], out[1]) if has_next else (out[0], None)


def _loss_fwd_bwd(y, target, name, ts=512):
    Bl, S, D = y.shape
    ts = _tile(S, ts, 8)

    def body(y_ref, t_ref, dy_ref, l_ref):
        @pl.when((pl.program_id(0) == 0) & (pl.program_id(1) == 0))
        def _():
            l_ref[...] = jnp.zeros_like(l_ref)

        e = y_ref[...] - t_ref[...]
        dy_ref[...] = e * (1.0 / D)
        l_ref[...] += 0.5 * jnp.sum(jnp.mean(e * e, axis=-1, keepdims=True), axis=0, keepdims=True)

    tok = pl.BlockSpec((None, ts, D), lambda b, s: (b, s, 0))
    dy, l = pl.pallas_call(
        body, name=name, grid=(Bl, S // ts), in_specs=[tok, tok],
        out_specs=[tok, pl.BlockSpec((8, LANES), lambda b, s: (0, 0))],
        out_shape=[jax.ShapeDtypeStruct((Bl, S, D), F32), jax.ShapeDtypeStruct((8, LANES), F32)],
        compiler_params=_cp(("arbitrary", "arbitrary")),
    )(y, target)
    return dy, l[0, 0]


def _norm_bwd(dx_res, nxt, prv, name, ts=512):
    Bl, S, D = dx_res.shape
    ts = _tile(S, ts, 8)
    has_next, has_prev = nxt is not None, prv is not None

    def body(*refs):
        refs = list(refs)
        dxr_ref = refs.pop(0)
        if has_next:
            dh_ref, x_ref, g_ref, nmod_ref = refs[:4]
            refs = refs[4:]
        if has_prev:
            f_ref, pg_ref, pmod_ref = refs[:3]
            refs = refs[3:]
        dx_ref = refs.pop(0)
        if has_prev:
            df_ref = refs.pop(0)
        pb_ref, sh_ref = refs
        b, s = pl.program_id(0), pl.program_id(1)

        @pl.when(s == 0)
        def _():
            pb_ref[...] = jnp.zeros_like(pb_ref)

        @pl.when((b == 0) & (s == 0))
        def _():
            sh_ref[...] = jnp.zeros_like(sh_ref)

        dx = dxr_ref[...]
        if has_next:
            rs = nxt[4]
            x, dh, g = x_ref[...], dh_ref[...], g_ref[...]
            sc1 = 1.0 + nmod_ref[rs + 1:rs + 2, :]
            r = lax.rsqrt(jnp.mean(x * x, axis=-1, keepdims=True) + EPS)
            xn = x * r
            pb_ref[0:1, :] += jnp.sum(dh, axis=0, keepdims=True)
            pb_ref[1:2, :] += jnp.sum(dh * (xn * g), axis=0, keepdims=True)
            sh_ref[0:1, :] += jnp.sum(dh * sc1 * xn, axis=0, keepdims=True)
            dxn = dh * sc1 * g
            dx = dx + r * (dxn - xn * jnp.mean(dxn * xn, axis=-1, keepdims=True))
        dx_ref[...] = dx
        if has_prev:
            rg = prv[3]
            f, pg = f_ref[...], pg_ref[...]
            gt = pmod_ref[rg:rg + 1, :]
            r = lax.rsqrt(jnp.mean(f * f, axis=-1, keepdims=True) + EPS)
            fn = f * r
            pb_ref[2:3, :] += jnp.sum(dx * (fn * pg), axis=0, keepdims=True)
            drn = dx * gt
            sh_ref[1:2, :] += jnp.sum(drn * fn, axis=0, keepdims=True)
            dfn = drn * pg
            df_ref[...] = (r * (dfn - fn * jnp.mean(dfn * fn, axis=-1, keepdims=True))).astype(df_ref.dtype)

    tok = pl.BlockSpec((None, ts, D), lambda b, s: (b, s, 0))
    vec = pl.BlockSpec((1, D), lambda b, s: (0, 0))
    modspec = pl.BlockSpec((None, 8, D), lambda b, s: (b, 0, 0))
    ins, in_specs = [dx_res], [tok]
    if has_next:
        ins += [nxt[0], nxt[1], nxt[2].reshape(1, D), nxt[3]]
        in_specs += [tok, tok, vec, modspec]
    if has_prev:
        ins += [prv[0], prv[1].reshape(1, D), prv[2]]
        in_specs += [tok, vec, modspec]
    out_specs, out_shape = [tok], [jax.ShapeDtypeStruct((Bl, S, D), F32)]
    if has_prev:
        out_specs += [tok]
        out_shape += [jax.ShapeDtypeStruct((Bl, S, D), MXU_DTYPE)]
    out_specs += [modspec, pl.BlockSpec((8, D), lambda b, s: (0, 0))]
    out_shape += [jax.ShapeDtypeStruct((Bl, 8, D), F32), jax.ShapeDtypeStruct((8, D), F32)]
    out = pl.pallas_call(
        body, name=name, grid=(Bl, S // ts), in_specs=in_specs, out_specs=out_specs, out_shape=out_shape,
        compiler_params=_cp(("arbitrary", "arbitrary")),
    )(*ins)
    if has_prev:
        return out[0], out[1], out[2], out[3]
    return out[0], None, out[1], out[2]


def _shift_down(x, j):
    return x if j == 0 else pltpu.roll(x, j, axis=0)


def _shift_up(x, j):
    return x if j == 0 else pltpu.roll(x, x.shape[0] - j, axis=0)


def _conv(xall, w_ref, K):
    y = w_ref[K - 1:K, :] * xall
    for k in range(K - 1):
        y = y + w_ref[k:k + 1, :] * _shift_down(xall, K - 1 - k)
    return y


def _conv_t(dall, xall, w, K, rows):
    xt = xall[HALO:HALO + rows]
    y = w[K - 1:K, :] * dall[HALO:HALO + rows]
    gw = [None] * K
    gw[K - 1] = jnp.sum(dall[HALO:HALO + rows] * xt, axis=0, keepdims=True)
    for k in range(K - 1):
        sh = _shift_up(dall, K - 1 - k)[HALO:HALO + rows]
        y = y + w[k:k + 1, :] * sh
        gw[k] = jnp.sum(sh * xt, axis=0, keepdims=True)
    return y, gw


def _conv_t_rows(dall, w_ref, K):
    y = w_ref[K - 1:K, :] * dall
    for k in range(K - 1):
        y = y + w_ref[k:k + 1, :] * _shift_up(dall, K - 1 - k)
    return y


def _conv_wgrad(acc_ref, dtile, xall, K, ts):
    for k in range(K):
        xs = _shift_down(xall, K - 1 - k)[HALO:HALO + ts]
        acc_ref[k:k + 1, :] += jnp.sum(dtile * xs, axis=0, keepdims=True)


ROW_CHUNK = 128


def _lane_chunks(W):
    return [(j * LANES, LANES) for j in range(W // LANES)]


def _fill_ext(ext_ref, prev, tile, nxt, ts):
    ext_ref[0:HALO, :] = prev
    ext_ref[HALO:HALO + ts, :] = tile
    ext_ref[HALO + ts:2 * HALO + ts, :] = nxt


def _halo_specs(ts, W, nS, colblk):
    per = ts // HALO
    tile = pl.BlockSpec((None, ts, W), lambda b, c, s: (b, s, colblk(c)))
    prev = pl.BlockSpec((None, HALO, W), lambda b, c, s: (b, jnp.maximum(s * per - 1, 0), colblk(c)))
    nxt = pl.BlockSpec((None, HALO, W), lambda b, c, s: (b, jnp.minimum((s + 1) * per, nS * per - 1), colblk(c)))
    return tile, prev, nxt


def _masked(ref, keep):
    v = ref[...]
    return jnp.where(keep, v, jnp.zeros_like(v))


def _ssd_conv_fwd(proj, w8, b, off, CD, name, ts=512, W=1024):
    Bl, S, _ = proj.shape
    K = 4
    ts, W = _tile(S, ts, 8), _tile(math.gcd(CD, off), W)
    assert off % W == 0
    nS, nW, ob = S // ts, CD // W, off // W

    def body(x_ref, xp_ref, w_ref, b_ref, o_ref):
        xall = jnp.concatenate([_masked(xp_ref, pl.program_id(2) > 0), x_ref[...]], axis=0)
        xc = _conv(xall, w_ref, K)[HALO:] + b_ref[...]
        o_ref[...] = xc * _sigmoid(xc)

    tile, prev, _ = _halo_specs(ts, W, nS, lambda c: ob + c)
    return pl.pallas_call(
        body, name=name, grid=(Bl, nW, nS),
        in_specs=[tile, prev, pl.BlockSpec((8, W), lambda b_, c, s: (0, c)), pl.BlockSpec((1, W), lambda b_, c, s: (0, c))],
        out_specs=pl.BlockSpec((None, ts, W), lambda b_, c, s: (b_, s, c)),
        out_shape=jax.ShapeDtypeStruct((Bl, S, CD), F32),
        compiler_params=_cp(("parallel", "parallel", "parallel")),
    )(proj, proj, w8, b.reshape(1, CD))


def _ssd_conv_bwd(dxa, proj, w8, b, off, CD, dproj, name, ts=512, W=1024):
    Bl, S, _ = proj.shape
    K = 4
    ts, W = _tile(S, ts, 8), _tile(math.gcd(CD, off), W)
    nS, nW, ob = S // ts, CD // W, off // W

    rc = _tile(ts, ROW_CHUNK, 8)

    def body(d_ref, dn_ref, x_ref, xp_ref, xn_ref, w_ref, b_ref, _, dx_ref, dw_ref, db_ref, xext_ref, dext_ref):
        bb, s = pl.program_id(1), pl.program_id(2)

        @pl.when((bb == 0) & (s == 0))
        def _():
            dw_ref[...] = jnp.zeros_like(dw_ref)
            db_ref[...] = jnp.zeros_like(db_ref)

        last = s == nS - 1
        _fill_ext(xext_ref, _masked(xp_ref, s > 0), x_ref[...], _masked(xn_ref, ~last), ts)
        _fill_ext(dext_ref, jnp.zeros((HALO, W), F32), d_ref[...], _masked(dn_ref, ~last), ts)
        for l0, lc in _lane_chunks(W):
            w, bias = w_ref[:, l0:l0 + lc], b_ref[:, l0:l0 + lc]

            def chunk(i, acc):
                r0 = pl.multiple_of(i * rc, 8)
                xw = xext_ref[pl.ds(r0, rc + 2 * HALO), l0:l0 + lc]
                xc = _conv(xw, w, K) + bias
                sg = _sigmoid(xc)
                dxc = dext_ref[pl.ds(r0, rc + 2 * HALO), l0:l0 + lc] * (sg * (1.0 + xc * (1.0 - sg)))
                dx, gw = _conv_t(dxc, xw, w, K, rc)
                dx_ref[pl.ds(r0, rc), l0:l0 + lc] = dx.astype(dx_ref.dtype)
                gb = jnp.sum(dxc[HALO:HALO + rc], axis=0, keepdims=True)
                return tuple(a + g for a, g in zip(acc, gw + [gb]))

            acc = lax.fori_loop(0, ts // rc, chunk, tuple(jnp.zeros((1, lc), F32) for _ in range(K + 1)))
            for k in range(K):
                dw_ref[k:k + 1, l0:l0 + lc] += acc[k]
            db_ref[0:1, l0:l0 + lc] += acc[K]

    per = ts // HALO
    dtile_s = pl.BlockSpec((None, ts, W), lambda c, b_, s: (b_, s, c))
    dnext_s = pl.BlockSpec((None, HALO, W), lambda c, b_, s: (b_, jnp.minimum((s + 1) * per, nS * per - 1), c))
    xtile_s = pl.BlockSpec((None, ts, W), lambda c, b_, s: (b_, s, ob + c))
    xprev_s = pl.BlockSpec((None, HALO, W), lambda c, b_, s: (b_, jnp.maximum(s * per - 1, 0), ob + c))
    xnext_s = pl.BlockSpec((None, HALO, W), lambda c, b_, s: (b_, jnp.minimum((s + 1) * per, nS * per - 1), ob + c))
    wspec = pl.BlockSpec((8, W), lambda c, b_, s: (0, c))
    return pl.pallas_call(
        body, name=name, grid=(nW, Bl, nS),
        in_specs=[dtile_s, dnext_s, xtile_s, xprev_s, xnext_s, wspec, pl.BlockSpec((1, W), lambda c, b_, s: (0, c)),
                  _ANY],
        out_specs=[xtile_s, wspec, wspec],
        out_shape=[jax.ShapeDtypeStruct(dproj.shape, dproj.dtype), jax.ShapeDtypeStruct((8, CD), F32),
                   jax.ShapeDtypeStruct((8, CD), F32)],
        input_output_aliases={7: 0},
        scratch_shapes=[pltpu.VMEM((ts + 2 * HALO, W), F32)] * 2,
        compiler_params=_cp(("arbitrary", "arbitrary", "arbitrary")),
    )(dxa, dxa, proj, proj, proj, w8, b.reshape(1, CD), dproj)


def _sc_conv_fwd(proj, w8, offs, D, name, ts=512, W=1024):
    Bl, S, _ = proj.shape
    K = 3
    ts, W = _tile(S, ts, 8), _tile(D, W)
    nS, nW = S // ts, D // W
    ob, oc, oh = [o // W for o in offs]

    def body(b_ref, c_ref, cp_ref, h_ref, hp_ref, w_ref, o_ref):
        s = pl.program_id(2)
        keep = s > 0
        vall = jnp.concatenate([_masked(cp_ref, keep) * _masked(hp_ref, keep), c_ref[...] * h_ref[...]], axis=0)
        o_ref[...] = (b_ref[...] * _conv(vall, w_ref, K)[HALO:]).astype(o_ref.dtype)

    tb, _, _ = _halo_specs(ts, W, nS, lambda c: ob + c)
    tc, pc, _ = _halo_specs(ts, W, nS, lambda c: oc + c)
    th, ph, _ = _halo_specs(ts, W, nS, lambda c: oh + c)
    return pl.pallas_call(
        body, name=name, grid=(Bl, nW, nS),
        in_specs=[tb, tc, pc, th, ph, pl.BlockSpec((8, W), lambda b_, c, s: (0, c))],
        out_specs=pl.BlockSpec((None, ts, W), lambda b_, c, s: (b_, s, c)),
        out_shape=jax.ShapeDtypeStruct((Bl, S, D), MXU_DTYPE),
        compiler_params=_cp(("parallel", "parallel", "parallel")),
    )(proj, proj, proj, proj, proj, w8)


def _sc_conv_bwd(ds, proj, w8, offs, D, dproj, name, ts=256):
    Bl, S, _ = proj.shape
    K = 3
    ts, W = _tile(S, ts, 8), D
    nS, nW = S // ts, 1
    ob, oc, oh = [o // W for o in offs]
    assert offs[1] == offs[0] + D and offs[2] == offs[0] + 2 * D and offs[0] % (3 * D) == 0

    def body(d_ref, dn_ref, b_ref, bn_ref, c_ref, cp_ref, cn_ref, h_ref, hp_ref, hn_ref, w_ref, _,
             o_ref, dw_ref):
        db_ref, dc_ref, dh_ref = o_ref.at[:, 0:D], o_ref.at[:, D:2 * D], o_ref.at[:, 2 * D:3 * D]
        bb, s = pl.program_id(1), pl.program_id(2)

        @pl.when((bb == 0) & (s == 0))
        def _():
            dw_ref[...] = jnp.zeros_like(dw_ref)

        first, last = s > 0, s < nS - 1
        zeros = jnp.zeros((HALO, W), F32)
        c_t, h_t = c_ref[...], h_ref[...]
        vall = jnp.concatenate([_masked(cp_ref, first) * _masked(hp_ref, first), c_t * h_t,
                                _masked(cn_ref, last) * _masked(hn_ref, last)], axis=0)
        dcv = jnp.concatenate([zeros, d_ref[...] * b_ref[...], _masked(dn_ref, last) * _masked(bn_ref, last)], axis=0)
        cv = _conv(vall, w_ref, K)[HALO:HALO + ts]
        db_ref[...] = (d_ref[...] * cv).astype(db_ref.dtype)
        dv, gw = _conv_t(dcv, vall, w_ref, K, ts)
        dc_ref[...] = (dv * h_t).astype(dc_ref.dtype)
        dh_ref[...] = (dv * c_t).astype(dh_ref.dtype)
        for k in range(K):
            dw_ref[k:k + 1, :] += gw[k]

    per = ts // HALO

    def specs(o):
        t = pl.BlockSpec((None, ts, W), lambda c, b_, s: (b_, s, o + c))
        p = pl.BlockSpec((None, HALO, W), lambda c, b_, s: (b_, jnp.maximum(s * per - 1, 0), o + c))
        n = pl.BlockSpec((None, HALO, W), lambda c, b_, s: (b_, jnp.minimum((s + 1) * per, nS * per - 1), o + c))
        return t, p, n

    dt_, _, dn_ = specs(0)
    bt, _, bn = specs(ob)
    ct, cp, cn = specs(oc)
    ht, hp, hn = specs(oh)
    wspec = pl.BlockSpec((8, W), lambda c, b_, s: (0, c))
    o3 = offs[0] // (3 * D)
    return pl.pallas_call(
        body, name=name, grid=(nW, Bl, nS),
        in_specs=[dt_, dn_, bt, bn, ct, cp, cn, ht, hp, hn, wspec, _ANY],
        out_specs=[pl.BlockSpec((None, ts, 3 * D), lambda c, b_, s: (b_, s, o3)), wspec],
        out_shape=[jax.ShapeDtypeStruct(dproj.shape, dproj.dtype), jax.ShapeDtypeStruct((8, D), F32)],
        input_output_aliases={11: 0},
        compiler_params=_cp(("arbitrary", "arbitrary", "arbitrary")),
    )(ds, ds, proj, proj, proj, proj, proj, proj, proj, proj, w8, dproj)


def _ffn_conv_fwd(up, w8, b, DFF, name, ts=512, W=1408):
    Bl, S, _ = up.shape
    K = 3
    ts, W = _tile(S, ts, 8), _tile(DFF, W)
    nS, nW = S // ts, DFF // W

    def body(g_ref, gp_ref, v_ref, vp_ref, wg_ref, wv_ref, bg_ref, bv_ref, o_ref):
        keep = pl.program_id(2) > 0
        ug = _conv(jnp.concatenate([_masked(gp_ref, keep), g_ref[...]], axis=0), wg_ref, K)[HALO:] + bg_ref[...]
        uv = _conv(jnp.concatenate([_masked(vp_ref, keep), v_ref[...]], axis=0), wv_ref, K)[HALO:] + bv_ref[...]
        o_ref[...] = (ug * _sigmoid(ug) * uv).astype(o_ref.dtype)

    tg, pg, _ = _halo_specs(ts, W, nS, lambda c: c)
    tv, pv, _ = _halo_specs(ts, W, nS, lambda c: nW + c)
    wg = pl.BlockSpec((8, W), lambda b_, c, s: (0, c))
    wv = pl.BlockSpec((8, W), lambda b_, c, s: (0, nW + c))
    bg = pl.BlockSpec((1, W), lambda b_, c, s: (0, c))
    bv = pl.BlockSpec((1, W), lambda b_, c, s: (0, nW + c))
    b2 = b.reshape(1, 2 * DFF)
    return pl.pallas_call(
        body, name=name, grid=(Bl, nW, nS),
        in_specs=[tg, pg, tv, pv, wg, wv, bg, bv],
        out_specs=pl.BlockSpec((None, ts, W), lambda b_, c, s: (b_, s, c)),
        out_shape=jax.ShapeDtypeStruct((Bl, S, DFF), MXU_DTYPE),
        compiler_params=_cp(("parallel", "parallel", "parallel")),
    )(up, up, up, up, w8, w8, b2, b2)


def _ffn_conv_bwd(da, up, w8, b, DFF, name, ts=256):
    Bl, S, _ = up.shape
    K = 3
    ts, W = _tile(S, ts, 8), DFF
    nS = S // ts
    rc = _tile(ts, ROW_CHUNK, 8)

    def body(d_ref, dn_ref, g_ref, gp_ref, gn_ref, v_ref, vp_ref, vn_ref, w_ref, b_ref,
             dup_ref, dw_ref, db_ref, gext_ref, vext_ref, dext_ref):
        bb, s = pl.program_id(0), pl.program_id(1)

        @pl.when((bb == 0) & (s == 0))
        def _():
            dw_ref[...] = jnp.zeros_like(dw_ref)
            db_ref[...] = jnp.zeros_like(db_ref)

        first, last = s > 0, s < nS - 1
        _fill_ext(gext_ref, _masked(gp_ref, first), g_ref[...], _masked(gn_ref, last), ts)
        _fill_ext(vext_ref, _masked(vp_ref, first), v_ref[...], _masked(vn_ref, last), ts)
        _fill_ext(dext_ref, jnp.zeros((HALO, W), F32), d_ref[...], _masked(dn_ref, last), ts)
        for l0, lc in _lane_chunks(W):
            gl, vl = slice(l0, l0 + lc), slice(DFF + l0, DFF + l0 + lc)
            wg_, wv_, bg_, bv_ = w_ref[:, gl], w_ref[:, vl], b_ref[:, gl], b_ref[:, vl]

            def chunk(i, acc):
                r0 = pl.multiple_of(i * rc, 8)
                win = pl.ds(r0, rc + 2 * HALO)
                gw_, vw_, dw_ = gext_ref[win, gl], vext_ref[win, gl], dext_ref[win, gl]
                ug = _conv(gw_, wg_, K) + bg_
                uv = _conv(vw_, wv_, K) + bv_
                sg = _sigmoid(ug)
                dug = dw_ * uv * (sg * (1.0 + ug * (1.0 - sg)))
                duv = dw_ * (ug * sg)
                dg, gg = _conv_t(dug, gw_, wg_, K, rc)
                dv, gv = _conv_t(duv, vw_, wv_, K, rc)
                dup_ref[pl.ds(r0, rc), gl] = dg.astype(dup_ref.dtype)
                dup_ref[pl.ds(r0, rc), vl] = dv.astype(dup_ref.dtype)
                new = gg + gv + [jnp.sum(dug[HALO:HALO + rc], axis=0, keepdims=True),
                                 jnp.sum(duv[HALO:HALO + rc], axis=0, keepdims=True)]
                return tuple(a + g for a, g in zip(acc, new))

            acc = lax.fori_loop(0, ts // rc, chunk, tuple(jnp.zeros((1, lc), F32) for _ in range(2 * K + 2)))
            for k in range(K):
                dw_ref[k:k + 1, gl] += acc[k]
                dw_ref[k:k + 1, vl] += acc[K + k]
            db_ref[0:1, gl] += acc[2 * K]
            db_ref[0:1, vl] += acc[2 * K + 1]

    per = ts // HALO

    def specs(blk):
        t = pl.BlockSpec((None, ts, W), lambda b_, s: (b_, s, blk))
        p = pl.BlockSpec((None, HALO, W), lambda b_, s: (b_, jnp.maximum(s * per - 1, 0), blk))
        n = pl.BlockSpec((None, HALO, W), lambda b_, s: (b_, jnp.minimum((s + 1) * per, nS * per - 1), blk))
        return t, p, n

    dt_, _, dn_ = specs(0)
    gt, gp, gn = specs(0)
    vt, vp, vn = specs(1)
    small = pl.BlockSpec((8, 2 * DFF), lambda b_, s: (0, 0))
    return pl.pallas_call(
        body, name=name, grid=(Bl, nS),
        in_specs=[dt_, dn_, gt, gp, gn, vt, vp, vn, small, pl.BlockSpec((1, 2 * DFF), lambda b_, s: (0, 0))],
        out_specs=[pl.BlockSpec((None, ts, 2 * DFF), lambda b_, s: (b_, s, 0)), small, small],
        out_shape=[jax.ShapeDtypeStruct((Bl, S, 2 * DFF), MXU_DTYPE), jax.ShapeDtypeStruct((8, 2 * DFF), F32),
                   jax.ShapeDtypeStruct((8, 2 * DFF), F32)],
        scratch_shapes=[pltpu.VMEM((ts + 2 * HALO, W), F32)] * 3,
        compiler_params=_cp(("arbitrary", "arbitrary")),
    )(da, da, up, up, up, up, up, up, w8, b.reshape(1, 2 * DFF))


def _merge_fwd(proj, y_ssd, y_sc, off, D, name, ts=512):
    Bl, S, _ = proj.shape
    ts = _tile(S, ts, 8)
    og = off // D

    def body(g1_ref, g2_ref, a_ref, b_ref, o_ref):
        o_ref[...] = (_sigmoid(g1_ref[...]) * a_ref[...] + _sigmoid(g2_ref[...]) * b_ref[...]).astype(o_ref.dtype)

    tok = pl.BlockSpec((None, ts, D), lambda b, s: (b, s, 0))
    return pl.pallas_call(
        body, name=name, grid=(Bl, S // ts),
        in_specs=[pl.BlockSpec((None, ts, D), lambda b, s: (b, s, og)),
                  pl.BlockSpec((None, ts, D), lambda b, s: (b, s, og + 1)), tok, tok],
        out_specs=tok, out_shape=jax.ShapeDtypeStruct((Bl, S, D), MXU_DTYPE),
        compiler_params=_cp(("parallel", "parallel")),
    )(proj, proj, y_ssd, y_sc)


def _merge_bwd(dmixin, proj, y_ssd, y_sc, off, D, dproj0, name, ts=512):
    Bl, S, NP = proj.shape
    ts = _tile(S, ts, 8)
    og = off // D
    assert off % (2 * D) == 0

    def body(d_ref, g1_ref, g2_ref, a_ref, b_ref, *rest):
        da_ref, db_ref, dg_ref = rest[-3:]
        dg1_ref, dg2_ref = dg_ref.at[:, 0:D], dg_ref.at[:, D:2 * D]
        d = d_ref[...]
        s1, s2 = _sigmoid(g1_ref[...]), _sigmoid(g2_ref[...])
        da_ref[...] = (d * s1).astype(da_ref.dtype)
        db_ref[...] = (d * s2).astype(db_ref.dtype)
        dg1_ref[...] = (d * a_ref[...] * (s1 * (1.0 - s1))).astype(dg1_ref.dtype)
        dg2_ref[...] = (d * b_ref[...] * (s2 * (1.0 - s2))).astype(dg2_ref.dtype)

    tok = pl.BlockSpec((None, ts, D), lambda b, s: (b, s, 0))
    act = jax.ShapeDtypeStruct((Bl, S, D), MXU_DTYPE)
    carry = dproj0 is not None
    return pl.pallas_call(
        body, name=name, grid=(Bl, S // ts),
        in_specs=[tok, pl.BlockSpec((None, ts, D), lambda b, s: (b, s, og)),
                  pl.BlockSpec((None, ts, D), lambda b, s: (b, s, og + 1)), tok, tok] + ([_ANY] if carry else []),
        out_specs=[tok, tok, pl.BlockSpec((None, ts, 2 * D), lambda b, s: (b, s, og // 2))],
        out_shape=[act, act, jax.ShapeDtypeStruct((Bl, S, NP), MXU_DTYPE)],
        input_output_aliases={5: 2} if carry else {},
        compiler_params=_cp(("parallel", "parallel")),
    )(*((dmixin, proj, proj, y_ssd, y_sc) + ((dproj0,) if carry else ())))


def _exact_dot(a, b, dn, value):
    v = a if value == 0 else b
    m01 = (b if value == 0 else a).astype(MXU_DTYPE)
    hi = v.astype(MXU_DTYPE)
    r1 = v - hi.astype(F32)
    mid = r1.astype(MXU_DTYPE)
    lo = (r1 - mid.astype(F32)).astype(MXU_DTYPE)
    terms = [(t, m01) if value == 0 else (m01, t) for t in (hi, mid, lo)]
    return _dg(*terms[0], dn) + _dg(*terms[1], dn) + _dg(*terms[2], dn)


def _two_term_dot(v, m01):
    hi = v.astype(MXU_DTYPE)
    lo = (v - hi.astype(F32)).astype(MXU_DTYPE)
    return _dg(hi, m01, NN) + _dg(lo, m01, NN)


def _head_to_channels(R):
    rp = R * HEAD_DIM
    return (lax.shift_right_logical(lax.broadcasted_iota(jnp.int32, (LANES, rp), 1), 6)
            == lax.broadcasted_iota(jnp.int32, (LANES, rp), 0)).astype(MXU_DTYPE)


def _expand_heads(mats, R):
    L = mats[0].shape[0]
    out = _two_term_dot(jnp.concatenate(mats, axis=0), _head_to_channels(R))
    return [out[i * L:(i + 1) * L, :] for i in range(len(mats))]


def _expand_cols(mat, R, lane):
    half = lane < HEAD_DIM
    return jnp.concatenate(
        [jnp.where(half, mat[:, 2 * q:2 * q + 1], mat[:, 2 * q + 1:2 * q + 2]) for q in range(R // 2)], axis=1)


def _row_sums(v):
    return _two_term_dot(v, jnp.ones((v.shape[1], LANES), MXU_DTYPE))


def _lanes_to(v, width):
    return jnp.concatenate([v] * (width // LANES), axis=1)


def _head_rows(colvec, R, N):
    return jnp.concatenate([jnp.broadcast_to(colvec[r:r + 1, :], (HEAD_DIM, N)) for r in range(R)], axis=0)


def _ssd_prep(proj, dt_bias, a_log, selcat, selbig, dm, name):
    Bl, S, L, G = dm.Bl, dm.S, CHUNK, N_GROUPS
    nc = S // L
    odt = dm.ODT // LANES

    def body(dtr_ref, dtb_ref, alog_ref, selcat_ref, selbig_ref, dtg_ref, acs_ref, acst_ref):
        row = lax.broadcasted_iota(jnp.int32, (L, L), 0)
        col = lax.broadcasted_iota(jnp.int32, (L, L), 1)
        dt_all = _softplus(dtr_ref[...] + dtb_ref[...])
        acs_all = _exact_dot((row >= col).astype(F32), dt_all * (-jnp.exp(alog_ref[...])), NN, 1)
        dtg_ref[...] = _exact_dot(dt_all, selcat_ref[...], NN, 0)
        acs_ref[...] = _exact_dot(acs_all, selcat_ref[...], NN, 0)
        acst_ref[...] = _exact_dot(selbig_ref[...], acs_all, NT, 1)

    vec = pl.BlockSpec((1, LANES), lambda b, c: (0, 0))
    wide = pl.BlockSpec((None, L, G * LANES), lambda b, c: (b, c, 0))
    return pl.pallas_call(
        body, name=name, grid=(Bl, nc),
        in_specs=[pl.BlockSpec((None, L, LANES), lambda b, c: (b, c, odt)), vec, vec,
                  pl.BlockSpec((LANES, G * LANES), lambda b, c: (0, 0)),
                  pl.BlockSpec((G * 8, LANES), lambda b, c: (0, 0))],
        out_specs=[wide, wide, pl.BlockSpec((None, None, G * 8, L), lambda b, c: (b, c, 0, 0))],
        out_shape=[jax.ShapeDtypeStruct((Bl, S, G * LANES), F32), jax.ShapeDtypeStruct((Bl, S, G * LANES), F32),
                   jax.ShapeDtypeStruct((Bl, nc, G * 8, L), F32)],
        compiler_params=_cp(("parallel", "parallel")),
    )(proj, dt_bias, a_log, selcat, selbig)


def _ssd_post(d_a, ddt, dtg, proj, dt_bias, a_log, selcat, dproj, dm, name):
    Bl, S, L, G = dm.Bl, dm.S, CHUNK, N_GROUPS
    nc = S // L
    odt = dm.ODT // LANES

    def body(da_ref, ddt_ref, dtg_ref, dtr_ref, dtb_ref, alog_ref, selcat_ref, _, ddtr_ref, dpar_ref):
        @pl.when((pl.program_id(0) == 0) & (pl.program_id(1) == 0))
        def _():
            dpar_ref[...] = jnp.zeros_like(dpar_ref)

        row = lax.broadcasted_iota(jnp.int32, (L, L), 0)
        col = lax.broadcasted_iota(jnp.int32, (L, L), 1)
        selcat = selcat_ref[...]
        a_all = -jnp.exp(alog_ref[...])
        a4 = _exact_dot(jnp.broadcast_to(a_all, (8, LANES)), selcat, NN, 0)[0:1, :]
        dadt = _exact_dot((col >= row).astype(F32), da_ref[...], NN, 1)
        ddt4 = ddt_ref[...] + dadt * a4
        da4 = jnp.sum(dadt * dtg_ref[...], axis=0, keepdims=True)
        ddt_all = _exact_dot(ddt4, selcat, NT, 0)
        da_all = _exact_dot(jnp.broadcast_to(da4, (8, G * LANES)), selcat, NT, 0)[0:1, :]
        ddtr = ddt_all * _sigmoid(dtr_ref[...] + dtb_ref[...])
        ddtr_ref[...] = ddtr.astype(ddtr_ref.dtype)
        dpar_ref[0:1, :] += jnp.sum(ddtr, axis=0, keepdims=True)
        dpar_ref[1:2, :] += da_all * a_all

    vec = pl.BlockSpec((1, LANES), lambda b, c: (0, 0))
    wide = pl.BlockSpec((None, L, G * LANES), lambda b, c: (b, c, 0))
    return pl.pallas_call(
        body, name=name, grid=(Bl, nc),
        in_specs=[wide, wide, wide, pl.BlockSpec((None, L, LANES), lambda b, c: (b, c, odt)), vec, vec,
                  pl.BlockSpec((LANES, G * LANES), lambda b, c: (0, 0)), _ANY],
        out_specs=[pl.BlockSpec((None, L, LANES), lambda b, c: (b, c, odt)), pl.BlockSpec((8, LANES), lambda b, c: (0, 0))],
        out_shape=[jax.ShapeDtypeStruct(dproj.shape, dproj.dtype), jax.ShapeDtypeStruct((8, LANES), F32)],
        input_output_aliases={7: 0},
        compiler_params=_cp(("arbitrary", "arbitrary")),
    )(d_a, ddt, dtg, proj, dt_bias, a_log, selcat, dproj)


def _scan_fwd(xbc_a, proj, dtg4, acs4, acs_t4, dsk_exp, norm_g, dm, name):
    Bl, S, DI, N, R, L, G = dm.Bl, dm.S, dm.DI, dm.N, dm.R, CHUNK, N_GROUPS
    RP = R * HEAD_DIM
    nc = S // L

    def body(x_ref, z_ref, dtg_ref, acs_ref, acst_ref, dsk_ref, ng_ref, y_ref, yn_ref, hp_ref, h_ref):
        xs_ref, bm_ref, cm_ref = x_ref.at[:, 0:RP], x_ref.at[:, RP:RP + N], x_ref.at[:, RP + N:RP + 2 * N]
        @pl.when(pl.program_id(2) == 0)
        def _():
            h_ref[...] = jnp.zeros_like(h_ref)

        causal = lax.broadcasted_iota(jnp.int32, (L, L), 0) >= lax.broadcasted_iota(jnp.int32, (L, L), 1)
        lane = lax.broadcasted_iota(jnp.int32, (L, LANES), 1)
        dtg, acs, acs_t = dtg_ref[...], acs_ref[...], acst_ref[...]
        xs = xs_ref[...]
        bmb, cmb = bm_ref[...].astype(MXU_DTYPE), cm_ref[...].astype(MXU_DTYPE)
        sg = _dg(cmb, bmb, NT)
        acs_last = acs[L - 1:L, :]
        dt_exp, ea_exp, ds_exp = _expand_heads([dtg, jnp.exp(acs), jnp.exp(acs_last - acs)], R)
        xdt = xs * dt_exp
        xb = xdt.astype(MXU_DTYPE)
        parts = []
        for q in range(R // 2):
            x2 = xb[:, LANES * q:LANES * (q + 1)]
            ys = []
            for r in (2 * q, 2 * q + 1):
                dec = jnp.exp(jnp.where(causal, acs[:, r:r + 1] - acs_t[r:r + 1, :], -1e30))
                ys.append(_dg((sg * dec).astype(MXU_DTYPE), x2, NN))
            parts.append(jnp.where(lane < HEAD_DIM, ys[0], ys[1]))
        ydiag = jnp.concatenate(parts, axis=1)
        h_cur = h_ref[...]
        hb = h_cur.astype(MXU_DTYPE)
        yoff = _dg(cmb, hb, NT) * ea_exp
        st = _dg((xdt * ds_exp).astype(MXU_DTYPE), bmb, TN)
        hp_ref[...] = hb
        h_ref[...] = h_cur * _head_rows(jnp.exp(acs_t[:, L - 1:L]), R, N) + st
        y = ydiag + yoff + dsk_ref[...] * xs
        y_ref[...] = y
        z = z_ref[...]
        yg = y * (z * _sigmoid(z))
        rr = lax.rsqrt(_row_sums(yg * yg) * (1.0 / RP) + EPS)
        yn_ref[...] = (yg * _lanes_to(rr, RP) * ng_ref[...]).astype(yn_ref.dtype)

    oz = dm.OZ // RP
    grp = pl.BlockSpec((None, L, RP), lambda b, g, c: (b, c, g))
    lanes = pl.BlockSpec((None, L, LANES), lambda b, g, c: (b, c, g))
    chan = pl.BlockSpec((1, RP), lambda b, g, c: (0, g))
    return pl.pallas_call(
        body, name=name, grid=(Bl, G, nc),
        in_specs=[pl.BlockSpec((None, L, RP + 2 * N), lambda b, g, c: (b, c, g)),
                  pl.BlockSpec((None, L, RP), lambda b, g, c: (b, c, oz + g)), lanes, lanes,
                  pl.BlockSpec((None, None, 8, L), lambda b, g, c: (b, c, g, 0)),
                  chan, chan],
        out_specs=[grp, grp, pl.BlockSpec((None, None, None, RP, N), lambda b, g, c: (b, g, c, 0, 0))],
        out_shape=[jax.ShapeDtypeStruct((Bl, S, DI), F32), jax.ShapeDtypeStruct((Bl, S, DI), MXU_DTYPE),
                   jax.ShapeDtypeStruct((Bl, G, nc, RP, N), MXU_DTYPE)],
        scratch_shapes=[pltpu.VMEM((RP, N), F32)],
        compiler_params=_cp(("parallel", "parallel", "arbitrary")),
    )(xbc_a, proj, dtg4, acs4, acs_t4, dsk_exp, norm_g.reshape(1, DI))


def _scan_bwd(dyn, y, xbc_a, proj, hprev, dtg4, acs4, acs_t4, dsk_exp, norm_g, dproj, dm, name):
    Bl, S, DI, N, R, L, G = dm.Bl, dm.S, dm.DI, dm.N, dm.R, CHUNK, N_GROUPS
    RP = R * HEAD_DIM
    nc = S // L

    def body(dyn_ref, y_ref, z_ref, x_ref, hp_ref, dtg_ref, acs_ref, acst_ref, dsk_ref, ng_ref,
             _, dz_ref, dx_ref, da_ref, ddt_ref, ddsk_ref, dng_ref, dh_ref):
        xs_ref, bm_ref, cm_ref = x_ref.at[:, 0:RP], x_ref.at[:, RP:RP + N], x_ref.at[:, RP + N:RP + 2 * N]
        dxs_ref, dbm_ref, dcm_ref = dx_ref.at[:, 0:RP], dx_ref.at[:, RP:RP + N], dx_ref.at[:, RP + N:RP + 2 * N]
        b, c = pl.program_id(1), pl.program_id(2)

        @pl.when(c == 0)
        def _():
            dh_ref[...] = jnp.zeros_like(dh_ref)

        @pl.when((b == 0) & (c == 0))
        def _():
            ddsk_ref[...] = jnp.zeros_like(ddsk_ref)
            dng_ref[...] = jnp.zeros_like(dng_ref)

        row = lax.broadcasted_iota(jnp.int32, (L, L), 0)
        col = lax.broadcasted_iota(jnp.int32, (L, L), 1)
        causal, anti = row >= col, col >= row
        lane = lax.broadcasted_iota(jnp.int32, (L, LANES), 1)
        etb = (lax.shift_right_logical(lax.broadcasted_iota(jnp.int32, (RP, LANES), 0), 6)
               == lax.broadcasted_iota(jnp.int32, (RP, LANES), 1)).astype(MXU_DTYPE)

        dtg, acs, acs_t = dtg_ref[...], acs_ref[...], acst_ref[...]
        xs, z, y, dyn = xs_ref[...], z_ref[...], y_ref[...], dyn_ref[...]
        bmb, cmb = bm_ref[...].astype(MXU_DTYPE), cm_ref[...].astype(MXU_DTYPE)
        hpb = hp_ref[...]
        ng = ng_ref[...]

        sz = _sigmoid(z)
        siluz = z * sz
        yg = y * siluz
        rr = lax.rsqrt(jnp.mean(yg * yg, axis=-1, keepdims=True) + EPS)
        yhat = yg * rr
        dng_ref[...] += jnp.sum(dyn * yhat, axis=0, keepdims=True)
        dyhat = dyn * ng
        dyg = rr * (dyhat - yhat * jnp.mean(dyhat * yhat, axis=-1, keepdims=True))
        dy = dyg * siluz
        dz_ref[...] = (dyg * y * (sz * (1.0 + z * (1.0 - sz)))).astype(dz_ref.dtype)

        dxs = dy * dsk_ref[...]
        ddsk_ref[...] += jnp.sum(dy * xs, axis=0, keepdims=True)

        acs_last = acs[L - 1:L, :]
        dt_exp = _expand_cols(dtg, R, lane)
        ea_exp = _expand_cols(jnp.exp(acs), R, lane)
        ds_exp = _expand_cols(jnp.exp(acs_last - acs), R, lane)
        xdt = xs * dt_exp
        xb = xdt.astype(MXU_DTYPE)
        dyb = dy.astype(MXU_DTYPE)
        cd = jnp.exp(acs_last)
        cd_rows = _head_rows(jnp.exp(acs_t[:, L - 1:L]), R, N)

        q_ = _dg(cmb, hpb, NT)
        dq = dy * ea_exp
        dqb = dq.astype(MXU_DTYPE)
        dcm = _dg(dqb, hpb, NN)
        dh_yoff = _dg(dqb, cmb, TN)

        dhn = dh_ref[...]
        wprod = dhn * hpb.astype(F32)
        per_head = jnp.concatenate(
            [jnp.sum(wprod[HEAD_DIM * r:HEAD_DIM * (r + 1), :], axis=0, keepdims=True) for r in range(R)]
            + ([jnp.zeros((8 - R, N), F32)] if R < 8 else []), axis=0)
        dcd_col = jnp.sum(per_head, axis=1, keepdims=True)
        diag8 = lax.broadcasted_iota(jnp.int32, (8, LANES), 0) == lax.broadcasted_iota(jnp.int32, (8, LANES), 1)
        dcd_lane = jnp.sum(jnp.where(diag8, dcd_col, 0.0), axis=0, keepdims=True)
        d_a_last = dcd_lane * cd
        dh_ref[...] = dhn * cd_rows + dh_yoff
        dhnb = dhn.astype(MXU_DTYPE)

        e_ = _dg(bmb, dhnb, NT)
        dxdt = ds_exp * e_
        xds = xdt * ds_exp
        dbm = _dg(xds.astype(MXU_DTYPE), dhnb, NN)

        sg = _dg(cmb, bmb, NT)
        sg_t = _dg(bmb, cmb, NT)
        dsg = jnp.zeros((L, L), F32)
        dsg_t = jnp.zeros((L, L), F32)
        d_a = jnp.zeros((L, LANES), F32)
        parts = []
        for q in range(R // 2):
            x2 = xb[:, LANES * q:LANES * (q + 1)]
            dy2 = dyb[:, LANES * q:LANES * (q + 1)]
            dxs2 = []
            for hh, r in enumerate((2 * q, 2 * q + 1)):
                mine = (lane < HEAD_DIM) if hh == 0 else (lane >= HEAD_DIM)
                diff = acs[:, r:r + 1] - acs_t[r:r + 1, :]
                dec = jnp.exp(jnp.where(causal, diff, -1e30))
                dec_t = jnp.exp(jnp.where(anti, -diff, -1e30))
                dy2m = jnp.where(mine, dy2, jnp.zeros_like(dy2))
                dm_ = _dg(dy2m, x2, NT)
                dm_t = _dg(x2, dy2m, NT)
                m_t = sg_t * dec_t
                da_col = jnp.sum(dm_ * (sg * dec) - dm_t * m_t, axis=1, keepdims=True)
                d_a = d_a + jnp.where(lane == r, da_col, 0.0)
                dsg = dsg + dm_ * dec
                dsg_t = dsg_t + dm_t * dec_t
                dxs2.append(_dg(m_t.astype(MXU_DTYPE), dy2, NN))
            parts.append(jnp.where(lane < HEAD_DIM, dxs2[0], dxs2[1]))
        dxdt = dxdt + jnp.concatenate(parts, axis=1)
        dcm_ref[...] = dcm + _dg(dsg.astype(MXU_DTYPE), bmb, NN)
        dbm_ref[...] = dbm + _dg(dsg_t.astype(MXU_DTYPE), cmb, NN)
        dxs_ref[...] = dxs + dxdt * dt_exp

        hs = _two_term_dot(jnp.concatenate([dq * q_ - xds * e_, xds * e_, dxdt * xs], axis=0), etb)
        t2 = hs[L:2 * L, :]
        rowl = lax.broadcasted_iota(jnp.int32, (L, LANES), 0)
        d_a_last = d_a_last + jnp.sum(t2, axis=0, keepdims=True)
        da_ref[...] = d_a + hs[0:L, :] + jnp.where(rowl == L - 1, d_a_last, 0.0)
        ddt_ref[...] = hs[2 * L:3 * L, :]

    oz = dm.OZ // RP
    grp = pl.BlockSpec((None, L, RP), lambda g, b, c: (b, nc - 1 - c, g))
    zspec = pl.BlockSpec((None, L, RP), lambda g, b, c: (b, nc - 1 - c, oz + g))
    lanes = pl.BlockSpec((None, L, LANES), lambda g, b, c: (b, nc - 1 - c, g))
    xg = pl.BlockSpec((None, L, RP + 2 * N), lambda g, b, c: (b, nc - 1 - c, g))
    chan = pl.BlockSpec((1, RP), lambda g, b, c: (0, g))
    wide = jax.ShapeDtypeStruct((Bl, S, G * LANES), F32)
    return pl.pallas_call(
        body, name=name, grid=(G, Bl, nc),
        in_specs=[grp, grp, zspec, xg,
                  pl.BlockSpec((None, None, None, RP, N), lambda g, b, c: (b, g, nc - 1 - c, 0, 0)),
                  lanes, lanes, pl.BlockSpec((None, None, 8, L), lambda g, b, c: (b, nc - 1 - c, g, 0)),
                  chan, chan, _ANY],
        out_specs=[zspec, xg, lanes, lanes, chan, chan],
        out_shape=[jax.ShapeDtypeStruct(dproj.shape, dproj.dtype), jax.ShapeDtypeStruct(xbc_a.shape, F32),
                   wide, wide, jax.ShapeDtypeStruct((1, DI), F32), jax.ShapeDtypeStruct((1, DI), F32)],
        input_output_aliases={10: 0},
        scratch_shapes=[pltpu.VMEM((RP, N), F32)],
        compiler_params=_cp(("arbitrary", "arbitrary", "arbitrary")),
    )(dyn, y, proj, xbc_a, hprev, dtg4, acs4, acs_t4, dsk_exp, norm_g.reshape(1, DI), dproj)


def _adam_update(w, m, v, g):
    c1 = 1.0 - ADAM_B1 ** ADAM_STEP
    c2 = 1.0 - ADAM_B2 ** ADAM_STEP
    nm = ADAM_B1 * m + (1.0 - ADAM_B1) * g
    nv = ADAM_B2 * v + (1.0 - ADAM_B2) * (g * g)
    return nm, nv, -ADAM_LR * ((nm / c1) / (jnp.sqrt(nv / c2) + ADAM_EPS) + ADAM_WD * w)


def _adamw(w, m, v, g, name, tr=128):
    rows, cols = w.shape
    tr = _tile(rows, tr, 8)

    def body(w_ref, m_ref, v_ref, ga_ref, g_ref, d_ref, nm_ref, nv_ref):
        g = ga_ref[...]
        g_ref[...] = g
        nm_ref[...], nv_ref[...], d_ref[...] = _adam_update(w_ref[...], m_ref[...], v_ref[...], g)

    blk = pl.BlockSpec((tr, cols), lambda i: (i, 0))
    shp = jax.ShapeDtypeStruct((rows, cols), F32)
    return pl.pallas_call(
        body, name=name, grid=(rows // tr,), in_specs=[blk] * 4, out_specs=[blk] * 4,
        out_shape=[shp] * 4, compiler_params=_cp(("parallel",)),
    )(w, m, v, g)


def _adamw_layers(w, m, v, g_mine, g_theirs, core, name, tr=128):
    _, rows, cols = w.shape
    tr = _tile(rows, tr, 8)

    def body(c_ref, w_ref, m_ref, v_ref, ga_ref, gb_ref, g_ref, d_ref, nm_ref, nv_ref):
        g = jnp.where(pl.program_id(0) == c_ref[0], ga_ref[...], gb_ref[...])
        g_ref[...] = g
        nm_ref[...], nv_ref[...], d_ref[...] = _adam_update(w_ref[...], m_ref[...], v_ref[...], g)

    lay = pl.BlockSpec((None, tr, cols), lambda l, i, c_ref: (l, i, 0))
    one = pl.BlockSpec((tr, cols), lambda l, i, c_ref: (i, 0))
    shp = jax.ShapeDtypeStruct(w.shape, F32)
    return pl.pallas_call(
        body, name=name,
        grid_spec=pltpu.PrefetchScalarGridSpec(num_scalar_prefetch=1, grid=(2, rows // tr),
                                               in_specs=[lay, lay, lay, one, one], out_specs=[lay] * 4),
        out_shape=[shp] * 4, compiler_params=_cp(("parallel", "parallel")),
    )(core, w, m, v, g_mine, g_theirs)


def _sum_slots(buf, name, tr=256):
    n, rows, cols = buf.shape
    tr = _tile(rows, tr, 8)

    def body(b_ref, o_ref):
        acc = b_ref[0].astype(F32)
        for k in range(1, n):
            acc = acc + b_ref[k].astype(F32)
        o_ref[...] = acc

    return pl.pallas_call(
        body, name=name, grid=(rows // tr,),
        in_specs=[pl.BlockSpec((n, tr, cols), lambda i: (0, i, 0))],
        out_specs=pl.BlockSpec((tr, cols), lambda i: (i, 0)),
        out_shape=jax.ShapeDtypeStruct((rows, cols), F32), compiler_params=_cp(("parallel",)),
    )(buf)


_ANY = pl.BlockSpec(memory_space=pl.ANY)


def _exchange_chips(src, per_dest, name):
    rows, cols = src.shape[-2:]

    def body(in_ref, out_ref, send_sems, recv_sems, local_sem):
        x, y, c = lax.axis_index("x"), lax.axis_index("y"), lax.axis_index("c")
        me = 2 * x + y
        chips = [(1 - x, y), (x, 1 - y), (1 - x, 1 - y)]

        def block(j):
            return in_ref.at[j] if per_dest else in_ref

        mine = pltpu.make_async_copy(block(me), out_ref.at[me], local_sem)
        mine.start()
        sends = []
        for k, (px, py) in enumerate(chips):
            cp = pltpu.make_async_remote_copy(
                src_ref=block(2 * px + py), dst_ref=out_ref.at[me], send_sem=send_sems.at[k],
                recv_sem=recv_sems.at[k], device_id=(px, py, c), device_id_type=MESH)
            cp.start()
            sends.append(cp)
        for k, (px, py) in enumerate(chips):
            pltpu.make_async_remote_copy(
                src_ref=block(me), dst_ref=out_ref.at[2 * px + py], send_sem=send_sems.at[k],
                recv_sem=recv_sems.at[k], device_id=(px, py, c), device_id_type=MESH).wait_recv()
        for cp in sends:
            cp.wait_send()
        mine.wait()

    return pl.pallas_call(
        body, name=name, in_specs=[_ANY], out_specs=_ANY,
        out_shape=jax.ShapeDtypeStruct((N_CHIPS, rows, cols), src.dtype),
        scratch_shapes=[pltpu.SemaphoreType.DMA((3,)), pltpu.SemaphoreType.DMA((3,)), pltpu.SemaphoreType.DMA(())],
    )(src)


def _shard_window(ref, kind, j, lead=()):
    if kind == "slots":
        return ref.at[(j,) + lead]
    r, c = ref.shape[-2] // (N_CHIPS if kind == "rows" else 1), ref.shape[-1] // (N_CHIPS if kind == "cols" else 1)
    full = tuple(slice(None) for _ in range(len(ref.shape) - 2 - len(lead)))
    if kind == "rows":
        return ref.at[lead + full + (pl.ds(pl.multiple_of(j * r, 16), r), slice(None))]
    return ref.at[lead + full + (slice(None), pl.ds(pl.multiple_of(j * c, LANES), c))]


def _gather_kind(shard, axis):
    if axis == 1:
        return "rows"
    return "cols" if shard.shape[2] % LANES == 0 else "slots"


def _gather_weights(shards, axes, name):
    kinds = ["rows" if ax == 1 else "slots" for ax in axes]
    nw = len(shards)

    def out_shape(s, kind):
        d, r, c = s.shape
        shp = {"rows": (d, N_CHIPS * r, c), "cols": (d, r, N_CHIPS * c), "slots": (N_CHIPS, d, r, c)}[kind]
        return jax.ShapeDtypeStruct(shp, s.dtype)

    assert all(s.shape[0] == 2 for s in shards)

    def body(*refs):
        ins, outs = refs[:nw], refs[nw:2 * nw]
        ici_send, ici_recv, d2d_send, d2d_recv, local_sems = refs[2 * nw:]
        x, y, c = lax.axis_index("x"), lax.axis_index("y"), lax.axis_index("c")
        me = 2 * x + y
        chips = [(1 - x, y), (x, 1 - y), (1 - x, 1 - y)]
        sends = []
        for i in range(nw):
            if kinds[i] == "rows":
                own = pltpu.make_async_copy(ins[i], _shard_window(outs[i], kinds[i], me), local_sems.at[i])
                own.start()
                sends.append((own, False))
        for i in range(nw):
            for k, (px, py) in enumerate(chips):
                cp = pltpu.make_async_remote_copy(
                    src_ref=ins[i].at[c], dst_ref=_shard_window(outs[i], kinds[i], me, (c,)),
                    send_sem=ici_send.at[3 * i + k], recv_sem=ici_recv.at[3 * i + k],
                    device_id=(px, py, c), device_id_type=MESH)
                cp.start()
                sends.append((cp, True))
        for i in range(nw):
            for k, (px, py) in enumerate(chips):
                win = _shard_window(outs[i], kinds[i], 2 * px + py, (c,))
                pltpu.make_async_remote_copy(
                    src_ref=ins[i].at[c], dst_ref=win, send_sem=ici_send.at[3 * i + k], recv_sem=ici_recv.at[3 * i + k],
                    device_id=(px, py, c), device_id_type=MESH).wait_recv()
                fw = pltpu.make_async_remote_copy(
                    src_ref=win, dst_ref=win, send_sem=d2d_send.at[3 * i + k], recv_sem=d2d_recv.at[3 * i + k],
                    device_id=(x, y, 1 - c), device_id_type=MESH)
                fw.start()
                sends.append((fw, True))
        for i in range(nw):
            for k, (px, py) in enumerate(chips):
                win = _shard_window(outs[i], kinds[i], 2 * px + py, (1 - c,))
                pltpu.make_async_remote_copy(
                    src_ref=win, dst_ref=win, send_sem=d2d_send.at[3 * i + k], recv_sem=d2d_recv.at[3 * i + k],
                    device_id=(x, y, 1 - c), device_id_type=MESH).wait_recv()
        for cp, remote in sends:
            cp.wait_send() if remote else cp.wait()

    outs = pl.pallas_call(
        body, name=name, in_specs=[_ANY] * nw, out_specs=[_ANY] * nw,
        out_shape=[out_shape(s, k) for s, k in zip(shards, kinds)],
        scratch_shapes=[pltpu.SemaphoreType.DMA((3 * nw,)), pltpu.SemaphoreType.DMA((3 * nw,)),
                        pltpu.SemaphoreType.DMA((3 * nw,)), pltpu.SemaphoreType.DMA((3 * nw,)),
                        pltpu.SemaphoreType.DMA((nw,))],
    )(*shards)
    me = 2 * lax.axis_index("x") + lax.axis_index("y")
    return [jnp.concatenate([jnp.where(me == j, s, o[j]) for j in range(N_CHIPS)], axis=2) if k == "slots" else o
            for s, o, k in zip(shards, outs, kinds)]


def _swap_other_layer(gst, name):
    nv = len(gst)

    def body(*refs):
        ins, outs, send_sems, recv_sems = refs[:nv], refs[nv:2 * nv], refs[2 * nv], refs[2 * nv + 1]
        x, y, c = lax.axis_index("x"), lax.axis_index("y"), lax.axis_index("c")
        cps = [pltpu.make_async_remote_copy(src_ref=ins[i].at[1 - c], dst_ref=outs[i], send_sem=send_sems.at[i],
                                            recv_sem=recv_sems.at[i], device_id=(x, y, 1 - c), device_id_type=MESH)
               for i in range(nv)]
        for cp in cps:
            cp.start()
        for cp in cps:
            cp.wait()

    return pl.pallas_call(
        body, name=name, in_specs=[_ANY] * nv, out_specs=[_ANY] * nv,
        out_shape=[jax.ShapeDtypeStruct(v.shape[1:], v.dtype) for v in gst],
        scratch_shapes=[pltpu.SemaphoreType.DMA((nv,)), pltpu.SemaphoreType.DMA((nv,))],
    )(*gst)


def _pair_sum(g, other, core, name, tr=256):
    _, rows, cols = g.shape
    tr = _tile(rows, tr, 16)

    def body(c_ref, g_ref, o_ref, s_ref):
        s_ref[...] = (g_ref[...].astype(F32) + o_ref[...].astype(F32)).astype(s_ref.dtype)

    blk = pl.BlockSpec((tr, cols), lambda i, c_ref: (i, 0))
    return pl.pallas_call(
        body, name=name,
        grid_spec=pltpu.PrefetchScalarGridSpec(
            num_scalar_prefetch=1, grid=(rows // tr,),
            in_specs=[pl.BlockSpec((None, tr, cols), lambda i, c_ref: (c_ref[0], i, 0)), blk], out_specs=blk),
        out_shape=jax.ShapeDtypeStruct((rows, cols), g.dtype), compiler_params=_cp(("parallel",)),
    )(core, g, other)


def _scatter_layer(parts, kinds, name):
    nw = len(parts)

    def shard_shape(g, kind):
        return g.shape[-2] // (N_CHIPS if kind == "rows" else 1), g.shape[-1] // (N_CHIPS if kind == "cols" else 1)

    def body(*refs):
        ins, outs = refs[:nw], refs[nw:2 * nw]
        send_sems, recv_sems, local_sems = refs[2 * nw:]
        x, y, c = lax.axis_index("x"), lax.axis_index("y"), lax.axis_index("c")
        me = 2 * x + y
        chips = [(1 - x, y), (x, 1 - y), (1 - x, 1 - y)]
        sends = []
        for i in range(nw):
            own = pltpu.make_async_copy(_shard_window(ins[i], kinds[i], me), outs[i].at[me], local_sems.at[i])
            own.start()
            sends.append((own, False))
            for k, (px, py) in enumerate(chips):
                cp = pltpu.make_async_remote_copy(
                    src_ref=_shard_window(ins[i], kinds[i], 2 * px + py), dst_ref=outs[i].at[me],
                    send_sem=send_sems.at[3 * i + k], recv_sem=recv_sems.at[3 * i + k],
                    device_id=(px, py, c), device_id_type=MESH)
                cp.start()
                sends.append((cp, True))
        for i in range(nw):
            for k, (px, py) in enumerate(chips):
                pltpu.make_async_remote_copy(
                    src_ref=_shard_window(ins[i], kinds[i], me), dst_ref=outs[i].at[2 * px + py],
                    send_sem=send_sems.at[3 * i + k], recv_sem=recv_sems.at[3 * i + k],
                    device_id=(px, py, c), device_id_type=MESH).wait_recv()
        for cp, remote in sends:
            cp.wait_send() if remote else cp.wait()

    return pl.pallas_call(
        body, name=name, in_specs=[_ANY] * nw, out_specs=[_ANY] * nw,
        out_shape=[jax.ShapeDtypeStruct((N_CHIPS,) + shard_shape(g, k), g.dtype) for g, k in zip(parts, kinds)],
        scratch_shapes=[pltpu.SemaphoreType.DMA((3 * nw,)), pltpu.SemaphoreType.DMA((3 * nw,)),
                        pltpu.SemaphoreType.DMA((nw,))],
    )(*parts)


def _sibling_swap(vs, name):
    nv = len(vs)

    def body(*refs):
        ins, outs, send_sems, recv_sems = refs[:nv], refs[nv:2 * nv], refs[2 * nv], refs[2 * nv + 1]
        x, y, c = lax.axis_index("x"), lax.axis_index("y"), lax.axis_index("c")
        cps = [pltpu.make_async_remote_copy(src_ref=ins[i], dst_ref=outs[i], send_sem=send_sems.at[i],
                                            recv_sem=recv_sems.at[i], device_id=(x, y, 1 - c), device_id_type=MESH)
               for i in range(nv)]
        for cp in cps:
            cp.start()
        for cp in cps:
            cp.wait()

    return pl.pallas_call(
        body, name=name, in_specs=[_ANY] * nv, out_specs=[_ANY] * nv,
        out_shape=[jax.ShapeDtypeStruct(v.shape, v.dtype) for v in vs],
        scratch_shapes=[pltpu.SemaphoreType.DMA((nv,)), pltpu.SemaphoreType.DMA((nv,))],
    )(*vs)


def _allgather_all(v, name):
    rows, cols = v.shape

    def body(in_ref, out_ref, send_sems, recv_sems, local_sem):
        x, y, c = lax.axis_index("x"), lax.axis_index("y"), lax.axis_index("c")
        me = 4 * x + 2 * y + c
        peers = []
        for k in range(1, N_DEV):
            peers.append(((1 - x) if k & 4 else x, (1 - y) if k & 2 else y, (1 - c) if k & 1 else c))
        mine = pltpu.make_async_copy(in_ref, out_ref.at[me], local_sem)
        mine.start()
        sends = []
        for k, peer in enumerate(peers):
            cp = pltpu.make_async_remote_copy(src_ref=in_ref, dst_ref=out_ref.at[me], send_sem=send_sems.at[k],
                                              recv_sem=recv_sems.at[k], device_id=peer, device_id_type=MESH)
            cp.start()
            sends.append(cp)
        for k, (px, py, pc) in enumerate(peers):
            pltpu.make_async_remote_copy(src_ref=in_ref, dst_ref=out_ref.at[4 * px + 2 * py + pc],
                                         send_sem=send_sems.at[k], recv_sem=recv_sems.at[k],
                                         device_id=(px, py, pc), device_id_type=MESH).wait_recv()
        for cp in sends:
            cp.wait_send()
        mine.wait()

    return pl.pallas_call(
        body, name=name, in_specs=[_ANY], out_specs=_ANY,
        out_shape=jax.ShapeDtypeStruct((N_DEV, rows, cols), v.dtype),
        scratch_shapes=[pltpu.SemaphoreType.DMA((N_DEV - 1,)), pltpu.SemaphoreType.DMA((N_DEV - 1,)),
                        pltpu.SemaphoreType.DMA(())],
    )(v)


def _pack(arrs, dtype, width, row_mult):
    flat = jnp.concatenate([a.reshape(-1).astype(dtype) for a in arrs])
    unit = width * row_mult
    total = -(-flat.shape[0] // unit) * unit
    return jnp.pad(flat, (0, total - flat.shape[0])).reshape(-1, width)


def _unpack(buf, shapes):
    flat = buf.reshape(-1)
    out, off = [], 0
    for shp in shapes:
        n = 1
        for d in shp:
            n *= d
        out.append(flat[off:off + n].reshape(shp))
        off += n
    return out


class _Dims:
    pass


def _dims(x, ssd_dt_bias, ssd_norm_g, ssd_conv_b, ffn_conv_b):
    dm = _Dims()
    dm.Bl, dm.S, dm.D = x.shape
    dm.H, dm.DI, dm.CD = ssd_dt_bias.shape[-1], ssd_norm_g.shape[-1], ssd_conv_b.shape[-1]
    dm.N = (dm.CD - dm.DI) // (2 * N_GROUPS)
    dm.R = dm.H // N_GROUPS
    dm.DFF = ffn_conv_b.shape[-1] // 2
    D = dm.D
    dm.OB, dm.OC, dm.OH, dm.OX = 0, D, 2 * D, 3 * D
    dm.OZ = dm.OX + dm.CD
    dm.OG = -(-(dm.OZ + dm.DI) // (2 * D)) * (2 * D)
    dm.GPAD = dm.OG - (dm.OZ + dm.DI)
    dm.ODT = dm.OG + 2 * D
    dm.NP = dm.ODT + LANES
    assert dm.OZ % (dm.R * HEAD_DIM) == 0
    assert dm.DI // dm.H == HEAD_DIM and dm.N == LANES and dm.R % 2 == 0 and dm.S % CHUNK == 0 and dm.H <= LANES
    return dm


def _group_xbc(v, dm):
    rp, n = dm.R * HEAD_DIM, dm.N
    parts = []
    for g in range(N_GROUPS):
        parts += [v[..., g * rp:(g + 1) * rp], v[..., dm.DI + g * n:dm.DI + (g + 1) * n],
                  v[..., dm.DI + (N_GROUPS + g) * n:dm.DI + (N_GROUPS + g + 1) * n]]
    return jnp.concatenate(parts, axis=-1)


def _ungroup_xbc(v, dm):
    rp, n = dm.R * HEAD_DIM, dm.N
    gw = rp + 2 * n
    xs = [v[..., g * gw:g * gw + rp] for g in range(N_GROUPS)]
    bs = [v[..., g * gw + rp:g * gw + rp + n] for g in range(N_GROUPS)]
    cs = [v[..., g * gw + rp + n:(g + 1) * gw] for g in range(N_GROUPS)]
    return jnp.concatenate(xs + bs + cs, axis=-1)


def _permute_w_in(w, dm):
    o, sc = dm.DI + dm.CD, dm.DI + dm.CD + dm.H
    zeros = lambda n: jnp.zeros((w.shape[0], n), w.dtype)
    return jnp.concatenate([w[:, sc:sc + 3 * dm.D], _group_xbc(w[:, dm.DI:o], dm), w[:, :dm.DI], zeros(dm.GPAD),
                            w[:, sc + 3 * dm.D:], w[:, o:o + dm.H], zeros(LANES - dm.H)], axis=1)


def _unpermute_w_in(dw, dm):
    return jnp.concatenate([dw[..., dm.OZ:dm.OZ + dm.DI], _ungroup_xbc(dw[..., dm.OX:dm.OX + dm.CD], dm),
                            dw[..., dm.ODT:dm.ODT + dm.H], dw[..., :3 * dm.D], dw[..., dm.OG:dm.OG + 2 * dm.D]], axis=-1)


def _lane_pad(v):
    return jnp.pad(v.reshape(1, -1).astype(F32), ((0, 0), (0, LANES - v.shape[-1])))


def _pad8(w):
    return jnp.pad(w.astype(F32), ((0, 8 - w.shape[0]), (0, 0)))


def _head_select(dm):
    j = jnp.arange(LANES)[None, :, None]
    r = jnp.arange(LANES)[None, None, :]
    g = jnp.arange(N_GROUPS)[:, None, None]
    sel = ((j == dm.R * g + r) & (r < dm.R)).astype(F32)
    selcat = jnp.transpose(sel, (1, 0, 2)).reshape(LANES, N_GROUPS * LANES)
    selbig = jnp.transpose(sel[:, :, :8], (0, 2, 1)).reshape(N_GROUPS * 8, LANES)
    return selcat, selbig


def _mix_fwd(dm, h, w, sp, sel, tag):
    Bl, S, D = dm.Bl, dm.S, dm.D
    T = Bl * S
    proj = _matmul(h.reshape(T, D), w["w_in_p"], "NN", F32, tag + "_in_proj", tm=2048, tn=1152).reshape(Bl, S, dm.NP)
    xbc_a = _ssd_conv_fwd(proj, sp["ssd_conv_w8"], sp["ssd_conv_b"], dm.OX, dm.CD, tag + "_ssd_conv")
    dtg, acs, acs_t = _ssd_prep(proj, sp["dt_bias"], sp["a_log"], sel[0], sel[1], dm, tag + "_ssd_prep")
    y, yn, hprev = _scan_fwd(xbc_a, proj, dtg, acs, acs_t, sp["dsk_exp"], sp["ssd_norm_g"], dm, tag + "_ssd_scan")
    y_ssd = _matmul(yn.reshape(T, dm.DI), w["w_ssd_out"], "NN", F32, tag + "_ssd_out", tk=2048).reshape(Bl, S, D)
    s = _sc_conv_fwd(proj, sp["sc_conv_w8"], (dm.OB, dm.OC, dm.OH), D, tag + "_sc_conv")
    y_sc = _matmul(s.reshape(T, D), w["w_sc_out"], "NN", F32, tag + "_sc_out", tk=1024).reshape(Bl, S, D)
    mixin = _merge_fwd(proj, y_ssd, y_sc, dm.OG, D, tag + "_merge")
    mix = _matmul(mixin.reshape(T, D), w["w_o"], "NN", F32, tag + "_o", tk=1024).reshape(Bl, S, D)
    return mix, (h, proj, xbc_a, y, yn, hprev, y_ssd, y_sc, s, mixin, dtg, acs, acs_t)


def _mix_bwd(dm, dmix, saved, w, sp, sel, tag, gbig, slab):
    Bl, S, D = dm.Bl, dm.S, dm.D
    T = Bl * S
    h, proj, xbc_a, y, yn, hprev, y_ssd, y_sc, s, mixin, dtg, acs, acs_t = saved
    dmix2 = dmix.reshape(T, D)
    g = {}
    gbig["w_o"] = _matmul(mixin.reshape(T, D), dmix2, "TN", WIRE_DTYPE, tag + "_dw_o", slab=(gbig.get("w_o"),) + slab)
    dmixin = _matmul(dmix2, w["w_o"], "NT", F32, tag + "_d_o", tk=1024).reshape(Bl, S, D)
    dproj0 = jnp.zeros((Bl, S, dm.NP), MXU_DTYPE) if dm.GPAD else None
    dy_ssd, dy_sc, dproj = _merge_bwd(dmixin, proj, y_ssd, y_sc, dm.OG, D, dproj0, tag + "_merge_bwd")
    gbig["w_sc_out"] = _matmul(s.reshape(T, D), dy_sc.reshape(T, D), "TN", WIRE_DTYPE, tag + "_dw_sc_out",
                               slab=(gbig.get("w_sc_out"),) + slab)
    ds = _matmul(dy_sc.reshape(T, D), w["w_sc_out"], "NT", F32, tag + "_d_sc_out", tk=1024).reshape(Bl, S, D)
    dproj, dscw = _sc_conv_bwd(ds, proj, sp["sc_conv_w8"], (dm.OB, dm.OC, dm.OH), D, dproj, tag + "_sc_conv_bwd")
    g["sc_conv_w"] = dscw[:3]
    gbig["w_ssd_out"] = _matmul(yn.reshape(T, dm.DI), dy_ssd.reshape(T, D), "TN", WIRE_DTYPE, tag + "_dw_ssd_out",
                                slab=(gbig.get("w_ssd_out"),) + slab)
    dyn = _matmul(dy_ssd.reshape(T, D), w["w_ssd_out"], "NT", F32, tag + "_d_ssd_out", tk=1024).reshape(Bl, S, dm.DI)
    dproj, dxa, d_a, ddt, ddsk, dng = _scan_bwd(dyn, y, xbc_a, proj, hprev, dtg, acs, acs_t, sp["dsk_exp"],
                                                sp["ssd_norm_g"], dproj, dm, tag + "_ssd_scan_bwd")
    dproj, dpar = _ssd_post(d_a, ddt, dtg, proj, sp["dt_bias"], sp["a_log"], sel[0], dproj, dm, tag + "_ssd_post")
    g["ssd_dt_bias"], g["ssd_a_log"] = dpar[0, :dm.H], dpar[1, :dm.H]
    g["ssd_d"] = jnp.sum(ddsk.reshape(dm.H, HEAD_DIM), axis=-1)
    g["ssd_norm_g"] = dng[0]
    dproj, dcw, dcb = _ssd_conv_bwd(dxa, proj, sp["ssd_conv_w8"], sp["ssd_conv_b"], dm.OX, dm.CD, dproj,
                                    tag + "_ssd_conv_bwd")
    g["ssd_conv_w"], g["ssd_conv_b"] = _ungroup_xbc(dcw[:4], dm), _ungroup_xbc(dcb[0], dm)
    dproj = dproj.reshape(T, dm.NP)
    gbig["w_in_p"] = _matmul(h.reshape(T, D), dproj, "TN", WIRE_DTYPE, tag + "_dw_in", tn=1728,
                             slab=(gbig.get("w_in_p"),) + slab)
    dh = _matmul(dproj, w["w_in_p"], "NT", F32, tag + "_d_in", tm=2048, tk=1152).reshape(Bl, S, D)
    return dh, g


def _ffn_fwd(dm, h, w, sp, tag):
    Bl, S, D = dm.Bl, dm.S, dm.D
    T = Bl * S
    up = _matmul(h.reshape(T, D), w["w_up"], "NN", F32, tag + "_up", tm=2048, tn=1408).reshape(Bl, S, 2 * dm.DFF)
    a = _ffn_conv_fwd(up, sp["ffn_conv_w8"], sp["ffn_conv_b"], dm.DFF, tag + "_ffn_conv")
    f = _matmul(a.reshape(T, dm.DFF), w["w_down"], "NN", F32, tag + "_down", tk=2816).reshape(Bl, S, D)
    return f, (h, up, a)


def _ffn_bwd(dm, df, saved, w, sp, tag, gbig, slab):
    Bl, S, D = dm.Bl, dm.S, dm.D
    T = Bl * S
    h, up, a = saved
    df2 = df.reshape(T, D)
    g = {}
    gbig["w_down"] = _matmul(a.reshape(T, dm.DFF), df2, "TN", WIRE_DTYPE, tag + "_dw_down", tm=1408,
                             slab=(gbig.get("w_down"),) + slab)
    da = _matmul(df2, w["w_down"], "NT", F32, tag + "_d_down", tn=1408).reshape(Bl, S, dm.DFF)
    dup, dcw, dcb = _ffn_conv_bwd(da, up, sp["ffn_conv_w8"], sp["ffn_conv_b"], dm.DFF, tag + "_ffn_conv_bwd")
    g["ffn_conv_w"], g["ffn_conv_b"] = dcw[:3], dcb[0]
    dup = dup.reshape(T, 2 * dm.DFF)
    gbig["w_up"] = _matmul(h.reshape(T, D), dup, "TN", WIRE_DTYPE, tag + "_dw_up", tn=1408,
                           slab=(gbig.get("w_up"),) + slab)
    dh = _matmul(dup, w["w_up"], "NT", F32, tag + "_d_up", tm=2048, tk=1408).reshape(Bl, S, D)
    return dh, g


def _local_step(dm, x, c, target, wfull, small):
    Bl, S, D = dm.Bl, dm.S, dm.D
    depth = len(wfull)
    sel = _head_select(dm)
    c16 = jnp.pad(c.astype(F32), ((0, 16 - Bl), (0, 0)))
    sps, mods, acts = [], [], []
    for l in range(depth):
        sm = small[l]
        sps.append(dict(
            ssd_conv_w8=_pad8(_group_xbc(sm["ssd_conv_w"], dm)), ssd_conv_b=_group_xbc(sm["ssd_conv_b"], dm),
            dt_bias=_lane_pad(sm["ssd_dt_bias"]),
            a_log=_lane_pad(sm["ssd_a_log"]), dsk_exp=jnp.repeat(sm["ssd_d"].astype(F32), HEAD_DIM).reshape(1, dm.DI),
            ssd_norm_g=sm["ssd_norm_g"],
            sc_conv_w8=_pad8(sm["sc_conv_w"]), ffn_conv_w8=_pad8(sm["ffn_conv_w"]), ffn_conv_b=sm["ffn_conv_b"]))
        act, mod = _ada_fwd(c16, wfull[l]["ada_w"], sm["ada_b"], f"l{l}_ada")
        acts.append(act)
        mods.append(jnp.pad(mod[:Bl].reshape(Bl, 6, D), ((0, 0), (0, 2), (0, 0))))

    def sub(i):
        l, ffn = i // 2, i % 2
        sm = small[l]
        return dict(l=l, ffn=ffn, pre_g=sm["ffn_pre_g" if ffn else "mix_pre_g"],
                    post_g=sm["ffn_post_g" if ffn else "mix_post_g"], mod=mods[l], row=3 * ffn,
                    tag=f"l{l}_{'ffn' if ffn else 'mix'}")

    nsub = 2 * depth
    subs = [sub(i) for i in range(nsub)]
    xs, fs, saves = [x], [], []
    h = _norm_mod_fwd(x, subs[0]["pre_g"], subs[0]["mod"], subs[0]["row"], "l0_mix_pre_norm")
    for i, sb in enumerate(subs):
        l = sb["l"]
        if sb["ffn"]:
            f, sv = _ffn_fwd(dm, h, wfull[l], sps[l], sb["tag"])
        else:
            f, sv = _mix_fwd(dm, h, wfull[l], sps[l], sel, sb["tag"])
        nxt = None
        if i + 1 < nsub:
            nb = subs[i + 1]
            nxt = (nb["pre_g"], nb["mod"], nb["row"])
        xn, h = _post_norm_fwd(xs[-1], f, sb["post_g"], sb["mod"], sb["row"] + 2, nxt, sb["tag"] + "_post_norm")
        xs.append(xn)
        fs.append(f)
        saves.append(sv)

    dy, loss = _loss_fwd_bwd(xs[-1], target, "loss")

    grads = [dict() for _ in range(depth)]
    gbig = {}
    dmod = [[None] * 6 for _ in range(depth)]
    dx, dh = dy, None
    for i in reversed(range(nsub)):
        sb = subs[i]
        l = sb["l"]
        nxt = None
        if i + 1 < nsub:
            nb = subs[i + 1]
            nxt = (dh, xs[i + 1], nb["pre_g"], nb["mod"], nb["row"])
        dx, df, pb, shg = _norm_bwd(dx, nxt, (fs[i], sb["post_g"], sb["mod"], sb["row"] + 2), sb["tag"] + "_post_norm_bwd")
        if nxt is not None:
            nb = subs[i + 1]
            dmod[nb["l"]][nb["row"]], dmod[nb["l"]][nb["row"] + 1] = pb[:, 0], pb[:, 1]
            grads[nb["l"]]["ffn_pre_g" if nb["ffn"] else "mix_pre_g"] = shg[0]
        dmod[l][sb["row"] + 2] = pb[:, 2]
        grads[l]["ffn_post_g" if sb["ffn"] else "mix_post_g"] = shg[1]
        if sb["ffn"]:
            dh, g = _ffn_bwd(dm, df, saves[i], wfull[l], sps[l], sb["tag"], gbig, (l, depth))
        else:
            dh, g = _mix_bwd(dm, df, saves[i], wfull[l], sps[l], sel, sb["tag"], gbig, (l, depth))
        grads[l].update(g)
    sb = subs[0]
    grad_x, _, pb, shg = _norm_bwd(dx, (dh, xs[0], sb["pre_g"], sb["mod"], sb["row"]), None, "l0_mix_pre_norm_bwd")
    dmod[0][0], dmod[0][1] = pb[:, 0], pb[:, 1]
    grads[0]["mix_pre_g"] = shg[0]

    for l in range(depth):
        dm6 = jnp.concatenate(dmod[l], axis=-1)
        grads[l]["ada_b"] = jnp.sum(dm6, axis=0)
        dm16 = jnp.pad(dm6, ((0, 16 - Bl), (0, 0))).astype(MXU_DTYPE)
        gbig["ada_w"] = _matmul(acts[l], dm16, "TN", WIRE_DTYPE, f"l{l}_dw_ada", slab=(gbig.get("ada_w"), l, depth))
    return loss, grad_x, grads, gbig


_WEIGHTS = ("ada_w", "ada_b", "mix_pre_g", "mix_post_g", "w_in", "ssd_conv_w", "ssd_conv_b", "ssd_dt_bias",
            "ssd_a_log", "ssd_d", "ssd_norm_g", "w_ssd_out", "sc_conv_w", "w_sc_out", "w_o", "ffn_pre_g",
            "ffn_post_g", "w_up", "ffn_conv_w", "ffn_conv_b", "w_down")
_INPUTS = ("x", "c") + _WEIGHTS + ("loss_target",) + tuple("m_" + n for n in _WEIGHTS) + tuple("v_" + n for n in _WEIGHTS)
_BIG = {"ada_w": 2, "w_in": 2, "w_ssd_out": 1, "w_sc_out": 1, "w_o": 1, "w_up": 2, "w_down": 1}
_CONV = ("ssd_conv_w", "sc_conv_w", "ffn_conv_w")
_SMALL = tuple(n for n in _WEIGHTS if n not in _BIG)


def _step(a):
    x, c, target = a["x"], a["c"], a["loss_target"]
    depth = a["ada_w"].shape[0]
    dm = _dims(x, a["ssd_dt_bias"], a["ssd_norm_g"], a["ssd_conv_b"], a["ffn_conv_b"])
    chip = 2 * lax.axis_index("x") + lax.axis_index("y")

    shards = [a[n].astype(WIRE_DTYPE) for n in _BIG]
    axes = list(_BIG.values())
    kinds = [_gather_kind(s, ax) for s, ax in zip(shards, axes)]
    full = {n: w.astype(MXU_DTYPE) for n, w in zip(_BIG, _gather_weights(shards, axes, "gather_weights"))}
    conv_shapes = [a[n].shape for n in _CONV]
    gotc = _exchange_chips(_pack([a[n] for n in _CONV], F32, LANES, 8), False, "gather_conv_weights")
    piecesc = [_unpack(gotc[j], conv_shapes) for j in range(N_CHIPS)]
    fullc = {n: jnp.concatenate([piecesc[j][i] for j in range(N_CHIPS)], axis=2) for i, n in enumerate(_CONV)}

    wfull, small = [], []
    for l in range(depth):
        wf = {n: full[n][l] for n in _BIG if n != "w_in"}
        wf["w_in_p"] = _permute_w_in(full["w_in"][l], dm)
        wfull.append(wf)
        small.append({n: (fullc[n][l] if n in _CONV else a[n][l]) for n in _SMALL})

    loss_part, grad_x, grads, gbig = _local_step(dm, x, c, target, wfull, small)

    core = lax.axis_index("c").astype(jnp.int32).reshape(1)
    gst = []
    for n, kind in zip(_BIG, kinds):
        g = _unpermute_w_in(gbig["w_in_p"], dm) if n == "w_in" else gbig[n]
        if kind == "slots":
            g = jnp.moveaxis(g.reshape(depth, g.shape[1], N_CHIPS, g.shape[2] // N_CHIPS), 2, 1)
            g = g.reshape(depth, -1, g.shape[-1])
        gst.append(g)
    other = _swap_other_layer(gst, "swap_layer_grads")
    parts = [_pair_sum(g, o, core, "pair_sum_" + n) for g, o, n in zip(gst, other, _BIG)]
    parts = [p.reshape(N_CHIPS, -1, p.shape[-1]) if k == "slots" else p for p, k in zip(parts, kinds)]
    got = _scatter_layer(parts, kinds, "scatter_grads")
    mine = [_sum_slots(g, "sum_chip_grads_" + n) for g, n in zip(got, _BIG)]
    theirs = _sibling_swap(mine, "swap_core_grads")

    out = {}
    for i, n in enumerate(_BIG):
        out[n] = _adamw_layers(a[n], a["m_" + n], a["v_" + n], mine[i], theirs[i], core, "adamw_" + n)

    gsmall = [jnp.stack([grads[l][n] for l in range(depth)]) for n in _SMALL]
    small_shapes = [g.shape for g in gsmall]
    summed = _sum_slots(_allgather_all(_pack(gsmall, F32, LANES, 8), "gather_small_grads"), "sum_small_grads")
    gs = dict(zip(_SMALL, _unpack(summed, small_shapes)))
    for n in _CONV:
        wcols = a[n].shape[2]
        gs[n] = lax.dynamic_slice_in_dim(gs[n], chip * wcols, wcols, axis=2)
    local_shapes = [a[n].shape for n in _SMALL]
    res = _adamw(_pack([a[n] for n in _SMALL], F32, LANES, 8), _pack([a["m_" + n] for n in _SMALL], F32, LANES, 8),
                 _pack([a["v_" + n] for n in _SMALL], F32, LANES, 8), _pack([gs[n] for n in _SMALL], F32, LANES, 8),
                 "adamw_small")
    res = [_unpack(r, local_shapes) for r in res]
    for i, n in enumerate(_SMALL):
        out[n] = [r[i] for r in res]

    loss = lax.psum(loss_part, ("x", "y", "c"))
    return (loss, grad_x) + tuple(out[n][k] for k in range(4) for n in _WEIGHTS)


def kernel(x, c, ada_w, ada_b, mix_pre_g, mix_post_g, w_in, ssd_conv_w, ssd_conv_b, ssd_dt_bias, ssd_a_log, ssd_d, ssd_norm_g, w_ssd_out, sc_conv_w, w_sc_out, w_o, ffn_pre_g, ffn_post_g, w_up, ffn_conv_w, ffn_conv_b, w_down, loss_target, m_ada_w, m_ada_b, m_mix_pre_g, m_mix_post_g, m_w_in, m_ssd_conv_w, m_ssd_conv_b, m_ssd_dt_bias, m_ssd_a_log, m_ssd_d, m_ssd_norm_g, m_w_ssd_out, m_sc_conv_w, m_w_sc_out, m_w_o, m_ffn_pre_g, m_ffn_post_g, m_w_up, m_ffn_conv_w, m_ffn_conv_b, m_w_down, v_ada_w, v_ada_b, v_mix_pre_g, v_mix_post_g, v_w_in, v_ssd_conv_w, v_ssd_conv_b, v_ssd_dt_bias, v_ssd_a_log, v_ssd_d, v_ssd_norm_g, v_w_ssd_out, v_sc_conv_w, v_w_sc_out, v_w_o, v_ffn_pre_g, v_ffn_post_g, v_w_up, v_ffn_conv_w, v_ffn_conv_b, v_w_down):
    return _step(dict(zip(_INPUTS, (
        x, c, ada_w, ada_b, mix_pre_g, mix_post_g, w_in, ssd_conv_w, ssd_conv_b, ssd_dt_bias, ssd_a_log, ssd_d, ssd_norm_g, w_ssd_out, sc_conv_w, w_sc_out, w_o, ffn_pre_g, ffn_post_g, w_up, ffn_conv_w, ffn_conv_b, w_down, loss_target, m_ada_w, m_ada_b, m_mix_pre_g, m_mix_post_g, m_w_in, m_ssd_conv_w, m_ssd_conv_b, m_ssd_dt_bias, m_ssd_a_log, m_ssd_d, m_ssd_norm_g, m_w_ssd_out, m_sc_conv_w, m_w_sc_out, m_w_o, m_ffn_pre_g, m_ffn_post_g, m_w_up, m_ffn_conv_w, m_ffn_conv_b, m_w_down, v_ada_w, v_ada_b, v_mix_pre_g, v_mix_post_g, v_w_in, v_ssd_conv_w, v_ssd_conv_b, v_ssd_dt_bias, v_ssd_a_log, v_ssd_d, v_ssd_norm_g, v_w_ssd_out, v_sc_conv_w, v_w_sc_out, v_w_o, v_ffn_pre_g, v_ffn_post_g, v_w_up, v_ffn_conv_w, v_ffn_conv_b, v_w_down))))
```

```python
import math

import jax
import jax.numpy as jnp
from jax import lax
from jax.experimental import pallas as pl
from jax.experimental.pallas import tpu as pltpu

F32 = jnp.float32
MXU_DTYPE = jnp.bfloat16
WIRE_DTYPE = jnp.bfloat16
EPS = 1e-6
N_GROUPS = 4
CHUNK = 128
HEAD_DIM = 64
LANES = 128
HALO = 8
N_CHIPS = 4
N_DEV = 8
VMEM_LIMIT = 56 * 1024 * 1024
ADAM_LR, ADAM_B1, ADAM_B2, ADAM_EPS, ADAM_WD, ADAM_STEP = 0.001, 0.9, 0.999, 1e-08, 0.01, 10
MESH = pl.DeviceIdType.MESH

NN = (((1,), (0,)), ((), ()))
NT = (((1,), (1,)), ((), ()))
TN = (((0,), (0,)), ((), ()))


def _dg(a, b, dn, precision=None):
    return lax.dot_general(a, b, dn, precision=precision, preferred_element_type=F32)


def _tile(dim, pref, mult=LANES):
    t = (min(pref, dim) // mult) * mult
    while t >= mult:
        if dim % t == 0:
            return t
        t -= mult
    return dim


def _cp(sem):
    return pltpu.CompilerParams(dimension_semantics=sem, vmem_limit_bytes=VMEM_LIMIT)


def _sigmoid(x):
    return 1.0 / (1.0 + jnp.exp(-x))


def _softplus(x):
    return jnp.maximum(x, 0.0) + jnp.log1p(jnp.exp(-jnp.abs(x)))


def _matmul(a, b, mode, out_dtype, name, tm=1024, tn=1024, tk=1024, slab=None):
    if mode == "NN":
        (M, K), N = a.shape, b.shape[1]
    elif mode == "NT":
        (M, K), N = a.shape, b.shape[0]
    else:
        (K, M), N = a.shape, b.shape[1]
    tm, tn, tk = _tile(M, tm), _tile(N, tn), _tile(K, tk)
    nk = K // tk
    dn = {"NN": NN, "NT": NT, "TN": TN}[mode]
    carry = slab is not None and slab[0] is not None

    def body_one(a_ref, b_ref, *rest):
        o_ref = rest[-1]
        o_ref[...] = _dg(a_ref[...], b_ref[...], dn).astype(o_ref.dtype)

    def body_acc(a_ref, b_ref, *rest):
        o_ref, acc_ref = rest[-2:]
        k = pl.program_id(2)

        @pl.when(k == 0)
        def _():
            acc_ref[...] = jnp.zeros_like(acc_ref)

        acc_ref[...] += _dg(a_ref[...], b_ref[...], dn)

        @pl.when(k == nk - 1)
        def _():
            o_ref[...] = acc_ref[...].astype(o_ref.dtype)

    a_spec = (pl.BlockSpec((tk, tm), lambda i, j, k: (k, i)) if mode == "TN"
              else pl.BlockSpec((tm, tk), lambda i, j, k: (i, k)))
    b_spec = (pl.BlockSpec((tn, tk), lambda i, j, k: (j, k)) if mode == "NT"
              else pl.BlockSpec((tk, tn), lambda i, j, k: (k, j)))
    if slab is None:
        out_spec = pl.BlockSpec((tm, tn), lambda i, j, k: (i, j))
        out_shape = jax.ShapeDtypeStruct((M, N), out_dtype)
    else:
        layer = slab[1]
        out_spec = pl.BlockSpec((None, tm, tn), lambda i, j, k: (layer, i, j))
        out_shape = jax.ShapeDtypeStruct((slab[2], M, N), out_dtype)
    return pl.pallas_call(
        body_one if nk == 1 else body_acc, name=name, grid=(M // tm, N // tn, nk),
        in_specs=[a_spec, b_spec] + ([_ANY] if carry else []),
        out_specs=out_spec, out_shape=out_shape,
        input_output_aliases={2: 0} if carry else {},
        scratch_shapes=[] if nk == 1 else [pltpu.VMEM((tm, tn), F32)],
        compiler_params=_cp(("parallel", "parallel", "arbitrary")),
    )(*((a, b, slab[0]) if carry else (a, b)))


def _ada_fwd(c16, ada_w, ada_b, name):
    rows, D = c16.shape
    N6 = ada_w.shape[1]
    tn = _tile(N6, 1536)

    def body(c_ref, w_ref, b_ref, act_ref, mod_ref):
        c = c_ref[...]
        act = (c * _sigmoid(c)).astype(act_ref.dtype)
        act_ref[...] = act
        mod_ref[...] = _dg(act, w_ref[...], NN) + b_ref[...]

    return pl.pallas_call(
        body, name=name, grid=(N6 // tn,),
        in_specs=[pl.BlockSpec((rows, D), lambda j: (0, 0)),
                  pl.BlockSpec((D, tn), lambda j: (0, j)),
                  pl.BlockSpec((1, tn), lambda j: (0, j))],
        out_specs=[pl.BlockSpec((rows, D), lambda j: (0, 0)),
                   pl.BlockSpec((rows, tn), lambda j: (0, j))],
        out_shape=[jax.ShapeDtypeStruct((rows, D), MXU_DTYPE),
                   jax.ShapeDtypeStruct((rows, N6), F32)],
        compiler_params=_cp(("arbitrary",)),
    )(c16, ada_w, ada_b.reshape(1, N6))


def _norm_mod_rows(x, g, sc, sh):
    r = lax.rsqrt(jnp.mean(x * x, axis=-1, keepdims=True) + EPS)
    return ((x * r) * g) * (1.0 + sc) + sh


def _norm_mod_fwd(x, g, mod, row_sh, name, ts=512):
    Bl, S, D = x.shape
    ts = _tile(S, ts, 8)

    def body(x_ref, g_ref, mod_ref, h_ref):
        sh = mod_ref[row_sh:row_sh + 1, :]
        sc = mod_ref[row_sh + 1:row_sh + 2, :]
        h_ref[...] = _norm_mod_rows(x_ref[...], g_ref[...], sc, sh).astype(h_ref.dtype)

    tok = pl.BlockSpec((None, ts, D), lambda b, s: (b, s, 0))
    return pl.pallas_call(
        body, name=name, grid=(Bl, S // ts),
        in_specs=[tok, pl.BlockSpec((1, D), lambda b, s: (0, 0)),
                  pl.BlockSpec((None, 8, D), lambda b, s: (b, 0, 0))],
        out_specs=tok, out_shape=jax.ShapeDtypeStruct((Bl, S, D), MXU_DTYPE),
        compiler_params=_cp(("parallel", "parallel")),
    )(x, g.reshape(1, D), mod)


def _post_norm_fwd(xp, f, post_g, mod, row_gt, nxt, name, ts=512):
    Bl, S, D = xp.shape
    ts = _tile(S, ts, 8)
    has_next = nxt is not None

    def body(*refs):
        if has_next:
            xp_ref, f_ref, pg_ref, mod_ref, ng_ref, nmod_ref, x_ref, h_ref = refs
        else:
            xp_ref, f_ref, pg_ref, mod_ref, x_ref = refs
        f = f_ref[...]
        r = lax.rsqrt(jnp.mean(f * f, axis=-1, keepdims=True) + EPS)
        x = xp_ref[...] + mod_ref[row_gt:row_gt + 1, :] * ((f * r) * pg_ref[...])
        x_ref[...] = x
        if has_next:
            rs = nxt[2]
            h_ref[...] = _norm_mod_rows(x, ng_ref[...], nmod_ref[rs + 1:rs + 2, :], nmod_ref[rs:rs + 1, :]).astype(h_ref.dtype)

    tok = pl.BlockSpec((None, ts, D), lambda b, s: (b, s, 0))
    vec = pl.BlockSpec((1, D), lambda b, s: (0, 0))
    modspec = pl.BlockSpec((None, 8, D), lambda b, s: (b, 0, 0))
    ins = [xp, f, post_g.reshape(1, D), mod]
    in_specs = [tok, tok, vec, modspec]
    out_specs = [tok]
    out_shape = [jax.ShapeDtypeStruct((Bl, S, D), F32)]
    if has_next:
        ins += [nxt[0].reshape(1, D), nxt[1]]
        in_specs += [vec, modspec]
        out_specs += [tok]
        out_shape += [jax.ShapeDtypeStruct((Bl, S, D), MXU_DTYPE)]
    out = pl.pallas_call(
        body, name=name, grid=(Bl, S // ts), in_specs=in_specs, out_specs=out_specs, out_shape=out_shape,
        compiler_params=_cp(("parallel", "parallel")),
    )(*ins)
    return (out[0], out[1]) if has_next else (out[0], None)


def _loss_fwd_bwd(y, target, name, ts=512):
    Bl, S, D = y.shape
    ts = _tile(S, ts, 8)

    def body(y_ref, t_ref, dy_ref, l_ref):
        @pl.when((pl.program_id(0) == 0) & (pl.program_id(1) == 0))
        def _():
            l_ref[...] = jnp.zeros_like(l_ref)

        e = y_ref[...] - t_ref[...]
        dy_ref[...] = e * (1.0 / D)
        l_ref[...] += 0.5 * jnp.sum(jnp.mean(e * e, axis=-1, keepdims=True), axis=0, keepdims=True)

    tok = pl.BlockSpec((None, ts, D), lambda b, s: (b, s, 0))
    dy, l = pl.pallas_call(
        body, name=name, grid=(Bl, S // ts), in_specs=[tok, tok],
        out_specs=[tok, pl.BlockSpec((8, LANES), lambda b, s: (0, 0))],
        out_shape=[jax.ShapeDtypeStruct((Bl, S, D), F32), jax.ShapeDtypeStruct((8, LANES), F32)],
        compiler_params=_cp(("arbitrary", "arbitrary")),
    )(y, target)
    return dy, l[0, 0]


def _norm_bwd(dx_res, nxt, prv, name, ts=512):
    Bl, S, D = dx_res.shape
    ts = _tile(S, ts, 8)
    has_next, has_prev = nxt is not None, prv is not None

    def body(*refs):
        refs = list(refs)
        dxr_ref = refs.pop(0)
        if has_next:
            dh_ref, x_ref, g_ref, nmod_ref = refs[:4]
            refs = refs[4:]
        if has_prev:
            f_ref, pg_ref, pmod_ref = refs[:3]
            refs = refs[3:]
        dx_ref = refs.pop(0)
        if has_prev:
            df_ref = refs.pop(0)
        pb_ref, sh_ref = refs
        b, s = pl.program_id(0), pl.program_id(1)

        @pl.when(s == 0)
        def _():
            pb_ref[...] = jnp.zeros_like(pb_ref)

        @pl.when((b == 0) & (s == 0))
        def _():
            sh_ref[...] = jnp.zeros_like(sh_ref)

        dx = dxr_ref[...]
        if has_next:
            rs = nxt[4]
            x, dh, g = x_ref[...], dh_ref[...], g_ref[...]
            sc1 = 1.0 + nmod_ref[rs + 1:rs + 2, :]
            r = lax.rsqrt(jnp.mean(x * x, axis=-1, keepdims=True) + EPS)
            xn = x * r
            pb_ref[0:1, :] += jnp.sum(dh, axis=0, keepdims=True)
            pb_ref[1:2, :] += jnp.sum(dh * (xn * g), axis=0, keepdims=True)
            sh_ref[0:1, :] += jnp.sum(dh * sc1 * xn, axis=0, keepdims=True)
            dxn = dh * sc1 * g
            dx = dx + r * (dxn - xn * jnp.mean(dxn * xn, axis=-1, keepdims=True))
        dx_ref[...] = dx
        if has_prev:
            rg = prv[3]
            f, pg = f_ref[...], pg_ref[...]
            gt = pmod_ref[rg:rg + 1, :]
            r = lax.rsqrt(jnp.mean(f * f, axis=-1, keepdims=True) + EPS)
            fn = f * r
            pb_ref[2:3, :] += jnp.sum(dx * (fn * pg), axis=0, keepdims=True)
            drn = dx * gt
            sh_ref[1:2, :] += jnp.sum(drn * fn, axis=0, keepdims=True)
            dfn = drn * pg
            df_ref[...] = (r * (dfn - fn * jnp.mean(dfn * fn, axis=-1, keepdims=True))).astype(df_ref.dtype)

    tok = pl.BlockSpec((None, ts, D), lambda b, s: (b, s, 0))
    vec = pl.BlockSpec((1, D), lambda b, s: (0, 0))
    modspec = pl.BlockSpec((None, 8, D), lambda b, s: (b, 0, 0))
    ins, in_specs = [dx_res], [tok]
    if has_next:
        ins += [nxt[0], nxt[1], nxt[2].reshape(1, D), nxt[3]]
        in_specs += [tok, tok, vec, modspec]
    if has_prev:
        ins += [prv[0], prv[1].reshape(1, D), prv[2]]
        in_specs += [tok, vec, modspec]
    out_specs, out_shape = [tok], [jax.ShapeDtypeStruct((Bl, S, D), F32)]
    if has_prev:
        out_specs += [tok]
        out_shape += [jax.ShapeDtypeStruct((Bl, S, D), MXU_DTYPE)]
    out_specs += [modspec, pl.BlockSpec((8, D), lambda b, s: (0, 0))]
    out_shape += [jax.ShapeDtypeStruct((Bl, 8, D), F32), jax.ShapeDtypeStruct((8, D), F32)]
    out = pl.pallas_call(
        body, name=name, grid=(Bl, S // ts), in_specs=in_specs, out_specs=out_specs, out_shape=out_shape,
        compiler_params=_cp(("arbitrary", "arbitrary")),
    )(*ins)
    if has_prev:
        return out[0], out[1], out[2], out[3]
    return out[0], None, out[1], out[2]


def _shift_down(x, j):
    return x if j == 0 else pltpu.roll(x, j, axis=0)


def _shift_up(x, j):
    return x if j == 0 else pltpu.roll(x, x.shape[0] - j, axis=0)


def _conv(xall, w_ref, K):
    y = w_ref[K - 1:K, :] * xall
    for k in range(K - 1):
        y = y + w_ref[k:k + 1, :] * _shift_down(xall, K - 1 - k)
    return y


def _conv_t(dall, xall, w, K, rows):
    xt = xall[HALO:HALO + rows]
    y = w[K - 1:K, :] * dall[HALO:HALO + rows]
    gw = [None] * K
    gw[K - 1] = jnp.sum(dall[HALO:HALO + rows] * xt, axis=0, keepdims=True)
    for k in range(K - 1):
        sh = _shift_up(dall, K - 1 - k)[HALO:HALO + rows]
        y = y + w[k:k + 1, :] * sh
        gw[k] = jnp.sum(sh * xt, axis=0, keepdims=True)
    return y, gw


def _conv_t_rows(dall, w_ref, K):
    y = w_ref[K - 1:K, :] * dall
    for k in range(K - 1):
        y = y + w_ref[k:k + 1, :] * _shift_up(dall, K - 1 - k)
    return y


def _conv_wgrad(acc_ref, dtile, xall, K, ts):
    for k in range(K):
        xs = _shift_down(xall, K - 1 - k)[HALO:HALO + ts]
        acc_ref[k:k + 1, :] += jnp.sum(dtile * xs, axis=0, keepdims=True)


ROW_CHUNK = 128


def _lane_chunks(W):
    return [(j * LANES, LANES) for j in range(W // LANES)]


def _fill_ext(ext_ref, prev, tile, nxt, ts):
    ext_ref[0:HALO, :] = prev
    ext_ref[HALO:HALO + ts, :] = tile
    ext_ref[HALO + ts:2 * HALO + ts, :] = nxt


def _halo_specs(ts, W, nS, colblk):
    per = ts // HALO
    tile = pl.BlockSpec((None, ts, W), lambda b, c, s: (b, s, colblk(c)))
    prev = pl.BlockSpec((None, HALO, W), lambda b, c, s: (b, jnp.maximum(s * per - 1, 0), colblk(c)))
    nxt = pl.BlockSpec((None, HALO, W), lambda b, c, s: (b, jnp.minimum((s + 1) * per, nS * per - 1), colblk(c)))
    return tile, prev, nxt


def _masked(ref, keep):
    v = ref[...]
    return jnp.where(keep, v, jnp.zeros_like(v))


def _ssd_conv_fwd(proj, w8, b, off, CD, name, ts=512, W=1024):
    Bl, S, _ = proj.shape
    K = 4
    ts, W = _tile(S, ts, 8), _tile(math.gcd(CD, off), W)
    assert off % W == 0
    nS, nW, ob = S // ts, CD // W, off // W

    def body(x_ref, xp_ref, w_ref, b_ref, o_ref):
        xall = jnp.concatenate([_masked(xp_ref, pl.program_id(2) > 0), x_ref[...]], axis=0)
        xc = _conv(xall, w_ref, K)[HALO:] + b_ref[...]
        o_ref[...] = xc * _sigmoid(xc)

    tile, prev, _ = _halo_specs(ts, W, nS, lambda c: ob + c)
    return pl.pallas_call(
        body, name=name, grid=(Bl, nW, nS),
        in_specs=[tile, prev, pl.BlockSpec((8, W), lambda b_, c, s: (0, c)), pl.BlockSpec((1, W), lambda b_, c, s: (0, c))],
        out_specs=pl.BlockSpec((None, ts, W), lambda b_, c, s: (b_, s, c)),
        out_shape=jax.ShapeDtypeStruct((Bl, S, CD), F32),
        compiler_params=_cp(("parallel", "parallel", "parallel")),
    )(proj, proj, w8, b.reshape(1, CD))


def _ssd_conv_bwd(dxa, proj, w8, b, off, CD, dproj, name, ts=512, W=1024):
    Bl, S, _ = proj.shape
    K = 4
    ts, W = _tile(S, ts, 8), _tile(math.gcd(CD, off), W)
    nS, nW, ob = S // ts, CD // W, off // W

    rc = _tile(ts, ROW_CHUNK, 8)

    def body(d_ref, dn_ref, x_ref, xp_ref, xn_ref, w_ref, b_ref, _, dx_ref, dw_ref, db_ref, xext_ref, dext_ref):
        bb, s = pl.program_id(1), pl.program_id(2)

        @pl.when((bb == 0) & (s == 0))
        def _():
            dw_ref[...] = jnp.zeros_like(dw_ref)
            db_ref[...] = jnp.zeros_like(db_ref)

        last = s == nS - 1
        _fill_ext(xext_ref, _masked(xp_ref, s > 0), x_ref[...], _masked(xn_ref, ~last), ts)
        _fill_ext(dext_ref, jnp.zeros((HALO, W), F32), d_ref[...], _masked(dn_ref, ~last), ts)
        for l0, lc in _lane_chunks(W):
            w, bias = w_ref[:, l0:l0 + lc], b_ref[:, l0:l0 + lc]

            def chunk(i, acc):
                r0 = pl.multiple_of(i * rc, 8)
                xw = xext_ref[pl.ds(r0, rc + 2 * HALO), l0:l0 + lc]
                xc = _conv(xw, w, K) + bias
                sg = _sigmoid(xc)
                dxc = dext_ref[pl.ds(r0, rc + 2 * HALO), l0:l0 + lc] * (sg * (1.0 + xc * (1.0 - sg)))
                dx, gw = _conv_t(dxc, xw, w, K, rc)
                dx_ref[pl.ds(r0, rc), l0:l0 + lc] = dx.astype(dx_ref.dtype)
                gb = jnp.sum(dxc[HALO:HALO + rc], axis=0, keepdims=True)
                return tuple(a + g for a, g in zip(acc, gw + [gb]))

            acc = lax.fori_loop(0, ts // rc, chunk, tuple(jnp.zeros((1, lc), F32) for _ in range(K + 1)))
            for k in range(K):
                dw_ref[k:k + 1, l0:l0 + lc] += acc[k]
            db_ref[0:1, l0:l0 + lc] += acc[K]

    per = ts // HALO
    dtile_s = pl.BlockSpec((None, ts, W), lambda c, b_, s: (b_, s, c))
    dnext_s = pl.BlockSpec((None, HALO, W), lambda c, b_, s: (b_, jnp.minimum((s + 1) * per, nS * per - 1), c))
    xtile_s = pl.BlockSpec((None, ts, W), lambda c, b_, s: (b_, s, ob + c))
    xprev_s = pl.BlockSpec((None, HALO, W), lambda c, b_, s: (b_, jnp.maximum(s * per - 1, 0), ob + c))
    xnext_s = pl.BlockSpec((None, HALO, W), lambda c, b_, s: (b_, jnp.minimum((s + 1) * per, nS * per - 1), ob + c))
    wspec = pl.BlockSpec((8, W), lambda c, b_, s: (0, c))
    return pl.pallas_call(
        body, name=name, grid=(nW, Bl, nS),
        in_specs=[dtile_s, dnext_s, xtile_s, xprev_s, xnext_s, wspec, pl.BlockSpec((1, W), lambda c, b_, s: (0, c)),
                  _ANY],
        out_specs=[xtile_s, wspec, wspec],
        out_shape=[jax.ShapeDtypeStruct(dproj.shape, dproj.dtype), jax.ShapeDtypeStruct((8, CD), F32),
                   jax.ShapeDtypeStruct((8, CD), F32)],
        input_output_aliases={7: 0},
        scratch_shapes=[pltpu.VMEM((ts + 2 * HALO, W), F32)] * 2,
        compiler_params=_cp(("arbitrary", "arbitrary", "arbitrary")),
    )(dxa, dxa, proj, proj, proj, w8, b.reshape(1, CD), dproj)


def _sc_conv_fwd(proj, w8, offs, D, name, ts=512, W=1024):
    Bl, S, _ = proj.shape
    K = 3
    ts, W = _tile(S, ts, 8), _tile(D, W)
    nS, nW = S // ts, D // W
    ob, oc, oh = [o // W for o in offs]

    def body(b_ref, c_ref, cp_ref, h_ref, hp_ref, w_ref, o_ref):
        s = pl.program_id(2)
        keep = s > 0
        vall = jnp.concatenate([_masked(cp_ref, keep) * _masked(hp_ref, keep), c_ref[...] * h_ref[...]], axis=0)
        o_ref[...] = (b_ref[...] * _conv(vall, w_ref, K)[HALO:]).astype(o_ref.dtype)

    tb, _, _ = _halo_specs(ts, W, nS, lambda c: ob + c)
    tc, pc, _ = _halo_specs(ts, W, nS, lambda c: oc + c)
    th, ph, _ = _halo_specs(ts, W, nS, lambda c: oh + c)
    return pl.pallas_call(
        body, name=name, grid=(Bl, nW, nS),
        in_specs=[tb, tc, pc, th, ph, pl.BlockSpec((8, W), lambda b_, c, s: (0, c))],
        out_specs=pl.BlockSpec((None, ts, W), lambda b_, c, s: (b_, s, c)),
        out_shape=jax.ShapeDtypeStruct((Bl, S, D), MXU_DTYPE),
        compiler_params=_cp(("parallel", "parallel", "parallel")),
    )(proj, proj, proj, proj, proj, w8)


def _sc_conv_bwd(ds, proj, w8, offs, D, dproj, name, ts=256):
    Bl, S, _ = proj.shape
    K = 3
    ts, W = _tile(S, ts, 8), D
    nS, nW = S // ts, 1
    ob, oc, oh = [o // W for o in offs]
    assert offs[1] == offs[0] + D and offs[2] == offs[0] + 2 * D and offs[0] % (3 * D) == 0

    def body(d_ref, dn_ref, b_ref, bn_ref, c_ref, cp_ref, cn_ref, h_ref, hp_ref, hn_ref, w_ref, _,
             o_ref, dw_ref):
        db_ref, dc_ref, dh_ref = o_ref.at[:, 0:D], o_ref.at[:, D:2 * D], o_ref.at[:, 2 * D:3 * D]
        bb, s = pl.program_id(1), pl.program_id(2)

        @pl.when((bb == 0) & (s == 0))
        def _():
            dw_ref[...] = jnp.zeros_like(dw_ref)

        first, last = s > 0, s < nS - 1
        zeros = jnp.zeros((HALO, W), F32)
        c_t, h_t = c_ref[...], h_ref[...]
        vall = jnp.concatenate([_masked(cp_ref, first) * _masked(hp_ref, first), c_t * h_t,
                                _masked(cn_ref, last) * _masked(hn_ref, last)], axis=0)
        dcv = jnp.concatenate([zeros, d_ref[...] * b_ref[...], _masked(dn_ref, last) * _masked(bn_ref, last)], axis=0)
        cv = _conv(vall, w_ref, K)[HALO:HALO + ts]
        db_ref[...] = (d_ref[...] * cv).astype(db_ref.dtype)
        dv, gw = _conv_t(dcv, vall, w_ref, K, ts)
        dc_ref[...] = (dv * h_t).astype(dc_ref.dtype)
        dh_ref[...] = (dv * c_t).astype(dh_ref.dtype)
        for k in range(K):
            dw_ref[k:k + 1, :] += gw[k]

    per = ts // HALO

    def specs(o):
        t = pl.BlockSpec((None, ts, W), lambda c, b_, s: (b_, s, o + c))
        p = pl.BlockSpec((None, HALO, W), lambda c, b_, s: (b_, jnp.maximum(s * per - 1, 0), o + c))
        n = pl.BlockSpec((None, HALO, W), lambda c, b_, s: (b_, jnp.minimum((s + 1) * per, nS * per - 1), o + c))
        return t, p, n

    dt_, _, dn_ = specs(0)
    bt, _, bn = specs(ob)
    ct, cp, cn = specs(oc)
    ht, hp, hn = specs(oh)
    wspec = pl.BlockSpec((8, W), lambda c, b_, s: (0, c))
    o3 = offs[0] // (3 * D)
    return pl.pallas_call(
        body, name=name, grid=(nW, Bl, nS),
        in_specs=[dt_, dn_, bt, bn, ct, cp, cn, ht, hp, hn, wspec, _ANY],
        out_specs=[pl.BlockSpec((None, ts, 3 * D), lambda c, b_, s: (b_, s, o3)), wspec],
        out_shape=[jax.ShapeDtypeStruct(dproj.shape, dproj.dtype), jax.ShapeDtypeStruct((8, D), F32)],
        input_output_aliases={11: 0},
        compiler_params=_cp(("arbitrary", "arbitrary", "arbitrary")),
    )(ds, ds, proj, proj, proj, proj, proj, proj, proj, proj, w8, dproj)


def _ffn_conv_fwd(up, w8, b, DFF, name, ts=512, W=1408):
    Bl, S, _ = up.shape
    K = 3
    ts, W = _tile(S, ts, 8), _tile(DFF, W)
    nS, nW = S // ts, DFF // W

    def body(g_ref, gp_ref, v_ref, vp_ref, wg_ref, wv_ref, bg_ref, bv_ref, o_ref):
        keep = pl.program_id(2) > 0
        ug = _conv(jnp.concatenate([_masked(gp_ref, keep), g_ref[...]], axis=0), wg_ref, K)[HALO:] + bg_ref[...]
        uv = _conv(jnp.concatenate([_masked(vp_ref, keep), v_ref[...]], axis=0), wv_ref, K)[HALO:] + bv_ref[...]
        o_ref[...] = (ug * _sigmoid(ug) * uv).astype(o_ref.dtype)

    tg, pg, _ = _halo_specs(ts, W, nS, lambda c: c)
    tv, pv, _ = _halo_specs(ts, W, nS, lambda c: nW + c)
    wg = pl.BlockSpec((8, W), lambda b_, c, s: (0, c))
    wv = pl.BlockSpec((8, W), lambda b_, c, s: (0, nW + c))
    bg = pl.BlockSpec((1, W), lambda b_, c, s: (0, c))
    bv = pl.BlockSpec((1, W), lambda b_, c, s: (0, nW + c))
    b2 = b.reshape(1, 2 * DFF)
    return pl.pallas_call(
        body, name=name, grid=(Bl, nW, nS),
        in_specs=[tg, pg, tv, pv, wg, wv, bg, bv],
        out_specs=pl.BlockSpec((None, ts, W), lambda b_, c, s: (b_, s, c)),
        out_shape=jax.ShapeDtypeStruct((Bl, S, DFF), MXU_DTYPE),
        compiler_params=_cp(("parallel", "parallel", "parallel")),
    )(up, up, up, up, w8, w8, b2, b2)


def _ffn_conv_bwd(da, up, w8, b, DFF, name, ts=256):
    Bl, S, _ = up.shape
    K = 3
    ts, W = _tile(S, ts, 8), DFF
    nS = S // ts
    rc = _tile(ts, ROW_CHUNK, 8)

    def body(d_ref, dn_ref, g_ref, gp_ref, gn_ref, v_ref, vp_ref, vn_ref, w_ref, b_ref,
             dup_ref, dw_ref, db_ref, gext_ref, vext_ref, dext_ref):
        bb, s = pl.program_id(0), pl.program_id(1)

        @pl.when((bb == 0) & (s == 0))
        def _():
            dw_ref[...] = jnp.zeros_like(dw_ref)
            db_ref[...] = jnp.zeros_like(db_ref)

        first, last = s > 0, s < nS - 1
        _fill_ext(gext_ref, _masked(gp_ref, first), g_ref[...], _masked(gn_ref, last), ts)
        _fill_ext(vext_ref, _masked(vp_ref, first), v_ref[...], _masked(vn_ref, last), ts)
        _fill_ext(dext_ref, jnp.zeros((HALO, W), F32), d_ref[...], _masked(dn_ref, last), ts)
        for l0, lc in _lane_chunks(W):
            gl, vl = slice(l0, l0 + lc), slice(DFF + l0, DFF + l0 + lc)
            wg_, wv_, bg_, bv_ = w_ref[:, gl], w_ref[:, vl], b_ref[:, gl], b_ref[:, vl]

            def chunk(i, acc):
                r0 = pl.multiple_of(i * rc, 8)
                win = pl.ds(r0, rc + 2 * HALO)
                gw_, vw_, dw_ = gext_ref[win, gl], vext_ref[win, gl], dext_ref[win, gl]
                ug = _conv(gw_, wg_, K) + bg_
                uv = _conv(vw_, wv_, K) + bv_
                sg = _sigmoid(ug)
                dug = dw_ * uv * (sg * (1.0 + ug * (1.0 - sg)))
                duv = dw_ * (ug * sg)
                dg, gg = _conv_t(dug, gw_, wg_, K, rc)
                dv, gv = _conv_t(duv, vw_, wv_, K, rc)
                dup_ref[pl.ds(r0, rc), gl] = dg.astype(dup_ref.dtype)
                dup_ref[pl.ds(r0, rc), vl] = dv.astype(dup_ref.dtype)
                new = gg + gv + [jnp.sum(dug[HALO:HALO + rc], axis=0, keepdims=True),
                                 jnp.sum(duv[HALO:HALO + rc], axis=0, keepdims=True)]
                return tuple(a + g for a, g in zip(acc, new))

            acc = lax.fori_loop(0, ts // rc, chunk, tuple(jnp.zeros((1, lc), F32) for _ in range(2 * K + 2)))
            for k in range(K):
                dw_ref[k:k + 1, gl] += acc[k]
                dw_ref[k:k + 1, vl] += acc[K + k]
            db_ref[0:1, gl] += acc[2 * K]
            db_ref[0:1, vl] += acc[2 * K + 1]

    per = ts // HALO

    def specs(blk):
        t = pl.BlockSpec((None, ts, W), lambda b_, s: (b_, s, blk))
        p = pl.BlockSpec((None, HALO, W), lambda b_, s: (b_, jnp.maximum(s * per - 1, 0), blk))
        n = pl.BlockSpec((None, HALO, W), lambda b_, s: (b_, jnp.minimum((s + 1) * per, nS * per - 1), blk))
        return t, p, n

    dt_, _, dn_ = specs(0)
    gt, gp, gn = specs(0)
    vt, vp, vn = specs(1)
    small = pl.BlockSpec((8, 2 * DFF), lambda b_, s: (0, 0))
    return pl.pallas_call(
        body, name=name, grid=(Bl, nS),
        in_specs=[dt_, dn_, gt, gp, gn, vt, vp, vn, small, pl.BlockSpec((1, 2 * DFF), lambda b_, s: (0, 0))],
        out_specs=[pl.BlockSpec((None, ts, 2 * DFF), lambda b_, s: (b_, s, 0)), small, small],
        out_shape=[jax.ShapeDtypeStruct((Bl, S, 2 * DFF), MXU_DTYPE), jax.ShapeDtypeStruct((8, 2 * DFF), F32),
                   jax.ShapeDtypeStruct((8, 2 * DFF), F32)],
        scratch_shapes=[pltpu.VMEM((ts + 2 * HALO, W), F32)] * 3,
        compiler_params=_cp(("arbitrary", "arbitrary")),
    )(da, da, up, up, up, up, up, up, w8, b.reshape(1, 2 * DFF))


def _merge_fwd(proj, y_ssd, y_sc, off, D, name, ts=512):
    Bl, S, _ = proj.shape
    ts = _tile(S, ts, 8)
    og = off // D

    def body(g1_ref, g2_ref, a_ref, b_ref, o_ref):
        o_ref[...] = (_sigmoid(g1_ref[...]) * a_ref[...] + _sigmoid(g2_ref[...]) * b_ref[...]).astype(o_ref.dtype)

    tok = pl.BlockSpec((None, ts, D), lambda b, s: (b, s, 0))
    return pl.pallas_call(
        body, name=name, grid=(Bl, S // ts),
        in_specs=[pl.BlockSpec((None, ts, D), lambda b, s: (b, s, og)),
                  pl.BlockSpec((None, ts, D), lambda b, s: (b, s, og + 1)), tok, tok],
        out_specs=tok, out_shape=jax.ShapeDtypeStruct((Bl, S, D), MXU_DTYPE),
        compiler_params=_cp(("parallel", "parallel")),
    )(proj, proj, y_ssd, y_sc)


def _merge_bwd(dmixin, proj, y_ssd, y_sc, off, D, dproj0, name, ts=512):
    Bl, S, NP = proj.shape
    ts = _tile(S, ts, 8)
    og = off // D
    assert off % (2 * D) == 0

    def body(d_ref, g1_ref, g2_ref, a_ref, b_ref, *rest):
        da_ref, db_ref, dg_ref = rest[-3:]
        dg1_ref, dg2_ref = dg_ref.at[:, 0:D], dg_ref.at[:, D:2 * D]
        d = d_ref[...]
        s1, s2 = _sigmoid(g1_ref[...]), _sigmoid(g2_ref[...])
        da_ref[...] = (d * s1).astype(da_ref.dtype)
        db_ref[...] = (d * s2).astype(db_ref.dtype)
        dg1_ref[...] = (d * a_ref[...] * (s1 * (1.0 - s1))).astype(dg1_ref.dtype)
        dg2_ref[...] = (d * b_ref[...] * (s2 * (1.0 - s2))).astype(dg2_ref.dtype)

    tok = pl.BlockSpec((None, ts, D), lambda b, s: (b, s, 0))
    act = jax.ShapeDtypeStruct((Bl, S, D), MXU_DTYPE)
    carry = dproj0 is not None
    return pl.pallas_call(
        body, name=name, grid=(Bl, S // ts),
        in_specs=[tok, pl.BlockSpec((None, ts, D), lambda b, s: (b, s, og)),
                  pl.BlockSpec((None, ts, D), lambda b, s: (b, s, og + 1)), tok, tok] + ([_ANY] if carry else []),
        out_specs=[tok, tok, pl.BlockSpec((None, ts, 2 * D), lambda b, s: (b, s, og // 2))],
        out_shape=[act, act, jax.ShapeDtypeStruct((Bl, S, NP), MXU_DTYPE)],
        input_output_aliases={5: 2} if carry else {},
        compiler_params=_cp(("parallel", "parallel")),
    )(*((dmixin, proj, proj, y_ssd, y_sc) + ((dproj0,) if carry else ())))


def _exact_dot(a, b, dn, value):
    v = a if value == 0 else b
    m01 = (b if value == 0 else a).astype(MXU_DTYPE)
    hi = v.astype(MXU_DTYPE)
    r1 = v - hi.astype(F32)
    mid = r1.astype(MXU_DTYPE)
    lo = (r1 - mid.astype(F32)).astype(MXU_DTYPE)
    terms = [(t, m01) if value == 0 else (m01, t) for t in (hi, mid, lo)]
    return _dg(*terms[0], dn) + _dg(*terms[1], dn) + _dg(*terms[2], dn)


def _two_term_dot(v, m01):
    hi = v.astype(MXU_DTYPE)
    lo = (v - hi.astype(F32)).astype(MXU_DTYPE)
    return _dg(hi, m01, NN) + _dg(lo, m01, NN)


def _head_to_channels(R):
    rp = R * HEAD_DIM
    return (lax.shift_right_logical(lax.broadcasted_iota(jnp.int32, (LANES, rp), 1), 6)
            == lax.broadcasted_iota(jnp.int32, (LANES, rp), 0)).astype(MXU_DTYPE)


def _expand_heads(mats, R):
    L = mats[0].shape[0]
    out = _two_term_dot(jnp.concatenate(mats, axis=0), _head_to_channels(R))
    return [out[i * L:(i + 1) * L, :] for i in range(len(mats))]


def _expand_cols(mat, R, lane):
    half = lane < HEAD_DIM
    return jnp.concatenate(
        [jnp.where(half, mat[:, 2 * q:2 * q + 1], mat[:, 2 * q + 1:2 * q + 2]) for q in range(R // 2)], axis=1)


def _row_sums(v):
    return _two_term_dot(v, jnp.ones((v.shape[1], LANES), MXU_DTYPE))


def _lanes_to(v, width):
    return jnp.concatenate([v] * (width // LANES), axis=1)


def _head_rows(colvec, R, N):
    return jnp.concatenate([jnp.broadcast_to(colvec[r:r + 1, :], (HEAD_DIM, N)) for r in range(R)], axis=0)


def _ssd_prep(proj, dt_bias, a_log, selcat, selbig, dm, name):
    Bl, S, L, G = dm.Bl, dm.S, CHUNK, N_GROUPS
    nc = S // L
    odt = dm.ODT // LANES

    def body(dtr_ref, dtb_ref, alog_ref, selcat_ref, selbig_ref, dtg_ref, acs_ref, acst_ref):
        row = lax.broadcasted_iota(jnp.int32, (L, L), 0)
        col = lax.broadcasted_iota(jnp.int32, (L, L), 1)
        dt_all = _softplus(dtr_ref[...] + dtb_ref[...])
        acs_all = _exact_dot((row >= col).astype(F32), dt_all * (-jnp.exp(alog_ref[...])), NN, 1)
        dtg_ref[...] = _exact_dot(dt_all, selcat_ref[...], NN, 0)
        acs_ref[...] = _exact_dot(acs_all, selcat_ref[...], NN, 0)
        acst_ref[...] = _exact_dot(selbig_ref[...], acs_all, NT, 1)

    vec = pl.BlockSpec((1, LANES), lambda b, c: (0, 0))
    wide = pl.BlockSpec((None, L, G * LANES), lambda b, c: (b, c, 0))
    return pl.pallas_call(
        body, name=name, grid=(Bl, nc),
        in_specs=[pl.BlockSpec((None, L, LANES), lambda b, c: (b, c, odt)), vec, vec,
                  pl.BlockSpec((LANES, G * LANES), lambda b, c: (0, 0)),
                  pl.BlockSpec((G * 8, LANES), lambda b, c: (0, 0))],
        out_specs=[wide, wide, pl.BlockSpec((None, None, G * 8, L), lambda b, c: (b, c, 0, 0))],
        out_shape=[jax.ShapeDtypeStruct((Bl, S, G * LANES), F32), jax.ShapeDtypeStruct((Bl, S, G * LANES), F32),
                   jax.ShapeDtypeStruct((Bl, nc, G * 8, L), F32)],
        compiler_params=_cp(("parallel", "parallel")),
    )(proj, dt_bias, a_log, selcat, selbig)


def _ssd_post(d_a, ddt, dtg, proj, dt_bias, a_log, selcat, dproj, dm, name):
    Bl, S, L, G = dm.Bl, dm.S, CHUNK, N_GROUPS
    nc = S // L
    odt = dm.ODT // LANES

    def body(da_ref, ddt_ref, dtg_ref, dtr_ref, dtb_ref, alog_ref, selcat_ref, _, ddtr_ref, dpar_ref):
        @pl.when((pl.program_id(0) == 0) & (pl.program_id(1) == 0))
        def _():
            dpar_ref[...] = jnp.zeros_like(dpar_ref)

        row = lax.broadcasted_iota(jnp.int32, (L, L), 0)
        col = lax.broadcasted_iota(jnp.int32, (L, L), 1)
        selcat = selcat_ref[...]
        a_all = -jnp.exp(alog_ref[...])
        a4 = _exact_dot(jnp.broadcast_to(a_all, (8, LANES)), selcat, NN, 0)[0:1, :]
        dadt = _exact_dot((col >= row).astype(F32), da_ref[...], NN, 1)
        ddt4 = ddt_ref[...] + dadt * a4
        da4 = jnp.sum(dadt * dtg_ref[...], axis=0, keepdims=True)
        ddt_all = _exact_dot(ddt4, selcat, NT, 0)
        da_all = _exact_dot(jnp.broadcast_to(da4, (8, G * LANES)), selcat, NT, 0)[0:1, :]
        ddtr = ddt_all * _sigmoid(dtr_ref[...] + dtb_ref[...])
        ddtr_ref[...] = ddtr.astype(ddtr_ref.dtype)
        dpar_ref[0:1, :] += jnp.sum(ddtr, axis=0, keepdims=True)
        dpar_ref[1:2, :] += da_all * a_all

    vec = pl.BlockSpec((1, LANES), lambda b, c: (0, 0))
    wide = pl.BlockSpec((None, L, G * LANES), lambda b, c: (b, c, 0))
    return pl.pallas_call(
        body, name=name, grid=(Bl, nc),
        in_specs=[wide, wide, wide, pl.BlockSpec((None, L, LANES), lambda b, c: (b, c, odt)), vec, vec,
                  pl.BlockSpec((LANES, G * LANES), lambda b, c: (0, 0)), _ANY],
        out_specs=[pl.BlockSpec((None, L, LANES), lambda b, c: (b, c, odt)), pl.BlockSpec((8, LANES), lambda b, c: (0, 0))],
        out_shape=[jax.ShapeDtypeStruct(dproj.shape, dproj.dtype), jax.ShapeDtypeStruct((8, LANES), F32)],
        input_output_aliases={7: 0},
        compiler_params=_cp(("arbitrary", "arbitrary")),
    )(d_a, ddt, dtg, proj, dt_bias, a_log, selcat, dproj)


def _scan_fwd(xbc_a, proj, dtg4, acs4, acs_t4, dsk_exp, norm_g, dm, name):
    Bl, S, DI, N, R, L, G = dm.Bl, dm.S, dm.DI, dm.N, dm.R, CHUNK, N_GROUPS
    RP = R * HEAD_DIM
    nc = S // L

    def body(x_ref, z_ref, dtg_ref, acs_ref, acst_ref, dsk_ref, ng_ref, y_ref, yn_ref, hp_ref, h_ref):
        xs_ref, bm_ref, cm_ref = x_ref.at[:, 0:RP], x_ref.at[:, RP:RP + N], x_ref.at[:, RP + N:RP + 2 * N]
        @pl.when(pl.program_id(2) == 0)
        def _():
            h_ref[...] = jnp.zeros_like(h_ref)

        causal = lax.broadcasted_iota(jnp.int32, (L, L), 0) >= lax.broadcasted_iota(jnp.int32, (L, L), 1)
        lane = lax.broadcasted_iota(jnp.int32, (L, LANES), 1)
        dtg, acs, acs_t = dtg_ref[...], acs_ref[...], acst_ref[...]
        xs = xs_ref[...]
        bmb, cmb = bm_ref[...].astype(MXU_DTYPE), cm_ref[...].astype(MXU_DTYPE)
        sg = _dg(cmb, bmb, NT)
        acs_last = acs[L - 1:L, :]
        dt_exp, ea_exp, ds_exp = _expand_heads([dtg, jnp.exp(acs), jnp.exp(acs_last - acs)], R)
        xdt = xs * dt_exp
        xb = xdt.astype(MXU_DTYPE)
        parts = []
        for q in range(R // 2):
            x2 = xb[:, LANES * q:LANES * (q + 1)]
            ys = []
            for r in (2 * q, 2 * q + 1):
                dec = jnp.exp(jnp.where(causal, acs[:, r:r + 1] - acs_t[r:r + 1, :], -1e30))
                ys.append(_dg((sg * dec).astype(MXU_DTYPE), x2, NN))
            parts.append(jnp.where(lane < HEAD_DIM, ys[0], ys[1]))
        ydiag = jnp.concatenate(parts, axis=1)
        h_cur = h_ref[...]
        hb = h_cur.astype(MXU_DTYPE)
        yoff = _dg(cmb, hb, NT) * ea_exp
        st = _dg((xdt * ds_exp).astype(MXU_DTYPE), bmb, TN)
        hp_ref[...] = hb
        h_ref[...] = h_cur * _head_rows(jnp.exp(acs_t[:, L - 1:L]), R, N) + st
        y = ydiag + yoff + dsk_ref[...] * xs
        y_ref[...] = y
        z = z_ref[...]
        yg = y * (z * _sigmoid(z))
        rr = lax.rsqrt(_row_sums(yg * yg) * (1.0 / RP) + EPS)
        yn_ref[...] = (yg * _lanes_to(rr, RP) * ng_ref[...]).astype(yn_ref.dtype)

    oz = dm.OZ // RP
    grp = pl.BlockSpec((None, L, RP), lambda b, g, c: (b, c, g))
    lanes = pl.BlockSpec((None, L, LANES), lambda b, g, c: (b, c, g))
    chan = pl.BlockSpec((1, RP), lambda b, g, c: (0, g))
    return pl.pallas_call(
        body, name=name, grid=(Bl, G, nc),
        in_specs=[pl.BlockSpec((None, L, RP + 2 * N), lambda b, g, c: (b, c, g)),
                  pl.BlockSpec((None, L, RP), lambda b, g, c: (b, c, oz + g)), lanes, lanes,
                  pl.BlockSpec((None, None, 8, L), lambda b, g, c: (b, c, g, 0)),
                  chan, chan],
        out_specs=[grp, grp, pl.BlockSpec((None, None, None, RP, N), lambda b, g, c: (b, g, c, 0, 0))],
        out_shape=[jax.ShapeDtypeStruct((Bl, S, DI), F32), jax.ShapeDtypeStruct((Bl, S, DI), MXU_DTYPE),
                   jax.ShapeDtypeStruct((Bl, G, nc, RP, N), MXU_DTYPE)],
        scratch_shapes=[pltpu.VMEM((RP, N), F32)],
        compiler_params=_cp(("parallel", "parallel", "arbitrary")),
    )(xbc_a, proj, dtg4, acs4, acs_t4, dsk_exp, norm_g.reshape(1, DI))


def _scan_bwd(dyn, y, xbc_a, proj, hprev, dtg4, acs4, acs_t4, dsk_exp, norm_g, dproj, dm, name):
    Bl, S, DI, N, R, L, G = dm.Bl, dm.S, dm.DI, dm.N, dm.R, CHUNK, N_GROUPS
    RP = R * HEAD_DIM
    nc = S // L

    def body(dyn_ref, y_ref, z_ref, x_ref, hp_ref, dtg_ref, acs_ref, acst_ref, dsk_ref, ng_ref,
             _, dz_ref, dx_ref, da_ref, ddt_ref, ddsk_ref, dng_ref, dh_ref):
        xs_ref, bm_ref, cm_ref = x_ref.at[:, 0:RP], x_ref.at[:, RP:RP + N], x_ref.at[:, RP + N:RP + 2 * N]
        dxs_ref, dbm_ref, dcm_ref = dx_ref.at[:, 0:RP], dx_ref.at[:, RP:RP + N], dx_ref.at[:, RP + N:RP + 2 * N]
        b, c = pl.program_id(1), pl.program_id(2)

        @pl.when(c == 0)
        def _():
            dh_ref[...] = jnp.zeros_like(dh_ref)

        @pl.when((b == 0) & (c == 0))
        def _():
            ddsk_ref[...] = jnp.zeros_like(ddsk_ref)
            dng_ref[...] = jnp.zeros_like(dng_ref)

        row = lax.broadcasted_iota(jnp.int32, (L, L), 0)
        col = lax.broadcasted_iota(jnp.int32, (L, L), 1)
        causal, anti = row >= col, col >= row
        lane = lax.broadcasted_iota(jnp.int32, (L, LANES), 1)
        etb = (lax.shift_right_logical(lax.broadcasted_iota(jnp.int32, (RP, LANES), 0), 6)
               == lax.broadcasted_iota(jnp.int32, (RP, LANES), 1)).astype(MXU_DTYPE)

        dtg, acs, acs_t = dtg_ref[...], acs_ref[...], acst_ref[...]
        xs, z, y, dyn = xs_ref[...], z_ref[...], y_ref[...], dyn_ref[...]
        bmb, cmb = bm_ref[...].astype(MXU_DTYPE), cm_ref[...].astype(MXU_DTYPE)
        hpb = hp_ref[...]
        ng = ng_ref[...]

        sz = _sigmoid(z)
        siluz = z * sz
        yg = y * siluz
        rr = lax.rsqrt(jnp.mean(yg * yg, axis=-1, keepdims=True) + EPS)
        yhat = yg * rr
        dng_ref[...] += jnp.sum(dyn * yhat, axis=0, keepdims=True)
        dyhat = dyn * ng
        dyg = rr * (dyhat - yhat * jnp.mean(dyhat * yhat, axis=-1, keepdims=True))
        dy = dyg * siluz
        dz_ref[...] = (dyg * y * (sz * (1.0 + z * (1.0 - sz)))).astype(dz_ref.dtype)

        dxs = dy * dsk_ref[...]
        ddsk_ref[...] += jnp.sum(dy * xs, axis=0, keepdims=True)

        acs_last = acs[L - 1:L, :]
        dt_exp = _expand_cols(dtg, R, lane)
        ea_exp = _expand_cols(jnp.exp(acs), R, lane)
        ds_exp = _expand_cols(jnp.exp(acs_last - acs), R, lane)
        xdt = xs * dt_exp
        xb = xdt.astype(MXU_DTYPE)
        dyb = dy.astype(MXU_DTYPE)
        cd = jnp.exp(acs_last)
        cd_rows = _head_rows(jnp.exp(acs_t[:, L - 1:L]), R, N)

        q_ = _dg(cmb, hpb, NT)
        dq = dy * ea_exp
        dqb = dq.astype(MXU_DTYPE)
        dcm = _dg(dqb, hpb, NN)
        dh_yoff = _dg(dqb, cmb, TN)

        dhn = dh_ref[...]
        wprod = dhn * hpb.astype(F32)
        per_head = jnp.concatenate(
            [jnp.sum(wprod[HEAD_DIM * r:HEAD_DIM * (r + 1), :], axis=0, keepdims=True) for r in range(R)]
            + ([jnp.zeros((8 - R, N), F32)] if R < 8 else []), axis=0)
        dcd_col = jnp.sum(per_head, axis=1, keepdims=True)
        diag8 = lax.broadcasted_iota(jnp.int32, (8, LANES), 0) == lax.broadcasted_iota(jnp.int32, (8, LANES), 1)
        dcd_lane = jnp.sum(jnp.where(diag8, dcd_col, 0.0), axis=0, keepdims=True)
        d_a_last = dcd_lane * cd
        dh_ref[...] = dhn * cd_rows + dh_yoff
        dhnb = dhn.astype(MXU_DTYPE)

        e_ = _dg(bmb, dhnb, NT)
        dxdt = ds_exp * e_
        xds = xdt * ds_exp
        dbm = _dg(xds.astype(MXU_DTYPE), dhnb, NN)

        sg = _dg(cmb, bmb, NT)
        sg_t = _dg(bmb, cmb, NT)
        dsg = jnp.zeros((L, L), F32)
        dsg_t = jnp.zeros((L, L), F32)
        d_a = jnp.zeros((L, LANES), F32)
        parts = []
        for q in range(R // 2):
            x2 = xb[:, LANES * q:LANES * (q + 1)]
            dy2 = dyb[:, LANES * q:LANES * (q + 1)]
            dxs2 = []
            for hh, r in enumerate((2 * q, 2 * q + 1)):
                mine = (lane < HEAD_DIM) if hh == 0 else (lane >= HEAD_DIM)
                diff = acs[:, r:r + 1] - acs_t[r:r + 1, :]
                dec = jnp.exp(jnp.where(causal, diff, -1e30))
                dec_t = jnp.exp(jnp.where(anti, -diff, -1e30))
                dy2m = jnp.where(mine, dy2, jnp.zeros_like(dy2))
                dm_ = _dg(dy2m, x2, NT)
                dm_t = _dg(x2, dy2m, NT)
                m_t = sg_t * dec_t
                da_col = jnp.sum(dm_ * (sg * dec) - dm_t * m_t, axis=1, keepdims=True)
                d_a = d_a + jnp.where(lane == r, da_col, 0.0)
                dsg = dsg + dm_ * dec
                dsg_t = dsg_t + dm_t * dec_t
                dxs2.append(_dg(m_t.astype(MXU_DTYPE), dy2, NN))
            parts.append(jnp.where(lane < HEAD_DIM, dxs2[0], dxs2[1]))
        dxdt = dxdt + jnp.concatenate(parts, axis=1)
        dcm_ref[...] = dcm + _dg(dsg.astype(MXU_DTYPE), bmb, NN)
        dbm_ref[...] = dbm + _dg(dsg_t.astype(MXU_DTYPE), cmb, NN)
        dxs_ref[...] = dxs + dxdt * dt_exp

        hs = _two_term_dot(jnp.concatenate([dq * q_ - xds * e_, xds * e_, dxdt * xs], axis=0), etb)
        t2 = hs[L:2 * L, :]
        rowl = lax.broadcasted_iota(jnp.int32, (L, LANES), 0)
        d_a_last = d_a_last + jnp.sum(t2, axis=0, keepdims=True)
        da_ref[...] = d_a + hs[0:L, :] + jnp.where(rowl == L - 1, d_a_last, 0.0)
        ddt_ref[...] = hs[2 * L:3 * L, :]

    oz = dm.OZ // RP
    grp = pl.BlockSpec((None, L, RP), lambda g, b, c: (b, nc - 1 - c, g))
    zspec = pl.BlockSpec((None, L, RP), lambda g, b, c: (b, nc - 1 - c, oz + g))
    lanes = pl.BlockSpec((None, L, LANES), lambda g, b, c: (b, nc - 1 - c, g))
    xg = pl.BlockSpec((None, L, RP + 2 * N), lambda g, b, c: (b, nc - 1 - c, g))
    chan = pl.BlockSpec((1, RP), lambda g, b, c: (0, g))
    wide = jax.ShapeDtypeStruct((Bl, S, G * LANES), F32)
    return pl.pallas_call(
        body, name=name, grid=(G, Bl, nc),
        in_specs=[grp, grp, zspec, xg,
                  pl.BlockSpec((None, None, None, RP, N), lambda g, b, c: (b, g, nc - 1 - c, 0, 0)),
                  lanes, lanes, pl.BlockSpec((None, None, 8, L), lambda g, b, c: (b, nc - 1 - c, g, 0)),
                  chan, chan, _ANY],
        out_specs=[zspec, xg, lanes, lanes, chan, chan],
        out_shape=[jax.ShapeDtypeStruct(dproj.shape, dproj.dtype), jax.ShapeDtypeStruct(xbc_a.shape, F32),
                   wide, wide, jax.ShapeDtypeStruct((1, DI), F32), jax.ShapeDtypeStruct((1, DI), F32)],
        input_output_aliases={10: 0},
        scratch_shapes=[pltpu.VMEM((RP, N), F32)],
        compiler_params=_cp(("arbitrary", "arbitrary", "arbitrary")),
    )(dyn, y, proj, xbc_a, hprev, dtg4, acs4, acs_t4, dsk_exp, norm_g.reshape(1, DI), dproj)


def _adam_update(w, m, v, g):
    c1 = 1.0 - ADAM_B1 ** ADAM_STEP
    c2 = 1.0 - ADAM_B2 ** ADAM_STEP
    nm = ADAM_B1 * m + (1.0 - ADAM_B1) * g
    nv = ADAM_B2 * v + (1.0 - ADAM_B2) * (g * g)
    return nm, nv, -ADAM_LR * ((nm / c1) / (jnp.sqrt(nv / c2) + ADAM_EPS) + ADAM_WD * w)


def _adamw(w, m, v, g, name, tr=128):
    rows, cols = w.shape
    tr = _tile(rows, tr, 8)

    def body(w_ref, m_ref, v_ref, ga_ref, g_ref, d_ref, nm_ref, nv_ref):
        g = ga_ref[...]
        g_ref[...] = g
        nm_ref[...], nv_ref[...], d_ref[...] = _adam_update(w_ref[...], m_ref[...], v_ref[...], g)

    blk = pl.BlockSpec((tr, cols), lambda i: (i, 0))
    shp = jax.ShapeDtypeStruct((rows, cols), F32)
    return pl.pallas_call(
        body, name=name, grid=(rows // tr,), in_specs=[blk] * 4, out_specs=[blk] * 4,
        out_shape=[shp] * 4, compiler_params=_cp(("parallel",)),
    )(w, m, v, g)


def _adamw_layers(w, m, v, g_mine, g_theirs, core, name, tr=128):
    _, rows, cols = w.shape
    tr = _tile(rows, tr, 8)

    def body(c_ref, w_ref, m_ref, v_ref, ga_ref, gb_ref, g_ref, d_ref, nm_ref, nv_ref):
        g = jnp.where(pl.program_id(0) == c_ref[0], ga_ref[...], gb_ref[...])
        g_ref[...] = g
        nm_ref[...], nv_ref[...], d_ref[...] = _adam_update(w_ref[...], m_ref[...], v_ref[...], g)

    lay = pl.BlockSpec((None, tr, cols), lambda l, i, c_ref: (l, i, 0))
    one = pl.BlockSpec((tr, cols), lambda l, i, c_ref: (i, 0))
    shp = jax.ShapeDtypeStruct(w.shape, F32)
    return pl.pallas_call(
        body, name=name,
        grid_spec=pltpu.PrefetchScalarGridSpec(num_scalar_prefetch=1, grid=(2, rows // tr),
                                               in_specs=[lay, lay, lay, one, one], out_specs=[lay] * 4),
        out_shape=[shp] * 4, compiler_params=_cp(("parallel", "parallel")),
    )(core, w, m, v, g_mine, g_theirs)


def _sum_slots(buf, name, tr=256):
    n, rows, cols = buf.shape
    tr = _tile(rows, tr, 8)

    def body(b_ref, o_ref):
        acc = b_ref[0].astype(F32)
        for k in range(1, n):
            acc = acc + b_ref[k].astype(F32)
        o_ref[...] = acc

    return pl.pallas_call(
        body, name=name, grid=(rows // tr,),
        in_specs=[pl.BlockSpec((n, tr, cols), lambda i: (0, i, 0))],
        out_specs=pl.BlockSpec((tr, cols), lambda i: (i, 0)),
        out_shape=jax.ShapeDtypeStruct((rows, cols), F32), compiler_params=_cp(("parallel",)),
    )(buf)


_ANY = pl.BlockSpec(memory_space=pl.ANY)


def _exchange_chips(src, per_dest, name):
    rows, cols = src.shape[-2:]

    def body(in_ref, out_ref, send_sems, recv_sems, local_sem):
        x, y, c = lax.axis_index("x"), lax.axis_index("y"), lax.axis_index("c")
        me = 2 * x + y
        chips = [(1 - x, y), (x, 1 - y), (1 - x, 1 - y)]

        def block(j):
            return in_ref.at[j] if per_dest else in_ref

        mine = pltpu.make_async_copy(block(me), out_ref.at[me], local_sem)
        mine.start()
        sends = []
        for k, (px, py) in enumerate(chips):
            cp = pltpu.make_async_remote_copy(
                src_ref=block(2 * px + py), dst_ref=out_ref.at[me], send_sem=send_sems.at[k],
                recv_sem=recv_sems.at[k], device_id=(px, py, c), device_id_type=MESH)
            cp.start()
            sends.append(cp)
        for k, (px, py) in enumerate(chips):
            pltpu.make_async_remote_copy(
                src_ref=block(me), dst_ref=out_ref.at[2 * px + py], send_sem=send_sems.at[k],
                recv_sem=recv_sems.at[k], device_id=(px, py, c), device_id_type=MESH).wait_recv()
        for cp in sends:
            cp.wait_send()
        mine.wait()

    return pl.pallas_call(
        body, name=name, in_specs=[_ANY], out_specs=_ANY,
        out_shape=jax.ShapeDtypeStruct((N_CHIPS, rows, cols), src.dtype),
        scratch_shapes=[pltpu.SemaphoreType.DMA((3,)), pltpu.SemaphoreType.DMA((3,)), pltpu.SemaphoreType.DMA(())],
    )(src)


def _shard_window(ref, kind, j, lead=()):
    if kind == "slots":
        return ref.at[(j,) + lead]
    r, c = ref.shape[-2] // (N_CHIPS if kind == "rows" else 1), ref.shape[-1] // (N_CHIPS if kind == "cols" else 1)
    full = tuple(slice(None) for _ in range(len(ref.shape) - 2 - len(lead)))
    if kind == "rows":
        return ref.at[lead + full + (pl.ds(pl.multiple_of(j * r, 16), r), slice(None))]
    return ref.at[lead + full + (slice(None), pl.ds(pl.multiple_of(j * c, LANES), c))]


def _gather_kind(shard, axis):
    if axis == 1:
        return "rows"
    return "cols" if shard.shape[2] % LANES == 0 else "slots"


def _gather_weights(shards, axes, name):
    kinds = ["rows" if ax == 1 else "slots" for ax in axes]
    nw = len(shards)

    def out_shape(s, kind):
        d, r, c = s.shape
        shp = {"rows": (d, N_CHIPS * r, c), "cols": (d, r, N_CHIPS * c), "slots": (N_CHIPS, d, r, c)}[kind]
        return jax.ShapeDtypeStruct(shp, s.dtype)

    assert all(s.shape[0] == 2 for s in shards)

    def body(*refs):
        ins, outs = refs[:nw], refs[nw:2 * nw]
        ici_send, ici_recv, d2d_send, d2d_recv, local_sems = refs[2 * nw:]
        x, y, c = lax.axis_index("x"), lax.axis_index("y"), lax.axis_index("c")
        me = 2 * x + y
        chips = [(1 - x, y), (x, 1 - y), (1 - x, 1 - y)]
        sends = []
        for i in range(nw):
            if kinds[i] == "rows":
                own = pltpu.make_async_copy(ins[i], _shard_window(outs[i], kinds[i], me), local_sems.at[i])
                own.start()
                sends.append((own, False))
        for i in range(nw):
            for k, (px, py) in enumerate(chips):
                cp = pltpu.make_async_remote_copy(
                    src_ref=ins[i].at[c], dst_ref=_shard_window(outs[i], kinds[i], me, (c,)),
                    send_sem=ici_send.at[3 * i + k], recv_sem=ici_recv.at[3 * i + k],
                    device_id=(px, py, c), device_id_type=MESH)
                cp.start()
                sends.append((cp, True))
        for i in range(nw):
            for k, (px, py) in enumerate(chips):
                win = _shard_window(outs[i], kinds[i], 2 * px + py, (c,))
                pltpu.make_async_remote_copy(
                    src_ref=ins[i].at[c], dst_ref=win, send_sem=ici_send.at[3 * i + k], recv_sem=ici_recv.at[3 * i + k],
                    device_id=(px, py, c), device_id_type=MESH).wait_recv()
                fw = pltpu.make_async_remote_copy(
                    src_ref=win, dst_ref=win, send_sem=d2d_send.at[3 * i + k], recv_sem=d2d_recv.at[3 * i + k],
                    device_id=(x, y, 1 - c), device_id_type=MESH)
                fw.start()
                sends.append((fw, True))
        for i in range(nw):
            for k, (px, py) in enumerate(chips):
                win = _shard_window(outs[i], kinds[i], 2 * px + py, (1 - c,))
                pltpu.make_async_remote_copy(
                    src_ref=win, dst_ref=win, send_sem=d2d_send.at[3 * i + k], recv_sem=d2d_recv.at[3 * i + k],
                    device_id=(x, y, 1 - c), device_id_type=MESH).wait_recv()
        for cp, remote in sends:
            cp.wait_send() if remote else cp.wait()

    outs = pl.pallas_call(
        body, name=name, in_specs=[_ANY] * nw, out_specs=[_ANY] * nw,
        out_shape=[out_shape(s, k) for s, k in zip(shards, kinds)],
        scratch_shapes=[pltpu.SemaphoreType.DMA((3 * nw,)), pltpu.SemaphoreType.DMA((3 * nw,)),
                        pltpu.SemaphoreType.DMA((3 * nw,)), pltpu.SemaphoreType.DMA((3 * nw,)),
                        pltpu.SemaphoreType.DMA((nw,))],
    )(*shards)
    me = 2 * lax.axis_index("x") + lax.axis_index("y")
    return [jnp.concatenate([jnp.where(me == j, s, o[j]) for j in range(N_CHIPS)], axis=2) if k == "slots" else o
            for s, o, k in zip(shards, outs, kinds)]


def _swap_other_layer(gst, name):
    nv = len(gst)

    def body(*refs):
        ins, outs, send_sems, recv_sems = refs[:nv], refs[nv:2 * nv], refs[2 * nv], refs[2 * nv + 1]
        x, y, c = lax.axis_index("x"), lax.axis_index("y"), lax.axis_index("c")
        cps = [pltpu.make_async_remote_copy(src_ref=ins[i].at[1 - c], dst_ref=outs[i], send_sem=send_sems.at[i],
                                            recv_sem=recv_sems.at[i], device_id=(x, y, 1 - c), device_id_type=MESH)
               for i in range(nv)]
        for cp in cps:
            cp.start()
        for cp in cps:
            cp.wait()

    return pl.pallas_call(
        body, name=name, in_specs=[_ANY] * nv, out_specs=[_ANY] * nv,
        out_shape=[jax.ShapeDtypeStruct(v.shape[1:], v.dtype) for v in gst],
        scratch_shapes=[pltpu.SemaphoreType.DMA((nv,)), pltpu.SemaphoreType.DMA((nv,))],
    )(*gst)


def _pair_sum(g, other, core, name, tr=256):
    _, rows, cols = g.shape
    tr = _tile(rows, tr, 16)

    def body(c_ref, g_ref, o_ref, s_ref):
        s_ref[...] = (g_ref[...].astype(F32) + o_ref[...].astype(F32)).astype(s_ref.dtype)

    blk = pl.BlockSpec((tr, cols), lambda i, c_ref: (i, 0))
    return pl.pallas_call(
        body, name=name,
        grid_spec=pltpu.PrefetchScalarGridSpec(
            num_scalar_prefetch=1, grid=(rows // tr,),
            in_specs=[pl.BlockSpec((None, tr, cols), lambda i, c_ref: (c_ref[0], i, 0)), blk], out_specs=blk),
        out_shape=jax.ShapeDtypeStruct((rows, cols), g.dtype), compiler_params=_cp(("parallel",)),
    )(core, g, other)


def _scatter_layer(parts, kinds, name):
    nw = len(parts)

    def shard_shape(g, kind):
        return g.shape[-2] // (N_CHIPS if kind == "rows" else 1), g.shape[-1] // (N_CHIPS if kind == "cols" else 1)

    def body(*refs):
        ins, outs = refs[:nw], refs[nw:2 * nw]
        send_sems, recv_sems, local_sems = refs[2 * nw:]
        x, y, c = lax.axis_index("x"), lax.axis_index("y"), lax.axis_index("c")
        me = 2 * x + y
        chips = [(1 - x, y), (x, 1 - y), (1 - x, 1 - y)]
        sends = []
        for i in range(nw):
            own = pltpu.make_async_copy(_shard_window(ins[i], kinds[i], me), outs[i].at[me], local_sems.at[i])
            own.start()
            sends.append((own, False))
            for k, (px, py) in enumerate(chips):
                cp = pltpu.make_async_remote_copy(
                    src_ref=_shard_window(ins[i], kinds[i], 2 * px + py), dst_ref=outs[i].at[me],
                    send_sem=send_sems.at[3 * i + k], recv_sem=recv_sems.at[3 * i + k],
                    device_id=(px, py, c), device_id_type=MESH)
                cp.start()
                sends.append((cp, True))
        for i in range(nw):
            for k, (px, py) in enumerate(chips):
                pltpu.make_async_remote_copy(
                    src_ref=_shard_window(ins[i], kinds[i], me), dst_ref=outs[i].at[2 * px + py],
                    send_sem=send_sems.at[3 * i + k], recv_sem=recv_sems.at[3 * i + k],
                    device_id=(px, py, c), device_id_type=MESH).wait_recv()
        for cp, remote in sends:
            cp.wait_send() if remote else cp.wait()

    return pl.pallas_call(
        body, name=name, in_specs=[_ANY] * nw, out_specs=[_ANY] * nw,
        out_shape=[jax.ShapeDtypeStruct((N_CHIPS,) + shard_shape(g, k), g.dtype) for g, k in zip(parts, kinds)],
        scratch_shapes=[pltpu.SemaphoreType.DMA((3 * nw,)), pltpu.SemaphoreType.DMA((3 * nw,)),
                        pltpu.SemaphoreType.DMA((nw,))],
    )(*parts)


def _sibling_swap(vs, name):
    nv = len(vs)

    def body(*refs):
        ins, outs, send_sems, recv_sems = refs[:nv], refs[nv:2 * nv], refs[2 * nv], refs[2 * nv + 1]
        x, y, c = lax.axis_index("x"), lax.axis_index("y"), lax.axis_index("c")
        cps = [pltpu.make_async_remote_copy(src_ref=ins[i], dst_ref=outs[i], send_sem=send_sems.at[i],
                                            recv_sem=recv_sems.at[i], device_id=(x, y, 1 - c), device_id_type=MESH)
               for i in range(nv)]
        for cp in cps:
            cp.start()
        for cp in cps:
            cp.wait()

    return pl.pallas_call(
        body, name=name, in_specs=[_ANY] * nv, out_specs=[_ANY] * nv,
        out_shape=[jax.ShapeDtypeStruct(v.shape, v.dtype) for v in vs],
        scratch_shapes=[pltpu.SemaphoreType.DMA((nv,)), pltpu.SemaphoreType.DMA((nv,))],
    )(*vs)


def _allgather_all(v, name):
    rows, cols = v.shape

    def body(in_ref, out_ref, send_sems, recv_sems, local_sem):
        x, y, c = lax.axis_index("x"), lax.axis_index("y"), lax.axis_index("c")
        me = 4 * x + 2 * y + c
        peers = []
        for k in range(1, N_DEV):
            peers.append(((1 - x) if k & 4 else x, (1 - y) if k & 2 else y, (1 - c) if k & 1 else c))
        mine = pltpu.make_async_copy(in_ref, out_ref.at[me], local_sem)
        mine.start()
        sends = []
        for k, peer in enumerate(peers):
            cp = pltpu.make_async_remote_copy(src_ref=in_ref, dst_ref=out_ref.at[me], send_sem=send_sems.at[k],
                                              recv_sem=recv_sems.at[k], device_id=peer, device_id_type=MESH)
            cp.start()
            sends.append(cp)
        for k, (px, py, pc) in enumerate(peers):
            pltpu.make_async_remote_copy(src_ref=in_ref, dst_ref=out_ref.at[4 * px + 2 * py + pc],
                                         send_sem=send_sems.at[k], recv_sem=recv_sems.at[k],
                                         device_id=(px, py, pc), device_id_type=MESH).wait_recv()
        for cp in sends:
            cp.wait_send()
        mine.wait()

    return pl.pallas_call(
        body, name=name, in_specs=[_ANY], out_specs=_ANY,
        out_shape=jax.ShapeDtypeStruct((N_DEV, rows, cols), v.dtype),
        scratch_shapes=[pltpu.SemaphoreType.DMA((N_DEV - 1,)), pltpu.SemaphoreType.DMA((N_DEV - 1,)),
                        pltpu.SemaphoreType.DMA(())],
    )(v)


def _pack(arrs, dtype, width, row_mult):
    flat = jnp.concatenate([a.reshape(-1).astype(dtype) for a in arrs])
    unit = width * row_mult
    total = -(-flat.shape[0] // unit) * unit
    return jnp.pad(flat, (0, total - flat.shape[0])).reshape(-1, width)


def _unpack(buf, shapes):
    flat = buf.reshape(-1)
    out, off = [], 0
    for shp in shapes:
        n = 1
        for d in shp:
            n *= d
        out.append(flat[off:off + n].reshape(shp))
        off += n
    return out


class _Dims:
    pass


def _dims(x, ssd_dt_bias, ssd_norm_g, ssd_conv_b, ffn_conv_b):
    dm = _Dims()
    dm.Bl, dm.S, dm.D = x.shape
    dm.H, dm.DI, dm.CD = ssd_dt_bias.shape[-1], ssd_norm_g.shape[-1], ssd_conv_b.shape[-1]
    dm.N = (dm.CD - dm.DI) // (2 * N_GROUPS)
    dm.R = dm.H // N_GROUPS
    dm.DFF = ffn_conv_b.shape[-1] // 2
    D = dm.D
    dm.OB, dm.OC, dm.OH, dm.OX = 0, D, 2 * D, 3 * D
    dm.OZ = dm.OX + dm.CD
    dm.OG = -(-(dm.OZ + dm.DI) // (2 * D)) * (2 * D)
    dm.GPAD = dm.OG - (dm.OZ + dm.DI)
    dm.ODT = dm.OG + 2 * D
    dm.NP = dm.ODT + LANES
    assert dm.OZ % (dm.R * HEAD_DIM) == 0
    assert dm.DI // dm.H == HEAD_DIM and dm.N == LANES and dm.R % 2 == 0 and dm.S % CHUNK == 0 and dm.H <= LANES
    return dm


def _group_xbc(v, dm):
    rp, n = dm.R * HEAD_DIM, dm.N
    parts = []
    for g in range(N_GROUPS):
        parts += [v[..., g * rp:(g + 1) * rp], v[..., dm.DI + g * n:dm.DI + (g + 1) * n],
                  v[..., dm.DI + (N_GROUPS + g) * n:dm.DI + (N_GROUPS + g + 1) * n]]
    return jnp.concatenate(parts, axis=-1)


def _ungroup_xbc(v, dm):
    rp, n = dm.R * HEAD_DIM, dm.N
    gw = rp + 2 * n
    xs = [v[..., g * gw:g * gw + rp] for g in range(N_GROUPS)]
    bs = [v[..., g * gw + rp:g * gw + rp + n] for g in range(N_GROUPS)]
    cs = [v[..., g * gw + rp + n:(g + 1) * gw] for g in range(N_GROUPS)]
    return jnp.concatenate(xs + bs + cs, axis=-1)


def _permute_w_in(w, dm):
    o, sc = dm.DI + dm.CD, dm.DI + dm.CD + dm.H
    zeros = lambda n: jnp.zeros((w.shape[0], n), w.dtype)
    return jnp.concatenate([w[:, sc:sc + 3 * dm.D], _group_xbc(w[:, dm.DI:o], dm), w[:, :dm.DI], zeros(dm.GPAD),
                            w[:, sc + 3 * dm.D:], w[:, o:o + dm.H], zeros(LANES - dm.H)], axis=1)


def _unpermute_w_in(dw, dm):
    return jnp.concatenate([dw[..., dm.OZ:dm.OZ + dm.DI], _ungroup_xbc(dw[..., dm.OX:dm.OX + dm.CD], dm),
                            dw[..., dm.ODT:dm.ODT + dm.H], dw[..., :3 * dm.D], dw[..., dm.OG:dm.OG + 2 * dm.D]], axis=-1)


def _lane_pad(v):
    return jnp.pad(v.reshape(1, -1).astype(F32), ((0, 0), (0, LANES - v.shape[-1])))


def _pad8(w):
    return jnp.pad(w.astype(F32), ((0, 8 - w.shape[0]), (0, 0)))


def _head_select(dm):
    j = jnp.arange(LANES)[None, :, None]
    r = jnp.arange(LANES)[None, None, :]
    g = jnp.arange(N_GROUPS)[:, None, None]
    sel = ((j == dm.R * g + r) & (r < dm.R)).astype(F32)
    selcat = jnp.transpose(sel, (1, 0, 2)).reshape(LANES, N_GROUPS * LANES)
    selbig = jnp.transpose(sel[:, :, :8], (0, 2, 1)).reshape(N_GROUPS * 8, LANES)
    return selcat, selbig


def _mix_fwd(dm, h, w, sp, sel, tag):
    Bl, S, D = dm.Bl, dm.S, dm.D
    T = Bl * S
    proj = _matmul(h.reshape(T, D), w["w_in_p"], "NN", F32, tag + "_in_proj", tm=2048, tn=1152).reshape(Bl, S, dm.NP)
    xbc_a = _ssd_conv_fwd(proj, sp["ssd_conv_w8"], sp["ssd_conv_b"], dm.OX, dm.CD, tag + "_ssd_conv")
    dtg, acs, acs_t = _ssd_prep(proj, sp["dt_bias"], sp["a_log"], sel[0], sel[1], dm, tag + "_ssd_prep")
    y, yn, hprev = _scan_fwd(xbc_a, proj, dtg, acs, acs_t, sp["dsk_exp"], sp["ssd_norm_g"], dm, tag + "_ssd_scan")
    y_ssd = _matmul(yn.reshape(T, dm.DI), w["w_ssd_out"], "NN", F32, tag + "_ssd_out", tk=2048).reshape(Bl, S, D)
    s = _sc_conv_fwd(proj, sp["sc_conv_w8"], (dm.OB, dm.OC, dm.OH), D, tag + "_sc_conv")
    y_sc = _matmul(s.reshape(T, D), w["w_sc_out"], "NN", F32, tag + "_sc_out", tk=1024).reshape(Bl, S, D)
    mixin = _merge_fwd(proj, y_ssd, y_sc, dm.OG, D, tag + "_merge")
    mix = _matmul(mixin.reshape(T, D), w["w_o"], "NN", F32, tag + "_o", tk=1024).reshape(Bl, S, D)
    return mix, (h, proj, xbc_a, y, yn, hprev, y_ssd, y_sc, s, mixin, dtg, acs, acs_t)


def _mix_bwd(dm, dmix, saved, w, sp, sel, tag, gbig, slab):
    Bl, S, D = dm.Bl, dm.S, dm.D
    T = Bl * S
    h, proj, xbc_a, y, yn, hprev, y_ssd, y_sc, s, mixin, dtg, acs, acs_t = saved
    dmix2 = dmix.reshape(T, D)
    g = {}
    gbig["w_o"] = _matmul(mixin.reshape(T, D), dmix2, "TN", WIRE_DTYPE, tag + "_dw_o", slab=(gbig.get("w_o"),) + slab)
    dmixin = _matmul(dmix2, w["w_o"], "NT", F32, tag + "_d_o", tk=1024).reshape(Bl, S, D)
    dproj0 = jnp.zeros((Bl, S, dm.NP), MXU_DTYPE) if dm.GPAD else None
    dy_ssd, dy_sc, dproj = _merge_bwd(dmixin, proj, y_ssd, y_sc, dm.OG, D, dproj0, tag + "_merge_bwd")
    gbig["w_sc_out"] = _matmul(s.reshape(T, D), dy_sc.reshape(T, D), "TN", WIRE_DTYPE, tag + "_dw_sc_out",
                               slab=(gbig.get("w_sc_out"),) + slab)
    ds = _matmul(dy_sc.reshape(T, D), w["w_sc_out"], "NT", F32, tag + "_d_sc_out", tk=1024).reshape(Bl, S, D)
    dproj, dscw = _sc_conv_bwd(ds, proj, sp["sc_conv_w8"], (dm.OB, dm.OC, dm.OH), D, dproj, tag + "_sc_conv_bwd")
    g["sc_conv_w"] = dscw[:3]
    gbig["w_ssd_out"] = _matmul(yn.reshape(T, dm.DI), dy_ssd.reshape(T, D), "TN", WIRE_DTYPE, tag + "_dw_ssd_out",
                                slab=(gbig.get("w_ssd_out"),) + slab)
    dyn = _matmul(dy_ssd.reshape(T, D), w["w_ssd_out"], "NT", F32, tag + "_d_ssd_out", tk=1024).reshape(Bl, S, dm.DI)
    dproj, dxa, d_a, ddt, ddsk, dng = _scan_bwd(dyn, y, xbc_a, proj, hprev, dtg, acs, acs_t, sp["dsk_exp"],
                                                sp["ssd_norm_g"], dproj, dm, tag + "_ssd_scan_bwd")
    dproj, dpar = _ssd_post(d_a, ddt, dtg, proj, sp["dt_bias"], sp["a_log"], sel[0], dproj, dm, tag + "_ssd_post")
    g["ssd_dt_bias"], g["ssd_a_log"] = dpar[0, :dm.H], dpar[1, :dm.H]
    g["ssd_d"] = jnp.sum(ddsk.reshape(dm.H, HEAD_DIM), axis=-1)
    g["ssd_norm_g"] = dng[0]
    dproj, dcw, dcb = _ssd_conv_bwd(dxa, proj, sp["ssd_conv_w8"], sp["ssd_conv_b"], dm.OX, dm.CD, dproj,
                                    tag + "_ssd_conv_bwd")
    g["ssd_conv_w"], g["ssd_conv_b"] = _ungroup_xbc(dcw[:4], dm), _ungroup_xbc(dcb[0], dm)
    dproj = dproj.reshape(T, dm.NP)
    gbig["w_in_p"] = _matmul(h.reshape(T, D), dproj, "TN", WIRE_DTYPE, tag + "_dw_in", tn=1728, tk=2048,
                             slab=(gbig.get("w_in_p"),) + slab)
    dh = _matmul(dproj, w["w_in_p"], "NT", F32, tag + "_d_in", tm=2048, tk=1152).reshape(Bl, S, D)
    return dh, g


def _ffn_fwd(dm, h, w, sp, tag):
    Bl, S, D = dm.Bl, dm.S, dm.D
    T = Bl * S
    up = _matmul(h.reshape(T, D), w["w_up"], "NN", F32, tag + "_up", tm=2048, tn=1408).reshape(Bl, S, 2 * dm.DFF)
    a = _ffn_conv_fwd(up, sp["ffn_conv_w8"], sp["ffn_conv_b"], dm.DFF, tag + "_ffn_conv")
    f = _matmul(a.reshape(T, dm.DFF), w["w_down"], "NN", F32, tag + "_down", tk=2816).reshape(Bl, S, D)
    return f, (h, up, a)


def _ffn_bwd(dm, df, saved, w, sp, tag, gbig, slab):
    Bl, S, D = dm.Bl, dm.S, dm.D
    T = Bl * S
    h, up, a = saved
    df2 = df.reshape(T, D)
    g = {}
    gbig["w_down"] = _matmul(a.reshape(T, dm.DFF), df2, "TN", WIRE_DTYPE, tag + "_dw_down", tm=1408, tk=2048,
                             slab=(gbig.get("w_down"),) + slab)
    da = _matmul(df2, w["w_down"], "NT", F32, tag + "_d_down", tn=1408).reshape(Bl, S, dm.DFF)
    dup, dcw, dcb = _ffn_conv_bwd(da, up, sp["ffn_conv_w8"], sp["ffn_conv_b"], dm.DFF, tag + "_ffn_conv_bwd")
    g["ffn_conv_w"], g["ffn_conv_b"] = dcw[:3], dcb[0]
    dup = dup.reshape(T, 2 * dm.DFF)
    gbig["w_up"] = _matmul(h.reshape(T, D), dup, "TN", WIRE_DTYPE, tag + "_dw_up", tn=1408, tk=2048,
                           slab=(gbig.get("w_up"),) + slab)
    dh = _matmul(dup, w["w_up"], "NT", F32, tag + "_d_up", tm=2048, tk=1408).reshape(Bl, S, D)
    return dh, g


def _local_step(dm, x, c, target, wfull, small):
    Bl, S, D = dm.Bl, dm.S, dm.D
    depth = len(wfull)
    sel = _head_select(dm)
    c16 = jnp.pad(c.astype(F32), ((0, 16 - Bl), (0, 0)))
    sps, mods, acts = [], [], []
    for l in range(depth):
        sm = small[l]
        sps.append(dict(
            ssd_conv_w8=_pad8(_group_xbc(sm["ssd_conv_w"], dm)), ssd_conv_b=_group_xbc(sm["ssd_conv_b"], dm),
            dt_bias=_lane_pad(sm["ssd_dt_bias"]),
            a_log=_lane_pad(sm["ssd_a_log"]), dsk_exp=jnp.repeat(sm["ssd_d"].astype(F32), HEAD_DIM).reshape(1, dm.DI),
            ssd_norm_g=sm["ssd_norm_g"],
            sc_conv_w8=_pad8(sm["sc_conv_w"]), ffn_conv_w8=_pad8(sm["ffn_conv_w"]), ffn_conv_b=sm["ffn_conv_b"]))
        act, mod = _ada_fwd(c16, wfull[l]["ada_w"], sm["ada_b"], f"l{l}_ada")
        acts.append(act)
        mods.append(jnp.pad(mod[:Bl].reshape(Bl, 6, D), ((0, 0), (0, 2), (0, 0))))

    def sub(i):
        l, ffn = i // 2, i % 2
        sm = small[l]
        return dict(l=l, ffn=ffn, pre_g=sm["ffn_pre_g" if ffn else "mix_pre_g"],
                    post_g=sm["ffn_post_g" if ffn else "mix_post_g"], mod=mods[l], row=3 * ffn,
                    tag=f"l{l}_{'ffn' if ffn else 'mix'}")

    nsub = 2 * depth
    subs = [sub(i) for i in range(nsub)]
    xs, fs, saves = [x], [], []
    h = _norm_mod_fwd(x, subs[0]["pre_g"], subs[0]["mod"], subs[0]["row"], "l0_mix_pre_norm")
    for i, sb in enumerate(subs):
        l = sb["l"]
        if sb["ffn"]:
            f, sv = _ffn_fwd(dm, h, wfull[l], sps[l], sb["tag"])
        else:
            f, sv = _mix_fwd(dm, h, wfull[l], sps[l], sel, sb["tag"])
        nxt = None
        if i + 1 < nsub:
            nb = subs[i + 1]
            nxt = (nb["pre_g"], nb["mod"], nb["row"])
        xn, h = _post_norm_fwd(xs[-1], f, sb["post_g"], sb["mod"], sb["row"] + 2, nxt, sb["tag"] + "_post_norm")
        xs.append(xn)
        fs.append(f)
        saves.append(sv)

    dy, loss = _loss_fwd_bwd(xs[-1], target, "loss")

    grads = [dict() for _ in range(depth)]
    gbig = {}
    dmod = [[None] * 6 for _ in range(depth)]
    dx, dh = dy, None
    for i in reversed(range(nsub)):
        sb = subs[i]
        l = sb["l"]
        nxt = None
        if i + 1 < nsub:
            nb = subs[i + 1]
            nxt = (dh, xs[i + 1], nb["pre_g"], nb["mod"], nb["row"])
        dx, df, pb, shg = _norm_bwd(dx, nxt, (fs[i], sb["post_g"], sb["mod"], sb["row"] + 2), sb["tag"] + "_post_norm_bwd")
        if nxt is not None:
            nb = subs[i + 1]
            dmod[nb["l"]][nb["row"]], dmod[nb["l"]][nb["row"] + 1] = pb[:, 0], pb[:, 1]
            grads[nb["l"]]["ffn_pre_g" if nb["ffn"] else "mix_pre_g"] = shg[0]
        dmod[l][sb["row"] + 2] = pb[:, 2]
        grads[l]["ffn_post_g" if sb["ffn"] else "mix_post_g"] = shg[1]
        if sb["ffn"]:
            dh, g = _ffn_bwd(dm, df, saves[i], wfull[l], sps[l], sb["tag"], gbig, (l, depth))
        else:
            dh, g = _mix_bwd(dm, df, saves[i], wfull[l], sps[l], sel, sb["tag"], gbig, (l, depth))
        grads[l].update(g)
    sb = subs[0]
    grad_x, _, pb, shg = _norm_bwd(dx, (dh, xs[0], sb["pre_g"], sb["mod"], sb["row"]), None, "l0_mix_pre_norm_bwd")
    dmod[0][0], dmod[0][1] = pb[:, 0], pb[:, 1]
    grads[0]["mix_pre_g"] = shg[0]

    for l in range(depth):
        dm6 = jnp.concatenate(dmod[l], axis=-1)
        grads[l]["ada_b"] = jnp.sum(dm6, axis=0)
        dm16 = jnp.pad(dm6, ((0, 16 - Bl), (0, 0))).astype(MXU_DTYPE)
        gbig["ada_w"] = _matmul(acts[l], dm16, "TN", WIRE_DTYPE, f"l{l}_dw_ada", slab=(gbig.get("ada_w"), l, depth))
    return loss, grad_x, grads, gbig


_WEIGHTS = ("ada_w", "ada_b", "mix_pre_g", "mix_post_g", "w_in", "ssd_conv_w", "ssd_conv_b", "ssd_dt_bias",
            "ssd_a_log", "ssd_d", "ssd_norm_g", "w_ssd_out", "sc_conv_w", "w_sc_out", "w_o", "ffn_pre_g",
            "ffn_post_g", "w_up", "ffn_conv_w", "ffn_conv_b", "w_down")
_INPUTS = ("x", "c") + _WEIGHTS + ("loss_target",) + tuple("m_" + n for n in _WEIGHTS) + tuple("v_" + n for n in _WEIGHTS)
_BIG = {"ada_w": 2, "w_in": 2, "w_ssd_out": 1, "w_sc_out": 1, "w_o": 1, "w_up": 2, "w_down": 1}
_CONV = ("ssd_conv_w", "sc_conv_w", "ffn_conv_w")
_SMALL = tuple(n for n in _WEIGHTS if n not in _BIG)


def _step(a):
    x, c, target = a["x"], a["c"], a["loss_target"]
    depth = a["ada_w"].shape[0]
    dm = _dims(x, a["ssd_dt_bias"], a["ssd_norm_g"], a["ssd_conv_b"], a["ffn_conv_b"])
    chip = 2 * lax.axis_index("x") + lax.axis_index("y")

    shards = [a[n].astype(WIRE_DTYPE) for n in _BIG]
    axes = list(_BIG.values())
    kinds = [_gather_kind(s, ax) for s, ax in zip(shards, axes)]
    full = {n: w.astype(MXU_DTYPE) for n, w in zip(_BIG, _gather_weights(shards, axes, "gather_weights"))}
    conv_shapes = [a[n].shape for n in _CONV]
    gotc = _exchange_chips(_pack([a[n] for n in _CONV], F32, LANES, 8), False, "gather_conv_weights")
    piecesc = [_unpack(gotc[j], conv_shapes) for j in range(N_CHIPS)]
    fullc = {n: jnp.concatenate([piecesc[j][i] for j in range(N_CHIPS)], axis=2) for i, n in enumerate(_CONV)}

    wfull, small = [], []
    for l in range(depth):
        wf = {n: full[n][l] for n in _BIG if n != "w_in"}
        wf["w_in_p"] = _permute_w_in(full["w_in"][l], dm)
        wfull.append(wf)
        small.append({n: (fullc[n][l] if n in _CONV else a[n][l]) for n in _SMALL})

    loss_part, grad_x, grads, gbig = _local_step(dm, x, c, target, wfull, small)

    core = lax.axis_index("c").astype(jnp.int32).reshape(1)
    gst = []
    for n, kind in zip(_BIG, kinds):
        g = _unpermute_w_in(gbig["w_in_p"], dm) if n == "w_in" else gbig[n]
        if kind == "slots":
            g = jnp.moveaxis(g.reshape(depth, g.shape[1], N_CHIPS, g.shape[2] // N_CHIPS), 2, 1)
            g = g.reshape(depth, -1, g.shape[-1])
        gst.append(g)
    other = _swap_other_layer(gst, "swap_layer_grads")
    parts = [_pair_sum(g, o, core, "pair_sum_" + n) for g, o, n in zip(gst, other, _BIG)]
    parts = [p.reshape(N_CHIPS, -1, p.shape[-1]) if k == "slots" else p for p, k in zip(parts, kinds)]
    got = _scatter_layer(parts, kinds, "scatter_grads")
    mine = [_sum_slots(g, "sum_chip_grads_" + n) for g, n in zip(got, _BIG)]
    theirs = _sibling_swap(mine, "swap_core_grads")

    out = {}
    for i, n in enumerate(_BIG):
        out[n] = _adamw_layers(a[n], a["m_" + n], a["v_" + n], mine[i], theirs[i], core, "adamw_" + n)

    gsmall = [jnp.stack([grads[l][n] for l in range(depth)]) for n in _SMALL]
    small_shapes = [g.shape for g in gsmall]
    summed = _sum_slots(_allgather_all(_pack(gsmall, F32, LANES, 8), "gather_small_grads"), "sum_small_grads")
    gs = dict(zip(_SMALL, _unpack(summed, small_shapes)))
    for n in _CONV:
        wcols = a[n].shape[2]
        gs[n] = lax.dynamic_slice_in_dim(gs[n], chip * wcols, wcols, axis=2)
    local_shapes = [a[n].shape for n in _SMALL]
    res = _adamw(_pack([a[n] for n in _SMALL], F32, LANES, 8), _pack([a["m_" + n] for n in _SMALL], F32, LANES, 8),
                 _pack([a["v_" + n] for n in _SMALL], F32, LANES, 8), _pack([gs[n] for n in _SMALL], F32, LANES, 8),
                 "adamw_small")
    res = [_unpack(r, local_shapes) for r in res]
    for i, n in enumerate(_SMALL):
        out[n] = [r[i] for r in res]

    loss = lax.psum(loss_part, ("x", "y", "c"))
    return (loss, grad_x) + tuple(out[n][k] for k in range(4) for n in _WEIGHTS)


def kernel(x, c, ada_w, ada_b, mix_pre_g, mix_post_g, w_in, ssd_conv_w, ssd_conv_b, ssd_dt_bias, ssd_a_log, ssd_d, ssd_norm_g, w_ssd_out, sc_conv_w, w_sc_out, w_o, ffn_pre_g, ffn_post_g, w_up, ffn_conv_w, ffn_conv_b, w_down, loss_target, m_ada_w, m_ada_b, m_mix_pre_g, m_mix_post_g, m_w_in, m_ssd_conv_w, m_ssd_conv_b, m_ssd_dt_bias, m_ssd_a_log, m_ssd_d, m_ssd_norm_g, m_w_ssd_out, m_sc_conv_w, m_w_sc_out, m_w_o, m_ffn_pre_g, m_ffn_post_g, m_w_up, m_ffn_conv_w, m_ffn_conv_b, m_w_down, v_ada_w, v_ada_b, v_mix_pre_g, v_mix_post_g, v_w_in, v_ssd_conv_w, v_ssd_conv_b, v_ssd_dt_bias, v_ssd_a_log, v_ssd_d, v_ssd_norm_g, v_w_ssd_out, v_sc_conv_w, v_w_sc_out, v_w_o, v_ffn_pre_g, v_ffn_post_g, v_w_up, v_ffn_conv_w, v_ffn_conv_b, v_w_down):
    return _step(dict(zip(_INPUTS, (
        x, c, ada_w, ada_b, mix_pre_g, mix_post_g, w_in, ssd_conv_w, ssd_conv_b, ssd_dt_bias, ssd_a_log, ssd_d, ssd_norm_g, w_ssd_out, sc_conv_w, w_sc_out, w_o, ffn_pre_g, ffn_post_g, w_up, ffn_conv_w, ffn_conv_b, w_down, loss_target, m_ada_w, m_ada_b, m_mix_pre_g, m_mix_post_g, m_w_in, m_ssd_conv_w, m_ssd_conv_b, m_ssd_dt_bias, m_ssd_a_log, m_ssd_d, m_ssd_norm_g, m_w_ssd_out, m_sc_conv_w, m_w_sc_out, m_w_o, m_ffn_pre_g, m_ffn_post_g, m_w_up, m_ffn_conv_w, m_ffn_conv_b, m_w_down, v_ada_w, v_ada_b, v_mix_pre_g, v_mix_post_g, v_w_in, v_ssd_conv_w, v_ssd_conv_b, v_ssd_dt_bias, v_ssd_a_log, v_ssd_d, v_ssd_norm_g, v_w_ssd_out, v_sc_conv_w, v_w_sc_out, v_w_o, v_ffn_pre_g, v_ffn_post_g, v_w_up, v_ffn_conv_w, v_ffn_conv_b, v_w_down))))
```

```python
import math

import jax
import jax.numpy as jnp
from jax import lax
from jax.experimental import pallas as pl
from jax.experimental.pallas import tpu as pltpu

F32 = jnp.float32
MXU_DTYPE = jnp.bfloat16
WIRE_DTYPE = jnp.bfloat16
EPS = 1e-6
N_GROUPS = 4
CHUNK = 128
HEAD_DIM = 64
LANES = 128
HALO = 8
N_CHIPS = 4
N_DEV = 8
VMEM_LIMIT = 56 * 1024 * 1024
ADAM_LR, ADAM_B1, ADAM_B2, ADAM_EPS, ADAM_WD, ADAM_STEP = 0.001, 0.9, 0.999, 1e-08, 0.01, 10
MESH = pl.DeviceIdType.MESH

NN = (((1,), (0,)), ((), ()))
NT = (((1,), (1,)), ((), ()))
TN = (((0,), (0,)), ((), ()))


def _dg(a, b, dn, precision=None):
    return lax.dot_general(a, b, dn, precision=precision, preferred_element_type=F32)


def _tile(dim, pref, mult=LANES):
    t = (min(pref, dim) // mult) * mult
    while t >= mult:
        if dim % t == 0:
            return t
        t -= mult
    return dim


def _cp(sem):
    return pltpu.CompilerParams(dimension_semantics=sem, vmem_limit_bytes=VMEM_LIMIT)


def _sigmoid(x):
    return 1.0 / (1.0 + jnp.exp(-x))


def _softplus(x):
    return jnp.maximum(x, 0.0) + jnp.log1p(jnp.exp(-jnp.abs(x)))


def _matmul(a, b, mode, out_dtype, name, tm=1024, tn=1024, tk=1024, slab=None):
    if mode == "NN":
        (M, K), N = a.shape, b.shape[1]
    elif mode == "NT":
        (M, K), N = a.shape, b.shape[0]
    else:
        (K, M), N = a.shape, b.shape[1]
    tm, tn, tk = _tile(M, tm), _tile(N, tn), _tile(K, tk)
    nk = K // tk
    dn = {"NN": NN, "NT": NT, "TN": TN}[mode]
    carry = slab is not None and slab[0] is not None

    def body_one(a_ref, b_ref, *rest):
        o_ref = rest[-1]
        o_ref[...] = _dg(a_ref[...], b_ref[...], dn).astype(o_ref.dtype)

    def body_acc(a_ref, b_ref, *rest):
        o_ref, acc_ref = rest[-2:]
        k = pl.program_id(2)

        @pl.when(k == 0)
        def _():
            acc_ref[...] = jnp.zeros_like(acc_ref)

        acc_ref[...] += _dg(a_ref[...], b_ref[...], dn)

        @pl.when(k == nk - 1)
        def _():
            o_ref[...] = acc_ref[...].astype(o_ref.dtype)

    a_spec = (pl.BlockSpec((tk, tm), lambda i, j, k: (k, i)) if mode == "TN"
              else pl.BlockSpec((tm, tk), lambda i, j, k: (i, k)))
    b_spec = (pl.BlockSpec((tn, tk), lambda i, j, k: (j, k)) if mode == "NT"
              else pl.BlockSpec((tk, tn), lambda i, j, k: (k, j)))
    if slab is None:
        out_spec = pl.BlockSpec((tm, tn), lambda i, j, k: (i, j))
        out_shape = jax.ShapeDtypeStruct((M, N), out_dtype)
    else:
        layer = slab[1]
        out_spec = pl.BlockSpec((None, tm, tn), lambda i, j, k: (layer, i, j))
        out_shape = jax.ShapeDtypeStruct((slab[2], M, N), out_dtype)
    return pl.pallas_call(
        body_one if nk == 1 else body_acc, name=name, grid=(M // tm, N // tn, nk),
        in_specs=[a_spec, b_spec] + ([_ANY] if carry else []),
        out_specs=out_spec, out_shape=out_shape,
        input_output_aliases={2: 0} if carry else {},
        scratch_shapes=[] if nk == 1 else [pltpu.VMEM((tm, tn), F32)],
        compiler_params=_cp(("parallel", "parallel", "arbitrary")),
    )(*((a, b, slab[0]) if carry else (a, b)))


def _ada_fwd(c16, ada_w, ada_b, name):
    rows, D = c16.shape
    N6 = ada_w.shape[1]
    tn = _tile(N6, 1536)

    def body(c_ref, w_ref, b_ref, act_ref, mod_ref):
        c = c_ref[...]
        act = (c * _sigmoid(c)).astype(act_ref.dtype)
        act_ref[...] = act
        mod_ref[...] = _dg(act, w_ref[...], NN) + b_ref[...]

    return pl.pallas_call(
        body, name=name, grid=(N6 // tn,),
        in_specs=[pl.BlockSpec((rows, D), lambda j: (0, 0)),
                  pl.BlockSpec((D, tn), lambda j: (0, j)),
                  pl.BlockSpec((1, tn), lambda j: (0, j))],
        out_specs=[pl.BlockSpec((rows, D), lambda j: (0, 0)),
                   pl.BlockSpec((rows, tn), lambda j: (0, j))],
        out_shape=[jax.ShapeDtypeStruct((rows, D), MXU_DTYPE),
                   jax.ShapeDtypeStruct((rows, N6), F32)],
        compiler_params=_cp(("arbitrary",)),
    )(c16, ada_w, ada_b.reshape(1, N6))


def _norm_mod_rows(x, g, sc, sh):
    r = lax.rsqrt(jnp.mean(x * x, axis=-1, keepdims=True) + EPS)
    return ((x * r) * g) * (1.0 + sc) + sh


def _norm_mod_fwd(x, g, mod, row_sh, name, ts=512):
    Bl, S, D = x.shape
    ts = _tile(S, ts, 8)

    def body(x_ref, g_ref, mod_ref, h_ref):
        sh = mod_ref[row_sh:row_sh + 1, :]
        sc = mod_ref[row_sh + 1:row_sh + 2, :]
        h_ref[...] = _norm_mod_rows(x_ref[...], g_ref[...], sc, sh).astype(h_ref.dtype)

    tok = pl.BlockSpec((None, ts, D), lambda b, s: (b, s, 0))
    return pl.pallas_call(
        body, name=name, grid=(Bl, S // ts),
        in_specs=[tok, pl.BlockSpec((1, D), lambda b, s: (0, 0)),
                  pl.BlockSpec((None, 8, D), lambda b, s: (b, 0, 0))],
        out_specs=tok, out_shape=jax.ShapeDtypeStruct((Bl, S, D), MXU_DTYPE),
        compiler_params=_cp(("parallel", "parallel")),
    )(x, g.reshape(1, D), mod)


def _post_norm_fwd(xp, f, post_g, mod, row_gt, nxt, name, ts=512):
    Bl, S, D = xp.shape
    ts = _tile(S, ts, 8)
    has_next = nxt is not None

    def body(*refs):
        if has_next:
            xp_ref, f_ref, pg_ref, mod_ref, ng_ref, nmod_ref, x_ref, h_ref = refs
        else:
            xp_ref, f_ref, pg_ref, mod_ref, x_ref = refs
        f = f_ref[...]
        r = lax.rsqrt(jnp.mean(f * f, axis=-1, keepdims=True) + EPS)
        x = xp_ref[...] + mod_ref[row_gt:row_gt + 1, :] * ((f * r) * pg_ref[...])
        x_ref[...] = x
        if has_next:
            rs = nxt[2]
            h_ref[...] = _norm_mod_rows(x, ng_ref[...], nmod_ref[rs + 1:rs + 2, :], nmod_ref[rs:rs + 1, :]).astype(h_ref.dtype)

    tok = pl.BlockSpec((None, ts, D), lambda b, s: (b, s, 0))
    vec = pl.BlockSpec((1, D), lambda b, s: (0, 0))
    modspec = pl.BlockSpec((None, 8, D), lambda b, s: (b, 0, 0))
    ins = [xp, f, post_g.reshape(1, D), mod]
    in_specs = [tok, tok, vec, modspec]
    out_specs = [tok]
    out_shape = [jax.ShapeDtypeStruct((Bl, S, D), F32)]
    if has_next:
        ins += [nxt[0].reshape(1, D), nxt[1]]
        in_specs += [vec, modspec]
        out_specs += [tok]
        out_shape += [jax.ShapeDtypeStruct((Bl, S, D), MXU_DTYPE)]
    out = pl.pallas_call(
        body, name=name, grid=(Bl, S // ts), in_specs=in_specs, out_specs=out_specs, out_shape=out_shape,
        compiler_params=_cp(("parallel", "parallel")),
    )(*ins)
    return (out[0], out[1]) if has_next else (out[0], None)


def _loss_fwd_bwd(y, target, name, ts=512):
    Bl, S, D = y.shape
    ts = _tile(S, ts, 8)

    def body(y_ref, t_ref, dy_ref, l_ref):
        @pl.when((pl.program_id(0) == 0) & (pl.program_id(1) == 0))
        def _():
            l_ref[...] = jnp.zeros_like(l_ref)

        e = y_ref[...] - t_ref[...]
        dy_ref[...] = e * (1.0 / D)
        l_ref[...] += 0.5 * jnp.sum(jnp.mean(e * e, axis=-1, keepdims=True), axis=0, keepdims=True)

    tok = pl.BlockSpec((None, ts, D), lambda b, s: (b, s, 0))
    dy, l = pl.pallas_call(
        body, name=name, grid=(Bl, S // ts), in_specs=[tok, tok],
        out_specs=[tok, pl.BlockSpec((8, LANES), lambda b, s: (0, 0))],
        out_shape=[jax.ShapeDtypeStruct((Bl, S, D), F32), jax.ShapeDtypeStruct((8, LANES), F32)],
        compiler_params=_cp(("arbitrary", "arbitrary")),
    )(y, target)
    return dy, l[0, 0]


def _norm_bwd(dx_res, nxt, prv, name, ts=512):
    Bl, S, D = dx_res.shape
    ts = _tile(S, ts, 8)
    has_next, has_prev = nxt is not None, prv is not None

    def body(*refs):
        refs = list(refs)
        dxr_ref = refs.pop(0)
        if has_next:
            dh_ref, x_ref, g_ref, nmod_ref = refs[:4]
            refs = refs[4:]
        if has_prev:
            f_ref, pg_ref, pmod_ref = refs[:3]
            refs = refs[3:]
        dx_ref = refs.pop(0)
        if has_prev:
            df_ref = refs.pop(0)
        pb_ref, sh_ref = refs
        b, s = pl.program_id(0), pl.program_id(1)

        @pl.when(s == 0)
        def _():
            pb_ref[...] = jnp.zeros_like(pb_ref)

        @pl.when((b == 0) & (s == 0))
        def _():
            sh_ref[...] = jnp.zeros_like(sh_ref)

        dx = dxr_ref[...]
        if has_next:
            rs = nxt[4]
            x, dh, g = x_ref[...], dh_ref[...], g_ref[...]
            sc1 = 1.0 + nmod_ref[rs + 1:rs + 2, :]
            r = lax.rsqrt(jnp.mean(x * x, axis=-1, keepdims=True) + EPS)
            xn = x * r
            pb_ref[0:1, :] += jnp.sum(dh, axis=0, keepdims=True)
            pb_ref[1:2, :] += jnp.sum(dh * (xn * g), axis=0, keepdims=True)
            sh_ref[0:1, :] += jnp.sum(dh * sc1 * xn, axis=0, keepdims=True)
            dxn = dh * sc1 * g
            dx = dx + r * (dxn - xn * jnp.mean(dxn * xn, axis=-1, keepdims=True))
        dx_ref[...] = dx
        if has_prev:
            rg = prv[3]
            f, pg = f_ref[...], pg_ref[...]
            gt = pmod_ref[rg:rg + 1, :]
            r = lax.rsqrt(jnp.mean(f * f, axis=-1, keepdims=True) + EPS)
            fn = f * r
            pb_ref[2:3, :] += jnp.sum(dx * (fn * pg), axis=0, keepdims=True)
            drn = dx * gt
            sh_ref[1:2, :] += jnp.sum(drn * fn, axis=0, keepdims=True)
            dfn = drn * pg
            df_ref[...] = (r * (dfn - fn * jnp.mean(dfn * fn, axis=-1, keepdims=True))).astype(df_ref.dtype)

    tok = pl.BlockSpec((None, ts, D), lambda b, s: (b, s, 0))
    vec = pl.BlockSpec((1, D), lambda b, s: (0, 0))
    modspec = pl.BlockSpec((None, 8, D), lambda b, s: (b, 0, 0))
    ins, in_specs = [dx_res], [tok]
    if has_next:
        ins += [nxt[0], nxt[1], nxt[2].reshape(1, D), nxt[3]]
        in_specs += [tok, tok, vec, modspec]
    if has_prev:
        ins += [prv[0], prv[1].reshape(1, D), prv[2]]
        in_specs += [tok, vec, modspec]
    out_specs, out_shape = [tok], [jax.ShapeDtypeStruct((Bl, S, D), F32)]
    if has_prev:
        out_specs += [tok]
        out_shape += [jax.ShapeDtypeStruct((Bl, S, D), MXU_DTYPE)]
    out_specs += [modspec, pl.BlockSpec((8, D), lambda b, s: (0, 0))]
    out_shape += [jax.ShapeDtypeStruct((Bl, 8, D), F32), jax.ShapeDtypeStruct((8, D), F32)]
    out = pl.pallas_call(
        body, name=name, grid=(Bl, S // ts), in_specs=in_specs, out_specs=out_specs, out_shape=out_shape,
        compiler_params=_cp(("arbitrary", "arbitrary")),
    )(*ins)
    if has_prev:
        return out[0], out[1], out[2], out[3]
    return out[0], None, out[1], out[2]


def _shift_down(x, j):
    return x if j == 0 else pltpu.roll(x, j, axis=0)


def _shift_up(x, j):
    return x if j == 0 else pltpu.roll(x, x.shape[0] - j, axis=0)


def _conv(xall, w_ref, K):
    y = w_ref[K - 1:K, :] * xall
    for k in range(K - 1):
        y = y + w_ref[k:k + 1, :] * _shift_down(xall, K - 1 - k)
    return y


def _conv_t(dall, xall, w, K, rows):
    xt = xall[HALO:HALO + rows]
    y = w[K - 1:K, :] * dall[HALO:HALO + rows]
    gw = [None] * K
    gw[K - 1] = jnp.sum(dall[HALO:HALO + rows] * xt, axis=0, keepdims=True)
    for k in range(K - 1):
        sh = _shift_up(dall, K - 1 - k)[HALO:HALO + rows]
        y = y + w[k:k + 1, :] * sh
        gw[k] = jnp.sum(sh * xt, axis=0, keepdims=True)
    return y, gw


def _conv_t_rows(dall, w_ref, K):
    y = w_ref[K - 1:K, :] * dall
    for k in range(K - 1):
        y = y + w_ref[k:k + 1, :] * _shift_up(dall, K - 1 - k)
    return y


def _conv_wgrad(acc_ref, dtile, xall, K, ts):
    for k in range(K):
        xs = _shift_down(xall, K - 1 - k)[HALO:HALO + ts]
        acc_ref[k:k + 1, :] += jnp.sum(dtile * xs, axis=0, keepdims=True)


ROW_CHUNK = 128


def _lane_chunks(W):
    return [(j * LANES, LANES) for j in range(W // LANES)]


def _fill_ext(ext_ref, prev, tile, nxt, ts):
    ext_ref[0:HALO, :] = prev
    ext_ref[HALO:HALO + ts, :] = tile
    ext_ref[HALO + ts:2 * HALO + ts, :] = nxt


def _halo_specs(ts, W, nS, colblk):
    per = ts // HALO
    tile = pl.BlockSpec((None, ts, W), lambda b, c, s: (b, s, colblk(c)))
    prev = pl.BlockSpec((None, HALO, W), lambda b, c, s: (b, jnp.maximum(s * per - 1, 0), colblk(c)))
    nxt = pl.BlockSpec((None, HALO, W), lambda b, c, s: (b, jnp.minimum((s + 1) * per, nS * per - 1), colblk(c)))
    return tile, prev, nxt


def _masked(ref, keep):
    v = ref[...]
    return jnp.where(keep, v, jnp.zeros_like(v))


def _ssd_conv_fwd(proj, w8, b, off, CD, name, ts=512, W=1024):
    Bl, S, _ = proj.shape
    K = 4
    ts, W = _tile(S, ts, 8), _tile(math.gcd(CD, off), W)
    assert off % W == 0
    nS, nW, ob = S // ts, CD // W, off // W

    def body(x_ref, xp_ref, w_ref, b_ref, o_ref):
        xall = jnp.concatenate([_masked(xp_ref, pl.program_id(2) > 0), x_ref[...]], axis=0)
        xc = _conv(xall, w_ref, K)[HALO:] + b_ref[...]
        o_ref[...] = xc * _sigmoid(xc)

    tile, prev, _ = _halo_specs(ts, W, nS, lambda c: ob + c)
    return pl.pallas_call(
        body, name=name, grid=(Bl, nW, nS),
        in_specs=[tile, prev, pl.BlockSpec((8, W), lambda b_, c, s: (0, c)), pl.BlockSpec((1, W), lambda b_, c, s: (0, c))],
        out_specs=pl.BlockSpec((None, ts, W), lambda b_, c, s: (b_, s, c)),
        out_shape=jax.ShapeDtypeStruct((Bl, S, CD), F32),
        compiler_params=_cp(("parallel", "parallel", "parallel")),
    )(proj, proj, w8, b.reshape(1, CD))


def _ssd_conv_bwd(dxa, proj, w8, b, off, CD, dproj, name, ts=512, W=1024):
    Bl, S, _ = proj.shape
    K = 4
    ts, W = _tile(S, ts, 8), _tile(math.gcd(CD, off), W)
    nS, nW, ob = S // ts, CD // W, off // W

    rc = _tile(ts, ROW_CHUNK, 8)

    def body(d_ref, dn_ref, x_ref, xp_ref, xn_ref, w_ref, b_ref, _, dx_ref, dw_ref, db_ref, xext_ref, dext_ref):
        bb, s = pl.program_id(1), pl.program_id(2)

        @pl.when((bb == 0) & (s == 0))
        def _():
            dw_ref[...] = jnp.zeros_like(dw_ref)
            db_ref[...] = jnp.zeros_like(db_ref)

        last = s == nS - 1
        _fill_ext(xext_ref, _masked(xp_ref, s > 0), x_ref[...], _masked(xn_ref, ~last), ts)
        _fill_ext(dext_ref, jnp.zeros((HALO, W), F32), d_ref[...], _masked(dn_ref, ~last), ts)
        for l0, lc in _lane_chunks(W):
            w, bias = w_ref[:, l0:l0 + lc], b_ref[:, l0:l0 + lc]

            def chunk(i, acc):
                r0 = pl.multiple_of(i * rc, 8)
                xw = xext_ref[pl.ds(r0, rc + 2 * HALO), l0:l0 + lc]
                xc = _conv(xw, w, K) + bias
                sg = _sigmoid(xc)
                dxc = dext_ref[pl.ds(r0, rc + 2 * HALO), l0:l0 + lc] * (sg * (1.0 + xc * (1.0 - sg)))
                dx, gw = _conv_t(dxc, xw, w, K, rc)
                dx_ref[pl.ds(r0, rc), l0:l0 + lc] = dx.astype(dx_ref.dtype)
                gb = jnp.sum(dxc[HALO:HALO + rc], axis=0, keepdims=True)
                return tuple(a + g for a, g in zip(acc, gw + [gb]))

            acc = lax.fori_loop(0, ts // rc, chunk, tuple(jnp.zeros((1, lc), F32) for _ in range(K + 1)))
            for k in range(K):
                dw_ref[k:k + 1, l0:l0 + lc] += acc[k]
            db_ref[0:1, l0:l0 + lc] += acc[K]

    per = ts // HALO
    dtile_s = pl.BlockSpec((None, ts, W), lambda c, b_, s: (b_, s, c))
    dnext_s = pl.BlockSpec((None, HALO, W), lambda c, b_, s: (b_, jnp.minimum((s + 1) * per, nS * per - 1), c))
    xtile_s = pl.BlockSpec((None, ts, W), lambda c, b_, s: (b_, s, ob + c))
    xprev_s = pl.BlockSpec((None, HALO, W), lambda c, b_, s: (b_, jnp.maximum(s * per - 1, 0), ob + c))
    xnext_s = pl.BlockSpec((None, HALO, W), lambda c, b_, s: (b_, jnp.minimum((s + 1) * per, nS * per - 1), ob + c))
    wspec = pl.BlockSpec((8, W), lambda c, b_, s: (0, c))
    return pl.pallas_call(
        body, name=name, grid=(nW, Bl, nS),
        in_specs=[dtile_s, dnext_s, xtile_s, xprev_s, xnext_s, wspec, pl.BlockSpec((1, W), lambda c, b_, s: (0, c)),
                  _ANY],
        out_specs=[xtile_s, wspec, wspec],
        out_shape=[jax.ShapeDtypeStruct(dproj.shape, dproj.dtype), jax.ShapeDtypeStruct((8, CD), F32),
                   jax.ShapeDtypeStruct((8, CD), F32)],
        input_output_aliases={7: 0},
        scratch_shapes=[pltpu.VMEM((ts + 2 * HALO, W), F32)] * 2,
        compiler_params=_cp(("arbitrary", "arbitrary", "arbitrary")),
    )(dxa, dxa, proj, proj, proj, w8, b.reshape(1, CD), dproj)


def _sc_conv_fwd(proj, w8, offs, D, name, ts=512, W=1024):
    Bl, S, _ = proj.shape
    K = 3
    ts, W = _tile(S, ts, 8), _tile(D, W)
    nS, nW = S // ts, D // W
    ob, oc, oh = [o // W for o in offs]

    def body(b_ref, c_ref, cp_ref, h_ref, hp_ref, w_ref, o_ref):
        s = pl.program_id(2)
        keep = s > 0
        vall = jnp.concatenate([_masked(cp_ref, keep) * _masked(hp_ref, keep), c_ref[...] * h_ref[...]], axis=0)
        o_ref[...] = (b_ref[...] * _conv(vall, w_ref, K)[HALO:]).astype(o_ref.dtype)

    tb, _, _ = _halo_specs(ts, W, nS, lambda c: ob + c)
    tc, pc, _ = _halo_specs(ts, W, nS, lambda c: oc + c)
    th, ph, _ = _halo_specs(ts, W, nS, lambda c: oh + c)
    return pl.pallas_call(
        body, name=name, grid=(Bl, nW, nS),
        in_specs=[tb, tc, pc, th, ph, pl.BlockSpec((8, W), lambda b_, c, s: (0, c))],
        out_specs=pl.BlockSpec((None, ts, W), lambda b_, c, s: (b_, s, c)),
        out_shape=jax.ShapeDtypeStruct((Bl, S, D), MXU_DTYPE),
        compiler_params=_cp(("parallel", "parallel", "parallel")),
    )(proj, proj, proj, proj, proj, w8)


def _sc_conv_bwd(ds, proj, w8, offs, D, dproj, name, ts=128):
    Bl, S, _ = proj.shape
    K = 3
    ts, W = _tile(S, ts, 8), D
    nS, nW = S // ts, 1
    ob, oc, oh = [o // W for o in offs]
    assert offs[1] == offs[0] + D and offs[2] == offs[0] + 2 * D and offs[0] % (3 * D) == 0

    def body(d_ref, dn_ref, b_ref, bn_ref, c_ref, cp_ref, cn_ref, h_ref, hp_ref, hn_ref, w_ref, _,
             o_ref, dw_ref):
        db_ref, dc_ref, dh_ref = o_ref.at[:, 0:D], o_ref.at[:, D:2 * D], o_ref.at[:, 2 * D:3 * D]
        bb, s = pl.program_id(1), pl.program_id(2)

        @pl.when((bb == 0) & (s == 0))
        def _():
            dw_ref[...] = jnp.zeros_like(dw_ref)

        first, last = s > 0, s < nS - 1
        zeros = jnp.zeros((HALO, W), F32)
        c_t, h_t = c_ref[...], h_ref[...]
        vall = jnp.concatenate([_masked(cp_ref, first) * _masked(hp_ref, first), c_t * h_t,
                                _masked(cn_ref, last) * _masked(hn_ref, last)], axis=0)
        dcv = jnp.concatenate([zeros, d_ref[...] * b_ref[...], _masked(dn_ref, last) * _masked(bn_ref, last)], axis=0)
        cv = _conv(vall, w_ref, K)[HALO:HALO + ts]
        db_ref[...] = (d_ref[...] * cv).astype(db_ref.dtype)
        dv, gw = _conv_t(dcv, vall, w_ref, K, ts)
        dc_ref[...] = (dv * h_t).astype(dc_ref.dtype)
        dh_ref[...] = (dv * c_t).astype(dh_ref.dtype)
        for k in range(K):
            dw_ref[k:k + 1, :] += gw[k]

    per = ts // HALO

    def specs(o):
        t = pl.BlockSpec((None, ts, W), lambda c, b_, s: (b_, s, o + c))
        p = pl.BlockSpec((None, HALO, W), lambda c, b_, s: (b_, jnp.maximum(s * per - 1, 0), o + c))
        n = pl.BlockSpec((None, HALO, W), lambda c, b_, s: (b_, jnp.minimum((s + 1) * per, nS * per - 1), o + c))
        return t, p, n

    dt_, _, dn_ = specs(0)
    bt, _, bn = specs(ob)
    ct, cp, cn = specs(oc)
    ht, hp, hn = specs(oh)
    wspec = pl.BlockSpec((8, W), lambda c, b_, s: (0, c))
    o3 = offs[0] // (3 * D)
    return pl.pallas_call(
        body, name=name, grid=(nW, Bl, nS),
        in_specs=[dt_, dn_, bt, bn, ct, cp, cn, ht, hp, hn, wspec, _ANY],
        out_specs=[pl.BlockSpec((None, ts, 3 * D), lambda c, b_, s: (b_, s, o3)), wspec],
        out_shape=[jax.ShapeDtypeStruct(dproj.shape, dproj.dtype), jax.ShapeDtypeStruct((8, D), F32)],
        input_output_aliases={11: 0},
        compiler_params=_cp(("arbitrary", "arbitrary", "arbitrary")),
    )(ds, ds, proj, proj, proj, proj, proj, proj, proj, proj, w8, dproj)


def _ffn_conv_fwd(up, w8, b, DFF, name, ts=512, W=1408):
    Bl, S, _ = up.shape
    K = 3
    ts, W = _tile(S, ts, 8), _tile(DFF, W)
    nS, nW = S // ts, DFF // W

    def body(g_ref, gp_ref, v_ref, vp_ref, wg_ref, wv_ref, bg_ref, bv_ref, o_ref):
        keep = pl.program_id(2) > 0
        ug = _conv(jnp.concatenate([_masked(gp_ref, keep), g_ref[...]], axis=0), wg_ref, K)[HALO:] + bg_ref[...]
        uv = _conv(jnp.concatenate([_masked(vp_ref, keep), v_ref[...]], axis=0), wv_ref, K)[HALO:] + bv_ref[...]
        o_ref[...] = (ug * _sigmoid(ug) * uv).astype(o_ref.dtype)

    tg, pg, _ = _halo_specs(ts, W, nS, lambda c: c)
    tv, pv, _ = _halo_specs(ts, W, nS, lambda c: nW + c)
    wg = pl.BlockSpec((8, W), lambda b_, c, s: (0, c))
    wv = pl.BlockSpec((8, W), lambda b_, c, s: (0, nW + c))
    bg = pl.BlockSpec((1, W), lambda b_, c, s: (0, c))
    bv = pl.BlockSpec((1, W), lambda b_, c, s: (0, nW + c))
    b2 = b.reshape(1, 2 * DFF)
    return pl.pallas_call(
        body, name=name, grid=(Bl, nW, nS),
        in_specs=[tg, pg, tv, pv, wg, wv, bg, bv],
        out_specs=pl.BlockSpec((None, ts, W), lambda b_, c, s: (b_, s, c)),
        out_shape=jax.ShapeDtypeStruct((Bl, S, DFF), MXU_DTYPE),
        compiler_params=_cp(("parallel", "parallel", "parallel")),
    )(up, up, up, up, w8, w8, b2, b2)


def _ffn_conv_bwd(da, up, w8, b, DFF, name, ts=256):
    Bl, S, _ = up.shape
    K = 3
    ts, W = _tile(S, ts, 8), DFF
    nS = S // ts
    rc = _tile(ts, ROW_CHUNK, 8)

    def body(d_ref, dn_ref, g_ref, gp_ref, gn_ref, v_ref, vp_ref, vn_ref, w_ref, b_ref,
             dup_ref, dw_ref, db_ref, gext_ref, vext_ref, dext_ref):
        bb, s = pl.program_id(0), pl.program_id(1)

        @pl.when((bb == 0) & (s == 0))
        def _():
            dw_ref[...] = jnp.zeros_like(dw_ref)
            db_ref[...] = jnp.zeros_like(db_ref)

        first, last = s > 0, s < nS - 1
        _fill_ext(gext_ref, _masked(gp_ref, first), g_ref[...], _masked(gn_ref, last), ts)
        _fill_ext(vext_ref, _masked(vp_ref, first), v_ref[...], _masked(vn_ref, last), ts)
        _fill_ext(dext_ref, jnp.zeros((HALO, W), F32), d_ref[...], _masked(dn_ref, last), ts)
        for l0, lc in _lane_chunks(W):
            gl, vl = slice(l0, l0 + lc), slice(DFF + l0, DFF + l0 + lc)
            wg_, wv_, bg_, bv_ = w_ref[:, gl], w_ref[:, vl], b_ref[:, gl], b_ref[:, vl]

            def chunk(i, acc):
                r0 = pl.multiple_of(i * rc, 8)
                win = pl.ds(r0, rc + 2 * HALO)
                gw_, vw_, dw_ = gext_ref[win, gl], vext_ref[win, gl], dext_ref[win, gl]
                ug = _conv(gw_, wg_, K) + bg_
                uv = _conv(vw_, wv_, K) + bv_
                sg = _sigmoid(ug)
                dug = dw_ * uv * (sg * (1.0 + ug * (1.0 - sg)))
                duv = dw_ * (ug * sg)
                dg, gg = _conv_t(dug, gw_, wg_, K, rc)
                dv, gv = _conv_t(duv, vw_, wv_, K, rc)
                dup_ref[pl.ds(r0, rc), gl] = dg.astype(dup_ref.dtype)
                dup_ref[pl.ds(r0, rc), vl] = dv.astype(dup_ref.dtype)
                new = gg + gv + [jnp.sum(dug[HALO:HALO + rc], axis=0, keepdims=True),
                                 jnp.sum(duv[HALO:HALO + rc], axis=0, keepdims=True)]
                return tuple(a + g for a, g in zip(acc, new))

            acc = lax.fori_loop(0, ts // rc, chunk, tuple(jnp.zeros((1, lc), F32) for _ in range(2 * K + 2)))
            for k in range(K):
                dw_ref[k:k + 1, gl] += acc[k]
                dw_ref[k:k + 1, vl] += acc[K + k]
            db_ref[0:1, gl] += acc[2 * K]
            db_ref[0:1, vl] += acc[2 * K + 1]

    per = ts // HALO

    def specs(blk):
        t = pl.BlockSpec((None, ts, W), lambda b_, s: (b_, s, blk))
        p = pl.BlockSpec((None, HALO, W), lambda b_, s: (b_, jnp.maximum(s * per - 1, 0), blk))
        n = pl.BlockSpec((None, HALO, W), lambda b_, s: (b_, jnp.minimum((s + 1) * per, nS * per - 1), blk))
        return t, p, n

    dt_, _, dn_ = specs(0)
    gt, gp, gn = specs(0)
    vt, vp, vn = specs(1)
    small = pl.BlockSpec((8, 2 * DFF), lambda b_, s: (0, 0))
    return pl.pallas_call(
        body, name=name, grid=(Bl, nS),
        in_specs=[dt_, dn_, gt, gp, gn, vt, vp, vn, small, pl.BlockSpec((1, 2 * DFF), lambda b_, s: (0, 0))],
        out_specs=[pl.BlockSpec((None, ts, 2 * DFF), lambda b_, s: (b_, s, 0)), small, small],
        out_shape=[jax.ShapeDtypeStruct((Bl, S, 2 * DFF), MXU_DTYPE), jax.ShapeDtypeStruct((8, 2 * DFF), F32),
                   jax.ShapeDtypeStruct((8, 2 * DFF), F32)],
        scratch_shapes=[pltpu.VMEM((ts + 2 * HALO, W), F32)] * 3,
        compiler_params=_cp(("arbitrary", "arbitrary")),
    )(da, da, up, up, up, up, up, up, w8, b.reshape(1, 2 * DFF))


def _merge_fwd(proj, y_ssd, y_sc, off, D, name, ts=512):
    Bl, S, _ = proj.shape
    ts = _tile(S, ts, 8)
    og = off // D

    def body(g1_ref, g2_ref, a_ref, b_ref, o_ref):
        o_ref[...] = (_sigmoid(g1_ref[...]) * a_ref[...] + _sigmoid(g2_ref[...]) * b_ref[...]).astype(o_ref.dtype)

    tok = pl.BlockSpec((None, ts, D), lambda b, s: (b, s, 0))
    return pl.pallas_call(
        body, name=name, grid=(Bl, S // ts),
        in_specs=[pl.BlockSpec((None, ts, D), lambda b, s: (b, s, og)),
                  pl.BlockSpec((None, ts, D), lambda b, s: (b, s, og + 1)), tok, tok],
        out_specs=tok, out_shape=jax.ShapeDtypeStruct((Bl, S, D), MXU_DTYPE),
        compiler_params=_cp(("parallel", "parallel")),
    )(proj, proj, y_ssd, y_sc)


def _merge_bwd(dmixin, proj, y_ssd, y_sc, off, D, dproj0, name, ts=512):
    Bl, S, NP = proj.shape
    ts = _tile(S, ts, 8)
    og = off // D
    assert off % (2 * D) == 0

    def body(d_ref, g1_ref, g2_ref, a_ref, b_ref, *rest):
        da_ref, db_ref, dg_ref = rest[-3:]
        dg1_ref, dg2_ref = dg_ref.at[:, 0:D], dg_ref.at[:, D:2 * D]
        d = d_ref[...]
        s1, s2 = _sigmoid(g1_ref[...]), _sigmoid(g2_ref[...])
        da_ref[...] = (d * s1).astype(da_ref.dtype)
        db_ref[...] = (d * s2).astype(db_ref.dtype)
        dg1_ref[...] = (d * a_ref[...] * (s1 * (1.0 - s1))).astype(dg1_ref.dtype)
        dg2_ref[...] = (d * b_ref[...] * (s2 * (1.0 - s2))).astype(dg2_ref.dtype)

    tok = pl.BlockSpec((None, ts, D), lambda b, s: (b, s, 0))
    act = jax.ShapeDtypeStruct((Bl, S, D), MXU_DTYPE)
    carry = dproj0 is not None
    return pl.pallas_call(
        body, name=name, grid=(Bl, S // ts),
        in_specs=[tok, pl.BlockSpec((None, ts, D), lambda b, s: (b, s, og)),
                  pl.BlockSpec((None, ts, D), lambda b, s: (b, s, og + 1)), tok, tok] + ([_ANY] if carry else []),
        out_specs=[tok, tok, pl.BlockSpec((None, ts, 2 * D), lambda b, s: (b, s, og // 2))],
        out_shape=[act, act, jax.ShapeDtypeStruct((Bl, S, NP), MXU_DTYPE)],
        input_output_aliases={5: 2} if carry else {},
        compiler_params=_cp(("parallel", "parallel")),
    )(*((dmixin, proj, proj, y_ssd, y_sc) + ((dproj0,) if carry else ())))


def _exact_dot(a, b, dn, value):
    v = a if value == 0 else b
    m01 = (b if value == 0 else a).astype(MXU_DTYPE)
    hi = v.astype(MXU_DTYPE)
    r1 = v - hi.astype(F32)
    mid = r1.astype(MXU_DTYPE)
    lo = (r1 - mid.astype(F32)).astype(MXU_DTYPE)
    terms = [(t, m01) if value == 0 else (m01, t) for t in (hi, mid, lo)]
    return _dg(*terms[0], dn) + _dg(*terms[1], dn) + _dg(*terms[2], dn)


def _two_term_dot(v, m01):
    hi = v.astype(MXU_DTYPE)
    lo = (v - hi.astype(F32)).astype(MXU_DTYPE)
    return _dg(hi, m01, NN) + _dg(lo, m01, NN)


def _head_to_channels(R):
    rp = R * HEAD_DIM
    return (lax.shift_right_logical(lax.broadcasted_iota(jnp.int32, (LANES, rp), 1), 6)
            == lax.broadcasted_iota(jnp.int32, (LANES, rp), 0)).astype(MXU_DTYPE)


def _expand_heads(mats, R):
    L = mats[0].shape[0]
    out = _two_term_dot(jnp.concatenate(mats, axis=0), _head_to_channels(R))
    return [out[i * L:(i + 1) * L, :] for i in range(len(mats))]


def _expand_cols(mat, R, lane):
    half = lane < HEAD_DIM
    return jnp.concatenate(
        [jnp.where(half, mat[:, 2 * q:2 * q + 1], mat[:, 2 * q + 1:2 * q + 2]) for q in range(R // 2)], axis=1)


def _row_sums(v):
    return _two_term_dot(v, jnp.ones((v.shape[1], LANES), MXU_DTYPE))


def _lanes_to(v, width):
    return jnp.concatenate([v] * (width // LANES), axis=1)


def _head_rows(colvec, R, N):
    return jnp.concatenate([jnp.broadcast_to(colvec[r:r + 1, :], (HEAD_DIM, N)) for r in range(R)], axis=0)


def _ssd_prep(proj, dt_bias, a_log, selcat, selbig, dm, name):
    Bl, S, L, G = dm.Bl, dm.S, CHUNK, N_GROUPS
    nc = S // L
    odt = dm.ODT // LANES

    def body(dtr_ref, dtb_ref, alog_ref, selcat_ref, selbig_ref, dtg_ref, acs_ref, acst_ref):
        row = lax.broadcasted_iota(jnp.int32, (L, L), 0)
        col = lax.broadcasted_iota(jnp.int32, (L, L), 1)
        dt_all = _softplus(dtr_ref[...] + dtb_ref[...])
        acs_all = _exact_dot((row >= col).astype(F32), dt_all * (-jnp.exp(alog_ref[...])), NN, 1)
        dtg_ref[...] = _exact_dot(dt_all, selcat_ref[...], NN, 0)
        acs_ref[...] = _exact_dot(acs_all, selcat_ref[...], NN, 0)
        acst_ref[...] = _exact_dot(selbig_ref[...], acs_all, NT, 1)

    vec = pl.BlockSpec((1, LANES), lambda b, c: (0, 0))
    wide = pl.BlockSpec((None, L, G * LANES), lambda b, c: (b, c, 0))
    return pl.pallas_call(
        body, name=name, grid=(Bl, nc),
        in_specs=[pl.BlockSpec((None, L, LANES), lambda b, c: (b, c, odt)), vec, vec,
                  pl.BlockSpec((LANES, G * LANES), lambda b, c: (0, 0)),
                  pl.BlockSpec((G * 8, LANES), lambda b, c: (0, 0))],
        out_specs=[wide, wide, pl.BlockSpec((None, None, G * 8, L), lambda b, c: (b, c, 0, 0))],
        out_shape=[jax.ShapeDtypeStruct((Bl, S, G * LANES), F32), jax.ShapeDtypeStruct((Bl, S, G * LANES), F32),
                   jax.ShapeDtypeStruct((Bl, nc, G * 8, L), F32)],
        compiler_params=_cp(("parallel", "parallel")),
    )(proj, dt_bias, a_log, selcat, selbig)


def _ssd_post(d_a, ddt, dtg, proj, dt_bias, a_log, selcat, dproj, dm, name):
    Bl, S, L, G = dm.Bl, dm.S, CHUNK, N_GROUPS
    nc = S // L
    odt = dm.ODT // LANES

    def body(da_ref, ddt_ref, dtg_ref, dtr_ref, dtb_ref, alog_ref, selcat_ref, _, ddtr_ref, dpar_ref):
        @pl.when((pl.program_id(0) == 0) & (pl.program_id(1) == 0))
        def _():
            dpar_ref[...] = jnp.zeros_like(dpar_ref)

        row = lax.broadcasted_iota(jnp.int32, (L, L), 0)
        col = lax.broadcasted_iota(jnp.int32, (L, L), 1)
        selcat = selcat_ref[...]
        a_all = -jnp.exp(alog_ref[...])
        a4 = _exact_dot(jnp.broadcast_to(a_all, (8, LANES)), selcat, NN, 0)[0:1, :]
        dadt = _exact_dot((col >= row).astype(F32), da_ref[...], NN, 1)
        ddt4 = ddt_ref[...] + dadt * a4
        da4 = jnp.sum(dadt * dtg_ref[...], axis=0, keepdims=True)
        ddt_all = _exact_dot(ddt4, selcat, NT, 0)
        da_all = _exact_dot(jnp.broadcast_to(da4, (8, G * LANES)), selcat, NT, 0)[0:1, :]
        ddtr = ddt_all * _sigmoid(dtr_ref[...] + dtb_ref[...])
        ddtr_ref[...] = ddtr.astype(ddtr_ref.dtype)
        dpar_ref[0:1, :] += jnp.sum(ddtr, axis=0, keepdims=True)
        dpar_ref[1:2, :] += da_all * a_all

    vec = pl.BlockSpec((1, LANES), lambda b, c: (0, 0))
    wide = pl.BlockSpec((None, L, G * LANES), lambda b, c: (b, c, 0))
    return pl.pallas_call(
        body, name=name, grid=(Bl, nc),
        in_specs=[wide, wide, wide, pl.BlockSpec((None, L, LANES), lambda b, c: (b, c, odt)), vec, vec,
                  pl.BlockSpec((LANES, G * LANES), lambda b, c: (0, 0)), _ANY],
        out_specs=[pl.BlockSpec((None, L, LANES), lambda b, c: (b, c, odt)), pl.BlockSpec((8, LANES), lambda b, c: (0, 0))],
        out_shape=[jax.ShapeDtypeStruct(dproj.shape, dproj.dtype), jax.ShapeDtypeStruct((8, LANES), F32)],
        input_output_aliases={7: 0},
        compiler_params=_cp(("arbitrary", "arbitrary")),
    )(d_a, ddt, dtg, proj, dt_bias, a_log, selcat, dproj)


def _scan_fwd(xbc_a, proj, dtg4, acs4, acs_t4, dsk_exp, norm_g, dm, name):
    Bl, S, DI, N, R, L, G = dm.Bl, dm.S, dm.DI, dm.N, dm.R, CHUNK, N_GROUPS
    RP = R * HEAD_DIM
    nc = S // L

    def body(x_ref, z_ref, dtg_ref, acs_ref, acst_ref, dsk_ref, ng_ref, y_ref, yn_ref, hp_ref, h_ref):
        xs_ref, bm_ref, cm_ref = x_ref.at[:, 0:RP], x_ref.at[:, RP:RP + N], x_ref.at[:, RP + N:RP + 2 * N]
        @pl.when(pl.program_id(2) == 0)
        def _():
            h_ref[...] = jnp.zeros_like(h_ref)

        causal = lax.broadcasted_iota(jnp.int32, (L, L), 0) >= lax.broadcasted_iota(jnp.int32, (L, L), 1)
        lane = lax.broadcasted_iota(jnp.int32, (L, LANES), 1)
        dtg, acs, acs_t = dtg_ref[...], acs_ref[...], acst_ref[...]
        xs = xs_ref[...]
        bmb, cmb = bm_ref[...].astype(MXU_DTYPE), cm_ref[...].astype(MXU_DTYPE)
        sg = _dg(cmb, bmb, NT)
        acs_last = acs[L - 1:L, :]
        dt_exp, ea_exp, ds_exp = _expand_heads([dtg, jnp.exp(acs), jnp.exp(acs_last - acs)], R)
        xdt = xs * dt_exp
        xb = xdt.astype(MXU_DTYPE)
        parts = []
        for q in range(R // 2):
            x2 = xb[:, LANES * q:LANES * (q + 1)]
            ys = []
            for r in (2 * q, 2 * q + 1):
                dec = jnp.exp(jnp.where(causal, acs[:, r:r + 1] - acs_t[r:r + 1, :], -1e30))
                ys.append(_dg((sg * dec).astype(MXU_DTYPE), x2, NN))
            parts.append(jnp.where(lane < HEAD_DIM, ys[0], ys[1]))
        ydiag = jnp.concatenate(parts, axis=1)
        h_cur = h_ref[...]
        hb = h_cur.astype(MXU_DTYPE)
        yoff = _dg(cmb, hb, NT) * ea_exp
        st = _dg((xdt * ds_exp).astype(MXU_DTYPE), bmb, TN)
        hp_ref[...] = hb
        h_ref[...] = h_cur * _head_rows(jnp.exp(acs_t[:, L - 1:L]), R, N) + st
        y = ydiag + yoff + dsk_ref[...] * xs
        y_ref[...] = y
        z = z_ref[...]
        yg = y * (z * _sigmoid(z))
        rr = lax.rsqrt(_row_sums(yg * yg) * (1.0 / RP) + EPS)
        yn_ref[...] = (yg * _lanes_to(rr, RP) * ng_ref[...]).astype(yn_ref.dtype)

    oz = dm.OZ // RP
    grp = pl.BlockSpec((None, L, RP), lambda b, g, c: (b, c, g))
    lanes = pl.BlockSpec((None, L, LANES), lambda b, g, c: (b, c, g))
    chan = pl.BlockSpec((1, RP), lambda b, g, c: (0, g))
    return pl.pallas_call(
        body, name=name, grid=(Bl, G, nc),
        in_specs=[pl.BlockSpec((None, L, RP + 2 * N), lambda b, g, c: (b, c, g)),
                  pl.BlockSpec((None, L, RP), lambda b, g, c: (b, c, oz + g)), lanes, lanes,
                  pl.BlockSpec((None, None, 8, L), lambda b, g, c: (b, c, g, 0)),
                  chan, chan],
        out_specs=[grp, grp, pl.BlockSpec((None, None, None, RP, N), lambda b, g, c: (b, g, c, 0, 0))],
        out_shape=[jax.ShapeDtypeStruct((Bl, S, DI), F32), jax.ShapeDtypeStruct((Bl, S, DI), MXU_DTYPE),
                   jax.ShapeDtypeStruct((Bl, G, nc, RP, N), MXU_DTYPE)],
        scratch_shapes=[pltpu.VMEM((RP, N), F32)],
        compiler_params=_cp(("parallel", "parallel", "arbitrary")),
    )(xbc_a, proj, dtg4, acs4, acs_t4, dsk_exp, norm_g.reshape(1, DI))


def _scan_bwd(dyn, y, xbc_a, proj, hprev, dtg4, acs4, acs_t4, dsk_exp, norm_g, dproj, dm, name):
    Bl, S, DI, N, R, L, G = dm.Bl, dm.S, dm.DI, dm.N, dm.R, CHUNK, N_GROUPS
    RP = R * HEAD_DIM
    nc = S // L

    def body(dyn_ref, y_ref, z_ref, x_ref, hp_ref, dtg_ref, acs_ref, acst_ref, dsk_ref, ng_ref,
             _, dz_ref, dx_ref, da_ref, ddt_ref, ddsk_ref, dng_ref, dh_ref):
        xs_ref, bm_ref, cm_ref = x_ref.at[:, 0:RP], x_ref.at[:, RP:RP + N], x_ref.at[:, RP + N:RP + 2 * N]
        dxs_ref, dbm_ref, dcm_ref = dx_ref.at[:, 0:RP], dx_ref.at[:, RP:RP + N], dx_ref.at[:, RP + N:RP + 2 * N]
        b, c = pl.program_id(1), pl.program_id(2)

        @pl.when(c == 0)
        def _():
            dh_ref[...] = jnp.zeros_like(dh_ref)

        @pl.when((b == 0) & (c == 0))
        def _():
            ddsk_ref[...] = jnp.zeros_like(ddsk_ref)
            dng_ref[...] = jnp.zeros_like(dng_ref)

        row = lax.broadcasted_iota(jnp.int32, (L, L), 0)
        col = lax.broadcasted_iota(jnp.int32, (L, L), 1)
        causal, anti = row >= col, col >= row
        lane = lax.broadcasted_iota(jnp.int32, (L, LANES), 1)
        etb = (lax.shift_right_logical(lax.broadcasted_iota(jnp.int32, (RP, LANES), 0), 6)
               == lax.broadcasted_iota(jnp.int32, (RP, LANES), 1)).astype(MXU_DTYPE)

        dtg, acs, acs_t = dtg_ref[...], acs_ref[...], acst_ref[...]
        xs, z, y, dyn = xs_ref[...], z_ref[...], y_ref[...], dyn_ref[...]
        bmb, cmb = bm_ref[...].astype(MXU_DTYPE), cm_ref[...].astype(MXU_DTYPE)
        hpb = hp_ref[...]
        ng = ng_ref[...]

        sz = _sigmoid(z)
        siluz = z * sz
        yg = y * siluz
        rr = lax.rsqrt(jnp.mean(yg * yg, axis=-1, keepdims=True) + EPS)
        yhat = yg * rr
        dng_ref[...] += jnp.sum(dyn * yhat, axis=0, keepdims=True)
        dyhat = dyn * ng
        dyg = rr * (dyhat - yhat * jnp.mean(dyhat * yhat, axis=-1, keepdims=True))
        dy = dyg * siluz
        dz_ref[...] = (dyg * y * (sz * (1.0 + z * (1.0 - sz)))).astype(dz_ref.dtype)

        dxs = dy * dsk_ref[...]
        ddsk_ref[...] += jnp.sum(dy * xs, axis=0, keepdims=True)

        acs_last = acs[L - 1:L, :]
        dt_exp = _expand_cols(dtg, R, lane)
        ea_exp = _expand_cols(jnp.exp(acs), R, lane)
        ds_exp = _expand_cols(jnp.exp(acs_last - acs), R, lane)
        xdt = xs * dt_exp
        xb = xdt.astype(MXU_DTYPE)
        dyb = dy.astype(MXU_DTYPE)
        cd = jnp.exp(acs_last)
        cd_rows = _head_rows(jnp.exp(acs_t[:, L - 1:L]), R, N)

        q_ = _dg(cmb, hpb, NT)
        dq = dy * ea_exp
        dqb = dq.astype(MXU_DTYPE)
        dcm = _dg(dqb, hpb, NN)
        dh_yoff = _dg(dqb, cmb, TN)

        dhn = dh_ref[...]
        wprod = dhn * hpb.astype(F32)
        per_head = jnp.concatenate(
            [jnp.sum(wprod[HEAD_DIM * r:HEAD_DIM * (r + 1), :], axis=0, keepdims=True) for r in range(R)]
            + ([jnp.zeros((8 - R, N), F32)] if R < 8 else []), axis=0)
        dcd_col = jnp.sum(per_head, axis=1, keepdims=True)
        diag8 = lax.broadcasted_iota(jnp.int32, (8, LANES), 0) == lax.broadcasted_iota(jnp.int32, (8, LANES), 1)
        dcd_lane = jnp.sum(jnp.where(diag8, dcd_col, 0.0), axis=0, keepdims=True)
        d_a_last = dcd_lane * cd
        dh_ref[...] = dhn * cd_rows + dh_yoff
        dhnb = dhn.astype(MXU_DTYPE)

        e_ = _dg(bmb, dhnb, NT)
        dxdt = ds_exp * e_
        xds = xdt * ds_exp
        dbm = _dg(xds.astype(MXU_DTYPE), dhnb, NN)

        sg = _dg(cmb, bmb, NT)
        sg_t = _dg(bmb, cmb, NT)
        dsg = jnp.zeros((L, L), F32)
        dsg_t = jnp.zeros((L, L), F32)
        d_a = jnp.zeros((L, LANES), F32)
        parts = []
        for q in range(R // 2):
            x2 = xb[:, LANES * q:LANES * (q + 1)]
            dy2 = dyb[:, LANES * q:LANES * (q + 1)]
            dxs2 = []
            for hh, r in enumerate((2 * q, 2 * q + 1)):
                mine = (lane < HEAD_DIM) if hh == 0 else (lane >= HEAD_DIM)
                diff = acs[:, r:r + 1] - acs_t[r:r + 1, :]
                dec = jnp.exp(jnp.where(causal, diff, -1e30))
                dec_t = jnp.exp(jnp.where(anti, -diff, -1e30))
                dy2m = jnp.where(mine, dy2, jnp.zeros_like(dy2))
                dm_ = _dg(dy2m, x2, NT)
                dm_t = _dg(x2, dy2m, NT)
                m_t = sg_t * dec_t
                da_col = jnp.sum(dm_ * (sg * dec) - dm_t * m_t, axis=1, keepdims=True)
                d_a = d_a + jnp.where(lane == r, da_col, 0.0)
                dsg = dsg + dm_ * dec
                dsg_t = dsg_t + dm_t * dec_t
                dxs2.append(_dg(m_t.astype(MXU_DTYPE), dy2, NN))
            parts.append(jnp.where(lane < HEAD_DIM, dxs2[0], dxs2[1]))
        dxdt = dxdt + jnp.concatenate(parts, axis=1)
        dcm_ref[...] = dcm + _dg(dsg.astype(MXU_DTYPE), bmb, NN)
        dbm_ref[...] = dbm + _dg(dsg_t.astype(MXU_DTYPE), cmb, NN)
        dxs_ref[...] = dxs + dxdt * dt_exp

        hs = _two_term_dot(jnp.concatenate([dq * q_ - xds * e_, xds * e_, dxdt * xs], axis=0), etb)
        t2 = hs[L:2 * L, :]
        rowl = lax.broadcasted_iota(jnp.int32, (L, LANES), 0)
        d_a_last = d_a_last + jnp.sum(t2, axis=0, keepdims=True)
        da_ref[...] = d_a + hs[0:L, :] + jnp.where(rowl == L - 1, d_a_last, 0.0)
        ddt_ref[...] = hs[2 * L:3 * L, :]

    oz = dm.OZ // RP
    grp = pl.BlockSpec((None, L, RP), lambda g, b, c: (b, nc - 1 - c, g))
    zspec = pl.BlockSpec((None, L, RP), lambda g, b, c: (b, nc - 1 - c, oz + g))
    lanes = pl.BlockSpec((None, L, LANES), lambda g, b, c: (b, nc - 1 - c, g))
    xg = pl.BlockSpec((None, L, RP + 2 * N), lambda g, b, c: (b, nc - 1 - c, g))
    chan = pl.BlockSpec((1, RP), lambda g, b, c: (0, g))
    wide = jax.ShapeDtypeStruct((Bl, S, G * LANES), F32)
    return pl.pallas_call(
        body, name=name, grid=(G, Bl, nc),
        in_specs=[grp, grp, zspec, xg,
                  pl.BlockSpec((None, None, None, RP, N), lambda g, b, c: (b, g, nc - 1 - c, 0, 0)),
                  lanes, lanes, pl.BlockSpec((None, None, 8, L), lambda g, b, c: (b, nc - 1 - c, g, 0)),
                  chan, chan, _ANY],
        out_specs=[zspec, xg, lanes, lanes, chan, chan],
        out_shape=[jax.ShapeDtypeStruct(dproj.shape, dproj.dtype), jax.ShapeDtypeStruct(xbc_a.shape, F32),
                   wide, wide, jax.ShapeDtypeStruct((1, DI), F32), jax.ShapeDtypeStruct((1, DI), F32)],
        input_output_aliases={10: 0},
        scratch_shapes=[pltpu.VMEM((RP, N), F32)],
        compiler_params=_cp(("arbitrary", "arbitrary", "arbitrary")),
    )(dyn, y, proj, xbc_a, hprev, dtg4, acs4, acs_t4, dsk_exp, norm_g.reshape(1, DI), dproj)


def _adam_update(w, m, v, g):
    c1 = 1.0 - ADAM_B1 ** ADAM_STEP
    c2 = 1.0 - ADAM_B2 ** ADAM_STEP
    nm = ADAM_B1 * m + (1.0 - ADAM_B1) * g
    nv = ADAM_B2 * v + (1.0 - ADAM_B2) * (g * g)
    return nm, nv, -ADAM_LR * ((nm / c1) / (jnp.sqrt(nv / c2) + ADAM_EPS) + ADAM_WD * w)


def _adamw(w, m, v, g, name, tr=128):
    rows, cols = w.shape
    tr = _tile(rows, tr, 8)

    def body(w_ref, m_ref, v_ref, ga_ref, g_ref, d_ref, nm_ref, nv_ref):
        g = ga_ref[...]
        g_ref[...] = g
        nm_ref[...], nv_ref[...], d_ref[...] = _adam_update(w_ref[...], m_ref[...], v_ref[...], g)

    blk = pl.BlockSpec((tr, cols), lambda i: (i, 0))
    shp = jax.ShapeDtypeStruct((rows, cols), F32)
    return pl.pallas_call(
        body, name=name, grid=(rows // tr,), in_specs=[blk] * 4, out_specs=[blk] * 4,
        out_shape=[shp] * 4, compiler_params=_cp(("parallel",)),
    )(w, m, v, g)


def _adamw_layers(w, m, v, g_mine, g_theirs, core, name, tr=128):
    _, rows, cols = w.shape
    tr = _tile(rows, tr, 8)

    def body(c_ref, w_ref, m_ref, v_ref, ga_ref, gb_ref, g_ref, d_ref, nm_ref, nv_ref):
        g = jnp.where(pl.program_id(0) == c_ref[0], ga_ref[...], gb_ref[...])
        g_ref[...] = g
        nm_ref[...], nv_ref[...], d_ref[...] = _adam_update(w_ref[...], m_ref[...], v_ref[...], g)

    lay = pl.BlockSpec((None, tr, cols), lambda l, i, c_ref: (l, i, 0))
    one = pl.BlockSpec((tr, cols), lambda l, i, c_ref: (i, 0))
    shp = jax.ShapeDtypeStruct(w.shape, F32)
    return pl.pallas_call(
        body, name=name,
        grid_spec=pltpu.PrefetchScalarGridSpec(num_scalar_prefetch=1, grid=(2, rows // tr),
                                               in_specs=[lay, lay, lay, one, one], out_specs=[lay] * 4),
        out_shape=[shp] * 4, compiler_params=_cp(("parallel", "parallel")),
    )(core, w, m, v, g_mine, g_theirs)


def _sum_slots(buf, name, tr=256):
    n, rows, cols = buf.shape
    tr = _tile(rows, tr, 8)

    def body(b_ref, o_ref):
        acc = b_ref[0].astype(F32)
        for k in range(1, n):
            acc = acc + b_ref[k].astype(F32)
        o_ref[...] = acc

    return pl.pallas_call(
        body, name=name, grid=(rows // tr,),
        in_specs=[pl.BlockSpec((n, tr, cols), lambda i: (0, i, 0))],
        out_specs=pl.BlockSpec((tr, cols), lambda i: (i, 0)),
        out_shape=jax.ShapeDtypeStruct((rows, cols), F32), compiler_params=_cp(("parallel",)),
    )(buf)


_ANY = pl.BlockSpec(memory_space=pl.ANY)


def _exchange_chips(src, per_dest, name):
    rows, cols = src.shape[-2:]

    def body(in_ref, out_ref, send_sems, recv_sems, local_sem):
        x, y, c = lax.axis_index("x"), lax.axis_index("y"), lax.axis_index("c")
        me = 2 * x + y
        chips = [(1 - x, y), (x, 1 - y), (1 - x, 1 - y)]

        def block(j):
            return in_ref.at[j] if per_dest else in_ref

        mine = pltpu.make_async_copy(block(me), out_ref.at[me], local_sem)
        mine.start()
        sends = []
        for k, (px, py) in enumerate(chips):
            cp = pltpu.make_async_remote_copy(
                src_ref=block(2 * px + py), dst_ref=out_ref.at[me], send_sem=send_sems.at[k],
                recv_sem=recv_sems.at[k], device_id=(px, py, c), device_id_type=MESH)
            cp.start()
            sends.append(cp)
        for k, (px, py) in enumerate(chips):
            pltpu.make_async_remote_copy(
                src_ref=block(me), dst_ref=out_ref.at[2 * px + py], send_sem=send_sems.at[k],
                recv_sem=recv_sems.at[k], device_id=(px, py, c), device_id_type=MESH).wait_recv()
        for cp in sends:
            cp.wait_send()
        mine.wait()

    return pl.pallas_call(
        body, name=name, in_specs=[_ANY], out_specs=_ANY,
        out_shape=jax.ShapeDtypeStruct((N_CHIPS, rows, cols), src.dtype),
        scratch_shapes=[pltpu.SemaphoreType.DMA((3,)), pltpu.SemaphoreType.DMA((3,)), pltpu.SemaphoreType.DMA(())],
    )(src)


def _shard_window(ref, kind, j, lead=()):
    if kind == "slots":
        return ref.at[(j,) + lead]
    r, c = ref.shape[-2] // (N_CHIPS if kind == "rows" else 1), ref.shape[-1] // (N_CHIPS if kind == "cols" else 1)
    full = tuple(slice(None) for _ in range(len(ref.shape) - 2 - len(lead)))
    if kind == "rows":
        return ref.at[lead + full + (pl.ds(pl.multiple_of(j * r, 16), r), slice(None))]
    return ref.at[lead + full + (slice(None), pl.ds(pl.multiple_of(j * c, LANES), c))]


def _gather_kind(shard, axis):
    if axis == 1:
        return "rows"
    return "cols" if shard.shape[2] % LANES == 0 else "slots"


def _gather_weights(shards, axes, name):
    kinds = ["rows" if ax == 1 else "slots" for ax in axes]
    nw = len(shards)

    def out_shape(s, kind):
        d, r, c = s.shape
        shp = {"rows": (d, N_CHIPS * r, c), "cols": (d, r, N_CHIPS * c), "slots": (N_CHIPS, d, r, c)}[kind]
        return jax.ShapeDtypeStruct(shp, s.dtype)

    assert all(s.shape[0] == 2 for s in shards)

    def body(*refs):
        ins, outs = refs[:nw], refs[nw:2 * nw]
        ici_send, ici_recv, d2d_send, d2d_recv, local_sems = refs[2 * nw:]
        x, y, c = lax.axis_index("x"), lax.axis_index("y"), lax.axis_index("c")
        me = 2 * x + y
        chips = [(1 - x, y), (x, 1 - y), (1 - x, 1 - y)]
        sends = []
        for i in range(nw):
            if kinds[i] == "rows":
                own = pltpu.make_async_copy(ins[i], _shard_window(outs[i], kinds[i], me), local_sems.at[i])
                own.start()
                sends.append((own, False))
        for i in range(nw):
            for k, (px, py) in enumerate(chips):
                cp = pltpu.make_async_remote_copy(
                    src_ref=ins[i].at[c], dst_ref=_shard_window(outs[i], kinds[i], me, (c,)),
                    send_sem=ici_send.at[3 * i + k], recv_sem=ici_recv.at[3 * i + k],
                    device_id=(px, py, c), device_id_type=MESH)
                cp.start()
                sends.append((cp, True))
        for i in range(nw):
            for k, (px, py) in enumerate(chips):
                win = _shard_window(outs[i], kinds[i], 2 * px + py, (c,))
                pltpu.make_async_remote_copy(
                    src_ref=ins[i].at[c], dst_ref=win, send_sem=ici_send.at[3 * i + k], recv_sem=ici_recv.at[3 * i + k],
                    device_id=(px, py, c), device_id_type=MESH).wait_recv()
                fw = pltpu.make_async_remote_copy(
                    src_ref=win, dst_ref=win, send_sem=d2d_send.at[3 * i + k], recv_sem=d2d_recv.at[3 * i + k],
                    device_id=(x, y, 1 - c), device_id_type=MESH)
                fw.start()
                sends.append((fw, True))
        for i in range(nw):
            for k, (px, py) in enumerate(chips):
                win = _shard_window(outs[i], kinds[i], 2 * px + py, (1 - c,))
                pltpu.make_async_remote_copy(
                    src_ref=win, dst_ref=win, send_sem=d2d_send.at[3 * i + k], recv_sem=d2d_recv.at[3 * i + k],
                    device_id=(x, y, 1 - c), device_id_type=MESH).wait_recv()
        for cp, remote in sends:
            cp.wait_send() if remote else cp.wait()

    outs = pl.pallas_call(
        body, name=name, in_specs=[_ANY] * nw, out_specs=[_ANY] * nw,
        out_shape=[out_shape(s, k) for s, k in zip(shards, kinds)],
        scratch_shapes=[pltpu.SemaphoreType.DMA((3 * nw,)), pltpu.SemaphoreType.DMA((3 * nw,)),
                        pltpu.SemaphoreType.DMA((3 * nw,)), pltpu.SemaphoreType.DMA((3 * nw,)),
                        pltpu.SemaphoreType.DMA((nw,))],
    )(*shards)
    me = 2 * lax.axis_index("x") + lax.axis_index("y")
    return [jnp.concatenate([jnp.where(me == j, s, o[j]) for j in range(N_CHIPS)], axis=2) if k == "slots" else o
            for s, o, k in zip(shards, outs, kinds)]


def _swap_other_layer(gst, name):
    nv = len(gst)

    def body(*refs):
        ins, outs, send_sems, recv_sems = refs[:nv], refs[nv:2 * nv], refs[2 * nv], refs[2 * nv + 1]
        x, y, c = lax.axis_index("x"), lax.axis_index("y"), lax.axis_index("c")
        cps = [pltpu.make_async_remote_copy(src_ref=ins[i].at[1 - c], dst_ref=outs[i], send_sem=send_sems.at[i],
                                            recv_sem=recv_sems.at[i], device_id=(x, y, 1 - c), device_id_type=MESH)
               for i in range(nv)]
        for cp in cps:
            cp.start()
        for cp in cps:
            cp.wait()

    return pl.pallas_call(
        body, name=name, in_specs=[_ANY] * nv, out_specs=[_ANY] * nv,
        out_shape=[jax.ShapeDtypeStruct(v.shape[1:], v.dtype) for v in gst],
        scratch_shapes=[pltpu.SemaphoreType.DMA((nv,)), pltpu.SemaphoreType.DMA((nv,))],
    )(*gst)


def _pair_sum(g, other, core, name, tr=256):
    _, rows, cols = g.shape
    tr = _tile(rows, tr, 16)

    def body(c_ref, g_ref, o_ref, s_ref):
        s_ref[...] = (g_ref[...].astype(F32) + o_ref[...].astype(F32)).astype(s_ref.dtype)

    blk = pl.BlockSpec((tr, cols), lambda i, c_ref: (i, 0))
    return pl.pallas_call(
        body, name=name,
        grid_spec=pltpu.PrefetchScalarGridSpec(
            num_scalar_prefetch=1, grid=(rows // tr,),
            in_specs=[pl.BlockSpec((None, tr, cols), lambda i, c_ref: (c_ref[0], i, 0)), blk], out_specs=blk),
        out_shape=jax.ShapeDtypeStruct((rows, cols), g.dtype), compiler_params=_cp(("parallel",)),
    )(core, g, other)


def _scatter_layer(parts, kinds, name):
    nw = len(parts)

    def shard_shape(g, kind):
        return g.shape[-2] // (N_CHIPS if kind == "rows" else 1), g.shape[-1] // (N_CHIPS if kind == "cols" else 1)

    def body(*refs):
        ins, outs = refs[:nw], refs[nw:2 * nw]
        send_sems, recv_sems, local_sems = refs[2 * nw:]
        x, y, c = lax.axis_index("x"), lax.axis_index("y"), lax.axis_index("c")
        me = 2 * x + y
        chips = [(1 - x, y), (x, 1 - y), (1 - x, 1 - y)]
        sends = []
        for i in range(nw):
            own = pltpu.make_async_copy(_shard_window(ins[i], kinds[i], me), outs[i].at[me], local_sems.at[i])
            own.start()
            sends.append((own, False))
            for k, (px, py) in enumerate(chips):
                cp = pltpu.make_async_remote_copy(
                    src_ref=_shard_window(ins[i], kinds[i], 2 * px + py), dst_ref=outs[i].at[me],
                    send_sem=send_sems.at[3 * i + k], recv_sem=recv_sems.at[3 * i + k],
                    device_id=(px, py, c), device_id_type=MESH)
                cp.start()
                sends.append((cp, True))
        for i in range(nw):
            for k, (px, py) in enumerate(chips):
                pltpu.make_async_remote_copy(
                    src_ref=_shard_window(ins[i], kinds[i], me), dst_ref=outs[i].at[2 * px + py],
                    send_sem=send_sems.at[3 * i + k], recv_sem=recv_sems.at[3 * i + k],
                    device_id=(px, py, c), device_id_type=MESH).wait_recv()
        for cp, remote in sends:
            cp.wait_send() if remote else cp.wait()

    return pl.pallas_call(
        body, name=name, in_specs=[_ANY] * nw, out_specs=[_ANY] * nw,
        out_shape=[jax.ShapeDtypeStruct((N_CHIPS,) + shard_shape(g, k), g.dtype) for g, k in zip(parts, kinds)],
        scratch_shapes=[pltpu.SemaphoreType.DMA((3 * nw,)), pltpu.SemaphoreType.DMA((3 * nw,)),
                        pltpu.SemaphoreType.DMA((nw,))],
    )(*parts)


def _sibling_swap(vs, name):
    nv = len(vs)

    def body(*refs):
        ins, outs, send_sems, recv_sems = refs[:nv], refs[nv:2 * nv], refs[2 * nv], refs[2 * nv + 1]
        x, y, c = lax.axis_index("x"), lax.axis_index("y"), lax.axis_index("c")
        cps = [pltpu.make_async_remote_copy(src_ref=ins[i], dst_ref=outs[i], send_sem=send_sems.at[i],
                                            recv_sem=recv_sems.at[i], device_id=(x, y, 1 - c), device_id_type=MESH)
               for i in range(nv)]
        for cp in cps:
            cp.start()
        for cp in cps:
            cp.wait()

    return pl.pallas_call(
        body, name=name, in_specs=[_ANY] * nv, out_specs=[_ANY] * nv,
        out_shape=[jax.ShapeDtypeStruct(v.shape, v.dtype) for v in vs],
        scratch_shapes=[pltpu.SemaphoreType.DMA((nv,)), pltpu.SemaphoreType.DMA((nv,))],
    )(*vs)


def _allgather_all(v, name):
    rows, cols = v.shape

    def body(in_ref, out_ref, send_sems, recv_sems, local_sem):
        x, y, c = lax.axis_index("x"), lax.axis_index("y"), lax.axis_index("c")
        me = 4 * x + 2 * y + c
        peers = []
        for k in range(1, N_DEV):
            peers.append(((1 - x) if k & 4 else x, (1 - y) if k & 2 else y, (1 - c) if k & 1 else c))
        mine = pltpu.make_async_copy(in_ref, out_ref.at[me], local_sem)
        mine.start()
        sends = []
        for k, peer in enumerate(peers):
            cp = pltpu.make_async_remote_copy(src_ref=in_ref, dst_ref=out_ref.at[me], send_sem=send_sems.at[k],
                                              recv_sem=recv_sems.at[k], device_id=peer, device_id_type=MESH)
            cp.start()
            sends.append(cp)
        for k, (px, py, pc) in enumerate(peers):
            pltpu.make_async_remote_copy(src_ref=in_ref, dst_ref=out_ref.at[4 * px + 2 * py + pc],
                                         send_sem=send_sems.at[k], recv_sem=recv_sems.at[k],
                                         device_id=(px, py, pc), device_id_type=MESH).wait_recv()
        for cp in sends:
            cp.wait_send()
        mine.wait()

    return pl.pallas_call(
        body, name=name, in_specs=[_ANY], out_specs=_ANY,
        out_shape=jax.ShapeDtypeStruct((N_DEV, rows, cols), v.dtype),
        scratch_shapes=[pltpu.SemaphoreType.DMA((N_DEV - 1,)), pltpu.SemaphoreType.DMA((N_DEV - 1,)),
                        pltpu.SemaphoreType.DMA(())],
    )(v)


def _pack(arrs, dtype, width, row_mult):
    flat = jnp.concatenate([a.reshape(-1).astype(dtype) for a in arrs])
    unit = width * row_mult
    total = -(-flat.shape[0] // unit) * unit
    return jnp.pad(flat, (0, total - flat.shape[0])).reshape(-1, width)


def _unpack(buf, shapes):
    flat = buf.reshape(-1)
    out, off = [], 0
    for shp in shapes:
        n = 1
        for d in shp:
            n *= d
        out.append(flat[off:off + n].reshape(shp))
        off += n
    return out


class _Dims:
    pass


def _dims(x, ssd_dt_bias, ssd_norm_g, ssd_conv_b, ffn_conv_b):
    dm = _Dims()
    dm.Bl, dm.S, dm.D = x.shape
    dm.H, dm.DI, dm.CD = ssd_dt_bias.shape[-1], ssd_norm_g.shape[-1], ssd_conv_b.shape[-1]
    dm.N = (dm.CD - dm.DI) // (2 * N_GROUPS)
    dm.R = dm.H // N_GROUPS
    dm.DFF = ffn_conv_b.shape[-1] // 2
    D = dm.D
    dm.OB, dm.OC, dm.OH, dm.OX = 0, D, 2 * D, 3 * D
    dm.OZ = dm.OX + dm.CD
    dm.OG = -(-(dm.OZ + dm.DI) // (2 * D)) * (2 * D)
    dm.GPAD = dm.OG - (dm.OZ + dm.DI)
    dm.ODT = dm.OG + 2 * D
    dm.NP = dm.ODT + LANES
    assert dm.OZ % (dm.R * HEAD_DIM) == 0
    assert dm.DI // dm.H == HEAD_DIM and dm.N == LANES and dm.R % 2 == 0 and dm.S % CHUNK == 0 and dm.H <= LANES
    return dm


def _group_xbc(v, dm):
    rp, n = dm.R * HEAD_DIM, dm.N
    parts = []
    for g in range(N_GROUPS):
        parts += [v[..., g * rp:(g + 1) * rp], v[..., dm.DI + g * n:dm.DI + (g + 1) * n],
                  v[..., dm.DI + (N_GROUPS + g) * n:dm.DI + (N_GROUPS + g + 1) * n]]
    return jnp.concatenate(parts, axis=-1)


def _ungroup_xbc(v, dm):
    rp, n = dm.R * HEAD_DIM, dm.N
    gw = rp + 2 * n
    xs = [v[..., g * gw:g * gw + rp] for g in range(N_GROUPS)]
    bs = [v[..., g * gw + rp:g * gw + rp + n] for g in range(N_GROUPS)]
    cs = [v[..., g * gw + rp + n:(g + 1) * gw] for g in range(N_GROUPS)]
    return jnp.concatenate(xs + bs + cs, axis=-1)


def _permute_w_in(w, dm):
    o, sc = dm.DI + dm.CD, dm.DI + dm.CD + dm.H
    zeros = lambda n: jnp.zeros((w.shape[0], n), w.dtype)
    return jnp.concatenate([w[:, sc:sc + 3 * dm.D], _group_xbc(w[:, dm.DI:o], dm), w[:, :dm.DI], zeros(dm.GPAD),
                            w[:, sc + 3 * dm.D:], w[:, o:o + dm.H], zeros(LANES - dm.H)], axis=1)


def _unpermute_w_in(dw, dm):
    return jnp.concatenate([dw[..., dm.OZ:dm.OZ + dm.DI], _ungroup_xbc(dw[..., dm.OX:dm.OX + dm.CD], dm),
                            dw[..., dm.ODT:dm.ODT + dm.H], dw[..., :3 * dm.D], dw[..., dm.OG:dm.OG + 2 * dm.D]], axis=-1)


def _lane_pad(v):
    return jnp.pad(v.reshape(1, -1).astype(F32), ((0, 0), (0, LANES - v.shape[-1])))


def _pad8(w):
    return jnp.pad(w.astype(F32), ((0, 8 - w.shape[0]), (0, 0)))


def _head_select(dm):
    j = jnp.arange(LANES)[None, :, None]
    r = jnp.arange(LANES)[None, None, :]
    g = jnp.arange(N_GROUPS)[:, None, None]
    sel = ((j == dm.R * g + r) & (r < dm.R)).astype(F32)
    selcat = jnp.transpose(sel, (1, 0, 2)).reshape(LANES, N_GROUPS * LANES)
    selbig = jnp.transpose(sel[:, :, :8], (0, 2, 1)).reshape(N_GROUPS * 8, LANES)
    return selcat, selbig


def _mix_fwd(dm, h, w, sp, sel, tag):
    Bl, S, D = dm.Bl, dm.S, dm.D
    T = Bl * S
    proj = _matmul(h.reshape(T, D), w["w_in_p"], "NN", F32, tag + "_in_proj", tm=2048, tn=1152).reshape(Bl, S, dm.NP)
    xbc_a = _ssd_conv_fwd(proj, sp["ssd_conv_w8"], sp["ssd_conv_b"], dm.OX, dm.CD, tag + "_ssd_conv")
    dtg, acs, acs_t = _ssd_prep(proj, sp["dt_bias"], sp["a_log"], sel[0], sel[1], dm, tag + "_ssd_prep")
    y, yn, hprev = _scan_fwd(xbc_a, proj, dtg, acs, acs_t, sp["dsk_exp"], sp["ssd_norm_g"], dm, tag + "_ssd_scan")
    y_ssd = _matmul(yn.reshape(T, dm.DI), w["w_ssd_out"], "NN", F32, tag + "_ssd_out", tk=2048).reshape(Bl, S, D)
    s = _sc_conv_fwd(proj, sp["sc_conv_w8"], (dm.OB, dm.OC, dm.OH), D, tag + "_sc_conv")
    y_sc = _matmul(s.reshape(T, D), w["w_sc_out"], "NN", F32, tag + "_sc_out", tk=1024).reshape(Bl, S, D)
    mixin = _merge_fwd(proj, y_ssd, y_sc, dm.OG, D, tag + "_merge")
    mix = _matmul(mixin.reshape(T, D), w["w_o"], "NN", F32, tag + "_o", tk=1024).reshape(Bl, S, D)
    return mix, (h, proj, xbc_a, y, yn, hprev, y_ssd, y_sc, s, mixin, dtg, acs, acs_t)


def _mix_bwd(dm, dmix, saved, w, sp, sel, tag, gbig, slab):
    Bl, S, D = dm.Bl, dm.S, dm.D
    T = Bl * S
    h, proj, xbc_a, y, yn, hprev, y_ssd, y_sc, s, mixin, dtg, acs, acs_t = saved
    dmix2 = dmix.reshape(T, D)
    g = {}
    gbig["w_o"] = _matmul(mixin.reshape(T, D), dmix2, "TN", WIRE_DTYPE, tag + "_dw_o", slab=(gbig.get("w_o"),) + slab)
    dmixin = _matmul(dmix2, w["w_o"], "NT", F32, tag + "_d_o", tk=1024).reshape(Bl, S, D)
    dproj0 = jnp.zeros((Bl, S, dm.NP), MXU_DTYPE) if dm.GPAD else None
    dy_ssd, dy_sc, dproj = _merge_bwd(dmixin, proj, y_ssd, y_sc, dm.OG, D, dproj0, tag + "_merge_bwd")
    gbig["w_sc_out"] = _matmul(s.reshape(T, D), dy_sc.reshape(T, D), "TN", WIRE_DTYPE, tag + "_dw_sc_out",
                               slab=(gbig.get("w_sc_out"),) + slab)
    ds = _matmul(dy_sc.reshape(T, D), w["w_sc_out"], "NT", F32, tag + "_d_sc_out", tk=1024).reshape(Bl, S, D)
    dproj, dscw = _sc_conv_bwd(ds, proj, sp["sc_conv_w8"], (dm.OB, dm.OC, dm.OH), D, dproj, tag + "_sc_conv_bwd")
    g["sc_conv_w"] = dscw[:3]
    gbig["w_ssd_out"] = _matmul(yn.reshape(T, dm.DI), dy_ssd.reshape(T, D), "TN", WIRE_DTYPE, tag + "_dw_ssd_out",
                                slab=(gbig.get("w_ssd_out"),) + slab)
    dyn = _matmul(dy_ssd.reshape(T, D), w["w_ssd_out"], "NT", F32, tag + "_d_ssd_out", tk=1024).reshape(Bl, S, dm.DI)
    dproj, dxa, d_a, ddt, ddsk, dng = _scan_bwd(dyn, y, xbc_a, proj, hprev, dtg, acs, acs_t, sp["dsk_exp"],
                                                sp["ssd_norm_g"], dproj, dm, tag + "_ssd_scan_bwd")
    dproj, dpar = _ssd_post(d_a, ddt, dtg, proj, sp["dt_bias"], sp["a_log"], sel[0], dproj, dm, tag + "_ssd_post")
    g["ssd_dt_bias"], g["ssd_a_log"] = dpar[0, :dm.H], dpar[1, :dm.H]
    g["ssd_d"] = jnp.sum(ddsk.reshape(dm.H, HEAD_DIM), axis=-1)
    g["ssd_norm_g"] = dng[0]
    dproj, dcw, dcb = _ssd_conv_bwd(dxa, proj, sp["ssd_conv_w8"], sp["ssd_conv_b"], dm.OX, dm.CD, dproj,
                                    tag + "_ssd_conv_bwd")
    g["ssd_conv_w"], g["ssd_conv_b"] = _ungroup_xbc(dcw[:4], dm), _ungroup_xbc(dcb[0], dm)
    dproj = dproj.reshape(T, dm.NP)
    gbig["w_in_p"] = _matmul(h.reshape(T, D), dproj, "TN", WIRE_DTYPE, tag + "_dw_in", tn=1728, tk=2048,
                             slab=(gbig.get("w_in_p"),) + slab)
    dh = _matmul(dproj, w["w_in_p"], "NT", F32, tag + "_d_in", tm=2048, tk=1152).reshape(Bl, S, D)
    return dh, g


def _ffn_fwd(dm, h, w, sp, tag):
    Bl, S, D = dm.Bl, dm.S, dm.D
    T = Bl * S
    up = _matmul(h.reshape(T, D), w["w_up"], "NN", F32, tag + "_up", tm=2048, tn=1408).reshape(Bl, S, 2 * dm.DFF)
    a = _ffn_conv_fwd(up, sp["ffn_conv_w8"], sp["ffn_conv_b"], dm.DFF, tag + "_ffn_conv")
    f = _matmul(a.reshape(T, dm.DFF), w["w_down"], "NN", F32, tag + "_down", tk=2816).reshape(Bl, S, D)
    return f, (h, up, a)


def _ffn_bwd(dm, df, saved, w, sp, tag, gbig, slab):
    Bl, S, D = dm.Bl, dm.S, dm.D
    T = Bl * S
    h, up, a = saved
    df2 = df.reshape(T, D)
    g = {}
    gbig["w_down"] = _matmul(a.reshape(T, dm.DFF), df2, "TN", WIRE_DTYPE, tag + "_dw_down", tm=1408, tk=2048,
                             slab=(gbig.get("w_down"),) + slab)
    da = _matmul(df2, w["w_down"], "NT", F32, tag + "_d_down", tn=1408).reshape(Bl, S, dm.DFF)
    dup, dcw, dcb = _ffn_conv_bwd(da, up, sp["ffn_conv_w8"], sp["ffn_conv_b"], dm.DFF, tag + "_ffn_conv_bwd")
    g["ffn_conv_w"], g["ffn_conv_b"] = dcw[:3], dcb[0]
    dup = dup.reshape(T, 2 * dm.DFF)
    gbig["w_up"] = _matmul(h.reshape(T, D), dup, "TN", WIRE_DTYPE, tag + "_dw_up", tn=1408, tk=2048,
                           slab=(gbig.get("w_up"),) + slab)
    dh = _matmul(dup, w["w_up"], "NT", F32, tag + "_d_up", tm=2048, tk=1408).reshape(Bl, S, D)
    return dh, g


def _local_step(dm, x, c, target, wfull, small):
    Bl, S, D = dm.Bl, dm.S, dm.D
    depth = len(wfull)
    sel = _head_select(dm)
    c16 = jnp.pad(c.astype(F32), ((0, 16 - Bl), (0, 0)))
    sps, mods, acts = [], [], []
    for l in range(depth):
        sm = small[l]
        sps.append(dict(
            ssd_conv_w8=_pad8(_group_xbc(sm["ssd_conv_w"], dm)), ssd_conv_b=_group_xbc(sm["ssd_conv_b"], dm),
            dt_bias=_lane_pad(sm["ssd_dt_bias"]),
            a_log=_lane_pad(sm["ssd_a_log"]), dsk_exp=jnp.repeat(sm["ssd_d"].astype(F32), HEAD_DIM).reshape(1, dm.DI),
            ssd_norm_g=sm["ssd_norm_g"],
            sc_conv_w8=_pad8(sm["sc_conv_w"]), ffn_conv_w8=_pad8(sm["ffn_conv_w"]), ffn_conv_b=sm["ffn_conv_b"]))
        act, mod = _ada_fwd(c16, wfull[l]["ada_w"], sm["ada_b"], f"l{l}_ada")
        acts.append(act)
        mods.append(jnp.pad(mod[:Bl].reshape(Bl, 6, D), ((0, 0), (0, 2), (0, 0))))

    def sub(i):
        l, ffn = i // 2, i % 2
        sm = small[l]
        return dict(l=l, ffn=ffn, pre_g=sm["ffn_pre_g" if ffn else "mix_pre_g"],
                    post_g=sm["ffn_post_g" if ffn else "mix_post_g"], mod=mods[l], row=3 * ffn,
                    tag=f"l{l}_{'ffn' if ffn else 'mix'}")

    nsub = 2 * depth
    subs = [sub(i) for i in range(nsub)]
    xs, fs, saves = [x], [], []
    h = _norm_mod_fwd(x, subs[0]["pre_g"], subs[0]["mod"], subs[0]["row"], "l0_mix_pre_norm")
    for i, sb in enumerate(subs):
        l = sb["l"]
        if sb["ffn"]:
            f, sv = _ffn_fwd(dm, h, wfull[l], sps[l], sb["tag"])
        else:
            f, sv = _mix_fwd(dm, h, wfull[l], sps[l], sel, sb["tag"])
        nxt = None
        if i + 1 < nsub:
            nb = subs[i + 1]
            nxt = (nb["pre_g"], nb["mod"], nb["row"])
        xn, h = _post_norm_fwd(xs[-1], f, sb["post_g"], sb["mod"], sb["row"] + 2, nxt, sb["tag"] + "_post_norm")
        xs.append(xn)
        fs.append(f)
        saves.append(sv)

    dy, loss = _loss_fwd_bwd(xs[-1], target, "loss")

    grads = [dict() for _ in range(depth)]
    gbig = {}
    dmod = [[None] * 6 for _ in range(depth)]
    dx, dh = dy, None
    for i in reversed(range(nsub)):
        sb = subs[i]
        l = sb["l"]
        nxt = None
        if i + 1 < nsub:
            nb = subs[i + 1]
            nxt = (dh, xs[i + 1], nb["pre_g"], nb["mod"], nb["row"])
        dx, df, pb, shg = _norm_bwd(dx, nxt, (fs[i], sb["post_g"], sb["mod"], sb["row"] + 2), sb["tag"] + "_post_norm_bwd")
        if nxt is not None:
            nb = subs[i + 1]
            dmod[nb["l"]][nb["row"]], dmod[nb["l"]][nb["row"] + 1] = pb[:, 0], pb[:, 1]
            grads[nb["l"]]["ffn_pre_g" if nb["ffn"] else "mix_pre_g"] = shg[0]
        dmod[l][sb["row"] + 2] = pb[:, 2]
        grads[l]["ffn_post_g" if sb["ffn"] else "mix_post_g"] = shg[1]
        if sb["ffn"]:
            dh, g = _ffn_bwd(dm, df, saves[i], wfull[l], sps[l], sb["tag"], gbig, (l, depth))
        else:
            dh, g = _mix_bwd(dm, df, saves[i], wfull[l], sps[l], sel, sb["tag"], gbig, (l, depth))
        grads[l].update(g)
    sb = subs[0]
    grad_x, _, pb, shg = _norm_bwd(dx, (dh, xs[0], sb["pre_g"], sb["mod"], sb["row"]), None, "l0_mix_pre_norm_bwd")
    dmod[0][0], dmod[0][1] = pb[:, 0], pb[:, 1]
    grads[0]["mix_pre_g"] = shg[0]

    for l in range(depth):
        dm6 = jnp.concatenate(dmod[l], axis=-1)
        grads[l]["ada_b"] = jnp.sum(dm6, axis=0)
        dm16 = jnp.pad(dm6, ((0, 16 - Bl), (0, 0))).astype(MXU_DTYPE)
        gbig["ada_w"] = _matmul(acts[l], dm16, "TN", WIRE_DTYPE, f"l{l}_dw_ada", slab=(gbig.get("ada_w"), l, depth))
    return loss, grad_x, grads, gbig


_WEIGHTS = ("ada_w", "ada_b", "mix_pre_g", "mix_post_g", "w_in", "ssd_conv_w", "ssd_conv_b", "ssd_dt_bias",
            "ssd_a_log", "ssd_d", "ssd_norm_g", "w_ssd_out", "sc_conv_w", "w_sc_out", "w_o", "ffn_pre_g",
            "ffn_post_g", "w_up", "ffn_conv_w", "ffn_conv_b", "w_down")
_INPUTS = ("x", "c") + _WEIGHTS + ("loss_target",) + tuple("m_" + n for n in _WEIGHTS) + tuple("v_" + n for n in _WEIGHTS)
_BIG = {"ada_w": 2, "w_in": 2, "w_ssd_out": 1, "w_sc_out": 1, "w_o": 1, "w_up": 2, "w_down": 1}
_CONV = ("ssd_conv_w", "sc_conv_w", "ffn_conv_w")
_SMALL = tuple(n for n in _WEIGHTS if n not in _BIG)


def _step(a):
    x, c, target = a["x"], a["c"], a["loss_target"]
    depth = a["ada_w"].shape[0]
    dm = _dims(x, a["ssd_dt_bias"], a["ssd_norm_g"], a["ssd_conv_b"], a["ffn_conv_b"])
    chip = 2 * lax.axis_index("x") + lax.axis_index("y")

    shards = [a[n].astype(WIRE_DTYPE) for n in _BIG]
    axes = list(_BIG.values())
    kinds = [_gather_kind(s, ax) for s, ax in zip(shards, axes)]
    full = {n: w.astype(MXU_DTYPE) for n, w in zip(_BIG, _gather_weights(shards, axes, "gather_weights"))}
    conv_shapes = [a[n].shape for n in _CONV]
    gotc = _exchange_chips(_pack([a[n] for n in _CONV], F32, LANES, 8), False, "gather_conv_weights")
    piecesc = [_unpack(gotc[j], conv_shapes) for j in range(N_CHIPS)]
    fullc = {n: jnp.concatenate([piecesc[j][i] for j in range(N_CHIPS)], axis=2) for i, n in enumerate(_CONV)}

    wfull, small = [], []
    for l in range(depth):
        wf = {n: full[n][l] for n in _BIG if n != "w_in"}
        wf["w_in_p"] = _permute_w_in(full["w_in"][l], dm)
        wfull.append(wf)
        small.append({n: (fullc[n][l] if n in _CONV else a[n][l]) for n in _SMALL})

    loss_part, grad_x, grads, gbig = _local_step(dm, x, c, target, wfull, small)

    core = lax.axis_index("c").astype(jnp.int32).reshape(1)
    gst = []
    for n, kind in zip(_BIG, kinds):
        g = _unpermute_w_in(gbig["w_in_p"], dm) if n == "w_in" else gbig[n]
        if kind == "slots":
            g = jnp.moveaxis(g.reshape(depth, g.shape[1], N_CHIPS, g.shape[2] // N_CHIPS), 2, 1)
            g = g.reshape(depth, -1, g.shape[-1])
        gst.append(g)
    other = _swap_other_layer(gst, "swap_layer_grads")
    parts = [_pair_sum(g, o, core, "pair_sum_" + n) for g, o, n in zip(gst, other, _BIG)]
    parts = [p.reshape(N_CHIPS, -1, p.shape[-1]) if k == "slots" else p for p, k in zip(parts, kinds)]
    got = _scatter_layer(parts, kinds, "scatter_grads")
    mine = [_sum_slots(g, "sum_chip_grads_" + n) for g, n in zip(got, _BIG)]
    theirs = _sibling_swap(mine, "swap_core_grads")

    out = {}
    for i, n in enumerate(_BIG):
        out[n] = _adamw_layers(a[n], a["m_" + n], a["v_" + n], mine[i], theirs[i], core, "adamw_" + n)

    gsmall = [jnp.stack([grads[l][n] for l in range(depth)]) for n in _SMALL]
    small_shapes = [g.shape for g in gsmall]
    summed = _sum_slots(_allgather_all(_pack(gsmall, F32, LANES, 8), "gather_small_grads"), "sum_small_grads")
    gs = dict(zip(_SMALL, _unpack(summed, small_shapes)))
    for n in _CONV:
        wcols = a[n].shape[2]
        gs[n] = lax.dynamic_slice_in_dim(gs[n], chip * wcols, wcols, axis=2)
    local_shapes = [a[n].shape for n in _SMALL]
    res = _adamw(_pack([a[n] for n in _SMALL], F32, LANES, 8), _pack([a["m_" + n] for n in _SMALL], F32, LANES, 8),
                 _pack([a["v_" + n] for n in _SMALL], F32, LANES, 8), _pack([gs[n] for n in _SMALL], F32, LANES, 8),
                 "adamw_small")
    res = [_unpack(r, local_shapes) for r in res]
    for i, n in enumerate(_SMALL):
        out[n] = [r[i] for r in res]

    loss = lax.psum(loss_part, ("x", "y", "c"))
    return (loss, grad_x) + tuple(out[n][k] for k in range(4) for n in _WEIGHTS)


def kernel(x, c, ada_w, ada_b, mix_pre_g, mix_post_g, w_in, ssd_conv_w, ssd_conv_b, ssd_dt_bias, ssd_a_log, ssd_d, ssd_norm_g, w_ssd_out, sc_conv_w, w_sc_out, w_o, ffn_pre_g, ffn_post_g, w_up, ffn_conv_w, ffn_conv_b, w_down, loss_target, m_ada_w, m_ada_b, m_mix_pre_g, m_mix_post_g, m_w_in, m_ssd_conv_w, m_ssd_conv_b, m_ssd_dt_bias, m_ssd_a_log, m_ssd_d, m_ssd_norm_g, m_w_ssd_out, m_sc_conv_w, m_w_sc_out, m_w_o, m_ffn_pre_g, m_ffn_post_g, m_w_up, m_ffn_conv_w, m_ffn_conv_b, m_w_down, v_ada_w, v_ada_b, v_mix_pre_g, v_mix_post_g, v_w_in, v_ssd_conv_w, v_ssd_conv_b, v_ssd_dt_bias, v_ssd_a_log, v_ssd_d, v_ssd_norm_g, v_w_ssd_out, v_sc_conv_w, v_w_sc_out, v_w_o, v_ffn_pre_g, v_ffn_post_g, v_w_up, v_ffn_conv_w, v_ffn_conv_b, v_w_down):
    return _step(dict(zip(_INPUTS, (
        x, c, ada_w, ada_b, mix_pre_g, mix_post_g, w_in, ssd_conv_w, ssd_conv_b, ssd_dt_bias, ssd_a_log, ssd_d, ssd_norm_g, w_ssd_out, sc_conv_w, w_sc_out, w_o, ffn_pre_g, ffn_post_g, w_up, ffn_conv_w, ffn_conv_b, w_down, loss_target, m_ada_w, m_ada_b, m_mix_pre_g, m_mix_post_g, m_w_in, m_ssd_conv_w, m_ssd_conv_b, m_ssd_dt_bias, m_ssd_a_log, m_ssd_d, m_ssd_norm_g, m_w_ssd_out, m_sc_conv_w, m_w_sc_out, m_w_o, m_ffn_pre_g, m_ffn_post_g, m_w_up, m_ffn_conv_w, m_ffn_conv_b, m_w_down, v_ada_w, v_ada_b, v_mix_pre_g, v_mix_post_g, v_w_in, v_ssd_conv_w, v_ssd_conv_b, v_ssd_dt_bias, v_ssd_a_log, v_ssd_d, v_ssd_norm_g, v_w_ssd_out, v_sc_conv_w, v_w_sc_out, v_w_o, v_ffn_pre_g, v_ffn_post_g, v_w_up, v_ffn_conv_w, v_ffn_conv_b, v_w_down))))
```
